```python
import jax, jax.numpy as jnp
from jax import lax
import numpy as np

D_MODEL = 1024
BATCH = 32
SEQ = 2048
DEPTH = 1

CTX_LEN = 256
GRID_W = 64
MIX_WIDTH = D_MODEL
RWKV_WIDTH = MIX_WIDTH // 2
CONV_WIDTH = MIX_WIDTH - RWKV_WIDTH
HEAD_SIZE = 64
RWKV_HEADS = RWKV_WIDTH // HEAD_SIZE
DECAY_RANK = max(32, int(round(1.8 * D_MODEL ** 0.5 / 32)) * 32)
ICLR_RANK = max(32, int(round(1.8 * D_MODEL ** 0.5 / 32)) * 32)
GATE_RANK = max(32, int(round(0.6 * D_MODEL ** 0.8 / 32)) * 32)
CONV_KERNEL = 31
D_FF = 4 * D_MODEL
N_MOD = 6
EPS_RMS = 1e-6
EPS_LN = 1e-5
EPS_GN = 64e-5

IN_SPLITS = (RWKV_WIDTH, RWKV_WIDTH, RWKV_WIDTH, DECAY_RANK, DECAY_RANK,
             ICLR_RANK, ICLR_RANK, GATE_RANK, 2 * CONV_WIDTH)
IN_COLS = sum(IN_SPLITS)
SHIFT_COLS = sum(IN_SPLITS[:-1])
RWKV_CUTS = tuple(int(v) for v in np.cumsum(IN_SPLITS[:-1])[:-1])

kernel_name = "hymba_rwkv7_conformer_dit_block"


def rms_norm(x, g):
    xf = x.astype(jnp.float32)
    y = xf * lax.rsqrt(jnp.mean(xf * xf, axis=-1, keepdims=True) + EPS_RMS)
    return (y * g.astype(jnp.float32)).astype(x.dtype)


def modulate(h, shift, scale):
    return h * (1 + scale) + shift


def split_heads(t):
    return t.reshape(t.shape[:-1] + (RWKV_HEADS, HEAD_SIZE))


def token_shift(z, mu_prev, mu_next):
    zp = jnp.pad(z, ((0, 0), (1, 1), (0, 0)))
    mp = mu_prev.astype(jnp.float32)
    mn = mu_next.astype(jnp.float32)
    return z + mp * (zp[:, :-2] - z) + mn * (zp[:, 2:] - z)


def project_stream(h, w_in, mu_prev, mu_next):
    p = h @ w_in
    rw = token_shift(p[..., :SHIFT_COLS].astype(jnp.float32), mu_prev, mu_next)
    pieces = tuple(jnp.split(rw, RWKV_CUTS, axis=-1))
    return pieces, p[..., SHIFT_COLS:]


def wkv_scan(s0, r, decay, k, v, a_vec, b_vec, reverse):
    xs = tuple(jnp.moveaxis(t, 1, 0) for t in (r, decay, k, v, a_vec, b_vec))

    def step(s, inp):
        r_t, w_t, k_t, v_t, a_t, b_t = inp
        sa = jnp.einsum('bhvk,bhk->bhv', s, a_t)
        s = (s * w_t[:, :, None, :] + sa[..., None] * b_t[:, :, None, :]
             + v_t[..., None] * k_t[:, :, None, :])
        return s, jnp.einsum('bhvk,bhk->bhv', s, r_t)

    s_fin, ys = lax.scan(step, s0, xs, reverse=reverse)
    return s_fin, jnp.moveaxis(ys, 0, 1)


def rwkv_direction(s0, r, k, v, wd, ad, w0, w2, a0, a2, k_k, k_a, reverse):
    w_log = -jax.nn.softplus(-(w0 + jnp.tanh(wd) @ w2)) - 0.5
    decay = jnp.exp(-jnp.exp(w_log))
    iclr = jax.nn.sigmoid(a0 + ad @ a2)
    kk = split_heads(k * k_k)
    kk = kk / jnp.maximum(jnp.sqrt(jnp.sum(kk * kk, axis=-1, keepdims=True)), 1e-12)
    k_dir = split_heads(k * (1 + (iclr - 1) * k_a))
    s_fin, y = wkv_scan(s0, split_heads(r), split_heads(decay), k_dir, split_heads(v),
                        -kk, kk * split_heads(iclr), reverse)
    return s_fin, y, k_dir


def rwkv_bidir(s0_f, s0_b, pieces, decay_w0, decay_w2, iclr_a0, iclr_a2, k_k, k_a):
    r, k, v, wd_f, wd_b, ad_f, ad_b, _ = pieces
    s_f, y_f, kd_f = rwkv_direction(s0_f, r, k, v, wd_f, ad_f, decay_w0[0], decay_w2[0],
                                    iclr_a0[0], iclr_a2[0], k_k, k_a, False)
    s_b, y_b, kd_b = rwkv_direction(s0_b, r, k, v, wd_b, ad_b, decay_w0[1], decay_w2[1],
                                    iclr_a0[1], iclr_a2[1], k_k, k_a, True)
    return s_f, s_b, y_f + y_b, 0.5 * (kd_f + kd_b)


def rwkv_readout(y, k_bar, pieces, r_k, gate_w2, lnx_w, lnx_b):
    r, _, v, _, _, _, _, gd = pieces
    mu = jnp.mean(y, axis=-1, keepdims=True)
    var = jnp.mean(jnp.square(y - mu), axis=-1, keepdims=True)
    yn = ((y - mu) * lax.rsqrt(var + EPS_GN)).reshape(r.shape) * lnx_w + lnx_b
    rh = split_heads(r)
    bonus = jnp.sum(rh * k_bar * r_k, axis=-1, keepdims=True) * split_heads(v)
    g = jax.nn.sigmoid(gd) @ gate_w2
    return (yn + bonus.reshape(r.shape)) * g


def conformer_conv(pcv, n_lines, line_len, conv_w, conv_b, ln_w, ln_b):
    u = pcv[..., :CONV_WIDTH] * jax.nn.sigmoid(pcv[..., CONV_WIDTH:])
    bsz, n_tok, ch = u.shape
    lines = u.reshape(bsz * n_lines, line_len, ch)
    pad = CONV_KERNEL // 2
    y = lax.conv_general_dilated(lines, conv_w[:, None, :].astype(lines.dtype), (1,),
                                 [(pad, pad)], dimension_numbers=('NWC', 'WIO', 'NWC'),
                                 feature_group_count=ch)
    yf = (y.reshape(bsz, n_tok, ch) + conv_b).astype(jnp.float32)
    mu = jnp.mean(yf, axis=-1, keepdims=True)
    var = jnp.mean(jnp.square(yf - mu), axis=-1, keepdims=True)
    yn = (yf - mu) * lax.rsqrt(var + EPS_LN) * ln_w + ln_b
    return jax.nn.silu(yn).astype(pcv.dtype)


def sqrelu_mlp(h, w1, w2):
    return jnp.square(jax.nn.relu(h @ w1)) @ w2


def _fwd_setup_inputs(seed: int = 0) -> dict:
    key = jax.random.key(seed)
    ks = jax.random.split(key, 32)
    L, D, W, CW = DEPTH, D_MODEL, RWKV_WIDTH, CONV_WIDTH
    nrm = lambda k, shape, s: jax.random.normal(k, shape, jnp.float32) * s
    gain = lambda k, shape: 1.0 + nrm(k, shape, 0.02)
    return {
        "x": nrm(ks[0], (BATCH, SEQ, D), 1.0),
        "c": nrm(ks[1], (BATCH, D), 1.0),
        "ctx": nrm(ks[2], (BATCH, CTX_LEN, D), 1.0),
        "c_ctx": nrm(ks[3], (D,), 1.0),
        "ada_w": nrm(ks[4], (L, D, N_MOD * D), 0.5 * D ** -0.5),
        "ada_b": nrm(ks[5], (L, N_MOD * D), 0.02),
        "mix_pre_g": gain(ks[6], (L, D)),
        "mix_post_g": gain(ks[7], (L, D)),
        "mlp_pre_g": gain(ks[8], (L, D)),
        "mlp_post_g": gain(ks[9], (L, D)),
        "w_in": nrm(ks[10], (L, D, IN_COLS), D ** -0.5),
        "mu_prev": jax.random.uniform(ks[11], (L, SHIFT_COLS), jnp.float32, 0.0, 0.5),
        "mu_next": jax.random.uniform(ks[12], (L, SHIFT_COLS), jnp.float32, 0.0, 0.5),
        "decay_w0": jax.random.uniform(ks[13], (L, 2, W), jnp.float32, -6.0, 1.0),
        "decay_w2": nrm(ks[14], (L, 2, DECAY_RANK, W), 0.5 * DECAY_RANK ** -0.5),
        "iclr_a0": nrm(ks[15], (L, 2, W), 0.5),
        "iclr_a2": nrm(ks[16], (L, 2, ICLR_RANK, W), 0.5 * ICLR_RANK ** -0.5),
        "k_k": 0.85 + nrm(ks[17], (L, W), 0.02),
        "k_a": 1.0 + nrm(ks[18], (L, W), 0.02),
        "r_k": nrm(ks[19], (L, RWKV_HEADS, HEAD_SIZE), 0.1),
        "gate_w2": nrm(ks[20], (L, GATE_RANK, W), GATE_RANK ** -0.5),
        "lnx_w": gain(ks[21], (L, W)),
        "lnx_b": nrm(ks[22], (L, W), 0.02),
        "conv_w": nrm(ks[23], (L, CONV_KERNEL, CW), CONV_KERNEL ** -0.5),
        "conv_b": nrm(ks[24], (L, CW), 0.02),
        "conv_ln_w": gain(ks[25], (L, CW)),
        "conv_ln_b": nrm(ks[26], (L, CW), 0.02),
        "w_out": nrm(ks[27], (L, MIX_WIDTH, D), MIX_WIDTH ** -0.5),
        "mlp_w1": nrm(ks[28], (L, D, D_FF), D ** -0.5),
        "mlp_w2": nrm(ks[29], (L, D_FF, D), D_FF ** -0.5),
    }


def _fwd_reference(x, c, ctx, c_ctx, ada_w, ada_b, mix_pre_g, mix_post_g, mlp_pre_g, mlp_post_g,
              w_in, mu_prev, mu_next, decay_w0, decay_w2, iclr_a0, iclr_a2, k_k, k_a, r_k,
              gate_w2, lnx_w, lnx_b, conv_w, conv_b, conv_ln_w, conv_ln_b, w_out,
              mlp_w1, mlp_w2):
    n_rows = x.shape[1] // GRID_W
    s_zero = jnp.zeros((x.shape[0], RWKV_HEADS, HEAD_SIZE, HEAD_SIZE), jnp.float32)
    for l in range(DEPTH):
        update_ctx = l + 1 < DEPTH
        mod_x = jnp.split((jax.nn.silu(c) @ ada_w[l] + ada_b[l])[:, None, :], N_MOD, axis=-1)
        mod_c = jnp.split(jax.nn.silu(c_ctx) @ ada_w[l] + ada_b[l], N_MOD, axis=-1)
        rw_params = (decay_w0[l], decay_w2[l], iclr_a0[l], iclr_a2[l], k_k[l], k_a[l])
        ro_params = (r_k[l], gate_w2[l], lnx_w[l], lnx_b[l])
        cv_params = (conv_w[l], conv_b[l], conv_ln_w[l], conv_ln_b[l])

        hx = modulate(rms_norm(x, mix_pre_g[l]), mod_x[0], mod_x[1])
        hc = modulate(rms_norm(ctx, mix_pre_g[l]), mod_c[0], mod_c[1])
        px, cvx = project_stream(hx, w_in[l], mu_prev[l], mu_next[l])
        pc, cvc = project_stream(hc, w_in[l], mu_prev[l], mu_next[l])
        s_f_c, s_b_c, y_c, kbar_c = rwkv_bidir(s_zero, s_zero, pc, *rw_params)
        _, _, y_x, kbar_x = rwkv_bidir(s_f_c, s_b_c, px, *rw_params)
        mix_x = jnp.concatenate(
            [rwkv_readout(y_x, kbar_x, px, *ro_params).astype(x.dtype),
             conformer_conv(cvx, n_rows, GRID_W, *cv_params)], axis=-1) @ w_out[l]
        x = x + mod_x[2] * rms_norm(mix_x, mix_post_g[l])
        if update_ctx:
            mix_c = jnp.concatenate(
                [rwkv_readout(y_c, kbar_c, pc, *ro_params).astype(ctx.dtype),
                 conformer_conv(cvc, 1, ctx.shape[1], *cv_params)], axis=-1) @ w_out[l]
            ctx = ctx + mod_c[2] * rms_norm(mix_c, mix_post_g[l])

        hx = modulate(rms_norm(x, mlp_pre_g[l]), mod_x[3], mod_x[4])
        x = x + mod_x[5] * rms_norm(sqrelu_mlp(hx, mlp_w1[l], mlp_w2[l]), mlp_post_g[l])
        if update_ctx:
            hc = modulate(rms_norm(ctx, mlp_pre_g[l]), mod_c[3], mod_c[4])
            ctx = ctx + mod_c[5] * rms_norm(sqrelu_mlp(hc, mlp_w1[l], mlp_w2[l]), mlp_post_g[l])
    return x


import jax as _jax
import jax.numpy as _jnp

TWIN_FORMAT = 'train_step'
FWD_PARAMS = ['x', 'c', 'ctx', 'c_ctx', 'ada_w', 'ada_b', 'mix_pre_g', 'mix_post_g', 'mlp_pre_g', 'mlp_post_g', 'w_in', 'mu_prev', 'mu_next', 'decay_w0', 'decay_w2', 'iclr_a0', 'iclr_a2', 'k_k', 'k_a', 'r_k', 'gate_w2', 'lnx_w', 'lnx_b', 'conv_w', 'conv_b', 'conv_ln_w', 'conv_ln_b', 'w_out', 'mlp_w1', 'mlp_w2']
TWIN_WEIGHTS = ['c_ctx', 'ada_w', 'ada_b', 'mix_pre_g', 'mix_post_g', 'mlp_pre_g', 'mlp_post_g', 'w_in', 'mu_prev', 'mu_next', 'decay_w0', 'decay_w2', 'iclr_a0', 'iclr_a2', 'k_k', 'k_a', 'r_k', 'gate_w2', 'lnx_w', 'lnx_b', 'conv_w', 'conv_b', 'conv_ln_w', 'conv_ln_b', 'w_out', 'mlp_w1', 'mlp_w2']
TWIN_DIFF_INPUT = 'x'
TWIN_INPUTS = ['x', 'c', 'ctx', 'c_ctx', 'ada_w', 'ada_b', 'mix_pre_g', 'mix_post_g', 'mlp_pre_g', 'mlp_post_g', 'w_in', 'mu_prev', 'mu_next', 'decay_w0', 'decay_w2', 'iclr_a0', 'iclr_a2', 'k_k', 'k_a', 'r_k', 'gate_w2', 'lnx_w', 'lnx_b', 'conv_w', 'conv_b', 'conv_ln_w', 'conv_ln_b', 'w_out', 'mlp_w1', 'mlp_w2', 'loss_target', 'm_c_ctx', 'm_ada_w', 'm_ada_b', 'm_mix_pre_g', 'm_mix_post_g', 'm_mlp_pre_g', 'm_mlp_post_g', 'm_w_in', 'm_mu_prev', 'm_mu_next', 'm_decay_w0', 'm_decay_w2', 'm_iclr_a0', 'm_iclr_a2', 'm_k_k', 'm_k_a', 'm_r_k', 'm_gate_w2', 'm_lnx_w', 'm_lnx_b', 'm_conv_w', 'm_conv_b', 'm_conv_ln_w', 'm_conv_ln_b', 'm_w_out', 'm_mlp_w1', 'm_mlp_w2', 'v_c_ctx', 'v_ada_w', 'v_ada_b', 'v_mix_pre_g', 'v_mix_post_g', 'v_mlp_pre_g', 'v_mlp_post_g', 'v_w_in', 'v_mu_prev', 'v_mu_next', 'v_decay_w0', 'v_decay_w2', 'v_iclr_a0', 'v_iclr_a2', 'v_k_k', 'v_k_a', 'v_r_k', 'v_gate_w2', 'v_lnx_w', 'v_lnx_b', 'v_conv_w', 'v_conv_b', 'v_conv_ln_w', 'v_conv_ln_b', 'v_w_out', 'v_mlp_w1', 'v_mlp_w2']
TWIN_OUTPUTS = ['loss', 'grad_x', 'grad_c_ctx', 'grad_ada_w', 'grad_ada_b', 'grad_mix_pre_g', 'grad_mix_post_g', 'grad_mlp_pre_g', 'grad_mlp_post_g', 'grad_w_in', 'grad_mu_prev', 'grad_mu_next', 'grad_decay_w0', 'grad_decay_w2', 'grad_iclr_a0', 'grad_iclr_a2', 'grad_k_k', 'grad_k_a', 'grad_r_k', 'grad_gate_w2', 'grad_lnx_w', 'grad_lnx_b', 'grad_conv_w', 'grad_conv_b', 'grad_conv_ln_w', 'grad_conv_ln_b', 'grad_w_out', 'grad_mlp_w1', 'grad_mlp_w2', 'delta_c_ctx', 'delta_ada_w', 'delta_ada_b', 'delta_mix_pre_g', 'delta_mix_post_g', 'delta_mlp_pre_g', 'delta_mlp_post_g', 'delta_w_in', 'delta_mu_prev', 'delta_mu_next', 'delta_decay_w0', 'delta_decay_w2', 'delta_iclr_a0', 'delta_iclr_a2', 'delta_k_k', 'delta_k_a', 'delta_r_k', 'delta_gate_w2', 'delta_lnx_w', 'delta_lnx_b', 'delta_conv_w', 'delta_conv_b', 'delta_conv_ln_w', 'delta_conv_ln_b', 'delta_w_out', 'delta_mlp_w1', 'delta_mlp_w2', 'new_m_c_ctx', 'new_m_ada_w', 'new_m_ada_b', 'new_m_mix_pre_g', 'new_m_mix_post_g', 'new_m_mlp_pre_g', 'new_m_mlp_post_g', 'new_m_w_in', 'new_m_mu_prev', 'new_m_mu_next', 'new_m_decay_w0', 'new_m_decay_w2', 'new_m_iclr_a0', 'new_m_iclr_a2', 'new_m_k_k', 'new_m_k_a', 'new_m_r_k', 'new_m_gate_w2', 'new_m_lnx_w', 'new_m_lnx_b', 'new_m_conv_w', 'new_m_conv_b', 'new_m_conv_ln_w', 'new_m_conv_ln_b', 'new_m_w_out', 'new_m_mlp_w1', 'new_m_mlp_w2', 'new_v_c_ctx', 'new_v_ada_w', 'new_v_ada_b', 'new_v_mix_pre_g', 'new_v_mix_post_g', 'new_v_mlp_pre_g', 'new_v_mlp_post_g', 'new_v_w_in', 'new_v_mu_prev', 'new_v_mu_next', 'new_v_decay_w0', 'new_v_decay_w2', 'new_v_iclr_a0', 'new_v_iclr_a2', 'new_v_k_k', 'new_v_k_a', 'new_v_r_k', 'new_v_gate_w2', 'new_v_lnx_w', 'new_v_lnx_b', 'new_v_conv_w', 'new_v_conv_b', 'new_v_conv_ln_w', 'new_v_conv_ln_b', 'new_v_w_out', 'new_v_mlp_w1', 'new_v_mlp_w2']
TWIN_LEAF_KINDS = {'loss': 'loss', 'grad_x': 'grad_x', 'grad_c_ctx': 'grad_w', 'grad_ada_w': 'grad_w', 'grad_ada_b': 'grad_w', 'grad_mix_pre_g': 'grad_w', 'grad_mix_post_g': 'grad_w', 'grad_mlp_pre_g': 'grad_w', 'grad_mlp_post_g': 'grad_w', 'grad_w_in': 'grad_w', 'grad_mu_prev': 'grad_w', 'grad_mu_next': 'grad_w', 'grad_decay_w0': 'grad_w', 'grad_decay_w2': 'grad_w', 'grad_iclr_a0': 'grad_w', 'grad_iclr_a2': 'grad_w', 'grad_k_k': 'grad_w', 'grad_k_a': 'grad_w', 'grad_r_k': 'grad_w', 'grad_gate_w2': 'grad_w', 'grad_lnx_w': 'grad_w', 'grad_lnx_b': 'grad_w', 'grad_conv_w': 'grad_w', 'grad_conv_b': 'grad_w', 'grad_conv_ln_w': 'grad_w', 'grad_conv_ln_b': 'grad_w', 'grad_w_out': 'grad_w', 'grad_mlp_w1': 'grad_w', 'grad_mlp_w2': 'grad_w', 'delta_c_ctx': 'delta_w', 'delta_ada_w': 'delta_w', 'delta_ada_b': 'delta_w', 'delta_mix_pre_g': 'delta_w', 'delta_mix_post_g': 'delta_w', 'delta_mlp_pre_g': 'delta_w', 'delta_mlp_post_g': 'delta_w', 'delta_w_in': 'delta_w', 'delta_mu_prev': 'delta_w', 'delta_mu_next': 'delta_w', 'delta_decay_w0': 'delta_w', 'delta_decay_w2': 'delta_w', 'delta_iclr_a0': 'delta_w', 'delta_iclr_a2': 'delta_w', 'delta_k_k': 'delta_w', 'delta_k_a': 'delta_w', 'delta_r_k': 'delta_w', 'delta_gate_w2': 'delta_w', 'delta_lnx_w': 'delta_w', 'delta_lnx_b': 'delta_w', 'delta_conv_w': 'delta_w', 'delta_conv_b': 'delta_w', 'delta_conv_ln_w': 'delta_w', 'delta_conv_ln_b': 'delta_w', 'delta_w_out': 'delta_w', 'delta_mlp_w1': 'delta_w', 'delta_mlp_w2': 'delta_w', 'new_m_c_ctx': 'new_m', 'new_m_ada_w': 'new_m', 'new_m_ada_b': 'new_m', 'new_m_mix_pre_g': 'new_m', 'new_m_mix_post_g': 'new_m', 'new_m_mlp_pre_g': 'new_m', 'new_m_mlp_post_g': 'new_m', 'new_m_w_in': 'new_m', 'new_m_mu_prev': 'new_m', 'new_m_mu_next': 'new_m', 'new_m_decay_w0': 'new_m', 'new_m_decay_w2': 'new_m', 'new_m_iclr_a0': 'new_m', 'new_m_iclr_a2': 'new_m', 'new_m_k_k': 'new_m', 'new_m_k_a': 'new_m', 'new_m_r_k': 'new_m', 'new_m_gate_w2': 'new_m', 'new_m_lnx_w': 'new_m', 'new_m_lnx_b': 'new_m', 'new_m_conv_w': 'new_m', 'new_m_conv_b': 'new_m', 'new_m_conv_ln_w': 'new_m', 'new_m_conv_ln_b': 'new_m', 'new_m_w_out': 'new_m', 'new_m_mlp_w1': 'new_m', 'new_m_mlp_w2': 'new_m', 'new_v_c_ctx': 'new_v', 'new_v_ada_w': 'new_v', 'new_v_ada_b': 'new_v', 'new_v_mix_pre_g': 'new_v', 'new_v_mix_post_g': 'new_v', 'new_v_mlp_pre_g': 'new_v', 'new_v_mlp_post_g': 'new_v', 'new_v_w_in': 'new_v', 'new_v_mu_prev': 'new_v', 'new_v_mu_next': 'new_v', 'new_v_decay_w0': 'new_v', 'new_v_decay_w2': 'new_v', 'new_v_iclr_a0': 'new_v', 'new_v_iclr_a2': 'new_v', 'new_v_k_k': 'new_v', 'new_v_k_a': 'new_v', 'new_v_r_k': 'new_v', 'new_v_gate_w2': 'new_v', 'new_v_lnx_w': 'new_v', 'new_v_lnx_b': 'new_v', 'new_v_conv_w': 'new_v', 'new_v_conv_b': 'new_v', 'new_v_conv_ln_w': 'new_v', 'new_v_conv_ln_b': 'new_v', 'new_v_w_out': 'new_v', 'new_v_mlp_w1': 'new_v', 'new_v_mlp_w2': 'new_v'}


def _forward(args):
    return _fwd_reference(*[args[k] for k in FWD_PARAMS])


def _output_shape():
    out = _jax.eval_shape(lambda: _forward(_fwd_setup_inputs(0)))
    return out.shape, out.dtype

N_MICROBATCH = 1
ADAM_LR = 0.001
ADAM_B1 = 0.9
ADAM_B2 = 0.999
ADAM_EPS = 1e-08
ADAM_WD = 0.01
ADAM_STEP = 10
PER_EXAMPLE_BATCH_AXIS = {'x': 0, 'c': 0, 'ctx': 0, 'loss_target': 0}
SHARED_INPUTS = []
_WEIGHT_DTYPES = {'c_ctx': _jnp.float32, 'ada_w': _jnp.float32, 'ada_b': _jnp.float32, 'mix_pre_g': _jnp.float32, 'mix_post_g': _jnp.float32, 'mlp_pre_g': _jnp.float32, 'mlp_post_g': _jnp.float32, 'w_in': _jnp.float32, 'mu_prev': _jnp.float32, 'mu_next': _jnp.float32, 'decay_w0': _jnp.float32, 'decay_w2': _jnp.float32, 'iclr_a0': _jnp.float32, 'iclr_a2': _jnp.float32, 'k_k': _jnp.float32, 'k_a': _jnp.float32, 'r_k': _jnp.float32, 'gate_w2': _jnp.float32, 'lnx_w': _jnp.float32, 'lnx_b': _jnp.float32, 'conv_w': _jnp.float32, 'conv_b': _jnp.float32, 'conv_ln_w': _jnp.float32, 'conv_ln_b': _jnp.float32, 'w_out': _jnp.float32, 'mlp_w1': _jnp.float32, 'mlp_w2': _jnp.float32}
MOMENT_SCALE = {'c_ctx': 3.486293e-02, 'ada_w': 3.594579e+00, 'ada_b': 6.774195e+00, 'mix_pre_g': 2.424447e-01, 'mix_post_g': 7.463289e+00, 'mlp_pre_g': 2.031605e-01, 'mlp_post_g': 7.966108e+00, 'w_in': 2.366553e-01, 'mu_prev': 2.903510e-01, 'mu_next': 3.049621e-01, 'decay_w0': 7.800639e-02, 'decay_w2': 1.629194e-02, 'iclr_a0': 4.080541e-02, 'iclr_a2': 3.816421e-02, 'k_k': 1.279467e+00, 'k_a': 1.239431e+00, 'r_k': 2.694377e-01, 'gate_w2': 3.064231e-01, 'lnx_w': 5.982874e-01, 'lnx_b': 1.540745e+00, 'conv_w': 2.828413e-01, 'conv_b': 2.845445e+00, 'conv_ln_w': 9.161957e-01, 'conv_ln_b': 1.573157e+00, 'w_out': 4.755222e-01, 'mlp_w1': 1.604005e-01, 'mlp_w2': 8.059219e-01}


def _to_microbatches(a, axis):
    t = _jnp.moveaxis(a, axis, 0)
    t = t.reshape((N_MICROBATCH, t.shape[0] // N_MICROBATCH) + t.shape[1:])
    return _jnp.moveaxis(t, 1, axis + 1)


def setup_inputs(seed: int = 0) -> dict:
    inp = _fwd_setup_inputs(seed)
    key = _jax.random.fold_in(_jax.random.key(seed), 7919)
    shape, _ = _output_shape()
    out = dict(inp)
    out["loss_target"] = _jax.random.normal(_jax.random.fold_in(key, 0), shape, _jnp.float32)
    for i, name in enumerate(TWIN_WEIGHTS):
        w = inp[name].astype(_jnp.float32)
        if MOMENT_SCALE is None:
            s = _jnp.sqrt(_jnp.mean(_jnp.square(w)) + 1e-30)
        else:
            s = MOMENT_SCALE[name]
        km, kv = _jax.random.split(_jax.random.fold_in(key, i + 1))
        out[name] = w
        out["m_" + name] = s * _jax.random.normal(km, w.shape, _jnp.float32)
        out["v_" + name] = (s * s) * _jax.random.uniform(kv, w.shape, _jnp.float32, 0.5, 1.5)
    if N_MICROBATCH > 1:
        for name, axis in PER_EXAMPLE_BATCH_AXIS.items():
            out[name] = _to_microbatches(out[name], axis)
    return {'x': out['x'], 'c': out['c'], 'ctx': out['ctx'], 'c_ctx': out['c_ctx'], 'ada_w': out['ada_w'], 'ada_b': out['ada_b'], 'mix_pre_g': out['mix_pre_g'], 'mix_post_g': out['mix_post_g'], 'mlp_pre_g': out['mlp_pre_g'], 'mlp_post_g': out['mlp_post_g'], 'w_in': out['w_in'], 'mu_prev': out['mu_prev'], 'mu_next': out['mu_next'], 'decay_w0': out['decay_w0'], 'decay_w2': out['decay_w2'], 'iclr_a0': out['iclr_a0'], 'iclr_a2': out['iclr_a2'], 'k_k': out['k_k'], 'k_a': out['k_a'], 'r_k': out['r_k'], 'gate_w2': out['gate_w2'], 'lnx_w': out['lnx_w'], 'lnx_b': out['lnx_b'], 'conv_w': out['conv_w'], 'conv_b': out['conv_b'], 'conv_ln_w': out['conv_ln_w'], 'conv_ln_b': out['conv_ln_b'], 'w_out': out['w_out'], 'mlp_w1': out['mlp_w1'], 'mlp_w2': out['mlp_w2'], 'loss_target': out['loss_target'], 'm_c_ctx': out['m_c_ctx'], 'm_ada_w': out['m_ada_w'], 'm_ada_b': out['m_ada_b'], 'm_mix_pre_g': out['m_mix_pre_g'], 'm_mix_post_g': out['m_mix_post_g'], 'm_mlp_pre_g': out['m_mlp_pre_g'], 'm_mlp_post_g': out['m_mlp_post_g'], 'm_w_in': out['m_w_in'], 'm_mu_prev': out['m_mu_prev'], 'm_mu_next': out['m_mu_next'], 'm_decay_w0': out['m_decay_w0'], 'm_decay_w2': out['m_decay_w2'], 'm_iclr_a0': out['m_iclr_a0'], 'm_iclr_a2': out['m_iclr_a2'], 'm_k_k': out['m_k_k'], 'm_k_a': out['m_k_a'], 'm_r_k': out['m_r_k'], 'm_gate_w2': out['m_gate_w2'], 'm_lnx_w': out['m_lnx_w'], 'm_lnx_b': out['m_lnx_b'], 'm_conv_w': out['m_conv_w'], 'm_conv_b': out['m_conv_b'], 'm_conv_ln_w': out['m_conv_ln_w'], 'm_conv_ln_b': out['m_conv_ln_b'], 'm_w_out': out['m_w_out'], 'm_mlp_w1': out['m_mlp_w1'], 'm_mlp_w2': out['m_mlp_w2'], 'v_c_ctx': out['v_c_ctx'], 'v_ada_w': out['v_ada_w'], 'v_ada_b': out['v_ada_b'], 'v_mix_pre_g': out['v_mix_pre_g'], 'v_mix_post_g': out['v_mix_post_g'], 'v_mlp_pre_g': out['v_mlp_pre_g'], 'v_mlp_post_g': out['v_mlp_post_g'], 'v_w_in': out['v_w_in'], 'v_mu_prev': out['v_mu_prev'], 'v_mu_next': out['v_mu_next'], 'v_decay_w0': out['v_decay_w0'], 'v_decay_w2': out['v_decay_w2'], 'v_iclr_a0': out['v_iclr_a0'], 'v_iclr_a2': out['v_iclr_a2'], 'v_k_k': out['v_k_k'], 'v_k_a': out['v_k_a'], 'v_r_k': out['v_r_k'], 'v_gate_w2': out['v_gate_w2'], 'v_lnx_w': out['v_lnx_w'], 'v_lnx_b': out['v_lnx_b'], 'v_conv_w': out['v_conv_w'], 'v_conv_b': out['v_conv_b'], 'v_conv_ln_w': out['v_conv_ln_w'], 'v_conv_ln_b': out['v_conv_ln_b'], 'v_w_out': out['v_w_out'], 'v_mlp_w1': out['v_mlp_w1'], 'v_mlp_w2': out['v_mlp_w2']}


def _loss(weights, diff, rest, loss_target):
    with _jax.named_scope("forward"):
        args = {**rest, TWIN_DIFF_INPUT: diff, **{k: w.astype(_WEIGHT_DTYPES[k]) for k, w in weights.items()}}
        y = _forward(args)
    with _jax.named_scope("loss_head"):
        err = _jnp.square(y.astype(_jnp.float32) - loss_target)
        return 0.5 * _jnp.sum(_jnp.mean(err, axis=-1)) if err.ndim else 0.5 * err


def _adamw(w, g, m, v):
    m = ADAM_B1 * m + (1.0 - ADAM_B1) * g
    v = ADAM_B2 * v + (1.0 - ADAM_B2) * _jnp.square(g)
    m_hat = m / (1.0 - ADAM_B1 ** ADAM_STEP)
    v_hat = v / (1.0 - ADAM_B2 ** ADAM_STEP)
    delta = -ADAM_LR * (m_hat / (_jnp.sqrt(v_hat) + ADAM_EPS) + ADAM_WD * w)
    return delta, m, v


def reference(x, c, ctx, c_ctx, ada_w, ada_b, mix_pre_g, mix_post_g, mlp_pre_g, mlp_post_g, w_in, mu_prev, mu_next, decay_w0, decay_w2, iclr_a0, iclr_a2, k_k, k_a, r_k, gate_w2, lnx_w, lnx_b, conv_w, conv_b, conv_ln_w, conv_ln_b, w_out, mlp_w1, mlp_w2, loss_target, m_c_ctx, m_ada_w, m_ada_b, m_mix_pre_g, m_mix_post_g, m_mlp_pre_g, m_mlp_post_g, m_w_in, m_mu_prev, m_mu_next, m_decay_w0, m_decay_w2, m_iclr_a0, m_iclr_a2, m_k_k, m_k_a, m_r_k, m_gate_w2, m_lnx_w, m_lnx_b, m_conv_w, m_conv_b, m_conv_ln_w, m_conv_ln_b, m_w_out, m_mlp_w1, m_mlp_w2, v_c_ctx, v_ada_w, v_ada_b, v_mix_pre_g, v_mix_post_g, v_mlp_pre_g, v_mlp_post_g, v_w_in, v_mu_prev, v_mu_next, v_decay_w0, v_decay_w2, v_iclr_a0, v_iclr_a2, v_k_k, v_k_a, v_r_k, v_gate_w2, v_lnx_w, v_lnx_b, v_conv_w, v_conv_b, v_conv_ln_w, v_conv_ln_b, v_w_out, v_mlp_w1, v_mlp_w2):
    given = dict(x=x, c=c, ctx=ctx, c_ctx=c_ctx, ada_w=ada_w, ada_b=ada_b, mix_pre_g=mix_pre_g, mix_post_g=mix_post_g, mlp_pre_g=mlp_pre_g, mlp_post_g=mlp_post_g, w_in=w_in, mu_prev=mu_prev, mu_next=mu_next, decay_w0=decay_w0, decay_w2=decay_w2, iclr_a0=iclr_a0, iclr_a2=iclr_a2, k_k=k_k, k_a=k_a, r_k=r_k, gate_w2=gate_w2, lnx_w=lnx_w, lnx_b=lnx_b, conv_w=conv_w, conv_b=conv_b, conv_ln_w=conv_ln_w, conv_ln_b=conv_ln_b, w_out=w_out, mlp_w1=mlp_w1, mlp_w2=mlp_w2, loss_target=loss_target, m_c_ctx=m_c_ctx, m_ada_w=m_ada_w, m_ada_b=m_ada_b, m_mix_pre_g=m_mix_pre_g, m_mix_post_g=m_mix_post_g, m_mlp_pre_g=m_mlp_pre_g, m_mlp_post_g=m_mlp_post_g, m_w_in=m_w_in, m_mu_prev=m_mu_prev, m_mu_next=m_mu_next, m_decay_w0=m_decay_w0, m_decay_w2=m_decay_w2, m_iclr_a0=m_iclr_a0, m_iclr_a2=m_iclr_a2, m_k_k=m_k_k, m_k_a=m_k_a, m_r_k=m_r_k, m_gate_w2=m_gate_w2, m_lnx_w=m_lnx_w, m_lnx_b=m_lnx_b, m_conv_w=m_conv_w, m_conv_b=m_conv_b, m_conv_ln_w=m_conv_ln_w, m_conv_ln_b=m_conv_ln_b, m_w_out=m_w_out, m_mlp_w1=m_mlp_w1, m_mlp_w2=m_mlp_w2, v_c_ctx=v_c_ctx, v_ada_w=v_ada_w, v_ada_b=v_ada_b, v_mix_pre_g=v_mix_pre_g, v_mix_post_g=v_mix_post_g, v_mlp_pre_g=v_mlp_pre_g, v_mlp_post_g=v_mlp_post_g, v_w_in=v_w_in, v_mu_prev=v_mu_prev, v_mu_next=v_mu_next, v_decay_w0=v_decay_w0, v_decay_w2=v_decay_w2, v_iclr_a0=v_iclr_a0, v_iclr_a2=v_iclr_a2, v_k_k=v_k_k, v_k_a=v_k_a, v_r_k=v_r_k, v_gate_w2=v_gate_w2, v_lnx_w=v_lnx_w, v_lnx_b=v_lnx_b, v_conv_w=v_conv_w, v_conv_b=v_conv_b, v_conv_ln_w=v_conv_ln_w, v_conv_ln_b=v_conv_ln_b, v_w_out=v_w_out, v_mlp_w1=v_mlp_w1, v_mlp_w2=v_mlp_w2)
    weights = {n: given[n] for n in TWIN_WEIGHTS}
    shared = {n: given[n] for n in SHARED_INPUTS}
    per_example = {n: given[n] for n in ['x', 'c', 'ctx']}
    grad_fn = _jax.value_and_grad(_loss, argnums=(0, 1))

    def one_microbatch(ex, loss_target):
        ex = dict(ex)
        diff = ex.pop(TWIN_DIFF_INPUT)
        return grad_fn(weights, diff, {**shared, **ex}, loss_target)

    if N_MICROBATCH == 1:
        loss, (grad_w, grad_x) = one_microbatch(per_example, given["loss_target"])
    else:
        def body(carry, xs):
            loss_sum, grad_sum = carry
            l_k, (gw_k, gx_k) = one_microbatch(xs[0], xs[1])
            with _jax.named_scope("update"):
                return (loss_sum + l_k, _jax.tree.map(_jnp.add, grad_sum, gw_k)), gx_k

        init = (_jnp.zeros((), _jnp.float32), _jax.tree.map(_jnp.zeros_like, weights))
        (loss, grad_w), grad_x = _jax.lax.scan(body, init, (per_example, given["loss_target"]))
    with _jax.named_scope("update"):
        delta_w, new_m, new_v = {}, {}, {}
        for n in TWIN_WEIGHTS:
            delta_w[n], new_m[n], new_v[n] = _adamw(weights[n], grad_w[n], given["m_" + n], given["v_" + n])
    return (loss, grad_x, *[grad_w[n] for n in TWIN_WEIGHTS], *[delta_w[n] for n in TWIN_WEIGHTS],
            *[new_m[n] for n in TWIN_WEIGHTS], *[new_v[n] for n in TWIN_WEIGHTS])
```

```python
import functools

import numpy as np
import jax
import jax.numpy as jnp
from jax import lax
from jax.experimental import pallas as pl
from jax.experimental.pallas import tpu as pltpu

F32 = jnp.float32
BF16 = jnp.bfloat16

EPS_RMS = 1e-6
EPS_LN = 1e-5
EPS_GN = 64e-5
LINE = 64
HEAD = 64
LANES = 128
SUBLANES = 8
SCAN_CHUNK = 32
N_DEV = 8
VMEM_LIMIT = 56 * 1024 * 1024

ADAM_LR = 0.001
ADAM_B1 = 0.9
ADAM_B2 = 0.999
ADAM_EPS = 1e-08
ADAM_WD = 0.01
ADAM_STEP = 10


def _round_up(n, m):
    return (n + m - 1) // m * m


def _params(semantics=None, vmem=VMEM_LIMIT):
    return pltpu.CompilerParams(dimension_semantics=semantics, vmem_limit_bytes=vmem)


def _bdot(a, b):
    return jnp.dot(a.astype(BF16), b.astype(BF16), preferred_element_type=F32)


def _bdot_nt(a, b):
    return lax.dot_general(a.astype(BF16), b.astype(BF16), (((1,), (1,)), ((), ())),
                           preferred_element_type=F32)


def _bdot_tn(a, b):
    return lax.dot_general(a.astype(BF16), b.astype(BF16), (((0,), (0,)), ((), ())),
                           preferred_element_type=F32)


@jax.custom_vjp
def _mm(a, b):
    return _bdot(a, b)


def _mm_fwd(a, b):
    return _bdot(a, b), (a, b)


def _mm_bwd(res, g):
    a, b = res
    return _bdot_nt(g, b), _bdot_tn(a, g)


_mm.defvjp(_mm_fwd, _mm_bwd)


def _seg_sum_raw(x, e):
    hi = x.astype(BF16)
    lo = (x - hi.astype(F32)).astype(BF16)
    return (jnp.dot(hi, e, preferred_element_type=F32)
            + jnp.dot(lo, e, preferred_element_type=F32))


@jax.custom_vjp
def _seg_sum(x, e):
    return _seg_sum_raw(x, e)


def _seg_sum_fwd(x, e):
    return _seg_sum_raw(x, e), e


def _seg_sum_bwd(e, g):
    return _seg_sum_raw(g, e), None


_seg_sum.defvjp(_seg_sum_fwd, _seg_sum_bwd)


def _block_ones(n, seg=HEAD):
    i = np.arange(n) // seg
    return jnp.asarray((i[:, None] == i[None, :]).astype(np.float32), dtype=BF16)


def _rms(xv, g):
    ms = jnp.mean(xv * xv, axis=-1, keepdims=True)
    return xv * lax.rsqrt(ms + EPS_RMS) * g


def _rms_mod(xv, g, shift, scale):
    return _rms(xv, g) * (1.0 + scale) + shift


def _sigmoid(z):
    return 1.0 / (1.0 + jnp.exp(-z))


def _silu(z):
    return z * _sigmoid(z)


def _softplus(z):
    return jnp.maximum(z, 0.0) + jnp.log(1.0 + jnp.exp(-jnp.abs(z)))


class _Cfg:
    def __init__(self, B, TX, TC, D, W, CW, R, GR, KC, F):
        self.B, self.TX, self.TC, self.D = B, TX, TC, D
        self.W, self.CW, self.R, self.GR, self.KC, self.F = W, CW, R, GR, KC, F
        self.T = TX + TC
        self.TT = min(256, TC)
        assert TC % self.TT == 0 and TX % self.TT == 0 and self.TT % LINE == 0
        self.JC = TC // self.TT
        self.JX = TX // self.TT
        self.J = self.JC + self.JX
        self.HP = W // LANES
        self.G = B * self.HP
        self.PW = _round_up(2 * R, LANES)
        self.GP = _round_up(GR, LANES)
        self.SP = 3 * W + 2 * self.PW + self.GP
        self.CP = self.SP + 2 * CW
        self.KP = _round_up(KC, SUBLANES)
        assert self.T % SCAN_CHUNK == 0 and TC % SCAN_CHUNK == 0
        self.NCH = self.T // SCAN_CHUNK
        self.NCC = TC // SCAN_CHUNK
        W_, R_ = W, R
        segs = [(0, 3 * W_, 0),
                (3 * W_, 2 * R_, 3 * W_),
                (3 * W_ + 2 * R_, 2 * R_, 3 * W_ + self.PW),
                (3 * W_ + 4 * R_, GR, 3 * W_ + 2 * self.PW),
                (3 * W_ + 4 * R_ + GR, 2 * CW, self.SP)]
        self.col_segs = segs
        self.shift_cols = 3 * W_ + 4 * R_ + GR
        self.in_cols = self.shift_cols + 2 * CW


def _pad_cols(a, cfg, upto_shift=False):
    width = cfg.SP if upto_shift else cfg.CP
    pieces, pos = [], 0
    for src, n, dst in cfg.col_segs:
        if upto_shift and dst >= cfg.SP:
            break
        if dst > pos:
            pieces.append(jnp.zeros(a.shape[:-1] + (dst - pos,), a.dtype))
        pieces.append(a[..., src:src + n])
        pos = dst + n
    if width > pos:
        pieces.append(jnp.zeros(a.shape[:-1] + (width - pos,), a.dtype))
    return jnp.concatenate(pieces, axis=-1)


def _unpad_cols(a, cfg, upto_shift=False):
    pieces = []
    for src, n, dst in cfg.col_segs:
        if upto_shift and dst >= cfg.SP:
            break
        pieces.append(a[..., dst:dst + n])
    return jnp.concatenate(pieces, axis=-1)


def _pair_weight(w2, cfg):
    R, W = cfg.R, cfg.W
    out = jnp.zeros((cfg.PW, 2 * W), w2.dtype)
    out = out.at[0:R, 0:W].set(w2[0])
    out = out.at[R:2 * R, W:2 * W].set(w2[1])
    return out


def _unpair_weight(g, cfg):
    R, W = cfg.R, cfg.W
    return jnp.stack([g[0:R, 0:W], g[R:2 * R, W:2 * W]])


def _row_ids(n):
    return lax.broadcasted_iota(jnp.int32, (n, 1), 0)


def _shift_rows(z, prev_row, next_row):
    n = z.shape[0]
    rows = _row_ids(n)
    zp = jnp.where(rows == 0, prev_row, pltpu.roll(z, 1, 0))
    zn = jnp.where(rows == n - 1, next_row, pltpu.roll(z, n - 1, 0))
    return zp, zn


def _line_shift(u, d):
    if d == 0:
        return u
    n = u.shape[0]
    lt = _row_ids(n) % LINE
    ok = jnp.logical_and(lt + d >= 0, lt + d < LINE)
    return jnp.where(ok, pltpu.roll(u, (-d) % n, 0), 0.0)


def _dwconv(u, cw, kc):
    pad = kc // 2
    acc = jnp.zeros_like(u)
    for i in range(kc):
        acc = acc + _line_shift(u, i - pad) * cw[i:i + 1, :]
    return acc


def _rwkv_prep(rw, w0, w2p, a0, a2p, k_k, k_a, e, cfg):
    W, PW = cfg.W, cfg.PW
    r = rw[:, 0:W]
    k = rw[:, W:2 * W]
    v = rw[:, 2 * W:3 * W]
    wdp = rw[:, 3 * W:3 * W + PW]
    adp = rw[:, 3 * W + PW:3 * W + 2 * PW]
    wl = w0 + _mm(jnp.tanh(wdp), w2p)
    w_log = -_softplus(-wl) - 0.5
    decay = jnp.exp(-jnp.exp(w_log))
    iclr = _sigmoid(a0 + _mm(adp, a2p))
    kkr = k * k_k
    nrm = jnp.sqrt(_seg_sum(kkr * kkr, e))
    kk = kkr / jnp.maximum(nrm, 1e-12)
    outs = [r, v, kk]
    for d in range(2):
        ic = iclr[:, d * W:(d + 1) * W]
        outs += [decay[:, d * W:(d + 1) * W], k * (1.0 + (ic - 1.0) * k_a), kk * ic]
    return tuple(outs)


def _glu(cv, cfg):
    return cv[:, :cfg.CW] * _sigmoid(cv[:, cfg.CW:])


def _conv_post(y, cb, lw, lb):
    yf = y + cb
    mu = jnp.mean(yf, axis=-1, keepdims=True)
    var = jnp.mean(jnp.square(yf - mu), axis=-1, keepdims=True)
    return _silu((yf - mu) * lax.rsqrt(var + EPS_LN) * lw + lb)


def _readout(y, kbar, r, v, gd, r_k, gw2, lnx_w, lnx_b, e):
    inv = 1.0 / HEAD
    mu = _seg_sum(y, e) * inv
    yc = y - mu
    var = _seg_sum(yc * yc, e) * inv
    yn = yc * lax.rsqrt(var + EPS_GN) * lnx_w + lnx_b
    bonus = _seg_sum(r * kbar * r_k, e) * v
    g = _mm(_sigmoid(gd), gw2)
    return (yn + bonus) * g


def _post_res(xv, mix, gate, g):
    return xv + gate * _rms(mix, g)


def _head_spec(cfg, tmap):
    return pl.BlockSpec((1, cfg.HP, cfg.TT, LANES), lambda b, j: (b, 0, tmap(j), 0))


def _full_spec(shape):
    n = len(shape)
    return pl.BlockSpec(shape, lambda *_: (0,) * n)


def _to_heads(ref, val, cfg):
    for hp in range(cfg.HP):
        ref[0, hp] = val[:, hp * LANES:(hp + 1) * LANES]


def _from_heads(ref, cfg):
    return jnp.concatenate([ref[0, hp] for hp in range(cfg.HP)], axis=-1)


def _in_proj(xcat, modt, g1, w_in_p, cfg):
    B, T, D, TT, CP = cfg.B, cfg.T, cfg.D, cfg.TT, cfg.CP

    def body(x_ref, mod_ref, g_ref, w_ref, p_ref, h_ref):
        h = _rms_mod(x_ref[0], g_ref[...], mod_ref[0, 0, 0:1, :], mod_ref[0, 0, 1:2, :])
        hb = h.astype(BF16)
        h_ref[0] = hb
        p_ref[0] = jnp.dot(hb, w_ref[...], preferred_element_type=F32)

    return pl.pallas_call(
        body, name="in_proj", grid=(B, cfg.J),
        in_specs=[pl.BlockSpec((1, TT, D), lambda b, j: (b, j, 0)),
                  pl.BlockSpec((1, 1, 2, D), lambda b, j: (b, j, 0, 0)),
                  _full_spec((1, D)), _full_spec((D, CP))],
        out_specs=[pl.BlockSpec((1, TT, CP), lambda b, j: (b, j, 0)),
                   pl.BlockSpec((1, TT, D), lambda b, j: (b, j, 0))],
        out_shape=[jax.ShapeDtypeStruct((B, T, CP), F32), jax.ShapeDtypeStruct((B, T, D), BF16)],
        compiler_params=_params(("parallel", "parallel")),
    )(xcat, modt, g1, w_in_p)


def _halo_specs(cfg, width):
    per = cfg.TT // SUBLANES
    last = cfg.T // SUBLANES - 1
    prev = pl.BlockSpec((1, SUBLANES, width), lambda b, j: (b, jnp.maximum(j * per - 1, 0), 0))
    nxt = pl.BlockSpec((1, SUBLANES, width), lambda b, j: (b, jnp.minimum((j + 1) * per, last), 0))
    return prev, nxt


def _halo_flags(j, cfg):
    has_prev = jnp.logical_and(j != 0, j != cfg.JC).astype(F32)
    has_next = jnp.logical_and(j != cfg.JC - 1, j != cfg.J - 1).astype(F32)
    return has_prev, has_next


def _shifted(p_ref, prev_ref, next_ref, mup, mun, j, cfg):
    SP = cfg.SP
    has_prev, has_next = _halo_flags(j, cfg)
    z = p_ref[0][:, :SP]
    zp, zn = _shift_rows(z, prev_ref[0, SUBLANES - 1:SUBLANES, :] * has_prev, next_ref[0, 0:1, :] * has_next)
    return z, zp, zn, z + mup * (zp - z) + mun * (zn - z)


def _mix_prep(p, mup, mun, w0, w2p, a0, a2p, k_k, k_a, cw, cb, clw, clb, e_w, cfg):
    B, T, TT, SP, CP, W, CW, HP = cfg.B, cfg.T, cfg.TT, cfg.SP, cfg.CP, cfg.W, cfg.CW, cfg.HP

    def body(p_ref, prev_ref, next_ref, mup_ref, mun_ref, w0_ref, w2_ref, a0_ref, a2_ref, kk_ref, ka_ref,
             cw_ref, cb_ref, clw_ref, clb_ref, e_ref, *outs):
        j = pl.program_id(1)
        _, _, _, rw = _shifted(p_ref, prev_ref, next_ref, mup_ref[...], mun_ref[...], j, cfg)
        vals = _rwkv_prep(rw, w0_ref[...], w2_ref[...], a0_ref[...], a2_ref[...], kk_ref[...], ka_ref[...],
                          e_ref[...], cfg)
        for ref, val in zip(outs[:9], vals):
            _to_heads(ref, val, cfg)
        outs[9][0] = rw[:, 3 * W + 2 * cfg.PW:SP]
        u = _glu(p_ref[0][:, SP:], cfg)
        outs[10][0] = _conv_post(_dwconv(u, cw_ref[...], cfg.KC), cb_ref[...], clw_ref[...], clb_ref[...])

    prev, nxt = _halo_specs(cfg, SP)
    head = jax.ShapeDtypeStruct((B, HP, T, LANES), F32)
    return pl.pallas_call(
        body, name="mix_prep", grid=(B, cfg.J),
        in_specs=[pl.BlockSpec((1, TT, CP), lambda b, j: (b, j, 0)), prev, nxt,
                  _full_spec((1, SP)), _full_spec((1, SP)),
                  _full_spec((1, 2 * W)), _full_spec((cfg.PW, 2 * W)),
                  _full_spec((1, 2 * W)), _full_spec((cfg.PW, 2 * W)),
                  _full_spec((1, W)), _full_spec((1, W)),
                  _full_spec((cfg.KP, CW)), _full_spec((1, CW)), _full_spec((1, CW)), _full_spec((1, CW)),
                  _full_spec((W, W))],
        out_specs=[_head_spec(cfg, lambda j: j)] * 9
                  + [pl.BlockSpec((1, TT, cfg.GP), lambda b, j: (b, j, 0)),
                     pl.BlockSpec((1, TT, CW), lambda b, j: (b, j, 0))],
        out_shape=[head] * 9 + [jax.ShapeDtypeStruct((B, T, cfg.GP), F32),
                                jax.ShapeDtypeStruct((B, T, CW), F32)],
        compiler_params=_params(("parallel", "parallel")),
    )(p, p, p, mup, mun, w0, w2p, a0, a2p, k_k, k_a, cw, cb, clw, clb, e_w)


def _chunk_pos(c, reverse, cfg):
    if not reverse:
        return c
    return jnp.where(c < cfg.NCC, cfg.NCC - 1 - c, cfg.NCH - 1 + cfg.NCC - c)


def _diag_mask():
    r = lax.broadcasted_iota(jnp.int32, (HEAD, LANES), 0)
    l = lax.broadcasted_iota(jnp.int32, (HEAD, LANES), 1)
    return (r == l % HEAD).astype(F32)


def _seg3(x, e):
    g = x.shape[0]
    return _seg_sum_raw(x.reshape(g * HEAD, LANES), e).reshape(g, HEAD, LANES)


def _scan_step(S, r, w, k, v, kk, b, diag, e):
    sa = _seg3(S * (-kk), e)
    vb = _seg3(diag * v, e)
    S = S * w + sa * b + vb * k
    yb = _seg3(S * r, e)
    y = jnp.sum(diag * yb, axis=1, keepdims=True)
    return S, y


def _scan_fwd(ops, e128, reverse, cfg):
    G, T, NCH = cfg.G, cfg.T, cfg.NCH

    def body(r_ref, w_ref, k_ref, v_ref, kk_ref, b_ref, e_ref, y_ref, ck_ref, s_ref):
        c = pl.program_id(0)

        @pl.when(c == 0)
        def _():
            s_ref[...] = jnp.zeros_like(s_ref)

        ck_ref[0] = s_ref[...]
        diag = _diag_mask()
        e = e_ref[...]

        def step(i, carry):
            t = (SCAN_CHUNK - 1 - i) if reverse else i
            row = lambda ref: ref[:, pl.ds(t, 1), :]
            S, y = _scan_step(s_ref[...], row(r_ref), row(w_ref), row(k_ref), row(v_ref), row(kk_ref),
                              row(b_ref), diag, e)
            s_ref[...] = S
            y_ref[:, pl.ds(t, 1), :] = y
            return carry

        lax.fori_loop(0, SCAN_CHUNK, step, 0)

    tok = pl.BlockSpec((G, SCAN_CHUNK, LANES), lambda c: (0, _chunk_pos(c, reverse, cfg), 0))
    return pl.pallas_call(
        body, name="scan_fwd_" + ("b" if reverse else "f"), grid=(NCH,),
        in_specs=[tok] * 6 + [_full_spec((LANES, LANES))],
        out_specs=[tok, pl.BlockSpec((1, G, HEAD, LANES), lambda c: (c, 0, 0, 0))],
        out_shape=[jax.ShapeDtypeStruct((G, T, LANES), F32),
                   jax.ShapeDtypeStruct((NCH, G, HEAD, LANES), F32)],
        scratch_shapes=[pltpu.VMEM((G, HEAD, LANES), F32)],
        compiler_params=_params(("arbitrary",)),
    )(*ops, e128)


def _mix_out(yf, yb, kdf, kdb, r, v, gd, conv, x, mod2, r_k, gw2p, lnx_w, lnx_b, w_out, g2, e_w, cfg):
    B, TX, D, TT, W, CW, JC = cfg.B, cfg.TX, cfg.D, cfg.TT, cfg.W, cfg.CW, cfg.JC

    def body(yf_ref, yb_ref, kdf_ref, kdb_ref, r_ref, v_ref, gd_ref, cv_ref, x_ref, mod_ref,
             rk_ref, gw_ref, lw_ref, lb_ref, wo_ref, g_ref, e_ref, x1_ref):
        y = _from_heads(yf_ref, cfg) + _from_heads(yb_ref, cfg)
        kbar = 0.5 * (_from_heads(kdf_ref, cfg) + _from_heads(kdb_ref, cfg))
        ro = _readout(y, kbar, _from_heads(r_ref, cfg), _from_heads(v_ref, cfg), gd_ref[0], rk_ref[...],
                      gw_ref[...], lw_ref[...], lb_ref[...], e_ref[...])
        cat = jnp.concatenate([ro, cv_ref[0]], axis=-1)
        mix = _bdot(cat, wo_ref[...])
        x1_ref[0] = _post_res(x_ref[0], mix, mod_ref[0], g_ref[...])

    hs = _head_spec(cfg, lambda j: j + JC)
    lat = lambda n: pl.BlockSpec((1, TT, n), lambda b, j: (b, j + JC, 0))
    return pl.pallas_call(
        body, name="mix_out", grid=(B, cfg.JX),
        in_specs=[hs] * 6 + [lat(cfg.GP), lat(CW),
                             pl.BlockSpec((1, TT, D), lambda b, j: (b, j, 0)),
                             pl.BlockSpec((1, 1, D), lambda b, j: (b, 0, 0)),
                             _full_spec((1, W)), _full_spec((cfg.GP, W)), _full_spec((1, W)), _full_spec((1, W)),
                             _full_spec((W + CW, D)), _full_spec((1, D)), _full_spec((W, W))],
        out_specs=pl.BlockSpec((1, TT, D), lambda b, j: (b, j, 0)),
        out_shape=jax.ShapeDtypeStruct((B, TX, D), F32),
        compiler_params=_params(("parallel", "parallel")),
    )(yf, yb, kdf, kdb, r, v, gd, conv, x, mod2, r_k, gw2p, lnx_w, lnx_b, w_out, g2, e_w)


def _acc(ref, val, first):
    @pl.when(first)
    def _():
        ref[...] = val

    @pl.when(jnp.logical_not(first))
    def _():
        ref[...] += val


def _mlp_fwd_bwd(x1, tgt, mod345, g3, g4, w1, w2, cfg):
    B, TX, D, TT, F, JX = cfg.B, cfg.TX, cfg.D, cfg.TT, cfg.F, cfg.JX

    def body(x1_ref, t_ref, mod_ref, g3_ref, g4_ref, w1_ref, w2_ref,
             dx1_ref, loss_ref, h2_ref, dpre_ref, act_ref, dff_ref, dmod_ref, dg3_ref, dg4_ref):
        b, j = pl.program_id(0), pl.program_id(1)
        x1v = x1_ref[0]
        sh, sc, gt = mod_ref[0, 0:1, :], mod_ref[0, 1:2, :], mod_ref[0, 2:3, :]
        h2, vjp_pre = jax.vjp(_rms_mod, x1v, g3_ref[...], sh, sc)
        h2b = h2.astype(BF16)
        pre = jnp.dot(h2b, w1_ref[...], preferred_element_type=F32)
        rl = jnp.maximum(pre, 0.0)
        actb = (rl * rl).astype(BF16)
        ff = jnp.dot(actb, w2_ref[...], preferred_element_type=F32)
        x2, vjp_post = jax.vjp(_post_res, x1v, ff, gt, g4_ref[...])
        err = x2 - t_ref[0]
        loss = 0.5 * jnp.sum(jnp.mean(err * err, axis=-1, keepdims=True))
        dx1a, dff, dgt, dg4 = vjp_post(err * (1.0 / D))
        dffb = dff.astype(BF16)
        dpre = _bdot_nt(dffb, w2_ref[...]) * (2.0 * rl)
        dpreb = dpre.astype(BF16)
        dx1b, dg3, dsh, dsc = vjp_pre(_bdot_nt(dpreb, w1_ref[...]))
        dx1_ref[0] = dx1a + dx1b
        loss_ref[0, 0] = jnp.zeros((SUBLANES, LANES), F32) + loss
        h2_ref[0] = h2b
        dpre_ref[0] = dpreb
        act_ref[0] = actb
        dff_ref[0] = dffb
        _acc(dmod_ref, jnp.concatenate([dsh, dsc, dgt], axis=0)[None], j == 0)
        first = jnp.logical_and(b == 0, j == 0)
        _acc(dg3_ref, dg3, first)
        _acc(dg4_ref, dg4, first)

    tile = lambda n: pl.BlockSpec((1, TT, n), lambda b, j: (b, j, 0))
    return pl.pallas_call(
        body, name="mlp_fwd_bwd", grid=(B, JX),
        in_specs=[tile(D), tile(D), pl.BlockSpec((1, 3, D), lambda b, j: (b, 0, 0)),
                  _full_spec((1, D)), _full_spec((1, D)),
                  pl.BlockSpec((D, F), lambda b, j: (0, 0), pipeline_mode=pl.Buffered(1)),
                  pl.BlockSpec((F, D), lambda b, j: (0, 0), pipeline_mode=pl.Buffered(1))],
        out_specs=[tile(D), pl.BlockSpec((1, 1, SUBLANES, LANES), lambda b, j: (b, j, 0, 0)),
                   tile(D), tile(F), tile(F), tile(D),
                   pl.BlockSpec((1, 3, D), lambda b, j: (b, 0, 0)),
                   _full_spec((1, D)), _full_spec((1, D))],
        out_shape=[jax.ShapeDtypeStruct((B, TX, D), F32),
                   jax.ShapeDtypeStruct((B, JX, SUBLANES, LANES), F32),
                   jax.ShapeDtypeStruct((B, TX, D), BF16), jax.ShapeDtypeStruct((B, TX, F), BF16),
                   jax.ShapeDtypeStruct((B, TX, F), BF16), jax.ShapeDtypeStruct((B, TX, D), BF16),
                   jax.ShapeDtypeStruct((B, 3, D), F32),
                   jax.ShapeDtypeStruct((1, D), F32), jax.ShapeDtypeStruct((1, D), F32)],
        compiler_params=_params(("arbitrary", "arbitrary")),
    )(x1, tgt, mod345, g3, g4, w1, w2)


def _mix_out_bwd(yf, yb, kdf, kdb, r, v, gd, conv, x, mod2, r_k, gw2p, lnx_w, lnx_b, w_out, g2, e_w, dx1, cfg):
    B, TX, D, TT, W, CW, JC, HP, GP = cfg.B, cfg.TX, cfg.D, cfg.TT, cfg.W, cfg.CW, cfg.JC, cfg.HP, cfg.GP

    def body(yf_ref, yb_ref, kdf_ref, kdb_ref, r_ref, v_ref, gd_ref, cv_ref, x_ref, mod_ref,
             rk_ref, gw_ref, lw_ref, lb_ref, wo_ref, g_ref, e_ref, dx1_ref,
             dy_ref, dkb_ref, dr_ref, dv_ref, dgd_ref, dcv_ref, cat_ref, dmix_ref,
             dmod_ref, dg2_ref, drk_ref, dgw_ref, dlw_ref, dlb_ref):
        b, j = pl.program_id(0), pl.program_id(1)
        e = e_ref[...]
        y = _from_heads(yf_ref, cfg) + _from_heads(yb_ref, cfg)
        kbar = 0.5 * (_from_heads(kdf_ref, cfg) + _from_heads(kdb_ref, cfg))
        ro, vjp_ro = jax.vjp(lambda *a: _readout(*a, e), y, kbar, _from_heads(r_ref, cfg),
                             _from_heads(v_ref, cfg), gd_ref[0], rk_ref[...], gw_ref[...], lw_ref[...], lb_ref[...])
        catb = jnp.concatenate([ro, cv_ref[0]], axis=-1).astype(BF16)
        mix = jnp.dot(catb, wo_ref[...], preferred_element_type=F32)
        _, vjp_post = jax.vjp(_post_res, x_ref[0], mix, mod_ref[0], g_ref[...])
        _, dmix, dgate, dg2 = vjp_post(dx1_ref[0])
        dmixb = dmix.astype(BF16)
        dcat = _bdot_nt(dmixb, wo_ref[...])
        dy, dkb, dr, dv, dgd, drk, dgw, dlw, dlb = vjp_ro(dcat[:, :W])
        _to_heads(dy_ref, dy, cfg)
        _to_heads(dkb_ref, dkb, cfg)
        _to_heads(dr_ref, dr, cfg)
        _to_heads(dv_ref, dv, cfg)
        dgd_ref[0] = dgd
        dcv_ref[0] = dcat[:, W:]
        cat_ref[0] = catb
        dmix_ref[0] = dmixb
        _acc(dmod_ref, dgate[None], j == 0)
        first = jnp.logical_and(b == 0, j == 0)
        _acc(dg2_ref, dg2, first)
        _acc(drk_ref, drk, first)
        _acc(dgw_ref, dgw, first)
        _acc(dlw_ref, dlw, first)
        _acc(dlb_ref, dlb, first)

    hs = _head_spec(cfg, lambda j: j + JC)
    ho = _head_spec(cfg, lambda j: j)
    lat = lambda n: pl.BlockSpec((1, TT, n), lambda b, j: (b, j + JC, 0))
    tile = lambda n: pl.BlockSpec((1, TT, n), lambda b, j: (b, j, 0))
    head = jax.ShapeDtypeStruct((B, HP, TX, LANES), F32)
    vec = lambda n: jax.ShapeDtypeStruct((1, n), F32)
    return pl.pallas_call(
        body, name="mix_out_bwd", grid=(B, cfg.JX),
        in_specs=[hs] * 6 + [lat(GP), lat(CW), tile(D),
                             pl.BlockSpec((1, 1, D), lambda b, j: (b, 0, 0)),
                             _full_spec((1, W)), _full_spec((GP, W)), _full_spec((1, W)), _full_spec((1, W)),
                             _full_spec((W + CW, D)), _full_spec((1, D)), _full_spec((W, W)), tile(D)],
        out_specs=[ho] * 4 + [tile(GP), tile(CW), tile(W + CW), tile(D),
                              pl.BlockSpec((1, 1, D), lambda b, j: (b, 0, 0)),
                              _full_spec((1, D)), _full_spec((1, W)), _full_spec((GP, W)),
                              _full_spec((1, W)), _full_spec((1, W))],
        out_shape=[head] * 4 + [jax.ShapeDtypeStruct((B, TX, GP), F32), jax.ShapeDtypeStruct((B, TX, CW), F32),
                                jax.ShapeDtypeStruct((B, TX, W + CW), BF16), jax.ShapeDtypeStruct((B, TX, D), BF16),
                                jax.ShapeDtypeStruct((B, 1, D), F32),
                                vec(D), vec(W), jax.ShapeDtypeStruct((GP, W), F32), vec(W), vec(W)],
        compiler_params=_params(("arbitrary", "arbitrary")),
    )(yf, yb, kdf, kdb, r, v, gd, conv, x, mod2, r_k, gw2p, lnx_w, lnx_b, w_out, g2, e_w, dx1)


def _scan_bwd(ops, dy, ck, e128, reverse, cfg):
    G, T, NCH, NCC = cfg.G, cfg.T, cfg.NCH, cfg.NCC
    CH = SCAN_CHUNK

    def body(r_ref, w_ref, k_ref, v_ref, kk_ref, b_ref, dy_ref, ck_ref, e_ref,
             dr_ref, dw_ref, dk_ref, dv_ref, dkk_ref, db_ref, hist_ref, ds_ref):
        gi = pl.program_id(0)
        pos = _chunk_pos(NCH - 1 - gi, reverse, cfg)
        latent = (pos >= NCC).astype(F32)

        @pl.when(gi == 0)
        def _():
            ds_ref[...] = jnp.zeros_like(ds_ref)

        diag = _diag_mask()
        e = e_ref[...]
        hist_ref[0] = ck_ref[0]

        def tok(i):
            return (CH - 1 - i) if reverse else i

        def fstep(i, carry):
            t = tok(i)
            row = lambda ref: ref[:, pl.ds(t, 1), :]
            S = hist_ref[i]
            sa = _seg3(S * (-row(kk_ref)), e)
            vb = _seg3(diag * row(v_ref), e)
            hist_ref[i + 1] = S * row(w_ref) + sa * row(b_ref) + vb * row(k_ref)
            return carry

        lax.fori_loop(0, CH, fstep, 0)

        def bstep(ii, carry):
            i = CH - 1 - ii
            t = tok(i)
            row = lambda ref: ref[:, pl.ds(t, 1), :]
            rsum = lambda z: jnp.sum(z, axis=1, keepdims=True)
            r, w, k, v, kk, bv = (row(r_ref), row(w_ref), row(k_ref), row(v_ref), row(kk_ref), row(b_ref))
            sp = hist_ref[i]
            sn = hist_ref[i + 1]
            dyc = _seg3(diag * (row(dy_ref) * latent), e)
            ds = ds_ref[...] + dyc * r
            dr_ref[:, pl.ds(t, 1), :] = rsum(sn * dyc)
            sa = _seg3(sp * (-kk), e)
            dw_ref[:, pl.ds(t, 1), :] = rsum(ds * sp)
            db_ref[:, pl.ds(t, 1), :] = rsum(ds * sa)
            dsa = _seg3(ds * bv, e)
            dkk_ref[:, pl.ds(t, 1), :] = -rsum(sp * dsa)
            dvc = _seg3(ds * k, e)
            dv_ref[:, pl.ds(t, 1), :] = rsum(diag * dvc)
            vc = _seg3(diag * v, e)
            dk_ref[:, pl.ds(t, 1), :] = rsum(ds * vc)
            ds_ref[...] = ds * w - dsa * kk
            return carry

        lax.fori_loop(0, CH, bstep, 0)

    cpos = lambda g: _chunk_pos(NCH - 1 - g, reverse, cfg)
    tok_spec = pl.BlockSpec((G, CH, LANES), lambda g: (0, cpos(g), 0))
    dy_spec = pl.BlockSpec((G, CH, LANES), lambda g: (0, jnp.maximum(cpos(g) - NCC, 0), 0))
    out = jax.ShapeDtypeStruct((G, T, LANES), F32)
    return pl.pallas_call(
        body, name="scan_bwd_" + ("b" if reverse else "f"), grid=(NCH,),
        in_specs=[tok_spec] * 6 + [dy_spec, pl.BlockSpec((1, G, HEAD, LANES), lambda g: (NCH - 1 - g, 0, 0, 0)),
                                   _full_spec((LANES, LANES))],
        out_specs=[tok_spec] * 6,
        out_shape=[out] * 6,
        scratch_shapes=[pltpu.VMEM((CH + 1, G, HEAD, LANES), F32), pltpu.VMEM((G, HEAD, LANES), F32)],
        compiler_params=_params(("arbitrary",)),
    )(*ops, dy, ck, e128)


def _mix_prep_bwd(p, mup, mun, w0, w2p, a0, a2p, k_k, k_a, cw, cb, clw, clb, e_w, sf, sb, ro, cfg):
    B, T, TT, SP, CP, W, CW, HP, JC, PW, GP, KC, KP = (cfg.B, cfg.T, cfg.TT, cfg.SP, cfg.CP, cfg.W, cfg.CW,
                                                       cfg.HP, cfg.JC, cfg.PW, cfg.GP, cfg.KC, cfg.KP)
    pad = KC // 2

    def body(p_ref, prev_ref, next_ref, mup_ref, mun_ref, w0_ref, w2_ref, a0_ref, a2_ref, kk_ref, ka_ref,
             cw_ref, cb_ref, clw_ref, clb_ref, e_ref, *rest):
        sf_refs, sb_refs = rest[0:6], rest[6:12]
        rdr_ref, rdv_ref, rdkb_ref, rdgd_ref, rdcv_ref = rest[12:17]
        (dpz_ref, dmup_ref, dmun_ref, dw0_ref, dw2_ref, da0_ref, da2_ref, dkk_ref, dka_ref,
         dcw_ref, dcb_ref, dclw_ref, dclb_ref) = rest[17:]
        b, j = pl.program_id(0), pl.program_id(1)
        first = jnp.logical_and(b == 0, j == 0)
        lat = (j >= JC).astype(F32)
        e = e_ref[...]
        mup_v, mun_v = mup_ref[...], mun_ref[...]
        z, zp, zn, rw = _shifted(p_ref, prev_ref, next_ref, mup_v, mun_v, j, cfg)

        def prep(rw_, w0_, w2_, a0_, a2_, kk_, ka_):
            return _rwkv_prep(rw_, w0_, w2_, a0_, a2_, kk_, ka_, e, cfg) + (rw_[:, 3 * W + 2 * PW:SP],)

        _, vjp_prep = jax.vjp(prep, rw, w0_ref[...], w2_ref[...], a0_ref[...], a2_ref[...], kk_ref[...], ka_ref[...])
        fr, fw, fk, fv, fkk, fb = [_from_heads(r_, cfg) for r_ in sf_refs]
        br, bw, bk, bv, bkk, bb = [_from_heads(r_, cfg) for r_ in sb_refs]
        half_kb = (0.5 * lat) * _from_heads(rdkb_ref, cfg)
        cots = (fr + br + lat * _from_heads(rdr_ref, cfg), fv + bv + lat * _from_heads(rdv_ref, cfg), fkk + bkk,
                fw, fk + half_kb, fb, bw, bk + half_kb, bb, lat * rdgd_ref[0])
        drw, dw0, dw2, da0, da2, dkk, dka = vjp_prep(cots)
        _acc(dmup_ref, jnp.sum(drw * (zp - z), axis=0, keepdims=True), first)
        _acc(dmun_ref, jnp.sum(drw * (zn - z), axis=0, keepdims=True), first)
        for ref, val in ((dw0_ref, dw0), (dw2_ref, dw2), (da0_ref, da0), (da2_ref, da2), (dkk_ref, dkk), (dka_ref, dka)):
            _acc(ref, val, first)

        cvv = p_ref[0, :, SP:]
        u, vjp_glu = jax.vjp(lambda c_: _glu(c_, cfg), cvv)
        cwv = cw_ref[...]
        _, vjp_post = jax.vjp(_conv_post, _dwconv(u, cwv, KC), cb_ref[...], clw_ref[...], clb_ref[...])
        dyc, dcb, dclw, dclb = vjp_post(lat * rdcv_ref[0])
        du = jnp.zeros_like(u)
        for i in range(KC):
            du = du + _line_shift(dyc, pad - i) * cwv[i:i + 1, :]
        (dcv,) = vjp_glu(du)
        dpz_ref[0, :, 0:SP] = drw
        dpz_ref[0, :, SP:] = dcv

        @pl.when(first)
        def _():
            dcw_ref[...] = jnp.zeros_like(dcw_ref)

        for i in range(KC):
            dcw_ref[i:i + 1, :] += jnp.sum(dyc * _line_shift(u, i - pad), axis=0, keepdims=True)
        _acc(dcb_ref, dcb, first)
        _acc(dclw_ref, dclw, first)
        _acc(dclb_ref, dclb, first)

    prev, nxt = _halo_specs(cfg, SP)
    hs = _head_spec(cfg, lambda j: j)
    hl = _head_spec(cfg, lambda j: jnp.maximum(j - JC, 0))
    latn = lambda n: pl.BlockSpec((1, TT, n), lambda b, j: (b, jnp.maximum(j - JC, 0), 0))
    vec = lambda n: jax.ShapeDtypeStruct((1, n), F32)
    small_shapes = [vec(SP), vec(SP), vec(2 * W), jax.ShapeDtypeStruct((PW, 2 * W), F32), vec(2 * W),
                    jax.ShapeDtypeStruct((PW, 2 * W), F32), vec(W), vec(W),
                    jax.ShapeDtypeStruct((KP, CW), F32), vec(CW), vec(CW), vec(CW)]
    return pl.pallas_call(
        body, name="mix_prep_bwd", grid=(B, cfg.J),
        in_specs=[pl.BlockSpec((1, TT, CP), lambda b, j: (b, j, 0)), prev, nxt,
                  _full_spec((1, SP)), _full_spec((1, SP)),
                  _full_spec((1, 2 * W)), _full_spec((PW, 2 * W)),
                  _full_spec((1, 2 * W)), _full_spec((PW, 2 * W)),
                  _full_spec((1, W)), _full_spec((1, W)),
                  _full_spec((KP, CW)), _full_spec((1, CW)), _full_spec((1, CW)), _full_spec((1, CW)),
                  _full_spec((W, W))] + [hs] * 12 + [hl] * 3 + [latn(GP), latn(CW)],
        out_specs=[pl.BlockSpec((1, TT, CP), lambda b, j: (b, j, 0))] + [_full_spec(s.shape) for s in small_shapes],
        out_shape=[jax.ShapeDtypeStruct((B, T, CP), F32)] + small_shapes,
        compiler_params=_params(("arbitrary", "arbitrary")),
    )(p, p, p, mup, mun, w0, w2p, a0, a2p, k_k, k_a, cw, cb, clw, clb, e_w, *sf, *sb, *ro)


def _in_proj_bwd(dpz, xcat, modt, g1, w_in_p, mup, mun, dx1, cfg):
    B, T, TX, D, TT, SP, CP, JC = cfg.B, cfg.T, cfg.TX, cfg.D, cfg.TT, cfg.SP, cfg.CP, cfg.JC

    def body(d_ref, prev_ref, next_ref, x_ref, mod_ref, g_ref, w_ref, mup_ref, mun_ref, dx1_ref,
             gx_ref, dp_ref, dmod_ref, dg_ref):
        b, j = pl.program_id(0), pl.program_id(1)
        has_prev, has_next = _halo_flags(j, cfg)
        mp, mn = mup_ref[...], mun_ref[...]
        drw = d_ref[0, :, 0:SP]
        dprev, dnext = _shift_rows(drw, prev_ref[0, SUBLANES - 1:SUBLANES, :] * has_prev,
                                   next_ref[0, 0:1, :] * has_next)
        dz = drw * (1.0 - mp - mn) + mp * dnext + mn * dprev
        dpb = jnp.concatenate([dz, d_ref[0, :, SP:]], axis=-1).astype(BF16)
        dp_ref[0] = dpb
        dh = _bdot_nt(dpb, w_ref[...])
        _, vjp_h = jax.vjp(_rms_mod, x_ref[0], g_ref[...], mod_ref[0, 0, 0:1, :], mod_ref[0, 0, 1:2, :])
        dx, dg, dsh, dsc = vjp_h(dh)
        dmod_ref[0, 0] = jnp.concatenate([dsh, dsc], axis=0)
        _acc(dg_ref, dg, jnp.logical_and(b == 0, j == 0))

        @pl.when(j >= JC)
        def _():
            gx_ref[0] = dx + dx1_ref[0]

    prev, nxt = _halo_specs(cfg, SP)
    lat = pl.BlockSpec((1, TT, D), lambda b, j: (b, jnp.maximum(j - JC, 0), 0))
    return pl.pallas_call(
        body, name="in_proj_bwd", grid=(B, cfg.J),
        in_specs=[pl.BlockSpec((1, TT, CP), lambda b, j: (b, j, 0)), prev, nxt,
                  pl.BlockSpec((1, TT, D), lambda b, j: (b, j, 0)),
                  pl.BlockSpec((1, 1, 2, D), lambda b, j: (b, j, 0, 0)),
                  _full_spec((1, D)), _full_spec((D, CP)), _full_spec((1, SP)), _full_spec((1, SP)), lat],
        out_specs=[lat, pl.BlockSpec((1, TT, CP), lambda b, j: (b, j, 0)),
                   pl.BlockSpec((1, 1, 2, D), lambda b, j: (b, j, 0, 0)), _full_spec((1, D))],
        out_shape=[jax.ShapeDtypeStruct((B, TX, D), F32), jax.ShapeDtypeStruct((B, T, CP), BF16),
                   jax.ShapeDtypeStruct((B, cfg.J, 2, D), F32), jax.ShapeDtypeStruct((1, D), F32)],
        compiler_params=_params(("arbitrary", "arbitrary")),
    )(dpz, dpz, dpz, xcat, modt, g1, w_in_p, mup, mun, dx1)


def _pick_tile(n, pref):
    for t in pref:
        if n % t == 0:
            return t
    return n


def _grad_matmul(a, g, name):
    K, M = a.shape
    N = g.shape[1]
    tm = _pick_tile(M, (512, 256, 128))
    tn = _pick_tile(N, (1024, 768, 512, 256, 128))
    tk = _pick_tile(K, (512, 256, 128, 64))
    nk = K // tk

    def body(a_ref, g_ref, o_ref):
        k = pl.program_id(2)
        _acc(o_ref, _bdot_tn(a_ref[...], g_ref[...]), k == 0)

    return pl.pallas_call(
        body, name=name, grid=(M // tm, N // tn, nk),
        in_specs=[pl.BlockSpec((tk, tm), lambda i, j, k: (k, i)),
                  pl.BlockSpec((tk, tn), lambda i, j, k: (k, j))],
        out_specs=pl.BlockSpec((tm, tn), lambda i, j, k: (i, j)),
        out_shape=jax.ShapeDtypeStruct((M, N), F32),
        compiler_params=_params(("parallel", "parallel", "arbitrary")),
    )(a, g)


def _ada_fwd(crows, ada_w, ada_b):
    D = crows.shape[1]
    n6 = ada_w.shape[1]
    tn = _pick_tile(n6, (1024, 512, 256, 128))

    def body(c_ref, w_ref, b_ref, s_ref, m_ref):
        s = _silu(c_ref[...])
        s_ref[...] = s
        m_ref[...] = _bdot(s, w_ref[...]) + b_ref[...]

    return pl.pallas_call(
        body, name="ada_fwd", grid=(n6 // tn,),
        in_specs=[_full_spec((SUBLANES, D)), pl.BlockSpec((D, tn), lambda i: (0, i)),
                  pl.BlockSpec((1, tn), lambda i: (0, i))],
        out_specs=[_full_spec((SUBLANES, D)), pl.BlockSpec((SUBLANES, tn), lambda i: (0, i))],
        out_shape=[jax.ShapeDtypeStruct((SUBLANES, D), F32), jax.ShapeDtypeStruct((SUBLANES, n6), F32)],
        compiler_params=_params(("arbitrary",)),
    )(crows, ada_w, ada_b)


def _ada_bwd(s_all, g_all, g_mine, c_ctx, ada_w, nb):
    D = s_all.shape[1]
    n6 = g_all.shape[1]
    ns = g_mine.shape[1]

    def body(s_ref, g_ref, gm_ref, c_ref, w_ref, dw_ref, db_ref, dc_ref):
        g = g_ref[...]
        dw_ref[...] = _bdot_tn(s_ref[...], gm_ref[...])
        db_ref[...] = jnp.sum(g, axis=0, keepdims=True)
        rows = lax.broadcasted_iota(jnp.int32, (g.shape[0], 1), 0)
        gc = jnp.sum(jnp.where(rows % SUBLANES == nb, g, 0.0), axis=0, keepdims=True)
        ds = _bdot_nt(gc, w_ref[...])
        c = c_ref[...]
        sg = _sigmoid(c)
        dc_ref[...] = ds * (sg + c * sg * (1.0 - sg))

    return pl.pallas_call(
        body, name="ada_bwd",
        out_shape=[jax.ShapeDtypeStruct((D, ns), F32), jax.ShapeDtypeStruct((1, n6), F32),
                   jax.ShapeDtypeStruct((1, D), F32)],
        compiler_params=_params(),
    )(s_all, g_all, g_mine, c_ctx, ada_w)


def _adamw(parts, w, m, v, name):
    P, R, C = parts.shape
    tr = _pick_tile(R, (256, 128, 64, 32, 16, 8))

    def body(p_ref, w_ref, m_ref, v_ref, g_ref, d_ref, nm_ref, nv_ref):
        g = p_ref[0]
        for i in range(1, P):
            g = g + p_ref[i]
        nm = ADAM_B1 * m_ref[...] + (1.0 - ADAM_B1) * g
        nv = ADAM_B2 * v_ref[...] + (1.0 - ADAM_B2) * (g * g)
        m_hat = nm / (1.0 - ADAM_B1 ** ADAM_STEP)
        v_hat = nv / (1.0 - ADAM_B2 ** ADAM_STEP)
        g_ref[...] = g
        d_ref[...] = -ADAM_LR * (m_hat / (jnp.sqrt(v_hat) + ADAM_EPS) + ADAM_WD * w_ref[...])
        nm_ref[...] = nm
        nv_ref[...] = nv

    blk = pl.BlockSpec((tr, C), lambda i: (i, 0))
    out = jax.ShapeDtypeStruct((R, C), F32)
    return pl.pallas_call(
        body, name=name, grid=(R // tr,),
        in_specs=[pl.BlockSpec((P, tr, C), lambda i: (0, i, 0)), blk, blk, blk],
        out_specs=[blk] * 4, out_shape=[out] * 4,
        compiler_params=_params(("parallel",)),
    )(parts, w, m, v)


def _local_step(cfg, x, c, ctx, tgt, fw):
    B, D, W, CW, JC, T, TX = cfg.B, cfg.D, cfg.W, cfg.CW, cfg.JC, cfg.T, cfg.TX
    e_w, e128 = _block_ones(W), _block_ones(LANES)
    row = lambda a: a.reshape(1, -1)

    ada_wb = fw["ada_w"].astype(BF16)
    w_in_p = _pad_cols(fw["w_in"], cfg).astype(BF16)
    mup = _pad_cols(fw["mu_prev"], cfg, True)
    mun = _pad_cols(fw["mu_next"], cfg, True)
    w0, a0 = row(fw["decay_w0"]), row(fw["iclr_a0"])
    w2p, a2p = _pair_weight(fw["decay_w2"], cfg), _pair_weight(fw["iclr_a2"], cfg)
    cw = jnp.pad(fw["conv_w"], ((0, cfg.KP - cfg.KC), (0, 0)))
    gw2p = jnp.pad(fw["gate_w2"], ((0, cfg.GP - cfg.GR), (0, 0)))
    r_k = row(fw["r_k"])
    w_outb, w1b, w2b = fw["w_out"].astype(BF16), fw["mlp_w1"].astype(BF16), fw["mlp_w2"].astype(BF16)

    crows = jnp.concatenate([c, fw["c_ctx"], jnp.zeros((SUBLANES - B - 1, D), F32)], axis=0)
    s_rows, mods = _ada_fwd(crows, ada_wb, fw["ada_b"])
    mod_x = mods[:B].reshape(B, 6, D)
    mod_c = mods[B].reshape(6, D)
    modt = jnp.concatenate([jnp.broadcast_to(mod_c[None, None, 0:2], (B, JC, 2, D)),
                            jnp.broadcast_to(mod_x[:, None, 0:2], (B, cfg.JX, 2, D))], axis=1)
    mod2, mod345 = mod_x[:, 2:3], mod_x[:, 3:6]

    xcat = jnp.concatenate([ctx, x], axis=1)
    p, hb = _in_proj(xcat, modt, fw["mix_pre_g"], w_in_p, cfg)
    prep_w = (mup, mun, w0, w2p, a0, a2p, fw["k_k"], fw["k_a"], cw, fw["conv_b"], fw["conv_ln_w"],
              fw["conv_ln_b"], e_w)
    r, v, kk, w_f, kd_f, b_f, w_b, kd_b, b_b, gd, conv = _mix_prep(p, *prep_w, cfg)
    flat = lambda a: a.reshape(cfg.G, a.shape[2], LANES)
    heads = lambda a: a.reshape(B, cfg.HP, a.shape[1], LANES)
    ops_f = tuple(flat(a) for a in (r, w_f, kd_f, v, kk, b_f))
    ops_b = tuple(flat(a) for a in (r, w_b, kd_b, v, kk, b_b))
    y_f, ck_f = _scan_fwd(ops_f, e128, False, cfg)
    y_b, ck_b = _scan_fwd(ops_b, e128, True, cfg)
    out_args = (heads(y_f), heads(y_b), kd_f, kd_b, r, v, gd, conv, x, mod2, r_k, gw2p, fw["lnx_w"], fw["lnx_b"],
                w_outb, fw["mix_post_g"], e_w)
    x1 = _mix_out(*out_args, cfg)

    dx1, loss_t, h2b, dpreb, actb, dffb, dmod345, dg3, dg4 = _mlp_fwd_bwd(
        x1, tgt, mod345, fw["mlp_pre_g"], fw["mlp_post_g"], w1b, w2b, cfg)
    (dy, dkb, dr_c, dv_c, dgd, dconv, catb, dmixb, dmod2, dg2, drk, dgw, dlw, dlb) = _mix_out_bwd(
        *out_args, dx1, cfg)
    sf = _scan_bwd(ops_f, flat(dy), ck_f, e128, False, cfg)
    sb = _scan_bwd(ops_b, flat(dy), ck_b, e128, True, cfg)
    (dpz, dmup, dmun, dw0, dw2p, da0, da2p, dkk, dka, dcw, dcb, dclw, dclb) = _mix_prep_bwd(
        p, *prep_w, [heads(a) for a in sf], [heads(a) for a in sb], (dr_c, dv_c, dkb, dgd, dconv), cfg)
    grad_x, dpb, dmodt, dg1 = _in_proj_bwd(dpz, xcat, modt, fw["mix_pre_g"], w_in_p, mup, mun, dx1, cfg)

    tokens = lambda a: a.reshape(-1, a.shape[-1])
    d_w_in = _grad_matmul(tokens(hb), tokens(dpb), "grad_w_in")
    d_w_out = _grad_matmul(tokens(catb), tokens(dmixb), "grad_w_out")
    d_w1 = _grad_matmul(tokens(h2b), tokens(dpreb), "grad_mlp_w1")
    d_w2 = _grad_matmul(tokens(actb), tokens(dffb), "grad_mlp_w2")

    grads = {
        "mix_pre_g": dg1, "mix_post_g": dg2, "mlp_pre_g": dg3, "mlp_post_g": dg4,
        "w_in": _unpad_cols(d_w_in, cfg),
        "mu_prev": _unpad_cols(dmup, cfg, True), "mu_next": _unpad_cols(dmun, cfg, True),
        "decay_w0": dw0.reshape(2, W), "decay_w2": _unpair_weight(dw2p, cfg),
        "iclr_a0": da0.reshape(2, W), "iclr_a2": _unpair_weight(da2p, cfg),
        "k_k": dkk, "k_a": dka, "r_k": drk.reshape(fw["r_k"].shape),
        "gate_w2": dgw[:cfg.GR], "lnx_w": dlw, "lnx_b": dlb,
        "conv_w": dcw[:cfg.KC], "conv_b": dcb, "conv_ln_w": dclw, "conv_ln_b": dclb,
        "w_out": d_w_out, "mlp_w1": d_w1, "mlp_w2": d_w2,
    }
    dmod_x = jnp.concatenate([jnp.sum(dmodt[:, JC:], axis=1), dmod2, dmod345], axis=1).reshape(B, 6 * D)
    dmod_c = jnp.concatenate([jnp.sum(dmodt[:, :JC], axis=(0, 1)), jnp.zeros((4, D), F32)], axis=0).reshape(1, 6 * D)
    g_rows = jnp.concatenate([dmod_x, dmod_c, jnp.zeros((SUBLANES - B - 1, 6 * D), F32)], axis=0)
    return loss_t, grad_x, grads, s_rows, g_rows


def _my_index():
    return 4 * lax.axis_index("x") + 2 * lax.axis_index("y") + lax.axis_index("c")


def _exchange(arrays, scatter, name):
    n = len(arrays)
    out_shape = [jax.ShapeDtypeStruct(a.shape if s else (N_DEV,) + a.shape, a.dtype)
                 for a, s in zip(arrays, scatter)]

    def body(*refs):
        ins, outs = refs[:n], refs[n:2 * n]
        send_sems, recv_sems, local_sems = refs[2 * n:]
        x, y, c = lax.axis_index("x"), lax.axis_index("y"), lax.axis_index("c")
        me = 4 * x + 2 * y + c
        flip = lambda v, f: 1 - v if f else v

        def piece(a, dest):
            return ins[a].at[dest] if scatter[a] else ins[a]

        local = [pltpu.make_async_copy(piece(a, me), outs[a].at[me], local_sems.at[a]) for a in range(n)]
        for cp in local:
            cp.start()
        sends, recvs = [], []
        for k in range(1, N_DEV):
            fx, fy, fc = (k >> 2) & 1, (k >> 1) & 1, k & 1
            peer = (flip(x, fx), flip(y, fy), flip(c, fc))
            peer_idx = 4 * peer[0] + 2 * peer[1] + peer[2]
            for a in range(n):
                sends.append(pltpu.make_async_remote_copy(
                    src_ref=piece(a, peer_idx), dst_ref=outs[a].at[me],
                    send_sem=send_sems.at[k - 1, a], recv_sem=recv_sems.at[k - 1, a],
                    device_id=peer, device_id_type=pl.DeviceIdType.MESH))
                recvs.append(pltpu.make_async_remote_copy(
                    src_ref=piece(a, peer_idx), dst_ref=outs[a].at[peer_idx],
                    send_sem=send_sems.at[k - 1, a], recv_sem=recv_sems.at[k - 1, a],
                    device_id=peer, device_id_type=pl.DeviceIdType.MESH))
        for cp in sends:
            cp.start()
        for cp in recvs:
            cp.wait_recv()
        for cp in sends:
            cp.wait_send()
        for cp in local:
            cp.wait()

    hbm = pl.BlockSpec(memory_space=pltpu.HBM)
    return pl.pallas_call(
        body, name=name, out_shape=out_shape,
        in_specs=[hbm] * n, out_specs=[hbm] * n,
        scratch_shapes=[pltpu.SemaphoreType.DMA((N_DEV - 1, n)), pltpu.SemaphoreType.DMA((N_DEV - 1, n)),
                        pltpu.SemaphoreType.DMA((n,))],
    )(*arrays)


def _pack(parts):
    flat = jnp.concatenate([p.reshape(-1) for p in parts])
    total = _round_up(flat.shape[0], SUBLANES * LANES)
    return jnp.pad(flat, (0, total - flat.shape[0])).reshape(-1, LANES)


def _unpack(buf, shapes):
    flat = buf.reshape(-1)
    out, pos = [], 0
    for s in shapes:
        n = int(np.prod(s))
        out.append(flat[pos:pos + n].reshape(s))
        pos += n
    return out


_SHARDED_SMALL = ("decay_w0", "decay_w2", "iclr_a0", "iclr_a2", "gate_w2", "conv_w")
_REPLICATED = ("mix_pre_g", "mix_post_g", "mlp_pre_g", "mlp_post_g", "mu_prev", "mu_next", "k_k", "k_a", "r_k",
               "lnx_w", "lnx_b", "conv_b", "conv_ln_w", "conv_ln_b")
_ADA_SMALL = ("c_ctx", "ada_b")
_WEIGHTS = ("c_ctx", "ada_w", "ada_b", "mix_pre_g", "mix_post_g", "mlp_pre_g", "mlp_post_g", "w_in", "mu_prev",
            "mu_next", "decay_w0", "decay_w2", "iclr_a0", "iclr_a2", "k_k", "k_a", "r_k", "gate_w2", "lnx_w", "lnx_b",
            "conv_w", "conv_b", "conv_ln_w", "conv_ln_b", "w_out", "mlp_w1", "mlp_w2")
_INPUTS = ("x", "c", "ctx") + _WEIGHTS + ("loss_target",) + tuple("m_" + n for n in _WEIGHTS) + tuple(
    "v_" + n for n in _WEIGHTS)


def _cols_to_blocks(a):
    a = a.reshape(a.shape[:-1] + (N_DEV, a.shape[-1] // N_DEV))
    return jnp.moveaxis(a, -2, 0)


def _blocks_to_cols(a):
    a = jnp.moveaxis(a, 0, -2)
    return a.reshape(a.shape[:-2] + (a.shape[-2] * a.shape[-1],))


def kernel(x, c, ctx, c_ctx, ada_w, ada_b, mix_pre_g, mix_post_g, mlp_pre_g, mlp_post_g, w_in, mu_prev, mu_next, decay_w0, decay_w2, iclr_a0, iclr_a2, k_k, k_a, r_k, gate_w2, lnx_w, lnx_b, conv_w, conv_b, conv_ln_w, conv_ln_b, w_out, mlp_w1, mlp_w2, loss_target, m_c_ctx, m_ada_w, m_ada_b, m_mix_pre_g, m_mix_post_g, m_mlp_pre_g, m_mlp_post_g, m_w_in, m_mu_prev, m_mu_next, m_decay_w0, m_decay_w2, m_iclr_a0, m_iclr_a2, m_k_k, m_k_a, m_r_k, m_gate_w2, m_lnx_w, m_lnx_b, m_conv_w, m_conv_b, m_conv_ln_w, m_conv_ln_b, m_w_out, m_mlp_w1, m_mlp_w2, v_c_ctx, v_ada_w, v_ada_b, v_mix_pre_g, v_mix_post_g, v_mlp_pre_g, v_mlp_post_g, v_w_in, v_mu_prev, v_mu_next, v_decay_w0, v_decay_w2, v_iclr_a0, v_iclr_a2, v_k_k, v_k_a, v_r_k, v_gate_w2, v_lnx_w, v_lnx_b, v_conv_w, v_conv_b, v_conv_ln_w, v_conv_ln_b, v_w_out, v_mlp_w1, v_mlp_w2):
    given = dict(zip(_INPUTS, (x, c, ctx, c_ctx, ada_w, ada_b, mix_pre_g, mix_post_g, mlp_pre_g, mlp_post_g, w_in, mu_prev, mu_next, decay_w0, decay_w2, iclr_a0, iclr_a2, k_k, k_a, r_k, gate_w2, lnx_w, lnx_b, conv_w, conv_b, conv_ln_w, conv_ln_b, w_out, mlp_w1, mlp_w2, loss_target, m_c_ctx, m_ada_w, m_ada_b, m_mix_pre_g, m_mix_post_g, m_mlp_pre_g, m_mlp_post_g, m_w_in, m_mu_prev, m_mu_next, m_decay_w0, m_decay_w2, m_iclr_a0, m_iclr_a2, m_k_k, m_k_a, m_r_k, m_gate_w2, m_lnx_w, m_lnx_b, m_conv_w, m_conv_b, m_conv_ln_w, m_conv_ln_b, m_w_out, m_mlp_w1, m_mlp_w2, v_c_ctx, v_ada_w, v_ada_b, v_mix_pre_g, v_mix_post_g, v_mlp_pre_g, v_mlp_post_g, v_w_in, v_mu_prev, v_mu_next, v_decay_w0, v_decay_w2, v_iclr_a0, v_iclr_a2, v_k_k, v_k_a, v_r_k, v_gate_w2, v_lnx_w, v_lnx_b, v_conv_w, v_conv_b, v_conv_ln_w, v_conv_ln_b, v_w_out, v_mlp_w1, v_mlp_w2)))
    loc = {}
    for pre in ("", "m_", "v_"):
        for n in _WEIGHTS:
            a = given[pre + n]
            a = a.reshape(1, -1) if n == "c_ctx" else a[0]
            loc[pre + n] = a.reshape(1, -1) if a.ndim == 1 else a
    B, TX, D = x.shape
    W, CW = loc["k_k"].shape[1], loc["conv_b"].shape[1]
    cfg = _Cfg(B, TX, ctx.shape[1], D, W, CW, loc["decay_w2"].shape[1], loc["gate_w2"].shape[0],
               loc["conv_w"].shape[0], loc["mlp_w1"].shape[1] * N_DEV)
    me = _my_index()

    small_shapes = [loc[n].shape for n in _SHARDED_SMALL]
    got = _exchange(
        [loc["ada_w"].astype(BF16), loc["w_in"].astype(BF16), loc["w_out"].astype(BF16),
         loc["mlp_w1"].astype(BF16), loc["mlp_w2"].astype(BF16), _pack([loc[n] for n in _SHARDED_SMALL])],
        [False] * 6, "gather_weights")
    fw = {n: loc[n] for n in _REPLICATED + _ADA_SMALL}
    fw["ada_w"] = _blocks_to_cols(got[0])
    fw["w_in"] = _blocks_to_cols(got[1])
    fw["w_out"] = got[2].reshape(-1, D)
    fw["mlp_w1"] = _blocks_to_cols(got[3])
    fw["mlp_w2"] = got[4].reshape(-1, D)
    per_dev = [_unpack(got[5][i], small_shapes) for i in range(N_DEV)]
    for j, n in enumerate(_SHARDED_SMALL):
        fw[n] = jnp.concatenate([per_dev[i][j] for i in range(N_DEV)], axis=-1)

    loss_t, grad_x, grads, s_rows, g_rows = _local_step(cfg, x, c, ctx, loss_target, fw)
    loss = lax.psum(jnp.sum(loss_t[:, :, 0, 0]), ("x", "y", "c"))

    small_blocks = jnp.stack([_pack([_cols_to_blocks(grads[n])[i] for n in _SHARDED_SMALL]) for i in range(N_DEV)])
    sent = _exchange(
        [_cols_to_blocks(grads["w_in"]), grads["w_out"].reshape(N_DEV, -1, D), _cols_to_blocks(grads["mlp_w1"]),
         grads["mlp_w2"].reshape(N_DEV, -1, D), small_blocks,
         _pack([grads[n] for n in _REPLICATED]), s_rows, g_rows],
        [True] * 5 + [False] * 3, "exchange_grads")
    s_all = sent[6].reshape(N_DEV * SUBLANES, D)
    g_all = sent[7].reshape(N_DEV * SUBLANES, 6 * D)
    ns = 6 * D // N_DEV
    g_mine = lax.dynamic_slice_in_dim(g_all, me * ns, ns, axis=1)
    d_ada_w, d_ada_b, d_c_ctx = _ada_bwd(s_all, g_all, g_mine, loc["c_ctx"], fw["ada_w"], B)

    res = {}

    def update(name, parts):
        res[name] = _adamw(parts, loc[name], loc["m_" + name], loc["v_" + name], "adamw_" + name)

    update("w_in", sent[0])
    update("w_out", sent[1])
    update("mlp_w1", sent[2])
    update("mlp_w2", sent[3])
    update("ada_w", d_ada_w[None])

    def update_packed(names, parts, tag):
        shapes = [loc[n].shape for n in names]
        packed = _adamw(parts, *[_pack([loc[pre + n] for n in names]) for pre in ("", "m_", "v_")], "adamw_" + tag)
        unpacked = [_unpack(p, shapes) for p in packed]
        for j, n in enumerate(names):
            res[n] = tuple(u[j] for u in unpacked)

    update_packed(_SHARDED_SMALL, sent[4], "sharded_small")
    update_packed(_REPLICATED, sent[5], "replicated")
    update_packed(_ADA_SMALL, _pack([d_c_ctx, d_ada_b])[None], "ada_small")

    outs = [loss, grad_x]
    for k in range(4):
        for n in _WEIGHTS:
            outs.append(res[n][k].reshape(given[n].shape))
    return tuple(outs)
```

```python
import functools

import numpy as np
import jax
import jax.numpy as jnp
from jax import lax
from jax.experimental import pallas as pl
from jax.experimental.pallas import tpu as pltpu

F32 = jnp.float32
BF16 = jnp.bfloat16

EPS_RMS = 1e-6
EPS_LN = 1e-5
EPS_GN = 64e-5
LINE = 64
HEAD = 64
LANES = 128
SUBLANES = 8
SCAN_CHUNK = 16
N_DEV = 8
VMEM_LIMIT = 56 * 1024 * 1024

ADAM_LR = 0.001
ADAM_B1 = 0.9
ADAM_B2 = 0.999
ADAM_EPS = 1e-08
ADAM_WD = 0.01
ADAM_STEP = 10


def _round_up(n, m):
    return (n + m - 1) // m * m


def _params(semantics=None, vmem=VMEM_LIMIT):
    return pltpu.CompilerParams(dimension_semantics=semantics, vmem_limit_bytes=vmem)


def _bdot(a, b):
    return jnp.dot(a.astype(BF16), b.astype(BF16), preferred_element_type=F32)


def _bdot_nt(a, b):
    return lax.dot_general(a.astype(BF16), b.astype(BF16), (((1,), (1,)), ((), ())),
                           preferred_element_type=F32)


def _bdot_tn(a, b):
    return lax.dot_general(a.astype(BF16), b.astype(BF16), (((0,), (0,)), ((), ())),
                           preferred_element_type=F32)


@jax.custom_vjp
def _mm(a, b):
    return _bdot(a, b)


def _mm_fwd(a, b):
    return _bdot(a, b), (a, b)


def _mm_bwd(res, g):
    a, b = res
    return _bdot_nt(g, b), _bdot_tn(a, g)


_mm.defvjp(_mm_fwd, _mm_bwd)


def _seg_sum_raw(x, e):
    hi = x.astype(BF16)
    lo = (x - hi.astype(F32)).astype(BF16)
    return (jnp.dot(hi, e, preferred_element_type=F32)
            + jnp.dot(lo, e, preferred_element_type=F32))


@jax.custom_vjp
def _seg_sum(x, e):
    return _seg_sum_raw(x, e)


def _seg_sum_fwd(x, e):
    return _seg_sum_raw(x, e), e


def _seg_sum_bwd(e, g):
    return _seg_sum_raw(g, e), None


_seg_sum.defvjp(_seg_sum_fwd, _seg_sum_bwd)


def _block_ones(n, seg=HEAD):
    i = np.arange(n) // seg
    return jnp.asarray((i[:, None] == i[None, :]).astype(np.float32), dtype=BF16)


def _rms(xv, g):
    ms = jnp.mean(xv * xv, axis=-1, keepdims=True)
    return xv * lax.rsqrt(ms + EPS_RMS) * g


def _rms_mod(xv, g, shift, scale):
    return _rms(xv, g) * (1.0 + scale) + shift


def _sigmoid(z):
    return 1.0 / (1.0 + jnp.exp(-z))


def _silu(z):
    return z * _sigmoid(z)


def _softplus(z):
    return jnp.maximum(z, 0.0) + jnp.log(1.0 + jnp.exp(-jnp.abs(z)))


class _Cfg:
    def __init__(self, B, TX, TC, D, W, CW, R, GR, KC, F):
        self.B, self.TX, self.TC, self.D = B, TX, TC, D
        self.W, self.CW, self.R, self.GR, self.KC, self.F = W, CW, R, GR, KC, F
        self.T = TX + TC
        self.TT = min(256, TC)
        assert TC % self.TT == 0 and TX % self.TT == 0 and self.TT % LINE == 0
        self.JC = TC // self.TT
        self.JX = TX // self.TT
        self.J = self.JC + self.JX
        self.HP = W // LANES
        self.G = B * self.HP
        self.PW = _round_up(2 * R, LANES)
        self.GP = _round_up(GR, LANES)
        self.SP = 3 * W + 2 * self.PW + self.GP
        self.CP = self.SP + 2 * CW
        self.KP = _round_up(KC, SUBLANES)
        assert self.T % SCAN_CHUNK == 0 and TC % SCAN_CHUNK == 0
        self.NCH = self.T // SCAN_CHUNK
        self.NCC = TC // SCAN_CHUNK
        W_, R_ = W, R
        segs = [(0, 3 * W_, 0),
                (3 * W_, 2 * R_, 3 * W_),
                (3 * W_ + 2 * R_, 2 * R_, 3 * W_ + self.PW),
                (3 * W_ + 4 * R_, GR, 3 * W_ + 2 * self.PW),
                (3 * W_ + 4 * R_ + GR, 2 * CW, self.SP)]
        self.col_segs = segs
        self.shift_cols = 3 * W_ + 4 * R_ + GR
        self.in_cols = self.shift_cols + 2 * CW


def _pad_cols(a, cfg, upto_shift=False):
    width = cfg.SP if upto_shift else cfg.CP
    pieces, pos = [], 0
    for src, n, dst in cfg.col_segs:
        if upto_shift and dst >= cfg.SP:
            break
        if dst > pos:
            pieces.append(jnp.zeros(a.shape[:-1] + (dst - pos,), a.dtype))
        pieces.append(a[..., src:src + n])
        pos = dst + n
    if width > pos:
        pieces.append(jnp.zeros(a.shape[:-1] + (width - pos,), a.dtype))
    return jnp.concatenate(pieces, axis=-1)


def _unpad_cols(a, cfg, upto_shift=False):
    pieces = []
    for src, n, dst in cfg.col_segs:
        if upto_shift and dst >= cfg.SP:
            break
        pieces.append(a[..., dst:dst + n])
    return jnp.concatenate(pieces, axis=-1)


def _pair_weight(w2, cfg):
    R, W = cfg.R, cfg.W
    out = jnp.zeros((cfg.PW, 2 * W), w2.dtype)
    out = out.at[0:R, 0:W].set(w2[0])
    out = out.at[R:2 * R, W:2 * W].set(w2[1])
    return out


def _unpair_weight(g, cfg):
    R, W = cfg.R, cfg.W
    return jnp.stack([g[0:R, 0:W], g[R:2 * R, W:2 * W]])


def _row_ids(n):
    return lax.broadcasted_iota(jnp.int32, (n, 1), 0)


def _shift_rows(z, prev_row, next_row):
    n = z.shape[0]
    rows = _row_ids(n)
    zp = jnp.where(rows == 0, prev_row, pltpu.roll(z, 1, 0))
    zn = jnp.where(rows == n - 1, next_row, pltpu.roll(z, n - 1, 0))
    return zp, zn


def _line_shift(u, d):
    if d == 0:
        return u
    n = u.shape[0]
    lt = _row_ids(n) % LINE
    ok = jnp.logical_and(lt + d >= 0, lt + d < LINE)
    return jnp.where(ok, pltpu.roll(u, (-d) % n, 0), 0.0)


def _dwconv(u, cw, kc):
    pad = kc // 2
    acc = jnp.zeros_like(u)
    for i in range(kc):
        acc = acc + _line_shift(u, i - pad) * cw[i:i + 1, :]
    return acc


def _rwkv_prep(rw, w0, w2p, a0, a2p, k_k, k_a, e, cfg):
    W, PW = cfg.W, cfg.PW
    r = rw[:, 0:W]
    k = rw[:, W:2 * W]
    v = rw[:, 2 * W:3 * W]
    wdp = rw[:, 3 * W:3 * W + PW]
    adp = rw[:, 3 * W + PW:3 * W + 2 * PW]
    wl = w0 + _mm(jnp.tanh(wdp), w2p)
    w_log = -_softplus(-wl) - 0.5
    decay = jnp.exp(-jnp.exp(w_log))
    iclr = _sigmoid(a0 + _mm(adp, a2p))
    kkr = k * k_k
    nrm = jnp.sqrt(_seg_sum(kkr * kkr, e))
    kk = kkr / jnp.maximum(nrm, 1e-12)
    outs = [r, v, kk]
    for d in range(2):
        ic = iclr[:, d * W:(d + 1) * W]
        outs += [decay[:, d * W:(d + 1) * W], k * (1.0 + (ic - 1.0) * k_a), kk * ic]
    return tuple(outs)


def _glu(cv, cfg):
    return cv[:, :cfg.CW] * _sigmoid(cv[:, cfg.CW:])


def _conv_post(y, cb, lw, lb):
    yf = y + cb
    mu = jnp.mean(yf, axis=-1, keepdims=True)
    var = jnp.mean(jnp.square(yf - mu), axis=-1, keepdims=True)
    return _silu((yf - mu) * lax.rsqrt(var + EPS_LN) * lw + lb)


def _readout(y, kbar, r, v, gd, r_k, gw2, lnx_w, lnx_b, e):
    inv = 1.0 / HEAD
    mu = _seg_sum(y, e) * inv
    yc = y - mu
    var = _seg_sum(yc * yc, e) * inv
    yn = yc * lax.rsqrt(var + EPS_GN) * lnx_w + lnx_b
    bonus = _seg_sum(r * kbar * r_k, e) * v
    g = _mm(_sigmoid(gd), gw2)
    return (yn + bonus) * g


def _post_res(xv, mix, gate, g):
    return xv + gate * _rms(mix, g)


def _head_spec(cfg, tmap):
    return pl.BlockSpec((1, cfg.HP, cfg.TT, LANES), lambda b, j: (b, 0, tmap(j), 0))


def _full_spec(shape):
    n = len(shape)
    return pl.BlockSpec(shape, lambda *_: (0,) * n)


def _to_heads(ref, val, cfg):
    for hp in range(cfg.HP):
        ref[0, hp] = val[:, hp * LANES:(hp + 1) * LANES]


def _from_heads(ref, cfg):
    return jnp.concatenate([ref[0, hp] for hp in range(cfg.HP)], axis=-1)


def _in_proj(xcat, modt, g1, w_in_p, cfg):
    B, T, D, TT, CP = cfg.B, cfg.T, cfg.D, cfg.TT, cfg.CP

    def body(x_ref, mod_ref, g_ref, w_ref, p_ref, h_ref):
        h = _rms_mod(x_ref[0], g_ref[...], mod_ref[0, 0, 0:1, :], mod_ref[0, 0, 1:2, :])
        hb = h.astype(BF16)
        h_ref[0] = hb
        p_ref[0] = jnp.dot(hb, w_ref[...], preferred_element_type=F32)

    return pl.pallas_call(
        body, name="in_proj", grid=(B, cfg.J),
        in_specs=[pl.BlockSpec((1, TT, D), lambda b, j: (b, j, 0)),
                  pl.BlockSpec((1, 1, 2, D), lambda b, j: (b, j, 0, 0)),
                  _full_spec((1, D)), _full_spec((D, CP))],
        out_specs=[pl.BlockSpec((1, TT, CP), lambda b, j: (b, j, 0)),
                   pl.BlockSpec((1, TT, D), lambda b, j: (b, j, 0))],
        out_shape=[jax.ShapeDtypeStruct((B, T, CP), F32), jax.ShapeDtypeStruct((B, T, D), BF16)],
        compiler_params=_params(("parallel", "parallel")),
    )(xcat, modt, g1, w_in_p)


def _halo_specs(cfg, width):
    per = cfg.TT // SUBLANES
    last = cfg.T // SUBLANES - 1
    prev = pl.BlockSpec((1, SUBLANES, width), lambda b, j: (b, jnp.maximum(j * per - 1, 0), 0))
    nxt = pl.BlockSpec((1, SUBLANES, width), lambda b, j: (b, jnp.minimum((j + 1) * per, last), 0))
    return prev, nxt


def _halo_flags(j, cfg):
    has_prev = jnp.logical_and(j != 0, j != cfg.JC).astype(F32)
    has_next = jnp.logical_and(j != cfg.JC - 1, j != cfg.J - 1).astype(F32)
    return has_prev, has_next


def _shifted(p_ref, prev_ref, next_ref, mup, mun, j, cfg):
    SP = cfg.SP
    has_prev, has_next = _halo_flags(j, cfg)
    z = p_ref[0][:, :SP]
    zp, zn = _shift_rows(z, prev_ref[0, SUBLANES - 1:SUBLANES, :] * has_prev, next_ref[0, 0:1, :] * has_next)
    return z, zp, zn, z + mup * (zp - z) + mun * (zn - z)


def _mix_prep(p, mup, mun, w0, w2p, a0, a2p, k_k, k_a, cw, cb, clw, clb, e_w, cfg):
    B, T, TT, SP, CP, W, CW, HP = cfg.B, cfg.T, cfg.TT, cfg.SP, cfg.CP, cfg.W, cfg.CW, cfg.HP

    def body(p_ref, prev_ref, next_ref, mup_ref, mun_ref, w0_ref, w2_ref, a0_ref, a2_ref, kk_ref, ka_ref,
             cw_ref, cb_ref, clw_ref, clb_ref, e_ref, *outs):
        j = pl.program_id(1)
        _, _, _, rw = _shifted(p_ref, prev_ref, next_ref, mup_ref[...], mun_ref[...], j, cfg)
        vals = _rwkv_prep(rw, w0_ref[...], w2_ref[...], a0_ref[...], a2_ref[...], kk_ref[...], ka_ref[...],
                          e_ref[...], cfg)
        for ref, val in zip(outs[:9], vals):
            _to_heads(ref, val, cfg)
        outs[9][0] = rw[:, 3 * W + 2 * cfg.PW:SP]
        u = _glu(p_ref[0][:, SP:], cfg)
        outs[10][0] = _conv_post(_dwconv(u, cw_ref[...], cfg.KC), cb_ref[...], clw_ref[...], clb_ref[...])

    prev, nxt = _halo_specs(cfg, SP)
    head = jax.ShapeDtypeStruct((B, HP, T, LANES), F32)
    return pl.pallas_call(
        body, name="mix_prep", grid=(B, cfg.J),
        in_specs=[pl.BlockSpec((1, TT, CP), lambda b, j: (b, j, 0)), prev, nxt,
                  _full_spec((1, SP)), _full_spec((1, SP)),
                  _full_spec((1, 2 * W)), _full_spec((cfg.PW, 2 * W)),
                  _full_spec((1, 2 * W)), _full_spec((cfg.PW, 2 * W)),
                  _full_spec((1, W)), _full_spec((1, W)),
                  _full_spec((cfg.KP, CW)), _full_spec((1, CW)), _full_spec((1, CW)), _full_spec((1, CW)),
                  _full_spec((W, W))],
        out_specs=[_head_spec(cfg, lambda j: j)] * 9
                  + [pl.BlockSpec((1, TT, cfg.GP), lambda b, j: (b, j, 0)),
                     pl.BlockSpec((1, TT, CW), lambda b, j: (b, j, 0))],
        out_shape=[head] * 9 + [jax.ShapeDtypeStruct((B, T, cfg.GP), F32),
                                jax.ShapeDtypeStruct((B, T, CW), F32)],
        compiler_params=_params(("parallel", "parallel")),
    )(p, p, p, mup, mun, w0, w2p, a0, a2p, k_k, k_a, cw, cb, clw, clb, e_w)


def _chunk_pos(c, reverse, cfg):
    if not reverse:
        return c
    return jnp.where(c < cfg.NCC, cfg.NCC - 1 - c, cfg.NCH - 1 + cfg.NCC - c)


def _diag_mask():
    r = lax.broadcasted_iota(jnp.int32, (HEAD, LANES), 0)
    l = lax.broadcasted_iota(jnp.int32, (HEAD, LANES), 1)
    return (r == l % HEAD).astype(F32)


def _col_lhs(row, diag_b):
    hi = row.astype(BF16)
    lo = (row - hi.astype(F32)).astype(BF16)
    return diag_b * hi, diag_b * lo


def _both_rows(ins, idx, i):
    return jnp.concatenate([ins[d][idx][:, pl.ds(_tok(i, d == 1), 1), :] for d in range(2)], axis=0)


def _seg_dot(blocks, e):
    n, g = len(blocks), blocks[0].shape[0]
    lhs = jnp.concatenate(blocks, axis=0).reshape(n * g * HEAD, LANES)
    return jnp.dot(lhs, e, preferred_element_type=F32).reshape(n, g, HEAD, LANES)


def _tok(i, reverse):
    return (SCAN_CHUNK - 1 - i) if reverse else i


def _scan_fwd(ops_f, ops_b, e128, cfg):
    G, T, NCH = cfg.G, cfg.T, cfg.NCH
    CH = SCAN_CHUNK

    G2 = 2 * G

    def body(*refs):
        ins = (refs[0:6], refs[6:12])
        e_ref = refs[12]
        ys, cks = (refs[13], refs[15]), (refs[14], refs[16])
        s_ref, mm_ref = refs[17], refs[18]
        c = pl.program_id(0)

        @pl.when(c == 0)
        def _():
            s_ref[...] = jnp.zeros_like(s_ref)

        diag = _diag_mask()
        diag_b = diag.astype(BF16)
        e = e_ref[...]

        rows = functools.partial(_both_rows, ins)

        s0 = s_ref[...]
        cks[0][0] = s0[:G]
        cks[1][0] = s0[G:]
        res = _seg_dot([(s0 * (-rows(4, 0))).astype(BF16), *_col_lhs(rows(3, 0), diag_b)], e)
        mm_ref[0] = res[0]
        mm_ref[1] = res[1] + res[2]

        def step(i, carry):
            nxt = jnp.minimum(i + 1, CH - 1)
            S = s_ref[...] * rows(1, i) + mm_ref[0] * rows(5, i) + mm_ref[1] * rows(2, i)
            s_ref[...] = S
            res = _seg_dot([(S * rows(0, i)).astype(BF16), (S * (-rows(4, nxt))).astype(BF16),
                            *_col_lhs(rows(3, nxt), diag_b)], e)
            mm_ref[0] = res[1]
            mm_ref[1] = res[2] + res[3]
            y = jnp.sum(diag * res[0], axis=1, keepdims=True)
            ys[0][:, pl.ds(_tok(i, False), 1), :] = y[:G]
            ys[1][:, pl.ds(_tok(i, True), 1), :] = y[G:]
            return carry

        lax.fori_loop(0, CH, step, 0)

    toks = [pl.BlockSpec((G, CH, LANES), lambda c, rev=rev: (0, _chunk_pos(c, rev, cfg), 0)) for rev in (False, True)]
    ck_spec = pl.BlockSpec((1, G, HEAD, LANES), lambda c: (c, 0, 0, 0))
    y_shape = jax.ShapeDtypeStruct((G, T, LANES), F32)
    ck_shape = jax.ShapeDtypeStruct((NCH, G, HEAD, LANES), F32)
    return pl.pallas_call(
        body, name="scan_fwd", grid=(NCH,),
        in_specs=[toks[0]] * 6 + [toks[1]] * 6 + [_full_spec((LANES, LANES))],
        out_specs=[toks[0], ck_spec, toks[1], ck_spec],
        out_shape=[y_shape, ck_shape, y_shape, ck_shape],
        scratch_shapes=[pltpu.VMEM((G2, HEAD, LANES), F32), pltpu.VMEM((2, G2, HEAD, LANES), F32)],
        compiler_params=_params(("arbitrary",)),
    )(*ops_f, *ops_b, e128)


def _mix_out(yf, yb, kdf, kdb, r, v, gd, conv, x, mod2, r_k, gw2p, lnx_w, lnx_b, w_out, g2, e_w, cfg):
    B, TX, D, TT, W, CW, JC = cfg.B, cfg.TX, cfg.D, cfg.TT, cfg.W, cfg.CW, cfg.JC

    def body(yf_ref, yb_ref, kdf_ref, kdb_ref, r_ref, v_ref, gd_ref, cv_ref, x_ref, mod_ref,
             rk_ref, gw_ref, lw_ref, lb_ref, wo_ref, g_ref, e_ref, x1_ref):
        y = _from_heads(yf_ref, cfg) + _from_heads(yb_ref, cfg)
        kbar = 0.5 * (_from_heads(kdf_ref, cfg) + _from_heads(kdb_ref, cfg))
        ro = _readout(y, kbar, _from_heads(r_ref, cfg), _from_heads(v_ref, cfg), gd_ref[0], rk_ref[...],
                      gw_ref[...], lw_ref[...], lb_ref[...], e_ref[...])
        cat = jnp.concatenate([ro, cv_ref[0]], axis=-1)
        mix = _bdot(cat, wo_ref[...])
        x1_ref[0] = _post_res(x_ref[0], mix, mod_ref[0], g_ref[...])

    hs = _head_spec(cfg, lambda j: j + JC)
    lat = lambda n: pl.BlockSpec((1, TT, n), lambda b, j: (b, j + JC, 0))
    return pl.pallas_call(
        body, name="mix_out", grid=(B, cfg.JX),
        in_specs=[hs] * 6 + [lat(cfg.GP), lat(CW),
                             pl.BlockSpec((1, TT, D), lambda b, j: (b, j, 0)),
                             pl.BlockSpec((1, 1, D), lambda b, j: (b, 0, 0)),
                             _full_spec((1, W)), _full_spec((cfg.GP, W)), _full_spec((1, W)), _full_spec((1, W)),
                             _full_spec((W + CW, D)), _full_spec((1, D)), _full_spec((W, W))],
        out_specs=pl.BlockSpec((1, TT, D), lambda b, j: (b, j, 0)),
        out_shape=jax.ShapeDtypeStruct((B, TX, D), F32),
        compiler_params=_params(("parallel", "parallel")),
    )(yf, yb, kdf, kdb, r, v, gd, conv, x, mod2, r_k, gw2p, lnx_w, lnx_b, w_out, g2, e_w)


def _acc(ref, val, first):
    @pl.when(first)
    def _():
        ref[...] = val

    @pl.when(jnp.logical_not(first))
    def _():
        ref[...] += val


def _mlp_fwd_bwd(x1, tgt, mod345, g3, g4, w1, w2, cfg):
    B, TX, D, TT, F, JX = cfg.B, cfg.TX, cfg.D, cfg.TT, cfg.F, cfg.JX

    def body(x1_ref, t_ref, mod_ref, g3_ref, g4_ref, w1_ref, w2_ref,
             dx1_ref, loss_ref, h2_ref, dpre_ref, act_ref, dff_ref, dmod_ref, dg3_ref, dg4_ref):
        b, j = pl.program_id(0), pl.program_id(1)
        x1v = x1_ref[0]
        sh, sc, gt = mod_ref[0, 0:1, :], mod_ref[0, 1:2, :], mod_ref[0, 2:3, :]
        h2, vjp_pre = jax.vjp(_rms_mod, x1v, g3_ref[...], sh, sc)
        h2b = h2.astype(BF16)
        pre = jnp.dot(h2b, w1_ref[...], preferred_element_type=F32)
        rl = jnp.maximum(pre, 0.0)
        actb = (rl * rl).astype(BF16)
        ff = jnp.dot(actb, w2_ref[...], preferred_element_type=F32)
        x2, vjp_post = jax.vjp(_post_res, x1v, ff, gt, g4_ref[...])
        err = x2 - t_ref[0]
        loss = 0.5 * jnp.sum(jnp.mean(err * err, axis=-1, keepdims=True))
        dx1a, dff, dgt, dg4 = vjp_post(err * (1.0 / D))
        dffb = dff.astype(BF16)
        dpre = _bdot_nt(dffb, w2_ref[...]) * (2.0 * rl)
        dpreb = dpre.astype(BF16)
        dx1b, dg3, dsh, dsc = vjp_pre(_bdot_nt(dpreb, w1_ref[...]))
        dx1_ref[0] = dx1a + dx1b
        loss_ref[0, 0] = jnp.zeros((SUBLANES, LANES), F32) + loss
        h2_ref[0] = h2b
        dpre_ref[0] = dpreb
        act_ref[0] = actb
        dff_ref[0] = dffb
        _acc(dmod_ref, jnp.concatenate([dsh, dsc, dgt], axis=0)[None], j == 0)
        first = jnp.logical_and(b == 0, j == 0)
        _acc(dg3_ref, dg3, first)
        _acc(dg4_ref, dg4, first)

    tile = lambda n: pl.BlockSpec((1, TT, n), lambda b, j: (b, j, 0))
    return pl.pallas_call(
        body, name="mlp_fwd_bwd", grid=(B, JX),
        in_specs=[tile(D), tile(D), pl.BlockSpec((1, 3, D), lambda b, j: (b, 0, 0)),
                  _full_spec((1, D)), _full_spec((1, D)),
                  pl.BlockSpec((D, F), lambda b, j: (0, 0), pipeline_mode=pl.Buffered(1)),
                  pl.BlockSpec((F, D), lambda b, j: (0, 0), pipeline_mode=pl.Buffered(1))],
        out_specs=[tile(D), pl.BlockSpec((1, 1, SUBLANES, LANES), lambda b, j: (b, j, 0, 0)),
                   tile(D), tile(F), tile(F), tile(D),
                   pl.BlockSpec((1, 3, D), lambda b, j: (b, 0, 0)),
                   _full_spec((1, D)), _full_spec((1, D))],
        out_shape=[jax.ShapeDtypeStruct((B, TX, D), F32),
                   jax.ShapeDtypeStruct((B, JX, SUBLANES, LANES), F32),
                   jax.ShapeDtypeStruct((B, TX, D), BF16), jax.ShapeDtypeStruct((B, TX, F), BF16),
                   jax.ShapeDtypeStruct((B, TX, F), BF16), jax.ShapeDtypeStruct((B, TX, D), BF16),
                   jax.ShapeDtypeStruct((B, 3, D), F32),
                   jax.ShapeDtypeStruct((1, D), F32), jax.ShapeDtypeStruct((1, D), F32)],
        compiler_params=_params(("arbitrary", "arbitrary")),
    )(x1, tgt, mod345, g3, g4, w1, w2)


def _mix_out_bwd(yf, yb, kdf, kdb, r, v, gd, conv, x, mod2, r_k, gw2p, lnx_w, lnx_b, w_out, g2, e_w, dx1, cfg):
    B, TX, D, TT, W, CW, JC, HP, GP = cfg.B, cfg.TX, cfg.D, cfg.TT, cfg.W, cfg.CW, cfg.JC, cfg.HP, cfg.GP

    def body(yf_ref, yb_ref, kdf_ref, kdb_ref, r_ref, v_ref, gd_ref, cv_ref, x_ref, mod_ref,
             rk_ref, gw_ref, lw_ref, lb_ref, wo_ref, g_ref, e_ref, dx1_ref,
             dy_ref, dkb_ref, dr_ref, dv_ref, dgd_ref, dcv_ref, cat_ref, dmix_ref,
             dmod_ref, dg2_ref, drk_ref, dgw_ref, dlw_ref, dlb_ref):
        b, j = pl.program_id(0), pl.program_id(1)
        e = e_ref[...]
        y = _from_heads(yf_ref, cfg) + _from_heads(yb_ref, cfg)
        kbar = 0.5 * (_from_heads(kdf_ref, cfg) + _from_heads(kdb_ref, cfg))
        ro, vjp_ro = jax.vjp(lambda *a: _readout(*a, e), y, kbar, _from_heads(r_ref, cfg),
                             _from_heads(v_ref, cfg), gd_ref[0], rk_ref[...], gw_ref[...], lw_ref[...], lb_ref[...])
        catb = jnp.concatenate([ro, cv_ref[0]], axis=-1).astype(BF16)
        mix = jnp.dot(catb, wo_ref[...], preferred_element_type=F32)
        _, vjp_post = jax.vjp(_post_res, x_ref[0], mix, mod_ref[0], g_ref[...])
        _, dmix, dgate, dg2 = vjp_post(dx1_ref[0])
        dmixb = dmix.astype(BF16)
        dcat = _bdot_nt(dmixb, wo_ref[...])
        dy, dkb, dr, dv, dgd, drk, dgw, dlw, dlb = vjp_ro(dcat[:, :W])
        _to_heads(dy_ref, dy, cfg)
        _to_heads(dkb_ref, dkb, cfg)
        _to_heads(dr_ref, dr, cfg)
        _to_heads(dv_ref, dv, cfg)
        dgd_ref[0] = dgd
        dcv_ref[0] = dcat[:, W:]
        cat_ref[0] = catb
        dmix_ref[0] = dmixb
        _acc(dmod_ref, dgate[None], j == 0)
        first = jnp.logical_and(b == 0, j == 0)
        _acc(dg2_ref, dg2, first)
        _acc(drk_ref, drk, first)
        _acc(dgw_ref, dgw, first)
        _acc(dlw_ref, dlw, first)
        _acc(dlb_ref, dlb, first)

    hs = _head_spec(cfg, lambda j: j + JC)
    ho = _head_spec(cfg, lambda j: j)
    lat = lambda n: pl.BlockSpec((1, TT, n), lambda b, j: (b, j + JC, 0))
    tile = lambda n: pl.BlockSpec((1, TT, n), lambda b, j: (b, j, 0))
    head = jax.ShapeDtypeStruct((B, HP, TX, LANES), F32)
    vec = lambda n: jax.ShapeDtypeStruct((1, n), F32)
    return pl.pallas_call(
        body, name="mix_out_bwd", grid=(B, cfg.JX),
        in_specs=[hs] * 6 + [lat(GP), lat(CW), tile(D),
                             pl.BlockSpec((1, 1, D), lambda b, j: (b, 0, 0)),
                             _full_spec((1, W)), _full_spec((GP, W)), _full_spec((1, W)), _full_spec((1, W)),
                             _full_spec((W + CW, D)), _full_spec((1, D)), _full_spec((W, W)), tile(D)],
        out_specs=[ho] * 4 + [tile(GP), tile(CW), tile(W + CW), tile(D),
                              pl.BlockSpec((1, 1, D), lambda b, j: (b, 0, 0)),
                              _full_spec((1, D)), _full_spec((1, W)), _full_spec((GP, W)),
                              _full_spec((1, W)), _full_spec((1, W))],
        out_shape=[head] * 4 + [jax.ShapeDtypeStruct((B, TX, GP), F32), jax.ShapeDtypeStruct((B, TX, CW), F32),
                                jax.ShapeDtypeStruct((B, TX, W + CW), BF16), jax.ShapeDtypeStruct((B, TX, D), BF16),
                                jax.ShapeDtypeStruct((B, 1, D), F32),
                                vec(D), vec(W), jax.ShapeDtypeStruct((GP, W), F32), vec(W), vec(W)],
        compiler_params=_params(("arbitrary", "arbitrary")),
    )(yf, yb, kdf, kdb, r, v, gd, conv, x, mod2, r_k, gw2p, lnx_w, lnx_b, w_out, g2, e_w, dx1)


def _scan_bwd(ops_f, ops_b, dy, ck_f, ck_b, e128, cfg):
    G, T, NCH, NCC = cfg.G, cfg.T, cfg.NCH, cfg.NCC
    CH = SCAN_CHUNK

    def body(*refs):
        ins = (refs[0:6], refs[6:12])
        dys, cks, e_ref = (refs[12], refs[13]), (refs[14], refs[15]), refs[16]
        outs = (refs[17:23], refs[23:29])
        hist_ref, sah_ref, ds_ref, mm_ref = refs[29:33]
        gi = pl.program_id(0)

        @pl.when(gi == 0)
        def _():
            ds_ref[...] = jnp.zeros_like(ds_ref)

        diag = _diag_mask()
        diag_b = diag.astype(BF16)
        e = e_ref[...]
        rows = functools.partial(_both_rows, ins)
        latent = [(_chunk_pos(NCH - 1 - gi, d == 1, cfg) >= NCC).astype(F32) for d in range(2)]

        def dy_rows(i):
            return jnp.concatenate([dys[d][:, pl.ds(_tok(i, d == 1), 1), :] * latent[d] for d in range(2)], axis=0)

        def put(idx, i, val):
            outs[0][idx][:, pl.ds(_tok(i, False), 1), :] = val[:G]
            outs[1][idx][:, pl.ds(_tok(i, True), 1), :] = val[G:]

        rsum = lambda z: jnp.sum(z, axis=1, keepdims=True)

        s0 = jnp.concatenate([cks[0][0], cks[1][0]], axis=0)
        hist_ref[0] = s0
        res = _seg_dot([(s0 * (-rows(4, 0))).astype(BF16), *_col_lhs(rows(3, 0), diag_b)], e)
        sah_ref[0] = res[0]
        mm_ref[1] = res[1] + res[2]

        def fstep(i, carry):
            nxt = jnp.minimum(i + 1, CH - 1)
            S = hist_ref[i] * rows(1, i) + sah_ref[i] * rows(5, i) + mm_ref[1] * rows(2, i)
            hist_ref[i + 1] = S
            res = _seg_dot([(S * (-rows(4, nxt))).astype(BF16), *_col_lhs(rows(3, nxt), diag_b)], e)
            sah_ref[i + 1] = res[0]
            mm_ref[1] = res[1] + res[2]
            return carry

        lax.fori_loop(0, CH, fstep, 0)

        res = _seg_dot([*_col_lhs(dy_rows(CH - 1), diag_b), *_col_lhs(rows(3, CH - 1), diag_b)], e)
        mm_ref[0] = res[0] + res[1]
        mm_ref[1] = res[2] + res[3]

        def bstep(ii, carry):
            i = CH - 1 - ii
            prv = jnp.maximum(i - 1, 0)
            sp, dyc = hist_ref[i], mm_ref[0]
            ds = ds_ref[...] + dyc * rows(0, i)
            put(0, i, rsum(hist_ref[i + 1] * dyc))
            put(1, i, rsum(ds * sp))
            put(5, i, rsum(ds * sah_ref[i]))
            put(2, i, rsum(ds * mm_ref[1]))
            res = _seg_dot([(ds * rows(5, i)).astype(BF16), (ds * rows(2, i)).astype(BF16),
                            *_col_lhs(dy_rows(prv), diag_b), *_col_lhs(rows(3, prv), diag_b)], e)
            dsa = res[0]
            put(4, i, -rsum(sp * dsa))
            put(3, i, rsum(diag * res[1]))
            mm_ref[0] = res[2] + res[3]
            mm_ref[1] = res[4] + res[5]
            ds_ref[...] = ds * rows(1, i) - dsa * rows(4, i)
            return carry

        lax.fori_loop(0, CH, bstep, 0)

    cpos = lambda g, rev: _chunk_pos(NCH - 1 - g, rev, cfg)
    toks = [pl.BlockSpec((G, CH, LANES), lambda g, rev=rev: (0, cpos(g, rev), 0)) for rev in (False, True)]
    dy_specs = [pl.BlockSpec((G, CH, LANES), lambda g, rev=rev: (0, jnp.maximum(cpos(g, rev) - NCC, 0), 0))
                for rev in (False, True)]
    ck_spec = pl.BlockSpec((1, G, HEAD, LANES), lambda g: (NCH - 1 - g, 0, 0, 0))
    out = jax.ShapeDtypeStruct((G, T, LANES), F32)
    res = pl.pallas_call(
        body, name="scan_bwd", grid=(NCH,),
        in_specs=[toks[0]] * 6 + [toks[1]] * 6 + dy_specs + [ck_spec, ck_spec, _full_spec((LANES, LANES))],
        out_specs=[toks[0]] * 6 + [toks[1]] * 6,
        out_shape=[out] * 12,
        scratch_shapes=[pltpu.VMEM((CH + 1, 2 * G, HEAD, LANES), F32), pltpu.VMEM((CH + 1, 2 * G, HEAD, LANES), F32),
                        pltpu.VMEM((2 * G, HEAD, LANES), F32), pltpu.VMEM((2, 2 * G, HEAD, LANES), F32)],
        compiler_params=_params(("arbitrary",)),
    )(*ops_f, *ops_b, dy, dy, ck_f, ck_b, e128)
    return res[:6], res[6:]


def _mix_prep_bwd(p, mup, mun, w0, w2p, a0, a2p, k_k, k_a, cw, cb, clw, clb, e_w, sf, sb, ro, cfg):
    B, T, TT, SP, CP, W, CW, HP, JC, PW, GP, KC, KP = (cfg.B, cfg.T, cfg.TT, cfg.SP, cfg.CP, cfg.W, cfg.CW,
                                                       cfg.HP, cfg.JC, cfg.PW, cfg.GP, cfg.KC, cfg.KP)
    pad = KC // 2

    def body(p_ref, prev_ref, next_ref, mup_ref, mun_ref, w0_ref, w2_ref, a0_ref, a2_ref, kk_ref, ka_ref,
             cw_ref, cb_ref, clw_ref, clb_ref, e_ref, *rest):
        sf_refs, sb_refs = rest[0:6], rest[6:12]
        rdr_ref, rdv_ref, rdkb_ref, rdgd_ref, rdcv_ref = rest[12:17]
        (dpz_ref, dmup_ref, dmun_ref, dw0_ref, dw2_ref, da0_ref, da2_ref, dkk_ref, dka_ref,
         dcw_ref, dcb_ref, dclw_ref, dclb_ref) = rest[17:]
        b, j = pl.program_id(0), pl.program_id(1)
        first = jnp.logical_and(b == 0, j == 0)
        lat = (j >= JC).astype(F32)
        e = e_ref[...]
        mup_v, mun_v = mup_ref[...], mun_ref[...]
        z, zp, zn, rw = _shifted(p_ref, prev_ref, next_ref, mup_v, mun_v, j, cfg)

        def prep(rw_, w0_, w2_, a0_, a2_, kk_, ka_):
            return _rwkv_prep(rw_, w0_, w2_, a0_, a2_, kk_, ka_, e, cfg) + (rw_[:, 3 * W + 2 * PW:SP],)

        _, vjp_prep = jax.vjp(prep, rw, w0_ref[...], w2_ref[...], a0_ref[...], a2_ref[...], kk_ref[...], ka_ref[...])
        fr, fw, fk, fv, fkk, fb = [_from_heads(r_, cfg) for r_ in sf_refs]
        br, bw, bk, bv, bkk, bb = [_from_heads(r_, cfg) for r_ in sb_refs]
        half_kb = (0.5 * lat) * _from_heads(rdkb_ref, cfg)
        cots = (fr + br + lat * _from_heads(rdr_ref, cfg), fv + bv + lat * _from_heads(rdv_ref, cfg), fkk + bkk,
                fw, fk + half_kb, fb, bw, bk + half_kb, bb, lat * rdgd_ref[0])
        drw, dw0, dw2, da0, da2, dkk, dka = vjp_prep(cots)
        _acc(dmup_ref, jnp.sum(drw * (zp - z), axis=0, keepdims=True), first)
        _acc(dmun_ref, jnp.sum(drw * (zn - z), axis=0, keepdims=True), first)
        for ref, val in ((dw0_ref, dw0), (dw2_ref, dw2), (da0_ref, da0), (da2_ref, da2), (dkk_ref, dkk), (dka_ref, dka)):
            _acc(ref, val, first)

        cvv = p_ref[0, :, SP:]
        u, vjp_glu = jax.vjp(lambda c_: _glu(c_, cfg), cvv)
        cwv = cw_ref[...]
        _, vjp_post = jax.vjp(_conv_post, _dwconv(u, cwv, KC), cb_ref[...], clw_ref[...], clb_ref[...])
        dyc, dcb, dclw, dclb = vjp_post(lat * rdcv_ref[0])
        du = jnp.zeros_like(u)
        for i in range(KC):
            du = du + _line_shift(dyc, pad - i) * cwv[i:i + 1, :]
        (dcv,) = vjp_glu(du)
        dpz_ref[0, :, 0:SP] = drw
        dpz_ref[0, :, SP:] = dcv

        @pl.when(first)
        def _():
            dcw_ref[...] = jnp.zeros_like(dcw_ref)

        for i in range(KC):
            dcw_ref[i:i + 1, :] += jnp.sum(dyc * _line_shift(u, i - pad), axis=0, keepdims=True)
        _acc(dcb_ref, dcb, first)
        _acc(dclw_ref, dclw, first)
        _acc(dclb_ref, dclb, first)

    prev, nxt = _halo_specs(cfg, SP)
    hs = _head_spec(cfg, lambda j: j)
    hl = _head_spec(cfg, lambda j: jnp.maximum(j - JC, 0))
    latn = lambda n: pl.BlockSpec((1, TT, n), lambda b, j: (b, jnp.maximum(j - JC, 0), 0))
    vec = lambda n: jax.ShapeDtypeStruct((1, n), F32)
    small_shapes = [vec(SP), vec(SP), vec(2 * W), jax.ShapeDtypeStruct((PW, 2 * W), F32), vec(2 * W),
                    jax.ShapeDtypeStruct((PW, 2 * W), F32), vec(W), vec(W),
                    jax.ShapeDtypeStruct((KP, CW), F32), vec(CW), vec(CW), vec(CW)]
    return pl.pallas_call(
        body, name="mix_prep_bwd", grid=(B, cfg.J),
        in_specs=[pl.BlockSpec((1, TT, CP), lambda b, j: (b, j, 0)), prev, nxt,
                  _full_spec((1, SP)), _full_spec((1, SP)),
                  _full_spec((1, 2 * W)), _full_spec((PW, 2 * W)),
                  _full_spec((1, 2 * W)), _full_spec((PW, 2 * W)),
                  _full_spec((1, W)), _full_spec((1, W)),
                  _full_spec((KP, CW)), _full_spec((1, CW)), _full_spec((1, CW)), _full_spec((1, CW)),
                  _full_spec((W, W))] + [hs] * 12 + [hl] * 3 + [latn(GP), latn(CW)],
        out_specs=[pl.BlockSpec((1, TT, CP), lambda b, j: (b, j, 0))] + [_full_spec(s.shape) for s in small_shapes],
        out_shape=[jax.ShapeDtypeStruct((B, T, CP), F32)] + small_shapes,
        compiler_params=_params(("arbitrary", "arbitrary")),
    )(p, p, p, mup, mun, w0, w2p, a0, a2p, k_k, k_a, cw, cb, clw, clb, e_w, *sf, *sb, *ro)


def _in_proj_bwd(dpz, xcat, modt, g1, w_in_p, mup, mun, dx1, cfg):
    B, T, TX, D, TT, SP, CP, JC = cfg.B, cfg.T, cfg.TX, cfg.D, cfg.TT, cfg.SP, cfg.CP, cfg.JC

    def body(d_ref, prev_ref, next_ref, x_ref, mod_ref, g_ref, w_ref, mup_ref, mun_ref, dx1_ref,
             gx_ref, dp_ref, dmod_ref, dg_ref):
        b, j = pl.program_id(0), pl.program_id(1)
        has_prev, has_next = _halo_flags(j, cfg)
        mp, mn = mup_ref[...], mun_ref[...]
        drw = d_ref[0, :, 0:SP]
        dprev, dnext = _shift_rows(drw, prev_ref[0, SUBLANES - 1:SUBLANES, :] * has_prev,
                                   next_ref[0, 0:1, :] * has_next)
        dz = drw * (1.0 - mp - mn) + mp * dnext + mn * dprev
        dpb = jnp.concatenate([dz, d_ref[0, :, SP:]], axis=-1).astype(BF16)
        dp_ref[0] = dpb
        dh = _bdot_nt(dpb, w_ref[...])
        _, vjp_h = jax.vjp(_rms_mod, x_ref[0], g_ref[...], mod_ref[0, 0, 0:1, :], mod_ref[0, 0, 1:2, :])
        dx, dg, dsh, dsc = vjp_h(dh)
        dmod_ref[0, 0] = jnp.concatenate([dsh, dsc], axis=0)
        _acc(dg_ref, dg, jnp.logical_and(b == 0, j == 0))

        @pl.when(j >= JC)
        def _():
            gx_ref[0] = dx + dx1_ref[0]

    prev, nxt = _halo_specs(cfg, SP)
    lat = pl.BlockSpec((1, TT, D), lambda b, j: (b, jnp.maximum(j - JC, 0), 0))
    return pl.pallas_call(
        body, name="in_proj_bwd", grid=(B, cfg.J),
        in_specs=[pl.BlockSpec((1, TT, CP), lambda b, j: (b, j, 0)), prev, nxt,
                  pl.BlockSpec((1, TT, D), lambda b, j: (b, j, 0)),
                  pl.BlockSpec((1, 1, 2, D), lambda b, j: (b, j, 0, 0)),
                  _full_spec((1, D)), _full_spec((D, CP)), _full_spec((1, SP)), _full_spec((1, SP)), lat],
        out_specs=[lat, pl.BlockSpec((1, TT, CP), lambda b, j: (b, j, 0)),
                   pl.BlockSpec((1, 1, 2, D), lambda b, j: (b, j, 0, 0)), _full_spec((1, D))],
        out_shape=[jax.ShapeDtypeStruct((B, TX, D), F32), jax.ShapeDtypeStruct((B, T, CP), BF16),
                   jax.ShapeDtypeStruct((B, cfg.J, 2, D), F32), jax.ShapeDtypeStruct((1, D), F32)],
        compiler_params=_params(("arbitrary", "arbitrary")),
    )(dpz, dpz, dpz, xcat, modt, g1, w_in_p, mup, mun, dx1)


def _pick_tile(n, pref):
    for t in pref:
        if n % t == 0:
            return t
    return n


def _grad_matmul(a, g, name):
    K, M = a.shape
    N = g.shape[1]
    tm = _pick_tile(M, (512, 256, 128))
    tn = _pick_tile(N, (1024, 768, 512, 256, 128))
    tk = _pick_tile(K, (512, 256, 128, 64))
    nk = K // tk

    def body(a_ref, g_ref, o_ref):
        k = pl.program_id(2)
        _acc(o_ref, _bdot_tn(a_ref[...], g_ref[...]), k == 0)

    return pl.pallas_call(
        body, name=name, grid=(M // tm, N // tn, nk),
        in_specs=[pl.BlockSpec((tk, tm), lambda i, j, k: (k, i)),
                  pl.BlockSpec((tk, tn), lambda i, j, k: (k, j))],
        out_specs=pl.BlockSpec((tm, tn), lambda i, j, k: (i, j)),
        out_shape=jax.ShapeDtypeStruct((M, N), F32),
        compiler_params=_params(("parallel", "parallel", "arbitrary")),
    )(a, g)


def _ada_fwd(crows, ada_w, ada_b):
    D = crows.shape[1]
    n6 = ada_w.shape[1]
    tn = _pick_tile(n6, (1024, 512, 256, 128))

    def body(c_ref, w_ref, b_ref, s_ref, m_ref):
        s = _silu(c_ref[...])
        s_ref[...] = s
        m_ref[...] = _bdot(s, w_ref[...]) + b_ref[...]

    return pl.pallas_call(
        body, name="ada_fwd", grid=(n6 // tn,),
        in_specs=[_full_spec((SUBLANES, D)), pl.BlockSpec((D, tn), lambda i: (0, i)),
                  pl.BlockSpec((1, tn), lambda i: (0, i))],
        out_specs=[_full_spec((SUBLANES, D)), pl.BlockSpec((SUBLANES, tn), lambda i: (0, i))],
        out_shape=[jax.ShapeDtypeStruct((SUBLANES, D), F32), jax.ShapeDtypeStruct((SUBLANES, n6), F32)],
        compiler_params=_params(("arbitrary",)),
    )(crows, ada_w, ada_b)


def _ada_bwd(s_all, g_all, g_mine, c_ctx, ada_w, nb):
    D = s_all.shape[1]
    n6 = g_all.shape[1]
    ns = g_mine.shape[1]

    def body(s_ref, g_ref, gm_ref, c_ref, w_ref, dw_ref, db_ref, dc_ref):
        g = g_ref[...]
        dw_ref[...] = _bdot_tn(s_ref[...], gm_ref[...])
        db_ref[...] = jnp.sum(g, axis=0, keepdims=True)
        rows = lax.broadcasted_iota(jnp.int32, (g.shape[0], 1), 0)
        gc = jnp.sum(jnp.where(rows % SUBLANES == nb, g, 0.0), axis=0, keepdims=True)
        ds = _bdot_nt(gc, w_ref[...])
        c = c_ref[...]
        sg = _sigmoid(c)
        dc_ref[...] = ds * (sg + c * sg * (1.0 - sg))

    return pl.pallas_call(
        body, name="ada_bwd",
        out_shape=[jax.ShapeDtypeStruct((D, ns), F32), jax.ShapeDtypeStruct((1, n6), F32),
                   jax.ShapeDtypeStruct((1, D), F32)],
        compiler_params=_params(),
    )(s_all, g_all, g_mine, c_ctx, ada_w)


def _adamw(parts, w, m, v, name):
    P, R, C = parts.shape
    tr = _pick_tile(R, (256, 128, 64, 32, 16, 8))

    def body(p_ref, w_ref, m_ref, v_ref, g_ref, d_ref, nm_ref, nv_ref):
        g = p_ref[0]
        for i in range(1, P):
            g = g + p_ref[i]
        nm = ADAM_B1 * m_ref[...] + (1.0 - ADAM_B1) * g
        nv = ADAM_B2 * v_ref[...] + (1.0 - ADAM_B2) * (g * g)
        m_hat = nm / (1.0 - ADAM_B1 ** ADAM_STEP)
        v_hat = nv / (1.0 - ADAM_B2 ** ADAM_STEP)
        g_ref[...] = g
        d_ref[...] = -ADAM_LR * (m_hat / (jnp.sqrt(v_hat) + ADAM_EPS) + ADAM_WD * w_ref[...])
        nm_ref[...] = nm
        nv_ref[...] = nv

    blk = pl.BlockSpec((tr, C), lambda i: (i, 0))
    out = jax.ShapeDtypeStruct((R, C), F32)
    return pl.pallas_call(
        body, name=name, grid=(R // tr,),
        in_specs=[pl.BlockSpec((P, tr, C), lambda i: (0, i, 0)), blk, blk, blk],
        out_specs=[blk] * 4, out_shape=[out] * 4,
        compiler_params=_params(("parallel",)),
    )(parts, w, m, v)


def _local_step(cfg, x, c, ctx, tgt, fw):
    B, D, W, CW, JC, T, TX = cfg.B, cfg.D, cfg.W, cfg.CW, cfg.JC, cfg.T, cfg.TX
    e_w = _block_ones(W)
    e128 = _block_ones(LANES)
    row = lambda a: a.reshape(1, -1)

    ada_wb = fw["ada_w"].astype(BF16)
    w_in_p = _pad_cols(fw["w_in"], cfg).astype(BF16)
    mup = _pad_cols(fw["mu_prev"], cfg, True)
    mun = _pad_cols(fw["mu_next"], cfg, True)
    w0, a0 = row(fw["decay_w0"]), row(fw["iclr_a0"])
    w2p, a2p = _pair_weight(fw["decay_w2"], cfg), _pair_weight(fw["iclr_a2"], cfg)
    cw = jnp.pad(fw["conv_w"], ((0, cfg.KP - cfg.KC), (0, 0)))
    gw2p = jnp.pad(fw["gate_w2"], ((0, cfg.GP - cfg.GR), (0, 0)))
    r_k = row(fw["r_k"])
    w_outb, w1b, w2b = fw["w_out"].astype(BF16), fw["mlp_w1"].astype(BF16), fw["mlp_w2"].astype(BF16)

    crows = jnp.concatenate([c, fw["c_ctx"], jnp.zeros((SUBLANES - B - 1, D), F32)], axis=0)
    s_rows, mods = _ada_fwd(crows, ada_wb, fw["ada_b"])
    mod_x = mods[:B].reshape(B, 6, D)
    mod_c = mods[B].reshape(6, D)
    modt = jnp.concatenate([jnp.broadcast_to(mod_c[None, None, 0:2], (B, JC, 2, D)),
                            jnp.broadcast_to(mod_x[:, None, 0:2], (B, cfg.JX, 2, D))], axis=1)
    mod2, mod345 = mod_x[:, 2:3], mod_x[:, 3:6]

    xcat = jnp.concatenate([ctx, x], axis=1)
    p, hb = _in_proj(xcat, modt, fw["mix_pre_g"], w_in_p, cfg)
    prep_w = (mup, mun, w0, w2p, a0, a2p, fw["k_k"], fw["k_a"], cw, fw["conv_b"], fw["conv_ln_w"],
              fw["conv_ln_b"], e_w)
    r, v, kk, w_f, kd_f, b_f, w_b, kd_b, b_b, gd, conv = _mix_prep(p, *prep_w, cfg)
    flat = lambda a: a.reshape(cfg.G, a.shape[2], LANES)
    heads = lambda a: a.reshape(B, cfg.HP, a.shape[1], LANES)
    ops_f = tuple(flat(a) for a in (r, w_f, kd_f, v, kk, b_f))
    ops_b = tuple(flat(a) for a in (r, w_b, kd_b, v, kk, b_b))
    y_f, ck_f, y_b, ck_b = _scan_fwd(ops_f, ops_b, e128, cfg)
    out_args = (heads(y_f), heads(y_b), kd_f, kd_b, r, v, gd, conv, x, mod2, r_k, gw2p, fw["lnx_w"], fw["lnx_b"],
                w_outb, fw["mix_post_g"], e_w)
    x1 = _mix_out(*out_args, cfg)

    dx1, loss_t, h2b, dpreb, actb, dffb, dmod345, dg3, dg4 = _mlp_fwd_bwd(
        x1, tgt, mod345, fw["mlp_pre_g"], fw["mlp_post_g"], w1b, w2b, cfg)
    (dy, dkb, dr_c, dv_c, dgd, dconv, catb, dmixb, dmod2, dg2, drk, dgw, dlw, dlb) = _mix_out_bwd(
        *out_args, dx1, cfg)
    sf, sb = _scan_bwd(ops_f, ops_b, flat(dy), ck_f, ck_b, e128, cfg)
    (dpz, dmup, dmun, dw0, dw2p, da0, da2p, dkk, dka, dcw, dcb, dclw, dclb) = _mix_prep_bwd(
        p, *prep_w, [heads(a) for a in sf], [heads(a) for a in sb], (dr_c, dv_c, dkb, dgd, dconv), cfg)
    grad_x, dpb, dmodt, dg1 = _in_proj_bwd(dpz, xcat, modt, fw["mix_pre_g"], w_in_p, mup, mun, dx1, cfg)

    tokens = lambda a: a.reshape(-1, a.shape[-1])
    d_w_in = _grad_matmul(tokens(hb), tokens(dpb), "grad_w_in")
    d_w_out = _grad_matmul(tokens(catb), tokens(dmixb), "grad_w_out")
    d_w1 = _grad_matmul(tokens(h2b), tokens(dpreb), "grad_mlp_w1")
    d_w2 = _grad_matmul(tokens(actb), tokens(dffb), "grad_mlp_w2")

    grads = {
        "mix_pre_g": dg1, "mix_post_g": dg2, "mlp_pre_g": dg3, "mlp_post_g": dg4,
        "w_in": _unpad_cols(d_w_in, cfg),
        "mu_prev": _unpad_cols(dmup, cfg, True), "mu_next": _unpad_cols(dmun, cfg, True),
        "decay_w0": dw0.reshape(2, W), "decay_w2": _unpair_weight(dw2p, cfg),
        "iclr_a0": da0.reshape(2, W), "iclr_a2": _unpair_weight(da2p, cfg),
        "k_k": dkk, "k_a": dka, "r_k": drk.reshape(fw["r_k"].shape),
        "gate_w2": dgw[:cfg.GR], "lnx_w": dlw, "lnx_b": dlb,
        "conv_w": dcw[:cfg.KC], "conv_b": dcb, "conv_ln_w": dclw, "conv_ln_b": dclb,
        "w_out": d_w_out, "mlp_w1": d_w1, "mlp_w2": d_w2,
    }
    dmod_x = jnp.concatenate([jnp.sum(dmodt[:, JC:], axis=1), dmod2, dmod345], axis=1).reshape(B, 6 * D)
    dmod_c = jnp.concatenate([jnp.sum(dmodt[:, :JC], axis=(0, 1)), jnp.zeros((4, D), F32)], axis=0).reshape(1, 6 * D)
    g_rows = jnp.concatenate([dmod_x, dmod_c, jnp.zeros((SUBLANES - B - 1, 6 * D), F32)], axis=0)
    return loss_t, grad_x, grads, s_rows, g_rows


def _my_index():
    return 4 * lax.axis_index("x") + 2 * lax.axis_index("y") + lax.axis_index("c")


def _exchange(arrays, scatter, name):
    n = len(arrays)
    out_shape = [jax.ShapeDtypeStruct(a.shape if s else (N_DEV,) + a.shape, a.dtype)
                 for a, s in zip(arrays, scatter)]

    def body(*refs):
        ins, outs = refs[:n], refs[n:2 * n]
        send_sems, recv_sems, local_sems = refs[2 * n:]
        x, y, c = lax.axis_index("x"), lax.axis_index("y"), lax.axis_index("c")
        me = 4 * x + 2 * y + c
        flip = lambda v, f: 1 - v if f else v

        def piece(a, dest):
            return ins[a].at[dest] if scatter[a] else ins[a]

        local = [pltpu.make_async_copy(piece(a, me), outs[a].at[me], local_sems.at[a]) for a in range(n)]
        for cp in local:
            cp.start()
        sends, recvs = [], []
        for k in range(1, N_DEV):
            fx, fy, fc = (k >> 2) & 1, (k >> 1) & 1, k & 1
            peer = (flip(x, fx), flip(y, fy), flip(c, fc))
            peer_idx = 4 * peer[0] + 2 * peer[1] + peer[2]
            for a in range(n):
                sends.append(pltpu.make_async_remote_copy(
                    src_ref=piece(a, peer_idx), dst_ref=outs[a].at[me],
                    send_sem=send_sems.at[k - 1, a], recv_sem=recv_sems.at[k - 1, a],
                    device_id=peer, device_id_type=pl.DeviceIdType.MESH))
                recvs.append(pltpu.make_async_remote_copy(
                    src_ref=piece(a, peer_idx), dst_ref=outs[a].at[peer_idx],
                    send_sem=send_sems.at[k - 1, a], recv_sem=recv_sems.at[k - 1, a],
                    device_id=peer, device_id_type=pl.DeviceIdType.MESH))
        for cp in sends:
            cp.start()
        for cp in recvs:
            cp.wait_recv()
        for cp in sends:
            cp.wait_send()
        for cp in local:
            cp.wait()

    hbm = pl.BlockSpec(memory_space=pltpu.HBM)
    return pl.pallas_call(
        body, name=name, out_shape=out_shape,
        in_specs=[hbm] * n, out_specs=[hbm] * n,
        scratch_shapes=[pltpu.SemaphoreType.DMA((N_DEV - 1, n)), pltpu.SemaphoreType.DMA((N_DEV - 1, n)),
                        pltpu.SemaphoreType.DMA((n,))],
    )(*arrays)


def _pack(parts):
    flat = jnp.concatenate([p.reshape(-1) for p in parts])
    total = _round_up(flat.shape[0], SUBLANES * LANES)
    return jnp.pad(flat, (0, total - flat.shape[0])).reshape(-1, LANES)


def _unpack(buf, shapes):
    flat = buf.reshape(-1)
    out, pos = [], 0
    for s in shapes:
        n = int(np.prod(s))
        out.append(flat[pos:pos + n].reshape(s))
        pos += n
    return out


_SHARDED_SMALL = ("decay_w0", "decay_w2", "iclr_a0", "iclr_a2", "gate_w2", "conv_w")
_REPLICATED = ("mix_pre_g", "mix_post_g", "mlp_pre_g", "mlp_post_g", "mu_prev", "mu_next", "k_k", "k_a", "r_k",
               "lnx_w", "lnx_b", "conv_b", "conv_ln_w", "conv_ln_b")
_ADA_SMALL = ("c_ctx", "ada_b")
_WEIGHTS = ("c_ctx", "ada_w", "ada_b", "mix_pre_g", "mix_post_g", "mlp_pre_g", "mlp_post_g", "w_in", "mu_prev",
            "mu_next", "decay_w0", "decay_w2", "iclr_a0", "iclr_a2", "k_k", "k_a", "r_k", "gate_w2", "lnx_w", "lnx_b",
            "conv_w", "conv_b", "conv_ln_w", "conv_ln_b", "w_out", "mlp_w1", "mlp_w2")
_INPUTS = ("x", "c", "ctx") + _WEIGHTS + ("loss_target",) + tuple("m_" + n for n in _WEIGHTS) + tuple(
    "v_" + n for n in _WEIGHTS)


def _cols_to_blocks(a):
    a = a.reshape(a.shape[:-1] + (N_DEV, a.shape[-1] // N_DEV))
    return jnp.moveaxis(a, -2, 0)


def _blocks_to_cols(a):
    a = jnp.moveaxis(a, 0, -2)
    return a.reshape(a.shape[:-2] + (a.shape[-2] * a.shape[-1],))


def kernel(x, c, ctx, c_ctx, ada_w, ada_b, mix_pre_g, mix_post_g, mlp_pre_g, mlp_post_g, w_in, mu_prev, mu_next, decay_w0, decay_w2, iclr_a0, iclr_a2, k_k, k_a, r_k, gate_w2, lnx_w, lnx_b, conv_w, conv_b, conv_ln_w, conv_ln_b, w_out, mlp_w1, mlp_w2, loss_target, m_c_ctx, m_ada_w, m_ada_b, m_mix_pre_g, m_mix_post_g, m_mlp_pre_g, m_mlp_post_g, m_w_in, m_mu_prev, m_mu_next, m_decay_w0, m_decay_w2, m_iclr_a0, m_iclr_a2, m_k_k, m_k_a, m_r_k, m_gate_w2, m_lnx_w, m_lnx_b, m_conv_w, m_conv_b, m_conv_ln_w, m_conv_ln_b, m_w_out, m_mlp_w1, m_mlp_w2, v_c_ctx, v_ada_w, v_ada_b, v_mix_pre_g, v_mix_post_g, v_mlp_pre_g, v_mlp_post_g, v_w_in, v_mu_prev, v_mu_next, v_decay_w0, v_decay_w2, v_iclr_a0, v_iclr_a2, v_k_k, v_k_a, v_r_k, v_gate_w2, v_lnx_w, v_lnx_b, v_conv_w, v_conv_b, v_conv_ln_w, v_conv_ln_b, v_w_out, v_mlp_w1, v_mlp_w2):
    given = dict(zip(_INPUTS, (x, c, ctx, c_ctx, ada_w, ada_b, mix_pre_g, mix_post_g, mlp_pre_g, mlp_post_g, w_in, mu_prev, mu_next, decay_w0, decay_w2, iclr_a0, iclr_a2, k_k, k_a, r_k, gate_w2, lnx_w, lnx_b, conv_w, conv_b, conv_ln_w, conv_ln_b, w_out, mlp_w1, mlp_w2, loss_target, m_c_ctx, m_ada_w, m_ada_b, m_mix_pre_g, m_mix_post_g, m_mlp_pre_g, m_mlp_post_g, m_w_in, m_mu_prev, m_mu_next, m_decay_w0, m_decay_w2, m_iclr_a0, m_iclr_a2, m_k_k, m_k_a, m_r_k, m_gate_w2, m_lnx_w, m_lnx_b, m_conv_w, m_conv_b, m_conv_ln_w, m_conv_ln_b, m_w_out, m_mlp_w1, m_mlp_w2, v_c_ctx, v_ada_w, v_ada_b, v_mix_pre_g, v_mix_post_g, v_mlp_pre_g, v_mlp_post_g, v_w_in, v_mu_prev, v_mu_next, v_decay_w0, v_decay_w2, v_iclr_a0, v_iclr_a2, v_k_k, v_k_a, v_r_k, v_gate_w2, v_lnx_w, v_lnx_b, v_conv_w, v_conv_b, v_conv_ln_w, v_conv_ln_b, v_w_out, v_mlp_w1, v_mlp_w2)))
    loc = {}
    for pre in ("", "m_", "v_"):
        for n in _WEIGHTS:
            a = given[pre + n]
            a = a.reshape(1, -1) if n == "c_ctx" else a[0]
            loc[pre + n] = a.reshape(1, -1) if a.ndim == 1 else a
    B, TX, D = x.shape
    W, CW = loc["k_k"].shape[1], loc["conv_b"].shape[1]
    cfg = _Cfg(B, TX, ctx.shape[1], D, W, CW, loc["decay_w2"].shape[1], loc["gate_w2"].shape[0],
               loc["conv_w"].shape[0], loc["mlp_w1"].shape[1] * N_DEV)
    me = _my_index()

    small_shapes = [loc[n].shape for n in _SHARDED_SMALL]
    got = _exchange(
        [loc["ada_w"].astype(BF16), loc["w_in"].astype(BF16), loc["w_out"].astype(BF16),
         loc["mlp_w1"].astype(BF16), loc["mlp_w2"].astype(BF16), _pack([loc[n] for n in _SHARDED_SMALL])],
        [False] * 6, "gather_weights")
    fw = {n: loc[n] for n in _REPLICATED + _ADA_SMALL}
    fw["ada_w"] = _blocks_to_cols(got[0])
    fw["w_in"] = _blocks_to_cols(got[1])
    fw["w_out"] = got[2].reshape(-1, D)
    fw["mlp_w1"] = _blocks_to_cols(got[3])
    fw["mlp_w2"] = got[4].reshape(-1, D)
    per_dev = [_unpack(got[5][i], small_shapes) for i in range(N_DEV)]
    for j, n in enumerate(_SHARDED_SMALL):
        fw[n] = jnp.concatenate([per_dev[i][j] for i in range(N_DEV)], axis=-1)

    loss_t, grad_x, grads, s_rows, g_rows = _local_step(cfg, x, c, ctx, loss_target, fw)
    loss = lax.psum(jnp.sum(loss_t[:, :, 0, 0]), ("x", "y", "c"))

    small_blocks = jnp.stack([_pack([_cols_to_blocks(grads[n])[i] for n in _SHARDED_SMALL]) for i in range(N_DEV)])
    sent = _exchange(
        [_cols_to_blocks(grads["w_in"]), grads["w_out"].reshape(N_DEV, -1, D), _cols_to_blocks(grads["mlp_w1"]),
         grads["mlp_w2"].reshape(N_DEV, -1, D), small_blocks,
         _pack([grads[n] for n in _REPLICATED]), s_rows, g_rows],
        [True] * 5 + [False] * 3, "exchange_grads")
    s_all = sent[6].reshape(N_DEV * SUBLANES, D)
    g_all = sent[7].reshape(N_DEV * SUBLANES, 6 * D)
    ns = 6 * D // N_DEV
    g_mine = lax.dynamic_slice_in_dim(g_all, me * ns, ns, axis=1)
    d_ada_w, d_ada_b, d_c_ctx = _ada_bwd(s_all, g_all, g_mine, loc["c_ctx"], fw["ada_w"], B)

    res = {}

    def update(name, parts):
        res[name] = _adamw(parts, loc[name], loc["m_" + name], loc["v_" + name], "adamw_" + name)

    update("w_in", sent[0])
    update("w_out", sent[1])
    update("mlp_w1", sent[2])
    update("mlp_w2", sent[3])
    update("ada_w", d_ada_w[None])

    def update_packed(names, parts, tag):
        shapes = [loc[n].shape for n in names]
        packed = _adamw(parts, *[_pack([loc[pre + n] for n in names]) for pre in ("", "m_", "v_")], "adamw_" + tag)
        unpacked = [_unpack(p, shapes) for p in packed]
        for j, n in enumerate(names):
            res[n] = tuple(u[j] for u in unpacked)

    update_packed(_SHARDED_SMALL, sent[4], "sharded_small")
    update_packed(_REPLICATED, sent[5], "replicated")
    update_packed(_ADA_SMALL, _pack([d_c_ctx, d_ada_b])[None], "ada_small")

    outs = [loss, grad_x]
    for k in range(4):
        for n in _WEIGHTS:
            outs.append(res[n][k].reshape(given[n].shape))
    return tuple(outs)
```

```python
import functools

import numpy as np
import jax
import jax.numpy as jnp
from jax import lax
from jax.experimental import pallas as pl
from jax.experimental.pallas import tpu as pltpu

F32 = jnp.float32
BF16 = jnp.bfloat16

EPS_RMS = 1e-6
EPS_LN = 1e-5
EPS_GN = 64e-5
LINE = 64
HEAD = 64
LANES = 128
SUBLANES = 8
SCAN_CHUNK = 16
N_DEV = 8
VMEM_LIMIT = 56 * 1024 * 1024

ADAM_LR = 0.001
ADAM_B1 = 0.9
ADAM_B2 = 0.999
ADAM_EPS = 1e-08
ADAM_WD = 0.01
ADAM_STEP = 10


def _round_up(n, m):
    return (n + m - 1) // m * m


def _params(semantics=None, vmem=VMEM_LIMIT):
    return pltpu.CompilerParams(dimension_semantics=semantics, vmem_limit_bytes=vmem)


def _bdot(a, b):
    return jnp.dot(a.astype(BF16), b.astype(BF16), preferred_element_type=F32)


def _bdot_nt(a, b):
    return lax.dot_general(a.astype(BF16), b.astype(BF16), (((1,), (1,)), ((), ())),
                           preferred_element_type=F32)


def _bdot_tn(a, b):
    return lax.dot_general(a.astype(BF16), b.astype(BF16), (((0,), (0,)), ((), ())),
                           preferred_element_type=F32)


@jax.custom_vjp
def _mm(a, b):
    return _bdot(a, b)


def _mm_fwd(a, b):
    return _bdot(a, b), (a, b)


def _mm_bwd(res, g):
    a, b = res
    return _bdot_nt(g, b), _bdot_tn(a, g)


_mm.defvjp(_mm_fwd, _mm_bwd)


def _seg_sum_raw(x, e):
    hi = x.astype(BF16)
    lo = (x - hi.astype(F32)).astype(BF16)
    return (jnp.dot(hi, e, preferred_element_type=F32)
            + jnp.dot(lo, e, preferred_element_type=F32))


@jax.custom_vjp
def _seg_sum(x, e):
    return _seg_sum_raw(x, e)


def _seg_sum_fwd(x, e):
    return _seg_sum_raw(x, e), e


def _seg_sum_bwd(e, g):
    return _seg_sum_raw(g, e), None


_seg_sum.defvjp(_seg_sum_fwd, _seg_sum_bwd)


def _block_ones(n, seg=HEAD):
    i = np.arange(n) // seg
    return jnp.asarray((i[:, None] == i[None, :]).astype(np.float32), dtype=BF16)


def _rms(xv, g):
    ms = jnp.mean(xv * xv, axis=-1, keepdims=True)
    return xv * lax.rsqrt(ms + EPS_RMS) * g


def _rms_mod(xv, g, shift, scale):
    return _rms(xv, g) * (1.0 + scale) + shift


def _sigmoid(z):
    return 1.0 / (1.0 + jnp.exp(-z))


def _silu(z):
    return z * _sigmoid(z)


def _softplus(z):
    return jnp.maximum(z, 0.0) + jnp.log(1.0 + jnp.exp(-jnp.abs(z)))


class _Cfg:
    def __init__(self, B, TX, TC, D, W, CW, R, GR, KC, F):
        self.B, self.TX, self.TC, self.D = B, TX, TC, D
        self.W, self.CW, self.R, self.GR, self.KC, self.F = W, CW, R, GR, KC, F
        self.T = TX + TC
        self.TT = min(256, TC)
        assert TC % self.TT == 0 and TX % self.TT == 0 and self.TT % LINE == 0
        self.JC = TC // self.TT
        self.JX = TX // self.TT
        self.J = self.JC + self.JX
        self.HP = W // LANES
        self.G = B * self.HP
        self.PW = _round_up(2 * R, LANES)
        self.GP = _round_up(GR, LANES)
        self.SP = 3 * W + 2 * self.PW + self.GP
        self.CP = self.SP + 2 * CW
        self.KP = _round_up(KC, SUBLANES)
        assert self.T % SCAN_CHUNK == 0 and TC % SCAN_CHUNK == 0
        self.NCH = self.T // SCAN_CHUNK
        self.NCC = TC // SCAN_CHUNK
        W_, R_ = W, R
        segs = [(0, 3 * W_, 0),
                (3 * W_, 2 * R_, 3 * W_),
                (3 * W_ + 2 * R_, 2 * R_, 3 * W_ + self.PW),
                (3 * W_ + 4 * R_, GR, 3 * W_ + 2 * self.PW),
                (3 * W_ + 4 * R_ + GR, 2 * CW, self.SP)]
        self.col_segs = segs
        self.shift_cols = 3 * W_ + 4 * R_ + GR
        self.in_cols = self.shift_cols + 2 * CW


def _pad_cols(a, cfg, upto_shift=False):
    width = cfg.SP if upto_shift else cfg.CP
    pieces, pos = [], 0
    for src, n, dst in cfg.col_segs:
        if upto_shift and dst >= cfg.SP:
            break
        if dst > pos:
            pieces.append(jnp.zeros(a.shape[:-1] + (dst - pos,), a.dtype))
        pieces.append(a[..., src:src + n])
        pos = dst + n
    if width > pos:
        pieces.append(jnp.zeros(a.shape[:-1] + (width - pos,), a.dtype))
    return jnp.concatenate(pieces, axis=-1)


def _unpad_cols(a, cfg, upto_shift=False):
    pieces = []
    for src, n, dst in cfg.col_segs:
        if upto_shift and dst >= cfg.SP:
            break
        pieces.append(a[..., dst:dst + n])
    return jnp.concatenate(pieces, axis=-1)


def _pair_weight(w2, cfg):
    R, W = cfg.R, cfg.W
    out = jnp.zeros((cfg.PW, 2 * W), w2.dtype)
    out = out.at[0:R, 0:W].set(w2[0])
    out = out.at[R:2 * R, W:2 * W].set(w2[1])
    return out


def _unpair_weight(g, cfg):
    R, W = cfg.R, cfg.W
    return jnp.stack([g[0:R, 0:W], g[R:2 * R, W:2 * W]])


def _row_ids(n):
    return lax.broadcasted_iota(jnp.int32, (n, 1), 0)


def _shift_rows(z, prev_row, next_row):
    n = z.shape[0]
    rows = _row_ids(n)
    zp = jnp.where(rows == 0, prev_row, pltpu.roll(z, 1, 0))
    zn = jnp.where(rows == n - 1, next_row, pltpu.roll(z, n - 1, 0))
    return zp, zn


def _line_shift(u, d):
    if d == 0:
        return u
    n = u.shape[0]
    lt = _row_ids(n) % LINE
    ok = jnp.logical_and(lt + d >= 0, lt + d < LINE)
    return jnp.where(ok, pltpu.roll(u, (-d) % n, 0), 0.0)


def _dwconv(u, cw, kc):
    pad = kc // 2
    acc = jnp.zeros_like(u)
    for i in range(kc):
        acc = acc + _line_shift(u, i - pad) * cw[i:i + 1, :]
    return acc


def _rwkv_prep(rw, w0, w2p, a0, a2p, k_k, k_a, e, cfg):
    W, PW = cfg.W, cfg.PW
    r = rw[:, 0:W]
    k = rw[:, W:2 * W]
    v = rw[:, 2 * W:3 * W]
    wdp = rw[:, 3 * W:3 * W + PW]
    adp = rw[:, 3 * W + PW:3 * W + 2 * PW]
    wl = w0 + _mm(jnp.tanh(wdp), w2p)
    w_log = -_softplus(-wl) - 0.5
    decay = jnp.exp(-jnp.exp(w_log))
    iclr = _sigmoid(a0 + _mm(adp, a2p))
    kkr = k * k_k
    nrm = jnp.sqrt(_seg_sum(kkr * kkr, e))
    kk = kkr / jnp.maximum(nrm, 1e-12)
    outs = [r, v, kk]
    for d in range(2):
        ic = iclr[:, d * W:(d + 1) * W]
        outs += [decay[:, d * W:(d + 1) * W], k * (1.0 + (ic - 1.0) * k_a), kk * ic]
    return tuple(outs)


def _glu(cv, cfg):
    return cv[:, :cfg.CW] * _sigmoid(cv[:, cfg.CW:])


def _conv_post(y, cb, lw, lb):
    yf = y + cb
    mu = jnp.mean(yf, axis=-1, keepdims=True)
    var = jnp.mean(jnp.square(yf - mu), axis=-1, keepdims=True)
    return _silu((yf - mu) * lax.rsqrt(var + EPS_LN) * lw + lb)


def _readout(y, kbar, r, v, gd, r_k, gw2, lnx_w, lnx_b, e):
    inv = 1.0 / HEAD
    mu = _seg_sum(y, e) * inv
    yc = y - mu
    var = _seg_sum(yc * yc, e) * inv
    yn = yc * lax.rsqrt(var + EPS_GN) * lnx_w + lnx_b
    bonus = _seg_sum(r * kbar * r_k, e) * v
    g = _mm(_sigmoid(gd), gw2)
    return (yn + bonus) * g


def _post_res(xv, mix, gate, g):
    return xv + gate * _rms(mix, g)


def _head_spec(cfg, tmap):
    return pl.BlockSpec((1, cfg.HP, cfg.TT, LANES), lambda b, j: (b, 0, tmap(j), 0))


def _full_spec(shape):
    n = len(shape)
    return pl.BlockSpec(shape, lambda *_: (0,) * n)


def _to_heads(ref, val, cfg):
    for hp in range(cfg.HP):
        ref[0, hp] = val[:, hp * LANES:(hp + 1) * LANES]


def _from_heads(ref, cfg):
    return jnp.concatenate([ref[0, hp] for hp in range(cfg.HP)], axis=-1)


def _in_proj(xcat, modt, g1, w_in_p, cfg):
    B, T, D, TT, CP = cfg.B, cfg.T, cfg.D, cfg.TT, cfg.CP

    def body(x_ref, mod_ref, g_ref, w_ref, p_ref, h_ref):
        h = _rms_mod(x_ref[0], g_ref[...], mod_ref[0, 0, 0:1, :], mod_ref[0, 0, 1:2, :])
        hb = h.astype(BF16)
        h_ref[0] = hb
        p_ref[0] = jnp.dot(hb, w_ref[...], preferred_element_type=F32)

    return pl.pallas_call(
        body, name="in_proj", grid=(B, cfg.J),
        in_specs=[pl.BlockSpec((1, TT, D), lambda b, j: (b, j, 0)),
                  pl.BlockSpec((1, 1, 2, D), lambda b, j: (b, j, 0, 0)),
                  _full_spec((1, D)), _full_spec((D, CP))],
        out_specs=[pl.BlockSpec((1, TT, CP), lambda b, j: (b, j, 0)),
                   pl.BlockSpec((1, TT, D), lambda b, j: (b, j, 0))],
        out_shape=[jax.ShapeDtypeStruct((B, T, CP), F32), jax.ShapeDtypeStruct((B, T, D), BF16)],
        compiler_params=_params(("parallel", "parallel")),
    )(xcat, modt, g1, w_in_p)


def _halo_specs(cfg, width):
    per = cfg.TT // SUBLANES
    last = cfg.T // SUBLANES - 1
    prev = pl.BlockSpec((1, SUBLANES, width), lambda b, j: (b, jnp.maximum(j * per - 1, 0), 0))
    nxt = pl.BlockSpec((1, SUBLANES, width), lambda b, j: (b, jnp.minimum((j + 1) * per, last), 0))
    return prev, nxt


def _halo_flags(j, cfg):
    has_prev = jnp.logical_and(j != 0, j != cfg.JC).astype(F32)
    has_next = jnp.logical_and(j != cfg.JC - 1, j != cfg.J - 1).astype(F32)
    return has_prev, has_next


def _shifted(p_ref, prev_ref, next_ref, mup, mun, j, cfg):
    SP = cfg.SP
    has_prev, has_next = _halo_flags(j, cfg)
    z = p_ref[0][:, :SP]
    zp, zn = _shift_rows(z, prev_ref[0, SUBLANES - 1:SUBLANES, :] * has_prev, next_ref[0, 0:1, :] * has_next)
    return z, zp, zn, z + mup * (zp - z) + mun * (zn - z)


def _mix_prep(p, mup, mun, w0, w2p, a0, a2p, k_k, k_a, cw, cb, clw, clb, e_w, cfg):
    B, T, TT, SP, CP, W, CW, HP = cfg.B, cfg.T, cfg.TT, cfg.SP, cfg.CP, cfg.W, cfg.CW, cfg.HP

    def body(p_ref, prev_ref, next_ref, mup_ref, mun_ref, w0_ref, w2_ref, a0_ref, a2_ref, kk_ref, ka_ref,
             cw_ref, cb_ref, clw_ref, clb_ref, e_ref, *outs):
        j = pl.program_id(1)
        _, _, _, rw = _shifted(p_ref, prev_ref, next_ref, mup_ref[...], mun_ref[...], j, cfg)
        vals = _rwkv_prep(rw, w0_ref[...], w2_ref[...], a0_ref[...], a2_ref[...], kk_ref[...], ka_ref[...],
                          e_ref[...], cfg)
        for ref, val in zip(outs[:9], vals):
            _to_heads(ref, val, cfg)
        outs[9][0] = rw[:, 3 * W + 2 * cfg.PW:SP]
        u = _glu(p_ref[0][:, SP:], cfg)
        outs[10][0] = _conv_post(_dwconv(u, cw_ref[...], cfg.KC), cb_ref[...], clw_ref[...], clb_ref[...])

    prev, nxt = _halo_specs(cfg, SP)
    head = jax.ShapeDtypeStruct((B, HP, T, LANES), F32)
    return pl.pallas_call(
        body, name="mix_prep", grid=(B, cfg.J),
        in_specs=[pl.BlockSpec((1, TT, CP), lambda b, j: (b, j, 0)), prev, nxt,
                  _full_spec((1, SP)), _full_spec((1, SP)),
                  _full_spec((1, 2 * W)), _full_spec((cfg.PW, 2 * W)),
                  _full_spec((1, 2 * W)), _full_spec((cfg.PW, 2 * W)),
                  _full_spec((1, W)), _full_spec((1, W)),
                  _full_spec((cfg.KP, CW)), _full_spec((1, CW)), _full_spec((1, CW)), _full_spec((1, CW)),
                  _full_spec((W, W))],
        out_specs=[_head_spec(cfg, lambda j: j)] * 9
                  + [pl.BlockSpec((1, TT, cfg.GP), lambda b, j: (b, j, 0)),
                     pl.BlockSpec((1, TT, CW), lambda b, j: (b, j, 0))],
        out_shape=[head] * 9 + [jax.ShapeDtypeStruct((B, T, cfg.GP), F32),
                                jax.ShapeDtypeStruct((B, T, CW), F32)],
        compiler_params=_params(("parallel", "parallel")),
    )(p, p, p, mup, mun, w0, w2p, a0, a2p, k_k, k_a, cw, cb, clw, clb, e_w)


def _chunk_pos(c, reverse, cfg):
    if not reverse:
        return c
    return jnp.where(c < cfg.NCC, cfg.NCC - 1 - c, cfg.NCH - 1 + cfg.NCC - c)


def _diag_mask():
    r = lax.broadcasted_iota(jnp.int32, (HEAD, LANES), 0)
    l = lax.broadcasted_iota(jnp.int32, (HEAD, LANES), 1)
    return (r == l % HEAD).astype(F32)


def _col_lhs(row, diag_b):
    hi = row.astype(BF16)
    lo = (row - hi.astype(F32)).astype(BF16)
    return diag_b * hi, diag_b * lo


def _col_dot(row_list, diag_b, e2):
    n, g = len(row_list), row_list[0].shape[0]
    lhs = jnp.concatenate([jnp.concatenate(_col_lhs(r, diag_b), axis=-1) for r in row_list], axis=0)
    out = jnp.dot(lhs.reshape(n * g * HEAD, 2 * LANES), e2, preferred_element_type=F32)
    return out.reshape(n, g, HEAD, LANES)


def _col_form(row):
    n = row.shape[0]
    t = jnp.swapaxes(jnp.broadcast_to(row, (n, LANES, LANES)), 1, 2)
    lane = lax.broadcasted_iota(jnp.int32, (HEAD, LANES), 1)
    return jnp.where(lane < HEAD, t[:, :HEAD, :], t[:, HEAD:, :])


def _col_both(row, diag_b, e2):
    half = row.shape[0] // 2
    return jnp.concatenate([_col_form(row[:half]), _col_dot([row[half:]], diag_b, e2)[0]], axis=0)


def _both_rows(ins, idx, i):
    return jnp.concatenate([ins[d][idx][:, pl.ds(_tok(i, d == 1), 1), :] for d in range(2)], axis=0)


def _seg_dot(blocks, e):
    n, g = len(blocks), blocks[0].shape[0]
    lhs = jnp.concatenate(blocks, axis=0).reshape(n * g * HEAD, LANES)
    return jnp.dot(lhs, e, preferred_element_type=F32).reshape(n, g, HEAD, LANES)


def _tok(i, reverse):
    return (SCAN_CHUNK - 1 - i) if reverse else i


def _scan_fwd(ops_f, ops_b, e128, e256, cfg):
    G, T, NCH = cfg.G, cfg.T, cfg.NCH
    CH = SCAN_CHUNK
    G2 = 2 * G

    def body(*refs):
        ins = (refs[0:6], refs[6:12])
        e_ref, e2_ref = refs[12], refs[13]
        ys, hist_ref, fin_ref = (refs[14], refs[15]), refs[16], refs[17]
        s_ref, mm_ref = refs[18], refs[19]
        c = pl.program_id(0)

        @pl.when(c == 0)
        def _():
            s_ref[...] = jnp.zeros_like(s_ref)

        diag = _diag_mask()
        diag_b = diag.astype(BF16)
        e, e2 = e_ref[...], e2_ref[...]
        rows = functools.partial(_both_rows, ins)

        mm_ref[0] = _seg_dot([s_ref[...].astype(BF16) * (-rows(4, 0)).astype(BF16)], e)[0]
        mm_ref[1] = _col_both(rows(3, 0), diag_b, e2)

        def step(i, carry):
            nxt = jnp.minimum(i + 1, CH - 1)
            s_old = s_ref[...]
            hist_ref[i] = s_old
            S = s_old * rows(1, i) + mm_ref[0] * rows(5, i) + mm_ref[1] * rows(2, i)
            s_ref[...] = S
            sb = S.astype(BF16)
            res = _seg_dot([sb * rows(0, i).astype(BF16), sb * (-rows(4, nxt)).astype(BF16)], e)
            mm_ref[0] = res[1]
            mm_ref[1] = _col_both(rows(3, nxt), diag_b, e2)
            y = jnp.sum(diag * res[0], axis=1, keepdims=True)
            ys[0][:, pl.ds(_tok(i, False), 1), :] = y[:G]
            ys[1][:, pl.ds(_tok(i, True), 1), :] = y[G:]
            return carry

        lax.fori_loop(0, CH, step, 0)
        fin_ref[...] = s_ref[...]

    toks = [pl.BlockSpec((G, CH, LANES), lambda c, rev=rev: (0, _chunk_pos(c, rev, cfg), 0)) for rev in (False, True)]
    y_shape = jax.ShapeDtypeStruct((G, T, LANES), F32)
    return pl.pallas_call(
        body, name="scan_fwd", grid=(NCH,),
        in_specs=[toks[0]] * 6 + [toks[1]] * 6 + [_full_spec((LANES, LANES)), _full_spec((2 * LANES, LANES))],
        out_specs=[toks[0], toks[1], pl.BlockSpec((CH, G2, HEAD, LANES), lambda c: (c, 0, 0, 0)),
                   _full_spec((G2, HEAD, LANES))],
        out_shape=[y_shape, y_shape, jax.ShapeDtypeStruct(((NCH + 1) * CH, G2, HEAD, LANES), F32),
                   jax.ShapeDtypeStruct((G2, HEAD, LANES), F32)],
        scratch_shapes=[pltpu.VMEM((G2, HEAD, LANES), F32), pltpu.VMEM((2, G2, HEAD, LANES), F32)],
        compiler_params=_params(("arbitrary",)),
    )(*ops_f, *ops_b, e128, e256)


def _mix_out(yf, yb, kdf, kdb, r, v, gd, conv, x, mod2, r_k, gw2p, lnx_w, lnx_b, w_out, g2, e_w, cfg):
    B, TX, D, TT, W, CW, JC = cfg.B, cfg.TX, cfg.D, cfg.TT, cfg.W, cfg.CW, cfg.JC

    def body(yf_ref, yb_ref, kdf_ref, kdb_ref, r_ref, v_ref, gd_ref, cv_ref, x_ref, mod_ref,
             rk_ref, gw_ref, lw_ref, lb_ref, wo_ref, g_ref, e_ref, x1_ref):
        y = _from_heads(yf_ref, cfg) + _from_heads(yb_ref, cfg)
        kbar = 0.5 * (_from_heads(kdf_ref, cfg) + _from_heads(kdb_ref, cfg))
        ro = _readout(y, kbar, _from_heads(r_ref, cfg), _from_heads(v_ref, cfg), gd_ref[0], rk_ref[...],
                      gw_ref[...], lw_ref[...], lb_ref[...], e_ref[...])
        cat = jnp.concatenate([ro, cv_ref[0]], axis=-1)
        mix = _bdot(cat, wo_ref[...])
        x1_ref[0] = _post_res(x_ref[0], mix, mod_ref[0], g_ref[...])

    hs = _head_spec(cfg, lambda j: j + JC)
    lat = lambda n: pl.BlockSpec((1, TT, n), lambda b, j: (b, j + JC, 0))
    return pl.pallas_call(
        body, name="mix_out", grid=(B, cfg.JX),
        in_specs=[hs] * 6 + [lat(cfg.GP), lat(CW),
                             pl.BlockSpec((1, TT, D), lambda b, j: (b, j, 0)),
                             pl.BlockSpec((1, 1, D), lambda b, j: (b, 0, 0)),
                             _full_spec((1, W)), _full_spec((cfg.GP, W)), _full_spec((1, W)), _full_spec((1, W)),
                             _full_spec((W + CW, D)), _full_spec((1, D)), _full_spec((W, W))],
        out_specs=pl.BlockSpec((1, TT, D), lambda b, j: (b, j, 0)),
        out_shape=jax.ShapeDtypeStruct((B, TX, D), F32),
        compiler_params=_params(("parallel", "parallel")),
    )(yf, yb, kdf, kdb, r, v, gd, conv, x, mod2, r_k, gw2p, lnx_w, lnx_b, w_out, g2, e_w)


def _acc(ref, val, first):
    @pl.when(first)
    def _():
        ref[...] = val

    @pl.when(jnp.logical_not(first))
    def _():
        ref[...] += val


def _mlp_fwd_bwd(x1, tgt, mod345, g3, g4, w1, w2, cfg):
    B, TX, D, TT, F, JX = cfg.B, cfg.TX, cfg.D, cfg.TT, cfg.F, cfg.JX

    def body(x1_ref, t_ref, mod_ref, g3_ref, g4_ref, w1_ref, w2_ref,
             dx1_ref, loss_ref, h2_ref, dpre_ref, act_ref, dff_ref, dmod_ref, dg3_ref, dg4_ref):
        b, j = pl.program_id(0), pl.program_id(1)
        x1v = x1_ref[0]
        sh, sc, gt = mod_ref[0, 0:1, :], mod_ref[0, 1:2, :], mod_ref[0, 2:3, :]
        h2, vjp_pre = jax.vjp(_rms_mod, x1v, g3_ref[...], sh, sc)
        h2b = h2.astype(BF16)
        pre = jnp.dot(h2b, w1_ref[...], preferred_element_type=F32)
        rl = jnp.maximum(pre, 0.0)
        actb = (rl * rl).astype(BF16)
        ff = jnp.dot(actb, w2_ref[...], preferred_element_type=F32)
        x2, vjp_post = jax.vjp(_post_res, x1v, ff, gt, g4_ref[...])
        err = x2 - t_ref[0]
        loss = 0.5 * jnp.sum(jnp.mean(err * err, axis=-1, keepdims=True))
        dx1a, dff, dgt, dg4 = vjp_post(err * (1.0 / D))
        dffb = dff.astype(BF16)
        dpre = _bdot_nt(dffb, w2_ref[...]) * (2.0 * rl)
        dpreb = dpre.astype(BF16)
        dx1b, dg3, dsh, dsc = vjp_pre(_bdot_nt(dpreb, w1_ref[...]))
        dx1_ref[0] = dx1a + dx1b
        loss_ref[0, 0] = jnp.zeros((SUBLANES, LANES), F32) + loss
        h2_ref[0] = h2b
        dpre_ref[0] = dpreb
        act_ref[0] = actb
        dff_ref[0] = dffb
        _acc(dmod_ref, jnp.concatenate([dsh, dsc, dgt], axis=0)[None], j == 0)
        first = jnp.logical_and(b == 0, j == 0)
        _acc(dg3_ref, dg3, first)
        _acc(dg4_ref, dg4, first)

    tile = lambda n: pl.BlockSpec((1, TT, n), lambda b, j: (b, j, 0))
    return pl.pallas_call(
        body, name="mlp_fwd_bwd", grid=(B, JX),
        in_specs=[tile(D), tile(D), pl.BlockSpec((1, 3, D), lambda b, j: (b, 0, 0)),
                  _full_spec((1, D)), _full_spec((1, D)),
                  pl.BlockSpec((D, F), lambda b, j: (0, 0), pipeline_mode=pl.Buffered(1)),
                  pl.BlockSpec((F, D), lambda b, j: (0, 0), pipeline_mode=pl.Buffered(1))],
        out_specs=[tile(D), pl.BlockSpec((1, 1, SUBLANES, LANES), lambda b, j: (b, j, 0, 0)),
                   tile(D), tile(F), tile(F), tile(D),
                   pl.BlockSpec((1, 3, D), lambda b, j: (b, 0, 0)),
                   _full_spec((1, D)), _full_spec((1, D))],
        out_shape=[jax.ShapeDtypeStruct((B, TX, D), F32),
                   jax.ShapeDtypeStruct((B, JX, SUBLANES, LANES), F32),
                   jax.ShapeDtypeStruct((B, TX, D), BF16), jax.ShapeDtypeStruct((B, TX, F), BF16),
                   jax.ShapeDtypeStruct((B, TX, F), BF16), jax.ShapeDtypeStruct((B, TX, D), BF16),
                   jax.ShapeDtypeStruct((B, 3, D), F32),
                   jax.ShapeDtypeStruct((1, D), F32), jax.ShapeDtypeStruct((1, D), F32)],
        compiler_params=_params(("arbitrary", "arbitrary")),
    )(x1, tgt, mod345, g3, g4, w1, w2)


def _mix_out_bwd(yf, yb, kdf, kdb, r, v, gd, conv, x, mod2, r_k, gw2p, lnx_w, lnx_b, w_out, g2, e_w, dx1, cfg):
    B, TX, D, TT, W, CW, JC, HP, GP = cfg.B, cfg.TX, cfg.D, cfg.TT, cfg.W, cfg.CW, cfg.JC, cfg.HP, cfg.GP

    def body(yf_ref, yb_ref, kdf_ref, kdb_ref, r_ref, v_ref, gd_ref, cv_ref, x_ref, mod_ref,
             rk_ref, gw_ref, lw_ref, lb_ref, wo_ref, g_ref, e_ref, dx1_ref,
             dy_ref, dkb_ref, dr_ref, dv_ref, dgd_ref, dcv_ref, cat_ref, dmix_ref,
             dmod_ref, dg2_ref, drk_ref, dgw_ref, dlw_ref, dlb_ref):
        b, j = pl.program_id(0), pl.program_id(1)
        e = e_ref[...]
        y = _from_heads(yf_ref, cfg) + _from_heads(yb_ref, cfg)
        kbar = 0.5 * (_from_heads(kdf_ref, cfg) + _from_heads(kdb_ref, cfg))
        ro, vjp_ro = jax.vjp(lambda *a: _readout(*a, e), y, kbar, _from_heads(r_ref, cfg),
                             _from_heads(v_ref, cfg), gd_ref[0], rk_ref[...], gw_ref[...], lw_ref[...], lb_ref[...])
        catb = jnp.concatenate([ro, cv_ref[0]], axis=-1).astype(BF16)
        mix = jnp.dot(catb, wo_ref[...], preferred_element_type=F32)
        _, vjp_post = jax.vjp(_post_res, x_ref[0], mix, mod_ref[0], g_ref[...])
        _, dmix, dgate, dg2 = vjp_post(dx1_ref[0])
        dmixb = dmix.astype(BF16)
        dcat = _bdot_nt(dmixb, wo_ref[...])
        dy, dkb, dr, dv, dgd, drk, dgw, dlw, dlb = vjp_ro(dcat[:, :W])
        _to_heads(dy_ref, dy, cfg)
        _to_heads(dkb_ref, dkb, cfg)
        _to_heads(dr_ref, dr, cfg)
        _to_heads(dv_ref, dv, cfg)
        dgd_ref[0] = dgd
        dcv_ref[0] = dcat[:, W:]
        cat_ref[0] = catb
        dmix_ref[0] = dmixb
        _acc(dmod_ref, dgate[None], j == 0)
        first = jnp.logical_and(b == 0, j == 0)
        _acc(dg2_ref, dg2, first)
        _acc(drk_ref, drk, first)
        _acc(dgw_ref, dgw, first)
        _acc(dlw_ref, dlw, first)
        _acc(dlb_ref, dlb, first)

    hs = _head_spec(cfg, lambda j: j + JC)
    ho = _head_spec(cfg, lambda j: j)
    lat = lambda n: pl.BlockSpec((1, TT, n), lambda b, j: (b, j + JC, 0))
    tile = lambda n: pl.BlockSpec((1, TT, n), lambda b, j: (b, j, 0))
    head = jax.ShapeDtypeStruct((B, HP, TX, LANES), F32)
    vec = lambda n: jax.ShapeDtypeStruct((1, n), F32)
    return pl.pallas_call(
        body, name="mix_out_bwd", grid=(B, cfg.JX),
        in_specs=[hs] * 6 + [lat(GP), lat(CW), tile(D),
                             pl.BlockSpec((1, 1, D), lambda b, j: (b, 0, 0)),
                             _full_spec((1, W)), _full_spec((GP, W)), _full_spec((1, W)), _full_spec((1, W)),
                             _full_spec((W + CW, D)), _full_spec((1, D)), _full_spec((W, W)), tile(D)],
        out_specs=[ho] * 4 + [tile(GP), tile(CW), tile(W + CW), tile(D),
                              pl.BlockSpec((1, 1, D), lambda b, j: (b, 0, 0)),
                              _full_spec((1, D)), _full_spec((1, W)), _full_spec((GP, W)),
                              _full_spec((1, W)), _full_spec((1, W))],
        out_shape=[head] * 4 + [jax.ShapeDtypeStruct((B, TX, GP), F32), jax.ShapeDtypeStruct((B, TX, CW), F32),
                                jax.ShapeDtypeStruct((B, TX, W + CW), BF16), jax.ShapeDtypeStruct((B, TX, D), BF16),
                                jax.ShapeDtypeStruct((B, 1, D), F32),
                                vec(D), vec(W), jax.ShapeDtypeStruct((GP, W), F32), vec(W), vec(W)],
        compiler_params=_params(("arbitrary", "arbitrary")),
    )(yf, yb, kdf, kdb, r, v, gd, conv, x, mod2, r_k, gw2p, lnx_w, lnx_b, w_out, g2, e_w, dx1)


def _scan_bwd(ops_f, ops_b, dy, hist, e128, e256, cfg):
    G, T, NCH, NCC = cfg.G, cfg.T, cfg.NCH, cfg.NCC
    CH = SCAN_CHUNK
    G2 = 2 * G

    def body(*refs):
        ins = (refs[0:6], refs[6:12])
        dys, hist_ref, next_ref, e_ref, e2_ref = (refs[12], refs[13]), refs[14], refs[15], refs[16], refs[17]
        outs = (refs[18:24], refs[24:30])
        ds_ref, mm_ref = refs[30], refs[31]
        gi = pl.program_id(0)

        @pl.when(gi == 0)
        def _():
            ds_ref[...] = jnp.zeros_like(ds_ref)

        diag = _diag_mask()
        diag_b = diag.astype(BF16)
        e, e2 = e_ref[...], e2_ref[...]
        rows = functools.partial(_both_rows, ins)
        latent = [(_chunk_pos(NCH - 1 - gi, d == 1, cfg) >= NCC).astype(F32) for d in range(2)]

        def dy_rows(i):
            return jnp.concatenate([dys[d][:, pl.ds(_tok(i, d == 1), 1), :] * latent[d] for d in range(2)], axis=0)

        def put(idx, i, val):
            outs[0][idx][:, pl.ds(_tok(i, False), 1), :] = val[:G]
            outs[1][idx][:, pl.ds(_tok(i, True), 1), :] = val[G:]

        rsum = lambda z: jnp.sum(z, axis=1, keepdims=True)

        def prepare(i):
            return (_col_form(dy_rows(i)), _col_dot([rows(3, i)], diag_b, e2)[0],
                    hist_ref[i].astype(BF16) * (-rows(4, i)).astype(BF16))

        dyc0, vb0, sa_lhs = prepare(CH - 1)
        mm_ref[0] = dyc0
        mm_ref[1] = vb0
        mm_ref[2] = _seg_dot([sa_lhs], e)[0]

        def one_step(i, s_after):
            prv = jnp.maximum(i - 1, 0)
            sp, dyc = hist_ref[i], mm_ref[0]
            ds = ds_ref[...] + dyc * rows(0, i)
            put(0, i, rsum(s_after * dyc))
            put(1, i, rsum(ds * sp))
            put(5, i, rsum(ds * mm_ref[2]))
            put(2, i, rsum(ds * mm_ref[1]))
            dyc_n, vb_n, sa_lhs_n = prepare(prv)
            dsb = ds.astype(BF16)
            res = _seg_dot([dsb * rows(5, i).astype(BF16), dsb * rows(2, i).astype(BF16), sa_lhs_n], e)
            dsa = res[0]
            put(4, i, -rsum(sp * dsa))
            put(3, i, rsum(diag * res[1]))
            mm_ref[0] = dyc_n
            mm_ref[1] = vb_n
            mm_ref[2] = res[2]
            ds_ref[...] = ds * rows(1, i) - dsa * rows(4, i)

        one_step(CH - 1, next_ref[0])

        def bstep(ii, carry):
            i = CH - 1 - ii
            one_step(i, hist_ref[i + 1])
            return carry

        lax.fori_loop(1, CH, bstep, 0)

    cpos = lambda g, rev: _chunk_pos(NCH - 1 - g, rev, cfg)
    toks = [pl.BlockSpec((G, CH, LANES), lambda g, rev=rev: (0, cpos(g, rev), 0)) for rev in (False, True)]
    dy_specs = [pl.BlockSpec((G, CH, LANES), lambda g, rev=rev: (0, jnp.maximum(cpos(g, rev) - NCC, 0), 0))
                for rev in (False, True)]
    out = jax.ShapeDtypeStruct((G, T, LANES), F32)
    res = pl.pallas_call(
        body, name="scan_bwd", grid=(NCH,),
        in_specs=[toks[0]] * 6 + [toks[1]] * 6 + dy_specs
                 + [pl.BlockSpec((CH, G2, HEAD, LANES), lambda g: (NCH - 1 - g, 0, 0, 0)),
                    pl.BlockSpec((1, G2, HEAD, LANES), lambda g: ((NCH - g) * CH, 0, 0, 0)),
                    _full_spec((LANES, LANES)), _full_spec((2 * LANES, LANES))],
        out_specs=[toks[0]] * 6 + [toks[1]] * 6,
        out_shape=[out] * 12,
        scratch_shapes=[pltpu.VMEM((G2, HEAD, LANES), F32), pltpu.VMEM((3, G2, HEAD, LANES), F32)],
        compiler_params=_params(("arbitrary",)),
    )(*ops_f, *ops_b, dy, dy, hist, hist, e128, e256)
    return res[:6], res[6:]


def _mix_prep_bwd(p, mup, mun, w0, w2p, a0, a2p, k_k, k_a, cw, cb, clw, clb, e_w, sf, sb, ro, cfg):
    B, T, TT, SP, CP, W, CW, HP, JC, PW, GP, KC, KP = (cfg.B, cfg.T, cfg.TT, cfg.SP, cfg.CP, cfg.W, cfg.CW,
                                                       cfg.HP, cfg.JC, cfg.PW, cfg.GP, cfg.KC, cfg.KP)
    pad = KC // 2

    def body(p_ref, prev_ref, next_ref, mup_ref, mun_ref, w0_ref, w2_ref, a0_ref, a2_ref, kk_ref, ka_ref,
             cw_ref, cb_ref, clw_ref, clb_ref, e_ref, *rest):
        sf_refs, sb_refs = rest[0:6], rest[6:12]
        rdr_ref, rdv_ref, rdkb_ref, rdgd_ref, rdcv_ref = rest[12:17]
        (dpz_ref, dmup_ref, dmun_ref, dw0_ref, dw2_ref, da0_ref, da2_ref, dkk_ref, dka_ref,
         dcw_ref, dcb_ref, dclw_ref, dclb_ref) = rest[17:]
        b, j = pl.program_id(0), pl.program_id(1)
        first = jnp.logical_and(b == 0, j == 0)
        lat = (j >= JC).astype(F32)
        e = e_ref[...]
        mup_v, mun_v = mup_ref[...], mun_ref[...]
        z, zp, zn, rw = _shifted(p_ref, prev_ref, next_ref, mup_v, mun_v, j, cfg)

        def prep(rw_, w0_, w2_, a0_, a2_, kk_, ka_):
            return _rwkv_prep(rw_, w0_, w2_, a0_, a2_, kk_, ka_, e, cfg) + (rw_[:, 3 * W + 2 * PW:SP],)

        _, vjp_prep = jax.vjp(prep, rw, w0_ref[...], w2_ref[...], a0_ref[...], a2_ref[...], kk_ref[...], ka_ref[...])
        fr, fw, fk, fv, fkk, fb = [_from_heads(r_, cfg) for r_ in sf_refs]
        br, bw, bk, bv, bkk, bb = [_from_heads(r_, cfg) for r_ in sb_refs]
        half_kb = (0.5 * lat) * _from_heads(rdkb_ref, cfg)
        cots = (fr + br + lat * _from_heads(rdr_ref, cfg), fv + bv + lat * _from_heads(rdv_ref, cfg), fkk + bkk,
                fw, fk + half_kb, fb, bw, bk + half_kb, bb, lat * rdgd_ref[0])
        drw, dw0, dw2, da0, da2, dkk, dka = vjp_prep(cots)
        _acc(dmup_ref, jnp.sum(drw * (zp - z), axis=0, keepdims=True), first)
        _acc(dmun_ref, jnp.sum(drw * (zn - z), axis=0, keepdims=True), first)
        for ref, val in ((dw0_ref, dw0), (dw2_ref, dw2), (da0_ref, da0), (da2_ref, da2), (dkk_ref, dkk), (dka_ref, dka)):
            _acc(ref, val, first)

        cvv = p_ref[0, :, SP:]
        u, vjp_glu = jax.vjp(lambda c_: _glu(c_, cfg), cvv)
        cwv = cw_ref[...]
        _, vjp_post = jax.vjp(_conv_post, _dwconv(u, cwv, KC), cb_ref[...], clw_ref[...], clb_ref[...])
        dyc, dcb, dclw, dclb = vjp_post(lat * rdcv_ref[0])
        du = jnp.zeros_like(u)
        for i in range(KC):
            du = du + _line_shift(dyc, pad - i) * cwv[i:i + 1, :]
        (dcv,) = vjp_glu(du)
        dpz_ref[0, :, 0:SP] = drw
        dpz_ref[0, :, SP:] = dcv

        @pl.when(first)
        def _():
            dcw_ref[...] = jnp.zeros_like(dcw_ref)

        for i in range(KC):
            dcw_ref[i:i + 1, :] += jnp.sum(dyc * _line_shift(u, i - pad), axis=0, keepdims=True)
        _acc(dcb_ref, dcb, first)
        _acc(dclw_ref, dclw, first)
        _acc(dclb_ref, dclb, first)

    prev, nxt = _halo_specs(cfg, SP)
    hs = _head_spec(cfg, lambda j: j)
    hl = _head_spec(cfg, lambda j: jnp.maximum(j - JC, 0))
    latn = lambda n: pl.BlockSpec((1, TT, n), lambda b, j: (b, jnp.maximum(j - JC, 0), 0))
    vec = lambda n: jax.ShapeDtypeStruct((1, n), F32)
    small_shapes = [vec(SP), vec(SP), vec(2 * W), jax.ShapeDtypeStruct((PW, 2 * W), F32), vec(2 * W),
                    jax.ShapeDtypeStruct((PW, 2 * W), F32), vec(W), vec(W),
                    jax.ShapeDtypeStruct((KP, CW), F32), vec(CW), vec(CW), vec(CW)]
    return pl.pallas_call(
        body, name="mix_prep_bwd", grid=(B, cfg.J),
        in_specs=[pl.BlockSpec((1, TT, CP), lambda b, j: (b, j, 0)), prev, nxt,
                  _full_spec((1, SP)), _full_spec((1, SP)),
                  _full_spec((1, 2 * W)), _full_spec((PW, 2 * W)),
                  _full_spec((1, 2 * W)), _full_spec((PW, 2 * W)),
                  _full_spec((1, W)), _full_spec((1, W)),
                  _full_spec((KP, CW)), _full_spec((1, CW)), _full_spec((1, CW)), _full_spec((1, CW)),
                  _full_spec((W, W))] + [hs] * 12 + [hl] * 3 + [latn(GP), latn(CW)],
        out_specs=[pl.BlockSpec((1, TT, CP), lambda b, j: (b, j, 0))] + [_full_spec(s.shape) for s in small_shapes],
        out_shape=[jax.ShapeDtypeStruct((B, T, CP), F32)] + small_shapes,
        compiler_params=_params(("arbitrary", "arbitrary")),
    )(p, p, p, mup, mun, w0, w2p, a0, a2p, k_k, k_a, cw, cb, clw, clb, e_w, *sf, *sb, *ro)


def _in_proj_bwd(dpz, xcat, modt, g1, w_in_p, mup, mun, dx1, cfg):
    B, T, TX, D, TT, SP, CP, JC = cfg.B, cfg.T, cfg.TX, cfg.D, cfg.TT, cfg.SP, cfg.CP, cfg.JC

    def body(d_ref, prev_ref, next_ref, x_ref, mod_ref, g_ref, w_ref, mup_ref, mun_ref, dx1_ref,
             gx_ref, dp_ref, dmod_ref, dg_ref):
        b, j = pl.program_id(0), pl.program_id(1)
        has_prev, has_next = _halo_flags(j, cfg)
        mp, mn = mup_ref[...], mun_ref[...]
        drw = d_ref[0, :, 0:SP]
        dprev, dnext = _shift_rows(drw, prev_ref[0, SUBLANES - 1:SUBLANES, :] * has_prev,
                                   next_ref[0, 0:1, :] * has_next)
        dz = drw * (1.0 - mp - mn) + mp * dnext + mn * dprev
        dpb = jnp.concatenate([dz, d_ref[0, :, SP:]], axis=-1).astype(BF16)
        dp_ref[0] = dpb
        dh = _bdot_nt(dpb, w_ref[...])
        _, vjp_h = jax.vjp(_rms_mod, x_ref[0], g_ref[...], mod_ref[0, 0, 0:1, :], mod_ref[0, 0, 1:2, :])
        dx, dg, dsh, dsc = vjp_h(dh)
        dmod_ref[0, 0] = jnp.concatenate([dsh, dsc], axis=0)
        _acc(dg_ref, dg, jnp.logical_and(b == 0, j == 0))

        @pl.when(j >= JC)
        def _():
            gx_ref[0] = dx + dx1_ref[0]

    prev, nxt = _halo_specs(cfg, SP)
    lat = pl.BlockSpec((1, TT, D), lambda b, j: (b, jnp.maximum(j - JC, 0), 0))
    return pl.pallas_call(
        body, name="in_proj_bwd", grid=(B, cfg.J),
        in_specs=[pl.BlockSpec((1, TT, CP), lambda b, j: (b, j, 0)), prev, nxt,
                  pl.BlockSpec((1, TT, D), lambda b, j: (b, j, 0)),
                  pl.BlockSpec((1, 1, 2, D), lambda b, j: (b, j, 0, 0)),
                  _full_spec((1, D)), _full_spec((D, CP)), _full_spec((1, SP)), _full_spec((1, SP)), lat],
        out_specs=[lat, pl.BlockSpec((1, TT, CP), lambda b, j: (b, j, 0)),
                   pl.BlockSpec((1, 1, 2, D), lambda b, j: (b, j, 0, 0)), _full_spec((1, D))],
        out_shape=[jax.ShapeDtypeStruct((B, TX, D), F32), jax.ShapeDtypeStruct((B, T, CP), BF16),
                   jax.ShapeDtypeStruct((B, cfg.J, 2, D), F32), jax.ShapeDtypeStruct((1, D), F32)],
        compiler_params=_params(("arbitrary", "arbitrary")),
    )(dpz, dpz, dpz, xcat, modt, g1, w_in_p, mup, mun, dx1)


def _pick_tile(n, pref):
    for t in pref:
        if n % t == 0:
            return t
    return n


def _grad_matmul(a, g, name):
    K, M = a.shape
    N = g.shape[1]
    tm = _pick_tile(M, (512, 256, 128))
    tn = _pick_tile(N, (1024, 768, 512, 256, 128))
    tk = _pick_tile(K, (512, 256, 128, 64))
    nk = K // tk

    def body(a_ref, g_ref, o_ref):
        k = pl.program_id(2)
        _acc(o_ref, _bdot_tn(a_ref[...], g_ref[...]), k == 0)

    return pl.pallas_call(
        body, name=name, grid=(M // tm, N // tn, nk),
        in_specs=[pl.BlockSpec((tk, tm), lambda i, j, k: (k, i)),
                  pl.BlockSpec((tk, tn), lambda i, j, k: (k, j))],
        out_specs=pl.BlockSpec((tm, tn), lambda i, j, k: (i, j)),
        out_shape=jax.ShapeDtypeStruct((M, N), F32),
        compiler_params=_params(("parallel", "parallel", "arbitrary")),
    )(a, g)


def _ada_fwd(crows, ada_w, ada_b):
    D = crows.shape[1]
    n6 = ada_w.shape[1]
    tn = _pick_tile(n6, (1024, 512, 256, 128))

    def body(c_ref, w_ref, b_ref, s_ref, m_ref):
        s = _silu(c_ref[...])
        s_ref[...] = s
        m_ref[...] = _bdot(s, w_ref[...]) + b_ref[...]

    return pl.pallas_call(
        body, name="ada_fwd", grid=(n6 // tn,),
        in_specs=[_full_spec((SUBLANES, D)), pl.BlockSpec((D, tn), lambda i: (0, i)),
                  pl.BlockSpec((1, tn), lambda i: (0, i))],
        out_specs=[_full_spec((SUBLANES, D)), pl.BlockSpec((SUBLANES, tn), lambda i: (0, i))],
        out_shape=[jax.ShapeDtypeStruct((SUBLANES, D), F32), jax.ShapeDtypeStruct((SUBLANES, n6), F32)],
        compiler_params=_params(("arbitrary",)),
    )(crows, ada_w, ada_b)


def _ada_bwd(s_all, g_all, g_mine, c_ctx, ada_w, nb):
    D = s_all.shape[1]
    n6 = g_all.shape[1]
    ns = g_mine.shape[1]

    def body(s_ref, g_ref, gm_ref, c_ref, w_ref, dw_ref, db_ref, dc_ref):
        g = g_ref[...]
        dw_ref[...] = _bdot_tn(s_ref[...], gm_ref[...])
        db_ref[...] = jnp.sum(g, axis=0, keepdims=True)
        rows = lax.broadcasted_iota(jnp.int32, (g.shape[0], 1), 0)
        gc = jnp.sum(jnp.where(rows % SUBLANES == nb, g, 0.0), axis=0, keepdims=True)
        ds = _bdot_nt(gc, w_ref[...])
        c = c_ref[...]
        sg = _sigmoid(c)
        dc_ref[...] = ds * (sg + c * sg * (1.0 - sg))

    return pl.pallas_call(
        body, name="ada_bwd",
        out_shape=[jax.ShapeDtypeStruct((D, ns), F32), jax.ShapeDtypeStruct((1, n6), F32),
                   jax.ShapeDtypeStruct((1, D), F32)],
        compiler_params=_params(),
    )(s_all, g_all, g_mine, c_ctx, ada_w)


def _adamw(parts, w, m, v, name):
    P, R, C = parts.shape
    tr = _pick_tile(R, (256, 128, 64, 32, 16, 8))

    def body(p_ref, w_ref, m_ref, v_ref, g_ref, d_ref, nm_ref, nv_ref):
        g = p_ref[0].astype(F32)
        for i in range(1, P):
            g = g + p_ref[i].astype(F32)
        nm = ADAM_B1 * m_ref[...] + (1.0 - ADAM_B1) * g
        nv = ADAM_B2 * v_ref[...] + (1.0 - ADAM_B2) * (g * g)
        m_hat = nm / (1.0 - ADAM_B1 ** ADAM_STEP)
        v_hat = nv / (1.0 - ADAM_B2 ** ADAM_STEP)
        g_ref[...] = g
        d_ref[...] = -ADAM_LR * (m_hat / (jnp.sqrt(v_hat) + ADAM_EPS) + ADAM_WD * w_ref[...])
        nm_ref[...] = nm
        nv_ref[...] = nv

    blk = pl.BlockSpec((tr, C), lambda i: (i, 0))
    out = jax.ShapeDtypeStruct((R, C), F32)
    return pl.pallas_call(
        body, name=name, grid=(R // tr,),
        in_specs=[pl.BlockSpec((P, tr, C), lambda i: (0, i, 0)), blk, blk, blk],
        out_specs=[blk] * 4, out_shape=[out] * 4,
        compiler_params=_params(("parallel",)),
    )(parts, w, m, v)


def _local_step(cfg, x, c, ctx, tgt, fw):
    B, D, W, CW, JC, T, TX = cfg.B, cfg.D, cfg.W, cfg.CW, cfg.JC, cfg.T, cfg.TX
    e_w = _block_ones(W)
    e128 = _block_ones(LANES)
    e256 = jnp.concatenate([e128, e128], axis=0)
    row = lambda a: a.reshape(1, -1)

    ada_wb = fw["ada_w"].astype(BF16)
    w_in_p = _pad_cols(fw["w_in"], cfg).astype(BF16)
    mup = _pad_cols(fw["mu_prev"], cfg, True)
    mun = _pad_cols(fw["mu_next"], cfg, True)
    w0, a0 = row(fw["decay_w0"]), row(fw["iclr_a0"])
    w2p, a2p = _pair_weight(fw["decay_w2"], cfg), _pair_weight(fw["iclr_a2"], cfg)
    cw = jnp.pad(fw["conv_w"], ((0, cfg.KP - cfg.KC), (0, 0)))
    gw2p = jnp.pad(fw["gate_w2"], ((0, cfg.GP - cfg.GR), (0, 0)))
    r_k = row(fw["r_k"])
    w_outb, w1b, w2b = fw["w_out"].astype(BF16), fw["mlp_w1"].astype(BF16), fw["mlp_w2"].astype(BF16)

    crows = jnp.concatenate([c, fw["c_ctx"], jnp.zeros((SUBLANES - B - 1, D), F32)], axis=0)
    s_rows, mods = _ada_fwd(crows, ada_wb, fw["ada_b"])
    mod_x = mods[:B].reshape(B, 6, D)
    mod_c = mods[B].reshape(6, D)
    modt = jnp.concatenate([jnp.broadcast_to(mod_c[None, None, 0:2], (B, JC, 2, D)),
                            jnp.broadcast_to(mod_x[:, None, 0:2], (B, cfg.JX, 2, D))], axis=1)
    mod2, mod345 = mod_x[:, 2:3], mod_x[:, 3:6]

    xcat = jnp.concatenate([ctx, x], axis=1)
    p, hb = _in_proj(xcat, modt, fw["mix_pre_g"], w_in_p, cfg)
    prep_w = (mup, mun, w0, w2p, a0, a2p, fw["k_k"], fw["k_a"], cw, fw["conv_b"], fw["conv_ln_w"],
              fw["conv_ln_b"], e_w)
    r, v, kk, w_f, kd_f, b_f, w_b, kd_b, b_b, gd, conv = _mix_prep(p, *prep_w, cfg)
    flat = lambda a: a.reshape(cfg.G, a.shape[2], LANES)
    heads = lambda a: a.reshape(B, cfg.HP, a.shape[1], LANES)
    ops_f = tuple(flat(a) for a in (r, w_f, kd_f, v, kk, b_f))
    ops_b = tuple(flat(a) for a in (r, w_b, kd_b, v, kk, b_b))
    y_f, y_b, hist, s_fin = _scan_fwd(ops_f, ops_b, e128, e256, cfg)
    hist = lax.dynamic_update_slice_in_dim(hist, s_fin[None], cfg.NCH * SCAN_CHUNK, axis=0)
    out_args = (heads(y_f), heads(y_b), kd_f, kd_b, r, v, gd, conv, x, mod2, r_k, gw2p, fw["lnx_w"], fw["lnx_b"],
                w_outb, fw["mix_post_g"], e_w)
    x1 = _mix_out(*out_args, cfg)

    dx1, loss_t, h2b, dpreb, actb, dffb, dmod345, dg3, dg4 = _mlp_fwd_bwd(
        x1, tgt, mod345, fw["mlp_pre_g"], fw["mlp_post_g"], w1b, w2b, cfg)
    (dy, dkb, dr_c, dv_c, dgd, dconv, catb, dmixb, dmod2, dg2, drk, dgw, dlw, dlb) = _mix_out_bwd(
        *out_args, dx1, cfg)
    sf, sb = _scan_bwd(ops_f, ops_b, flat(dy), hist, e128, e256, cfg)
    (dpz, dmup, dmun, dw0, dw2p, da0, da2p, dkk, dka, dcw, dcb, dclw, dclb) = _mix_prep_bwd(
        p, *prep_w, [heads(a) for a in sf], [heads(a) for a in sb], (dr_c, dv_c, dkb, dgd, dconv), cfg)
    grad_x, dpb, dmodt, dg1 = _in_proj_bwd(dpz, xcat, modt, fw["mix_pre_g"], w_in_p, mup, mun, dx1, cfg)

    tokens = lambda a: a.reshape(-1, a.shape[-1])
    d_w_in = _grad_matmul(tokens(hb), tokens(dpb), "grad_w_in")
    d_w_out = _grad_matmul(tokens(catb), tokens(dmixb), "grad_w_out")
    d_w1 = _grad_matmul(tokens(h2b), tokens(dpreb), "grad_mlp_w1")
    d_w2 = _grad_matmul(tokens(actb), tokens(dffb), "grad_mlp_w2")

    grads = {
        "mix_pre_g": dg1, "mix_post_g": dg2, "mlp_pre_g": dg3, "mlp_post_g": dg4,
        "w_in": _unpad_cols(d_w_in, cfg),
        "mu_prev": _unpad_cols(dmup, cfg, True), "mu_next": _unpad_cols(dmun, cfg, True),
        "decay_w0": dw0.reshape(2, W), "decay_w2": _unpair_weight(dw2p, cfg),
        "iclr_a0": da0.reshape(2, W), "iclr_a2": _unpair_weight(da2p, cfg),
        "k_k": dkk, "k_a": dka, "r_k": drk.reshape(fw["r_k"].shape),
        "gate_w2": dgw[:cfg.GR], "lnx_w": dlw, "lnx_b": dlb,
        "conv_w": dcw[:cfg.KC], "conv_b": dcb, "conv_ln_w": dclw, "conv_ln_b": dclb,
        "w_out": d_w_out, "mlp_w1": d_w1, "mlp_w2": d_w2,
    }
    dmod_x = jnp.concatenate([jnp.sum(dmodt[:, JC:], axis=1), dmod2, dmod345], axis=1).reshape(B, 6 * D)
    dmod_c = jnp.concatenate([jnp.sum(dmodt[:, :JC], axis=(0, 1)), jnp.zeros((4, D), F32)], axis=0).reshape(1, 6 * D)
    g_rows = jnp.concatenate([dmod_x, dmod_c, jnp.zeros((SUBLANES - B - 1, 6 * D), F32)], axis=0)
    return loss_t, grad_x, grads, s_rows, g_rows


def _my_index():
    return 4 * lax.axis_index("x") + 2 * lax.axis_index("y") + lax.axis_index("c")


def _exchange(arrays, scatter, name):
    n = len(arrays)
    out_shape = [jax.ShapeDtypeStruct(a.shape if s else (N_DEV,) + a.shape, a.dtype)
                 for a, s in zip(arrays, scatter)]

    def body(*refs):
        ins, outs = refs[:n], refs[n:2 * n]
        send_sems, recv_sems, local_sems = refs[2 * n:]
        x, y, c = lax.axis_index("x"), lax.axis_index("y"), lax.axis_index("c")
        me = 4 * x + 2 * y + c
        flip = lambda v, f: 1 - v if f else v

        def piece(a, dest):
            return ins[a].at[dest] if scatter[a] else ins[a]

        local = [pltpu.make_async_copy(piece(a, me), outs[a].at[me], local_sems.at[a]) for a in range(n)]
        for cp in local:
            cp.start()
        sends, recvs = [], []
        for k in range(1, N_DEV):
            fx, fy, fc = (k >> 2) & 1, (k >> 1) & 1, k & 1
            peer = (flip(x, fx), flip(y, fy), flip(c, fc))
            peer_idx = 4 * peer[0] + 2 * peer[1] + peer[2]
            for a in range(n):
                sends.append(pltpu.make_async_remote_copy(
                    src_ref=piece(a, peer_idx), dst_ref=outs[a].at[me],
                    send_sem=send_sems.at[k - 1, a], recv_sem=recv_sems.at[k - 1, a],
                    device_id=peer, device_id_type=pl.DeviceIdType.MESH))
                recvs.append(pltpu.make_async_remote_copy(
                    src_ref=piece(a, peer_idx), dst_ref=outs[a].at[peer_idx],
                    send_sem=send_sems.at[k - 1, a], recv_sem=recv_sems.at[k - 1, a],
                    device_id=peer, device_id_type=pl.DeviceIdType.MESH))
        for cp in sends:
            cp.start()
        for cp in recvs:
            cp.wait_recv()
        for cp in sends:
            cp.wait_send()
        for cp in local:
            cp.wait()

    hbm = pl.BlockSpec(memory_space=pltpu.HBM)
    return pl.pallas_call(
        body, name=name, out_shape=out_shape,
        in_specs=[hbm] * n, out_specs=[hbm] * n,
        scratch_shapes=[pltpu.SemaphoreType.DMA((N_DEV - 1, n)), pltpu.SemaphoreType.DMA((N_DEV - 1, n)),
                        pltpu.SemaphoreType.DMA((n,))],
    )(*arrays)


def _pack(parts):
    flat = jnp.concatenate([p.reshape(-1) for p in parts])
    total = _round_up(flat.shape[0], SUBLANES * LANES)
    return jnp.pad(flat, (0, total - flat.shape[0])).reshape(-1, LANES)


def _unpack(buf, shapes):
    flat = buf.reshape(-1)
    out, pos = [], 0
    for s in shapes:
        n = int(np.prod(s))
        out.append(flat[pos:pos + n].reshape(s))
        pos += n
    return out


_SHARDED_SMALL = ("decay_w0", "decay_w2", "iclr_a0", "iclr_a2", "gate_w2", "conv_w")
_REPLICATED = ("mix_pre_g", "mix_post_g", "mlp_pre_g", "mlp_post_g", "mu_prev", "mu_next", "k_k", "k_a", "r_k",
               "lnx_w", "lnx_b", "conv_b", "conv_ln_w", "conv_ln_b")
_ADA_SMALL = ("c_ctx", "ada_b")
_WEIGHTS = ("c_ctx", "ada_w", "ada_b", "mix_pre_g", "mix_post_g", "mlp_pre_g", "mlp_post_g", "w_in", "mu_prev",
            "mu_next", "decay_w0", "decay_w2", "iclr_a0", "iclr_a2", "k_k", "k_a", "r_k", "gate_w2", "lnx_w", "lnx_b",
            "conv_w", "conv_b", "conv_ln_w", "conv_ln_b", "w_out", "mlp_w1", "mlp_w2")
_INPUTS = ("x", "c", "ctx") + _WEIGHTS + ("loss_target",) + tuple("m_" + n for n in _WEIGHTS) + tuple(
    "v_" + n for n in _WEIGHTS)


def _cols_to_blocks(a):
    a = a.reshape(a.shape[:-1] + (N_DEV, a.shape[-1] // N_DEV))
    return jnp.moveaxis(a, -2, 0)


def _blocks_to_cols(a):
    a = jnp.moveaxis(a, 0, -2)
    return a.reshape(a.shape[:-2] + (a.shape[-2] * a.shape[-1],))


def kernel(x, c, ctx, c_ctx, ada_w, ada_b, mix_pre_g, mix_post_g, mlp_pre_g, mlp_post_g, w_in, mu_prev, mu_next, decay_w0, decay_w2, iclr_a0, iclr_a2, k_k, k_a, r_k, gate_w2, lnx_w, lnx_b, conv_w, conv_b, conv_ln_w, conv_ln_b, w_out, mlp_w1, mlp_w2, loss_target, m_c_ctx, m_ada_w, m_ada_b, m_mix_pre_g, m_mix_post_g, m_mlp_pre_g, m_mlp_post_g, m_w_in, m_mu_prev, m_mu_next, m_decay_w0, m_decay_w2, m_iclr_a0, m_iclr_a2, m_k_k, m_k_a, m_r_k, m_gate_w2, m_lnx_w, m_lnx_b, m_conv_w, m_conv_b, m_conv_ln_w, m_conv_ln_b, m_w_out, m_mlp_w1, m_mlp_w2, v_c_ctx, v_ada_w, v_ada_b, v_mix_pre_g, v_mix_post_g, v_mlp_pre_g, v_mlp_post_g, v_w_in, v_mu_prev, v_mu_next, v_decay_w0, v_decay_w2, v_iclr_a0, v_iclr_a2, v_k_k, v_k_a, v_r_k, v_gate_w2, v_lnx_w, v_lnx_b, v_conv_w, v_conv_b, v_conv_ln_w, v_conv_ln_b, v_w_out, v_mlp_w1, v_mlp_w2):
    given = dict(zip(_INPUTS, (x, c, ctx, c_ctx, ada_w, ada_b, mix_pre_g, mix_post_g, mlp_pre_g, mlp_post_g, w_in, mu_prev, mu_next, decay_w0, decay_w2, iclr_a0, iclr_a2, k_k, k_a, r_k, gate_w2, lnx_w, lnx_b, conv_w, conv_b, conv_ln_w, conv_ln_b, w_out, mlp_w1, mlp_w2, loss_target, m_c_ctx, m_ada_w, m_ada_b, m_mix_pre_g, m_mix_post_g, m_mlp_pre_g, m_mlp_post_g, m_w_in, m_mu_prev, m_mu_next, m_decay_w0, m_decay_w2, m_iclr_a0, m_iclr_a2, m_k_k, m_k_a, m_r_k, m_gate_w2, m_lnx_w, m_lnx_b, m_conv_w, m_conv_b, m_conv_ln_w, m_conv_ln_b, m_w_out, m_mlp_w1, m_mlp_w2, v_c_ctx, v_ada_w, v_ada_b, v_mix_pre_g, v_mix_post_g, v_mlp_pre_g, v_mlp_post_g, v_w_in, v_mu_prev, v_mu_next, v_decay_w0, v_decay_w2, v_iclr_a0, v_iclr_a2, v_k_k, v_k_a, v_r_k, v_gate_w2, v_lnx_w, v_lnx_b, v_conv_w, v_conv_b, v_conv_ln_w, v_conv_ln_b, v_w_out, v_mlp_w1, v_mlp_w2)))
    loc = {}
    for pre in ("", "m_", "v_"):
        for n in _WEIGHTS:
            a = given[pre + n]
            a = a.reshape(1, -1) if n == "c_ctx" else a[0]
            loc[pre + n] = a.reshape(1, -1) if a.ndim == 1 else a
    B, TX, D = x.shape
    W, CW = loc["k_k"].shape[1], loc["conv_b"].shape[1]
    cfg = _Cfg(B, TX, ctx.shape[1], D, W, CW, loc["decay_w2"].shape[1], loc["gate_w2"].shape[0],
               loc["conv_w"].shape[0], loc["mlp_w1"].shape[1] * N_DEV)
    me = _my_index()

    small_shapes = [loc[n].shape for n in _SHARDED_SMALL]
    got = _exchange(
        [loc["ada_w"].astype(BF16), loc["w_in"].astype(BF16), loc["w_out"].astype(BF16),
         loc["mlp_w1"].astype(BF16), loc["mlp_w2"].astype(BF16), _pack([loc[n] for n in _SHARDED_SMALL])],
        [False] * 6, "gather_weights")
    fw = {n: loc[n] for n in _REPLICATED + _ADA_SMALL}
    fw["ada_w"] = _blocks_to_cols(got[0])
    fw["w_in"] = _blocks_to_cols(got[1])
    fw["w_out"] = got[2].reshape(-1, D)
    fw["mlp_w1"] = _blocks_to_cols(got[3])
    fw["mlp_w2"] = got[4].reshape(-1, D)
    per_dev = [_unpack(got[5][i], small_shapes) for i in range(N_DEV)]
    for j, n in enumerate(_SHARDED_SMALL):
        fw[n] = jnp.concatenate([per_dev[i][j] for i in range(N_DEV)], axis=-1)

    loss_t, grad_x, grads, s_rows, g_rows = _local_step(cfg, x, c, ctx, loss_target, fw)
    loss = lax.psum(jnp.sum(loss_t[:, :, 0, 0]), ("x", "y", "c"))

    small_blocks = jnp.stack([_pack([_cols_to_blocks(grads[n])[i] for n in _SHARDED_SMALL]) for i in range(N_DEV)])
    sent = _exchange(
        [_cols_to_blocks(grads["w_in"]).astype(BF16), grads["w_out"].reshape(N_DEV, -1, D).astype(BF16),
         _cols_to_blocks(grads["mlp_w1"]).astype(BF16), grads["mlp_w2"].reshape(N_DEV, -1, D).astype(BF16),
         small_blocks,
         _pack([grads[n] for n in _REPLICATED]), s_rows, g_rows],
        [True] * 5 + [False] * 3, "exchange_grads")
    s_all = sent[6].reshape(N_DEV * SUBLANES, D)
    g_all = sent[7].reshape(N_DEV * SUBLANES, 6 * D)
    ns = 6 * D // N_DEV
    g_mine = lax.dynamic_slice_in_dim(g_all, me * ns, ns, axis=1)
    d_ada_w, d_ada_b, d_c_ctx = _ada_bwd(s_all, g_all, g_mine, loc["c_ctx"], fw["ada_w"], B)

    res = {}

    def update(name, parts):
        res[name] = _adamw(parts, loc[name], loc["m_" + name], loc["v_" + name], "adamw_" + name)

    update("w_in", sent[0])
    update("w_out", sent[1])
    update("mlp_w1", sent[2])
    update("mlp_w2", sent[3])
    update("ada_w", d_ada_w[None])

    def update_packed(names, parts, tag):
        shapes = [loc[n].shape for n in names]
        packed = _adamw(parts, *[_pack([loc[pre + n] for n in names]) for pre in ("", "m_", "v_")], "adamw_" + tag)
        unpacked = [_unpack(p, shapes) for p in packed]
        for j, n in enumerate(names):
            res[n] = tuple(u[j] for u in unpacked)

    update_packed(_SHARDED_SMALL, sent[4], "sharded_small")
    update_packed(_REPLICATED, sent[5], "replicated")
    update_packed(_ADA_SMALL, _pack([d_c_ctx, d_ada_b])[None], "ada_small")

    outs = [loss, grad_x]
    for k in range(4):
        for n in _WEIGHTS:
            outs.append(res[n][k].reshape(given[n].shape))
    return tuple(outs)
```

```python
import functools

import numpy as np
import jax
import jax.numpy as jnp
from jax import lax
from jax.experimental import pallas as pl
from jax.experimental.pallas import tpu as pltpu

F32 = jnp.float32
BF16 = jnp.bfloat16

EPS_RMS = 1e-6
EPS_LN = 1e-5
EPS_GN = 64e-5
LINE = 64
HEAD = 64
LANES = 128
SUBLANES = 8
SCAN_CHUNK = 16
N_DEV = 8
VMEM_LIMIT = 56 * 1024 * 1024

ADAM_LR = 0.001
ADAM_B1 = 0.9
ADAM_B2 = 0.999
ADAM_EPS = 1e-08
ADAM_WD = 0.01
ADAM_STEP = 10


def _round_up(n, m):
    return (n + m - 1) // m * m


def _params(semantics=None, vmem=VMEM_LIMIT):
    return pltpu.CompilerParams(dimension_semantics=semantics, vmem_limit_bytes=vmem)


def _bdot(a, b):
    return jnp.dot(a.astype(BF16), b.astype(BF16), preferred_element_type=F32)


def _bdot_nt(a, b):
    return lax.dot_general(a.astype(BF16), b.astype(BF16), (((1,), (1,)), ((), ())),
                           preferred_element_type=F32)


def _bdot_tn(a, b):
    return lax.dot_general(a.astype(BF16), b.astype(BF16), (((0,), (0,)), ((), ())),
                           preferred_element_type=F32)


@jax.custom_vjp
def _mm(a, b):
    return _bdot(a, b)


def _mm_fwd(a, b):
    return _bdot(a, b), (a, b)


def _mm_bwd(res, g):
    a, b = res
    return _bdot_nt(g, b), _bdot_tn(a, g)


_mm.defvjp(_mm_fwd, _mm_bwd)


def _seg_sum_raw(x, e):
    hi = x.astype(BF16)
    lo = (x - hi.astype(F32)).astype(BF16)
    return (jnp.dot(hi, e, preferred_element_type=F32)
            + jnp.dot(lo, e, preferred_element_type=F32))


@jax.custom_vjp
def _seg_sum(x, e):
    return _seg_sum_raw(x, e)


def _seg_sum_fwd(x, e):
    return _seg_sum_raw(x, e), e


def _seg_sum_bwd(e, g):
    return _seg_sum_raw(g, e), None


_seg_sum.defvjp(_seg_sum_fwd, _seg_sum_bwd)


def _block_ones(n, seg=HEAD):
    i = np.arange(n) // seg
    return jnp.asarray((i[:, None] == i[None, :]).astype(np.float32), dtype=BF16)


def _rms(xv, g):
    ms = jnp.mean(xv * xv, axis=-1, keepdims=True)
    return xv * lax.rsqrt(ms + EPS_RMS) * g


def _rms_mod(xv, g, shift, scale):
    return _rms(xv, g) * (1.0 + scale) + shift


def _sigmoid(z):
    return 1.0 / (1.0 + jnp.exp(-z))


def _silu(z):
    return z * _sigmoid(z)


def _softplus(z):
    return jnp.maximum(z, 0.0) + jnp.log(1.0 + jnp.exp(-jnp.abs(z)))


class _Cfg:
    def __init__(self, B, TX, TC, D, W, CW, R, GR, KC, F):
        self.B, self.TX, self.TC, self.D = B, TX, TC, D
        self.W, self.CW, self.R, self.GR, self.KC, self.F = W, CW, R, GR, KC, F
        self.T = TX + TC
        self.TT = min(256, TC)
        assert TC % self.TT == 0 and TX % self.TT == 0 and self.TT % LINE == 0
        self.JC = TC // self.TT
        self.JX = TX // self.TT
        self.J = self.JC + self.JX
        self.HP = W // LANES
        self.G = B * self.HP
        self.PW = _round_up(2 * R, LANES)
        self.GP = _round_up(GR, LANES)
        self.SP = 3 * W + 2 * self.PW + self.GP
        self.CP = self.SP + 2 * CW
        self.KP = _round_up(KC, SUBLANES)
        assert self.T % SCAN_CHUNK == 0 and TC % SCAN_CHUNK == 0
        self.NCH = self.T // SCAN_CHUNK
        self.NCC = TC // SCAN_CHUNK
        W_, R_ = W, R
        segs = [(0, 3 * W_, 0),
                (3 * W_, 2 * R_, 3 * W_),
                (3 * W_ + 2 * R_, 2 * R_, 3 * W_ + self.PW),
                (3 * W_ + 4 * R_, GR, 3 * W_ + 2 * self.PW),
                (3 * W_ + 4 * R_ + GR, 2 * CW, self.SP)]
        self.col_segs = segs
        self.shift_cols = 3 * W_ + 4 * R_ + GR
        self.in_cols = self.shift_cols + 2 * CW


def _pad_cols(a, cfg, upto_shift=False):
    width = cfg.SP if upto_shift else cfg.CP
    pieces, pos = [], 0
    for src, n, dst in cfg.col_segs:
        if upto_shift and dst >= cfg.SP:
            break
        if dst > pos:
            pieces.append(jnp.zeros(a.shape[:-1] + (dst - pos,), a.dtype))
        pieces.append(a[..., src:src + n])
        pos = dst + n
    if width > pos:
        pieces.append(jnp.zeros(a.shape[:-1] + (width - pos,), a.dtype))
    return jnp.concatenate(pieces, axis=-1)


def _unpad_cols(a, cfg, upto_shift=False):
    pieces = []
    for src, n, dst in cfg.col_segs:
        if upto_shift and dst >= cfg.SP:
            break
        pieces.append(a[..., dst:dst + n])
    return jnp.concatenate(pieces, axis=-1)


def _pair_weight(w2, cfg):
    R, W = cfg.R, cfg.W
    out = jnp.zeros((cfg.PW, 2 * W), w2.dtype)
    out = out.at[0:R, 0:W].set(w2[0])
    out = out.at[R:2 * R, W:2 * W].set(w2[1])
    return out


def _unpair_weight(g, cfg):
    R, W = cfg.R, cfg.W
    return jnp.stack([g[0:R, 0:W], g[R:2 * R, W:2 * W]])


def _row_ids(n):
    return lax.broadcasted_iota(jnp.int32, (n, 1), 0)


def _shift_rows(z, prev_row, next_row):
    n = z.shape[0]
    rows = _row_ids(n)
    zp = jnp.where(rows == 0, prev_row, pltpu.roll(z, 1, 0))
    zn = jnp.where(rows == n - 1, next_row, pltpu.roll(z, n - 1, 0))
    return zp, zn


def _line_shift(u, d):
    if d == 0:
        return u
    n = u.shape[0]
    lt = _row_ids(n) % LINE
    ok = jnp.logical_and(lt + d >= 0, lt + d < LINE)
    return jnp.where(ok, pltpu.roll(u, (-d) % n, 0), 0.0)


def _dwconv(u, cw, kc):
    pad = kc // 2
    acc = jnp.zeros_like(u)
    for i in range(kc):
        acc = acc + _line_shift(u, i - pad) * cw[i:i + 1, :]
    return acc


def _rwkv_prep(rw, w0, w2p, a0, a2p, k_k, k_a, e, cfg):
    W, PW = cfg.W, cfg.PW
    r = rw[:, 0:W]
    k = rw[:, W:2 * W]
    v = rw[:, 2 * W:3 * W]
    wdp = rw[:, 3 * W:3 * W + PW]
    adp = rw[:, 3 * W + PW:3 * W + 2 * PW]
    wl = w0 + _mm(jnp.tanh(wdp), w2p)
    w_log = -_softplus(-wl) - 0.5
    decay = jnp.exp(-jnp.exp(w_log))
    iclr = _sigmoid(a0 + _mm(adp, a2p))
    kkr = k * k_k
    nrm = jnp.sqrt(_seg_sum(kkr * kkr, e))
    kk = kkr / jnp.maximum(nrm, 1e-12)
    outs = [r, v, kk]
    for d in range(2):
        ic = iclr[:, d * W:(d + 1) * W]
        outs += [decay[:, d * W:(d + 1) * W], k * (1.0 + (ic - 1.0) * k_a), kk * ic]
    return tuple(outs)


def _glu(cv, cfg):
    return cv[:, :cfg.CW] * _sigmoid(cv[:, cfg.CW:])


def _conv_post(y, cb, lw, lb):
    yf = y + cb
    mu = jnp.mean(yf, axis=-1, keepdims=True)
    var = jnp.mean(jnp.square(yf - mu), axis=-1, keepdims=True)
    return _silu((yf - mu) * lax.rsqrt(var + EPS_LN) * lw + lb)


def _readout(y, kbar, r, v, gd, r_k, gw2, lnx_w, lnx_b, e):
    inv = 1.0 / HEAD
    mu = _seg_sum(y, e) * inv
    yc = y - mu
    var = _seg_sum(yc * yc, e) * inv
    yn = yc * lax.rsqrt(var + EPS_GN) * lnx_w + lnx_b
    bonus = _seg_sum(r * kbar * r_k, e) * v
    g = _mm(_sigmoid(gd), gw2)
    return (yn + bonus) * g


def _post_res(xv, mix, gate, g):
    return xv + gate * _rms(mix, g)


def _head_spec(cfg, tmap):
    return pl.BlockSpec((1, cfg.HP, cfg.TT, LANES), lambda b, j: (b, 0, tmap(j), 0))


def _full_spec(shape):
    n = len(shape)
    return pl.BlockSpec(shape, lambda *_: (0,) * n)


def _to_heads(ref, val, cfg):
    for hp in range(cfg.HP):
        ref[0, hp] = val[:, hp * LANES:(hp + 1) * LANES]


def _from_heads(ref, cfg):
    return jnp.concatenate([ref[0, hp] for hp in range(cfg.HP)], axis=-1)


def _in_proj(xcat, modt, g1, w_in_p, cfg):
    B, T, D, TT, CP = cfg.B, cfg.T, cfg.D, cfg.TT, cfg.CP

    def body(x_ref, mod_ref, g_ref, w_ref, p_ref, h_ref):
        h = _rms_mod(x_ref[0], g_ref[...], mod_ref[0, 0, 0:1, :], mod_ref[0, 0, 1:2, :])
        hb = h.astype(BF16)
        h_ref[0] = hb
        p_ref[0] = jnp.dot(hb, w_ref[...], preferred_element_type=F32)

    return pl.pallas_call(
        body, name="in_proj", grid=(B, cfg.J),
        in_specs=[pl.BlockSpec((1, TT, D), lambda b, j: (b, j, 0)),
                  pl.BlockSpec((1, 1, 2, D), lambda b, j: (b, j, 0, 0)),
                  _full_spec((1, D)), _full_spec((D, CP))],
        out_specs=[pl.BlockSpec((1, TT, CP), lambda b, j: (b, j, 0)),
                   pl.BlockSpec((1, TT, D), lambda b, j: (b, j, 0))],
        out_shape=[jax.ShapeDtypeStruct((B, T, CP), F32), jax.ShapeDtypeStruct((B, T, D), BF16)],
        compiler_params=_params(("parallel", "parallel")),
    )(xcat, modt, g1, w_in_p)


def _halo_specs(cfg, width):
    per = cfg.TT // SUBLANES
    last = cfg.T // SUBLANES - 1
    prev = pl.BlockSpec((1, SUBLANES, width), lambda b, j: (b, jnp.maximum(j * per - 1, 0), 0))
    nxt = pl.BlockSpec((1, SUBLANES, width), lambda b, j: (b, jnp.minimum((j + 1) * per, last), 0))
    return prev, nxt


def _halo_flags(j, cfg):
    has_prev = jnp.logical_and(j != 0, j != cfg.JC).astype(F32)
    has_next = jnp.logical_and(j != cfg.JC - 1, j != cfg.J - 1).astype(F32)
    return has_prev, has_next


def _shifted(p_ref, prev_ref, next_ref, mup, mun, j, cfg):
    SP = cfg.SP
    has_prev, has_next = _halo_flags(j, cfg)
    z = p_ref[0][:, :SP]
    zp, zn = _shift_rows(z, prev_ref[0, SUBLANES - 1:SUBLANES, :] * has_prev, next_ref[0, 0:1, :] * has_next)
    return z, zp, zn, z + mup * (zp - z) + mun * (zn - z)


def _mix_prep(p, mup, mun, w0, w2p, a0, a2p, k_k, k_a, cw, cb, clw, clb, e_w, cfg):
    B, T, TT, SP, CP, W, CW, HP = cfg.B, cfg.T, cfg.TT, cfg.SP, cfg.CP, cfg.W, cfg.CW, cfg.HP

    def body(p_ref, prev_ref, next_ref, mup_ref, mun_ref, w0_ref, w2_ref, a0_ref, a2_ref, kk_ref, ka_ref,
             cw_ref, cb_ref, clw_ref, clb_ref, e_ref, *outs):
        j = pl.program_id(1)
        _, _, _, rw = _shifted(p_ref, prev_ref, next_ref, mup_ref[...], mun_ref[...], j, cfg)
        vals = _rwkv_prep(rw, w0_ref[...], w2_ref[...], a0_ref[...], a2_ref[...], kk_ref[...], ka_ref[...],
                          e_ref[...], cfg)
        for ref, val in zip(outs[:9], vals):
            _to_heads(ref, val, cfg)
        outs[9][0] = rw[:, 3 * W + 2 * cfg.PW:SP]
        u = _glu(p_ref[0][:, SP:], cfg)
        outs[10][0] = _conv_post(_dwconv(u, cw_ref[...], cfg.KC), cb_ref[...], clw_ref[...], clb_ref[...])

    prev, nxt = _halo_specs(cfg, SP)
    head = jax.ShapeDtypeStruct((B, HP, T, LANES), F32)
    return pl.pallas_call(
        body, name="mix_prep", grid=(B, cfg.J),
        in_specs=[pl.BlockSpec((1, TT, CP), lambda b, j: (b, j, 0)), prev, nxt,
                  _full_spec((1, SP)), _full_spec((1, SP)),
                  _full_spec((1, 2 * W)), _full_spec((cfg.PW, 2 * W)),
                  _full_spec((1, 2 * W)), _full_spec((cfg.PW, 2 * W)),
                  _full_spec((1, W)), _full_spec((1, W)),
                  _full_spec((cfg.KP, CW)), _full_spec((1, CW)), _full_spec((1, CW)), _full_spec((1, CW)),
                  _full_spec((W, W))],
        out_specs=[_head_spec(cfg, lambda j: j)] * 9
                  + [pl.BlockSpec((1, TT, cfg.GP), lambda b, j: (b, j, 0)),
                     pl.BlockSpec((1, TT, CW), lambda b, j: (b, j, 0))],
        out_shape=[head] * 9 + [jax.ShapeDtypeStruct((B, T, cfg.GP), F32),
                                jax.ShapeDtypeStruct((B, T, CW), F32)],
        compiler_params=_params(("parallel", "parallel")),
    )(p, p, p, mup, mun, w0, w2p, a0, a2p, k_k, k_a, cw, cb, clw, clb, e_w)


def _chunk_pos(c, reverse, cfg):
    if not reverse:
        return c
    return jnp.where(c < cfg.NCC, cfg.NCC - 1 - c, cfg.NCH - 1 + cfg.NCC - c)


def _diag_mask():
    r = lax.broadcasted_iota(jnp.int32, (HEAD, LANES), 0)
    l = lax.broadcasted_iota(jnp.int32, (HEAD, LANES), 1)
    return (r == l % HEAD).astype(F32)


def _col_lhs(row, diag_b):
    hi = row.astype(BF16)
    lo = (row - hi.astype(F32)).astype(BF16)
    return diag_b * hi, diag_b * lo


def _col_dot(row_list, diag_b, e2):
    n, g = len(row_list), row_list[0].shape[0]
    lhs = jnp.concatenate([jnp.concatenate(_col_lhs(r, diag_b), axis=-1) for r in row_list], axis=0)
    out = jnp.dot(lhs.reshape(n * g * HEAD, 2 * LANES), e2, preferred_element_type=F32)
    return out.reshape(n, g, HEAD, LANES)


def _col_form(row):
    n = row.shape[0]
    t = jnp.swapaxes(jnp.broadcast_to(row, (n, LANES, LANES)), 1, 2)
    lane = lax.broadcasted_iota(jnp.int32, (HEAD, LANES), 1)
    return jnp.where(lane < HEAD, t[:, :HEAD, :], t[:, HEAD:, :])


def _col_both(row, diag_b, e2):
    half = row.shape[0] // 2
    return jnp.concatenate([_col_form(row[:half]), _col_dot([row[half:]], diag_b, e2)[0]], axis=0)


def _both_rows(ins, idx, i):
    return jnp.concatenate([ins[d][idx][:, pl.ds(_tok(i, d == 1), 1), :] for d in range(2)], axis=0)


def _seg_dot(blocks, e):
    n, g = len(blocks), blocks[0].shape[0]
    lhs = jnp.concatenate(blocks, axis=0).reshape(n * g * HEAD, LANES)
    return jnp.dot(lhs, e, preferred_element_type=F32).reshape(n, g, HEAD, LANES)


def _tok(i, reverse):
    return (SCAN_CHUNK - 1 - i) if reverse else i


def _scan_fwd(ops_f, ops_b, e128, e256, cfg):
    G, T, NCH = cfg.G, cfg.T, cfg.NCH
    CH = SCAN_CHUNK
    G2 = 2 * G

    def body(*refs):
        ins = (refs[0:6], refs[6:12])
        e_ref, e2_ref = refs[12], refs[13]
        ys, hist_ref, fin_ref = (refs[14], refs[15]), refs[16], refs[17]
        s_ref, mm_ref = refs[18], refs[19]
        c = pl.program_id(0)

        @pl.when(c == 0)
        def _():
            s_ref[...] = jnp.zeros_like(s_ref)

        diag = _diag_mask()
        diag_b = diag.astype(BF16)
        e, e2 = e_ref[...], e2_ref[...]
        rows = functools.partial(_both_rows, ins)

        mm_ref[0] = _seg_dot([s_ref[...].astype(BF16) * (-rows(4, 0)).astype(BF16)], e)[0]
        mm_ref[1] = _col_both(rows(3, 0), diag_b, e2)

        def step(i, carry):
            nxt = jnp.minimum(i + 1, CH - 1)
            res = []
            for d in range(2):
                sl = slice(d * G, (d + 1) * G)
                row = lambda idx, ii: ins[d][idx][:, pl.ds(_tok(ii, d == 1), 1), :]
                s_old = s_ref[sl]
                hist_ref[i, sl] = s_old
                S = s_old * row(1, i) + mm_ref[0, sl] * row(5, i) + mm_ref[1, sl] * row(2, i)
                s_ref[sl] = S
                sb = S.astype(BF16)
                res.append(_seg_dot([sb * (-row(4, nxt)).astype(BF16), sb * row(0, i).astype(BF16)], e))
            for d in range(2):
                sl = slice(d * G, (d + 1) * G)
                v_next = ins[d][3][:, pl.ds(_tok(nxt, d == 1), 1), :]
                mm_ref[0, sl] = res[d][0]
                mm_ref[1, sl] = _col_form(v_next) if d == 0 else _col_dot([v_next], diag_b, e2)[0]
                ys[d][:, pl.ds(_tok(i, d == 1), 1), :] = jnp.sum(diag * res[d][1], axis=1, keepdims=True)
            return carry

        lax.fori_loop(0, CH, step, 0)
        fin_ref[...] = s_ref[...]

    toks = [pl.BlockSpec((G, CH, LANES), lambda c, rev=rev: (0, _chunk_pos(c, rev, cfg), 0)) for rev in (False, True)]
    y_shape = jax.ShapeDtypeStruct((G, T, LANES), F32)
    return pl.pallas_call(
        body, name="scan_fwd", grid=(NCH,),
        in_specs=[toks[0]] * 6 + [toks[1]] * 6 + [_full_spec((LANES, LANES)), _full_spec((2 * LANES, LANES))],
        out_specs=[toks[0], toks[1], pl.BlockSpec((CH, G2, HEAD, LANES), lambda c: (c, 0, 0, 0)),
                   _full_spec((G2, HEAD, LANES))],
        out_shape=[y_shape, y_shape, jax.ShapeDtypeStruct(((NCH + 1) * CH, G2, HEAD, LANES), F32),
                   jax.ShapeDtypeStruct((G2, HEAD, LANES), F32)],
        scratch_shapes=[pltpu.VMEM((G2, HEAD, LANES), F32), pltpu.VMEM((2, G2, HEAD, LANES), F32)],
        compiler_params=_params(("arbitrary",)),
    )(*ops_f, *ops_b, e128, e256)


def _mix_out(yf, yb, kdf, kdb, r, v, gd, conv, x, mod2, r_k, gw2p, lnx_w, lnx_b, w_out, g2, e_w, cfg):
    B, TX, D, TT, W, CW, JC = cfg.B, cfg.TX, cfg.D, cfg.TT, cfg.W, cfg.CW, cfg.JC

    def body(yf_ref, yb_ref, kdf_ref, kdb_ref, r_ref, v_ref, gd_ref, cv_ref, x_ref, mod_ref,
             rk_ref, gw_ref, lw_ref, lb_ref, wo_ref, g_ref, e_ref, x1_ref):
        y = _from_heads(yf_ref, cfg) + _from_heads(yb_ref, cfg)
        kbar = 0.5 * (_from_heads(kdf_ref, cfg) + _from_heads(kdb_ref, cfg))
        ro = _readout(y, kbar, _from_heads(r_ref, cfg), _from_heads(v_ref, cfg), gd_ref[0], rk_ref[...],
                      gw_ref[...], lw_ref[...], lb_ref[...], e_ref[...])
        cat = jnp.concatenate([ro, cv_ref[0]], axis=-1)
        mix = _bdot(cat, wo_ref[...])
        x1_ref[0] = _post_res(x_ref[0], mix, mod_ref[0], g_ref[...])

    hs = _head_spec(cfg, lambda j: j + JC)
    lat = lambda n: pl.BlockSpec((1, TT, n), lambda b, j: (b, j + JC, 0))
    return pl.pallas_call(
        body, name="mix_out", grid=(B, cfg.JX),
        in_specs=[hs] * 6 + [lat(cfg.GP), lat(CW),
                             pl.BlockSpec((1, TT, D), lambda b, j: (b, j, 0)),
                             pl.BlockSpec((1, 1, D), lambda b, j: (b, 0, 0)),
                             _full_spec((1, W)), _full_spec((cfg.GP, W)), _full_spec((1, W)), _full_spec((1, W)),
                             _full_spec((W + CW, D)), _full_spec((1, D)), _full_spec((W, W))],
        out_specs=pl.BlockSpec((1, TT, D), lambda b, j: (b, j, 0)),
        out_shape=jax.ShapeDtypeStruct((B, TX, D), F32),
        compiler_params=_params(("parallel", "parallel")),
    )(yf, yb, kdf, kdb, r, v, gd, conv, x, mod2, r_k, gw2p, lnx_w, lnx_b, w_out, g2, e_w)


def _acc(ref, val, first):
    @pl.when(first)
    def _():
        ref[...] = val

    @pl.when(jnp.logical_not(first))
    def _():
        ref[...] += val


def _mlp_fwd_bwd(x1, tgt, mod345, g3, g4, w1, w2, cfg):
    B, TX, D, TT, F, JX = cfg.B, cfg.TX, cfg.D, cfg.TT, cfg.F, cfg.JX

    def body(x1_ref, t_ref, mod_ref, g3_ref, g4_ref, w1_ref, w2_ref,
             dx1_ref, loss_ref, h2_ref, dpre_ref, act_ref, dff_ref, dmod_ref, dg3_ref, dg4_ref):
        b, j = pl.program_id(0), pl.program_id(1)
        x1v = x1_ref[0]
        sh, sc, gt = mod_ref[0, 0:1, :], mod_ref[0, 1:2, :], mod_ref[0, 2:3, :]
        h2, vjp_pre = jax.vjp(_rms_mod, x1v, g3_ref[...], sh, sc)
        h2b = h2.astype(BF16)
        pre = jnp.dot(h2b, w1_ref[...], preferred_element_type=F32)
        rl = jnp.maximum(pre, 0.0)
        actb = (rl * rl).astype(BF16)
        ff = jnp.dot(actb, w2_ref[...], preferred_element_type=F32)
        x2, vjp_post = jax.vjp(_post_res, x1v, ff, gt, g4_ref[...])
        err = x2 - t_ref[0]
        loss = 0.5 * jnp.sum(jnp.mean(err * err, axis=-1, keepdims=True))
        dx1a, dff, dgt, dg4 = vjp_post(err * (1.0 / D))
        dffb = dff.astype(BF16)
        dpre = _bdot_nt(dffb, w2_ref[...]) * (2.0 * rl)
        dpreb = dpre.astype(BF16)
        dx1b, dg3, dsh, dsc = vjp_pre(_bdot_nt(dpreb, w1_ref[...]))
        dx1_ref[0] = dx1a + dx1b
        loss_ref[0, 0] = jnp.zeros((SUBLANES, LANES), F32) + loss
        h2_ref[0] = h2b
        dpre_ref[0] = dpreb
        act_ref[0] = actb
        dff_ref[0] = dffb
        _acc(dmod_ref, jnp.concatenate([dsh, dsc, dgt], axis=0)[None], j == 0)
        first = jnp.logical_and(b == 0, j == 0)
        _acc(dg3_ref, dg3, first)
        _acc(dg4_ref, dg4, first)

    tile = lambda n: pl.BlockSpec((1, TT, n), lambda b, j: (b, j, 0))
    return pl.pallas_call(
        body, name="mlp_fwd_bwd", grid=(B, JX),
        in_specs=[tile(D), tile(D), pl.BlockSpec((1, 3, D), lambda b, j: (b, 0, 0)),
                  _full_spec((1, D)), _full_spec((1, D)),
                  pl.BlockSpec((D, F), lambda b, j: (0, 0), pipeline_mode=pl.Buffered(1)),
                  pl.BlockSpec((F, D), lambda b, j: (0, 0), pipeline_mode=pl.Buffered(1))],
        out_specs=[tile(D), pl.BlockSpec((1, 1, SUBLANES, LANES), lambda b, j: (b, j, 0, 0)),
                   tile(D), tile(F), tile(F), tile(D),
                   pl.BlockSpec((1, 3, D), lambda b, j: (b, 0, 0)),
                   _full_spec((1, D)), _full_spec((1, D))],
        out_shape=[jax.ShapeDtypeStruct((B, TX, D), F32),
                   jax.ShapeDtypeStruct((B, JX, SUBLANES, LANES), F32),
                   jax.ShapeDtypeStruct((B, TX, D), BF16), jax.ShapeDtypeStruct((B, TX, F), BF16),
                   jax.ShapeDtypeStruct((B, TX, F), BF16), jax.ShapeDtypeStruct((B, TX, D), BF16),
                   jax.ShapeDtypeStruct((B, 3, D), F32),
                   jax.ShapeDtypeStruct((1, D), F32), jax.ShapeDtypeStruct((1, D), F32)],
        compiler_params=_params(("arbitrary", "arbitrary")),
    )(x1, tgt, mod345, g3, g4, w1, w2)


def _mix_out_bwd(yf, yb, kdf, kdb, r, v, gd, conv, x, mod2, r_k, gw2p, lnx_w, lnx_b, w_out, g2, e_w, dx1, cfg):
    B, TX, D, TT, W, CW, JC, HP, GP = cfg.B, cfg.TX, cfg.D, cfg.TT, cfg.W, cfg.CW, cfg.JC, cfg.HP, cfg.GP

    def body(yf_ref, yb_ref, kdf_ref, kdb_ref, r_ref, v_ref, gd_ref, cv_ref, x_ref, mod_ref,
             rk_ref, gw_ref, lw_ref, lb_ref, wo_ref, g_ref, e_ref, dx1_ref,
             dy_ref, dkb_ref, dr_ref, dv_ref, dgd_ref, dcv_ref, cat_ref, dmix_ref,
             dmod_ref, dg2_ref, drk_ref, dgw_ref, dlw_ref, dlb_ref):
        b, j = pl.program_id(0), pl.program_id(1)
        e = e_ref[...]
        y = _from_heads(yf_ref, cfg) + _from_heads(yb_ref, cfg)
        kbar = 0.5 * (_from_heads(kdf_ref, cfg) + _from_heads(kdb_ref, cfg))
        ro, vjp_ro = jax.vjp(lambda *a: _readout(*a, e), y, kbar, _from_heads(r_ref, cfg),
                             _from_heads(v_ref, cfg), gd_ref[0], rk_ref[...], gw_ref[...], lw_ref[...], lb_ref[...])
        catb = jnp.concatenate([ro, cv_ref[0]], axis=-1).astype(BF16)
        mix = jnp.dot(catb, wo_ref[...], preferred_element_type=F32)
        _, vjp_post = jax.vjp(_post_res, x_ref[0], mix, mod_ref[0], g_ref[...])
        _, dmix, dgate, dg2 = vjp_post(dx1_ref[0])
        dmixb = dmix.astype(BF16)
        dcat = _bdot_nt(dmixb, wo_ref[...])
        dy, dkb, dr, dv, dgd, drk, dgw, dlw, dlb = vjp_ro(dcat[:, :W])
        _to_heads(dy_ref, dy, cfg)
        _to_heads(dkb_ref, dkb, cfg)
        _to_heads(dr_ref, dr, cfg)
        _to_heads(dv_ref, dv, cfg)
        dgd_ref[0] = dgd
        dcv_ref[0] = dcat[:, W:]
        cat_ref[0] = catb
        dmix_ref[0] = dmixb
        _acc(dmod_ref, dgate[None], j == 0)
        first = jnp.logical_and(b == 0, j == 0)
        _acc(dg2_ref, dg2, first)
        _acc(drk_ref, drk, first)
        _acc(dgw_ref, dgw, first)
        _acc(dlw_ref, dlw, first)
        _acc(dlb_ref, dlb, first)

    hs = _head_spec(cfg, lambda j: j + JC)
    ho = _head_spec(cfg, lambda j: j)
    lat = lambda n: pl.BlockSpec((1, TT, n), lambda b, j: (b, j + JC, 0))
    tile = lambda n: pl.BlockSpec((1, TT, n), lambda b, j: (b, j, 0))
    head = jax.ShapeDtypeStruct((B, HP, TX, LANES), F32)
    vec = lambda n: jax.ShapeDtypeStruct((1, n), F32)
    return pl.pallas_call(
        body, name="mix_out_bwd", grid=(B, cfg.JX),
        in_specs=[hs] * 6 + [lat(GP), lat(CW), tile(D),
                             pl.BlockSpec((1, 1, D), lambda b, j: (b, 0, 0)),
                             _full_spec((1, W)), _full_spec((GP, W)), _full_spec((1, W)), _full_spec((1, W)),
                             _full_spec((W + CW, D)), _full_spec((1, D)), _full_spec((W, W)), tile(D)],
        out_specs=[ho] * 4 + [tile(GP), tile(CW), tile(W + CW), tile(D),
                              pl.BlockSpec((1, 1, D), lambda b, j: (b, 0, 0)),
                              _full_spec((1, D)), _full_spec((1, W)), _full_spec((GP, W)),
                              _full_spec((1, W)), _full_spec((1, W))],
        out_shape=[head] * 4 + [jax.ShapeDtypeStruct((B, TX, GP), F32), jax.ShapeDtypeStruct((B, TX, CW), F32),
                                jax.ShapeDtypeStruct((B, TX, W + CW), BF16), jax.ShapeDtypeStruct((B, TX, D), BF16),
                                jax.ShapeDtypeStruct((B, 1, D), F32),
                                vec(D), vec(W), jax.ShapeDtypeStruct((GP, W), F32), vec(W), vec(W)],
        compiler_params=_params(("arbitrary", "arbitrary")),
    )(yf, yb, kdf, kdb, r, v, gd, conv, x, mod2, r_k, gw2p, lnx_w, lnx_b, w_out, g2, e_w, dx1)


def _scan_bwd(ops_f, ops_b, dy, hist, e128, e256, cfg):
    G, T, NCH, NCC = cfg.G, cfg.T, cfg.NCH, cfg.NCC
    CH = SCAN_CHUNK
    G2 = 2 * G

    def body(*refs):
        ins = (refs[0:6], refs[6:12])
        dys, hist_ref, next_ref, e_ref, e2_ref = (refs[12], refs[13]), refs[14], refs[15], refs[16], refs[17]
        outs = (refs[18:24], refs[24:30])
        ds_ref, mm_ref = refs[30], refs[31]
        gi = pl.program_id(0)

        @pl.when(gi == 0)
        def _():
            ds_ref[...] = jnp.zeros_like(ds_ref)

        diag = _diag_mask()
        diag_b = diag.astype(BF16)
        e, e2 = e_ref[...], e2_ref[...]
        rows = functools.partial(_both_rows, ins)
        latent = [(_chunk_pos(NCH - 1 - gi, d == 1, cfg) >= NCC).astype(F32) for d in range(2)]

        def dy_rows(i):
            return jnp.concatenate([dys[d][:, pl.ds(_tok(i, d == 1), 1), :] * latent[d] for d in range(2)], axis=0)

        def put(idx, i, val):
            outs[0][idx][:, pl.ds(_tok(i, False), 1), :] = val[:G]
            outs[1][idx][:, pl.ds(_tok(i, True), 1), :] = val[G:]

        rsum = lambda z: jnp.sum(z, axis=1, keepdims=True)

        def prepare(i):
            return (_col_form(dy_rows(i)), _col_dot([rows(3, i)], diag_b, e2)[0],
                    hist_ref[i].astype(BF16) * (-rows(4, i)).astype(BF16))

        dyc0, vb0, sa_lhs = prepare(CH - 1)
        mm_ref[0] = dyc0
        mm_ref[1] = vb0
        mm_ref[2] = _seg_dot([sa_lhs], e)[0]

        def one_step(i, s_after):
            prv = jnp.maximum(i - 1, 0)
            sp, dyc = hist_ref[i], mm_ref[0]
            ds = ds_ref[...] + dyc * rows(0, i)
            put(0, i, rsum(s_after * dyc))
            put(1, i, rsum(ds * sp))
            put(5, i, rsum(ds * mm_ref[2]))
            put(2, i, rsum(ds * mm_ref[1]))
            dyc_n, vb_n, sa_lhs_n = prepare(prv)
            dsb = ds.astype(BF16)
            res = _seg_dot([dsb * rows(5, i).astype(BF16), dsb * rows(2, i).astype(BF16), sa_lhs_n], e)
            dsa = res[0]
            put(4, i, -rsum(sp * dsa))
            put(3, i, rsum(diag * res[1]))
            mm_ref[0] = dyc_n
            mm_ref[1] = vb_n
            mm_ref[2] = res[2]
            ds_ref[...] = ds * rows(1, i) - dsa * rows(4, i)

        one_step(CH - 1, next_ref[0])

        def bstep(ii, carry):
            i = CH - 1 - ii
            one_step(i, hist_ref[i + 1])
            return carry

        lax.fori_loop(1, CH, bstep, 0)

    cpos = lambda g, rev: _chunk_pos(NCH - 1 - g, rev, cfg)
    toks = [pl.BlockSpec((G, CH, LANES), lambda g, rev=rev: (0, cpos(g, rev), 0)) for rev in (False, True)]
    dy_specs = [pl.BlockSpec((G, CH, LANES), lambda g, rev=rev: (0, jnp.maximum(cpos(g, rev) - NCC, 0), 0))
                for rev in (False, True)]
    out = jax.ShapeDtypeStruct((G, T, LANES), F32)
    res = pl.pallas_call(
        body, name="scan_bwd", grid=(NCH,),
        in_specs=[toks[0]] * 6 + [toks[1]] * 6 + dy_specs
                 + [pl.BlockSpec((CH, G2, HEAD, LANES), lambda g: (NCH - 1 - g, 0, 0, 0)),
                    pl.BlockSpec((1, G2, HEAD, LANES), lambda g: ((NCH - g) * CH, 0, 0, 0)),
                    _full_spec((LANES, LANES)), _full_spec((2 * LANES, LANES))],
        out_specs=[toks[0]] * 6 + [toks[1]] * 6,
        out_shape=[out] * 12,
        scratch_shapes=[pltpu.VMEM((G2, HEAD, LANES), F32), pltpu.VMEM((3, G2, HEAD, LANES), F32)],
        compiler_params=_params(("arbitrary",)),
    )(*ops_f, *ops_b, dy, dy, hist, hist, e128, e256)
    return res[:6], res[6:]


def _mix_prep_bwd(p, mup, mun, w0, w2p, a0, a2p, k_k, k_a, cw, cb, clw, clb, e_w, sf, sb, ro, cfg):
    B, T, TT, SP, CP, W, CW, HP, JC, PW, GP, KC, KP = (cfg.B, cfg.T, cfg.TT, cfg.SP, cfg.CP, cfg.W, cfg.CW,
                                                       cfg.HP, cfg.JC, cfg.PW, cfg.GP, cfg.KC, cfg.KP)
    pad = KC // 2

    def body(p_ref, prev_ref, next_ref, mup_ref, mun_ref, w0_ref, w2_ref, a0_ref, a2_ref, kk_ref, ka_ref,
             cw_ref, cb_ref, clw_ref, clb_ref, e_ref, *rest):
        sf_refs, sb_refs = rest[0:6], rest[6:12]
        rdr_ref, rdv_ref, rdkb_ref, rdgd_ref, rdcv_ref = rest[12:17]
        (dpz_ref, dmup_ref, dmun_ref, dw0_ref, dw2_ref, da0_ref, da2_ref, dkk_ref, dka_ref,
         dcw_ref, dcb_ref, dclw_ref, dclb_ref) = rest[17:]
        b, j = pl.program_id(0), pl.program_id(1)
        first = jnp.logical_and(b == 0, j == 0)
        lat = (j >= JC).astype(F32)
        e = e_ref[...]
        mup_v, mun_v = mup_ref[...], mun_ref[...]
        z, zp, zn, rw = _shifted(p_ref, prev_ref, next_ref, mup_v, mun_v, j, cfg)

        def prep(rw_, w0_, w2_, a0_, a2_, kk_, ka_):
            return _rwkv_prep(rw_, w0_, w2_, a0_, a2_, kk_, ka_, e, cfg) + (rw_[:, 3 * W + 2 * PW:SP],)

        _, vjp_prep = jax.vjp(prep, rw, w0_ref[...], w2_ref[...], a0_ref[...], a2_ref[...], kk_ref[...], ka_ref[...])
        fr, fw, fk, fv, fkk, fb = [_from_heads(r_, cfg) for r_ in sf_refs]
        br, bw, bk, bv, bkk, bb = [_from_heads(r_, cfg) for r_ in sb_refs]
        half_kb = (0.5 * lat) * _from_heads(rdkb_ref, cfg)
        cots = (fr + br + lat * _from_heads(rdr_ref, cfg), fv + bv + lat * _from_heads(rdv_ref, cfg), fkk + bkk,
                fw, fk + half_kb, fb, bw, bk + half_kb, bb, lat * rdgd_ref[0])
        drw, dw0, dw2, da0, da2, dkk, dka = vjp_prep(cots)
        _acc(dmup_ref, jnp.sum(drw * (zp - z), axis=0, keepdims=True), first)
        _acc(dmun_ref, jnp.sum(drw * (zn - z), axis=0, keepdims=True), first)
        for ref, val in ((dw0_ref, dw0), (dw2_ref, dw2), (da0_ref, da0), (da2_ref, da2), (dkk_ref, dkk), (dka_ref, dka)):
            _acc(ref, val, first)

        cvv = p_ref[0, :, SP:]
        u, vjp_glu = jax.vjp(lambda c_: _glu(c_, cfg), cvv)
        cwv = cw_ref[...]
        _, vjp_post = jax.vjp(_conv_post, _dwconv(u, cwv, KC), cb_ref[...], clw_ref[...], clb_ref[...])
        dyc, dcb, dclw, dclb = vjp_post(lat * rdcv_ref[0])
        du = jnp.zeros_like(u)
        for i in range(KC):
            du = du + _line_shift(dyc, pad - i) * cwv[i:i + 1, :]
        (dcv,) = vjp_glu(du)
        dpz_ref[0, :, 0:SP] = drw
        dpz_ref[0, :, SP:] = dcv

        @pl.when(first)
        def _():
            dcw_ref[...] = jnp.zeros_like(dcw_ref)

        for i in range(KC):
            dcw_ref[i:i + 1, :] += jnp.sum(dyc * _line_shift(u, i - pad), axis=0, keepdims=True)
        _acc(dcb_ref, dcb, first)
        _acc(dclw_ref, dclw, first)
        _acc(dclb_ref, dclb, first)

    prev, nxt = _halo_specs(cfg, SP)
    hs = _head_spec(cfg, lambda j: j)
    hl = _head_spec(cfg, lambda j: jnp.maximum(j - JC, 0))
    latn = lambda n: pl.BlockSpec((1, TT, n), lambda b, j: (b, jnp.maximum(j - JC, 0), 0))
    vec = lambda n: jax.ShapeDtypeStruct((1, n), F32)
    small_shapes = [vec(SP), vec(SP), vec(2 * W), jax.ShapeDtypeStruct((PW, 2 * W), F32), vec(2 * W),
                    jax.ShapeDtypeStruct((PW, 2 * W), F32), vec(W), vec(W),
                    jax.ShapeDtypeStruct((KP, CW), F32), vec(CW), vec(CW), vec(CW)]
    return pl.pallas_call(
        body, name="mix_prep_bwd", grid=(B, cfg.J),
        in_specs=[pl.BlockSpec((1, TT, CP), lambda b, j: (b, j, 0)), prev, nxt,
                  _full_spec((1, SP)), _full_spec((1, SP)),
                  _full_spec((1, 2 * W)), _full_spec((PW, 2 * W)),
                  _full_spec((1, 2 * W)), _full_spec((PW, 2 * W)),
                  _full_spec((1, W)), _full_spec((1, W)),
                  _full_spec((KP, CW)), _full_spec((1, CW)), _full_spec((1, CW)), _full_spec((1, CW)),
                  _full_spec((W, W))] + [hs] * 12 + [hl] * 3 + [latn(GP), latn(CW)],
        out_specs=[pl.BlockSpec((1, TT, CP), lambda b, j: (b, j, 0))] + [_full_spec(s.shape) for s in small_shapes],
        out_shape=[jax.ShapeDtypeStruct((B, T, CP), F32)] + small_shapes,
        compiler_params=_params(("arbitrary", "arbitrary")),
    )(p, p, p, mup, mun, w0, w2p, a0, a2p, k_k, k_a, cw, cb, clw, clb, e_w, *sf, *sb, *ro)


def _in_proj_bwd(dpz, xcat, modt, g1, w_in_p, mup, mun, dx1, cfg):
    B, T, TX, D, TT, SP, CP, JC = cfg.B, cfg.T, cfg.TX, cfg.D, cfg.TT, cfg.SP, cfg.CP, cfg.JC

    def body(d_ref, prev_ref, next_ref, x_ref, mod_ref, g_ref, w_ref, mup_ref, mun_ref, dx1_ref,
             gx_ref, dp_ref, dmod_ref, dg_ref):
        b, j = pl.program_id(0), pl.program_id(1)
        has_prev, has_next = _halo_flags(j, cfg)
        mp, mn = mup_ref[...], mun_ref[...]
        drw = d_ref[0, :, 0:SP]
        dprev, dnext = _shift_rows(drw, prev_ref[0, SUBLANES - 1:SUBLANES, :] * has_prev,
                                   next_ref[0, 0:1, :] * has_next)
        dz = drw * (1.0 - mp - mn) + mp * dnext + mn * dprev
        dpb = jnp.concatenate([dz, d_ref[0, :, SP:]], axis=-1).astype(BF16)
        dp_ref[0] = dpb
        dh = _bdot_nt(dpb, w_ref[...])
        _, vjp_h = jax.vjp(_rms_mod, x_ref[0], g_ref[...], mod_ref[0, 0, 0:1, :], mod_ref[0, 0, 1:2, :])
        dx, dg, dsh, dsc = vjp_h(dh)
        dmod_ref[0, 0] = jnp.concatenate([dsh, dsc], axis=0)
        _acc(dg_ref, dg, jnp.logical_and(b == 0, j == 0))

        @pl.when(j >= JC)
        def _():
            gx_ref[0] = dx + dx1_ref[0]

    prev, nxt = _halo_specs(cfg, SP)
    lat = pl.BlockSpec((1, TT, D), lambda b, j: (b, jnp.maximum(j - JC, 0), 0))
    return pl.pallas_call(
        body, name="in_proj_bwd", grid=(B, cfg.J),
        in_specs=[pl.BlockSpec((1, TT, CP), lambda b, j: (b, j, 0)), prev, nxt,
                  pl.BlockSpec((1, TT, D), lambda b, j: (b, j, 0)),
                  pl.BlockSpec((1, 1, 2, D), lambda b, j: (b, j, 0, 0)),
                  _full_spec((1, D)), _full_spec((D, CP)), _full_spec((1, SP)), _full_spec((1, SP)), lat],
        out_specs=[lat, pl.BlockSpec((1, TT, CP), lambda b, j: (b, j, 0)),
                   pl.BlockSpec((1, 1, 2, D), lambda b, j: (b, j, 0, 0)), _full_spec((1, D))],
        out_shape=[jax.ShapeDtypeStruct((B, TX, D), F32), jax.ShapeDtypeStruct((B, T, CP), BF16),
                   jax.ShapeDtypeStruct((B, cfg.J, 2, D), F32), jax.ShapeDtypeStruct((1, D), F32)],
        compiler_params=_params(("arbitrary", "arbitrary")),
    )(dpz, dpz, dpz, xcat, modt, g1, w_in_p, mup, mun, dx1)


def _pick_tile(n, pref):
    for t in pref:
        if n % t == 0:
            return t
    return n


def _grad_matmul(a, g, name):
    K, M = a.shape
    N = g.shape[1]
    tm = _pick_tile(M, (512, 256, 128))
    tn = _pick_tile(N, (1024, 768, 512, 256, 128))
    tk = _pick_tile(K, (2048, 1024, 512, 256, 128, 64))
    nk = K // tk

    def body(a_ref, g_ref, o_ref):
        k = pl.program_id(2)
        _acc(o_ref, _bdot_tn(a_ref[...], g_ref[...]), k == 0)

    return pl.pallas_call(
        body, name=name, grid=(M // tm, N // tn, nk),
        in_specs=[pl.BlockSpec((tk, tm), lambda i, j, k: (k, i)),
                  pl.BlockSpec((tk, tn), lambda i, j, k: (k, j))],
        out_specs=pl.BlockSpec((tm, tn), lambda i, j, k: (i, j)),
        out_shape=jax.ShapeDtypeStruct((M, N), F32),
        compiler_params=_params(("parallel", "parallel", "arbitrary")),
    )(a, g)


def _ada_fwd(crows, ada_w, ada_b):
    D = crows.shape[1]
    n6 = ada_w.shape[1]
    tn = _pick_tile(n6, (1024, 512, 256, 128))

    def body(c_ref, w_ref, b_ref, s_ref, m_ref):
        s = _silu(c_ref[...])
        s_ref[...] = s
        m_ref[...] = _bdot(s, w_ref[...]) + b_ref[...]

    return pl.pallas_call(
        body, name="ada_fwd", grid=(n6 // tn,),
        in_specs=[_full_spec((SUBLANES, D)), pl.BlockSpec((D, tn), lambda i: (0, i)),
                  pl.BlockSpec((1, tn), lambda i: (0, i))],
        out_specs=[_full_spec((SUBLANES, D)), pl.BlockSpec((SUBLANES, tn), lambda i: (0, i))],
        out_shape=[jax.ShapeDtypeStruct((SUBLANES, D), F32), jax.ShapeDtypeStruct((SUBLANES, n6), F32)],
        compiler_params=_params(("arbitrary",)),
    )(crows, ada_w, ada_b)


def _ada_bwd(s_all, g_all, g_mine, c_ctx, ada_w, nb):
    D = s_all.shape[1]
    n6 = g_all.shape[1]
    ns = g_mine.shape[1]

    def body(s_ref, g_ref, gm_ref, c_ref, w_ref, dw_ref, db_ref, dc_ref):
        g = g_ref[...]
        dw_ref[...] = _bdot_tn(s_ref[...], gm_ref[...])
        db_ref[...] = jnp.sum(g, axis=0, keepdims=True)
        rows = lax.broadcasted_iota(jnp.int32, (g.shape[0], 1), 0)
        gc = jnp.sum(jnp.where(rows % SUBLANES == nb, g, 0.0), axis=0, keepdims=True)
        ds = _bdot_nt(gc, w_ref[...])
        c = c_ref[...]
        sg = _sigmoid(c)
        dc_ref[...] = ds * (sg + c * sg * (1.0 - sg))

    return pl.pallas_call(
        body, name="ada_bwd",
        out_shape=[jax.ShapeDtypeStruct((D, ns), F32), jax.ShapeDtypeStruct((1, n6), F32),
                   jax.ShapeDtypeStruct((1, D), F32)],
        compiler_params=_params(),
    )(s_all, g_all, g_mine, c_ctx, ada_w)


def _adamw(parts, w, m, v, name):
    P, R, C = parts.shape
    tr = _pick_tile(R, (256, 128, 64, 32, 16, 8))

    def body(p_ref, w_ref, m_ref, v_ref, g_ref, d_ref, nm_ref, nv_ref):
        g = p_ref[0].astype(F32)
        for i in range(1, P):
            g = g + p_ref[i].astype(F32)
        nm = ADAM_B1 * m_ref[...] + (1.0 - ADAM_B1) * g
        nv = ADAM_B2 * v_ref[...] + (1.0 - ADAM_B2) * (g * g)
        m_hat = nm / (1.0 - ADAM_B1 ** ADAM_STEP)
        v_hat = nv / (1.0 - ADAM_B2 ** ADAM_STEP)
        g_ref[...] = g
        d_ref[...] = -ADAM_LR * (m_hat / (jnp.sqrt(v_hat) + ADAM_EPS) + ADAM_WD * w_ref[...])
        nm_ref[...] = nm
        nv_ref[...] = nv

    blk = pl.BlockSpec((tr, C), lambda i: (i, 0))
    out = jax.ShapeDtypeStruct((R, C), F32)
    return pl.pallas_call(
        body, name=name, grid=(R // tr,),
        in_specs=[pl.BlockSpec((P, tr, C), lambda i: (0, i, 0)), blk, blk, blk],
        out_specs=[blk] * 4, out_shape=[out] * 4,
        compiler_params=_params(("parallel",)),
    )(parts, w, m, v)


def _local_step(cfg, x, c, ctx, tgt, fw):
    B, D, W, CW, JC, T, TX = cfg.B, cfg.D, cfg.W, cfg.CW, cfg.JC, cfg.T, cfg.TX
    e_w = _block_ones(W)
    e128 = _block_ones(LANES)
    e256 = jnp.concatenate([e128, e128], axis=0)
    row = lambda a: a.reshape(1, -1)

    ada_wb = fw["ada_w"].astype(BF16)
    w_in_p = _pad_cols(fw["w_in"], cfg).astype(BF16)
    mup = _pad_cols(fw["mu_prev"], cfg, True)
    mun = _pad_cols(fw["mu_next"], cfg, True)
    w0, a0 = row(fw["decay_w0"]), row(fw["iclr_a0"])
    w2p, a2p = _pair_weight(fw["decay_w2"], cfg), _pair_weight(fw["iclr_a2"], cfg)
    cw = jnp.pad(fw["conv_w"], ((0, cfg.KP - cfg.KC), (0, 0)))
    gw2p = jnp.pad(fw["gate_w2"], ((0, cfg.GP - cfg.GR), (0, 0)))
    r_k = row(fw["r_k"])
    w_outb, w1b, w2b = fw["w_out"].astype(BF16), fw["mlp_w1"].astype(BF16), fw["mlp_w2"].astype(BF16)

    crows = jnp.concatenate([c, fw["c_ctx"], jnp.zeros((SUBLANES - B - 1, D), F32)], axis=0)
    s_rows, mods = _ada_fwd(crows, ada_wb, fw["ada_b"])
    mod_x = mods[:B].reshape(B, 6, D)
    mod_c = mods[B].reshape(6, D)
    modt = jnp.concatenate([jnp.broadcast_to(mod_c[None, None, 0:2], (B, JC, 2, D)),
                            jnp.broadcast_to(mod_x[:, None, 0:2], (B, cfg.JX, 2, D))], axis=1)
    mod2, mod345 = mod_x[:, 2:3], mod_x[:, 3:6]

    xcat = jnp.concatenate([ctx, x], axis=1)
    p, hb = _in_proj(xcat, modt, fw["mix_pre_g"], w_in_p, cfg)
    prep_w = (mup, mun, w0, w2p, a0, a2p, fw["k_k"], fw["k_a"], cw, fw["conv_b"], fw["conv_ln_w"],
              fw["conv_ln_b"], e_w)
    r, v, kk, w_f, kd_f, b_f, w_b, kd_b, b_b, gd, conv = _mix_prep(p, *prep_w, cfg)
    flat = lambda a: a.reshape(cfg.G, a.shape[2], LANES)
    heads = lambda a: a.reshape(B, cfg.HP, a.shape[1], LANES)
    ops_f = tuple(flat(a) for a in (r, w_f, kd_f, v, kk, b_f))
    ops_b = tuple(flat(a) for a in (r, w_b, kd_b, v, kk, b_b))
    y_f, y_b, hist, s_fin = _scan_fwd(ops_f, ops_b, e128, e256, cfg)
    hist = lax.dynamic_update_slice_in_dim(hist, s_fin[None], cfg.NCH * SCAN_CHUNK, axis=0)
    out_args = (heads(y_f), heads(y_b), kd_f, kd_b, r, v, gd, conv, x, mod2, r_k, gw2p, fw["lnx_w"], fw["lnx_b"],
                w_outb, fw["mix_post_g"], e_w)
    x1 = _mix_out(*out_args, cfg)

    dx1, loss_t, h2b, dpreb, actb, dffb, dmod345, dg3, dg4 = _mlp_fwd_bwd(
        x1, tgt, mod345, fw["mlp_pre_g"], fw["mlp_post_g"], w1b, w2b, cfg)
    (dy, dkb, dr_c, dv_c, dgd, dconv, catb, dmixb, dmod2, dg2, drk, dgw, dlw, dlb) = _mix_out_bwd(
        *out_args, dx1, cfg)
    sf, sb = _scan_bwd(ops_f, ops_b, flat(dy), hist, e128, e256, cfg)
    (dpz, dmup, dmun, dw0, dw2p, da0, da2p, dkk, dka, dcw, dcb, dclw, dclb) = _mix_prep_bwd(
        p, *prep_w, [heads(a) for a in sf], [heads(a) for a in sb], (dr_c, dv_c, dkb, dgd, dconv), cfg)
    grad_x, dpb, dmodt, dg1 = _in_proj_bwd(dpz, xcat, modt, fw["mix_pre_g"], w_in_p, mup, mun, dx1, cfg)

    tokens = lambda a: a.reshape(-1, a.shape[-1])
    d_w_in = _grad_matmul(tokens(hb), tokens(dpb), "grad_w_in")
    d_w_out = _grad_matmul(tokens(catb), tokens(dmixb), "grad_w_out")
    d_w1 = _grad_matmul(tokens(h2b), tokens(dpreb), "grad_mlp_w1")
    d_w2 = _grad_matmul(tokens(actb), tokens(dffb), "grad_mlp_w2")

    grads = {
        "mix_pre_g": dg1, "mix_post_g": dg2, "mlp_pre_g": dg3, "mlp_post_g": dg4,
        "w_in": _unpad_cols(d_w_in, cfg),
        "mu_prev": _unpad_cols(dmup, cfg, True), "mu_next": _unpad_cols(dmun, cfg, True),
        "decay_w0": dw0.reshape(2, W), "decay_w2": _unpair_weight(dw2p, cfg),
        "iclr_a0": da0.reshape(2, W), "iclr_a2": _unpair_weight(da2p, cfg),
        "k_k": dkk, "k_a": dka, "r_k": drk.reshape(fw["r_k"].shape),
        "gate_w2": dgw[:cfg.GR], "lnx_w": dlw, "lnx_b": dlb,
        "conv_w": dcw[:cfg.KC], "conv_b": dcb, "conv_ln_w": dclw, "conv_ln_b": dclb,
        "w_out": d_w_out, "mlp_w1": d_w1, "mlp_w2": d_w2,
    }
    dmod_x = jnp.concatenate([jnp.sum(dmodt[:, JC:], axis=1), dmod2, dmod345], axis=1).reshape(B, 6 * D)
    dmod_c = jnp.concatenate([jnp.sum(dmodt[:, :JC], axis=(0, 1)), jnp.zeros((4, D), F32)], axis=0).reshape(1, 6 * D)
    g_rows = jnp.concatenate([dmod_x, dmod_c, jnp.zeros((SUBLANES - B - 1, 6 * D), F32)], axis=0)
    return loss_t, grad_x, grads, s_rows, g_rows


def _my_index():
    return 4 * lax.axis_index("x") + 2 * lax.axis_index("y") + lax.axis_index("c")


def _gather_two_level(arrays, name):
    n = len(arrays)
    out_shape = [jax.ShapeDtypeStruct((N_DEV,) + a.shape, a.dtype) for a in arrays]

    def body(*refs):
        ins, outs = refs[:n], refs[n:2 * n]
        send_sems, recv_sems, local_sems = refs[2 * n:]
        x, y, c = lax.axis_index("x"), lax.axis_index("y"), lax.axis_index("c")
        index = lambda px, py, pc: 4 * px + 2 * py + pc
        sibling = (x, y, 1 - c)
        chips = [(1 - x, y), (x, 1 - y), (1 - x, 1 - y)]

        def copy(a, k, block, to, src=None):
            dst = outs[a].at[index(*block)]
            return pltpu.make_async_remote_copy(
                src_ref=dst if src is None else src, dst_ref=dst,
                send_sem=send_sems.at[k, a], recv_sem=recv_sems.at[k, a],
                device_id=to, device_id_type=pl.DeviceIdType.MESH)

        local = [pltpu.make_async_copy(ins[a], outs[a].at[index(x, y, c)], local_sems.at[a]) for a in range(n)]
        for cp in local:
            cp.start()
        first = []
        for a in range(n):
            first.append(copy(a, 0, (x, y, c), sibling, src=ins[a]))
            first += [copy(a, 1 + j, (x, y, c), (*chip, c), src=ins[a]) for j, chip in enumerate(chips)]
        for cp in first:
            cp.start()
        passed = []
        for j, chip in enumerate(chips):
            for a in range(n):
                copy(a, 1 + j, (*chip, c), (x, y, c)).wait_recv()
                fwd = copy(a, 4 + j, (*chip, c), sibling)
                fwd.start()
                passed.append(fwd)
        for a in range(n):
            copy(a, 0, sibling, (x, y, c)).wait_recv()
            for j, chip in enumerate(chips):
                copy(a, 4 + j, (*chip, 1 - c), (x, y, c)).wait_recv()
        for cp in first + passed:
            cp.wait_send()
        for cp in local:
            cp.wait()

    hbm = pl.BlockSpec(memory_space=pltpu.HBM)
    return pl.pallas_call(
        body, name=name, out_shape=out_shape,
        in_specs=[hbm] * n, out_specs=[hbm] * n,
        scratch_shapes=[pltpu.SemaphoreType.DMA((N_DEV - 1, n)), pltpu.SemaphoreType.DMA((N_DEV - 1, n)),
                        pltpu.SemaphoreType.DMA((n,))],
    )(*arrays)


def _exchange(arrays, scatter, name):
    n = len(arrays)
    out_shape = [jax.ShapeDtypeStruct(a.shape if s else (N_DEV,) + a.shape, a.dtype)
                 for a, s in zip(arrays, scatter)]

    def body(*refs):
        ins, outs = refs[:n], refs[n:2 * n]
        send_sems, recv_sems, local_sems = refs[2 * n:]
        x, y, c = lax.axis_index("x"), lax.axis_index("y"), lax.axis_index("c")
        me = 4 * x + 2 * y + c
        flip = lambda v, f: 1 - v if f else v

        def piece(a, dest):
            return ins[a].at[dest] if scatter[a] else ins[a]

        local = [pltpu.make_async_copy(piece(a, me), outs[a].at[me], local_sems.at[a]) for a in range(n)]
        for cp in local:
            cp.start()
        sends, recvs = [], []
        for k in range(1, N_DEV):
            fx, fy, fc = (k >> 2) & 1, (k >> 1) & 1, k & 1
            peer = (flip(x, fx), flip(y, fy), flip(c, fc))
            peer_idx = 4 * peer[0] + 2 * peer[1] + peer[2]
            for a in range(n):
                sends.append(pltpu.make_async_remote_copy(
                    src_ref=piece(a, peer_idx), dst_ref=outs[a].at[me],
                    send_sem=send_sems.at[k - 1, a], recv_sem=recv_sems.at[k - 1, a],
                    device_id=peer, device_id_type=pl.DeviceIdType.MESH))
                recvs.append(pltpu.make_async_remote_copy(
                    src_ref=piece(a, peer_idx), dst_ref=outs[a].at[peer_idx],
                    send_sem=send_sems.at[k - 1, a], recv_sem=recv_sems.at[k - 1, a],
                    device_id=peer, device_id_type=pl.DeviceIdType.MESH))
        for cp in sends:
            cp.start()
        for cp in recvs:
            cp.wait_recv()
        for cp in sends:
            cp.wait_send()
        for cp in local:
            cp.wait()

    hbm = pl.BlockSpec(memory_space=pltpu.HBM)
    return pl.pallas_call(
        body, name=name, out_shape=out_shape,
        in_specs=[hbm] * n, out_specs=[hbm] * n,
        scratch_shapes=[pltpu.SemaphoreType.DMA((N_DEV - 1, n)), pltpu.SemaphoreType.DMA((N_DEV - 1, n)),
                        pltpu.SemaphoreType.DMA((n,))],
    )(*arrays)


def _pack(parts):
    flat = jnp.concatenate([p.reshape(-1) for p in parts])
    total = _round_up(flat.shape[0], SUBLANES * LANES)
    return jnp.pad(flat, (0, total - flat.shape[0])).reshape(-1, LANES)


def _unpack(buf, shapes):
    flat = buf.reshape(-1)
    out, pos = [], 0
    for s in shapes:
        n = int(np.prod(s))
        out.append(flat[pos:pos + n].reshape(s))
        pos += n
    return out


_SHARDED_SMALL = ("decay_w0", "decay_w2", "iclr_a0", "iclr_a2", "gate_w2", "conv_w")
_REPLICATED = ("mix_pre_g", "mix_post_g", "mlp_pre_g", "mlp_post_g", "mu_prev", "mu_next", "k_k", "k_a", "r_k",
               "lnx_w", "lnx_b", "conv_b", "conv_ln_w", "conv_ln_b")
_ADA_SMALL = ("c_ctx", "ada_b")
_WEIGHTS = ("c_ctx", "ada_w", "ada_b", "mix_pre_g", "mix_post_g", "mlp_pre_g", "mlp_post_g", "w_in", "mu_prev",
            "mu_next", "decay_w0", "decay_w2", "iclr_a0", "iclr_a2", "k_k", "k_a", "r_k", "gate_w2", "lnx_w", "lnx_b",
            "conv_w", "conv_b", "conv_ln_w", "conv_ln_b", "w_out", "mlp_w1", "mlp_w2")
_INPUTS = ("x", "c", "ctx") + _WEIGHTS + ("loss_target",) + tuple("m_" + n for n in _WEIGHTS) + tuple(
    "v_" + n for n in _WEIGHTS)


def _cols_to_blocks(a):
    a = a.reshape(a.shape[:-1] + (N_DEV, a.shape[-1] // N_DEV))
    return jnp.moveaxis(a, -2, 0)


def _blocks_to_cols(a):
    a = jnp.moveaxis(a, 0, -2)
    return a.reshape(a.shape[:-2] + (a.shape[-2] * a.shape[-1],))


def kernel(x, c, ctx, c_ctx, ada_w, ada_b, mix_pre_g, mix_post_g, mlp_pre_g, mlp_post_g, w_in, mu_prev, mu_next, decay_w0, decay_w2, iclr_a0, iclr_a2, k_k, k_a, r_k, gate_w2, lnx_w, lnx_b, conv_w, conv_b, conv_ln_w, conv_ln_b, w_out, mlp_w1, mlp_w2, loss_target, m_c_ctx, m_ada_w, m_ada_b, m_mix_pre_g, m_mix_post_g, m_mlp_pre_g, m_mlp_post_g, m_w_in, m_mu_prev, m_mu_next, m_decay_w0, m_decay_w2, m_iclr_a0, m_iclr_a2, m_k_k, m_k_a, m_r_k, m_gate_w2, m_lnx_w, m_lnx_b, m_conv_w, m_conv_b, m_conv_ln_w, m_conv_ln_b, m_w_out, m_mlp_w1, m_mlp_w2, v_c_ctx, v_ada_w, v_ada_b, v_mix_pre_g, v_mix_post_g, v_mlp_pre_g, v_mlp_post_g, v_w_in, v_mu_prev, v_mu_next, v_decay_w0, v_decay_w2, v_iclr_a0, v_iclr_a2, v_k_k, v_k_a, v_r_k, v_gate_w2, v_lnx_w, v_lnx_b, v_conv_w, v_conv_b, v_conv_ln_w, v_conv_ln_b, v_w_out, v_mlp_w1, v_mlp_w2):
    given = dict(zip(_INPUTS, (x, c, ctx, c_ctx, ada_w, ada_b, mix_pre_g, mix_post_g, mlp_pre_g, mlp_post_g, w_in, mu_prev, mu_next, decay_w0, decay_w2, iclr_a0, iclr_a2, k_k, k_a, r_k, gate_w2, lnx_w, lnx_b, conv_w, conv_b, conv_ln_w, conv_ln_b, w_out, mlp_w1, mlp_w2, loss_target, m_c_ctx, m_ada_w, m_ada_b, m_mix_pre_g, m_mix_post_g, m_mlp_pre_g, m_mlp_post_g, m_w_in, m_mu_prev, m_mu_next, m_decay_w0, m_decay_w2, m_iclr_a0, m_iclr_a2, m_k_k, m_k_a, m_r_k, m_gate_w2, m_lnx_w, m_lnx_b, m_conv_w, m_conv_b, m_conv_ln_w, m_conv_ln_b, m_w_out, m_mlp_w1, m_mlp_w2, v_c_ctx, v_ada_w, v_ada_b, v_mix_pre_g, v_mix_post_g, v_mlp_pre_g, v_mlp_post_g, v_w_in, v_mu_prev, v_mu_next, v_decay_w0, v_decay_w2, v_iclr_a0, v_iclr_a2, v_k_k, v_k_a, v_r_k, v_gate_w2, v_lnx_w, v_lnx_b, v_conv_w, v_conv_b, v_conv_ln_w, v_conv_ln_b, v_w_out, v_mlp_w1, v_mlp_w2)))
    loc = {}
    for pre in ("", "m_", "v_"):
        for n in _WEIGHTS:
            a = given[pre + n]
            a = a.reshape(1, -1) if n == "c_ctx" else a[0]
            loc[pre + n] = a.reshape(1, -1) if a.ndim == 1 else a
    B, TX, D = x.shape
    W, CW = loc["k_k"].shape[1], loc["conv_b"].shape[1]
    cfg = _Cfg(B, TX, ctx.shape[1], D, W, CW, loc["decay_w2"].shape[1], loc["gate_w2"].shape[0],
               loc["conv_w"].shape[0], loc["mlp_w1"].shape[1] * N_DEV)
    me = _my_index()

    small_shapes = [loc[n].shape for n in _SHARDED_SMALL]
    got = _gather_two_level(
        [loc["ada_w"].astype(BF16), loc["w_in"].astype(BF16), loc["w_out"].astype(BF16),
         loc["mlp_w1"].astype(BF16), loc["mlp_w2"].astype(BF16), _pack([loc[n] for n in _SHARDED_SMALL])],
        "gather_weights")
    fw = {n: loc[n] for n in _REPLICATED + _ADA_SMALL}
    fw["ada_w"] = _blocks_to_cols(got[0])
    fw["w_in"] = _blocks_to_cols(got[1])
    fw["w_out"] = got[2].reshape(-1, D)
    fw["mlp_w1"] = _blocks_to_cols(got[3])
    fw["mlp_w2"] = got[4].reshape(-1, D)
    per_dev = [_unpack(got[5][i], small_shapes) for i in range(N_DEV)]
    for j, n in enumerate(_SHARDED_SMALL):
        fw[n] = jnp.concatenate([per_dev[i][j] for i in range(N_DEV)], axis=-1)

    loss_t, grad_x, grads, s_rows, g_rows = _local_step(cfg, x, c, ctx, loss_target, fw)
    loss = lax.psum(jnp.sum(loss_t[:, :, 0, 0]), ("x", "y", "c"))

    small_blocks = jnp.stack([_pack([_cols_to_blocks(grads[n])[i] for n in _SHARDED_SMALL]) for i in range(N_DEV)])
    sent = _exchange(
        [_cols_to_blocks(grads["w_in"]).astype(BF16), grads["w_out"].reshape(N_DEV, -1, D).astype(BF16),
         _cols_to_blocks(grads["mlp_w1"]).astype(BF16), grads["mlp_w2"].reshape(N_DEV, -1, D).astype(BF16),
         small_blocks,
         _pack([grads[n] for n in _REPLICATED]), s_rows, g_rows],
        [True] * 5 + [False] * 3, "exchange_grads")
    s_all = sent[6].reshape(N_DEV * SUBLANES, D)
    g_all = sent[7].reshape(N_DEV * SUBLANES, 6 * D)
    ns = 6 * D // N_DEV
    g_mine = lax.dynamic_slice_in_dim(g_all, me * ns, ns, axis=1)
    d_ada_w, d_ada_b, d_c_ctx = _ada_bwd(s_all, g_all, g_mine, loc["c_ctx"], fw["ada_w"], B)

    res = {}

    def update(name, parts):
        res[name] = _adamw(parts, loc[name], loc["m_" + name], loc["v_" + name], "adamw_" + name)

    update("w_in", sent[0])
    update("w_out", sent[1])
    update("mlp_w1", sent[2])
    update("mlp_w2", sent[3])
    update("ada_w", d_ada_w[None])

    def update_packed(names, parts, tag):
        shapes = [loc[n].shape for n in names]
        packed = _adamw(parts, *[_pack([loc[pre + n] for n in names]) for pre in ("", "m_", "v_")], "adamw_" + tag)
        unpacked = [_unpack(p, shapes) for p in packed]
        for j, n in enumerate(names):
            res[n] = tuple(u[j] for u in unpacked)

    update_packed(_SHARDED_SMALL, sent[4], "sharded_small")
    update_packed(_REPLICATED, sent[5], "replicated")
    update_packed(_ADA_SMALL, _pack([d_c_ctx, d_ada_b])[None], "ada_small")

    outs = [loss, grad_x]
    for k in range(4):
        for n in _WEIGHTS:
            outs.append(res[n][k].reshape(given[n].shape))
    return tuple(outs)
```

```python
import functools

import numpy as np
import jax
import jax.numpy as jnp
from jax import lax
from jax.experimental import pallas as pl
from jax.experimental.pallas import tpu as pltpu

F32 = jnp.float32
BF16 = jnp.bfloat16

EPS_RMS = 1e-6
EPS_LN = 1e-5
EPS_GN = 64e-5
LINE = 64
HEAD = 64
LANES = 128
SUBLANES = 8
SCAN_CHUNK = 16
N_DEV = 8
VMEM_LIMIT = 56 * 1024 * 1024

ADAM_LR = 0.001
ADAM_B1 = 0.9
ADAM_B2 = 0.999
ADAM_EPS = 1e-08
ADAM_WD = 0.01
ADAM_STEP = 10


def _round_up(n, m):
    return (n + m - 1) // m * m


def _params(semantics=None, vmem=VMEM_LIMIT):
    return pltpu.CompilerParams(dimension_semantics=semantics, vmem_limit_bytes=vmem)


def _bdot(a, b):
    return jnp.dot(a.astype(BF16), b.astype(BF16), preferred_element_type=F32)


def _bdot_nt(a, b):
    return lax.dot_general(a.astype(BF16), b.astype(BF16), (((1,), (1,)), ((), ())),
                           preferred_element_type=F32)


def _bdot_tn(a, b):
    return lax.dot_general(a.astype(BF16), b.astype(BF16), (((0,), (0,)), ((), ())),
                           preferred_element_type=F32)


@jax.custom_vjp
def _mm(a, b):
    return _bdot(a, b)


def _mm_fwd(a, b):
    return _bdot(a, b), (a, b)


def _mm_bwd(res, g):
    a, b = res
    return _bdot_nt(g, b), _bdot_tn(a, g)


_mm.defvjp(_mm_fwd, _mm_bwd)


def _seg_sum_raw(x, e):
    hi = x.astype(BF16)
    lo = (x - hi.astype(F32)).astype(BF16)
    return (jnp.dot(hi, e, preferred_element_type=F32)
            + jnp.dot(lo, e, preferred_element_type=F32))


@jax.custom_vjp
def _seg_sum(x, e):
    return _seg_sum_raw(x, e)


def _seg_sum_fwd(x, e):
    return _seg_sum_raw(x, e), e


def _seg_sum_bwd(e, g):
    return _seg_sum_raw(g, e), None


_seg_sum.defvjp(_seg_sum_fwd, _seg_sum_bwd)


def _block_ones(n, seg=HEAD):
    i = np.arange(n) // seg
    return jnp.asarray((i[:, None] == i[None, :]).astype(np.float32), dtype=BF16)


def _rms(xv, g):
    ms = jnp.mean(xv * xv, axis=-1, keepdims=True)
    return xv * lax.rsqrt(ms + EPS_RMS) * g


def _rms_mod(xv, g, shift, scale):
    return _rms(xv, g) * (1.0 + scale) + shift


def _sigmoid(z):
    return 1.0 / (1.0 + jnp.exp(-z))


def _silu(z):
    return z * _sigmoid(z)


def _softplus(z):
    return jnp.maximum(z, 0.0) + jnp.log(1.0 + jnp.exp(-jnp.abs(z)))


class _Cfg:
    def __init__(self, B, TX, TC, D, W, CW, R, GR, KC, F):
        self.B, self.TX, self.TC, self.D = B, TX, TC, D
        self.W, self.CW, self.R, self.GR, self.KC, self.F = W, CW, R, GR, KC, F
        self.T = TX + TC
        self.TT = min(256, TC)
        assert TC % self.TT == 0 and TX % self.TT == 0 and self.TT % LINE == 0
        self.JC = TC // self.TT
        self.JX = TX // self.TT
        self.J = self.JC + self.JX
        self.HP = W // LANES
        self.G = B * self.HP
        self.PW = _round_up(2 * R, LANES)
        self.GP = _round_up(GR, LANES)
        self.SP = 3 * W + 2 * self.PW + self.GP
        self.CP = self.SP + 2 * CW
        self.KP = _round_up(KC, SUBLANES)
        assert self.T % SCAN_CHUNK == 0 and TC % SCAN_CHUNK == 0
        self.NCH = self.T // SCAN_CHUNK
        self.NCC = TC // SCAN_CHUNK
        W_, R_ = W, R
        segs = [(0, 3 * W_, 0),
                (3 * W_, 2 * R_, 3 * W_),
                (3 * W_ + 2 * R_, 2 * R_, 3 * W_ + self.PW),
                (3 * W_ + 4 * R_, GR, 3 * W_ + 2 * self.PW),
                (3 * W_ + 4 * R_ + GR, 2 * CW, self.SP)]
        self.col_segs = segs
        self.shift_cols = 3 * W_ + 4 * R_ + GR
        self.in_cols = self.shift_cols + 2 * CW


def _pad_cols(a, cfg, upto_shift=False):
    width = cfg.SP if upto_shift else cfg.CP
    pieces, pos = [], 0
    for src, n, dst in cfg.col_segs:
        if upto_shift and dst >= cfg.SP:
            break
        if dst > pos:
            pieces.append(jnp.zeros(a.shape[:-1] + (dst - pos,), a.dtype))
        pieces.append(a[..., src:src + n])
        pos = dst + n
    if width > pos:
        pieces.append(jnp.zeros(a.shape[:-1] + (width - pos,), a.dtype))
    return jnp.concatenate(pieces, axis=-1)


def _unpad_cols(a, cfg, upto_shift=False):
    pieces = []
    for src, n, dst in cfg.col_segs:
        if upto_shift and dst >= cfg.SP:
            break
        pieces.append(a[..., dst:dst + n])
    return jnp.concatenate(pieces, axis=-1)


def _pair_weight(w2, cfg):
    R, W = cfg.R, cfg.W
    out = jnp.zeros((cfg.PW, 2 * W), w2.dtype)
    out = out.at[0:R, 0:W].set(w2[0])
    out = out.at[R:2 * R, W:2 * W].set(w2[1])
    return out


def _unpair_weight(g, cfg):
    R, W = cfg.R, cfg.W
    return jnp.stack([g[0:R, 0:W], g[R:2 * R, W:2 * W]])


def _row_ids(n):
    return lax.broadcasted_iota(jnp.int32, (n, 1), 0)


def _shift_rows(z, prev_row, next_row):
    n = z.shape[0]
    rows = _row_ids(n)
    zp = jnp.where(rows == 0, prev_row, pltpu.roll(z, 1, 0))
    zn = jnp.where(rows == n - 1, next_row, pltpu.roll(z, n - 1, 0))
    return zp, zn


def _line_shift(u, d):
    if d == 0:
        return u
    n = u.shape[0]
    lt = _row_ids(n) % LINE
    ok = jnp.logical_and(lt + d >= 0, lt + d < LINE)
    return jnp.where(ok, pltpu.roll(u, (-d) % n, 0), 0.0)


def _conv_tables(cw, kc):
    pad = kc // 2
    t = np.arange(LINE)[None, :]
    d = (np.arange(kc) - pad)[:, None]
    fwd = ((t + d >= 0) & (t + d < LINE)).astype(np.float32)
    bwd = ((t - d >= 0) & (t - d < LINE)).astype(np.float32)
    w = cw[:kc, None, :]
    return jnp.asarray(fwd)[:, :, None] * w, jnp.asarray(bwd)[:, :, None] * w


def _conv_lines(src_ref, wm_ref, dst_ref, kc, transpose):
    pad = kc // 2
    n, width = src_ref.shape
    for l in range(n // LINE):
        for b in range(width // LANES):
            rs, cs = slice(l * LINE, (l + 1) * LINE), slice(b * LANES, (b + 1) * LANES)
            tile = src_ref[rs, cs]
            acc = jnp.zeros_like(tile)
            for i in range(kc):
                d = (pad - i) if transpose else (i - pad)
                acc = acc + pltpu.roll(tile, (-d) % LINE, 0) * wm_ref[i, :, cs]
            dst_ref[rs, cs] = acc


def _rwkv_prep(rw, w0, w2p, a0, a2p, k_k, k_a, e, cfg):
    W, PW = cfg.W, cfg.PW
    r = rw[:, 0:W]
    k = rw[:, W:2 * W]
    v = rw[:, 2 * W:3 * W]
    wdp = rw[:, 3 * W:3 * W + PW]
    adp = rw[:, 3 * W + PW:3 * W + 2 * PW]
    wl = w0 + _mm(jnp.tanh(wdp), w2p)
    w_log = -_softplus(-wl) - 0.5
    decay = jnp.exp(-jnp.exp(w_log))
    iclr = _sigmoid(a0 + _mm(adp, a2p))
    kkr = k * k_k
    nrm = jnp.sqrt(_seg_sum(kkr * kkr, e))
    kk = kkr / jnp.maximum(nrm, 1e-12)
    outs = [r, v, kk]
    for d in range(2):
        ic = iclr[:, d * W:(d + 1) * W]
        outs += [decay[:, d * W:(d + 1) * W], k * (1.0 + (ic - 1.0) * k_a), kk * ic]
    return tuple(outs)


def _glu(cv, cfg):
    return cv[:, :cfg.CW] * _sigmoid(cv[:, cfg.CW:])


def _conv_post(y, cb, lw, lb):
    yf = y + cb
    mu = jnp.mean(yf, axis=-1, keepdims=True)
    var = jnp.mean(jnp.square(yf - mu), axis=-1, keepdims=True)
    return _silu((yf - mu) * lax.rsqrt(var + EPS_LN) * lw + lb)


def _readout(y, kbar, r, v, gd, r_k, gw2, lnx_w, lnx_b, e):
    inv = 1.0 / HEAD
    mu = _seg_sum(y, e) * inv
    yc = y - mu
    var = _seg_sum(yc * yc, e) * inv
    yn = yc * lax.rsqrt(var + EPS_GN) * lnx_w + lnx_b
    bonus = _seg_sum(r * kbar * r_k, e) * v
    g = _mm(_sigmoid(gd), gw2)
    return (yn + bonus) * g


def _post_res(xv, mix, gate, g):
    return xv + gate * _rms(mix, g)


def _head_spec(cfg, tmap):
    return pl.BlockSpec((1, cfg.HP, cfg.TT, LANES), lambda b, j: (b, 0, tmap(j), 0))


def _full_spec(shape):
    n = len(shape)
    return pl.BlockSpec(shape, lambda *_: (0,) * n)


def _to_heads(ref, val, cfg):
    for hp in range(cfg.HP):
        ref[0, hp] = val[:, hp * LANES:(hp + 1) * LANES]


def _from_heads(ref, cfg):
    return jnp.concatenate([ref[0, hp] for hp in range(cfg.HP)], axis=-1)


def _in_proj(xcat, modt, g1, w_in_p, cfg):
    B, T, D, TT, CP = cfg.B, cfg.T, cfg.D, cfg.TT, cfg.CP

    def body(x_ref, mod_ref, g_ref, w_ref, p_ref, h_ref):
        h = _rms_mod(x_ref[0], g_ref[...], mod_ref[0, 0, 0:1, :], mod_ref[0, 0, 1:2, :])
        hb = h.astype(BF16)
        h_ref[0] = hb
        p_ref[0] = jnp.dot(hb, w_ref[...], preferred_element_type=F32)

    return pl.pallas_call(
        body, name="in_proj", grid=(B, cfg.J),
        in_specs=[pl.BlockSpec((1, TT, D), lambda b, j: (b, j, 0)),
                  pl.BlockSpec((1, 1, 2, D), lambda b, j: (b, j, 0, 0)),
                  _full_spec((1, D)), _full_spec((D, CP))],
        out_specs=[pl.BlockSpec((1, TT, CP), lambda b, j: (b, j, 0)),
                   pl.BlockSpec((1, TT, D), lambda b, j: (b, j, 0))],
        out_shape=[jax.ShapeDtypeStruct((B, T, CP), F32), jax.ShapeDtypeStruct((B, T, D), BF16)],
        compiler_params=_params(("parallel", "parallel")),
    )(xcat, modt, g1, w_in_p)


def _halo_specs(cfg, width):
    per = cfg.TT // SUBLANES
    last = cfg.T // SUBLANES - 1
    prev = pl.BlockSpec((1, SUBLANES, width), lambda b, j: (b, jnp.maximum(j * per - 1, 0), 0))
    nxt = pl.BlockSpec((1, SUBLANES, width), lambda b, j: (b, jnp.minimum((j + 1) * per, last), 0))
    return prev, nxt


def _halo_flags(j, cfg):
    has_prev = jnp.logical_and(j != 0, j != cfg.JC).astype(F32)
    has_next = jnp.logical_and(j != cfg.JC - 1, j != cfg.J - 1).astype(F32)
    return has_prev, has_next


def _shifted(p_ref, prev_ref, next_ref, mup, mun, j, cfg):
    SP = cfg.SP
    has_prev, has_next = _halo_flags(j, cfg)
    z = p_ref[0][:, :SP]
    zp, zn = _shift_rows(z, prev_ref[0, SUBLANES - 1:SUBLANES, :] * has_prev, next_ref[0, 0:1, :] * has_next)
    return z, zp, zn, z + mup * (zp - z) + mun * (zn - z)


def _mix_prep(p, mup, mun, w0, w2p, a0, a2p, k_k, k_a, wm, cb, clw, clb, e_w, cfg):
    B, T, TT, SP, CP, W, CW, HP, JC = cfg.B, cfg.T, cfg.TT, cfg.SP, cfg.CP, cfg.W, cfg.CW, cfg.HP, cfg.JC

    def body(p_ref, prev_ref, next_ref, mup_ref, mun_ref, w0_ref, w2_ref, a0_ref, a2_ref, kk_ref, ka_ref,
             wm_ref, cb_ref, clw_ref, clb_ref, e_ref, *rest):
        outs, u_ref = rest[:12], rest[12]
        j = pl.program_id(1)
        _, _, _, rw = _shifted(p_ref, prev_ref, next_ref, mup_ref[...], mun_ref[...], j, cfg)
        vals = _rwkv_prep(rw, w0_ref[...], w2_ref[...], a0_ref[...], a2_ref[...], kk_ref[...], ka_ref[...],
                          e_ref[...], cfg)
        for ref, val in zip(outs[:9], vals):
            _to_heads(ref, val, cfg)
        outs[9][0] = rw[:, 3 * W + 2 * cfg.PW:SP]

        @pl.when(j >= JC)
        def _():
            u_ref[...] = _glu(p_ref[0, :, SP:], cfg)
            _conv_lines(u_ref, wm_ref, outs[11].at[0], cfg.KC, False)
            outs[10][0] = _conv_post(outs[11][0], cb_ref[...], clw_ref[...], clb_ref[...])

    prev, nxt = _halo_specs(cfg, SP)
    head = jax.ShapeDtypeStruct((B, HP, T, LANES), F32)
    tile = lambda n: pl.BlockSpec((1, TT, n), lambda b, j: (b, j, 0))
    return pl.pallas_call(
        body, name="mix_prep", grid=(B, cfg.J),
        in_specs=[tile(CP), prev, nxt,
                  _full_spec((1, SP)), _full_spec((1, SP)),
                  _full_spec((1, 2 * W)), _full_spec((cfg.PW, 2 * W)),
                  _full_spec((1, 2 * W)), _full_spec((cfg.PW, 2 * W)),
                  _full_spec((1, W)), _full_spec((1, W)),
                  pl.BlockSpec((cfg.KC, LINE, CW), lambda b, j: (0, 0, 0), pipeline_mode=pl.Buffered(1)),
                  _full_spec((1, CW)), _full_spec((1, CW)), _full_spec((1, CW)),
                  _full_spec((W, W))],
        out_specs=[_head_spec(cfg, lambda j: j)] * 9 + [tile(cfg.GP), tile(CW), tile(CW)],
        out_shape=[head] * 9 + [jax.ShapeDtypeStruct((B, T, cfg.GP), F32),
                                jax.ShapeDtypeStruct((B, T, CW), F32), jax.ShapeDtypeStruct((B, T, CW), F32)],
        scratch_shapes=[pltpu.VMEM((TT, CW), F32)],
        compiler_params=_params(("parallel", "parallel")),
    )(p, p, p, mup, mun, w0, w2p, a0, a2p, k_k, k_a, wm, cb, clw, clb, e_w)


def _chunk_pos(c, reverse, cfg):
    if not reverse:
        return c
    return jnp.where(c < cfg.NCC, cfg.NCC - 1 - c, cfg.NCH - 1 + cfg.NCC - c)


def _diag_mask():
    r = lax.broadcasted_iota(jnp.int32, (HEAD, LANES), 0)
    l = lax.broadcasted_iota(jnp.int32, (HEAD, LANES), 1)
    return (r == l % HEAD).astype(F32)


def _col_lhs(row, diag_b):
    hi = row.astype(BF16)
    lo = (row - hi.astype(F32)).astype(BF16)
    return diag_b * hi, diag_b * lo


def _col_dot(row_list, diag_b, e2):
    n, g = len(row_list), row_list[0].shape[0]
    lhs = jnp.concatenate([jnp.concatenate(_col_lhs(r, diag_b), axis=-1) for r in row_list], axis=0)
    out = jnp.dot(lhs.reshape(n * g * HEAD, 2 * LANES), e2, preferred_element_type=F32)
    return out.reshape(n, g, HEAD, LANES)


def _col_form(row):
    n = row.shape[0]
    t = jnp.swapaxes(jnp.broadcast_to(row, (n, LANES, LANES)), 1, 2)
    lane = lax.broadcasted_iota(jnp.int32, (HEAD, LANES), 1)
    return jnp.where(lane < HEAD, t[:, :HEAD, :], t[:, HEAD:, :])


def _col_both(row, diag_b, e2):
    half = row.shape[0] // 2
    return jnp.concatenate([_col_form(row[:half]), _col_dot([row[half:]], diag_b, e2)[0]], axis=0)


def _both_rows(ins, idx, i):
    return jnp.concatenate([ins[d][idx][:, pl.ds(_tok(i, d == 1), 1), :] for d in range(2)], axis=0)


def _seg_dot(blocks, e):
    n, g = len(blocks), blocks[0].shape[0]
    lhs = jnp.concatenate(blocks, axis=0).reshape(n * g * HEAD, LANES)
    return jnp.dot(lhs, e, preferred_element_type=F32).reshape(n, g, HEAD, LANES)


def _tok(i, reverse):
    return (SCAN_CHUNK - 1 - i) if reverse else i


def _scan_fwd(ops_f, ops_b, e128, e256, cfg):
    G, T, NCH = cfg.G, cfg.T, cfg.NCH
    CH = SCAN_CHUNK
    G2 = 2 * G

    def body(*refs):
        ins = (refs[0:6], refs[6:12])
        e_ref, e2_ref = refs[12], refs[13]
        ys, hist_ref, fin_ref = (refs[14], refs[15]), refs[16], refs[17]
        s_ref, mm_ref = refs[18], refs[19]
        c = pl.program_id(0)

        @pl.when(c == 0)
        def _():
            s_ref[...] = jnp.zeros_like(s_ref)

        diag = _diag_mask()
        diag_b = diag.astype(BF16)
        e, e2 = e_ref[...], e2_ref[...]
        rows = functools.partial(_both_rows, ins)

        mm_ref[0] = _seg_dot([s_ref[...].astype(BF16) * (-rows(4, 0)).astype(BF16)], e)[0]
        mm_ref[1] = _col_both(rows(3, 0), diag_b, e2)

        def step(i, carry):
            nxt = jnp.minimum(i + 1, CH - 1)
            res = []
            for d in range(2):
                sl = slice(d * G, (d + 1) * G)
                row = lambda idx, ii: ins[d][idx][:, pl.ds(_tok(ii, d == 1), 1), :]
                s_old = s_ref[sl]
                hist_ref[i, sl] = s_old
                S = s_old * row(1, i) + mm_ref[0, sl] * row(5, i) + mm_ref[1, sl] * row(2, i)
                s_ref[sl] = S
                sb = S.astype(BF16)
                res.append(_seg_dot([sb * (-row(4, nxt)).astype(BF16), sb * row(0, i).astype(BF16)], e))
            for d in range(2):
                sl = slice(d * G, (d + 1) * G)
                v_next = ins[d][3][:, pl.ds(_tok(nxt, d == 1), 1), :]
                mm_ref[0, sl] = res[d][0]
                mm_ref[1, sl] = _col_form(v_next) if d == 0 else _col_dot([v_next], diag_b, e2)[0]
                ys[d][:, pl.ds(_tok(i, d == 1), 1), :] = jnp.sum(diag * res[d][1], axis=1, keepdims=True)
            return carry

        lax.fori_loop(0, CH, step, 0)
        fin_ref[...] = s_ref[...]

    toks = [pl.BlockSpec((G, CH, LANES), lambda c, rev=rev: (0, _chunk_pos(c, rev, cfg), 0)) for rev in (False, True)]
    y_shape = jax.ShapeDtypeStruct((G, T, LANES), F32)
    return pl.pallas_call(
        body, name="scan_fwd", grid=(NCH,),
        in_specs=[toks[0]] * 6 + [toks[1]] * 6 + [_full_spec((LANES, LANES)), _full_spec((2 * LANES, LANES))],
        out_specs=[toks[0], toks[1], pl.BlockSpec((CH, G2, HEAD, LANES), lambda c: (c, 0, 0, 0)),
                   _full_spec((G2, HEAD, LANES))],
        out_shape=[y_shape, y_shape, jax.ShapeDtypeStruct(((NCH + 1) * CH, G2, HEAD, LANES), F32),
                   jax.ShapeDtypeStruct((G2, HEAD, LANES), F32)],
        scratch_shapes=[pltpu.VMEM((G2, HEAD, LANES), F32), pltpu.VMEM((2, G2, HEAD, LANES), F32)],
        compiler_params=_params(("arbitrary",)),
    )(*ops_f, *ops_b, e128, e256)


def _mix_out(yf, yb, kdf, kdb, r, v, gd, conv, x, mod2, r_k, gw2p, lnx_w, lnx_b, w_out, g2, e_w, cfg):
    B, TX, D, TT, W, CW, JC = cfg.B, cfg.TX, cfg.D, cfg.TT, cfg.W, cfg.CW, cfg.JC

    def body(yf_ref, yb_ref, kdf_ref, kdb_ref, r_ref, v_ref, gd_ref, cv_ref, x_ref, mod_ref,
             rk_ref, gw_ref, lw_ref, lb_ref, wo_ref, g_ref, e_ref, x1_ref):
        y = _from_heads(yf_ref, cfg) + _from_heads(yb_ref, cfg)
        kbar = 0.5 * (_from_heads(kdf_ref, cfg) + _from_heads(kdb_ref, cfg))
        ro = _readout(y, kbar, _from_heads(r_ref, cfg), _from_heads(v_ref, cfg), gd_ref[0], rk_ref[...],
                      gw_ref[...], lw_ref[...], lb_ref[...], e_ref[...])
        cat = jnp.concatenate([ro, cv_ref[0]], axis=-1)
        mix = _bdot(cat, wo_ref[...])
        x1_ref[0] = _post_res(x_ref[0], mix, mod_ref[0], g_ref[...])

    hs = _head_spec(cfg, lambda j: j + JC)
    lat = lambda n: pl.BlockSpec((1, TT, n), lambda b, j: (b, j + JC, 0))
    return pl.pallas_call(
        body, name="mix_out", grid=(B, cfg.JX),
        in_specs=[hs] * 6 + [lat(cfg.GP), lat(CW),
                             pl.BlockSpec((1, TT, D), lambda b, j: (b, j, 0)),
                             pl.BlockSpec((1, 1, D), lambda b, j: (b, 0, 0)),
                             _full_spec((1, W)), _full_spec((cfg.GP, W)), _full_spec((1, W)), _full_spec((1, W)),
                             _full_spec((W + CW, D)), _full_spec((1, D)), _full_spec((W, W))],
        out_specs=pl.BlockSpec((1, TT, D), lambda b, j: (b, j, 0)),
        out_shape=jax.ShapeDtypeStruct((B, TX, D), F32),
        compiler_params=_params(("parallel", "parallel")),
    )(yf, yb, kdf, kdb, r, v, gd, conv, x, mod2, r_k, gw2p, lnx_w, lnx_b, w_out, g2, e_w)


def _acc(ref, val, first):
    @pl.when(first)
    def _():
        ref[...] = val

    @pl.when(jnp.logical_not(first))
    def _():
        ref[...] += val


def _mlp_fwd_bwd(x1, tgt, mod345, g3, g4, w1, w2, cfg):
    B, TX, D, TT, F, JX = cfg.B, cfg.TX, cfg.D, cfg.TT, cfg.F, cfg.JX

    def body(x1_ref, t_ref, mod_ref, g3_ref, g4_ref, w1_ref, w2_ref,
             dx1_ref, loss_ref, h2_ref, dpre_ref, act_ref, dff_ref, dmod_ref, dg3_ref, dg4_ref):
        b, j = pl.program_id(0), pl.program_id(1)
        x1v = x1_ref[0]
        sh, sc, gt = mod_ref[0, 0:1, :], mod_ref[0, 1:2, :], mod_ref[0, 2:3, :]
        h2, vjp_pre = jax.vjp(_rms_mod, x1v, g3_ref[...], sh, sc)
        h2b = h2.astype(BF16)
        pre = jnp.dot(h2b, w1_ref[...], preferred_element_type=F32)
        rl = jnp.maximum(pre, 0.0)
        actb = (rl * rl).astype(BF16)
        ff = jnp.dot(actb, w2_ref[...], preferred_element_type=F32)
        x2, vjp_post = jax.vjp(_post_res, x1v, ff, gt, g4_ref[...])
        err = x2 - t_ref[0]
        loss = 0.5 * jnp.sum(jnp.mean(err * err, axis=-1, keepdims=True))
        dx1a, dff, dgt, dg4 = vjp_post(err * (1.0 / D))
        dffb = dff.astype(BF16)
        dpre = _bdot_nt(dffb, w2_ref[...]) * (2.0 * rl)
        dpreb = dpre.astype(BF16)
        dx1b, dg3, dsh, dsc = vjp_pre(_bdot_nt(dpreb, w1_ref[...]))
        dx1_ref[0] = dx1a + dx1b
        loss_ref[0, 0] = jnp.zeros((SUBLANES, LANES), F32) + loss
        h2_ref[0] = h2b
        dpre_ref[0] = dpreb
        act_ref[0] = actb
        dff_ref[0] = dffb
        _acc(dmod_ref, jnp.concatenate([dsh, dsc, dgt], axis=0)[None], j == 0)
        first = jnp.logical_and(b == 0, j == 0)
        _acc(dg3_ref, dg3, first)
        _acc(dg4_ref, dg4, first)

    tile = lambda n: pl.BlockSpec((1, TT, n), lambda b, j: (b, j, 0))
    return pl.pallas_call(
        body, name="mlp_fwd_bwd", grid=(B, JX),
        in_specs=[tile(D), tile(D), pl.BlockSpec((1, 3, D), lambda b, j: (b, 0, 0)),
                  _full_spec((1, D)), _full_spec((1, D)),
                  pl.BlockSpec((D, F), lambda b, j: (0, 0), pipeline_mode=pl.Buffered(1)),
                  pl.BlockSpec((F, D), lambda b, j: (0, 0), pipeline_mode=pl.Buffered(1))],
        out_specs=[tile(D), pl.BlockSpec((1, 1, SUBLANES, LANES), lambda b, j: (b, j, 0, 0)),
                   tile(D), tile(F), tile(F), tile(D),
                   pl.BlockSpec((1, 3, D), lambda b, j: (b, 0, 0)),
                   _full_spec((1, D)), _full_spec((1, D))],
        out_shape=[jax.ShapeDtypeStruct((B, TX, D), F32),
                   jax.ShapeDtypeStruct((B, JX, SUBLANES, LANES), F32),
                   jax.ShapeDtypeStruct((B, TX, D), BF16), jax.ShapeDtypeStruct((B, TX, F), BF16),
                   jax.ShapeDtypeStruct((B, TX, F), BF16), jax.ShapeDtypeStruct((B, TX, D), BF16),
                   jax.ShapeDtypeStruct((B, 3, D), F32),
                   jax.ShapeDtypeStruct((1, D), F32), jax.ShapeDtypeStruct((1, D), F32)],
        compiler_params=_params(("arbitrary", "arbitrary")),
    )(x1, tgt, mod345, g3, g4, w1, w2)


def _mix_out_bwd(yf, yb, kdf, kdb, r, v, gd, conv, x, mod2, r_k, gw2p, lnx_w, lnx_b, w_out, g2, e_w, dx1, cfg):
    B, TX, D, TT, W, CW, JC, HP, GP = cfg.B, cfg.TX, cfg.D, cfg.TT, cfg.W, cfg.CW, cfg.JC, cfg.HP, cfg.GP

    def body(yf_ref, yb_ref, kdf_ref, kdb_ref, r_ref, v_ref, gd_ref, cv_ref, x_ref, mod_ref,
             rk_ref, gw_ref, lw_ref, lb_ref, wo_ref, g_ref, e_ref, dx1_ref,
             dy_ref, dkb_ref, dr_ref, dv_ref, dgd_ref, dcv_ref, cat_ref, dmix_ref,
             dmod_ref, dg2_ref, drk_ref, dgw_ref, dlw_ref, dlb_ref):
        b, j = pl.program_id(0), pl.program_id(1)
        e = e_ref[...]
        y = _from_heads(yf_ref, cfg) + _from_heads(yb_ref, cfg)
        kbar = 0.5 * (_from_heads(kdf_ref, cfg) + _from_heads(kdb_ref, cfg))
        ro, vjp_ro = jax.vjp(lambda *a: _readout(*a, e), y, kbar, _from_heads(r_ref, cfg),
                             _from_heads(v_ref, cfg), gd_ref[0], rk_ref[...], gw_ref[...], lw_ref[...], lb_ref[...])
        catb = jnp.concatenate([ro, cv_ref[0]], axis=-1).astype(BF16)
        mix = jnp.dot(catb, wo_ref[...], preferred_element_type=F32)
        _, vjp_post = jax.vjp(_post_res, x_ref[0], mix, mod_ref[0], g_ref[...])
        _, dmix, dgate, dg2 = vjp_post(dx1_ref[0])
        dmixb = dmix.astype(BF16)
        dcat = _bdot_nt(dmixb, wo_ref[...])
        dy, dkb, dr, dv, dgd, drk, dgw, dlw, dlb = vjp_ro(dcat[:, :W])
        _to_heads(dy_ref, dy, cfg)
        _to_heads(dkb_ref, dkb, cfg)
        _to_heads(dr_ref, dr, cfg)
        _to_heads(dv_ref, dv, cfg)
        dgd_ref[0] = dgd
        dcv_ref[0] = dcat[:, W:]
        cat_ref[0] = catb
        dmix_ref[0] = dmixb
        _acc(dmod_ref, dgate[None], j == 0)
        first = jnp.logical_and(b == 0, j == 0)
        _acc(dg2_ref, dg2, first)
        _acc(drk_ref, drk, first)
        _acc(dgw_ref, dgw, first)
        _acc(dlw_ref, dlw, first)
        _acc(dlb_ref, dlb, first)

    hs = _head_spec(cfg, lambda j: j + JC)
    ho = _head_spec(cfg, lambda j: j)
    lat = lambda n: pl.BlockSpec((1, TT, n), lambda b, j: (b, j + JC, 0))
    tile = lambda n: pl.BlockSpec((1, TT, n), lambda b, j: (b, j, 0))
    head = jax.ShapeDtypeStruct((B, HP, TX, LANES), F32)
    vec = lambda n: jax.ShapeDtypeStruct((1, n), F32)
    return pl.pallas_call(
        body, name="mix_out_bwd", grid=(B, cfg.JX),
        in_specs=[hs] * 6 + [lat(GP), lat(CW), tile(D),
                             pl.BlockSpec((1, 1, D), lambda b, j: (b, 0, 0)),
                             _full_spec((1, W)), _full_spec((GP, W)), _full_spec((1, W)), _full_spec((1, W)),
                             _full_spec((W + CW, D)), _full_spec((1, D)), _full_spec((W, W)), tile(D)],
        out_specs=[ho] * 4 + [tile(GP), tile(CW), tile(W + CW), tile(D),
                              pl.BlockSpec((1, 1, D), lambda b, j: (b, 0, 0)),
                              _full_spec((1, D)), _full_spec((1, W)), _full_spec((GP, W)),
                              _full_spec((1, W)), _full_spec((1, W))],
        out_shape=[head] * 4 + [jax.ShapeDtypeStruct((B, TX, GP), F32), jax.ShapeDtypeStruct((B, TX, CW), F32),
                                jax.ShapeDtypeStruct((B, TX, W + CW), BF16), jax.ShapeDtypeStruct((B, TX, D), BF16),
                                jax.ShapeDtypeStruct((B, 1, D), F32),
                                vec(D), vec(W), jax.ShapeDtypeStruct((GP, W), F32), vec(W), vec(W)],
        compiler_params=_params(("arbitrary", "arbitrary")),
    )(yf, yb, kdf, kdb, r, v, gd, conv, x, mod2, r_k, gw2p, lnx_w, lnx_b, w_out, g2, e_w, dx1)


def _scan_bwd(ops_f, ops_b, dy, hist, e128, e256, cfg):
    G, T, NCH, NCC = cfg.G, cfg.T, cfg.NCH, cfg.NCC
    CH = SCAN_CHUNK
    G2 = 2 * G

    def body(*refs):
        ins = (refs[0:6], refs[6:12])
        dys, hist_ref, next_ref, e_ref, e2_ref = (refs[12], refs[13]), refs[14], refs[15], refs[16], refs[17]
        outs = (refs[18:24], refs[24:30])
        ds_ref, mm_ref = refs[30], refs[31]
        gi = pl.program_id(0)

        @pl.when(gi == 0)
        def _():
            ds_ref[...] = jnp.zeros_like(ds_ref)

        diag = _diag_mask()
        diag_b = diag.astype(BF16)
        e, e2 = e_ref[...], e2_ref[...]
        rows = functools.partial(_both_rows, ins)
        latent = [(_chunk_pos(NCH - 1 - gi, d == 1, cfg) >= NCC).astype(F32) for d in range(2)]

        def dy_rows(i):
            return jnp.concatenate([dys[d][:, pl.ds(_tok(i, d == 1), 1), :] * latent[d] for d in range(2)], axis=0)

        def put(idx, i, val):
            outs[0][idx][:, pl.ds(_tok(i, False), 1), :] = val[:G]
            outs[1][idx][:, pl.ds(_tok(i, True), 1), :] = val[G:]

        rsum = lambda z: jnp.sum(z, axis=1, keepdims=True)

        def prepare(i):
            return (_col_form(dy_rows(i)), _col_dot([rows(3, i)], diag_b, e2)[0],
                    hist_ref[i].astype(BF16) * (-rows(4, i)).astype(BF16))

        dyc0, vb0, sa_lhs = prepare(CH - 1)
        mm_ref[0] = dyc0
        mm_ref[1] = vb0
        mm_ref[2] = _seg_dot([sa_lhs], e)[0]

        def one_step(i, s_after):
            prv = jnp.maximum(i - 1, 0)
            sp, dyc = hist_ref[i], mm_ref[0]
            ds = ds_ref[...] + dyc * rows(0, i)
            put(0, i, rsum(s_after * dyc))
            put(1, i, rsum(ds * sp))
            put(5, i, rsum(ds * mm_ref[2]))
            put(2, i, rsum(ds * mm_ref[1]))
            dyc_n, vb_n, sa_lhs_n = prepare(prv)
            dsb = ds.astype(BF16)
            res = _seg_dot([dsb * rows(5, i).astype(BF16), dsb * rows(2, i).astype(BF16), sa_lhs_n], e)
            dsa = res[0]
            put(4, i, -rsum(sp * dsa))
            put(3, i, rsum(diag * res[1]))
            mm_ref[0] = dyc_n
            mm_ref[1] = vb_n
            mm_ref[2] = res[2]
            ds_ref[...] = ds * rows(1, i) - dsa * rows(4, i)

        one_step(CH - 1, next_ref[0])

        def bstep(ii, carry):
            i = CH - 1 - ii
            one_step(i, hist_ref[i + 1])
            return carry

        lax.fori_loop(1, CH, bstep, 0)

    cpos = lambda g, rev: _chunk_pos(NCH - 1 - g, rev, cfg)
    toks = [pl.BlockSpec((G, CH, LANES), lambda g, rev=rev: (0, cpos(g, rev), 0)) for rev in (False, True)]
    dy_specs = [pl.BlockSpec((G, CH, LANES), lambda g, rev=rev: (0, jnp.maximum(cpos(g, rev) - NCC, 0), 0))
                for rev in (False, True)]
    out = jax.ShapeDtypeStruct((G, T, LANES), F32)
    res = pl.pallas_call(
        body, name="scan_bwd", grid=(NCH,),
        in_specs=[toks[0]] * 6 + [toks[1]] * 6 + dy_specs
                 + [pl.BlockSpec((CH, G2, HEAD, LANES), lambda g: (NCH - 1 - g, 0, 0, 0)),
                    pl.BlockSpec((1, G2, HEAD, LANES), lambda g: ((NCH - g) * CH, 0, 0, 0)),
                    _full_spec((LANES, LANES)), _full_spec((2 * LANES, LANES))],
        out_specs=[toks[0]] * 6 + [toks[1]] * 6,
        out_shape=[out] * 12,
        scratch_shapes=[pltpu.VMEM((G2, HEAD, LANES), F32), pltpu.VMEM((3, G2, HEAD, LANES), F32)],
        compiler_params=_params(("arbitrary",)),
    )(*ops_f, *ops_b, dy, dy, hist, hist, e128, e256)
    return res[:6], res[6:]


def _mix_prep_bwd(p, mup, mun, w0, w2p, a0, a2p, k_k, k_a, wm, cb, clw, clb, e_w, yconv, sf, sb, ro, cfg):
    B, T, TT, SP, CP, W, CW, HP, JC, PW, GP, KC, KP = (cfg.B, cfg.T, cfg.TT, cfg.SP, cfg.CP, cfg.W, cfg.CW,
                                                       cfg.HP, cfg.JC, cfg.PW, cfg.GP, cfg.KC, cfg.KP)
    pad = KC // 2

    def body(p_ref, prev_ref, next_ref, mup_ref, mun_ref, w0_ref, w2_ref, a0_ref, a2_ref, kk_ref, ka_ref,
             wm_ref, cb_ref, clw_ref, clb_ref, e_ref, yc_ref, *rest):
        sf_refs, sb_refs = rest[0:6], rest[6:12]
        rdr_ref, rdv_ref, rdkb_ref, rdgd_ref, rdcv_ref = rest[12:17]
        (dpz_ref, dmup_ref, dmun_ref, dw0_ref, dw2_ref, da0_ref, da2_ref, dkk_ref, dka_ref,
         dcw_ref, dcb_ref, dclw_ref, dclb_ref, dyc_ref, du_ref) = rest[17:]
        b, j = pl.program_id(0), pl.program_id(1)
        first = jnp.logical_and(b == 0, j == 0)
        lat = (j >= JC).astype(F32)
        e = e_ref[...]
        mup_v, mun_v = mup_ref[...], mun_ref[...]
        z, zp, zn, rw = _shifted(p_ref, prev_ref, next_ref, mup_v, mun_v, j, cfg)

        def prep(rw_, w0_, w2_, a0_, a2_, kk_, ka_):
            return _rwkv_prep(rw_, w0_, w2_, a0_, a2_, kk_, ka_, e, cfg) + (rw_[:, 3 * W + 2 * PW:SP],)

        _, vjp_prep = jax.vjp(prep, rw, w0_ref[...], w2_ref[...], a0_ref[...], a2_ref[...], kk_ref[...], ka_ref[...])
        fr, fw, fk, fv, fkk, fb = [_from_heads(r_, cfg) for r_ in sf_refs]
        br, bw, bk, bv, bkk, bb = [_from_heads(r_, cfg) for r_ in sb_refs]
        half_kb = (0.5 * lat) * _from_heads(rdkb_ref, cfg)
        cots = (fr + br + lat * _from_heads(rdr_ref, cfg), fv + bv + lat * _from_heads(rdv_ref, cfg), fkk + bkk,
                fw, fk + half_kb, fb, bw, bk + half_kb, bb, lat * rdgd_ref[0])
        drw, dw0, dw2, da0, da2, dkk, dka = vjp_prep(cots)
        _acc(dmup_ref, jnp.sum(drw * (zp - z), axis=0, keepdims=True), first)
        _acc(dmun_ref, jnp.sum(drw * (zn - z), axis=0, keepdims=True), first)
        for ref, val in ((dw0_ref, dw0), (dw2_ref, dw2), (da0_ref, da0), (da2_ref, da2), (dkk_ref, dkk), (dka_ref, dka)):
            _acc(ref, val, first)

        dpz_ref[0, :, 0:SP] = drw

        @pl.when(first)
        def _():
            for ref in (dcw_ref, dcb_ref, dclw_ref, dclb_ref):
                ref[...] = jnp.zeros_like(ref)

        @pl.when(j < JC)
        def _():
            dpz_ref[0, :, SP:] = jnp.zeros((TT, 2 * CW), F32)

        @pl.when(j >= JC)
        def _():
            u, vjp_glu = jax.vjp(lambda c_: _glu(c_, cfg), p_ref[0, :, SP:])
            _, vjp_post = jax.vjp(_conv_post, yc_ref[0], cb_ref[...], clw_ref[...], clb_ref[...])
            dyc, dcb, dclw, dclb = vjp_post(rdcv_ref[0])
            dyc_ref[...] = dyc
            _conv_lines(dyc_ref, wm_ref, du_ref, KC, True)
            (dcv,) = vjp_glu(du_ref[...])
            dpz_ref[0, :, SP:] = dcv
            for i in range(KC):
                dcw_ref[i:i + 1, :] += jnp.sum(dyc * _line_shift(u, i - pad), axis=0, keepdims=True)
            dcb_ref[...] += dcb
            dclw_ref[...] += dclw
            dclb_ref[...] += dclb

    prev, nxt = _halo_specs(cfg, SP)
    hs = _head_spec(cfg, lambda j: j)
    hl = _head_spec(cfg, lambda j: jnp.maximum(j - JC, 0))
    latn = lambda n: pl.BlockSpec((1, TT, n), lambda b, j: (b, jnp.maximum(j - JC, 0), 0))
    vec = lambda n: jax.ShapeDtypeStruct((1, n), F32)
    small_shapes = [vec(SP), vec(SP), vec(2 * W), jax.ShapeDtypeStruct((PW, 2 * W), F32), vec(2 * W),
                    jax.ShapeDtypeStruct((PW, 2 * W), F32), vec(W), vec(W),
                    jax.ShapeDtypeStruct((KP, CW), F32), vec(CW), vec(CW), vec(CW)]
    return pl.pallas_call(
        body, name="mix_prep_bwd", grid=(B, cfg.J),
        in_specs=[pl.BlockSpec((1, TT, CP), lambda b, j: (b, j, 0)), prev, nxt,
                  _full_spec((1, SP)), _full_spec((1, SP)),
                  _full_spec((1, 2 * W)), _full_spec((PW, 2 * W)),
                  _full_spec((1, 2 * W)), _full_spec((PW, 2 * W)),
                  _full_spec((1, W)), _full_spec((1, W)),
                  pl.BlockSpec((KC, LINE, CW), lambda b, j: (0, 0, 0), pipeline_mode=pl.Buffered(1)),
                  _full_spec((1, CW)), _full_spec((1, CW)), _full_spec((1, CW)),
                  _full_spec((W, W)), pl.BlockSpec((1, TT, CW), lambda b, j: (b, j, 0))]
                 + [hs] * 12 + [hl] * 3 + [latn(GP), latn(CW)],
        out_specs=[pl.BlockSpec((1, TT, CP), lambda b, j: (b, j, 0))] + [_full_spec(s.shape) for s in small_shapes],
        out_shape=[jax.ShapeDtypeStruct((B, T, CP), F32)] + small_shapes,
        scratch_shapes=[pltpu.VMEM((TT, CW), F32), pltpu.VMEM((TT, CW), F32)],
        compiler_params=_params(("arbitrary", "arbitrary")),
    )(p, p, p, mup, mun, w0, w2p, a0, a2p, k_k, k_a, wm, cb, clw, clb, e_w, yconv, *sf, *sb, *ro)


def _in_proj_bwd(dpz, xcat, modt, g1, w_in_p, mup, mun, dx1, cfg):
    B, T, TX, D, TT, SP, CP, JC = cfg.B, cfg.T, cfg.TX, cfg.D, cfg.TT, cfg.SP, cfg.CP, cfg.JC

    def body(d_ref, prev_ref, next_ref, x_ref, mod_ref, g_ref, w_ref, mup_ref, mun_ref, dx1_ref,
             gx_ref, dp_ref, dmod_ref, dg_ref):
        b, j = pl.program_id(0), pl.program_id(1)
        has_prev, has_next = _halo_flags(j, cfg)
        mp, mn = mup_ref[...], mun_ref[...]
        drw = d_ref[0, :, 0:SP]
        dprev, dnext = _shift_rows(drw, prev_ref[0, SUBLANES - 1:SUBLANES, :] * has_prev,
                                   next_ref[0, 0:1, :] * has_next)
        dz = drw * (1.0 - mp - mn) + mp * dnext + mn * dprev
        dpb = jnp.concatenate([dz, d_ref[0, :, SP:]], axis=-1).astype(BF16)
        dp_ref[0] = dpb
        dh = _bdot_nt(dpb, w_ref[...])
        _, vjp_h = jax.vjp(_rms_mod, x_ref[0], g_ref[...], mod_ref[0, 0, 0:1, :], mod_ref[0, 0, 1:2, :])
        dx, dg, dsh, dsc = vjp_h(dh)
        dmod_ref[0, 0] = jnp.concatenate([dsh, dsc], axis=0)
        _acc(dg_ref, dg, jnp.logical_and(b == 0, j == 0))

        @pl.when(j >= JC)
        def _():
            gx_ref[0] = dx + dx1_ref[0]

    prev, nxt = _halo_specs(cfg, SP)
    lat = pl.BlockSpec((1, TT, D), lambda b, j: (b, jnp.maximum(j - JC, 0), 0))
    return pl.pallas_call(
        body, name="in_proj_bwd", grid=(B, cfg.J),
        in_specs=[pl.BlockSpec((1, TT, CP), lambda b, j: (b, j, 0)), prev, nxt,
                  pl.BlockSpec((1, TT, D), lambda b, j: (b, j, 0)),
                  pl.BlockSpec((1, 1, 2, D), lambda b, j: (b, j, 0, 0)),
                  _full_spec((1, D)), _full_spec((D, CP)), _full_spec((1, SP)), _full_spec((1, SP)), lat],
        out_specs=[lat, pl.BlockSpec((1, TT, CP), lambda b, j: (b, j, 0)),
                   pl.BlockSpec((1, 1, 2, D), lambda b, j: (b, j, 0, 0)), _full_spec((1, D))],
        out_shape=[jax.ShapeDtypeStruct((B, TX, D), F32), jax.ShapeDtypeStruct((B, T, CP), BF16),
                   jax.ShapeDtypeStruct((B, cfg.J, 2, D), F32), jax.ShapeDtypeStruct((1, D), F32)],
        compiler_params=_params(("arbitrary", "arbitrary")),
    )(dpz, dpz, dpz, xcat, modt, g1, w_in_p, mup, mun, dx1)


def _pick_tile(n, pref):
    for t in pref:
        if n % t == 0:
            return t
    return n


def _grad_matmul(a, g, name):
    K, M = a.shape
    N = g.shape[1]
    tm = _pick_tile(M, (512, 256, 128))
    tn = _pick_tile(N, (1024, 768, 512, 256, 128))
    tk = _pick_tile(K, (2048, 1024, 512, 256, 128, 64))
    nk = K // tk

    def body(a_ref, g_ref, o_ref):
        k = pl.program_id(2)
        _acc(o_ref, _bdot_tn(a_ref[...], g_ref[...]), k == 0)

    return pl.pallas_call(
        body, name=name, grid=(M // tm, N // tn, nk),
        in_specs=[pl.BlockSpec((tk, tm), lambda i, j, k: (k, i)),
                  pl.BlockSpec((tk, tn), lambda i, j, k: (k, j))],
        out_specs=pl.BlockSpec((tm, tn), lambda i, j, k: (i, j)),
        out_shape=jax.ShapeDtypeStruct((M, N), F32),
        compiler_params=_params(("parallel", "parallel", "arbitrary")),
    )(a, g)


def _ada_fwd(crows, ada_w, ada_b):
    D = crows.shape[1]
    n6 = ada_w.shape[1]
    tn = _pick_tile(n6, (1024, 512, 256, 128))

    def body(c_ref, w_ref, b_ref, s_ref, m_ref):
        s = _silu(c_ref[...])
        s_ref[...] = s
        m_ref[...] = _bdot(s, w_ref[...]) + b_ref[...]

    return pl.pallas_call(
        body, name="ada_fwd", grid=(n6 // tn,),
        in_specs=[_full_spec((SUBLANES, D)), pl.BlockSpec((D, tn), lambda i: (0, i)),
                  pl.BlockSpec((1, tn), lambda i: (0, i))],
        out_specs=[_full_spec((SUBLANES, D)), pl.BlockSpec((SUBLANES, tn), lambda i: (0, i))],
        out_shape=[jax.ShapeDtypeStruct((SUBLANES, D), F32), jax.ShapeDtypeStruct((SUBLANES, n6), F32)],
        compiler_params=_params(("arbitrary",)),
    )(crows, ada_w, ada_b)


def _ada_bwd(s_all, g_all, g_mine, c_ctx, ada_w, nb):
    D = s_all.shape[1]
    n6 = g_all.shape[1]
    ns = g_mine.shape[1]

    def body(s_ref, g_ref, gm_ref, c_ref, w_ref, dw_ref, db_ref, dc_ref):
        g = g_ref[...]
        dw_ref[...] = _bdot_tn(s_ref[...], gm_ref[...])
        db_ref[...] = jnp.sum(g, axis=0, keepdims=True)
        rows = lax.broadcasted_iota(jnp.int32, (g.shape[0], 1), 0)
        gc = jnp.sum(jnp.where(rows % SUBLANES == nb, g, 0.0), axis=0, keepdims=True)
        ds = _bdot_nt(gc, w_ref[...])
        c = c_ref[...]
        sg = _sigmoid(c)
        dc_ref[...] = ds * (sg + c * sg * (1.0 - sg))

    return pl.pallas_call(
        body, name="ada_bwd",
        out_shape=[jax.ShapeDtypeStruct((D, ns), F32), jax.ShapeDtypeStruct((1, n6), F32),
                   jax.ShapeDtypeStruct((1, D), F32)],
        compiler_params=_params(),
    )(s_all, g_all, g_mine, c_ctx, ada_w)


def _adamw(parts, w, m, v, name):
    P, R, C = parts.shape
    small = R * C * (P + 7) * 4 <= 4 * 1024 * 1024
    tr = R if small else _pick_tile(R, (256, 128, 64, 32, 16, 8))

    def body(p_ref, w_ref, m_ref, v_ref, g_ref, d_ref, nm_ref, nv_ref):
        g = p_ref[0].astype(F32)
        for i in range(1, P):
            g = g + p_ref[i].astype(F32)
        nm = ADAM_B1 * m_ref[...] + (1.0 - ADAM_B1) * g
        nv = ADAM_B2 * v_ref[...] + (1.0 - ADAM_B2) * (g * g)
        m_hat = nm / (1.0 - ADAM_B1 ** ADAM_STEP)
        v_hat = nv / (1.0 - ADAM_B2 ** ADAM_STEP)
        g_ref[...] = g
        d_ref[...] = -ADAM_LR * (m_hat / (jnp.sqrt(v_hat) + ADAM_EPS) + ADAM_WD * w_ref[...])
        nm_ref[...] = nm
        nv_ref[...] = nv

    blk = pl.BlockSpec((tr, C), lambda i: (i, 0))
    out = jax.ShapeDtypeStruct((R, C), F32)
    return pl.pallas_call(
        body, name=name, grid=(R // tr,),
        in_specs=[pl.BlockSpec((P, tr, C), lambda i: (0, i, 0)), blk, blk, blk],
        out_specs=[blk] * 4, out_shape=[out] * 4,
        compiler_params=_params(("parallel",)),
    )(parts, w, m, v)


def _local_step(cfg, x, c, ctx, tgt, fw):
    B, D, W, CW, JC, T, TX = cfg.B, cfg.D, cfg.W, cfg.CW, cfg.JC, cfg.T, cfg.TX
    e_w = _block_ones(W)
    e128 = _block_ones(LANES)
    e256 = jnp.concatenate([e128, e128], axis=0)
    row = lambda a: a.reshape(1, -1)

    ada_wb = fw["ada_w"].astype(BF16)
    w_in_p = _pad_cols(fw["w_in"], cfg).astype(BF16)
    mup = _pad_cols(fw["mu_prev"], cfg, True)
    mun = _pad_cols(fw["mu_next"], cfg, True)
    w0, a0 = row(fw["decay_w0"]), row(fw["iclr_a0"])
    w2p, a2p = _pair_weight(fw["decay_w2"], cfg), _pair_weight(fw["iclr_a2"], cfg)
    wm_fwd, wm_bwd = _conv_tables(fw["conv_w"], cfg.KC)
    gw2p = jnp.pad(fw["gate_w2"], ((0, cfg.GP - cfg.GR), (0, 0)))
    r_k = row(fw["r_k"])
    w_outb, w1b, w2b = fw["w_out"].astype(BF16), fw["mlp_w1"].astype(BF16), fw["mlp_w2"].astype(BF16)

    crows = jnp.concatenate([c, fw["c_ctx"], jnp.zeros((SUBLANES - B - 1, D), F32)], axis=0)
    s_rows, mods = _ada_fwd(crows, ada_wb, fw["ada_b"])
    mod_x = mods[:B].reshape(B, 6, D)
    mod_c = mods[B].reshape(6, D)
    modt = jnp.concatenate([jnp.broadcast_to(mod_c[None, None, 0:2], (B, JC, 2, D)),
                            jnp.broadcast_to(mod_x[:, None, 0:2], (B, cfg.JX, 2, D))], axis=1)
    mod2, mod345 = mod_x[:, 2:3], mod_x[:, 3:6]

    xcat = jnp.concatenate([ctx, x], axis=1)
    p, hb = _in_proj(xcat, modt, fw["mix_pre_g"], w_in_p, cfg)
    prep_w = (mup, mun, w0, w2p, a0, a2p, fw["k_k"], fw["k_a"])
    conv_w = (fw["conv_b"], fw["conv_ln_w"], fw["conv_ln_b"], e_w)
    r, v, kk, w_f, kd_f, b_f, w_b, kd_b, b_b, gd, conv, yconv = _mix_prep(p, *prep_w, wm_fwd, *conv_w, cfg)
    flat = lambda a: a.reshape(cfg.G, a.shape[2], LANES)
    heads = lambda a: a.reshape(B, cfg.HP, a.shape[1], LANES)
    ops_f = tuple(flat(a) for a in (r, w_f, kd_f, v, kk, b_f))
    ops_b = tuple(flat(a) for a in (r, w_b, kd_b, v, kk, b_b))
    y_f, y_b, hist, s_fin = _scan_fwd(ops_f, ops_b, e128, e256, cfg)
    hist = lax.dynamic_update_slice_in_dim(hist, s_fin[None], cfg.NCH * SCAN_CHUNK, axis=0)
    out_args = (heads(y_f), heads(y_b), kd_f, kd_b, r, v, gd, conv, x, mod2, r_k, gw2p, fw["lnx_w"], fw["lnx_b"],
                w_outb, fw["mix_post_g"], e_w)
    x1 = _mix_out(*out_args, cfg)

    dx1, loss_t, h2b, dpreb, actb, dffb, dmod345, dg3, dg4 = _mlp_fwd_bwd(
        x1, tgt, mod345, fw["mlp_pre_g"], fw["mlp_post_g"], w1b, w2b, cfg)
    (dy, dkb, dr_c, dv_c, dgd, dconv, catb, dmixb, dmod2, dg2, drk, dgw, dlw, dlb) = _mix_out_bwd(
        *out_args, dx1, cfg)
    sf, sb = _scan_bwd(ops_f, ops_b, flat(dy), hist, e128, e256, cfg)
    (dpz, dmup, dmun, dw0, dw2p, da0, da2p, dkk, dka, dcw, dcb, dclw, dclb) = _mix_prep_bwd(
        p, *prep_w, wm_bwd, *conv_w, yconv, [heads(a) for a in sf], [heads(a) for a in sb],
        (dr_c, dv_c, dkb, dgd, dconv), cfg)
    grad_x, dpb, dmodt, dg1 = _in_proj_bwd(dpz, xcat, modt, fw["mix_pre_g"], w_in_p, mup, mun, dx1, cfg)

    tokens = lambda a: a.reshape(-1, a.shape[-1])
    d_w_in = _grad_matmul(tokens(hb), tokens(dpb), "grad_w_in")
    d_w_out = _grad_matmul(tokens(catb), tokens(dmixb), "grad_w_out")
    d_w1 = _grad_matmul(tokens(h2b), tokens(dpreb), "grad_mlp_w1")
    d_w2 = _grad_matmul(tokens(actb), tokens(dffb), "grad_mlp_w2")

    grads = {
        "mix_pre_g": dg1, "mix_post_g": dg2, "mlp_pre_g": dg3, "mlp_post_g": dg4,
        "w_in": _unpad_cols(d_w_in, cfg),
        "mu_prev": _unpad_cols(dmup, cfg, True), "mu_next": _unpad_cols(dmun, cfg, True),
        "decay_w0": dw0.reshape(2, W), "decay_w2": _unpair_weight(dw2p, cfg),
        "iclr_a0": da0.reshape(2, W), "iclr_a2": _unpair_weight(da2p, cfg),
        "k_k": dkk, "k_a": dka, "r_k": drk.reshape(fw["r_k"].shape),
        "gate_w2": dgw[:cfg.GR], "lnx_w": dlw, "lnx_b": dlb,
        "conv_w": dcw[:cfg.KC], "conv_b": dcb, "conv_ln_w": dclw, "conv_ln_b": dclb,
        "w_out": d_w_out, "mlp_w1": d_w1, "mlp_w2": d_w2,
    }
    dmod_x = jnp.concatenate([jnp.sum(dmodt[:, JC:], axis=1), dmod2, dmod345], axis=1).reshape(B, 6 * D)
    dmod_c = jnp.concatenate([jnp.sum(dmodt[:, :JC], axis=(0, 1)), jnp.zeros((4, D), F32)], axis=0).reshape(1, 6 * D)
    g_rows = jnp.concatenate([dmod_x, dmod_c, jnp.zeros((SUBLANES - B - 1, 6 * D), F32)], axis=0)
    return loss_t, grad_x, grads, s_rows, g_rows


def _my_index():
    return 4 * lax.axis_index("x") + 2 * lax.axis_index("y") + lax.axis_index("c")


def _gather_two_level(arrays, name):
    n = len(arrays)
    out_shape = [jax.ShapeDtypeStruct((N_DEV,) + a.shape, a.dtype) for a in arrays]

    def body(*refs):
        ins, outs = refs[:n], refs[n:2 * n]
        send_sems, recv_sems, local_sems = refs[2 * n:]
        x, y, c = lax.axis_index("x"), lax.axis_index("y"), lax.axis_index("c")
        index = lambda px, py, pc: 4 * px + 2 * py + pc
        sibling = (x, y, 1 - c)
        chips = [(1 - x, y), (x, 1 - y), (1 - x, 1 - y)]

        def copy(a, k, block, to, src=None):
            dst = outs[a].at[index(*block)]
            return pltpu.make_async_remote_copy(
                src_ref=dst if src is None else src, dst_ref=dst,
                send_sem=send_sems.at[k, a], recv_sem=recv_sems.at[k, a],
                device_id=to, device_id_type=pl.DeviceIdType.MESH)

        local = [pltpu.make_async_copy(ins[a], outs[a].at[index(x, y, c)], local_sems.at[a]) for a in range(n)]
        for cp in local:
            cp.start()
        first = []
        for a in range(n):
            first.append(copy(a, 0, (x, y, c), sibling, src=ins[a]))
            first += [copy(a, 1 + j, (x, y, c), (*chip, c), src=ins[a]) for j, chip in enumerate(chips)]
        for cp in first:
            cp.start()
        passed = []
        for j, chip in enumerate(chips):
            for a in range(n):
                copy(a, 1 + j, (*chip, c), (x, y, c)).wait_recv()
                fwd = copy(a, 4 + j, (*chip, c), sibling)
                fwd.start()
                passed.append(fwd)
        for a in range(n):
            copy(a, 0, sibling, (x, y, c)).wait_recv()
            for j, chip in enumerate(chips):
                copy(a, 4 + j, (*chip, 1 - c), (x, y, c)).wait_recv()
        for cp in first + passed:
            cp.wait_send()
        for cp in local:
            cp.wait()

    hbm = pl.BlockSpec(memory_space=pltpu.HBM)
    return pl.pallas_call(
        body, name=name, out_shape=out_shape,
        in_specs=[hbm] * n, out_specs=[hbm] * n,
        scratch_shapes=[pltpu.SemaphoreType.DMA((N_DEV - 1, n)), pltpu.SemaphoreType.DMA((N_DEV - 1, n)),
                        pltpu.SemaphoreType.DMA((n,))],
    )(*arrays)


def _exchange(arrays, scatter, name):
    n = len(arrays)
    out_shape = [jax.ShapeDtypeStruct(a.shape if s else (N_DEV,) + a.shape, a.dtype)
                 for a, s in zip(arrays, scatter)]

    def body(*refs):
        ins, outs = refs[:n], refs[n:2 * n]
        send_sems, recv_sems, local_sems = refs[2 * n:]
        x, y, c = lax.axis_index("x"), lax.axis_index("y"), lax.axis_index("c")
        me = 4 * x + 2 * y + c
        flip = lambda v, f: 1 - v if f else v

        def piece(a, dest):
            return ins[a].at[dest] if scatter[a] else ins[a]

        local = [pltpu.make_async_copy(piece(a, me), outs[a].at[me], local_sems.at[a]) for a in range(n)]
        for cp in local:
            cp.start()
        sends, recvs = [], []
        for k in range(1, N_DEV):
            fx, fy, fc = (k >> 2) & 1, (k >> 1) & 1, k & 1
            peer = (flip(x, fx), flip(y, fy), flip(c, fc))
            peer_idx = 4 * peer[0] + 2 * peer[1] + peer[2]
            for a in range(n):
                sends.append(pltpu.make_async_remote_copy(
                    src_ref=piece(a, peer_idx), dst_ref=outs[a].at[me],
                    send_sem=send_sems.at[k - 1, a], recv_sem=recv_sems.at[k - 1, a],
                    device_id=peer, device_id_type=pl.DeviceIdType.MESH))
                recvs.append(pltpu.make_async_remote_copy(
                    src_ref=piece(a, peer_idx), dst_ref=outs[a].at[peer_idx],
                    send_sem=send_sems.at[k - 1, a], recv_sem=recv_sems.at[k - 1, a],
                    device_id=peer, device_id_type=pl.DeviceIdType.MESH))
        for cp in sends:
            cp.start()
        for cp in recvs:
            cp.wait_recv()
        for cp in sends:
            cp.wait_send()
        for cp in local:
            cp.wait()

    hbm = pl.BlockSpec(memory_space=pltpu.HBM)
    return pl.pallas_call(
        body, name=name, out_shape=out_shape,
        in_specs=[hbm] * n, out_specs=[hbm] * n,
        scratch_shapes=[pltpu.SemaphoreType.DMA((N_DEV - 1, n)), pltpu.SemaphoreType.DMA((N_DEV - 1, n)),
                        pltpu.SemaphoreType.DMA((n,))],
    )(*arrays)


def _pack(parts):
    flat = jnp.concatenate([p.reshape(-1) for p in parts])
    total = _round_up(flat.shape[0], SUBLANES * LANES)
    return jnp.pad(flat, (0, total - flat.shape[0])).reshape(-1, LANES)


def _unpack(buf, shapes):
    flat = buf.reshape(-1)
    out, pos = [], 0
    for s in shapes:
        n = int(np.prod(s))
        out.append(flat[pos:pos + n].reshape(s))
        pos += n
    return out


_SHARDED_SMALL = ("decay_w0", "decay_w2", "iclr_a0", "iclr_a2", "gate_w2", "conv_w")
_REPLICATED = ("mix_pre_g", "mix_post_g", "mlp_pre_g", "mlp_post_g", "mu_prev", "mu_next", "k_k", "k_a", "r_k",
               "lnx_w", "lnx_b", "conv_b", "conv_ln_w", "conv_ln_b")
_ADA_SMALL = ("c_ctx", "ada_b")
_WEIGHTS = ("c_ctx", "ada_w", "ada_b", "mix_pre_g", "mix_post_g", "mlp_pre_g", "mlp_post_g", "w_in", "mu_prev",
            "mu_next", "decay_w0", "decay_w2", "iclr_a0", "iclr_a2", "k_k", "k_a", "r_k", "gate_w2", "lnx_w", "lnx_b",
            "conv_w", "conv_b", "conv_ln_w", "conv_ln_b", "w_out", "mlp_w1", "mlp_w2")
_INPUTS = ("x", "c", "ctx") + _WEIGHTS + ("loss_target",) + tuple("m_" + n for n in _WEIGHTS) + tuple(
    "v_" + n for n in _WEIGHTS)


def _cols_to_blocks(a):
    a = a.reshape(a.shape[:-1] + (N_DEV, a.shape[-1] // N_DEV))
    return jnp.moveaxis(a, -2, 0)


def _blocks_to_cols(a):
    a = jnp.moveaxis(a, 0, -2)
    return a.reshape(a.shape[:-2] + (a.shape[-2] * a.shape[-1],))


def kernel(x, c, ctx, c_ctx, ada_w, ada_b, mix_pre_g, mix_post_g, mlp_pre_g, mlp_post_g, w_in, mu_prev, mu_next, decay_w0, decay_w2, iclr_a0, iclr_a2, k_k, k_a, r_k, gate_w2, lnx_w, lnx_b, conv_w, conv_b, conv_ln_w, conv_ln_b, w_out, mlp_w1, mlp_w2, loss_target, m_c_ctx, m_ada_w, m_ada_b, m_mix_pre_g, m_mix_post_g, m_mlp_pre_g, m_mlp_post_g, m_w_in, m_mu_prev, m_mu_next, m_decay_w0, m_decay_w2, m_iclr_a0, m_iclr_a2, m_k_k, m_k_a, m_r_k, m_gate_w2, m_lnx_w, m_lnx_b, m_conv_w, m_conv_b, m_conv_ln_w, m_conv_ln_b, m_w_out, m_mlp_w1, m_mlp_w2, v_c_ctx, v_ada_w, v_ada_b, v_mix_pre_g, v_mix_post_g, v_mlp_pre_g, v_mlp_post_g, v_w_in, v_mu_prev, v_mu_next, v_decay_w0, v_decay_w2, v_iclr_a0, v_iclr_a2, v_k_k, v_k_a, v_r_k, v_gate_w2, v_lnx_w, v_lnx_b, v_conv_w, v_conv_b, v_conv_ln_w, v_conv_ln_b, v_w_out, v_mlp_w1, v_mlp_w2):
    given = dict(zip(_INPUTS, (x, c, ctx, c_ctx, ada_w, ada_b, mix_pre_g, mix_post_g, mlp_pre_g, mlp_post_g, w_in, mu_prev, mu_next, decay_w0, decay_w2, iclr_a0, iclr_a2, k_k, k_a, r_k, gate_w2, lnx_w, lnx_b, conv_w, conv_b, conv_ln_w, conv_ln_b, w_out, mlp_w1, mlp_w2, loss_target, m_c_ctx, m_ada_w, m_ada_b, m_mix_pre_g, m_mix_post_g, m_mlp_pre_g, m_mlp_post_g, m_w_in, m_mu_prev, m_mu_next, m_decay_w0, m_decay_w2, m_iclr_a0, m_iclr_a2, m_k_k, m_k_a, m_r_k, m_gate_w2, m_lnx_w, m_lnx_b, m_conv_w, m_conv_b, m_conv_ln_w, m_conv_ln_b, m_w_out, m_mlp_w1, m_mlp_w2, v_c_ctx, v_ada_w, v_ada_b, v_mix_pre_g, v_mix_post_g, v_mlp_pre_g, v_mlp_post_g, v_w_in, v_mu_prev, v_mu_next, v_decay_w0, v_decay_w2, v_iclr_a0, v_iclr_a2, v_k_k, v_k_a, v_r_k, v_gate_w2, v_lnx_w, v_lnx_b, v_conv_w, v_conv_b, v_conv_ln_w, v_conv_ln_b, v_w_out, v_mlp_w1, v_mlp_w2)))
    loc = {}
    for pre in ("", "m_", "v_"):
        for n in _WEIGHTS:
            a = given[pre + n]
            a = a.reshape(1, -1) if n == "c_ctx" else a[0]
            loc[pre + n] = a.reshape(1, -1) if a.ndim == 1 else a
    B, TX, D = x.shape
    W, CW = loc["k_k"].shape[1], loc["conv_b"].shape[1]
    cfg = _Cfg(B, TX, ctx.shape[1], D, W, CW, loc["decay_w2"].shape[1], loc["gate_w2"].shape[0],
               loc["conv_w"].shape[0], loc["mlp_w1"].shape[1] * N_DEV)
    me = _my_index()

    small_shapes = [loc[n].shape for n in _SHARDED_SMALL]
    got = _gather_two_level(
        [loc["ada_w"].astype(BF16), loc["w_in"].astype(BF16), loc["w_out"].astype(BF16),
         loc["mlp_w1"].astype(BF16), loc["mlp_w2"].astype(BF16), _pack([loc[n] for n in _SHARDED_SMALL])],
        "gather_weights")
    fw = {n: loc[n] for n in _REPLICATED + _ADA_SMALL}
    fw["ada_w"] = _blocks_to_cols(got[0])
    fw["w_in"] = _blocks_to_cols(got[1])
    fw["w_out"] = got[2].reshape(-1, D)
    fw["mlp_w1"] = _blocks_to_cols(got[3])
    fw["mlp_w2"] = got[4].reshape(-1, D)
    per_dev = [_unpack(got[5][i], small_shapes) for i in range(N_DEV)]
    for j, n in enumerate(_SHARDED_SMALL):
        fw[n] = jnp.concatenate([per_dev[i][j] for i in range(N_DEV)], axis=-1)

    loss_t, grad_x, grads, s_rows, g_rows = _local_step(cfg, x, c, ctx, loss_target, fw)
    loss = lax.psum(jnp.sum(loss_t[:, :, 0, 0]), ("x", "y", "c"))

    small_blocks = jnp.stack([_pack([_cols_to_blocks(grads[n])[i] for n in _SHARDED_SMALL]) for i in range(N_DEV)])
    sent = _exchange(
        [_cols_to_blocks(grads["w_in"].astype(BF16)), grads["w_out"].astype(BF16).reshape(N_DEV, -1, D),
         _cols_to_blocks(grads["mlp_w1"].astype(BF16)), grads["mlp_w2"].astype(BF16).reshape(N_DEV, -1, D),
         small_blocks,
         _pack([grads[n] for n in _REPLICATED]), s_rows, g_rows],
        [True] * 5 + [False] * 3, "exchange_grads")
    s_all = sent[6].reshape(N_DEV * SUBLANES, D)
    g_all = sent[7].reshape(N_DEV * SUBLANES, 6 * D)
    ns = 6 * D // N_DEV
    g_mine = lax.dynamic_slice_in_dim(g_all, me * ns, ns, axis=1)
    d_ada_w, d_ada_b, d_c_ctx = _ada_bwd(s_all, g_all, g_mine, loc["c_ctx"], fw["ada_w"], B)

    res = {}

    def update(name, parts):
        res[name] = _adamw(parts, loc[name], loc["m_" + name], loc["v_" + name], "adamw_" + name)

    update("w_in", sent[0])
    update("w_out", sent[1])
    update("mlp_w1", sent[2])
    update("mlp_w2", sent[3])
    update("ada_w", d_ada_w[None])

    def update_packed(names, parts, tag):
        shapes = [loc[n].shape for n in names]
        packed = _adamw(parts, *[_pack([loc[pre + n] for n in names]) for pre in ("", "m_", "v_")], "adamw_" + tag)
        unpacked = [_unpack(p, shapes) for p in packed]
        for j, n in enumerate(names):
            res[n] = tuple(u[j] for u in unpacked)

    update_packed(_SHARDED_SMALL, sent[4], "sharded_small")
    update_packed(_REPLICATED, sent[5], "replicated")
    update_packed(_ADA_SMALL, _pack([d_c_ctx, d_ada_b])[None], "ada_small")

    outs = [loss, grad_x]
    for k in range(4):
        for n in _WEIGHTS:
            outs.append(res[n][k].reshape(given[n].shape))
    return tuple(outs)
```

```python
import functools

import numpy as np
import jax
import jax.numpy as jnp
from jax import lax
from jax.experimental import pallas as pl
from jax.experimental.pallas import tpu as pltpu

F32 = jnp.float32
BF16 = jnp.bfloat16

EPS_RMS = 1e-6
EPS_LN = 1e-5
EPS_GN = 64e-5
LINE = 64
HEAD = 64
LANES = 128
SUBLANES = 8
SCAN_CHUNK = 16
N_DEV = 8
VMEM_LIMIT = 56 * 1024 * 1024

ADAM_LR = 0.001
ADAM_B1 = 0.9
ADAM_B2 = 0.999
ADAM_EPS = 1e-08
ADAM_WD = 0.01
ADAM_STEP = 10


def _round_up(n, m):
    return (n + m - 1) // m * m


def _params(semantics=None, vmem=VMEM_LIMIT):
    return pltpu.CompilerParams(dimension_semantics=semantics, vmem_limit_bytes=vmem)


def _bdot(a, b):
    return jnp.dot(a.astype(BF16), b.astype(BF16), preferred_element_type=F32)


def _bdot_nt(a, b):
    return lax.dot_general(a.astype(BF16), b.astype(BF16), (((1,), (1,)), ((), ())),
                           preferred_element_type=F32)


def _bdot_tn(a, b):
    return lax.dot_general(a.astype(BF16), b.astype(BF16), (((0,), (0,)), ((), ())),
                           preferred_element_type=F32)


@jax.custom_vjp
def _mm(a, b):
    return _bdot(a, b)


def _mm_fwd(a, b):
    return _bdot(a, b), (a, b)


def _mm_bwd(res, g):
    a, b = res
    return _bdot_nt(g, b), _bdot_tn(a, g)


_mm.defvjp(_mm_fwd, _mm_bwd)


def _seg_sum_raw(x, e):
    hi = x.astype(BF16)
    lo = (x - hi.astype(F32)).astype(BF16)
    return (jnp.dot(hi, e, preferred_element_type=F32)
            + jnp.dot(lo, e, preferred_element_type=F32))


@jax.custom_vjp
def _seg_sum(x, e):
    return _seg_sum_raw(x, e)


def _seg_sum_fwd(x, e):
    return _seg_sum_raw(x, e), e


def _seg_sum_bwd(e, g):
    return _seg_sum_raw(g, e), None


_seg_sum.defvjp(_seg_sum_fwd, _seg_sum_bwd)


def _block_ones(n, seg=HEAD):
    i = np.arange(n) // seg
    return jnp.asarray((i[:, None] == i[None, :]).astype(np.float32), dtype=BF16)


def _rms(xv, g):
    ms = jnp.mean(xv * xv, axis=-1, keepdims=True)
    return xv * lax.rsqrt(ms + EPS_RMS) * g


def _rms_mod(xv, g, shift, scale):
    return _rms(xv, g) * (1.0 + scale) + shift


def _sigmoid(z):
    return 1.0 / (1.0 + jnp.exp(-z))


def _silu(z):
    return z * _sigmoid(z)


def _softplus(z):
    return jnp.maximum(z, 0.0) + jnp.log(1.0 + jnp.exp(-jnp.abs(z)))


class _Cfg:
    def __init__(self, B, TX, TC, D, W, CW, R, GR, KC, F):
        self.B, self.TX, self.TC, self.D = B, TX, TC, D
        self.W, self.CW, self.R, self.GR, self.KC, self.F = W, CW, R, GR, KC, F
        self.T = TX + TC
        self.TT = min(256, TC)
        assert TC % self.TT == 0 and TX % self.TT == 0 and self.TT % LINE == 0
        self.JC = TC // self.TT
        self.JX = TX // self.TT
        self.J = self.JC + self.JX
        self.HP = W // LANES
        self.G = B * self.HP
        self.PW = _round_up(2 * R, LANES)
        self.GP = _round_up(GR, LANES)
        self.SP = 3 * W + 2 * self.PW + self.GP
        self.CP = self.SP + 2 * CW
        self.KP = _round_up(KC, SUBLANES)
        assert self.T % SCAN_CHUNK == 0 and TC % SCAN_CHUNK == 0
        self.NCH = self.T // SCAN_CHUNK
        self.NCC = TC // SCAN_CHUNK
        W_, R_ = W, R
        segs = [(0, 3 * W_, 0),
                (3 * W_, 2 * R_, 3 * W_),
                (3 * W_ + 2 * R_, 2 * R_, 3 * W_ + self.PW),
                (3 * W_ + 4 * R_, GR, 3 * W_ + 2 * self.PW),
                (3 * W_ + 4 * R_ + GR, 2 * CW, self.SP)]
        self.col_segs = segs
        self.shift_cols = 3 * W_ + 4 * R_ + GR
        self.in_cols = self.shift_cols + 2 * CW


def _pad_cols(a, cfg, upto_shift=False):
    width = cfg.SP if upto_shift else cfg.CP
    pieces, pos = [], 0
    for src, n, dst in cfg.col_segs:
        if upto_shift and dst >= cfg.SP:
            break
        if dst > pos:
            pieces.append(jnp.zeros(a.shape[:-1] + (dst - pos,), a.dtype))
        pieces.append(a[..., src:src + n])
        pos = dst + n
    if width > pos:
        pieces.append(jnp.zeros(a.shape[:-1] + (width - pos,), a.dtype))
    return jnp.concatenate(pieces, axis=-1)


def _unpad_cols(a, cfg, upto_shift=False):
    pieces = []
    for src, n, dst in cfg.col_segs:
        if upto_shift and dst >= cfg.SP:
            break
        pieces.append(a[..., dst:dst + n])
    return jnp.concatenate(pieces, axis=-1)


def _pair_weight(w2, cfg):
    R, W = cfg.R, cfg.W
    out = jnp.zeros((cfg.PW, 2 * W), w2.dtype)
    out = out.at[0:R, 0:W].set(w2[0])
    out = out.at[R:2 * R, W:2 * W].set(w2[1])
    return out


def _unpair_weight(g, cfg):
    R, W = cfg.R, cfg.W
    return jnp.stack([g[0:R, 0:W], g[R:2 * R, W:2 * W]])


def _row_ids(n):
    return lax.broadcasted_iota(jnp.int32, (n, 1), 0)


def _shift_rows(z, prev_row, next_row):
    n = z.shape[0]
    rows = _row_ids(n)
    zp = jnp.where(rows == 0, prev_row, pltpu.roll(z, 1, 0))
    zn = jnp.where(rows == n - 1, next_row, pltpu.roll(z, n - 1, 0))
    return zp, zn


def _line_shift(u, d):
    if d == 0:
        return u
    n = u.shape[0]
    lt = _row_ids(n) % LINE
    ok = jnp.logical_and(lt + d >= 0, lt + d < LINE)
    return jnp.where(ok, pltpu.roll(u, (-d) % n, 0), 0.0)


def _conv_tables(cw, kc):
    pad = kc // 2
    t = np.arange(LINE)[None, :]
    d = (np.arange(kc) - pad)[:, None]
    fwd = ((t + d >= 0) & (t + d < LINE)).astype(np.float32)
    bwd = ((t - d >= 0) & (t - d < LINE)).astype(np.float32)
    w = cw[:kc, None, :]
    return jnp.asarray(fwd)[:, :, None] * w, jnp.asarray(bwd)[:, :, None] * w


def _conv_lines(src_ref, wm_ref, dst_ref, kc, transpose):
    pad = kc // 2
    n, width = src_ref.shape
    for l in range(n // LINE):
        for b in range(width // LANES):
            rs, cs = slice(l * LINE, (l + 1) * LINE), slice(b * LANES, (b + 1) * LANES)
            tile = src_ref[rs, cs]
            acc = jnp.zeros_like(tile)
            for i in range(kc):
                d = (pad - i) if transpose else (i - pad)
                acc = acc + pltpu.roll(tile, (-d) % LINE, 0) * wm_ref[i, :, cs]
            dst_ref[rs, cs] = acc


def _rwkv_prep(rw, w0, w2p, a0, a2p, k_k, k_a, e, cfg):
    W, PW = cfg.W, cfg.PW
    r = rw[:, 0:W]
    k = rw[:, W:2 * W]
    v = rw[:, 2 * W:3 * W]
    wdp = rw[:, 3 * W:3 * W + PW]
    adp = rw[:, 3 * W + PW:3 * W + 2 * PW]
    wl = w0 + _mm(jnp.tanh(wdp), w2p)
    w_log = -_softplus(-wl) - 0.5
    decay = jnp.exp(-jnp.exp(w_log))
    iclr = _sigmoid(a0 + _mm(adp, a2p))
    kkr = k * k_k
    nrm = jnp.sqrt(_seg_sum(kkr * kkr, e))
    kk = kkr / jnp.maximum(nrm, 1e-12)
    outs = [r, v, kk]
    for d in range(2):
        ic = iclr[:, d * W:(d + 1) * W]
        outs += [decay[:, d * W:(d + 1) * W], k * (1.0 + (ic - 1.0) * k_a), kk * ic]
    return tuple(outs)


def _glu(cv, cfg):
    return cv[:, :cfg.CW] * _sigmoid(cv[:, cfg.CW:])


def _conv_post(y, cb, lw, lb):
    yf = y + cb
    mu = jnp.mean(yf, axis=-1, keepdims=True)
    var = jnp.mean(jnp.square(yf - mu), axis=-1, keepdims=True)
    return _silu((yf - mu) * lax.rsqrt(var + EPS_LN) * lw + lb)


def _readout(y, kbar, r, v, gd, r_k, gw2, lnx_w, lnx_b, e):
    inv = 1.0 / HEAD
    mu = _seg_sum(y, e) * inv
    yc = y - mu
    var = _seg_sum(yc * yc, e) * inv
    yn = yc * lax.rsqrt(var + EPS_GN) * lnx_w + lnx_b
    bonus = _seg_sum(r * kbar * r_k, e) * v
    g = _mm(_sigmoid(gd), gw2)
    return (yn + bonus) * g


def _post_res(xv, mix, gate, g):
    return xv + gate * _rms(mix, g)


def _head_spec(cfg, tmap):
    return pl.BlockSpec((1, cfg.HP, cfg.TT, LANES), lambda b, j: (b, 0, tmap(j), 0))


def _full_spec(shape):
    n = len(shape)
    return pl.BlockSpec(shape, lambda *_: (0,) * n)


def _to_heads(ref, val, cfg):
    for hp in range(cfg.HP):
        ref[0, hp] = val[:, hp * LANES:(hp + 1) * LANES]


def _from_heads(ref, cfg):
    return jnp.concatenate([ref[0, hp] for hp in range(cfg.HP)], axis=-1)


def _in_proj(xcat, modt, g1, w_in_p, cfg):
    B, T, D, TT, CP = cfg.B, cfg.T, cfg.D, cfg.TT, cfg.CP

    def body(x_ref, mod_ref, g_ref, w_ref, p_ref, h_ref):
        h = _rms_mod(x_ref[0], g_ref[...], mod_ref[0, 0, 0:1, :], mod_ref[0, 0, 1:2, :])
        hb = h.astype(BF16)
        h_ref[0] = hb
        p_ref[0] = jnp.dot(hb, w_ref[...], preferred_element_type=F32)

    return pl.pallas_call(
        body, name="in_proj", grid=(B, cfg.J),
        in_specs=[pl.BlockSpec((1, TT, D), lambda b, j: (b, j, 0)),
                  pl.BlockSpec((1, 1, 2, D), lambda b, j: (b, j, 0, 0)),
                  _full_spec((1, D)), _full_spec((D, CP))],
        out_specs=[pl.BlockSpec((1, TT, CP), lambda b, j: (b, j, 0)),
                   pl.BlockSpec((1, TT, D), lambda b, j: (b, j, 0))],
        out_shape=[jax.ShapeDtypeStruct((B, T, CP), F32), jax.ShapeDtypeStruct((B, T, D), BF16)],
        compiler_params=_params(("parallel", "parallel")),
    )(xcat, modt, g1, w_in_p)


def _halo_specs(cfg, width):
    per = cfg.TT // SUBLANES
    last = cfg.T // SUBLANES - 1
    prev = pl.BlockSpec((1, SUBLANES, width), lambda b, j: (b, jnp.maximum(j * per - 1, 0), 0))
    nxt = pl.BlockSpec((1, SUBLANES, width), lambda b, j: (b, jnp.minimum((j + 1) * per, last), 0))
    return prev, nxt


def _halo_flags(j, cfg):
    has_prev = jnp.logical_and(j != 0, j != cfg.JC).astype(F32)
    has_next = jnp.logical_and(j != cfg.JC - 1, j != cfg.J - 1).astype(F32)
    return has_prev, has_next


def _shifted(p_ref, prev_ref, next_ref, mup, mun, j, cfg):
    SP = cfg.SP
    has_prev, has_next = _halo_flags(j, cfg)
    z = p_ref[0][:, :SP]
    zp, zn = _shift_rows(z, prev_ref[0, SUBLANES - 1:SUBLANES, :] * has_prev, next_ref[0, 0:1, :] * has_next)
    return z, zp, zn, z + mup * (zp - z) + mun * (zn - z)


def _mix_prep(p, mup, mun, w0, w2p, a0, a2p, k_k, k_a, wm, cb, clw, clb, e_w, cfg):
    B, T, TT, SP, CP, W, CW, HP, JC = cfg.B, cfg.T, cfg.TT, cfg.SP, cfg.CP, cfg.W, cfg.CW, cfg.HP, cfg.JC

    def body(p_ref, prev_ref, next_ref, mup_ref, mun_ref, w0_ref, w2_ref, a0_ref, a2_ref, kk_ref, ka_ref,
             wm_ref, cb_ref, clw_ref, clb_ref, e_ref, *rest):
        outs, u_ref = rest[:12], rest[12]
        j = pl.program_id(1)
        _, _, _, rw = _shifted(p_ref, prev_ref, next_ref, mup_ref[...], mun_ref[...], j, cfg)
        vals = _rwkv_prep(rw, w0_ref[...], w2_ref[...], a0_ref[...], a2_ref[...], kk_ref[...], ka_ref[...],
                          e_ref[...], cfg)
        for ref, val in zip(outs[:9], vals):
            _to_heads(ref, val, cfg)
        outs[9][0] = rw[:, 3 * W + 2 * cfg.PW:SP]

        @pl.when(j >= JC)
        def _():
            u_ref[...] = _glu(p_ref[0, :, SP:], cfg)
            _conv_lines(u_ref, wm_ref, outs[11].at[0], cfg.KC, False)
            outs[10][0] = _conv_post(outs[11][0], cb_ref[...], clw_ref[...], clb_ref[...])

    prev, nxt = _halo_specs(cfg, SP)
    head = jax.ShapeDtypeStruct((B, HP, T, LANES), F32)
    tile = lambda n: pl.BlockSpec((1, TT, n), lambda b, j: (b, j, 0))
    return pl.pallas_call(
        body, name="mix_prep", grid=(B, cfg.J),
        in_specs=[tile(CP), prev, nxt,
                  _full_spec((1, SP)), _full_spec((1, SP)),
                  _full_spec((1, 2 * W)), _full_spec((cfg.PW, 2 * W)),
                  _full_spec((1, 2 * W)), _full_spec((cfg.PW, 2 * W)),
                  _full_spec((1, W)), _full_spec((1, W)),
                  pl.BlockSpec((cfg.KC, LINE, CW), lambda b, j: (0, 0, 0), pipeline_mode=pl.Buffered(1)),
                  _full_spec((1, CW)), _full_spec((1, CW)), _full_spec((1, CW)),
                  _full_spec((W, W))],
        out_specs=[_head_spec(cfg, lambda j: j)] * 9 + [tile(cfg.GP), tile(CW), tile(CW)],
        out_shape=[head] * 9 + [jax.ShapeDtypeStruct((B, T, cfg.GP), F32),
                                jax.ShapeDtypeStruct((B, T, CW), F32), jax.ShapeDtypeStruct((B, T, CW), F32)],
        scratch_shapes=[pltpu.VMEM((TT, CW), F32)],
        compiler_params=_params(("parallel", "parallel")),
    )(p, p, p, mup, mun, w0, w2p, a0, a2p, k_k, k_a, wm, cb, clw, clb, e_w)


def _chunk_pos(c, reverse, cfg):
    if not reverse:
        return c
    return jnp.where(c < cfg.NCC, cfg.NCC - 1 - c, cfg.NCH - 1 + cfg.NCC - c)


def _diag_mask():
    r = lax.broadcasted_iota(jnp.int32, (HEAD, LANES), 0)
    l = lax.broadcasted_iota(jnp.int32, (HEAD, LANES), 1)
    return (r == l % HEAD).astype(F32)


def _col_lhs(row, diag_b):
    hi = row.astype(BF16)
    lo = (row - hi.astype(F32)).astype(BF16)
    return diag_b * hi, diag_b * lo


def _col_dot(row_list, diag_b, e2):
    n, g = len(row_list), row_list[0].shape[0]
    lhs = jnp.concatenate([jnp.concatenate(_col_lhs(r, diag_b), axis=-1) for r in row_list], axis=0)
    out = jnp.dot(lhs.reshape(n * g * HEAD, 2 * LANES), e2, preferred_element_type=F32)
    return out.reshape(n, g, HEAD, LANES)


def _col_form(row):
    n = row.shape[0]
    t = jnp.swapaxes(jnp.broadcast_to(row, (n, LANES, LANES)), 1, 2)
    lane = lax.broadcasted_iota(jnp.int32, (HEAD, LANES), 1)
    return jnp.where(lane < HEAD, t[:, :HEAD, :], t[:, HEAD:, :])


def _col_both(row, diag_b, e2):
    half = row.shape[0] // 2
    return jnp.concatenate([_col_form(row[:half]), _col_dot([row[half:]], diag_b, e2)[0]], axis=0)


def _both_rows(ins, idx, i):
    return jnp.concatenate([ins[d][idx][:, pl.ds(_tok(i, d == 1), 1), :] for d in range(2)], axis=0)


def _seg_dot(blocks, e):
    n, g = len(blocks), blocks[0].shape[0]
    lhs = jnp.concatenate(blocks, axis=0).reshape(n * g * HEAD, LANES)
    return jnp.dot(lhs, e, preferred_element_type=F32).reshape(n, g, HEAD, LANES)


def _tok(i, reverse):
    return (SCAN_CHUNK - 1 - i) if reverse else i


def _scan_fwd(ops_f, ops_b, e128, e256, cfg):
    G, T, NCH = cfg.G, cfg.T, cfg.NCH
    CH = SCAN_CHUNK
    G2 = 2 * G

    def body(*refs):
        ins = (refs[0:6], refs[6:12])
        e_ref, e2_ref = refs[12], refs[13]
        ys, hist_ref, fin_ref = (refs[14], refs[15]), refs[16], refs[17]
        s_ref, mm_ref = refs[18], refs[19]
        c = pl.program_id(0)

        @pl.when(c == 0)
        def _():
            s_ref[...] = jnp.zeros_like(s_ref)

        diag = _diag_mask()
        diag_b = diag.astype(BF16)
        e, e2 = e_ref[...], e2_ref[...]
        rows = functools.partial(_both_rows, ins)

        mm_ref[0] = _seg_dot([s_ref[...].astype(BF16) * (-rows(4, 0)).astype(BF16)], e)[0]
        mm_ref[1] = _col_both(rows(3, 0), diag_b, e2)

        def step(i, carry):
            nxt = jnp.minimum(i + 1, CH - 1)
            res = []
            for d in range(2):
                sl = slice(d * G, (d + 1) * G)
                row = lambda idx, ii: ins[d][idx][:, pl.ds(_tok(ii, d == 1), 1), :]
                s_old = s_ref[sl]
                hist_ref[i, sl] = s_old
                S = s_old * row(1, i) + mm_ref[0, sl] * row(5, i) + mm_ref[1, sl] * row(2, i)
                s_ref[sl] = S
                sb = S.astype(BF16)
                res.append(_seg_dot([sb * (-row(4, nxt)).astype(BF16), sb * row(0, i).astype(BF16)], e))
            for d in range(2):
                sl = slice(d * G, (d + 1) * G)
                v_next = ins[d][3][:, pl.ds(_tok(nxt, d == 1), 1), :]
                mm_ref[0, sl] = res[d][0]
                mm_ref[1, sl] = _col_form(v_next) if d == 0 else _col_dot([v_next], diag_b, e2)[0]
                ys[d][:, pl.ds(_tok(i, d == 1), 1), :] = jnp.sum(diag * res[d][1], axis=1, keepdims=True)
            return carry

        lax.fori_loop(0, CH, step, 0)
        fin_ref[...] = s_ref[...]

    toks = [pl.BlockSpec((G, CH, LANES), lambda c, rev=rev: (0, _chunk_pos(c, rev, cfg), 0)) for rev in (False, True)]
    y_shape = jax.ShapeDtypeStruct((G, T, LANES), F32)
    return pl.pallas_call(
        body, name="scan_fwd", grid=(NCH,),
        in_specs=[toks[0]] * 6 + [toks[1]] * 6 + [_full_spec((LANES, LANES)), _full_spec((2 * LANES, LANES))],
        out_specs=[toks[0], toks[1], pl.BlockSpec((CH, G2, HEAD, LANES), lambda c: (c, 0, 0, 0)),
                   _full_spec((G2, HEAD, LANES))],
        out_shape=[y_shape, y_shape, jax.ShapeDtypeStruct(((NCH + 1) * CH, G2, HEAD, LANES), F32),
                   jax.ShapeDtypeStruct((G2, HEAD, LANES), F32)],
        scratch_shapes=[pltpu.VMEM((G2, HEAD, LANES), F32), pltpu.VMEM((2, G2, HEAD, LANES), F32)],
        compiler_params=_params(("arbitrary",)),
    )(*ops_f, *ops_b, e128, e256)


def _mix_out(yf, yb, kdf, kdb, r, v, gd, conv, x, mod2, r_k, gw2p, lnx_w, lnx_b, w_out, g2, e_w, cfg):
    B, TX, D, TT, W, CW, JC = cfg.B, cfg.TX, cfg.D, cfg.TT, cfg.W, cfg.CW, cfg.JC

    def body(yf_ref, yb_ref, kdf_ref, kdb_ref, r_ref, v_ref, gd_ref, cv_ref, x_ref, mod_ref,
             rk_ref, gw_ref, lw_ref, lb_ref, wo_ref, g_ref, e_ref, x1_ref):
        y = _from_heads(yf_ref, cfg) + _from_heads(yb_ref, cfg)
        kbar = 0.5 * (_from_heads(kdf_ref, cfg) + _from_heads(kdb_ref, cfg))
        ro = _readout(y, kbar, _from_heads(r_ref, cfg), _from_heads(v_ref, cfg), gd_ref[0], rk_ref[...],
                      gw_ref[...], lw_ref[...], lb_ref[...], e_ref[...])
        cat = jnp.concatenate([ro, cv_ref[0]], axis=-1)
        mix = _bdot(cat, wo_ref[...])
        x1_ref[0] = _post_res(x_ref[0], mix, mod_ref[0], g_ref[...])

    hs = _head_spec(cfg, lambda j: j + JC)
    lat = lambda n: pl.BlockSpec((1, TT, n), lambda b, j: (b, j + JC, 0))
    return pl.pallas_call(
        body, name="mix_out", grid=(B, cfg.JX),
        in_specs=[hs] * 6 + [lat(cfg.GP), lat(CW),
                             pl.BlockSpec((1, TT, D), lambda b, j: (b, j, 0)),
                             pl.BlockSpec((1, 1, D), lambda b, j: (b, 0, 0)),
                             _full_spec((1, W)), _full_spec((cfg.GP, W)), _full_spec((1, W)), _full_spec((1, W)),
                             _full_spec((W + CW, D)), _full_spec((1, D)), _full_spec((W, W))],
        out_specs=pl.BlockSpec((1, TT, D), lambda b, j: (b, j, 0)),
        out_shape=jax.ShapeDtypeStruct((B, TX, D), F32),
        compiler_params=_params(("parallel", "parallel")),
    )(yf, yb, kdf, kdb, r, v, gd, conv, x, mod2, r_k, gw2p, lnx_w, lnx_b, w_out, g2, e_w)


def _acc(ref, val, first):
    @pl.when(first)
    def _():
        ref[...] = val

    @pl.when(jnp.logical_not(first))
    def _():
        ref[...] += val


def _mlp_fwd_bwd(x1, tgt, mod345, g3, g4, w1, w2, cfg):
    B, TX, D, TT, F, JX = cfg.B, cfg.TX, cfg.D, cfg.TT, cfg.F, cfg.JX

    def body(x1_ref, t_ref, mod_ref, g3_ref, g4_ref, w1_ref, w2_ref,
             dx1_ref, loss_ref, h2_ref, dpre_ref, act_ref, dff_ref, dmod_ref, dg3_ref, dg4_ref):
        b, j = pl.program_id(0), pl.program_id(1)
        x1v = x1_ref[0]
        sh, sc, gt = mod_ref[0, 0:1, :], mod_ref[0, 1:2, :], mod_ref[0, 2:3, :]
        h2, vjp_pre = jax.vjp(_rms_mod, x1v, g3_ref[...], sh, sc)
        h2b = h2.astype(BF16)
        pre = jnp.dot(h2b, w1_ref[...], preferred_element_type=F32)
        rl = jnp.maximum(pre, 0.0)
        actb = (rl * rl).astype(BF16)
        ff = jnp.dot(actb, w2_ref[...], preferred_element_type=F32)
        x2, vjp_post = jax.vjp(_post_res, x1v, ff, gt, g4_ref[...])
        err = x2 - t_ref[0]
        loss = 0.5 * jnp.sum(jnp.mean(err * err, axis=-1, keepdims=True))
        dx1a, dff, dgt, dg4 = vjp_post(err * (1.0 / D))
        dffb = dff.astype(BF16)
        dpre = _bdot_nt(dffb, w2_ref[...]) * (2.0 * rl)
        dpreb = dpre.astype(BF16)
        dx1b, dg3, dsh, dsc = vjp_pre(_bdot_nt(dpreb, w1_ref[...]))
        dx1_ref[0] = dx1a + dx1b
        loss_ref[0, 0] = jnp.zeros((SUBLANES, LANES), F32) + loss
        h2_ref[0] = h2b
        dpre_ref[0] = dpreb
        act_ref[0] = actb
        dff_ref[0] = dffb
        _acc(dmod_ref, jnp.concatenate([dsh, dsc, dgt], axis=0)[None], j == 0)
        first = jnp.logical_and(b == 0, j == 0)
        _acc(dg3_ref, dg3, first)
        _acc(dg4_ref, dg4, first)

    tile = lambda n: pl.BlockSpec((1, TT, n), lambda b, j: (b, j, 0))
    return pl.pallas_call(
        body, name="mlp_fwd_bwd", grid=(B, JX),
        in_specs=[tile(D), tile(D), pl.BlockSpec((1, 3, D), lambda b, j: (b, 0, 0)),
                  _full_spec((1, D)), _full_spec((1, D)),
                  pl.BlockSpec((D, F), lambda b, j: (0, 0), pipeline_mode=pl.Buffered(1)),
                  pl.BlockSpec((F, D), lambda b, j: (0, 0), pipeline_mode=pl.Buffered(1))],
        out_specs=[tile(D), pl.BlockSpec((1, 1, SUBLANES, LANES), lambda b, j: (b, j, 0, 0)),
                   tile(D), tile(F), tile(F), tile(D),
                   pl.BlockSpec((1, 3, D), lambda b, j: (b, 0, 0)),
                   _full_spec((1, D)), _full_spec((1, D))],
        out_shape=[jax.ShapeDtypeStruct((B, TX, D), F32),
                   jax.ShapeDtypeStruct((B, JX, SUBLANES, LANES), F32),
                   jax.ShapeDtypeStruct((B, TX, D), BF16), jax.ShapeDtypeStruct((B, TX, F), BF16),
                   jax.ShapeDtypeStruct((B, TX, F), BF16), jax.ShapeDtypeStruct((B, TX, D), BF16),
                   jax.ShapeDtypeStruct((B, 3, D), F32),
                   jax.ShapeDtypeStruct((1, D), F32), jax.ShapeDtypeStruct((1, D), F32)],
        compiler_params=_params(("arbitrary", "arbitrary")),
    )(x1, tgt, mod345, g3, g4, w1, w2)


def _mix_out_bwd(yf, yb, kdf, kdb, r, v, gd, conv, x, mod2, r_k, gw2p, lnx_w, lnx_b, w_out, g2, e_w, dx1, cfg):
    B, TX, D, TT, W, CW, JC, HP, GP = cfg.B, cfg.TX, cfg.D, cfg.TT, cfg.W, cfg.CW, cfg.JC, cfg.HP, cfg.GP

    def body(yf_ref, yb_ref, kdf_ref, kdb_ref, r_ref, v_ref, gd_ref, cv_ref, x_ref, mod_ref,
             rk_ref, gw_ref, lw_ref, lb_ref, wo_ref, g_ref, e_ref, dx1_ref,
             dy_ref, dkb_ref, dr_ref, dv_ref, dgd_ref, dcv_ref, cat_ref, dmix_ref,
             dmod_ref, dg2_ref, drk_ref, dgw_ref, dlw_ref, dlb_ref):
        b, j = pl.program_id(0), pl.program_id(1)
        e = e_ref[...]
        y = _from_heads(yf_ref, cfg) + _from_heads(yb_ref, cfg)
        kbar = 0.5 * (_from_heads(kdf_ref, cfg) + _from_heads(kdb_ref, cfg))
        ro, vjp_ro = jax.vjp(lambda *a: _readout(*a, e), y, kbar, _from_heads(r_ref, cfg),
                             _from_heads(v_ref, cfg), gd_ref[0], rk_ref[...], gw_ref[...], lw_ref[...], lb_ref[...])
        catb = jnp.concatenate([ro, cv_ref[0]], axis=-1).astype(BF16)
        mix = jnp.dot(catb, wo_ref[...], preferred_element_type=F32)
        _, vjp_post = jax.vjp(_post_res, x_ref[0], mix, mod_ref[0], g_ref[...])
        _, dmix, dgate, dg2 = vjp_post(dx1_ref[0])
        dmixb = dmix.astype(BF16)
        dcat = _bdot_nt(dmixb, wo_ref[...])
        dy, dkb, dr, dv, dgd, drk, dgw, dlw, dlb = vjp_ro(dcat[:, :W])
        _to_heads(dy_ref, dy, cfg)
        _to_heads(dkb_ref, dkb, cfg)
        _to_heads(dr_ref, dr, cfg)
        _to_heads(dv_ref, dv, cfg)
        dgd_ref[0] = dgd
        dcv_ref[0] = dcat[:, W:]
        cat_ref[0] = catb
        dmix_ref[0] = dmixb
        _acc(dmod_ref, dgate[None], j == 0)
        first = jnp.logical_and(b == 0, j == 0)
        _acc(dg2_ref, dg2, first)
        _acc(drk_ref, drk, first)
        _acc(dgw_ref, dgw, first)
        _acc(dlw_ref, dlw, first)
        _acc(dlb_ref, dlb, first)

    hs = _head_spec(cfg, lambda j: j + JC)
    ho = _head_spec(cfg, lambda j: j)
    lat = lambda n: pl.BlockSpec((1, TT, n), lambda b, j: (b, j + JC, 0))
    tile = lambda n: pl.BlockSpec((1, TT, n), lambda b, j: (b, j, 0))
    head = jax.ShapeDtypeStruct((B, HP, TX, LANES), F32)
    vec = lambda n: jax.ShapeDtypeStruct((1, n), F32)
    return pl.pallas_call(
        body, name="mix_out_bwd", grid=(B, cfg.JX),
        in_specs=[hs] * 6 + [lat(GP), lat(CW), tile(D),
                             pl.BlockSpec((1, 1, D), lambda b, j: (b, 0, 0)),
                             _full_spec((1, W)), _full_spec((GP, W)), _full_spec((1, W)), _full_spec((1, W)),
                             _full_spec((W + CW, D)), _full_spec((1, D)), _full_spec((W, W)), tile(D)],
        out_specs=[ho] * 4 + [tile(GP), tile(CW), tile(W + CW), tile(D),
                              pl.BlockSpec((1, 1, D), lambda b, j: (b, 0, 0)),
                              _full_spec((1, D)), _full_spec((1, W)), _full_spec((GP, W)),
                              _full_spec((1, W)), _full_spec((1, W))],
        out_shape=[head] * 4 + [jax.ShapeDtypeStruct((B, TX, GP), F32), jax.ShapeDtypeStruct((B, TX, CW), F32),
                                jax.ShapeDtypeStruct((B, TX, W + CW), BF16), jax.ShapeDtypeStruct((B, TX, D), BF16),
                                jax.ShapeDtypeStruct((B, 1, D), F32),
                                vec(D), vec(W), jax.ShapeDtypeStruct((GP, W), F32), vec(W), vec(W)],
        compiler_params=_params(("arbitrary", "arbitrary")),
    )(yf, yb, kdf, kdb, r, v, gd, conv, x, mod2, r_k, gw2p, lnx_w, lnx_b, w_out, g2, e_w, dx1)


def _scan_bwd(ops_f, ops_b, dy, hist, e128, e256, cfg):
    G, T, NCH, NCC = cfg.G, cfg.T, cfg.NCH, cfg.NCC
    CH = SCAN_CHUNK
    G2 = 2 * G

    def body(*refs):
        ins = (refs[0:6], refs[6:12])
        dys, hist_ref, next_ref, e_ref, e2_ref = (refs[12], refs[13]), refs[14], refs[15], refs[16], refs[17]
        outs = (refs[18:24], refs[24:30])
        ds_ref, mm_ref = refs[30], refs[31]
        gi = pl.program_id(0)

        @pl.when(gi == 0)
        def _():
            ds_ref[...] = jnp.zeros_like(ds_ref)

        diag = _diag_mask()
        diag_b = diag.astype(BF16)
        e, e2 = e_ref[...], e2_ref[...]
        rows = functools.partial(_both_rows, ins)
        latent = [(_chunk_pos(NCH - 1 - gi, d == 1, cfg) >= NCC).astype(F32) for d in range(2)]

        def dy_rows(i):
            return jnp.concatenate([dys[d][:, pl.ds(_tok(i, d == 1), 1), :] * latent[d] for d in range(2)], axis=0)

        def put(idx, i, val):
            outs[0][idx][:, pl.ds(_tok(i, False), 1), :] = val[:G]
            outs[1][idx][:, pl.ds(_tok(i, True), 1), :] = val[G:]

        rsum = lambda z: jnp.sum(z, axis=1, keepdims=True)

        def prepare(i):
            return (_col_form(dy_rows(i)), _col_dot([rows(3, i)], diag_b, e2)[0],
                    hist_ref[i].astype(BF16) * (-rows(4, i)).astype(BF16))

        dyc0, vb0, sa_lhs = prepare(CH - 1)
        mm_ref[0] = dyc0
        mm_ref[1] = vb0
        mm_ref[2] = _seg_dot([sa_lhs], e)[0]

        def one_step(i, s_after):
            prv = jnp.maximum(i - 1, 0)
            sp, dyc = hist_ref[i], mm_ref[0]
            ds = ds_ref[...] + dyc * rows(0, i)
            put(0, i, rsum(s_after * dyc))
            put(1, i, rsum(ds * sp))
            put(5, i, rsum(ds * mm_ref[2]))
            put(2, i, rsum(ds * mm_ref[1]))
            dyc_n, vb_n, sa_lhs_n = prepare(prv)
            dsb = ds.astype(BF16)
            res = _seg_dot([dsb * rows(5, i).astype(BF16), dsb * rows(2, i).astype(BF16), sa_lhs_n], e)
            dsa = res[0]
            put(4, i, -rsum(sp * dsa))
            put(3, i, rsum(diag * res[1]))
            mm_ref[0] = dyc_n
            mm_ref[1] = vb_n
            mm_ref[2] = res[2]
            ds_ref[...] = ds * rows(1, i) - dsa * rows(4, i)

        one_step(CH - 1, next_ref[0])

        def bstep(ii, carry):
            i = CH - 1 - ii
            one_step(i, hist_ref[i + 1])
            return carry

        lax.fori_loop(1, CH, bstep, 0)

    cpos = lambda g, rev: _chunk_pos(NCH - 1 - g, rev, cfg)
    toks = [pl.BlockSpec((G, CH, LANES), lambda g, rev=rev: (0, cpos(g, rev), 0)) for rev in (False, True)]
    dy_specs = [pl.BlockSpec((G, CH, LANES), lambda g, rev=rev: (0, jnp.maximum(cpos(g, rev) - NCC, 0), 0))
                for rev in (False, True)]
    out = jax.ShapeDtypeStruct((G, T, LANES), F32)
    res = pl.pallas_call(
        body, name="scan_bwd", grid=(NCH,),
        in_specs=[toks[0]] * 6 + [toks[1]] * 6 + dy_specs
                 + [pl.BlockSpec((CH, G2, HEAD, LANES), lambda g: (NCH - 1 - g, 0, 0, 0)),
                    pl.BlockSpec((1, G2, HEAD, LANES), lambda g: ((NCH - g) * CH, 0, 0, 0)),
                    _full_spec((LANES, LANES)), _full_spec((2 * LANES, LANES))],
        out_specs=[toks[0]] * 6 + [toks[1]] * 6,
        out_shape=[out] * 12,
        scratch_shapes=[pltpu.VMEM((G2, HEAD, LANES), F32), pltpu.VMEM((3, G2, HEAD, LANES), F32)],
        compiler_params=_params(("arbitrary",)),
    )(*ops_f, *ops_b, dy, dy, hist, hist, e128, e256)
    return res[:6], res[6:]


def _mix_prep_bwd(p, mup, mun, w0, w2p, a0, a2p, k_k, k_a, wm, cb, clw, clb, e_w, yconv, sf, sb, ro, xch, cfg):
    B, T, TT, SP, CP, W, CW, HP, JC, PW, GP, KC, KP = (cfg.B, cfg.T, cfg.TT, cfg.SP, cfg.CP, cfg.W, cfg.CW,
                                                       cfg.HP, cfg.JC, cfg.PW, cfg.GP, cfg.KC, cfg.KP)
    pad = KC // 2
    nx = len(xch)

    def body(p_ref, prev_ref, next_ref, mup_ref, mun_ref, w0_ref, w2_ref, a0_ref, a2_ref, kk_ref, ka_ref,
             wm_ref, cb_ref, clw_ref, clb_ref, e_ref, yc_ref, *rest):
        sf_refs, sb_refs = rest[0:6], rest[6:12]
        rdr_ref, rdv_ref, rdkb_ref, rdgd_ref, rdcv_ref = rest[12:17]
        xin, rest = rest[17:17 + nx], rest[17 + nx:]
        (dpz_ref, dmup_ref, dmun_ref, dw0_ref, dw2_ref, da0_ref, da2_ref, dkk_ref, dka_ref,
         dcw_ref, dcb_ref, dclw_ref, dclb_ref) = rest[:13]
        xout, (dyc_ref, du_ref), sems = rest[13:13 + nx], rest[13 + nx:15 + nx], rest[15 + nx:]
        b, j = pl.program_id(0), pl.program_id(1)
        first = jnp.logical_and(b == 0, j == 0)

        @pl.when(first)
        def _():
            _exchange_start(_exchange_copies(xin, xout, *sems, [True] * nx))
        lat = (j >= JC).astype(F32)
        e = e_ref[...]
        mup_v, mun_v = mup_ref[...], mun_ref[...]
        z, zp, zn, rw = _shifted(p_ref, prev_ref, next_ref, mup_v, mun_v, j, cfg)

        def prep(rw_, w0_, w2_, a0_, a2_, kk_, ka_):
            return _rwkv_prep(rw_, w0_, w2_, a0_, a2_, kk_, ka_, e, cfg) + (rw_[:, 3 * W + 2 * PW:SP],)

        _, vjp_prep = jax.vjp(prep, rw, w0_ref[...], w2_ref[...], a0_ref[...], a2_ref[...], kk_ref[...], ka_ref[...])
        fr, fw, fk, fv, fkk, fb = [_from_heads(r_, cfg) for r_ in sf_refs]
        br, bw, bk, bv, bkk, bb = [_from_heads(r_, cfg) for r_ in sb_refs]
        half_kb = (0.5 * lat) * _from_heads(rdkb_ref, cfg)
        cots = (fr + br + lat * _from_heads(rdr_ref, cfg), fv + bv + lat * _from_heads(rdv_ref, cfg), fkk + bkk,
                fw, fk + half_kb, fb, bw, bk + half_kb, bb, lat * rdgd_ref[0])
        drw, dw0, dw2, da0, da2, dkk, dka = vjp_prep(cots)
        _acc(dmup_ref, jnp.sum(drw * (zp - z), axis=0, keepdims=True), first)
        _acc(dmun_ref, jnp.sum(drw * (zn - z), axis=0, keepdims=True), first)
        for ref, val in ((dw0_ref, dw0), (dw2_ref, dw2), (da0_ref, da0), (da2_ref, da2), (dkk_ref, dkk), (dka_ref, dka)):
            _acc(ref, val, first)

        dpz_ref[0, :, 0:SP] = drw

        @pl.when(first)
        def _():
            for ref in (dcw_ref, dcb_ref, dclw_ref, dclb_ref):
                ref[...] = jnp.zeros_like(ref)

        @pl.when(j < JC)
        def _():
            dpz_ref[0, :, SP:] = jnp.zeros((TT, 2 * CW), F32)

        @pl.when(j >= JC)
        def _():
            u, vjp_glu = jax.vjp(lambda c_: _glu(c_, cfg), p_ref[0, :, SP:])
            _, vjp_post = jax.vjp(_conv_post, yc_ref[0], cb_ref[...], clw_ref[...], clb_ref[...])
            dyc, dcb, dclw, dclb = vjp_post(rdcv_ref[0])
            dyc_ref[...] = dyc
            _conv_lines(dyc_ref, wm_ref, du_ref, KC, True)
            (dcv,) = vjp_glu(du_ref[...])
            dpz_ref[0, :, SP:] = dcv
            for i in range(KC):
                dcw_ref[i:i + 1, :] += jnp.sum(dyc * _line_shift(u, i - pad), axis=0, keepdims=True)
            dcb_ref[...] += dcb
            dclw_ref[...] += dclw
            dclb_ref[...] += dclb

        @pl.when(jnp.logical_and(b == B - 1, j == cfg.J - 1))
        def _():
            _exchange_wait(_exchange_copies(xin, xout, *sems, [True] * nx))

    prev, nxt = _halo_specs(cfg, SP)
    hs = _head_spec(cfg, lambda j: j)
    hl = _head_spec(cfg, lambda j: jnp.maximum(j - JC, 0))
    latn = lambda n: pl.BlockSpec((1, TT, n), lambda b, j: (b, jnp.maximum(j - JC, 0), 0))
    hbm = pl.BlockSpec(memory_space=pltpu.HBM)
    vec = lambda n: jax.ShapeDtypeStruct((1, n), F32)
    small_shapes = [vec(SP), vec(SP), vec(2 * W), jax.ShapeDtypeStruct((PW, 2 * W), F32), vec(2 * W),
                    jax.ShapeDtypeStruct((PW, 2 * W), F32), vec(W), vec(W),
                    jax.ShapeDtypeStruct((KP, CW), F32), vec(CW), vec(CW), vec(CW)]
    return pl.pallas_call(
        body, name="mix_prep_bwd", grid=(B, cfg.J),
        in_specs=[pl.BlockSpec((1, TT, CP), lambda b, j: (b, j, 0)), prev, nxt,
                  _full_spec((1, SP)), _full_spec((1, SP)),
                  _full_spec((1, 2 * W)), _full_spec((PW, 2 * W)),
                  _full_spec((1, 2 * W)), _full_spec((PW, 2 * W)),
                  _full_spec((1, W)), _full_spec((1, W)),
                  pl.BlockSpec((KC, LINE, CW), lambda b, j: (0, 0, 0), pipeline_mode=pl.Buffered(1)),
                  _full_spec((1, CW)), _full_spec((1, CW)), _full_spec((1, CW)),
                  _full_spec((W, W)), pl.BlockSpec((1, TT, CW), lambda b, j: (b, j, 0))]
                 + [hs] * 12 + [hl] * 3 + [latn(GP), latn(CW)] + [hbm] * nx,
        out_specs=[pl.BlockSpec((1, TT, CP), lambda b, j: (b, j, 0))] + [_full_spec(s.shape) for s in small_shapes]
                  + [hbm] * nx,
        out_shape=[jax.ShapeDtypeStruct((B, T, CP), F32)] + small_shapes + _exchange_shapes(xch, [True] * nx),
        scratch_shapes=[pltpu.VMEM((TT, CW), F32), pltpu.VMEM((TT, CW), F32)] + _exchange_sems(nx),
        compiler_params=_params(("arbitrary", "arbitrary")),
    )(p, p, p, mup, mun, w0, w2p, a0, a2p, k_k, k_a, wm, cb, clw, clb, e_w, yconv, *sf, *sb, *ro, *xch)


def _in_proj_bwd(dpz, xcat, modt, g1, w_in_p, mup, mun, dx1, cfg):
    B, T, TX, D, TT, SP, CP, JC = cfg.B, cfg.T, cfg.TX, cfg.D, cfg.TT, cfg.SP, cfg.CP, cfg.JC

    def body(d_ref, prev_ref, next_ref, x_ref, mod_ref, g_ref, w_ref, mup_ref, mun_ref, dx1_ref,
             gx_ref, dp_ref, dmod_ref, dg_ref):
        b, j = pl.program_id(0), pl.program_id(1)
        has_prev, has_next = _halo_flags(j, cfg)
        mp, mn = mup_ref[...], mun_ref[...]
        drw = d_ref[0, :, 0:SP]
        dprev, dnext = _shift_rows(drw, prev_ref[0, SUBLANES - 1:SUBLANES, :] * has_prev,
                                   next_ref[0, 0:1, :] * has_next)
        dz = drw * (1.0 - mp - mn) + mp * dnext + mn * dprev
        dpb = jnp.concatenate([dz, d_ref[0, :, SP:]], axis=-1).astype(BF16)
        dp_ref[0] = dpb
        dh = _bdot_nt(dpb, w_ref[...])
        _, vjp_h = jax.vjp(_rms_mod, x_ref[0], g_ref[...], mod_ref[0, 0, 0:1, :], mod_ref[0, 0, 1:2, :])
        dx, dg, dsh, dsc = vjp_h(dh)
        dmod_ref[0, 0] = jnp.concatenate([dsh, dsc], axis=0)
        _acc(dg_ref, dg, jnp.logical_and(b == 0, j == 0))

        @pl.when(j >= JC)
        def _():
            gx_ref[0] = dx + dx1_ref[0]

    prev, nxt = _halo_specs(cfg, SP)
    lat = pl.BlockSpec((1, TT, D), lambda b, j: (b, jnp.maximum(j - JC, 0), 0))
    return pl.pallas_call(
        body, name="in_proj_bwd", grid=(B, cfg.J),
        in_specs=[pl.BlockSpec((1, TT, CP), lambda b, j: (b, j, 0)), prev, nxt,
                  pl.BlockSpec((1, TT, D), lambda b, j: (b, j, 0)),
                  pl.BlockSpec((1, 1, 2, D), lambda b, j: (b, j, 0, 0)),
                  _full_spec((1, D)), _full_spec((D, CP)), _full_spec((1, SP)), _full_spec((1, SP)), lat],
        out_specs=[lat, pl.BlockSpec((1, TT, CP), lambda b, j: (b, j, 0)),
                   pl.BlockSpec((1, 1, 2, D), lambda b, j: (b, j, 0, 0)), _full_spec((1, D))],
        out_shape=[jax.ShapeDtypeStruct((B, TX, D), F32), jax.ShapeDtypeStruct((B, T, CP), BF16),
                   jax.ShapeDtypeStruct((B, cfg.J, 2, D), F32), jax.ShapeDtypeStruct((1, D), F32)],
        compiler_params=_params(("arbitrary", "arbitrary")),
    )(dpz, dpz, dpz, xcat, modt, g1, w_in_p, mup, mun, dx1)


def _pick_tile(n, pref):
    for t in pref:
        if n % t == 0:
            return t
    return n


def _grad_matmul(a, g, name):
    K, M = a.shape
    N = g.shape[1]
    tm = _pick_tile(M, (512, 256, 128))
    tn = _pick_tile(N, (1024, 768, 512, 256, 128))
    tk = _pick_tile(K, (2048, 1024, 512, 256, 128, 64))
    nk = K // tk

    def body(a_ref, g_ref, o_ref):
        k = pl.program_id(2)
        _acc(o_ref, _bdot_tn(a_ref[...], g_ref[...]), k == 0)

    return pl.pallas_call(
        body, name=name, grid=(M // tm, N // tn, nk),
        in_specs=[pl.BlockSpec((tk, tm), lambda i, j, k: (k, i)),
                  pl.BlockSpec((tk, tn), lambda i, j, k: (k, j))],
        out_specs=pl.BlockSpec((tm, tn), lambda i, j, k: (i, j)),
        out_shape=jax.ShapeDtypeStruct((M, N), F32),
        compiler_params=_params(("parallel", "parallel", "arbitrary")),
    )(a, g)


def _ada_fwd(crows, ada_w, ada_b):
    D = crows.shape[1]
    n6 = ada_w.shape[1]
    tn = _pick_tile(n6, (1024, 512, 256, 128))

    def body(c_ref, w_ref, b_ref, s_ref, m_ref):
        s = _silu(c_ref[...])
        s_ref[...] = s
        m_ref[...] = _bdot(s, w_ref[...]) + b_ref[...]

    return pl.pallas_call(
        body, name="ada_fwd", grid=(n6 // tn,),
        in_specs=[_full_spec((SUBLANES, D)), pl.BlockSpec((D, tn), lambda i: (0, i)),
                  pl.BlockSpec((1, tn), lambda i: (0, i))],
        out_specs=[_full_spec((SUBLANES, D)), pl.BlockSpec((SUBLANES, tn), lambda i: (0, i))],
        out_shape=[jax.ShapeDtypeStruct((SUBLANES, D), F32), jax.ShapeDtypeStruct((SUBLANES, n6), F32)],
        compiler_params=_params(("arbitrary",)),
    )(crows, ada_w, ada_b)


def _ada_bwd(s_all, g_all, g_mine, c_ctx, ada_w, nb):
    D = s_all.shape[1]
    n6 = g_all.shape[1]
    ns = g_mine.shape[1]

    def body(s_ref, g_ref, gm_ref, c_ref, w_ref, dw_ref, db_ref, dc_ref):
        g = g_ref[...]
        dw_ref[...] = _bdot_tn(s_ref[...], gm_ref[...])
        db_ref[...] = jnp.sum(g, axis=0, keepdims=True)
        rows = lax.broadcasted_iota(jnp.int32, (g.shape[0], 1), 0)
        gc = jnp.sum(jnp.where(rows % SUBLANES == nb, g, 0.0), axis=0, keepdims=True)
        ds = _bdot_nt(gc, w_ref[...])
        c = c_ref[...]
        sg = _sigmoid(c)
        dc_ref[...] = ds * (sg + c * sg * (1.0 - sg))

    return pl.pallas_call(
        body, name="ada_bwd",
        out_shape=[jax.ShapeDtypeStruct((D, ns), F32), jax.ShapeDtypeStruct((1, n6), F32),
                   jax.ShapeDtypeStruct((1, D), F32)],
        compiler_params=_params(),
    )(s_all, g_all, g_mine, c_ctx, ada_w)


def _adamw(parts, w, m, v, name):
    P, R, C = parts.shape
    small = R * C * (P + 7) * 4 <= 4 * 1024 * 1024
    tr = R if small else _pick_tile(R, (256, 128, 64, 32, 16, 8))

    def body(p_ref, w_ref, m_ref, v_ref, g_ref, d_ref, nm_ref, nv_ref):
        g = p_ref[0].astype(F32)
        for i in range(1, P):
            g = g + p_ref[i].astype(F32)
        nm = ADAM_B1 * m_ref[...] + (1.0 - ADAM_B1) * g
        nv = ADAM_B2 * v_ref[...] + (1.0 - ADAM_B2) * (g * g)
        m_hat = nm / (1.0 - ADAM_B1 ** ADAM_STEP)
        v_hat = nv / (1.0 - ADAM_B2 ** ADAM_STEP)
        g_ref[...] = g
        d_ref[...] = -ADAM_LR * (m_hat / (jnp.sqrt(v_hat) + ADAM_EPS) + ADAM_WD * w_ref[...])
        nm_ref[...] = nm
        nv_ref[...] = nv

    blk = pl.BlockSpec((tr, C), lambda i: (i, 0))
    out = jax.ShapeDtypeStruct((R, C), F32)
    return pl.pallas_call(
        body, name=name, grid=(R // tr,),
        in_specs=[pl.BlockSpec((P, tr, C), lambda i: (0, i, 0)), blk, blk, blk],
        out_specs=[blk] * 4, out_shape=[out] * 4,
        compiler_params=_params(("parallel",)),
    )(parts, w, m, v)


def _local_step(cfg, x, c, ctx, tgt, fw):
    B, D, W, CW, JC, T, TX = cfg.B, cfg.D, cfg.W, cfg.CW, cfg.JC, cfg.T, cfg.TX
    e_w = _block_ones(W)
    e128 = _block_ones(LANES)
    e256 = jnp.concatenate([e128, e128], axis=0)
    row = lambda a: a.reshape(1, -1)

    ada_wb = fw["ada_w"].astype(BF16)
    w_in_p = _pad_cols(fw["w_in"], cfg).astype(BF16)
    mup = _pad_cols(fw["mu_prev"], cfg, True)
    mun = _pad_cols(fw["mu_next"], cfg, True)
    w0, a0 = row(fw["decay_w0"]), row(fw["iclr_a0"])
    w2p, a2p = _pair_weight(fw["decay_w2"], cfg), _pair_weight(fw["iclr_a2"], cfg)
    wm_fwd, wm_bwd = _conv_tables(fw["conv_w"], cfg.KC)
    gw2p = jnp.pad(fw["gate_w2"], ((0, cfg.GP - cfg.GR), (0, 0)))
    r_k = row(fw["r_k"])
    w_outb, w1b, w2b = fw["w_out"].astype(BF16), fw["mlp_w1"].astype(BF16), fw["mlp_w2"].astype(BF16)

    crows = jnp.concatenate([c, fw["c_ctx"], jnp.zeros((SUBLANES - B - 1, D), F32)], axis=0)
    s_rows, mods = _ada_fwd(crows, ada_wb, fw["ada_b"])
    mod_x = mods[:B].reshape(B, 6, D)
    mod_c = mods[B].reshape(6, D)
    modt = jnp.concatenate([jnp.broadcast_to(mod_c[None, None, 0:2], (B, JC, 2, D)),
                            jnp.broadcast_to(mod_x[:, None, 0:2], (B, cfg.JX, 2, D))], axis=1)
    mod2, mod345 = mod_x[:, 2:3], mod_x[:, 3:6]

    xcat = jnp.concatenate([ctx, x], axis=1)
    p, hb = _in_proj(xcat, modt, fw["mix_pre_g"], w_in_p, cfg)
    prep_w = (mup, mun, w0, w2p, a0, a2p, fw["k_k"], fw["k_a"])
    conv_w = (fw["conv_b"], fw["conv_ln_w"], fw["conv_ln_b"], e_w)
    r, v, kk, w_f, kd_f, b_f, w_b, kd_b, b_b, gd, conv, yconv = _mix_prep(p, *prep_w, wm_fwd, *conv_w, cfg)
    flat = lambda a: a.reshape(cfg.G, a.shape[2], LANES)
    heads = lambda a: a.reshape(B, cfg.HP, a.shape[1], LANES)
    ops_f = tuple(flat(a) for a in (r, w_f, kd_f, v, kk, b_f))
    ops_b = tuple(flat(a) for a in (r, w_b, kd_b, v, kk, b_b))
    y_f, y_b, hist, s_fin = _scan_fwd(ops_f, ops_b, e128, e256, cfg)
    hist = lax.dynamic_update_slice_in_dim(hist, s_fin[None], cfg.NCH * SCAN_CHUNK, axis=0)
    out_args = (heads(y_f), heads(y_b), kd_f, kd_b, r, v, gd, conv, x, mod2, r_k, gw2p, fw["lnx_w"], fw["lnx_b"],
                w_outb, fw["mix_post_g"], e_w)
    x1 = _mix_out(*out_args, cfg)

    dx1, loss_t, h2b, dpreb, actb, dffb, dmod345, dg3, dg4 = _mlp_fwd_bwd(
        x1, tgt, mod345, fw["mlp_pre_g"], fw["mlp_post_g"], w1b, w2b, cfg)
    (dy, dkb, dr_c, dv_c, dgd, dconv, catb, dmixb, dmod2, dg2, drk, dgw, dlw, dlb) = _mix_out_bwd(
        *out_args, dx1, cfg)
    sf, sb = _scan_bwd(ops_f, ops_b, flat(dy), hist, e128, e256, cfg)

    tokens = lambda a: a.reshape(-1, a.shape[-1])
    d_w_out = _grad_matmul(tokens(catb), tokens(dmixb), "grad_w_out")
    d_w1 = _grad_matmul(tokens(h2b), tokens(dpreb), "grad_mlp_w1")
    d_w2 = _grad_matmul(tokens(actb), tokens(dffb), "grad_mlp_w2")
    early = [d_w_out.astype(BF16).reshape(N_DEV, -1, D), _cols_to_blocks(d_w1.astype(BF16)),
             d_w2.astype(BF16).reshape(N_DEV, -1, D)]
    (dpz, dmup, dmun, dw0, dw2p, da0, da2p, dkk, dka, dcw, dcb, dclw, dclb, x_w_out, x_w1, x_w2) = _mix_prep_bwd(
        p, *prep_w, wm_bwd, *conv_w, yconv, [heads(a) for a in sf], [heads(a) for a in sb],
        (dr_c, dv_c, dkb, dgd, dconv), early, cfg)
    grad_x, dpb, dmodt, dg1 = _in_proj_bwd(dpz, xcat, modt, fw["mix_pre_g"], w_in_p, mup, mun, dx1, cfg)
    d_w_in = _grad_matmul(tokens(hb), tokens(dpb), "grad_w_in")
    exchanged = {"w_out": x_w_out, "mlp_w1": x_w1, "mlp_w2": x_w2}

    grads = {
        "mix_pre_g": dg1, "mix_post_g": dg2, "mlp_pre_g": dg3, "mlp_post_g": dg4,
        "w_in": _unpad_cols(d_w_in, cfg),
        "mu_prev": _unpad_cols(dmup, cfg, True), "mu_next": _unpad_cols(dmun, cfg, True),
        "decay_w0": dw0.reshape(2, W), "decay_w2": _unpair_weight(dw2p, cfg),
        "iclr_a0": da0.reshape(2, W), "iclr_a2": _unpair_weight(da2p, cfg),
        "k_k": dkk, "k_a": dka, "r_k": drk.reshape(fw["r_k"].shape),
        "gate_w2": dgw[:cfg.GR], "lnx_w": dlw, "lnx_b": dlb,
        "conv_w": dcw[:cfg.KC], "conv_b": dcb, "conv_ln_w": dclw, "conv_ln_b": dclb,
    }
    dmod_x = jnp.concatenate([jnp.sum(dmodt[:, JC:], axis=1), dmod2, dmod345], axis=1).reshape(B, 6 * D)
    dmod_c = jnp.concatenate([jnp.sum(dmodt[:, :JC], axis=(0, 1)), jnp.zeros((4, D), F32)], axis=0).reshape(1, 6 * D)
    g_rows = jnp.concatenate([dmod_x, dmod_c, jnp.zeros((SUBLANES - B - 1, 6 * D), F32)], axis=0)
    return loss_t, grad_x, grads, exchanged, s_rows, g_rows


def _my_index():
    return 4 * lax.axis_index("x") + 2 * lax.axis_index("y") + lax.axis_index("c")


def _gather_two_level(arrays, name):
    n = len(arrays)
    out_shape = [jax.ShapeDtypeStruct((N_DEV,) + a.shape, a.dtype) for a in arrays]

    def body(*refs):
        ins, outs = refs[:n], refs[n:2 * n]
        send_sems, recv_sems, local_sems = refs[2 * n:]
        x, y, c = lax.axis_index("x"), lax.axis_index("y"), lax.axis_index("c")
        index = lambda px, py, pc: 4 * px + 2 * py + pc
        sibling = (x, y, 1 - c)
        chips = [(1 - x, y), (x, 1 - y), (1 - x, 1 - y)]

        def copy(a, k, block, to, src=None):
            dst = outs[a].at[index(*block)]
            return pltpu.make_async_remote_copy(
                src_ref=dst if src is None else src, dst_ref=dst,
                send_sem=send_sems.at[k, a], recv_sem=recv_sems.at[k, a],
                device_id=to, device_id_type=pl.DeviceIdType.MESH)

        local = [pltpu.make_async_copy(ins[a], outs[a].at[index(x, y, c)], local_sems.at[a]) for a in range(n)]
        for cp in local:
            cp.start()
        first = []
        for a in range(n):
            first.append(copy(a, 0, (x, y, c), sibling, src=ins[a]))
            first += [copy(a, 1 + j, (x, y, c), (*chip, c), src=ins[a]) for j, chip in enumerate(chips)]
        for cp in first:
            cp.start()
        passed = []
        for j, chip in enumerate(chips):
            for a in range(n):
                copy(a, 1 + j, (*chip, c), (x, y, c)).wait_recv()
                fwd = copy(a, 4 + j, (*chip, c), sibling)
                fwd.start()
                passed.append(fwd)
        for a in range(n):
            copy(a, 0, sibling, (x, y, c)).wait_recv()
            for j, chip in enumerate(chips):
                copy(a, 4 + j, (*chip, 1 - c), (x, y, c)).wait_recv()
        for cp in first + passed:
            cp.wait_send()
        for cp in local:
            cp.wait()

    hbm = pl.BlockSpec(memory_space=pltpu.HBM)
    return pl.pallas_call(
        body, name=name, out_shape=out_shape,
        in_specs=[hbm] * n, out_specs=[hbm] * n,
        scratch_shapes=[pltpu.SemaphoreType.DMA((N_DEV - 1, n)), pltpu.SemaphoreType.DMA((N_DEV - 1, n)),
                        pltpu.SemaphoreType.DMA((n,))],
    )(*arrays)


def _exchange_copies(ins, outs, send_sems, recv_sems, local_sems, scatter):
    n = len(ins)
    x, y, c = lax.axis_index("x"), lax.axis_index("y"), lax.axis_index("c")
    me = 4 * x + 2 * y + c
    flip = lambda v, f: 1 - v if f else v

    def piece(a, dest):
        return ins[a].at[dest] if scatter[a] else ins[a]

    local = [pltpu.make_async_copy(piece(a, me), outs[a].at[me], local_sems.at[a]) for a in range(n)]
    sends, recvs = [], []
    for k in range(1, N_DEV):
        fx, fy, fc = (k >> 2) & 1, (k >> 1) & 1, k & 1
        peer = (flip(x, fx), flip(y, fy), flip(c, fc))
        peer_idx = 4 * peer[0] + 2 * peer[1] + peer[2]
        for a in range(n):
            sends.append(pltpu.make_async_remote_copy(
                src_ref=piece(a, peer_idx), dst_ref=outs[a].at[me],
                send_sem=send_sems.at[k - 1, a], recv_sem=recv_sems.at[k - 1, a],
                device_id=peer, device_id_type=pl.DeviceIdType.MESH))
            recvs.append(pltpu.make_async_remote_copy(
                src_ref=piece(a, peer_idx), dst_ref=outs[a].at[peer_idx],
                send_sem=send_sems.at[k - 1, a], recv_sem=recv_sems.at[k - 1, a],
                device_id=peer, device_id_type=pl.DeviceIdType.MESH))
    return local, sends, recvs


def _exchange_start(copies):
    local, sends, _ = copies
    for cp in local + sends:
        cp.start()


def _exchange_wait(copies):
    local, sends, recvs = copies
    for cp in recvs:
        cp.wait_recv()
    for cp in sends:
        cp.wait_send()
    for cp in local:
        cp.wait()


def _exchange_shapes(arrays, scatter):
    return [jax.ShapeDtypeStruct(a.shape if s else (N_DEV,) + a.shape, a.dtype) for a, s in zip(arrays, scatter)]


def _exchange_sems(n):
    return [pltpu.SemaphoreType.DMA((N_DEV - 1, n)), pltpu.SemaphoreType.DMA((N_DEV - 1, n)),
            pltpu.SemaphoreType.DMA((n,))]


def _exchange(arrays, scatter, name):
    n = len(arrays)

    def body(*refs):
        copies = _exchange_copies(refs[:n], refs[n:2 * n], *refs[2 * n:], scatter)
        _exchange_start(copies)
        _exchange_wait(copies)

    hbm = pl.BlockSpec(memory_space=pltpu.HBM)
    return pl.pallas_call(
        body, name=name, out_shape=_exchange_shapes(arrays, scatter),
        in_specs=[hbm] * n, out_specs=[hbm] * n, scratch_shapes=_exchange_sems(n),
    )(*arrays)


def _pack(parts):
    flat = jnp.concatenate([p.reshape(-1) for p in parts])
    total = _round_up(flat.shape[0], SUBLANES * LANES)
    return jnp.pad(flat, (0, total - flat.shape[0])).reshape(-1, LANES)


def _unpack(buf, shapes):
    flat = buf.reshape(-1)
    out, pos = [], 0
    for s in shapes:
        n = int(np.prod(s))
        out.append(flat[pos:pos + n].reshape(s))
        pos += n
    return out


_SHARDED_SMALL = ("decay_w0", "decay_w2", "iclr_a0", "iclr_a2", "gate_w2", "conv_w")
_REPLICATED = ("mix_pre_g", "mix_post_g", "mlp_pre_g", "mlp_post_g", "mu_prev", "mu_next", "k_k", "k_a", "r_k",
               "lnx_w", "lnx_b", "conv_b", "conv_ln_w", "conv_ln_b")
_ADA_SMALL = ("c_ctx", "ada_b")
_WEIGHTS = ("c_ctx", "ada_w", "ada_b", "mix_pre_g", "mix_post_g", "mlp_pre_g", "mlp_post_g", "w_in", "mu_prev",
            "mu_next", "decay_w0", "decay_w2", "iclr_a0", "iclr_a2", "k_k", "k_a", "r_k", "gate_w2", "lnx_w", "lnx_b",
            "conv_w", "conv_b", "conv_ln_w", "conv_ln_b", "w_out", "mlp_w1", "mlp_w2")
_INPUTS = ("x", "c", "ctx") + _WEIGHTS + ("loss_target",) + tuple("m_" + n for n in _WEIGHTS) + tuple(
    "v_" + n for n in _WEIGHTS)


def _cols_to_blocks(a):
    a = a.reshape(a.shape[:-1] + (N_DEV, a.shape[-1] // N_DEV))
    return jnp.moveaxis(a, -2, 0)


def _blocks_to_cols(a):
    a = jnp.moveaxis(a, 0, -2)
    return a.reshape(a.shape[:-2] + (a.shape[-2] * a.shape[-1],))


def kernel(x, c, ctx, c_ctx, ada_w, ada_b, mix_pre_g, mix_post_g, mlp_pre_g, mlp_post_g, w_in, mu_prev, mu_next, decay_w0, decay_w2, iclr_a0, iclr_a2, k_k, k_a, r_k, gate_w2, lnx_w, lnx_b, conv_w, conv_b, conv_ln_w, conv_ln_b, w_out, mlp_w1, mlp_w2, loss_target, m_c_ctx, m_ada_w, m_ada_b, m_mix_pre_g, m_mix_post_g, m_mlp_pre_g, m_mlp_post_g, m_w_in, m_mu_prev, m_mu_next, m_decay_w0, m_decay_w2, m_iclr_a0, m_iclr_a2, m_k_k, m_k_a, m_r_k, m_gate_w2, m_lnx_w, m_lnx_b, m_conv_w, m_conv_b, m_conv_ln_w, m_conv_ln_b, m_w_out, m_mlp_w1, m_mlp_w2, v_c_ctx, v_ada_w, v_ada_b, v_mix_pre_g, v_mix_post_g, v_mlp_pre_g, v_mlp_post_g, v_w_in, v_mu_prev, v_mu_next, v_decay_w0, v_decay_w2, v_iclr_a0, v_iclr_a2, v_k_k, v_k_a, v_r_k, v_gate_w2, v_lnx_w, v_lnx_b, v_conv_w, v_conv_b, v_conv_ln_w, v_conv_ln_b, v_w_out, v_mlp_w1, v_mlp_w2):
    given = dict(zip(_INPUTS, (x, c, ctx, c_ctx, ada_w, ada_b, mix_pre_g, mix_post_g, mlp_pre_g, mlp_post_g, w_in, mu_prev, mu_next, decay_w0, decay_w2, iclr_a0, iclr_a2, k_k, k_a, r_k, gate_w2, lnx_w, lnx_b, conv_w, conv_b, conv_ln_w, conv_ln_b, w_out, mlp_w1, mlp_w2, loss_target, m_c_ctx, m_ada_w, m_ada_b, m_mix_pre_g, m_mix_post_g, m_mlp_pre_g, m_mlp_post_g, m_w_in, m_mu_prev, m_mu_next, m_decay_w0, m_decay_w2, m_iclr_a0, m_iclr_a2, m_k_k, m_k_a, m_r_k, m_gate_w2, m_lnx_w, m_lnx_b, m_conv_w, m_conv_b, m_conv_ln_w, m_conv_ln_b, m_w_out, m_mlp_w1, m_mlp_w2, v_c_ctx, v_ada_w, v_ada_b, v_mix_pre_g, v_mix_post_g, v_mlp_pre_g, v_mlp_post_g, v_w_in, v_mu_prev, v_mu_next, v_decay_w0, v_decay_w2, v_iclr_a0, v_iclr_a2, v_k_k, v_k_a, v_r_k, v_gate_w2, v_lnx_w, v_lnx_b, v_conv_w, v_conv_b, v_conv_ln_w, v_conv_ln_b, v_w_out, v_mlp_w1, v_mlp_w2)))
    loc = {}
    for pre in ("", "m_", "v_"):
        for n in _WEIGHTS:
            a = given[pre + n]
            a = a.reshape(1, -1) if n == "c_ctx" else a[0]
            loc[pre + n] = a.reshape(1, -1) if a.ndim == 1 else a
    B, TX, D = x.shape
    W, CW = loc["k_k"].shape[1], loc["conv_b"].shape[1]
    cfg = _Cfg(B, TX, ctx.shape[1], D, W, CW, loc["decay_w2"].shape[1], loc["gate_w2"].shape[0],
               loc["conv_w"].shape[0], loc["mlp_w1"].shape[1] * N_DEV)
    me = _my_index()

    small_shapes = [loc[n].shape for n in _SHARDED_SMALL]
    got = _gather_two_level(
        [loc["ada_w"].astype(BF16), loc["w_in"].astype(BF16), loc["w_out"].astype(BF16),
         loc["mlp_w1"].astype(BF16), loc["mlp_w2"].astype(BF16), _pack([loc[n] for n in _SHARDED_SMALL])],
        "gather_weights")
    fw = {n: loc[n] for n in _REPLICATED + _ADA_SMALL}
    fw["ada_w"] = _blocks_to_cols(got[0])
    fw["w_in"] = _blocks_to_cols(got[1])
    fw["w_out"] = got[2].reshape(-1, D)
    fw["mlp_w1"] = _blocks_to_cols(got[3])
    fw["mlp_w2"] = got[4].reshape(-1, D)
    per_dev = [_unpack(got[5][i], small_shapes) for i in range(N_DEV)]
    for j, n in enumerate(_SHARDED_SMALL):
        fw[n] = jnp.concatenate([per_dev[i][j] for i in range(N_DEV)], axis=-1)

    loss_t, grad_x, grads, exchanged, s_rows, g_rows = _local_step(cfg, x, c, ctx, loss_target, fw)
    loss = lax.psum(jnp.sum(loss_t[:, :, 0, 0]), ("x", "y", "c"))

    small_blocks = jnp.stack([_pack([_cols_to_blocks(grads[n])[i] for n in _SHARDED_SMALL]) for i in range(N_DEV)])
    sent = _exchange(
        [_cols_to_blocks(grads["w_in"].astype(BF16)), small_blocks,
         _pack([grads[n] for n in _REPLICATED]), s_rows, g_rows],
        [True] * 2 + [False] * 3, "exchange_grads")
    s_all = sent[3].reshape(N_DEV * SUBLANES, D)
    g_all = sent[4].reshape(N_DEV * SUBLANES, 6 * D)
    ns = 6 * D // N_DEV
    g_mine = lax.dynamic_slice_in_dim(g_all, me * ns, ns, axis=1)
    d_ada_w, d_ada_b, d_c_ctx = _ada_bwd(s_all, g_all, g_mine, loc["c_ctx"], fw["ada_w"], B)

    res = {}

    def update(name, parts):
        res[name] = _adamw(parts, loc[name], loc["m_" + name], loc["v_" + name], "adamw_" + name)

    update("w_in", sent[0])
    for n in ("w_out", "mlp_w1", "mlp_w2"):
        update(n, exchanged[n])
    update("ada_w", d_ada_w[None])

    def update_packed(names, parts, tag):
        shapes = [loc[n].shape for n in names]
        packed = _adamw(parts, *[_pack([loc[pre + n] for n in names]) for pre in ("", "m_", "v_")], "adamw_" + tag)
        unpacked = [_unpack(p, shapes) for p in packed]
        for j, n in enumerate(names):
            res[n] = tuple(u[j] for u in unpacked)

    update_packed(_SHARDED_SMALL, sent[1], "sharded_small")
    update_packed(_REPLICATED, sent[2], "replicated")
    update_packed(_ADA_SMALL, _pack([d_c_ctx, d_ada_b])[None], "ada_small")

    outs = [loss, grad_x]
    for k in range(4):
        for n in _WEIGHTS:
            outs.append(res[n][k].reshape(given[n].shape))
    return tuple(outs)
```

```python
import functools

import numpy as np
import jax
import jax.numpy as jnp
from jax import lax
from jax.experimental import pallas as pl
from jax.experimental.pallas import tpu as pltpu

F32 = jnp.float32
BF16 = jnp.bfloat16

EPS_RMS = 1e-6
EPS_LN = 1e-5
EPS_GN = 64e-5
LINE = 64
HEAD = 64
LANES = 128
SUBLANES = 8
SCAN_CHUNK = 16
N_DEV = 8
VMEM_LIMIT = 56 * 1024 * 1024

ADAM_LR = 0.001
ADAM_B1 = 0.9
ADAM_B2 = 0.999
ADAM_EPS = 1e-08
ADAM_WD = 0.01
ADAM_STEP = 10


def _round_up(n, m):
    return (n + m - 1) // m * m


def _params(semantics=None, vmem=VMEM_LIMIT):
    return pltpu.CompilerParams(dimension_semantics=semantics, vmem_limit_bytes=vmem)


def _bdot(a, b):
    return jnp.dot(a.astype(BF16), b.astype(BF16), preferred_element_type=F32)


def _bdot_nt(a, b):
    return lax.dot_general(a.astype(BF16), b.astype(BF16), (((1,), (1,)), ((), ())),
                           preferred_element_type=F32)


def _bdot_tn(a, b):
    return lax.dot_general(a.astype(BF16), b.astype(BF16), (((0,), (0,)), ((), ())),
                           preferred_element_type=F32)


@jax.custom_vjp
def _mm(a, b):
    return _bdot(a, b)


def _mm_fwd(a, b):
    return _bdot(a, b), (a, b)


def _mm_bwd(res, g):
    a, b = res
    return _bdot_nt(g, b), _bdot_tn(a, g)


_mm.defvjp(_mm_fwd, _mm_bwd)


def _seg_sum_raw(x, e):
    hi = x.astype(BF16)
    lo = (x - hi.astype(F32)).astype(BF16)
    return (jnp.dot(hi, e, preferred_element_type=F32)
            + jnp.dot(lo, e, preferred_element_type=F32))


@jax.custom_vjp
def _seg_sum(x, e):
    return _seg_sum_raw(x, e)


def _seg_sum_fwd(x, e):
    return _seg_sum_raw(x, e), e


def _seg_sum_bwd(e, g):
    return _seg_sum_raw(g, e), None


_seg_sum.defvjp(_seg_sum_fwd, _seg_sum_bwd)


def _block_ones(n, seg=HEAD):
    i = np.arange(n) // seg
    return jnp.asarray((i[:, None] == i[None, :]).astype(np.float32), dtype=BF16)


def _rms(xv, g):
    ms = jnp.mean(xv * xv, axis=-1, keepdims=True)
    return xv * lax.rsqrt(ms + EPS_RMS) * g


def _rms_mod(xv, g, shift, scale):
    return _rms(xv, g) * (1.0 + scale) + shift


def _sigmoid(z):
    return 1.0 / (1.0 + jnp.exp(-z))


def _silu(z):
    return z * _sigmoid(z)


def _softplus(z):
    return jnp.maximum(z, 0.0) + jnp.log(1.0 + jnp.exp(-jnp.abs(z)))


class _Cfg:
    def __init__(self, B, TX, TC, D, W, CW, R, GR, KC, F):
        self.B, self.TX, self.TC, self.D = B, TX, TC, D
        self.W, self.CW, self.R, self.GR, self.KC, self.F = W, CW, R, GR, KC, F
        self.T = TX + TC
        self.TT = min(256, TC)
        assert TC % self.TT == 0 and TX % self.TT == 0 and self.TT % LINE == 0
        self.JC = TC // self.TT
        self.JX = TX // self.TT
        self.J = self.JC + self.JX
        self.HP = W // LANES
        self.G = B * self.HP
        self.PW = _round_up(2 * R, LANES)
        self.GP = _round_up(GR, LANES)
        self.SP = 3 * W + 2 * self.PW + self.GP
        self.CP = self.SP + 2 * CW
        self.KP = _round_up(KC, SUBLANES)
        assert self.T % SCAN_CHUNK == 0 and TC % SCAN_CHUNK == 0
        self.NCH = self.T // SCAN_CHUNK
        self.NCC = TC // SCAN_CHUNK
        W_, R_ = W, R
        segs = [(0, 3 * W_, 0),
                (3 * W_, 2 * R_, 3 * W_),
                (3 * W_ + 2 * R_, 2 * R_, 3 * W_ + self.PW),
                (3 * W_ + 4 * R_, GR, 3 * W_ + 2 * self.PW),
                (3 * W_ + 4 * R_ + GR, 2 * CW, self.SP)]
        self.col_segs = segs
        self.shift_cols = 3 * W_ + 4 * R_ + GR
        self.in_cols = self.shift_cols + 2 * CW


def _pad_cols(a, cfg, upto_shift=False):
    width = cfg.SP if upto_shift else cfg.CP
    pieces, pos = [], 0
    for src, n, dst in cfg.col_segs:
        if upto_shift and dst >= cfg.SP:
            break
        if dst > pos:
            pieces.append(jnp.zeros(a.shape[:-1] + (dst - pos,), a.dtype))
        pieces.append(a[..., src:src + n])
        pos = dst + n
    if width > pos:
        pieces.append(jnp.zeros(a.shape[:-1] + (width - pos,), a.dtype))
    return jnp.concatenate(pieces, axis=-1)


def _unpad_cols(a, cfg, upto_shift=False):
    pieces = []
    for src, n, dst in cfg.col_segs:
        if upto_shift and dst >= cfg.SP:
            break
        pieces.append(a[..., dst:dst + n])
    return jnp.concatenate(pieces, axis=-1)


def _pair_weight(w2, cfg):
    R, W = cfg.R, cfg.W
    out = jnp.zeros((cfg.PW, 2 * W), w2.dtype)
    out = out.at[0:R, 0:W].set(w2[0])
    out = out.at[R:2 * R, W:2 * W].set(w2[1])
    return out


def _unpair_weight(g, cfg):
    R, W = cfg.R, cfg.W
    return jnp.stack([g[0:R, 0:W], g[R:2 * R, W:2 * W]])


def _row_ids(n):
    return lax.broadcasted_iota(jnp.int32, (n, 1), 0)


def _shift_rows(z, prev_row, next_row):
    n = z.shape[0]
    rows = _row_ids(n)
    zp = jnp.where(rows == 0, prev_row, pltpu.roll(z, 1, 0))
    zn = jnp.where(rows == n - 1, next_row, pltpu.roll(z, n - 1, 0))
    return zp, zn


def _line_shift(u, d):
    if d == 0:
        return u
    n = u.shape[0]
    lt = _row_ids(n) % LINE
    ok = jnp.logical_and(lt + d >= 0, lt + d < LINE)
    return jnp.where(ok, pltpu.roll(u, (-d) % n, 0), 0.0)


def _conv_tables(cw, kc):
    pad = kc // 2
    t = np.arange(LINE)[None, :]
    d = (np.arange(kc) - pad)[:, None]
    fwd = ((t + d >= 0) & (t + d < LINE)).astype(np.float32)
    bwd = ((t - d >= 0) & (t - d < LINE)).astype(np.float32)
    w = cw[:kc, None, :]
    return jnp.asarray(fwd)[:, :, None] * w, jnp.asarray(bwd)[:, :, None] * w


def _conv_lines(src_ref, wm_ref, dst_ref, kc, transpose):
    pad = kc // 2
    n, width = src_ref.shape
    for l in range(n // LINE):
        for b in range(width // LANES):
            rs, cs = slice(l * LINE, (l + 1) * LINE), slice(b * LANES, (b + 1) * LANES)
            tile = src_ref[rs, cs]
            acc = jnp.zeros_like(tile)
            for i in range(kc):
                d = (pad - i) if transpose else (i - pad)
                acc = acc + pltpu.roll(tile, (-d) % LINE, 0) * wm_ref[i, :, cs]
            dst_ref[rs, cs] = acc


def _rwkv_prep(rw, w0, w2p, a0, a2p, k_k, k_a, e, cfg):
    W, PW = cfg.W, cfg.PW
    r = rw[:, 0:W]
    k = rw[:, W:2 * W]
    v = rw[:, 2 * W:3 * W]
    wdp = rw[:, 3 * W:3 * W + PW]
    adp = rw[:, 3 * W + PW:3 * W + 2 * PW]
    wl = w0 + _mm(jnp.tanh(wdp), w2p)
    w_log = -_softplus(-wl) - 0.5
    decay = jnp.exp(-jnp.exp(w_log))
    iclr = _sigmoid(a0 + _mm(adp, a2p))
    kkr = k * k_k
    nrm = jnp.sqrt(_seg_sum(kkr * kkr, e))
    kk = kkr / jnp.maximum(nrm, 1e-12)
    outs = [r, v, kk]
    for d in range(2):
        ic = iclr[:, d * W:(d + 1) * W]
        outs += [decay[:, d * W:(d + 1) * W], k * (1.0 + (ic - 1.0) * k_a), kk * ic]
    return tuple(outs)


def _glu(cv, cfg):
    return cv[:, :cfg.CW] * _sigmoid(cv[:, cfg.CW:])


def _conv_post(y, cb, lw, lb):
    yf = y + cb
    mu = jnp.mean(yf, axis=-1, keepdims=True)
    var = jnp.mean(jnp.square(yf - mu), axis=-1, keepdims=True)
    return _silu((yf - mu) * lax.rsqrt(var + EPS_LN) * lw + lb)


def _readout(y, kbar, r, v, gd, r_k, gw2, lnx_w, lnx_b, e):
    inv = 1.0 / HEAD
    mu = _seg_sum(y, e) * inv
    yc = y - mu
    var = _seg_sum(yc * yc, e) * inv
    yn = yc * lax.rsqrt(var + EPS_GN) * lnx_w + lnx_b
    bonus = _seg_sum(r * kbar * r_k, e) * v
    g = _mm(_sigmoid(gd), gw2)
    return (yn + bonus) * g


def _post_res(xv, mix, gate, g):
    return xv + gate * _rms(mix, g)


def _head_spec(cfg, tmap):
    return pl.BlockSpec((1, cfg.HP, cfg.TT, LANES), lambda b, j: (b, 0, tmap(j), 0))


def _full_spec(shape):
    n = len(shape)
    return pl.BlockSpec(shape, lambda *_: (0,) * n)


def _to_heads(ref, val, cfg):
    for hp in range(cfg.HP):
        ref[0, hp] = val[:, hp * LANES:(hp + 1) * LANES]


def _from_heads(ref, cfg):
    return jnp.concatenate([ref[0, hp] for hp in range(cfg.HP)], axis=-1)


def _in_proj(xcat, modt, g1, w_in_p, cfg):
    B, T, D, TT, CP = cfg.B, cfg.T, cfg.D, cfg.TT, cfg.CP

    def body(x_ref, mod_ref, g_ref, w_ref, p_ref, h_ref):
        h = _rms_mod(x_ref[0], g_ref[...], mod_ref[0, 0, 0:1, :], mod_ref[0, 0, 1:2, :])
        hb = h.astype(BF16)
        h_ref[0] = hb
        p_ref[0] = jnp.dot(hb, w_ref[...], preferred_element_type=F32)

    return pl.pallas_call(
        body, name="in_proj", grid=(B, cfg.J),
        in_specs=[pl.BlockSpec((1, TT, D), lambda b, j: (b, j, 0)),
                  pl.BlockSpec((1, 1, 2, D), lambda b, j: (b, j, 0, 0)),
                  _full_spec((1, D)), _full_spec((D, CP))],
        out_specs=[pl.BlockSpec((1, TT, CP), lambda b, j: (b, j, 0)),
                   pl.BlockSpec((1, TT, D), lambda b, j: (b, j, 0))],
        out_shape=[jax.ShapeDtypeStruct((B, T, CP), F32), jax.ShapeDtypeStruct((B, T, D), BF16)],
        compiler_params=_params(("parallel", "parallel")),
    )(xcat, modt, g1, w_in_p)


def _halo_specs(cfg, width):
    per = cfg.TT // SUBLANES
    last = cfg.T // SUBLANES - 1
    prev = pl.BlockSpec((1, SUBLANES, width), lambda b, j: (b, jnp.maximum(j * per - 1, 0), 0))
    nxt = pl.BlockSpec((1, SUBLANES, width), lambda b, j: (b, jnp.minimum((j + 1) * per, last), 0))
    return prev, nxt


def _halo_flags(j, cfg):
    has_prev = jnp.logical_and(j != 0, j != cfg.JC).astype(F32)
    has_next = jnp.logical_and(j != cfg.JC - 1, j != cfg.J - 1).astype(F32)
    return has_prev, has_next


def _shifted(p_ref, prev_ref, next_ref, mup, mun, j, cfg):
    SP = cfg.SP
    has_prev, has_next = _halo_flags(j, cfg)
    z = p_ref[0][:, :SP]
    zp, zn = _shift_rows(z, prev_ref[0, SUBLANES - 1:SUBLANES, :] * has_prev, next_ref[0, 0:1, :] * has_next)
    return z, zp, zn, z + mup * (zp - z) + mun * (zn - z)


def _mix_prep(p, mup, mun, w0, w2p, a0, a2p, k_k, k_a, wm, cb, clw, clb, e_w, late, cfg):
    B, T, TT, SP, CP, W, CW, HP, JC = cfg.B, cfg.T, cfg.TT, cfg.SP, cfg.CP, cfg.W, cfg.CW, cfg.HP, cfg.JC
    nl = len(late)
    steps = B * cfg.J

    def body(p_ref, prev_ref, next_ref, mup_ref, mun_ref, w0_ref, w2_ref, a0_ref, a2_ref, kk_ref, ka_ref,
             wm_ref, cb_ref, clw_ref, clb_ref, e_ref, *rest):
        late_in, rest = rest[:nl], rest[nl:]
        outs, late_out, u_ref, sems = rest[:12], rest[12:12 + nl], rest[12 + nl], rest[13 + nl:]
        j = pl.program_id(1)
        step = pl.program_id(0) * cfg.J + j
        for phase, at in enumerate((0, steps // 2)):
            @pl.when(step == at)
            def _(phase=phase):
                _gather_phase(phase, late_in, late_out, *sems)
        _, _, _, rw = _shifted(p_ref, prev_ref, next_ref, mup_ref[...], mun_ref[...], j, cfg)
        vals = _rwkv_prep(rw, w0_ref[...], w2_ref[...], a0_ref[...], a2_ref[...], kk_ref[...], ka_ref[...],
                          e_ref[...], cfg)
        for ref, val in zip(outs[:9], vals):
            _to_heads(ref, val, cfg)
        outs[9][0] = rw[:, 3 * W + 2 * cfg.PW:SP]

        @pl.when(j >= JC)
        def _():
            u_ref[...] = _glu(p_ref[0, :, SP:], cfg)
            _conv_lines(u_ref, wm_ref, outs[11].at[0], cfg.KC, False)
            outs[10][0] = _conv_post(outs[11][0], cb_ref[...], clw_ref[...], clb_ref[...])

        @pl.when(step == steps - 1)
        def _():
            _gather_phase(2, late_in, late_out, *sems)

    prev, nxt = _halo_specs(cfg, SP)
    head = jax.ShapeDtypeStruct((B, HP, T, LANES), F32)
    hbm = pl.BlockSpec(memory_space=pltpu.HBM)
    tile = lambda n: pl.BlockSpec((1, TT, n), lambda b, j: (b, j, 0))
    return pl.pallas_call(
        body, name="mix_prep", grid=(B, cfg.J),
        in_specs=[tile(CP), prev, nxt,
                  _full_spec((1, SP)), _full_spec((1, SP)),
                  _full_spec((1, 2 * W)), _full_spec((cfg.PW, 2 * W)),
                  _full_spec((1, 2 * W)), _full_spec((cfg.PW, 2 * W)),
                  _full_spec((1, W)), _full_spec((1, W)),
                  pl.BlockSpec((cfg.KC, LINE, CW), lambda b, j: (0, 0, 0), pipeline_mode=pl.Buffered(1)),
                  _full_spec((1, CW)), _full_spec((1, CW)), _full_spec((1, CW)),
                  _full_spec((W, W))] + [hbm] * nl,
        out_specs=[_head_spec(cfg, lambda j: j)] * 9 + [tile(cfg.GP), tile(CW), tile(CW)] + [hbm] * nl,
        out_shape=[head] * 9 + [jax.ShapeDtypeStruct((B, T, cfg.GP), F32),
                                jax.ShapeDtypeStruct((B, T, CW), F32), jax.ShapeDtypeStruct((B, T, CW), F32)]
                  + _exchange_shapes(late, [False] * nl),
        scratch_shapes=[pltpu.VMEM((TT, CW), F32)] + _exchange_sems(nl),
        compiler_params=_params(("arbitrary", "arbitrary")),
    )(p, p, p, mup, mun, w0, w2p, a0, a2p, k_k, k_a, wm, cb, clw, clb, e_w, *late)


def _chunk_pos(c, reverse, cfg):
    if not reverse:
        return c
    return jnp.where(c < cfg.NCC, cfg.NCC - 1 - c, cfg.NCH - 1 + cfg.NCC - c)


def _diag_mask():
    r = lax.broadcasted_iota(jnp.int32, (HEAD, LANES), 0)
    l = lax.broadcasted_iota(jnp.int32, (HEAD, LANES), 1)
    return (r == l % HEAD).astype(F32)


def _col_lhs(row, diag_b):
    hi = row.astype(BF16)
    lo = (row - hi.astype(F32)).astype(BF16)
    return diag_b * hi, diag_b * lo


def _col_dot(row_list, diag_b, e2):
    n, g = len(row_list), row_list[0].shape[0]
    lhs = jnp.concatenate([jnp.concatenate(_col_lhs(r, diag_b), axis=-1) for r in row_list], axis=0)
    out = jnp.dot(lhs.reshape(n * g * HEAD, 2 * LANES), e2, preferred_element_type=F32)
    return out.reshape(n, g, HEAD, LANES)


def _col_form(row):
    n = row.shape[0]
    t = jnp.swapaxes(jnp.broadcast_to(row, (n, LANES, LANES)), 1, 2)
    lane = lax.broadcasted_iota(jnp.int32, (HEAD, LANES), 1)
    return jnp.where(lane < HEAD, t[:, :HEAD, :], t[:, HEAD:, :])


def _col_both(row, diag_b, e2):
    half = row.shape[0] // 2
    return jnp.concatenate([_col_form(row[:half]), _col_dot([row[half:]], diag_b, e2)[0]], axis=0)


def _both_rows(ins, idx, i):
    return jnp.concatenate([ins[d][idx][:, pl.ds(_tok(i, d == 1), 1), :] for d in range(2)], axis=0)


def _seg_dot(blocks, e):
    n, g = len(blocks), blocks[0].shape[0]
    lhs = jnp.concatenate(blocks, axis=0).reshape(n * g * HEAD, LANES)
    return jnp.dot(lhs, e, preferred_element_type=F32).reshape(n, g, HEAD, LANES)


def _tok(i, reverse):
    return (SCAN_CHUNK - 1 - i) if reverse else i


def _scan_fwd(ops_f, ops_b, e128, e256, cfg):
    G, T, NCH = cfg.G, cfg.T, cfg.NCH
    CH = SCAN_CHUNK
    G2 = 2 * G

    def body(*refs):
        ins = (refs[0:6], refs[6:12])
        e_ref, e2_ref = refs[12], refs[13]
        ys, hist_ref, fin_ref = (refs[14], refs[15]), refs[16], refs[17]
        s_ref, mm_ref = refs[18], refs[19]
        c = pl.program_id(0)

        @pl.when(c == 0)
        def _():
            s_ref[...] = jnp.zeros_like(s_ref)

        diag = _diag_mask()
        diag_b = diag.astype(BF16)
        e, e2 = e_ref[...], e2_ref[...]
        rows = functools.partial(_both_rows, ins)

        mm_ref[0] = _seg_dot([s_ref[...].astype(BF16) * (-rows(4, 0)).astype(BF16)], e)[0]
        mm_ref[1] = _col_both(rows(3, 0), diag_b, e2)

        def step(i, carry):
            nxt = jnp.minimum(i + 1, CH - 1)
            res = []
            for d in range(2):
                sl = slice(d * G, (d + 1) * G)
                row = lambda idx, ii: ins[d][idx][:, pl.ds(_tok(ii, d == 1), 1), :]
                s_old = s_ref[sl]
                hist_ref[i, sl] = s_old
                S = s_old * row(1, i) + mm_ref[0, sl] * row(5, i) + mm_ref[1, sl] * row(2, i)
                s_ref[sl] = S
                sb = S.astype(BF16)
                res.append(_seg_dot([sb * (-row(4, nxt)).astype(BF16), sb * row(0, i).astype(BF16)], e))
            for d in range(2):
                sl = slice(d * G, (d + 1) * G)
                v_next = ins[d][3][:, pl.ds(_tok(nxt, d == 1), 1), :]
                mm_ref[0, sl] = res[d][0]
                mm_ref[1, sl] = _col_form(v_next) if d == 0 else _col_dot([v_next], diag_b, e2)[0]
                ys[d][:, pl.ds(_tok(i, d == 1), 1), :] = jnp.sum(diag * res[d][1], axis=1, keepdims=True)
            return carry

        lax.fori_loop(0, CH, step, 0)
        fin_ref[...] = s_ref[...]

    toks = [pl.BlockSpec((G, CH, LANES), lambda c, rev=rev: (0, _chunk_pos(c, rev, cfg), 0)) for rev in (False, True)]
    y_shape = jax.ShapeDtypeStruct((G, T, LANES), F32)
    return pl.pallas_call(
        body, name="scan_fwd", grid=(NCH,),
        in_specs=[toks[0]] * 6 + [toks[1]] * 6 + [_full_spec((LANES, LANES)), _full_spec((2 * LANES, LANES))],
        out_specs=[toks[0], toks[1], pl.BlockSpec((CH, G2, HEAD, LANES), lambda c: (c, 0, 0, 0)),
                   _full_spec((G2, HEAD, LANES))],
        out_shape=[y_shape, y_shape, jax.ShapeDtypeStruct(((NCH + 1) * CH, G2, HEAD, LANES), F32),
                   jax.ShapeDtypeStruct((G2, HEAD, LANES), F32)],
        scratch_shapes=[pltpu.VMEM((G2, HEAD, LANES), F32), pltpu.VMEM((2, G2, HEAD, LANES), F32)],
        compiler_params=_params(("arbitrary",)),
    )(*ops_f, *ops_b, e128, e256)


def _mix_out(yf, yb, kdf, kdb, r, v, gd, conv, x, mod2, r_k, gw2p, lnx_w, lnx_b, w_out, g2, e_w, cfg):
    B, TX, D, TT, W, CW, JC = cfg.B, cfg.TX, cfg.D, cfg.TT, cfg.W, cfg.CW, cfg.JC

    def body(yf_ref, yb_ref, kdf_ref, kdb_ref, r_ref, v_ref, gd_ref, cv_ref, x_ref, mod_ref,
             rk_ref, gw_ref, lw_ref, lb_ref, wo_ref, g_ref, e_ref, x1_ref):
        y = _from_heads(yf_ref, cfg) + _from_heads(yb_ref, cfg)
        kbar = 0.5 * (_from_heads(kdf_ref, cfg) + _from_heads(kdb_ref, cfg))
        ro = _readout(y, kbar, _from_heads(r_ref, cfg), _from_heads(v_ref, cfg), gd_ref[0], rk_ref[...],
                      gw_ref[...], lw_ref[...], lb_ref[...], e_ref[...])
        cat = jnp.concatenate([ro, cv_ref[0]], axis=-1)
        mix = _bdot(cat, wo_ref[...])
        x1_ref[0] = _post_res(x_ref[0], mix, mod_ref[0], g_ref[...])

    hs = _head_spec(cfg, lambda j: j + JC)
    lat = lambda n: pl.BlockSpec((1, TT, n), lambda b, j: (b, j + JC, 0))
    return pl.pallas_call(
        body, name="mix_out", grid=(B, cfg.JX),
        in_specs=[hs] * 6 + [lat(cfg.GP), lat(CW),
                             pl.BlockSpec((1, TT, D), lambda b, j: (b, j, 0)),
                             pl.BlockSpec((1, 1, D), lambda b, j: (b, 0, 0)),
                             _full_spec((1, W)), _full_spec((cfg.GP, W)), _full_spec((1, W)), _full_spec((1, W)),
                             _full_spec((W + CW, D)), _full_spec((1, D)), _full_spec((W, W))],
        out_specs=pl.BlockSpec((1, TT, D), lambda b, j: (b, j, 0)),
        out_shape=jax.ShapeDtypeStruct((B, TX, D), F32),
        compiler_params=_params(("parallel", "parallel")),
    )(yf, yb, kdf, kdb, r, v, gd, conv, x, mod2, r_k, gw2p, lnx_w, lnx_b, w_out, g2, e_w)


def _acc(ref, val, first):
    @pl.when(first)
    def _():
        ref[...] = val

    @pl.when(jnp.logical_not(first))
    def _():
        ref[...] += val


def _mlp_fwd_bwd(x1, tgt, mod345, g3, g4, w1, w2, cfg):
    B, TX, D, TT, F, JX = cfg.B, cfg.TX, cfg.D, cfg.TT, cfg.F, cfg.JX

    def body(x1_ref, t_ref, mod_ref, g3_ref, g4_ref, w1_ref, w2_ref,
             dx1_ref, loss_ref, h2_ref, dpre_ref, act_ref, dff_ref, dmod_ref, dg3_ref, dg4_ref):
        b, j = pl.program_id(0), pl.program_id(1)
        x1v = x1_ref[0]
        sh, sc, gt = mod_ref[0, 0:1, :], mod_ref[0, 1:2, :], mod_ref[0, 2:3, :]
        h2, vjp_pre = jax.vjp(_rms_mod, x1v, g3_ref[...], sh, sc)
        h2b = h2.astype(BF16)
        pre = jnp.dot(h2b, w1_ref[...], preferred_element_type=F32)
        rl = jnp.maximum(pre, 0.0)
        actb = (rl * rl).astype(BF16)
        ff = jnp.dot(actb, w2_ref[...], preferred_element_type=F32)
        x2, vjp_post = jax.vjp(_post_res, x1v, ff, gt, g4_ref[...])
        err = x2 - t_ref[0]
        loss = 0.5 * jnp.sum(jnp.mean(err * err, axis=-1, keepdims=True))
        dx1a, dff, dgt, dg4 = vjp_post(err * (1.0 / D))
        dffb = dff.astype(BF16)
        dpre = _bdot_nt(dffb, w2_ref[...]) * (2.0 * rl)
        dpreb = dpre.astype(BF16)
        dx1b, dg3, dsh, dsc = vjp_pre(_bdot_nt(dpreb, w1_ref[...]))
        dx1_ref[0] = dx1a + dx1b
        loss_ref[0, 0] = jnp.zeros((SUBLANES, LANES), F32) + loss
        h2_ref[0] = h2b
        dpre_ref[0] = dpreb
        act_ref[0] = actb
        dff_ref[0] = dffb
        _acc(dmod_ref, jnp.concatenate([dsh, dsc, dgt], axis=0)[None], j == 0)
        first = jnp.logical_and(b == 0, j == 0)
        _acc(dg3_ref, dg3, first)
        _acc(dg4_ref, dg4, first)

    tile = lambda n: pl.BlockSpec((1, TT, n), lambda b, j: (b, j, 0))
    return pl.pallas_call(
        body, name="mlp_fwd_bwd", grid=(B, JX),
        in_specs=[tile(D), tile(D), pl.BlockSpec((1, 3, D), lambda b, j: (b, 0, 0)),
                  _full_spec((1, D)), _full_spec((1, D)),
                  pl.BlockSpec((D, F), lambda b, j: (0, 0), pipeline_mode=pl.Buffered(1)),
                  pl.BlockSpec((F, D), lambda b, j: (0, 0), pipeline_mode=pl.Buffered(1))],
        out_specs=[tile(D), pl.BlockSpec((1, 1, SUBLANES, LANES), lambda b, j: (b, j, 0, 0)),
                   tile(D), tile(F), tile(F), tile(D),
                   pl.BlockSpec((1, 3, D), lambda b, j: (b, 0, 0)),
                   _full_spec((1, D)), _full_spec((1, D))],
        out_shape=[jax.ShapeDtypeStruct((B, TX, D), F32),
                   jax.ShapeDtypeStruct((B, JX, SUBLANES, LANES), F32),
                   jax.ShapeDtypeStruct((B, TX, D), BF16), jax.ShapeDtypeStruct((B, TX, F), BF16),
                   jax.ShapeDtypeStruct((B, TX, F), BF16), jax.ShapeDtypeStruct((B, TX, D), BF16),
                   jax.ShapeDtypeStruct((B, 3, D), F32),
                   jax.ShapeDtypeStruct((1, D), F32), jax.ShapeDtypeStruct((1, D), F32)],
        compiler_params=_params(("arbitrary", "arbitrary")),
    )(x1, tgt, mod345, g3, g4, w1, w2)


def _mix_out_bwd(yf, yb, kdf, kdb, r, v, gd, conv, x, mod2, r_k, gw2p, lnx_w, lnx_b, w_out, g2, e_w, dx1, cfg):
    B, TX, D, TT, W, CW, JC, HP, GP = cfg.B, cfg.TX, cfg.D, cfg.TT, cfg.W, cfg.CW, cfg.JC, cfg.HP, cfg.GP

    def body(yf_ref, yb_ref, kdf_ref, kdb_ref, r_ref, v_ref, gd_ref, cv_ref, x_ref, mod_ref,
             rk_ref, gw_ref, lw_ref, lb_ref, wo_ref, g_ref, e_ref, dx1_ref,
             dy_ref, dkb_ref, dr_ref, dv_ref, dgd_ref, dcv_ref, cat_ref, dmix_ref,
             dmod_ref, dg2_ref, drk_ref, dgw_ref, dlw_ref, dlb_ref):
        b, j = pl.program_id(0), pl.program_id(1)
        e = e_ref[...]
        y = _from_heads(yf_ref, cfg) + _from_heads(yb_ref, cfg)
        kbar = 0.5 * (_from_heads(kdf_ref, cfg) + _from_heads(kdb_ref, cfg))
        ro, vjp_ro = jax.vjp(lambda *a: _readout(*a, e), y, kbar, _from_heads(r_ref, cfg),
                             _from_heads(v_ref, cfg), gd_ref[0], rk_ref[...], gw_ref[...], lw_ref[...], lb_ref[...])
        catb = jnp.concatenate([ro, cv_ref[0]], axis=-1).astype(BF16)
        mix = jnp.dot(catb, wo_ref[...], preferred_element_type=F32)
        _, vjp_post = jax.vjp(_post_res, x_ref[0], mix, mod_ref[0], g_ref[...])
        _, dmix, dgate, dg2 = vjp_post(dx1_ref[0])
        dmixb = dmix.astype(BF16)
        dcat = _bdot_nt(dmixb, wo_ref[...])
        dy, dkb, dr, dv, dgd, drk, dgw, dlw, dlb = vjp_ro(dcat[:, :W])
        _to_heads(dy_ref, dy, cfg)
        _to_heads(dkb_ref, dkb, cfg)
        _to_heads(dr_ref, dr, cfg)
        _to_heads(dv_ref, dv, cfg)
        dgd_ref[0] = dgd
        dcv_ref[0] = dcat[:, W:]
        cat_ref[0] = catb
        dmix_ref[0] = dmixb
        _acc(dmod_ref, dgate[None], j == 0)
        first = jnp.logical_and(b == 0, j == 0)
        _acc(dg2_ref, dg2, first)
        _acc(drk_ref, drk, first)
        _acc(dgw_ref, dgw, first)
        _acc(dlw_ref, dlw, first)
        _acc(dlb_ref, dlb, first)

    hs = _head_spec(cfg, lambda j: j + JC)
    ho = _head_spec(cfg, lambda j: j)
    lat = lambda n: pl.BlockSpec((1, TT, n), lambda b, j: (b, j + JC, 0))
    tile = lambda n: pl.BlockSpec((1, TT, n), lambda b, j: (b, j, 0))
    head = jax.ShapeDtypeStruct((B, HP, TX, LANES), F32)
    vec = lambda n: jax.ShapeDtypeStruct((1, n), F32)
    return pl.pallas_call(
        body, name="mix_out_bwd", grid=(B, cfg.JX),
        in_specs=[hs] * 6 + [lat(GP), lat(CW), tile(D),
                             pl.BlockSpec((1, 1, D), lambda b, j: (b, 0, 0)),
                             _full_spec((1, W)), _full_spec((GP, W)), _full_spec((1, W)), _full_spec((1, W)),
                             _full_spec((W + CW, D)), _full_spec((1, D)), _full_spec((W, W)), tile(D)],
        out_specs=[ho] * 4 + [tile(GP), tile(CW), tile(W + CW), tile(D),
                              pl.BlockSpec((1, 1, D), lambda b, j: (b, 0, 0)),
                              _full_spec((1, D)), _full_spec((1, W)), _full_spec((GP, W)),
                              _full_spec((1, W)), _full_spec((1, W))],
        out_shape=[head] * 4 + [jax.ShapeDtypeStruct((B, TX, GP), F32), jax.ShapeDtypeStruct((B, TX, CW), F32),
                                jax.ShapeDtypeStruct((B, TX, W + CW), BF16), jax.ShapeDtypeStruct((B, TX, D), BF16),
                                jax.ShapeDtypeStruct((B, 1, D), F32),
                                vec(D), vec(W), jax.ShapeDtypeStruct((GP, W), F32), vec(W), vec(W)],
        compiler_params=_params(("arbitrary", "arbitrary")),
    )(yf, yb, kdf, kdb, r, v, gd, conv, x, mod2, r_k, gw2p, lnx_w, lnx_b, w_out, g2, e_w, dx1)


def _scan_bwd(ops_f, ops_b, dy, hist, e128, e256, cfg):
    G, T, NCH, NCC = cfg.G, cfg.T, cfg.NCH, cfg.NCC
    CH = SCAN_CHUNK
    G2 = 2 * G

    def body(*refs):
        ins = (refs[0:6], refs[6:12])
        dys, hist_ref, next_ref, e_ref, e2_ref = (refs[12], refs[13]), refs[14], refs[15], refs[16], refs[17]
        outs = (refs[18:24], refs[24:30])
        ds_ref, mm_ref = refs[30], refs[31]
        gi = pl.program_id(0)

        @pl.when(gi == 0)
        def _():
            ds_ref[...] = jnp.zeros_like(ds_ref)

        diag = _diag_mask()
        diag_b = diag.astype(BF16)
        e, e2 = e_ref[...], e2_ref[...]
        rows = functools.partial(_both_rows, ins)
        latent = [(_chunk_pos(NCH - 1 - gi, d == 1, cfg) >= NCC).astype(F32) for d in range(2)]

        def dy_rows(i):
            return jnp.concatenate([dys[d][:, pl.ds(_tok(i, d == 1), 1), :] * latent[d] for d in range(2)], axis=0)

        def put(idx, i, val):
            outs[0][idx][:, pl.ds(_tok(i, False), 1), :] = val[:G]
            outs[1][idx][:, pl.ds(_tok(i, True), 1), :] = val[G:]

        rsum = lambda z: jnp.sum(z, axis=1, keepdims=True)

        def prepare(i):
            return (_col_form(dy_rows(i)), _col_dot([rows(3, i)], diag_b, e2)[0],
                    hist_ref[i].astype(BF16) * (-rows(4, i)).astype(BF16))

        dyc0, vb0, sa_lhs = prepare(CH - 1)
        mm_ref[0] = dyc0
        mm_ref[1] = vb0
        mm_ref[2] = _seg_dot([sa_lhs], e)[0]

        def one_step(i, s_after):
            prv = jnp.maximum(i - 1, 0)
            sp, dyc = hist_ref[i], mm_ref[0]
            ds = ds_ref[...] + dyc * rows(0, i)
            put(0, i, rsum(s_after * dyc))
            put(1, i, rsum(ds * sp))
            put(5, i, rsum(ds * mm_ref[2]))
            put(2, i, rsum(ds * mm_ref[1]))
            dyc_n, vb_n, sa_lhs_n = prepare(prv)
            dsb = ds.astype(BF16)
            res = _seg_dot([dsb * rows(5, i).astype(BF16), dsb * rows(2, i).astype(BF16), sa_lhs_n], e)
            dsa = res[0]
            put(4, i, -rsum(sp * dsa))
            put(3, i, rsum(diag * res[1]))
            mm_ref[0] = dyc_n
            mm_ref[1] = vb_n
            mm_ref[2] = res[2]
            ds_ref[...] = ds * rows(1, i) - dsa * rows(4, i)

        one_step(CH - 1, next_ref[0])

        def bstep(ii, carry):
            i = CH - 1 - ii
            one_step(i, hist_ref[i + 1])
            return carry

        lax.fori_loop(1, CH, bstep, 0)

    cpos = lambda g, rev: _chunk_pos(NCH - 1 - g, rev, cfg)
    toks = [pl.BlockSpec((G, CH, LANES), lambda g, rev=rev: (0, cpos(g, rev), 0)) for rev in (False, True)]
    dy_specs = [pl.BlockSpec((G, CH, LANES), lambda g, rev=rev: (0, jnp.maximum(cpos(g, rev) - NCC, 0), 0))
                for rev in (False, True)]
    out = jax.ShapeDtypeStruct((G, T, LANES), F32)
    res = pl.pallas_call(
        body, name="scan_bwd", grid=(NCH,),
        in_specs=[toks[0]] * 6 + [toks[1]] * 6 + dy_specs
                 + [pl.BlockSpec((CH, G2, HEAD, LANES), lambda g: (NCH - 1 - g, 0, 0, 0)),
                    pl.BlockSpec((1, G2, HEAD, LANES), lambda g: ((NCH - g) * CH, 0, 0, 0)),
                    _full_spec((LANES, LANES)), _full_spec((2 * LANES, LANES))],
        out_specs=[toks[0]] * 6 + [toks[1]] * 6,
        out_shape=[out] * 12,
        scratch_shapes=[pltpu.VMEM((G2, HEAD, LANES), F32), pltpu.VMEM((3, G2, HEAD, LANES), F32)],
        compiler_params=_params(("arbitrary",)),
    )(*ops_f, *ops_b, dy, dy, hist, hist, e128, e256)
    return res[:6], res[6:]


def _mix_prep_bwd(p, mup, mun, w0, w2p, a0, a2p, k_k, k_a, wm, cb, clw, clb, e_w, yconv, sf, sb, ro, xch, cfg):
    B, T, TT, SP, CP, W, CW, HP, JC, PW, GP, KC, KP = (cfg.B, cfg.T, cfg.TT, cfg.SP, cfg.CP, cfg.W, cfg.CW,
                                                       cfg.HP, cfg.JC, cfg.PW, cfg.GP, cfg.KC, cfg.KP)
    pad = KC // 2
    nx = len(xch)

    def body(p_ref, prev_ref, next_ref, mup_ref, mun_ref, w0_ref, w2_ref, a0_ref, a2_ref, kk_ref, ka_ref,
             wm_ref, cb_ref, clw_ref, clb_ref, e_ref, yc_ref, *rest):
        sf_refs, sb_refs = rest[0:6], rest[6:12]
        rdr_ref, rdv_ref, rdkb_ref, rdgd_ref, rdcv_ref = rest[12:17]
        xin, rest = rest[17:17 + nx], rest[17 + nx:]
        (dpz_ref, dmup_ref, dmun_ref, dw0_ref, dw2_ref, da0_ref, da2_ref, dkk_ref, dka_ref,
         dcw_ref, dcb_ref, dclw_ref, dclb_ref) = rest[:13]
        xout, (dyc_ref, du_ref), sems = rest[13:13 + nx], rest[13 + nx:15 + nx], rest[15 + nx:]
        b, j = pl.program_id(0), pl.program_id(1)
        first = jnp.logical_and(b == 0, j == 0)

        @pl.when(first)
        def _():
            _exchange_start(_exchange_copies(xin, xout, *sems, [True] * nx))
        lat = (j >= JC).astype(F32)
        e = e_ref[...]
        mup_v, mun_v = mup_ref[...], mun_ref[...]
        z, zp, zn, rw = _shifted(p_ref, prev_ref, next_ref, mup_v, mun_v, j, cfg)

        def prep(rw_, w0_, w2_, a0_, a2_, kk_, ka_):
            return _rwkv_prep(rw_, w0_, w2_, a0_, a2_, kk_, ka_, e, cfg) + (rw_[:, 3 * W + 2 * PW:SP],)

        _, vjp_prep = jax.vjp(prep, rw, w0_ref[...], w2_ref[...], a0_ref[...], a2_ref[...], kk_ref[...], ka_ref[...])
        fr, fw, fk, fv, fkk, fb = [_from_heads(r_, cfg) for r_ in sf_refs]
        br, bw, bk, bv, bkk, bb = [_from_heads(r_, cfg) for r_ in sb_refs]
        half_kb = (0.5 * lat) * _from_heads(rdkb_ref, cfg)
        cots = (fr + br + lat * _from_heads(rdr_ref, cfg), fv + bv + lat * _from_heads(rdv_ref, cfg), fkk + bkk,
                fw, fk + half_kb, fb, bw, bk + half_kb, bb, lat * rdgd_ref[0])
        drw, dw0, dw2, da0, da2, dkk, dka = vjp_prep(cots)
        _acc(dmup_ref, jnp.sum(drw * (zp - z), axis=0, keepdims=True), first)
        _acc(dmun_ref, jnp.sum(drw * (zn - z), axis=0, keepdims=True), first)
        for ref, val in ((dw0_ref, dw0), (dw2_ref, dw2), (da0_ref, da0), (da2_ref, da2), (dkk_ref, dkk), (dka_ref, dka)):
            _acc(ref, val, first)

        dpz_ref[0, :, 0:SP] = drw

        @pl.when(first)
        def _():
            for ref in (dcw_ref, dcb_ref, dclw_ref, dclb_ref):
                ref[...] = jnp.zeros_like(ref)

        @pl.when(j < JC)
        def _():
            dpz_ref[0, :, SP:] = jnp.zeros((TT, 2 * CW), F32)

        @pl.when(j >= JC)
        def _():
            u, vjp_glu = jax.vjp(lambda c_: _glu(c_, cfg), p_ref[0, :, SP:])
            _, vjp_post = jax.vjp(_conv_post, yc_ref[0], cb_ref[...], clw_ref[...], clb_ref[...])
            dyc, dcb, dclw, dclb = vjp_post(rdcv_ref[0])
            dyc_ref[...] = dyc
            _conv_lines(dyc_ref, wm_ref, du_ref, KC, True)
            (dcv,) = vjp_glu(du_ref[...])
            dpz_ref[0, :, SP:] = dcv
            for i in range(KC):
                dcw_ref[i:i + 1, :] += jnp.sum(dyc * _line_shift(u, i - pad), axis=0, keepdims=True)
            dcb_ref[...] += dcb
            dclw_ref[...] += dclw
            dclb_ref[...] += dclb

        @pl.when(jnp.logical_and(b == B - 1, j == cfg.J - 1))
        def _():
            _exchange_wait(_exchange_copies(xin, xout, *sems, [True] * nx))

    prev, nxt = _halo_specs(cfg, SP)
    hs = _head_spec(cfg, lambda j: j)
    hl = _head_spec(cfg, lambda j: jnp.maximum(j - JC, 0))
    latn = lambda n: pl.BlockSpec((1, TT, n), lambda b, j: (b, jnp.maximum(j - JC, 0), 0))
    hbm = pl.BlockSpec(memory_space=pltpu.HBM)
    vec = lambda n: jax.ShapeDtypeStruct((1, n), F32)
    small_shapes = [vec(SP), vec(SP), vec(2 * W), jax.ShapeDtypeStruct((PW, 2 * W), F32), vec(2 * W),
                    jax.ShapeDtypeStruct((PW, 2 * W), F32), vec(W), vec(W),
                    jax.ShapeDtypeStruct((KP, CW), F32), vec(CW), vec(CW), vec(CW)]
    return pl.pallas_call(
        body, name="mix_prep_bwd", grid=(B, cfg.J),
        in_specs=[pl.BlockSpec((1, TT, CP), lambda b, j: (b, j, 0)), prev, nxt,
                  _full_spec((1, SP)), _full_spec((1, SP)),
                  _full_spec((1, 2 * W)), _full_spec((PW, 2 * W)),
                  _full_spec((1, 2 * W)), _full_spec((PW, 2 * W)),
                  _full_spec((1, W)), _full_spec((1, W)),
                  pl.BlockSpec((KC, LINE, CW), lambda b, j: (0, 0, 0), pipeline_mode=pl.Buffered(1)),
                  _full_spec((1, CW)), _full_spec((1, CW)), _full_spec((1, CW)),
                  _full_spec((W, W)), pl.BlockSpec((1, TT, CW), lambda b, j: (b, j, 0))]
                 + [hs] * 12 + [hl] * 3 + [latn(GP), latn(CW)] + [hbm] * nx,
        out_specs=[pl.BlockSpec((1, TT, CP), lambda b, j: (b, j, 0))] + [_full_spec(s.shape) for s in small_shapes]
                  + [hbm] * nx,
        out_shape=[jax.ShapeDtypeStruct((B, T, CP), F32)] + small_shapes + _exchange_shapes(xch, [True] * nx),
        scratch_shapes=[pltpu.VMEM((TT, CW), F32), pltpu.VMEM((TT, CW), F32)] + _exchange_sems(nx),
        compiler_params=_params(("arbitrary", "arbitrary")),
    )(p, p, p, mup, mun, w0, w2p, a0, a2p, k_k, k_a, wm, cb, clw, clb, e_w, yconv, *sf, *sb, *ro, *xch)


def _in_proj_bwd(dpz, xcat, modt, g1, w_in_p, mup, mun, dx1, cfg):
    B, T, TX, D, TT, SP, CP, JC = cfg.B, cfg.T, cfg.TX, cfg.D, cfg.TT, cfg.SP, cfg.CP, cfg.JC

    def body(d_ref, prev_ref, next_ref, x_ref, mod_ref, g_ref, w_ref, mup_ref, mun_ref, dx1_ref,
             gx_ref, dp_ref, dmod_ref, dg_ref):
        b, j = pl.program_id(0), pl.program_id(1)
        has_prev, has_next = _halo_flags(j, cfg)
        mp, mn = mup_ref[...], mun_ref[...]
        drw = d_ref[0, :, 0:SP]
        dprev, dnext = _shift_rows(drw, prev_ref[0, SUBLANES - 1:SUBLANES, :] * has_prev,
                                   next_ref[0, 0:1, :] * has_next)
        dz = drw * (1.0 - mp - mn) + mp * dnext + mn * dprev
        dpb = jnp.concatenate([dz, d_ref[0, :, SP:]], axis=-1).astype(BF16)
        dp_ref[0] = dpb
        dh = _bdot_nt(dpb, w_ref[...])
        _, vjp_h = jax.vjp(_rms_mod, x_ref[0], g_ref[...], mod_ref[0, 0, 0:1, :], mod_ref[0, 0, 1:2, :])
        dx, dg, dsh, dsc = vjp_h(dh)
        dmod_ref[0, 0] = jnp.concatenate([dsh, dsc], axis=0)
        _acc(dg_ref, dg, jnp.logical_and(b == 0, j == 0))

        @pl.when(j >= JC)
        def _():
            gx_ref[0] = dx + dx1_ref[0]

    prev, nxt = _halo_specs(cfg, SP)
    lat = pl.BlockSpec((1, TT, D), lambda b, j: (b, jnp.maximum(j - JC, 0), 0))
    return pl.pallas_call(
        body, name="in_proj_bwd", grid=(B, cfg.J),
        in_specs=[pl.BlockSpec((1, TT, CP), lambda b, j: (b, j, 0)), prev, nxt,
                  pl.BlockSpec((1, TT, D), lambda b, j: (b, j, 0)),
                  pl.BlockSpec((1, 1, 2, D), lambda b, j: (b, j, 0, 0)),
                  _full_spec((1, D)), _full_spec((D, CP)), _full_spec((1, SP)), _full_spec((1, SP)), lat],
        out_specs=[lat, pl.BlockSpec((1, TT, CP), lambda b, j: (b, j, 0)),
                   pl.BlockSpec((1, 1, 2, D), lambda b, j: (b, j, 0, 0)), _full_spec((1, D))],
        out_shape=[jax.ShapeDtypeStruct((B, TX, D), F32), jax.ShapeDtypeStruct((B, T, CP), BF16),
                   jax.ShapeDtypeStruct((B, cfg.J, 2, D), F32), jax.ShapeDtypeStruct((1, D), F32)],
        compiler_params=_params(("arbitrary", "arbitrary")),
    )(dpz, dpz, dpz, xcat, modt, g1, w_in_p, mup, mun, dx1)


def _pick_tile(n, pref):
    for t in pref:
        if n % t == 0:
            return t
    return n


def _grad_matmul(a, g, name):
    K, M = a.shape
    N = g.shape[1]
    tm = _pick_tile(M, (512, 256, 128))
    tn = _pick_tile(N, (1024, 768, 512, 256, 128))
    tk = _pick_tile(K, (2048, 1024, 512, 256, 128, 64))
    nk = K // tk

    def body(a_ref, g_ref, o_ref):
        k = pl.program_id(2)
        _acc(o_ref, _bdot_tn(a_ref[...], g_ref[...]), k == 0)

    return pl.pallas_call(
        body, name=name, grid=(M // tm, N // tn, nk),
        in_specs=[pl.BlockSpec((tk, tm), lambda i, j, k: (k, i)),
                  pl.BlockSpec((tk, tn), lambda i, j, k: (k, j))],
        out_specs=pl.BlockSpec((tm, tn), lambda i, j, k: (i, j)),
        out_shape=jax.ShapeDtypeStruct((M, N), F32),
        compiler_params=_params(("parallel", "parallel", "arbitrary")),
    )(a, g)


def _ada_fwd(crows, ada_w, ada_b):
    D = crows.shape[1]
    n6 = ada_w.shape[1]
    tn = _pick_tile(n6, (1024, 512, 256, 128))

    def body(c_ref, w_ref, b_ref, s_ref, m_ref):
        s = _silu(c_ref[...])
        s_ref[...] = s
        m_ref[...] = _bdot(s, w_ref[...]) + b_ref[...]

    return pl.pallas_call(
        body, name="ada_fwd", grid=(n6 // tn,),
        in_specs=[_full_spec((SUBLANES, D)), pl.BlockSpec((D, tn), lambda i: (0, i)),
                  pl.BlockSpec((1, tn), lambda i: (0, i))],
        out_specs=[_full_spec((SUBLANES, D)), pl.BlockSpec((SUBLANES, tn), lambda i: (0, i))],
        out_shape=[jax.ShapeDtypeStruct((SUBLANES, D), F32), jax.ShapeDtypeStruct((SUBLANES, n6), F32)],
        compiler_params=_params(("arbitrary",)),
    )(crows, ada_w, ada_b)


def _ada_bwd(s_all, g_all, g_mine, c_ctx, ada_w, nb):
    D = s_all.shape[1]
    n6 = g_all.shape[1]
    ns = g_mine.shape[1]

    def body(s_ref, g_ref, gm_ref, c_ref, w_ref, dw_ref, db_ref, dc_ref):
        g = g_ref[...]
        dw_ref[...] = _bdot_tn(s_ref[...], gm_ref[...])
        db_ref[...] = jnp.sum(g, axis=0, keepdims=True)
        rows = lax.broadcasted_iota(jnp.int32, (g.shape[0], 1), 0)
        gc = jnp.sum(jnp.where(rows % SUBLANES == nb, g, 0.0), axis=0, keepdims=True)
        ds = _bdot_nt(gc, w_ref[...])
        c = c_ref[...]
        sg = _sigmoid(c)
        dc_ref[...] = ds * (sg + c * sg * (1.0 - sg))

    return pl.pallas_call(
        body, name="ada_bwd",
        out_shape=[jax.ShapeDtypeStruct((D, ns), F32), jax.ShapeDtypeStruct((1, n6), F32),
                   jax.ShapeDtypeStruct((1, D), F32)],
        compiler_params=_params(),
    )(s_all, g_all, g_mine, c_ctx, ada_w)


def _adamw(parts, w, m, v, name):
    P, R, C = parts.shape
    small = R * C * (P + 7) * 4 <= 4 * 1024 * 1024
    tr = R if small else _pick_tile(R, (256, 128, 64, 32, 16, 8))

    def body(p_ref, w_ref, m_ref, v_ref, g_ref, d_ref, nm_ref, nv_ref):
        g = p_ref[0].astype(F32)
        for i in range(1, P):
            g = g + p_ref[i].astype(F32)
        nm = ADAM_B1 * m_ref[...] + (1.0 - ADAM_B1) * g
        nv = ADAM_B2 * v_ref[...] + (1.0 - ADAM_B2) * (g * g)
        m_hat = nm / (1.0 - ADAM_B1 ** ADAM_STEP)
        v_hat = nv / (1.0 - ADAM_B2 ** ADAM_STEP)
        g_ref[...] = g
        d_ref[...] = -ADAM_LR * (m_hat / (jnp.sqrt(v_hat) + ADAM_EPS) + ADAM_WD * w_ref[...])
        nm_ref[...] = nm
        nv_ref[...] = nv

    blk = pl.BlockSpec((tr, C), lambda i: (i, 0))
    out = jax.ShapeDtypeStruct((R, C), F32)
    return pl.pallas_call(
        body, name=name, grid=(R // tr,),
        in_specs=[pl.BlockSpec((P, tr, C), lambda i: (0, i, 0)), blk, blk, blk],
        out_specs=[blk] * 4, out_shape=[out] * 4,
        compiler_params=_params(("parallel",)),
    )(parts, w, m, v)


def _local_step(cfg, x, c, ctx, tgt, fw, late):
    B, D, W, CW, JC, T, TX = cfg.B, cfg.D, cfg.W, cfg.CW, cfg.JC, cfg.T, cfg.TX
    e_w = _block_ones(W)
    e128 = _block_ones(LANES)
    e256 = jnp.concatenate([e128, e128], axis=0)
    row = lambda a: a.reshape(1, -1)

    ada_wb = fw["ada_w"].astype(BF16)
    w_in_p = _pad_cols(fw["w_in"], cfg).astype(BF16)
    mup = _pad_cols(fw["mu_prev"], cfg, True)
    mun = _pad_cols(fw["mu_next"], cfg, True)
    w0, a0 = row(fw["decay_w0"]), row(fw["iclr_a0"])
    w2p, a2p = _pair_weight(fw["decay_w2"], cfg), _pair_weight(fw["iclr_a2"], cfg)
    wm_fwd, wm_bwd = _conv_tables(fw["conv_w"], cfg.KC)
    gw2p = jnp.pad(fw["gate_w2"], ((0, cfg.GP - cfg.GR), (0, 0)))
    r_k = row(fw["r_k"])

    crows = jnp.concatenate([c, fw["c_ctx"], jnp.zeros((SUBLANES - B - 1, D), F32)], axis=0)
    s_rows, mods = _ada_fwd(crows, ada_wb, fw["ada_b"])
    mod_x = mods[:B].reshape(B, 6, D)
    mod_c = mods[B].reshape(6, D)
    modt = jnp.concatenate([jnp.broadcast_to(mod_c[None, None, 0:2], (B, JC, 2, D)),
                            jnp.broadcast_to(mod_x[:, None, 0:2], (B, cfg.JX, 2, D))], axis=1)
    mod2, mod345 = mod_x[:, 2:3], mod_x[:, 3:6]

    xcat = jnp.concatenate([ctx, x], axis=1)
    p, hb = _in_proj(xcat, modt, fw["mix_pre_g"], w_in_p, cfg)
    prep_w = (mup, mun, w0, w2p, a0, a2p, fw["k_k"], fw["k_a"])
    conv_w = (fw["conv_b"], fw["conv_ln_w"], fw["conv_ln_b"], e_w)
    (r, v, kk, w_f, kd_f, b_f, w_b, kd_b, b_b, gd, conv, yconv, g_w_out, g_w1, g_w2) = _mix_prep(
        p, *prep_w, wm_fwd, *conv_w, late, cfg)
    w_outb, w1b, w2b = g_w_out.reshape(-1, D), _blocks_to_cols(g_w1), g_w2.reshape(-1, D)
    flat = lambda a: a.reshape(cfg.G, a.shape[2], LANES)
    heads = lambda a: a.reshape(B, cfg.HP, a.shape[1], LANES)
    ops_f = tuple(flat(a) for a in (r, w_f, kd_f, v, kk, b_f))
    ops_b = tuple(flat(a) for a in (r, w_b, kd_b, v, kk, b_b))
    y_f, y_b, hist, s_fin = _scan_fwd(ops_f, ops_b, e128, e256, cfg)
    hist = lax.dynamic_update_slice_in_dim(hist, s_fin[None], cfg.NCH * SCAN_CHUNK, axis=0)
    out_args = (heads(y_f), heads(y_b), kd_f, kd_b, r, v, gd, conv, x, mod2, r_k, gw2p, fw["lnx_w"], fw["lnx_b"],
                w_outb, fw["mix_post_g"], e_w)
    x1 = _mix_out(*out_args, cfg)

    dx1, loss_t, h2b, dpreb, actb, dffb, dmod345, dg3, dg4 = _mlp_fwd_bwd(
        x1, tgt, mod345, fw["mlp_pre_g"], fw["mlp_post_g"], w1b, w2b, cfg)
    (dy, dkb, dr_c, dv_c, dgd, dconv, catb, dmixb, dmod2, dg2, drk, dgw, dlw, dlb) = _mix_out_bwd(
        *out_args, dx1, cfg)
    sf, sb = _scan_bwd(ops_f, ops_b, flat(dy), hist, e128, e256, cfg)

    tokens = lambda a: a.reshape(-1, a.shape[-1])
    d_w_out = _grad_matmul(tokens(catb), tokens(dmixb), "grad_w_out")
    d_w1 = _grad_matmul(tokens(h2b), tokens(dpreb), "grad_mlp_w1")
    d_w2 = _grad_matmul(tokens(actb), tokens(dffb), "grad_mlp_w2")
    early = [d_w_out.astype(BF16).reshape(N_DEV, -1, D), _cols_to_blocks(d_w1.astype(BF16)),
             d_w2.astype(BF16).reshape(N_DEV, -1, D)]
    (dpz, dmup, dmun, dw0, dw2p, da0, da2p, dkk, dka, dcw, dcb, dclw, dclb, x_w_out, x_w1, x_w2) = _mix_prep_bwd(
        p, *prep_w, wm_bwd, *conv_w, yconv, [heads(a) for a in sf], [heads(a) for a in sb],
        (dr_c, dv_c, dkb, dgd, dconv), early, cfg)
    grad_x, dpb, dmodt, dg1 = _in_proj_bwd(dpz, xcat, modt, fw["mix_pre_g"], w_in_p, mup, mun, dx1, cfg)
    d_w_in = _grad_matmul(tokens(hb), tokens(dpb), "grad_w_in")
    exchanged = {"w_out": x_w_out, "mlp_w1": x_w1, "mlp_w2": x_w2}

    grads = {
        "mix_pre_g": dg1, "mix_post_g": dg2, "mlp_pre_g": dg3, "mlp_post_g": dg4,
        "w_in": _unpad_cols(d_w_in, cfg),
        "mu_prev": _unpad_cols(dmup, cfg, True), "mu_next": _unpad_cols(dmun, cfg, True),
        "decay_w0": dw0.reshape(2, W), "decay_w2": _unpair_weight(dw2p, cfg),
        "iclr_a0": da0.reshape(2, W), "iclr_a2": _unpair_weight(da2p, cfg),
        "k_k": dkk, "k_a": dka, "r_k": drk.reshape(fw["r_k"].shape),
        "gate_w2": dgw[:cfg.GR], "lnx_w": dlw, "lnx_b": dlb,
        "conv_w": dcw[:cfg.KC], "conv_b": dcb, "conv_ln_w": dclw, "conv_ln_b": dclb,
    }
    dmod_x = jnp.concatenate([jnp.sum(dmodt[:, JC:], axis=1), dmod2, dmod345], axis=1).reshape(B, 6 * D)
    dmod_c = jnp.concatenate([jnp.sum(dmodt[:, :JC], axis=(0, 1)), jnp.zeros((4, D), F32)], axis=0).reshape(1, 6 * D)
    g_rows = jnp.concatenate([dmod_x, dmod_c, jnp.zeros((SUBLANES - B - 1, 6 * D), F32)], axis=0)
    return loss_t, grad_x, grads, exchanged, s_rows, g_rows


def _my_index():
    return 4 * lax.axis_index("x") + 2 * lax.axis_index("y") + lax.axis_index("c")


def _gather_phase(phase, ins, outs, send_sems, recv_sems, local_sems):
    n = len(ins)
    x, y, c = lax.axis_index("x"), lax.axis_index("y"), lax.axis_index("c")
    index = lambda px, py, pc: 4 * px + 2 * py + pc
    me, sibling = (x, y, c), (x, y, 1 - c)
    chips = [(1 - x, y), (x, 1 - y), (1 - x, 1 - y)]

    def copy(a, k, block, to, src=None):
        dst = outs[a].at[index(*block)]
        return pltpu.make_async_remote_copy(
            src_ref=dst if src is None else src, dst_ref=dst,
            send_sem=send_sems.at[k, a], recv_sem=recv_sems.at[k, a],
            device_id=to, device_id_type=pl.DeviceIdType.MESH)

    local = [pltpu.make_async_copy(ins[a], outs[a].at[index(*me)], local_sems.at[a]) for a in range(n)]
    first = []
    for a in range(n):
        first.append(copy(a, 0, me, sibling, src=ins[a]))
        first += [copy(a, 1 + j, me, (*chip, c), src=ins[a]) for j, chip in enumerate(chips)]
    passed = [copy(a, 4 + j, (*chip, c), sibling) for j, chip in enumerate(chips) for a in range(n)]
    if phase == 0:
        for cp in local + first:
            cp.start()
    elif phase == 1:
        arrived = [copy(a, 1 + j, (*chip, c), me) for j, chip in enumerate(chips) for a in range(n)]
        for got, fwd in zip(arrived, passed):
            got.wait_recv()
            fwd.start()
    else:
        for a in range(n):
            copy(a, 0, sibling, me).wait_recv()
            for j, chip in enumerate(chips):
                copy(a, 4 + j, (*chip, 1 - c), me).wait_recv()
        for cp in first + passed:
            cp.wait_send()
        for cp in local:
            cp.wait()


def _gather_two_level(arrays, name):
    n = len(arrays)

    def body(*refs):
        for phase in range(3):
            _gather_phase(phase, refs[:n], refs[n:2 * n], *refs[2 * n:])

    hbm = pl.BlockSpec(memory_space=pltpu.HBM)
    return pl.pallas_call(
        body, name=name, out_shape=_exchange_shapes(arrays, [False] * n),
        in_specs=[hbm] * n, out_specs=[hbm] * n, scratch_shapes=_exchange_sems(n),
    )(*arrays)


def _exchange_copies(ins, outs, send_sems, recv_sems, local_sems, scatter):
    n = len(ins)
    x, y, c = lax.axis_index("x"), lax.axis_index("y"), lax.axis_index("c")
    me = 4 * x + 2 * y + c
    flip = lambda v, f: 1 - v if f else v

    def piece(a, dest):
        return ins[a].at[dest] if scatter[a] else ins[a]

    local = [pltpu.make_async_copy(piece(a, me), outs[a].at[me], local_sems.at[a]) for a in range(n)]
    sends, recvs = [], []
    for k in range(1, N_DEV):
        fx, fy, fc = (k >> 2) & 1, (k >> 1) & 1, k & 1
        peer = (flip(x, fx), flip(y, fy), flip(c, fc))
        peer_idx = 4 * peer[0] + 2 * peer[1] + peer[2]
        for a in range(n):
            sends.append(pltpu.make_async_remote_copy(
                src_ref=piece(a, peer_idx), dst_ref=outs[a].at[me],
                send_sem=send_sems.at[k - 1, a], recv_sem=recv_sems.at[k - 1, a],
                device_id=peer, device_id_type=pl.DeviceIdType.MESH))
            recvs.append(pltpu.make_async_remote_copy(
                src_ref=piece(a, peer_idx), dst_ref=outs[a].at[peer_idx],
                send_sem=send_sems.at[k - 1, a], recv_sem=recv_sems.at[k - 1, a],
                device_id=peer, device_id_type=pl.DeviceIdType.MESH))
    return local, sends, recvs


def _exchange_start(copies):
    local, sends, _ = copies
    for cp in local + sends:
        cp.start()


def _exchange_wait(copies):
    local, sends, recvs = copies
    for cp in recvs:
        cp.wait_recv()
    for cp in sends:
        cp.wait_send()
    for cp in local:
        cp.wait()


def _exchange_shapes(arrays, scatter):
    return [jax.ShapeDtypeStruct(a.shape if s else (N_DEV,) + a.shape, a.dtype) for a, s in zip(arrays, scatter)]


def _exchange_sems(n):
    return [pltpu.SemaphoreType.DMA((N_DEV - 1, n)), pltpu.SemaphoreType.DMA((N_DEV - 1, n)),
            pltpu.SemaphoreType.DMA((n,))]


def _exchange(arrays, scatter, name):
    n = len(arrays)

    def body(*refs):
        copies = _exchange_copies(refs[:n], refs[n:2 * n], *refs[2 * n:], scatter)
        _exchange_start(copies)
        _exchange_wait(copies)

    hbm = pl.BlockSpec(memory_space=pltpu.HBM)
    return pl.pallas_call(
        body, name=name, out_shape=_exchange_shapes(arrays, scatter),
        in_specs=[hbm] * n, out_specs=[hbm] * n, scratch_shapes=_exchange_sems(n),
    )(*arrays)


def _pack(parts):
    flat = jnp.concatenate([p.reshape(-1) for p in parts])
    total = _round_up(flat.shape[0], SUBLANES * LANES)
    return jnp.pad(flat, (0, total - flat.shape[0])).reshape(-1, LANES)


def _unpack(buf, shapes):
    flat = buf.reshape(-1)
    out, pos = [], 0
    for s in shapes:
        n = int(np.prod(s))
        out.append(flat[pos:pos + n].reshape(s))
        pos += n
    return out


_SHARDED_SMALL = ("decay_w0", "decay_w2", "iclr_a0", "iclr_a2", "gate_w2", "conv_w")
_REPLICATED = ("mix_pre_g", "mix_post_g", "mlp_pre_g", "mlp_post_g", "mu_prev", "mu_next", "k_k", "k_a", "r_k",
               "lnx_w", "lnx_b", "conv_b", "conv_ln_w", "conv_ln_b")
_ADA_SMALL = ("c_ctx", "ada_b")
_WEIGHTS = ("c_ctx", "ada_w", "ada_b", "mix_pre_g", "mix_post_g", "mlp_pre_g", "mlp_post_g", "w_in", "mu_prev",
            "mu_next", "decay_w0", "decay_w2", "iclr_a0", "iclr_a2", "k_k", "k_a", "r_k", "gate_w2", "lnx_w", "lnx_b",
            "conv_w", "conv_b", "conv_ln_w", "conv_ln_b", "w_out", "mlp_w1", "mlp_w2")
_INPUTS = ("x", "c", "ctx") + _WEIGHTS + ("loss_target",) + tuple("m_" + n for n in _WEIGHTS) + tuple(
    "v_" + n for n in _WEIGHTS)


def _cols_to_blocks(a):
    a = a.reshape(a.shape[:-1] + (N_DEV, a.shape[-1] // N_DEV))
    return jnp.moveaxis(a, -2, 0)


def _blocks_to_cols(a):
    a = jnp.moveaxis(a, 0, -2)
    return a.reshape(a.shape[:-2] + (a.shape[-2] * a.shape[-1],))


def kernel(x, c, ctx, c_ctx, ada_w, ada_b, mix_pre_g, mix_post_g, mlp_pre_g, mlp_post_g, w_in, mu_prev, mu_next, decay_w0, decay_w2, iclr_a0, iclr_a2, k_k, k_a, r_k, gate_w2, lnx_w, lnx_b, conv_w, conv_b, conv_ln_w, conv_ln_b, w_out, mlp_w1, mlp_w2, loss_target, m_c_ctx, m_ada_w, m_ada_b, m_mix_pre_g, m_mix_post_g, m_mlp_pre_g, m_mlp_post_g, m_w_in, m_mu_prev, m_mu_next, m_decay_w0, m_decay_w2, m_iclr_a0, m_iclr_a2, m_k_k, m_k_a, m_r_k, m_gate_w2, m_lnx_w, m_lnx_b, m_conv_w, m_conv_b, m_conv_ln_w, m_conv_ln_b, m_w_out, m_mlp_w1, m_mlp_w2, v_c_ctx, v_ada_w, v_ada_b, v_mix_pre_g, v_mix_post_g, v_mlp_pre_g, v_mlp_post_g, v_w_in, v_mu_prev, v_mu_next, v_decay_w0, v_decay_w2, v_iclr_a0, v_iclr_a2, v_k_k, v_k_a, v_r_k, v_gate_w2, v_lnx_w, v_lnx_b, v_conv_w, v_conv_b, v_conv_ln_w, v_conv_ln_b, v_w_out, v_mlp_w1, v_mlp_w2):
    given = dict(zip(_INPUTS, (x, c, ctx, c_ctx, ada_w, ada_b, mix_pre_g, mix_post_g, mlp_pre_g, mlp_post_g, w_in, mu_prev, mu_next, decay_w0, decay_w2, iclr_a0, iclr_a2, k_k, k_a, r_k, gate_w2, lnx_w, lnx_b, conv_w, conv_b, conv_ln_w, conv_ln_b, w_out, mlp_w1, mlp_w2, loss_target, m_c_ctx, m_ada_w, m_ada_b, m_mix_pre_g, m_mix_post_g, m_mlp_pre_g, m_mlp_post_g, m_w_in, m_mu_prev, m_mu_next, m_decay_w0, m_decay_w2, m_iclr_a0, m_iclr_a2, m_k_k, m_k_a, m_r_k, m_gate_w2, m_lnx_w, m_lnx_b, m_conv_w, m_conv_b, m_conv_ln_w, m_conv_ln_b, m_w_out, m_mlp_w1, m_mlp_w2, v_c_ctx, v_ada_w, v_ada_b, v_mix_pre_g, v_mix_post_g, v_mlp_pre_g, v_mlp_post_g, v_w_in, v_mu_prev, v_mu_next, v_decay_w0, v_decay_w2, v_iclr_a0, v_iclr_a2, v_k_k, v_k_a, v_r_k, v_gate_w2, v_lnx_w, v_lnx_b, v_conv_w, v_conv_b, v_conv_ln_w, v_conv_ln_b, v_w_out, v_mlp_w1, v_mlp_w2)))
    loc = {}
    for pre in ("", "m_", "v_"):
        for n in _WEIGHTS:
            a = given[pre + n]
            a = a.reshape(1, -1) if n == "c_ctx" else a[0]
            loc[pre + n] = a.reshape(1, -1) if a.ndim == 1 else a
    B, TX, D = x.shape
    W, CW = loc["k_k"].shape[1], loc["conv_b"].shape[1]
    cfg = _Cfg(B, TX, ctx.shape[1], D, W, CW, loc["decay_w2"].shape[1], loc["gate_w2"].shape[0],
               loc["conv_w"].shape[0], loc["mlp_w1"].shape[1] * N_DEV)
    me = _my_index()

    small_shapes = [loc[n].shape for n in _SHARDED_SMALL]
    got = _gather_two_level(
        [loc["ada_w"].astype(BF16), loc["w_in"].astype(BF16), _pack([loc[n] for n in _SHARDED_SMALL])],
        "gather_weights")
    fw = {n: loc[n] for n in _REPLICATED + _ADA_SMALL}
    fw["ada_w"] = _blocks_to_cols(got[0])
    fw["w_in"] = _blocks_to_cols(got[1])
    per_dev = [_unpack(got[2][i], small_shapes) for i in range(N_DEV)]
    for j, n in enumerate(_SHARDED_SMALL):
        fw[n] = jnp.concatenate([per_dev[i][j] for i in range(N_DEV)], axis=-1)
    late = [loc[n].astype(BF16) for n in ("w_out", "mlp_w1", "mlp_w2")]

    loss_t, grad_x, grads, exchanged, s_rows, g_rows = _local_step(cfg, x, c, ctx, loss_target, fw, late)
    loss = lax.psum(jnp.sum(loss_t[:, :, 0, 0]), ("x", "y", "c"))

    small_blocks = jnp.stack([_pack([_cols_to_blocks(grads[n])[i] for n in _SHARDED_SMALL]) for i in range(N_DEV)])
    sent = _exchange(
        [_cols_to_blocks(grads["w_in"].astype(BF16)), small_blocks,
         _pack([grads[n] for n in _REPLICATED]), s_rows, g_rows],
        [True] * 2 + [False] * 3, "exchange_grads")
    s_all = sent[3].reshape(N_DEV * SUBLANES, D)
    g_all = sent[4].reshape(N_DEV * SUBLANES, 6 * D)
    ns = 6 * D // N_DEV
    g_mine = lax.dynamic_slice_in_dim(g_all, me * ns, ns, axis=1)
    d_ada_w, d_ada_b, d_c_ctx = _ada_bwd(s_all, g_all, g_mine, loc["c_ctx"], fw["ada_w"], B)

    res = {}

    def update(name, parts):
        res[name] = _adamw(parts, loc[name], loc["m_" + name], loc["v_" + name], "adamw_" + name)

    update("w_in", sent[0])
    for n in ("w_out", "mlp_w1", "mlp_w2"):
        update(n, exchanged[n])
    update("ada_w", d_ada_w[None])

    def update_packed(names, parts, tag):
        shapes = [loc[n].shape for n in names]
        packed = _adamw(parts, *[_pack([loc[pre + n] for n in names]) for pre in ("", "m_", "v_")], "adamw_" + tag)
        unpacked = [_unpack(p, shapes) for p in packed]
        for j, n in enumerate(names):
            res[n] = tuple(u[j] for u in unpacked)

    update_packed(_SHARDED_SMALL, sent[1], "sharded_small")
    update_packed(_REPLICATED, sent[2], "replicated")
    update_packed(_ADA_SMALL, _pack([d_c_ctx, d_ada_b])[None], "ada_small")

    outs = [loss, grad_x]
    for k in range(4):
        for n in _WEIGHTS:
            outs.append(res[n][k].reshape(given[n].shape))
    return tuple(outs)
```

```python
import functools

import numpy as np
import jax
import jax.numpy as jnp
from jax import lax
from jax.experimental import pallas as pl
from jax.experimental.pallas import tpu as pltpu

F32 = jnp.float32
BF16 = jnp.bfloat16

EPS_RMS = 1e-6
EPS_LN = 1e-5
EPS_GN = 64e-5
LINE = 64
HEAD = 64
LANES = 128
SUBLANES = 8
MXU_DIM = 256
SCAN_CHUNK = 16
N_DEV = 8
VMEM_LIMIT = 56 * 1024 * 1024

ADAM_LR = 0.001
ADAM_B1 = 0.9
ADAM_B2 = 0.999
ADAM_EPS = 1e-08
ADAM_WD = 0.01
ADAM_STEP = 10


def _round_up(n, m):
    return (n + m - 1) // m * m


def _params(semantics=None, vmem=VMEM_LIMIT):
    return pltpu.CompilerParams(dimension_semantics=semantics, vmem_limit_bytes=vmem)


def _bdot(a, b):
    return jnp.dot(a.astype(BF16), b.astype(BF16), preferred_element_type=F32)


def _bdot_nt(a, b):
    return lax.dot_general(a.astype(BF16), b.astype(BF16), (((1,), (1,)), ((), ())),
                           preferred_element_type=F32)


def _bdot_tn(a, b):
    return lax.dot_general(a.astype(BF16), b.astype(BF16), (((0,), (0,)), ((), ())),
                           preferred_element_type=F32)


@jax.custom_vjp
def _mm(a, b):
    return _bdot(a, b)


def _mm_fwd(a, b):
    return _bdot(a, b), (a, b)


def _mm_bwd(res, g):
    a, b = res
    return _bdot_nt(g, b), _bdot_tn(a, g)


_mm.defvjp(_mm_fwd, _mm_bwd)


def _seg_sum_raw(x, e):
    hi = x.astype(BF16)
    lo = (x - hi.astype(F32)).astype(BF16)
    n = e.shape[0]
    parts = [jnp.dot(hi[:, s:s + n], e, preferred_element_type=F32)
             + jnp.dot(lo[:, s:s + n], e, preferred_element_type=F32) for s in range(0, x.shape[1], n)]
    return parts[0] if len(parts) == 1 else jnp.concatenate(parts, axis=1)


@jax.custom_vjp
def _seg_sum(x, e):
    return _seg_sum_raw(x, e)


def _seg_sum_fwd(x, e):
    return _seg_sum_raw(x, e), e


def _seg_sum_bwd(e, g):
    return _seg_sum_raw(g, e), None


_seg_sum.defvjp(_seg_sum_fwd, _seg_sum_bwd)


def _block_ones(n, seg=HEAD):
    i = np.arange(n) // seg
    return jnp.asarray((i[:, None] == i[None, :]).astype(np.float32), dtype=BF16)


def _rms(xv, g):
    ms = jnp.mean(xv * xv, axis=-1, keepdims=True)
    return xv * lax.rsqrt(ms + EPS_RMS) * g


def _rms_mod(xv, g, shift, scale):
    return _rms(xv, g) * (1.0 + scale) + shift


def _sigmoid(z):
    return 1.0 / (1.0 + jnp.exp(-z))


def _silu(z):
    return z * _sigmoid(z)


def _softplus(z):
    return jnp.maximum(z, 0.0) + jnp.log(1.0 + jnp.exp(-jnp.abs(z)))


class _Cfg:
    def __init__(self, B, TX, TC, D, W, CW, R, GR, KC, F):
        self.B, self.TX, self.TC, self.D = B, TX, TC, D
        self.W, self.CW, self.R, self.GR, self.KC, self.F = W, CW, R, GR, KC, F
        self.T = TX + TC
        self.TT = min(256, TC)
        assert TC % self.TT == 0 and TX % self.TT == 0 and self.TT % LINE == 0
        self.JC = TC // self.TT
        self.JX = TX // self.TT
        self.J = self.JC + self.JX
        self.HP = W // LANES
        self.G = B * self.HP
        self.PW = _round_up(2 * R, LANES)
        self.GP = _round_up(GR, LANES)
        self.SP = 3 * W + 2 * self.PW + self.GP
        self.CP = self.SP + 2 * CW
        self.KP = _round_up(KC, SUBLANES)
        assert self.T % SCAN_CHUNK == 0 and TC % SCAN_CHUNK == 0
        self.NCH = self.T // SCAN_CHUNK
        self.NCC = TC // SCAN_CHUNK
        W_, R_ = W, R
        segs = [(0, 3 * W_, 0),
                (3 * W_, 2 * R_, 3 * W_),
                (3 * W_ + 2 * R_, 2 * R_, 3 * W_ + self.PW),
                (3 * W_ + 4 * R_, GR, 3 * W_ + 2 * self.PW),
                (3 * W_ + 4 * R_ + GR, 2 * CW, self.SP)]
        self.col_segs = segs
        self.shift_cols = 3 * W_ + 4 * R_ + GR
        self.in_cols = self.shift_cols + 2 * CW


def _pad_cols(a, cfg, upto_shift=False):
    width = cfg.SP if upto_shift else cfg.CP
    pieces, pos = [], 0
    for src, n, dst in cfg.col_segs:
        if upto_shift and dst >= cfg.SP:
            break
        if dst > pos:
            pieces.append(jnp.zeros(a.shape[:-1] + (dst - pos,), a.dtype))
        pieces.append(a[..., src:src + n])
        pos = dst + n
    if width > pos:
        pieces.append(jnp.zeros(a.shape[:-1] + (width - pos,), a.dtype))
    return jnp.concatenate(pieces, axis=-1)


def _unpad_cols(a, cfg, upto_shift=False):
    pieces = []
    for src, n, dst in cfg.col_segs:
        if upto_shift and dst >= cfg.SP:
            break
        pieces.append(a[..., dst:dst + n])
    return jnp.concatenate(pieces, axis=-1)


def _pair_weight(w2, cfg):
    R, W = cfg.R, cfg.W
    out = jnp.zeros((cfg.PW, 2 * W), w2.dtype)
    out = out.at[0:R, 0:W].set(w2[0])
    out = out.at[R:2 * R, W:2 * W].set(w2[1])
    return out


def _unpair_weight(g, cfg):
    R, W = cfg.R, cfg.W
    return jnp.stack([g[0:R, 0:W], g[R:2 * R, W:2 * W]])


def _row_ids(n):
    return lax.broadcasted_iota(jnp.int32, (n, 1), 0)


def _shift_rows(z, prev_row, next_row):
    n = z.shape[0]
    rows = _row_ids(n)
    zp = jnp.where(rows == 0, prev_row, pltpu.roll(z, 1, 0))
    zn = jnp.where(rows == n - 1, next_row, pltpu.roll(z, n - 1, 0))
    return zp, zn


def _line_shift(u, d):
    if d == 0:
        return u
    n = u.shape[0]
    lt = _row_ids(n) % LINE
    ok = jnp.logical_and(lt + d >= 0, lt + d < LINE)
    return jnp.where(ok, pltpu.roll(u, (-d) % n, 0), 0.0)


def _conv_tables(cw, kc):
    pad = kc // 2
    t = np.arange(LINE)[None, :]
    d = (np.arange(kc) - pad)[:, None]
    fwd = ((t + d >= 0) & (t + d < LINE)).astype(np.float32)
    bwd = ((t - d >= 0) & (t - d < LINE)).astype(np.float32)
    w = cw[:kc, None, :]
    return jnp.asarray(fwd)[:, :, None] * w, jnp.asarray(bwd)[:, :, None] * w


def _conv_lines(src_ref, wm_ref, dst_ref, kc, transpose):
    pad = kc // 2
    n, width = src_ref.shape
    for l in range(n // LINE):
        for b in range(width // LANES):
            rs, cs = slice(l * LINE, (l + 1) * LINE), slice(b * LANES, (b + 1) * LANES)
            tile = src_ref[rs, cs]
            acc = jnp.zeros_like(tile)
            for i in range(kc):
                d = (pad - i) if transpose else (i - pad)
                acc = acc + pltpu.roll(tile, (-d) % LINE, 0) * wm_ref[i, :, cs]
            dst_ref[rs, cs] = acc


def _rwkv_prep(rw, w0, w2p, a0, a2p, k_k, k_a, e, cfg):
    W, PW = cfg.W, cfg.PW
    r = rw[:, 0:W]
    k = rw[:, W:2 * W]
    v = rw[:, 2 * W:3 * W]
    wdp = rw[:, 3 * W:3 * W + PW]
    adp = rw[:, 3 * W + PW:3 * W + 2 * PW]
    wl = w0 + _mm(jnp.tanh(wdp), w2p)
    w_log = -_softplus(-wl) - 0.5
    decay = jnp.exp(-jnp.exp(w_log))
    iclr = _sigmoid(a0 + _mm(adp, a2p))
    kkr = k * k_k
    nrm = jnp.sqrt(_seg_sum(kkr * kkr, e))
    kk = kkr / jnp.maximum(nrm, 1e-12)
    outs = [r, v, kk]
    for d in range(2):
        ic = iclr[:, d * W:(d + 1) * W]
        outs += [decay[:, d * W:(d + 1) * W], k * (1.0 + (ic - 1.0) * k_a), kk * ic]
    return tuple(outs)


def _glu(cv, cfg):
    return cv[:, :cfg.CW] * _sigmoid(cv[:, cfg.CW:])


def _conv_post(y, cb, lw, lb):
    yf = y + cb
    mu = jnp.mean(yf, axis=-1, keepdims=True)
    var = jnp.mean(jnp.square(yf - mu), axis=-1, keepdims=True)
    return _silu((yf - mu) * lax.rsqrt(var + EPS_LN) * lw + lb)


def _readout(y, kbar, r, v, gd, r_k, gw2, lnx_w, lnx_b, e):
    inv = 1.0 / HEAD
    mu = _seg_sum(y, e) * inv
    yc = y - mu
    var = _seg_sum(yc * yc, e) * inv
    yn = yc * lax.rsqrt(var + EPS_GN) * lnx_w + lnx_b
    bonus = _seg_sum(r * kbar * r_k, e) * v
    g = _mm(_sigmoid(gd), gw2)
    return (yn + bonus) * g


def _post_res(xv, mix, gate, g):
    return xv + gate * _rms(mix, g)


def _head_spec(cfg, tmap):
    return pl.BlockSpec((1, cfg.HP, cfg.TT, LANES), lambda b, j: (b, 0, tmap(j), 0))


def _full_spec(shape):
    n = len(shape)
    return pl.BlockSpec(shape, lambda *_: (0,) * n)


def _to_heads(ref, val, cfg):
    for hp in range(cfg.HP):
        ref[0, hp] = val[:, hp * LANES:(hp + 1) * LANES]


def _from_heads(ref, cfg):
    return jnp.concatenate([ref[0, hp] for hp in range(cfg.HP)], axis=-1)


def _in_proj(xcat, modt, g1, w_in_p, cfg):
    B, T, D, TT, CP = cfg.B, cfg.T, cfg.D, cfg.TT, cfg.CP

    def body(x_ref, mod_ref, g_ref, w_ref, p_ref, h_ref):
        h = _rms_mod(x_ref[0], g_ref[...], mod_ref[0, 0, 0:1, :], mod_ref[0, 0, 1:2, :])
        hb = h.astype(BF16)
        h_ref[0] = hb
        p_ref[0] = jnp.dot(hb, w_ref[...], preferred_element_type=F32)

    return pl.pallas_call(
        body, name="in_proj", grid=(B, cfg.J),
        in_specs=[pl.BlockSpec((1, TT, D), lambda b, j: (b, j, 0)),
                  pl.BlockSpec((1, 1, 2, D), lambda b, j: (b, j, 0, 0)),
                  _full_spec((1, D)), _full_spec((D, CP))],
        out_specs=[pl.BlockSpec((1, TT, CP), lambda b, j: (b, j, 0)),
                   pl.BlockSpec((1, TT, D), lambda b, j: (b, j, 0))],
        out_shape=[jax.ShapeDtypeStruct((B, T, CP), F32), jax.ShapeDtypeStruct((B, T, D), BF16)],
        compiler_params=_params(("parallel", "parallel")),
    )(xcat, modt, g1, w_in_p)


def _halo_specs(cfg, width):
    per = cfg.TT // SUBLANES
    last = cfg.T // SUBLANES - 1
    prev = pl.BlockSpec((1, SUBLANES, width), lambda b, j: (b, jnp.maximum(j * per - 1, 0), 0))
    nxt = pl.BlockSpec((1, SUBLANES, width), lambda b, j: (b, jnp.minimum((j + 1) * per, last), 0))
    return prev, nxt


def _halo_flags(j, cfg):
    has_prev = jnp.logical_and(j != 0, j != cfg.JC).astype(F32)
    has_next = jnp.logical_and(j != cfg.JC - 1, j != cfg.J - 1).astype(F32)
    return has_prev, has_next


def _shifted(p_ref, prev_ref, next_ref, mup, mun, j, cfg):
    SP = cfg.SP
    has_prev, has_next = _halo_flags(j, cfg)
    z = p_ref[0][:, :SP]
    zp, zn = _shift_rows(z, prev_ref[0, SUBLANES - 1:SUBLANES, :] * has_prev, next_ref[0, 0:1, :] * has_next)
    return z, zp, zn, z + mup * (zp - z) + mun * (zn - z)


def _mix_prep(p, mup, mun, w0, w2p, a0, a2p, k_k, k_a, wm, cb, clw, clb, e_w, late, cfg):
    B, T, TT, SP, CP, W, CW, HP, JC = cfg.B, cfg.T, cfg.TT, cfg.SP, cfg.CP, cfg.W, cfg.CW, cfg.HP, cfg.JC
    nl = len(late)
    steps = B * cfg.J

    def body(p_ref, prev_ref, next_ref, mup_ref, mun_ref, w0_ref, w2_ref, a0_ref, a2_ref, kk_ref, ka_ref,
             wm_ref, cb_ref, clw_ref, clb_ref, e_ref, *rest):
        late_in, rest = rest[:nl], rest[nl:]
        outs, late_out, u_ref, sems = rest[:12], rest[12:12 + nl], rest[12 + nl], rest[13 + nl:]
        j = pl.program_id(1)
        step = pl.program_id(0) * cfg.J + j
        for phase, at in enumerate((0, steps // 2)):
            @pl.when(step == at)
            def _(phase=phase):
                _gather_phase(phase, late_in, late_out, *sems)
        _, _, _, rw = _shifted(p_ref, prev_ref, next_ref, mup_ref[...], mun_ref[...], j, cfg)
        vals = _rwkv_prep(rw, w0_ref[...], w2_ref[...], a0_ref[...], a2_ref[...], kk_ref[...], ka_ref[...],
                          e_ref[...], cfg)
        for ref, val in zip(outs[:9], vals):
            _to_heads(ref, val, cfg)
        outs[9][0] = rw[:, 3 * W + 2 * cfg.PW:SP]

        @pl.when(j >= JC)
        def _():
            u_ref[...] = _glu(p_ref[0, :, SP:], cfg)
            _conv_lines(u_ref, wm_ref, outs[11].at[0], cfg.KC, False)
            outs[10][0] = _conv_post(outs[11][0], cb_ref[...], clw_ref[...], clb_ref[...])

        @pl.when(step == steps - 1)
        def _():
            _gather_phase(2, late_in, late_out, *sems)

    prev, nxt = _halo_specs(cfg, SP)
    head = jax.ShapeDtypeStruct((B, HP, T, LANES), F32)
    hbm = pl.BlockSpec(memory_space=pltpu.HBM)
    tile = lambda n: pl.BlockSpec((1, TT, n), lambda b, j: (b, j, 0))
    return pl.pallas_call(
        body, name="mix_prep", grid=(B, cfg.J),
        in_specs=[tile(CP), prev, nxt,
                  _full_spec((1, SP)), _full_spec((1, SP)),
                  _full_spec((1, 2 * W)), _full_spec((cfg.PW, 2 * W)),
                  _full_spec((1, 2 * W)), _full_spec((cfg.PW, 2 * W)),
                  _full_spec((1, W)), _full_spec((1, W)),
                  pl.BlockSpec((cfg.KC, LINE, CW), lambda b, j: (0, 0, 0), pipeline_mode=pl.Buffered(1)),
                  _full_spec((1, CW)), _full_spec((1, CW)), _full_spec((1, CW)),
                  _full_spec(e_w.shape)] + [hbm] * nl,
        out_specs=[_head_spec(cfg, lambda j: j)] * 9 + [tile(cfg.GP), tile(CW), tile(CW)] + [hbm] * nl,
        out_shape=[head] * 9 + [jax.ShapeDtypeStruct((B, T, cfg.GP), F32),
                                jax.ShapeDtypeStruct((B, T, CW), F32), jax.ShapeDtypeStruct((B, T, CW), F32)]
                  + _exchange_shapes(late, [False] * nl),
        scratch_shapes=[pltpu.VMEM((TT, CW), F32)] + _exchange_sems(nl),
        compiler_params=_params(("arbitrary", "arbitrary")),
    )(p, p, p, mup, mun, w0, w2p, a0, a2p, k_k, k_a, wm, cb, clw, clb, e_w, *late)


def _chunk_pos(c, reverse, cfg):
    if not reverse:
        return c
    return jnp.where(c < cfg.NCC, cfg.NCC - 1 - c, cfg.NCH - 1 + cfg.NCC - c)


def _diag_mask():
    r = lax.broadcasted_iota(jnp.int32, (HEAD, LANES), 0)
    l = lax.broadcasted_iota(jnp.int32, (HEAD, LANES), 1)
    return (r == l % HEAD).astype(F32)


def _col_lhs(row, diag_b):
    hi = row.astype(BF16)
    lo = (row - hi.astype(F32)).astype(BF16)
    return diag_b * hi, diag_b * lo


def _col_dot(row_list, diag_b, e2):
    n, g = len(row_list), row_list[0].shape[0]
    lhs = jnp.concatenate([jnp.concatenate(_col_lhs(r, diag_b), axis=-1) for r in row_list], axis=0)
    out = jnp.dot(lhs.reshape(n * g * HEAD, 2 * LANES), e2, preferred_element_type=F32)
    return out.reshape(n, g, HEAD, LANES)


def _col_form(row):
    n = row.shape[0]
    t = jnp.swapaxes(jnp.broadcast_to(row, (n, LANES, LANES)), 1, 2)
    lane = lax.broadcasted_iota(jnp.int32, (HEAD, LANES), 1)
    return jnp.where(lane < HEAD, t[:, :HEAD, :], t[:, HEAD:, :])


def _col_both(row, diag_b, e2):
    half = row.shape[0] // 2
    return jnp.concatenate([_col_form(row[:half]), _col_dot([row[half:]], diag_b, e2)[0]], axis=0)


def _both_rows(ins, idx, i):
    return jnp.concatenate([ins[d][idx][:, pl.ds(_tok(i, d == 1), 1), :] for d in range(2)], axis=0)


def _seg_dot(blocks, e):
    n, g = len(blocks), blocks[0].shape[0]
    lhs = jnp.concatenate(blocks, axis=0).reshape(n * g * HEAD, LANES)
    return jnp.dot(lhs, e, preferred_element_type=F32).reshape(n, g, HEAD, LANES)


def _tok(i, reverse):
    return (SCAN_CHUNK - 1 - i) if reverse else i


def _scan_fwd(ops_f, ops_b, e128, e256, cfg):
    G, T, NCH = cfg.G, cfg.T, cfg.NCH
    CH = SCAN_CHUNK
    G2 = 2 * G

    def body(*refs):
        ins = (refs[0:6], refs[6:12])
        e_ref, e2_ref = refs[12], refs[13]
        ys, hist_ref, fin_ref = (refs[14], refs[15]), refs[16], refs[17]
        s_ref, mm_ref = refs[18], refs[19]
        c = pl.program_id(0)

        @pl.when(c == 0)
        def _():
            s_ref[...] = jnp.zeros_like(s_ref)

        diag = _diag_mask()
        diag_b = diag.astype(BF16)
        e, e2 = e_ref[...], e2_ref[...]
        rows = functools.partial(_both_rows, ins)

        mm_ref[0] = _seg_dot([s_ref[...].astype(BF16) * (-rows(4, 0)).astype(BF16)], e)[0]
        mm_ref[1] = _col_both(rows(3, 0), diag_b, e2)

        def step(i, carry):
            nxt = jnp.minimum(i + 1, CH - 1)
            res = []
            for d in range(2):
                sl = slice(d * G, (d + 1) * G)
                row = lambda idx, ii: ins[d][idx][:, pl.ds(_tok(ii, d == 1), 1), :]
                s_old = s_ref[sl]
                hist_ref[i, sl] = s_old
                S = s_old * row(1, i) + mm_ref[0, sl] * row(5, i) + mm_ref[1, sl] * row(2, i)
                s_ref[sl] = S
                sb = S.astype(BF16)
                res.append(_seg_dot([sb * (-row(4, nxt)).astype(BF16), sb * row(0, i).astype(BF16)], e))
            for d in range(2):
                sl = slice(d * G, (d + 1) * G)
                v_next = ins[d][3][:, pl.ds(_tok(nxt, d == 1), 1), :]
                mm_ref[0, sl] = res[d][0]
                mm_ref[1, sl] = _col_form(v_next) if d == 0 else _col_dot([v_next], diag_b, e2)[0]
                ys[d][:, pl.ds(_tok(i, d == 1), 1), :] = jnp.sum(diag * res[d][1], axis=1, keepdims=True)
            return carry

        lax.fori_loop(0, CH, step, 0)
        fin_ref[...] = s_ref[...]

    toks = [pl.BlockSpec((G, CH, LANES), lambda c, rev=rev: (0, _chunk_pos(c, rev, cfg), 0)) for rev in (False, True)]
    y_shape = jax.ShapeDtypeStruct((G, T, LANES), F32)
    return pl.pallas_call(
        body, name="scan_fwd", grid=(NCH,),
        in_specs=[toks[0]] * 6 + [toks[1]] * 6 + [_full_spec((LANES, LANES)), _full_spec((2 * LANES, LANES))],
        out_specs=[toks[0], toks[1], pl.BlockSpec((CH, G2, HEAD, LANES), lambda c: (c, 0, 0, 0)),
                   _full_spec((G2, HEAD, LANES))],
        out_shape=[y_shape, y_shape, jax.ShapeDtypeStruct(((NCH + 1) * CH, G2, HEAD, LANES), F32),
                   jax.ShapeDtypeStruct((G2, HEAD, LANES), F32)],
        scratch_shapes=[pltpu.VMEM((G2, HEAD, LANES), F32), pltpu.VMEM((2, G2, HEAD, LANES), F32)],
        compiler_params=_params(("arbitrary",)),
    )(*ops_f, *ops_b, e128, e256)


def _mix_out(yf, yb, kdf, kdb, r, v, gd, conv, x, mod2, r_k, gw2p, lnx_w, lnx_b, w_out, g2, e_w, cfg):
    B, TX, D, TT, W, CW, JC = cfg.B, cfg.TX, cfg.D, cfg.TT, cfg.W, cfg.CW, cfg.JC

    def body(yf_ref, yb_ref, kdf_ref, kdb_ref, r_ref, v_ref, gd_ref, cv_ref, x_ref, mod_ref,
             rk_ref, gw_ref, lw_ref, lb_ref, wo_ref, g_ref, e_ref, x1_ref):
        y = _from_heads(yf_ref, cfg) + _from_heads(yb_ref, cfg)
        kbar = 0.5 * (_from_heads(kdf_ref, cfg) + _from_heads(kdb_ref, cfg))
        ro = _readout(y, kbar, _from_heads(r_ref, cfg), _from_heads(v_ref, cfg), gd_ref[0], rk_ref[...],
                      gw_ref[...], lw_ref[...], lb_ref[...], e_ref[...])
        cat = jnp.concatenate([ro, cv_ref[0]], axis=-1)
        mix = _bdot(cat, wo_ref[...])
        x1_ref[0] = _post_res(x_ref[0], mix, mod_ref[0], g_ref[...])

    hs = _head_spec(cfg, lambda j: j + JC)
    lat = lambda n: pl.BlockSpec((1, TT, n), lambda b, j: (b, j + JC, 0))
    return pl.pallas_call(
        body, name="mix_out", grid=(B, cfg.JX),
        in_specs=[hs] * 6 + [lat(cfg.GP), lat(CW),
                             pl.BlockSpec((1, TT, D), lambda b, j: (b, j, 0)),
                             pl.BlockSpec((1, 1, D), lambda b, j: (b, 0, 0)),
                             _full_spec((1, W)), _full_spec((cfg.GP, W)), _full_spec((1, W)), _full_spec((1, W)),
                             _full_spec((W + CW, D)), _full_spec((1, D)), _full_spec(e_w.shape)],
        out_specs=pl.BlockSpec((1, TT, D), lambda b, j: (b, j, 0)),
        out_shape=jax.ShapeDtypeStruct((B, TX, D), F32),
        compiler_params=_params(("parallel", "parallel")),
    )(yf, yb, kdf, kdb, r, v, gd, conv, x, mod2, r_k, gw2p, lnx_w, lnx_b, w_out, g2, e_w)


def _acc(ref, val, first):
    @pl.when(first)
    def _():
        ref[...] = val

    @pl.when(jnp.logical_not(first))
    def _():
        ref[...] += val


def _mlp_fwd_bwd(x1, tgt, mod345, g3, g4, w1, w2, cfg):
    B, TX, D, TT, F, JX = cfg.B, cfg.TX, cfg.D, cfg.TT, cfg.F, cfg.JX

    def body(x1_ref, t_ref, mod_ref, g3_ref, g4_ref, w1_ref, w2_ref,
             dx1_ref, loss_ref, h2_ref, dpre_ref, act_ref, dff_ref, dmod_ref, dg3_ref, dg4_ref):
        b, j = pl.program_id(0), pl.program_id(1)
        x1v = x1_ref[0]
        sh, sc, gt = mod_ref[0, 0:1, :], mod_ref[0, 1:2, :], mod_ref[0, 2:3, :]
        h2, vjp_pre = jax.vjp(_rms_mod, x1v, g3_ref[...], sh, sc)
        h2b = h2.astype(BF16)
        pre = jnp.dot(h2b, w1_ref[...], preferred_element_type=F32)
        rl = jnp.maximum(pre, 0.0)
        actb = (rl * rl).astype(BF16)
        ff = jnp.dot(actb, w2_ref[...], preferred_element_type=F32)
        x2, vjp_post = jax.vjp(_post_res, x1v, ff, gt, g4_ref[...])
        err = x2 - t_ref[0]
        loss = 0.5 * jnp.sum(jnp.mean(err * err, axis=-1, keepdims=True))
        dx1a, dff, dgt, dg4 = vjp_post(err * (1.0 / D))
        dffb = dff.astype(BF16)
        dpre = _bdot_nt(dffb, w2_ref[...]) * (2.0 * rl)
        dpreb = dpre.astype(BF16)
        dx1b, dg3, dsh, dsc = vjp_pre(_bdot_nt(dpreb, w1_ref[...]))
        dx1_ref[0] = dx1a + dx1b
        loss_ref[0, 0] = jnp.zeros((SUBLANES, LANES), F32) + loss
        h2_ref[0] = h2b
        dpre_ref[0] = dpreb
        act_ref[0] = actb
        dff_ref[0] = dffb
        _acc(dmod_ref, jnp.concatenate([dsh, dsc, dgt], axis=0)[None], j == 0)
        first = jnp.logical_and(b == 0, j == 0)
        _acc(dg3_ref, dg3, first)
        _acc(dg4_ref, dg4, first)

    tile = lambda n: pl.BlockSpec((1, TT, n), lambda b, j: (b, j, 0))
    return pl.pallas_call(
        body, name="mlp_fwd_bwd", grid=(B, JX),
        in_specs=[tile(D), tile(D), pl.BlockSpec((1, 3, D), lambda b, j: (b, 0, 0)),
                  _full_spec((1, D)), _full_spec((1, D)),
                  pl.BlockSpec((D, F), lambda b, j: (0, 0), pipeline_mode=pl.Buffered(1)),
                  pl.BlockSpec((F, D), lambda b, j: (0, 0), pipeline_mode=pl.Buffered(1))],
        out_specs=[tile(D), pl.BlockSpec((1, 1, SUBLANES, LANES), lambda b, j: (b, j, 0, 0)),
                   tile(D), tile(F), tile(F), tile(D),
                   pl.BlockSpec((1, 3, D), lambda b, j: (b, 0, 0)),
                   _full_spec((1, D)), _full_spec((1, D))],
        out_shape=[jax.ShapeDtypeStruct((B, TX, D), F32),
                   jax.ShapeDtypeStruct((B, JX, SUBLANES, LANES), F32),
                   jax.ShapeDtypeStruct((B, TX, D), BF16), jax.ShapeDtypeStruct((B, TX, F), BF16),
                   jax.ShapeDtypeStruct((B, TX, F), BF16), jax.ShapeDtypeStruct((B, TX, D), BF16),
                   jax.ShapeDtypeStruct((B, 3, D), F32),
                   jax.ShapeDtypeStruct((1, D), F32), jax.ShapeDtypeStruct((1, D), F32)],
        compiler_params=_params(("arbitrary", "arbitrary")),
    )(x1, tgt, mod345, g3, g4, w1, w2)


def _mix_out_bwd(yf, yb, kdf, kdb, r, v, gd, conv, x, mod2, r_k, gw2p, lnx_w, lnx_b, w_out, g2, e_w, dx1, cfg):
    B, TX, D, TT, W, CW, JC, HP, GP = cfg.B, cfg.TX, cfg.D, cfg.TT, cfg.W, cfg.CW, cfg.JC, cfg.HP, cfg.GP

    def body(yf_ref, yb_ref, kdf_ref, kdb_ref, r_ref, v_ref, gd_ref, cv_ref, x_ref, mod_ref,
             rk_ref, gw_ref, lw_ref, lb_ref, wo_ref, g_ref, e_ref, dx1_ref,
             dy_ref, dkb_ref, dr_ref, dv_ref, dgd_ref, dcv_ref, cat_ref, dmix_ref,
             dmod_ref, dg2_ref, drk_ref, dgw_ref, dlw_ref, dlb_ref):
        b, j = pl.program_id(0), pl.program_id(1)
        e = e_ref[...]
        y = _from_heads(yf_ref, cfg) + _from_heads(yb_ref, cfg)
        kbar = 0.5 * (_from_heads(kdf_ref, cfg) + _from_heads(kdb_ref, cfg))
        ro, vjp_ro = jax.vjp(lambda *a: _readout(*a, e), y, kbar, _from_heads(r_ref, cfg),
                             _from_heads(v_ref, cfg), gd_ref[0], rk_ref[...], gw_ref[...], lw_ref[...], lb_ref[...])
        catb = jnp.concatenate([ro, cv_ref[0]], axis=-1).astype(BF16)
        mix = jnp.dot(catb, wo_ref[...], preferred_element_type=F32)
        _, vjp_post = jax.vjp(_post_res, x_ref[0], mix, mod_ref[0], g_ref[...])
        _, dmix, dgate, dg2 = vjp_post(dx1_ref[0])
        dmixb = dmix.astype(BF16)
        dcat = _bdot_nt(dmixb, wo_ref[...])
        dy, dkb, dr, dv, dgd, drk, dgw, dlw, dlb = vjp_ro(dcat[:, :W])
        _to_heads(dy_ref, dy, cfg)
        _to_heads(dkb_ref, dkb, cfg)
        _to_heads(dr_ref, dr, cfg)
        _to_heads(dv_ref, dv, cfg)
        dgd_ref[0] = dgd
        dcv_ref[0] = dcat[:, W:]
        cat_ref[0] = catb
        dmix_ref[0] = dmixb
        _acc(dmod_ref, dgate[None], j == 0)
        first = jnp.logical_and(b == 0, j == 0)
        _acc(dg2_ref, dg2, first)
        _acc(drk_ref, drk, first)
        _acc(dgw_ref, dgw, first)
        _acc(dlw_ref, dlw, first)
        _acc(dlb_ref, dlb, first)

    hs = _head_spec(cfg, lambda j: j + JC)
    ho = _head_spec(cfg, lambda j: j)
    lat = lambda n: pl.BlockSpec((1, TT, n), lambda b, j: (b, j + JC, 0))
    tile = lambda n: pl.BlockSpec((1, TT, n), lambda b, j: (b, j, 0))
    head = jax.ShapeDtypeStruct((B, HP, TX, LANES), F32)
    vec = lambda n: jax.ShapeDtypeStruct((1, n), F32)
    return pl.pallas_call(
        body, name="mix_out_bwd", grid=(B, cfg.JX),
        in_specs=[hs] * 6 + [lat(GP), lat(CW), tile(D),
                             pl.BlockSpec((1, 1, D), lambda b, j: (b, 0, 0)),
                             _full_spec((1, W)), _full_spec((GP, W)), _full_spec((1, W)), _full_spec((1, W)),
                             _full_spec((W + CW, D)), _full_spec((1, D)), _full_spec(e_w.shape), tile(D)],
        out_specs=[ho] * 4 + [tile(GP), tile(CW), tile(W + CW), tile(D),
                              pl.BlockSpec((1, 1, D), lambda b, j: (b, 0, 0)),
                              _full_spec((1, D)), _full_spec((1, W)), _full_spec((GP, W)),
                              _full_spec((1, W)), _full_spec((1, W))],
        out_shape=[head] * 4 + [jax.ShapeDtypeStruct((B, TX, GP), F32), jax.ShapeDtypeStruct((B, TX, CW), F32),
                                jax.ShapeDtypeStruct((B, TX, W + CW), BF16), jax.ShapeDtypeStruct((B, TX, D), BF16),
                                jax.ShapeDtypeStruct((B, 1, D), F32),
                                vec(D), vec(W), jax.ShapeDtypeStruct((GP, W), F32), vec(W), vec(W)],
        compiler_params=_params(("arbitrary", "arbitrary")),
    )(yf, yb, kdf, kdb, r, v, gd, conv, x, mod2, r_k, gw2p, lnx_w, lnx_b, w_out, g2, e_w, dx1)


def _scan_bwd(ops_f, ops_b, dy, hist, e128, e256, cfg):
    G, T, NCH, NCC = cfg.G, cfg.T, cfg.NCH, cfg.NCC
    CH = SCAN_CHUNK
    G2 = 2 * G

    def body(*refs):
        ins = (refs[0:6], refs[6:12])
        dys, hist_ref, next_ref, e_ref, e2_ref = (refs[12], refs[13]), refs[14], refs[15], refs[16], refs[17]
        outs = (refs[18:24], refs[24:30])
        ds_ref, mm_ref = refs[30], refs[31]
        gi = pl.program_id(0)

        @pl.when(gi == 0)
        def _():
            ds_ref[...] = jnp.zeros_like(ds_ref)

        diag = _diag_mask()
        diag_b = diag.astype(BF16)
        e, e2 = e_ref[...], e2_ref[...]
        rows = functools.partial(_both_rows, ins)
        latent = [(_chunk_pos(NCH - 1 - gi, d == 1, cfg) >= NCC).astype(F32) for d in range(2)]

        def dy_rows(i):
            return jnp.concatenate([dys[d][:, pl.ds(_tok(i, d == 1), 1), :] * latent[d] for d in range(2)], axis=0)

        def put(idx, i, val):
            outs[0][idx][:, pl.ds(_tok(i, False), 1), :] = val[:G]
            outs[1][idx][:, pl.ds(_tok(i, True), 1), :] = val[G:]

        rsum = lambda z: jnp.sum(z, axis=1, keepdims=True)

        def prepare(i):
            return (_col_form(dy_rows(i)), _col_dot([rows(3, i)], diag_b, e2)[0],
                    hist_ref[i].astype(BF16) * (-rows(4, i)).astype(BF16))

        dyc0, vb0, sa_lhs = prepare(CH - 1)
        mm_ref[0] = dyc0
        mm_ref[1] = vb0
        mm_ref[2] = _seg_dot([sa_lhs], e)[0]

        def one_step(i, s_after):
            prv = jnp.maximum(i - 1, 0)
            sp, dyc = hist_ref[i], mm_ref[0]
            ds = ds_ref[...] + dyc * rows(0, i)
            put(0, i, rsum(s_after * dyc))
            put(1, i, rsum(ds * sp))
            put(5, i, rsum(ds * mm_ref[2]))
            put(2, i, rsum(ds * mm_ref[1]))
            dyc_n, vb_n, sa_lhs_n = prepare(prv)
            dsb = ds.astype(BF16)
            res = _seg_dot([dsb * rows(5, i).astype(BF16), dsb * rows(2, i).astype(BF16), sa_lhs_n], e)
            dsa = res[0]
            put(4, i, -rsum(sp * dsa))
            put(3, i, rsum(diag * res[1]))
            mm_ref[0] = dyc_n
            mm_ref[1] = vb_n
            mm_ref[2] = res[2]
            ds_ref[...] = ds * rows(1, i) - dsa * rows(4, i)

        one_step(CH - 1, next_ref[0])

        def bstep(ii, carry):
            i = CH - 1 - ii
            one_step(i, hist_ref[i + 1])
            return carry

        lax.fori_loop(1, CH, bstep, 0)

    cpos = lambda g, rev: _chunk_pos(NCH - 1 - g, rev, cfg)
    toks = [pl.BlockSpec((G, CH, LANES), lambda g, rev=rev: (0, cpos(g, rev), 0)) for rev in (False, True)]
    dy_specs = [pl.BlockSpec((G, CH, LANES), lambda g, rev=rev: (0, jnp.maximum(cpos(g, rev) - NCC, 0), 0))
                for rev in (False, True)]
    out = jax.ShapeDtypeStruct((G, T, LANES), F32)
    res = pl.pallas_call(
        body, name="scan_bwd", grid=(NCH,),
        in_specs=[toks[0]] * 6 + [toks[1]] * 6 + dy_specs
                 + [pl.BlockSpec((CH, G2, HEAD, LANES), lambda g: (NCH - 1 - g, 0, 0, 0)),
                    pl.BlockSpec((1, G2, HEAD, LANES), lambda g: ((NCH - g) * CH, 0, 0, 0)),
                    _full_spec((LANES, LANES)), _full_spec((2 * LANES, LANES))],
        out_specs=[toks[0]] * 6 + [toks[1]] * 6,
        out_shape=[out] * 12,
        scratch_shapes=[pltpu.VMEM((G2, HEAD, LANES), F32), pltpu.VMEM((3, G2, HEAD, LANES), F32)],
        compiler_params=_params(("arbitrary",)),
    )(*ops_f, *ops_b, dy, dy, hist, hist, e128, e256)
    return res[:6], res[6:]


def _mix_prep_bwd(p, mup, mun, w0, w2p, a0, a2p, k_k, k_a, wm, cb, clw, clb, e_w, yconv, sf, sb, ro, xch, cfg):
    B, T, TT, SP, CP, W, CW, HP, JC, PW, GP, KC, KP = (cfg.B, cfg.T, cfg.TT, cfg.SP, cfg.CP, cfg.W, cfg.CW,
                                                       cfg.HP, cfg.JC, cfg.PW, cfg.GP, cfg.KC, cfg.KP)
    pad = KC // 2
    nx = len(xch)

    def body(p_ref, prev_ref, next_ref, mup_ref, mun_ref, w0_ref, w2_ref, a0_ref, a2_ref, kk_ref, ka_ref,
             wm_ref, cb_ref, clw_ref, clb_ref, e_ref, yc_ref, *rest):
        sf_refs, sb_refs = rest[0:6], rest[6:12]
        rdr_ref, rdv_ref, rdkb_ref, rdgd_ref, rdcv_ref = rest[12:17]
        xin, rest = rest[17:17 + nx], rest[17 + nx:]
        (dpz_ref, dmup_ref, dmun_ref, dw0_ref, dw2_ref, da0_ref, da2_ref, dkk_ref, dka_ref,
         dcw_ref, dcb_ref, dclw_ref, dclb_ref) = rest[:13]
        xout, (dyc_ref, du_ref), sems = rest[13:13 + nx], rest[13 + nx:15 + nx], rest[15 + nx:]
        b, j = pl.program_id(0), pl.program_id(1)
        first = jnp.logical_and(b == 0, j == 0)

        @pl.when(first)
        def _():
            _exchange_start(_exchange_copies(xin, xout, *sems, [True] * nx))
        lat = (j >= JC).astype(F32)
        e = e_ref[...]
        mup_v, mun_v = mup_ref[...], mun_ref[...]
        z, zp, zn, rw = _shifted(p_ref, prev_ref, next_ref, mup_v, mun_v, j, cfg)

        def prep(rw_, w0_, w2_, a0_, a2_, kk_, ka_):
            return _rwkv_prep(rw_, w0_, w2_, a0_, a2_, kk_, ka_, e, cfg) + (rw_[:, 3 * W + 2 * PW:SP],)

        _, vjp_prep = jax.vjp(prep, rw, w0_ref[...], w2_ref[...], a0_ref[...], a2_ref[...], kk_ref[...], ka_ref[...])
        fr, fw, fk, fv, fkk, fb = [_from_heads(r_, cfg) for r_ in sf_refs]
        br, bw, bk, bv, bkk, bb = [_from_heads(r_, cfg) for r_ in sb_refs]
        half_kb = (0.5 * lat) * _from_heads(rdkb_ref, cfg)
        cots = (fr + br + lat * _from_heads(rdr_ref, cfg), fv + bv + lat * _from_heads(rdv_ref, cfg), fkk + bkk,
                fw, fk + half_kb, fb, bw, bk + half_kb, bb, lat * rdgd_ref[0])
        drw, dw0, dw2, da0, da2, dkk, dka = vjp_prep(cots)
        _acc(dmup_ref, jnp.sum(drw * (zp - z), axis=0, keepdims=True), first)
        _acc(dmun_ref, jnp.sum(drw * (zn - z), axis=0, keepdims=True), first)
        for ref, val in ((dw0_ref, dw0), (dw2_ref, dw2), (da0_ref, da0), (da2_ref, da2), (dkk_ref, dkk), (dka_ref, dka)):
            _acc(ref, val, first)

        dpz_ref[0, :, 0:SP] = drw

        @pl.when(first)
        def _():
            for ref in (dcw_ref, dcb_ref, dclw_ref, dclb_ref):
                ref[...] = jnp.zeros_like(ref)

        @pl.when(j < JC)
        def _():
            dpz_ref[0, :, SP:] = jnp.zeros((TT, 2 * CW), F32)

        @pl.when(j >= JC)
        def _():
            u, vjp_glu = jax.vjp(lambda c_: _glu(c_, cfg), p_ref[0, :, SP:])
            _, vjp_post = jax.vjp(_conv_post, yc_ref[0], cb_ref[...], clw_ref[...], clb_ref[...])
            dyc, dcb, dclw, dclb = vjp_post(rdcv_ref[0])
            dyc_ref[...] = dyc
            _conv_lines(dyc_ref, wm_ref, du_ref, KC, True)
            (dcv,) = vjp_glu(du_ref[...])
            dpz_ref[0, :, SP:] = dcv
            for i in range(KC):
                dcw_ref[i:i + 1, :] += jnp.sum(dyc * _line_shift(u, i - pad), axis=0, keepdims=True)
            dcb_ref[...] += dcb
            dclw_ref[...] += dclw
            dclb_ref[...] += dclb

        @pl.when(jnp.logical_and(b == B - 1, j == cfg.J - 1))
        def _():
            _exchange_wait(_exchange_copies(xin, xout, *sems, [True] * nx))

    prev, nxt = _halo_specs(cfg, SP)
    hs = _head_spec(cfg, lambda j: j)
    hl = _head_spec(cfg, lambda j: jnp.maximum(j - JC, 0))
    latn = lambda n: pl.BlockSpec((1, TT, n), lambda b, j: (b, jnp.maximum(j - JC, 0), 0))
    hbm = pl.BlockSpec(memory_space=pltpu.HBM)
    vec = lambda n: jax.ShapeDtypeStruct((1, n), F32)
    small_shapes = [vec(SP), vec(SP), vec(2 * W), jax.ShapeDtypeStruct((PW, 2 * W), F32), vec(2 * W),
                    jax.ShapeDtypeStruct((PW, 2 * W), F32), vec(W), vec(W),
                    jax.ShapeDtypeStruct((KP, CW), F32), vec(CW), vec(CW), vec(CW)]
    return pl.pallas_call(
        body, name="mix_prep_bwd", grid=(B, cfg.J),
        in_specs=[pl.BlockSpec((1, TT, CP), lambda b, j: (b, j, 0)), prev, nxt,
                  _full_spec((1, SP)), _full_spec((1, SP)),
                  _full_spec((1, 2 * W)), _full_spec((PW, 2 * W)),
                  _full_spec((1, 2 * W)), _full_spec((PW, 2 * W)),
                  _full_spec((1, W)), _full_spec((1, W)),
                  pl.BlockSpec((KC, LINE, CW), lambda b, j: (0, 0, 0), pipeline_mode=pl.Buffered(1)),
                  _full_spec((1, CW)), _full_spec((1, CW)), _full_spec((1, CW)),
                  _full_spec(e_w.shape), pl.BlockSpec((1, TT, CW), lambda b, j: (b, j, 0))]
                 + [hs] * 12 + [hl] * 3 + [latn(GP), latn(CW)] + [hbm] * nx,
        out_specs=[pl.BlockSpec((1, TT, CP), lambda b, j: (b, j, 0))] + [_full_spec(s.shape) for s in small_shapes]
                  + [hbm] * nx,
        out_shape=[jax.ShapeDtypeStruct((B, T, CP), F32)] + small_shapes + _exchange_shapes(xch, [True] * nx),
        scratch_shapes=[pltpu.VMEM((TT, CW), F32), pltpu.VMEM((TT, CW), F32)] + _exchange_sems(nx),
        compiler_params=_params(("arbitrary", "arbitrary")),
    )(p, p, p, mup, mun, w0, w2p, a0, a2p, k_k, k_a, wm, cb, clw, clb, e_w, yconv, *sf, *sb, *ro, *xch)


def _in_proj_bwd(dpz, xcat, modt, g1, w_in_p, mup, mun, dx1, cfg):
    B, T, TX, D, TT, SP, CP, JC = cfg.B, cfg.T, cfg.TX, cfg.D, cfg.TT, cfg.SP, cfg.CP, cfg.JC

    def body(d_ref, prev_ref, next_ref, x_ref, mod_ref, g_ref, w_ref, mup_ref, mun_ref, dx1_ref,
             gx_ref, dp_ref, dmod_ref, dg_ref):
        b, j = pl.program_id(0), pl.program_id(1)
        has_prev, has_next = _halo_flags(j, cfg)
        mp, mn = mup_ref[...], mun_ref[...]
        drw = d_ref[0, :, 0:SP]
        dprev, dnext = _shift_rows(drw, prev_ref[0, SUBLANES - 1:SUBLANES, :] * has_prev,
                                   next_ref[0, 0:1, :] * has_next)
        dz = drw * (1.0 - mp - mn) + mp * dnext + mn * dprev
        dpb = jnp.concatenate([dz, d_ref[0, :, SP:]], axis=-1).astype(BF16)
        dp_ref[0] = dpb
        dh = _bdot_nt(dpb, w_ref[...])
        _, vjp_h = jax.vjp(_rms_mod, x_ref[0], g_ref[...], mod_ref[0, 0, 0:1, :], mod_ref[0, 0, 1:2, :])
        dx, dg, dsh, dsc = vjp_h(dh)
        dmod_ref[0, 0] = jnp.concatenate([dsh, dsc], axis=0)
        _acc(dg_ref, dg, jnp.logical_and(b == 0, j == 0))

        @pl.when(j >= JC)
        def _():
            gx_ref[0] = dx + dx1_ref[0]

    prev, nxt = _halo_specs(cfg, SP)
    lat = pl.BlockSpec((1, TT, D), lambda b, j: (b, jnp.maximum(j - JC, 0), 0))
    return pl.pallas_call(
        body, name="in_proj_bwd", grid=(B, cfg.J),
        in_specs=[pl.BlockSpec((1, TT, CP), lambda b, j: (b, j, 0)), prev, nxt,
                  pl.BlockSpec((1, TT, D), lambda b, j: (b, j, 0)),
                  pl.BlockSpec((1, 1, 2, D), lambda b, j: (b, j, 0, 0)),
                  _full_spec((1, D)), _full_spec((D, CP)), _full_spec((1, SP)), _full_spec((1, SP)), lat],
        out_specs=[lat, pl.BlockSpec((1, TT, CP), lambda b, j: (b, j, 0)),
                   pl.BlockSpec((1, 1, 2, D), lambda b, j: (b, j, 0, 0)), _full_spec((1, D))],
        out_shape=[jax.ShapeDtypeStruct((B, TX, D), F32), jax.ShapeDtypeStruct((B, T, CP), BF16),
                   jax.ShapeDtypeStruct((B, cfg.J, 2, D), F32), jax.ShapeDtypeStruct((1, D), F32)],
        compiler_params=_params(("arbitrary", "arbitrary")),
    )(dpz, dpz, dpz, xcat, modt, g1, w_in_p, mup, mun, dx1)


def _pick_tile(n, pref):
    for t in pref:
        if n % t == 0:
            return t
    return n


def _grad_matmul(a, g, name):
    K, M = a.shape
    N = g.shape[1]
    tm = _pick_tile(M, (512, 256, 128))
    tn = _pick_tile(N, (1024, 768, 512, 256, 128))
    tk = _pick_tile(K, (2048, 1024, 512, 256, 128, 64))
    nk = K // tk

    def body(a_ref, g_ref, o_ref):
        k = pl.program_id(2)
        _acc(o_ref, _bdot_tn(a_ref[...], g_ref[...]), k == 0)

    return pl.pallas_call(
        body, name=name, grid=(M // tm, N // tn, nk),
        in_specs=[pl.BlockSpec((tk, tm), lambda i, j, k: (k, i)),
                  pl.BlockSpec((tk, tn), lambda i, j, k: (k, j))],
        out_specs=pl.BlockSpec((tm, tn), lambda i, j, k: (i, j)),
        out_shape=jax.ShapeDtypeStruct((M, N), F32),
        compiler_params=_params(("parallel", "parallel", "arbitrary")),
    )(a, g)


def _ada_fwd(crows, ada_w, ada_b):
    D = crows.shape[1]
    n6 = ada_w.shape[1]
    tn = _pick_tile(n6, (1024, 512, 256, 128))

    def body(c_ref, w_ref, b_ref, s_ref, m_ref):
        s = _silu(c_ref[...])
        s_ref[...] = s
        m_ref[...] = _bdot(s, w_ref[...]) + b_ref[...]

    return pl.pallas_call(
        body, name="ada_fwd", grid=(n6 // tn,),
        in_specs=[_full_spec((SUBLANES, D)), pl.BlockSpec((D, tn), lambda i: (0, i)),
                  pl.BlockSpec((1, tn), lambda i: (0, i))],
        out_specs=[_full_spec((SUBLANES, D)), pl.BlockSpec((SUBLANES, tn), lambda i: (0, i))],
        out_shape=[jax.ShapeDtypeStruct((SUBLANES, D), F32), jax.ShapeDtypeStruct((SUBLANES, n6), F32)],
        compiler_params=_params(("arbitrary",)),
    )(crows, ada_w, ada_b)


def _ada_bwd(s_all, g_all, g_mine, c_ctx, ada_w, nb):
    D = s_all.shape[1]
    n6 = g_all.shape[1]
    ns = g_mine.shape[1]

    def body(s_ref, g_ref, gm_ref, c_ref, w_ref, dw_ref, db_ref, dc_ref):
        g = g_ref[...]
        dw_ref[...] = _bdot_tn(s_ref[...], gm_ref[...])
        db_ref[...] = jnp.sum(g, axis=0, keepdims=True)
        rows = lax.broadcasted_iota(jnp.int32, (g.shape[0], 1), 0)
        gc = jnp.sum(jnp.where(rows % SUBLANES == nb, g, 0.0), axis=0, keepdims=True)
        ds = _bdot_nt(gc, w_ref[...])
        c = c_ref[...]
        sg = _sigmoid(c)
        dc_ref[...] = ds * (sg + c * sg * (1.0 - sg))

    return pl.pallas_call(
        body, name="ada_bwd",
        out_shape=[jax.ShapeDtypeStruct((D, ns), F32), jax.ShapeDtypeStruct((1, n6), F32),
                   jax.ShapeDtypeStruct((1, D), F32)],
        compiler_params=_params(),
    )(s_all, g_all, g_mine, c_ctx, ada_w)


def _adamw(parts, w, m, v, name):
    P, R, C = parts.shape
    small = R * C * (P + 7) * 4 <= 4 * 1024 * 1024
    tr = R if small else _pick_tile(R, (256, 128, 64, 32, 16, 8))

    def body(p_ref, w_ref, m_ref, v_ref, g_ref, d_ref, nm_ref, nv_ref):
        g = p_ref[0].astype(F32)
        for i in range(1, P):
            g = g + p_ref[i].astype(F32)
        nm = ADAM_B1 * m_ref[...] + (1.0 - ADAM_B1) * g
        nv = ADAM_B2 * v_ref[...] + (1.0 - ADAM_B2) * (g * g)
        m_hat = nm / (1.0 - ADAM_B1 ** ADAM_STEP)
        v_hat = nv / (1.0 - ADAM_B2 ** ADAM_STEP)
        g_ref[...] = g
        d_ref[...] = -ADAM_LR * (m_hat / (jnp.sqrt(v_hat) + ADAM_EPS) + ADAM_WD * w_ref[...])
        nm_ref[...] = nm
        nv_ref[...] = nv

    blk = pl.BlockSpec((tr, C), lambda i: (i, 0))
    out = jax.ShapeDtypeStruct((R, C), F32)
    return pl.pallas_call(
        body, name=name, grid=(R // tr,),
        in_specs=[pl.BlockSpec((P, tr, C), lambda i: (0, i, 0)), blk, blk, blk],
        out_specs=[blk] * 4, out_shape=[out] * 4,
        compiler_params=_params(("parallel",)),
    )(parts, w, m, v)


def _local_step(cfg, x, c, ctx, tgt, fw, late):
    B, D, W, CW, JC, T, TX = cfg.B, cfg.D, cfg.W, cfg.CW, cfg.JC, cfg.T, cfg.TX
    e_w = _block_ones(min(W, MXU_DIM))
    e128 = _block_ones(LANES)
    e256 = jnp.concatenate([e128, e128], axis=0)
    row = lambda a: a.reshape(1, -1)

    ada_wb = fw["ada_w"].astype(BF16)
    w_in_p = _pad_cols(fw["w_in"], cfg).astype(BF16)
    mup = _pad_cols(fw["mu_prev"], cfg, True)
    mun = _pad_cols(fw["mu_next"], cfg, True)
    w0, a0 = row(fw["decay_w0"]), row(fw["iclr_a0"])
    w2p, a2p = _pair_weight(fw["decay_w2"], cfg), _pair_weight(fw["iclr_a2"], cfg)
    wm_fwd, wm_bwd = _conv_tables(fw["conv_w"], cfg.KC)
    gw2p = jnp.pad(fw["gate_w2"], ((0, cfg.GP - cfg.GR), (0, 0)))
    r_k = row(fw["r_k"])

    crows = jnp.concatenate([c, fw["c_ctx"], jnp.zeros((SUBLANES - B - 1, D), F32)], axis=0)
    s_rows, mods = _ada_fwd(crows, ada_wb, fw["ada_b"])
    mod_x = mods[:B].reshape(B, 6, D)
    mod_c = mods[B].reshape(6, D)
    modt = jnp.concatenate([jnp.broadcast_to(mod_c[None, None, 0:2], (B, JC, 2, D)),
                            jnp.broadcast_to(mod_x[:, None, 0:2], (B, cfg.JX, 2, D))], axis=1)
    mod2, mod345 = mod_x[:, 2:3], mod_x[:, 3:6]

    xcat = jnp.concatenate([ctx, x], axis=1)
    p, hb = _in_proj(xcat, modt, fw["mix_pre_g"], w_in_p, cfg)
    prep_w = (mup, mun, w0, w2p, a0, a2p, fw["k_k"], fw["k_a"])
    conv_w = (fw["conv_b"], fw["conv_ln_w"], fw["conv_ln_b"], e_w)
    (r, v, kk, w_f, kd_f, b_f, w_b, kd_b, b_b, gd, conv, yconv, g_w_out, g_w1, g_w2) = _mix_prep(
        p, *prep_w, wm_fwd, *conv_w, late, cfg)
    w_outb, w1b, w2b = g_w_out.reshape(-1, D), _blocks_to_cols(g_w1), g_w2.reshape(-1, D)
    flat = lambda a: a.reshape(cfg.G, a.shape[2], LANES)
    heads = lambda a: a.reshape(B, cfg.HP, a.shape[1], LANES)
    ops_f = tuple(flat(a) for a in (r, w_f, kd_f, v, kk, b_f))
    ops_b = tuple(flat(a) for a in (r, w_b, kd_b, v, kk, b_b))
    y_f, y_b, hist, s_fin = _scan_fwd(ops_f, ops_b, e128, e256, cfg)
    hist = lax.dynamic_update_slice_in_dim(hist, s_fin[None], cfg.NCH * SCAN_CHUNK, axis=0)
    out_args = (heads(y_f), heads(y_b), kd_f, kd_b, r, v, gd, conv, x, mod2, r_k, gw2p, fw["lnx_w"], fw["lnx_b"],
                w_outb, fw["mix_post_g"], e_w)
    x1 = _mix_out(*out_args, cfg)

    dx1, loss_t, h2b, dpreb, actb, dffb, dmod345, dg3, dg4 = _mlp_fwd_bwd(
        x1, tgt, mod345, fw["mlp_pre_g"], fw["mlp_post_g"], w1b, w2b, cfg)
    (dy, dkb, dr_c, dv_c, dgd, dconv, catb, dmixb, dmod2, dg2, drk, dgw, dlw, dlb) = _mix_out_bwd(
        *out_args, dx1, cfg)
    sf, sb = _scan_bwd(ops_f, ops_b, flat(dy), hist, e128, e256, cfg)

    tokens = lambda a: a.reshape(-1, a.shape[-1])
    d_w_out = _grad_matmul(tokens(catb), tokens(dmixb), "grad_w_out")
    d_w1 = _grad_matmul(tokens(h2b), tokens(dpreb), "grad_mlp_w1")
    d_w2 = _grad_matmul(tokens(actb), tokens(dffb), "grad_mlp_w2")
    early = [d_w_out.astype(BF16).reshape(N_DEV, -1, D), _cols_to_blocks(d_w1.astype(BF16)),
             d_w2.astype(BF16).reshape(N_DEV, -1, D)]
    (dpz, dmup, dmun, dw0, dw2p, da0, da2p, dkk, dka, dcw, dcb, dclw, dclb, x_w_out, x_w1, x_w2) = _mix_prep_bwd(
        p, *prep_w, wm_bwd, *conv_w, yconv, [heads(a) for a in sf], [heads(a) for a in sb],
        (dr_c, dv_c, dkb, dgd, dconv), early, cfg)
    grad_x, dpb, dmodt, dg1 = _in_proj_bwd(dpz, xcat, modt, fw["mix_pre_g"], w_in_p, mup, mun, dx1, cfg)
    d_w_in = _grad_matmul(tokens(hb), tokens(dpb), "grad_w_in")
    exchanged = {"w_out": x_w_out, "mlp_w1": x_w1, "mlp_w2": x_w2}

    grads = {
        "mix_pre_g": dg1, "mix_post_g": dg2, "mlp_pre_g": dg3, "mlp_post_g": dg4,
        "w_in": _unpad_cols(d_w_in, cfg),
        "mu_prev": _unpad_cols(dmup, cfg, True), "mu_next": _unpad_cols(dmun, cfg, True),
        "decay_w0": dw0.reshape(2, W), "decay_w2": _unpair_weight(dw2p, cfg),
        "iclr_a0": da0.reshape(2, W), "iclr_a2": _unpair_weight(da2p, cfg),
        "k_k": dkk, "k_a": dka, "r_k": drk.reshape(fw["r_k"].shape),
        "gate_w2": dgw[:cfg.GR], "lnx_w": dlw, "lnx_b": dlb,
        "conv_w": dcw[:cfg.KC], "conv_b": dcb, "conv_ln_w": dclw, "conv_ln_b": dclb,
    }
    dmod_x = jnp.concatenate([jnp.sum(dmodt[:, JC:], axis=1), dmod2, dmod345], axis=1).reshape(B, 6 * D)
    dmod_c = jnp.concatenate([jnp.sum(dmodt[:, :JC], axis=(0, 1)), jnp.zeros((4, D), F32)], axis=0).reshape(1, 6 * D)
    g_rows = jnp.concatenate([dmod_x, dmod_c, jnp.zeros((SUBLANES - B - 1, 6 * D), F32)], axis=0)
    return loss_t, grad_x, grads, exchanged, s_rows, g_rows


def _my_index():
    return 4 * lax.axis_index("x") + 2 * lax.axis_index("y") + lax.axis_index("c")


def _gather_phase(phase, ins, outs, send_sems, recv_sems, local_sems):
    n = len(ins)
    x, y, c = lax.axis_index("x"), lax.axis_index("y"), lax.axis_index("c")
    index = lambda px, py, pc: 4 * px + 2 * py + pc
    me, sibling = (x, y, c), (x, y, 1 - c)
    chips = [(1 - x, y), (x, 1 - y), (1 - x, 1 - y)]

    def copy(a, k, block, to, src=None):
        dst = outs[a].at[index(*block)]
        return pltpu.make_async_remote_copy(
            src_ref=dst if src is None else src, dst_ref=dst,
            send_sem=send_sems.at[k, a], recv_sem=recv_sems.at[k, a],
            device_id=to, device_id_type=pl.DeviceIdType.MESH)

    def local():
        return [pltpu.make_async_copy(ins[a], outs[a].at[index(*me)], local_sems.at[a]) for a in range(n)]

    def first():
        return [cp for a in range(n) for cp in
                [copy(a, 0, me, sibling, src=ins[a])]
                + [copy(a, 1 + j, me, (*chip, c), src=ins[a]) for j, chip in enumerate(chips)]]

    def passed():
        return [copy(a, 4 + j, (*chip, c), sibling) for j, chip in enumerate(chips) for a in range(n)]

    if phase == 0:
        for cp in local() + first():
            cp.start()
    elif phase == 1:
        arrived = [copy(a, 1 + j, (*chip, c), me) for j, chip in enumerate(chips) for a in range(n)]
        for got, fwd in zip(arrived, passed()):
            got.wait_recv()
            fwd.start()
    else:
        for a in range(n):
            copy(a, 0, sibling, me).wait_recv()
            for j, chip in enumerate(chips):
                copy(a, 4 + j, (*chip, 1 - c), me).wait_recv()
        for cp in first() + passed():
            cp.wait_send()
        for cp in local():
            cp.wait()


def _gather_two_level(arrays, name):
    n = len(arrays)

    def body(*refs):
        for phase in range(3):
            _gather_phase(phase, refs[:n], refs[n:2 * n], *refs[2 * n:])

    hbm = pl.BlockSpec(memory_space=pltpu.HBM)
    return pl.pallas_call(
        body, name=name, out_shape=_exchange_shapes(arrays, [False] * n),
        in_specs=[hbm] * n, out_specs=[hbm] * n, scratch_shapes=_exchange_sems(n),
    )(*arrays)


def _exchange_copies(ins, outs, send_sems, recv_sems, local_sems, scatter):
    n = len(ins)
    x, y, c = lax.axis_index("x"), lax.axis_index("y"), lax.axis_index("c")
    me = 4 * x + 2 * y + c
    flip = lambda v, f: 1 - v if f else v

    def piece(a, dest):
        return ins[a].at[dest] if scatter[a] else ins[a]

    local = [pltpu.make_async_copy(piece(a, me), outs[a].at[me], local_sems.at[a]) for a in range(n)]
    sends, recvs = [], []
    for k in range(1, N_DEV):
        fx, fy, fc = (k >> 2) & 1, (k >> 1) & 1, k & 1
        peer = (flip(x, fx), flip(y, fy), flip(c, fc))
        peer_idx = 4 * peer[0] + 2 * peer[1] + peer[2]
        for a in range(n):
            sends.append(pltpu.make_async_remote_copy(
                src_ref=piece(a, peer_idx), dst_ref=outs[a].at[me],
                send_sem=send_sems.at[k - 1, a], recv_sem=recv_sems.at[k - 1, a],
                device_id=peer, device_id_type=pl.DeviceIdType.MESH))
            recvs.append(pltpu.make_async_remote_copy(
                src_ref=piece(a, peer_idx), dst_ref=outs[a].at[peer_idx],
                send_sem=send_sems.at[k - 1, a], recv_sem=recv_sems.at[k - 1, a],
                device_id=peer, device_id_type=pl.DeviceIdType.MESH))
    return local, sends, recvs


def _exchange_start(copies):
    local, sends, _ = copies
    for cp in local + sends:
        cp.start()


def _exchange_wait(copies):
    local, sends, recvs = copies
    for cp in recvs:
        cp.wait_recv()
    for cp in sends:
        cp.wait_send()
    for cp in local:
        cp.wait()


def _exchange_shapes(arrays, scatter):
    return [jax.ShapeDtypeStruct(a.shape if s else (N_DEV,) + a.shape, a.dtype) for a, s in zip(arrays, scatter)]


def _exchange_sems(n):
    return [pltpu.SemaphoreType.DMA((N_DEV - 1, n)), pltpu.SemaphoreType.DMA((N_DEV - 1, n)),
            pltpu.SemaphoreType.DMA((n,))]


def _exchange(arrays, scatter, name):
    n = len(arrays)

    def body(*refs):
        copies = _exchange_copies(refs[:n], refs[n:2 * n], *refs[2 * n:], scatter)
        _exchange_start(copies)
        _exchange_wait(copies)

    hbm = pl.BlockSpec(memory_space=pltpu.HBM)
    return pl.pallas_call(
        body, name=name, out_shape=_exchange_shapes(arrays, scatter),
        in_specs=[hbm] * n, out_specs=[hbm] * n, scratch_shapes=_exchange_sems(n),
    )(*arrays)


def _pack(parts):
    flat = jnp.concatenate([p.reshape(-1) for p in parts])
    total = _round_up(flat.shape[0], SUBLANES * LANES)
    return jnp.pad(flat, (0, total - flat.shape[0])).reshape(-1, LANES)


def _unpack(buf, shapes):
    flat = buf.reshape(-1)
    out, pos = [], 0
    for s in shapes:
        n = int(np.prod(s))
        out.append(flat[pos:pos + n].reshape(s))
        pos += n
    return out


_SHARDED_SMALL = ("decay_w0", "decay_w2", "iclr_a0", "iclr_a2", "gate_w2", "conv_w")
_REPLICATED = ("mix_pre_g", "mix_post_g", "mlp_pre_g", "mlp_post_g", "mu_prev", "mu_next", "k_k", "k_a", "r_k",
               "lnx_w", "lnx_b", "conv_b", "conv_ln_w", "conv_ln_b")
_ADA_SMALL = ("c_ctx", "ada_b")
_WEIGHTS = ("c_ctx", "ada_w", "ada_b", "mix_pre_g", "mix_post_g", "mlp_pre_g", "mlp_post_g", "w_in", "mu_prev",
            "mu_next", "decay_w0", "decay_w2", "iclr_a0", "iclr_a2", "k_k", "k_a", "r_k", "gate_w2", "lnx_w", "lnx_b",
            "conv_w", "conv_b", "conv_ln_w", "conv_ln_b", "w_out", "mlp_w1", "mlp_w2")
_INPUTS = ("x", "c", "ctx") + _WEIGHTS + ("loss_target",) + tuple("m_" + n for n in _WEIGHTS) + tuple(
    "v_" + n for n in _WEIGHTS)


def _cols_to_blocks(a):
    a = a.reshape(a.shape[:-1] + (N_DEV, a.shape[-1] // N_DEV))
    return jnp.moveaxis(a, -2, 0)


def _blocks_to_cols(a):
    a = jnp.moveaxis(a, 0, -2)
    return a.reshape(a.shape[:-2] + (a.shape[-2] * a.shape[-1],))


def kernel(x, c, ctx, c_ctx, ada_w, ada_b, mix_pre_g, mix_post_g, mlp_pre_g, mlp_post_g, w_in, mu_prev, mu_next, decay_w0, decay_w2, iclr_a0, iclr_a2, k_k, k_a, r_k, gate_w2, lnx_w, lnx_b, conv_w, conv_b, conv_ln_w, conv_ln_b, w_out, mlp_w1, mlp_w2, loss_target, m_c_ctx, m_ada_w, m_ada_b, m_mix_pre_g, m_mix_post_g, m_mlp_pre_g, m_mlp_post_g, m_w_in, m_mu_prev, m_mu_next, m_decay_w0, m_decay_w2, m_iclr_a0, m_iclr_a2, m_k_k, m_k_a, m_r_k, m_gate_w2, m_lnx_w, m_lnx_b, m_conv_w, m_conv_b, m_conv_ln_w, m_conv_ln_b, m_w_out, m_mlp_w1, m_mlp_w2, v_c_ctx, v_ada_w, v_ada_b, v_mix_pre_g, v_mix_post_g, v_mlp_pre_g, v_mlp_post_g, v_w_in, v_mu_prev, v_mu_next, v_decay_w0, v_decay_w2, v_iclr_a0, v_iclr_a2, v_k_k, v_k_a, v_r_k, v_gate_w2, v_lnx_w, v_lnx_b, v_conv_w, v_conv_b, v_conv_ln_w, v_conv_ln_b, v_w_out, v_mlp_w1, v_mlp_w2):
    given = dict(zip(_INPUTS, (x, c, ctx, c_ctx, ada_w, ada_b, mix_pre_g, mix_post_g, mlp_pre_g, mlp_post_g, w_in, mu_prev, mu_next, decay_w0, decay_w2, iclr_a0, iclr_a2, k_k, k_a, r_k, gate_w2, lnx_w, lnx_b, conv_w, conv_b, conv_ln_w, conv_ln_b, w_out, mlp_w1, mlp_w2, loss_target, m_c_ctx, m_ada_w, m_ada_b, m_mix_pre_g, m_mix_post_g, m_mlp_pre_g, m_mlp_post_g, m_w_in, m_mu_prev, m_mu_next, m_decay_w0, m_decay_w2, m_iclr_a0, m_iclr_a2, m_k_k, m_k_a, m_r_k, m_gate_w2, m_lnx_w, m_lnx_b, m_conv_w, m_conv_b, m_conv_ln_w, m_conv_ln_b, m_w_out, m_mlp_w1, m_mlp_w2, v_c_ctx, v_ada_w, v_ada_b, v_mix_pre_g, v_mix_post_g, v_mlp_pre_g, v_mlp_post_g, v_w_in, v_mu_prev, v_mu_next, v_decay_w0, v_decay_w2, v_iclr_a0, v_iclr_a2, v_k_k, v_k_a, v_r_k, v_gate_w2, v_lnx_w, v_lnx_b, v_conv_w, v_conv_b, v_conv_ln_w, v_conv_ln_b, v_w_out, v_mlp_w1, v_mlp_w2)))
    loc = {}
    for pre in ("", "m_", "v_"):
        for n in _WEIGHTS:
            a = given[pre + n]
            a = a.reshape(1, -1) if n == "c_ctx" else a[0]
            loc[pre + n] = a.reshape(1, -1) if a.ndim == 1 else a
    B, TX, D = x.shape
    W, CW = loc["k_k"].shape[1], loc["conv_b"].shape[1]
    cfg = _Cfg(B, TX, ctx.shape[1], D, W, CW, loc["decay_w2"].shape[1], loc["gate_w2"].shape[0],
               loc["conv_w"].shape[0], loc["mlp_w1"].shape[1] * N_DEV)
    me = _my_index()

    small_shapes = [loc[n].shape for n in _SHARDED_SMALL]
    got = _gather_two_level(
        [loc["ada_w"].astype(BF16), loc["w_in"].astype(BF16), _pack([loc[n] for n in _SHARDED_SMALL])],
        "gather_weights")
    fw = {n: loc[n] for n in _REPLICATED + _ADA_SMALL}
    fw["ada_w"] = _blocks_to_cols(got[0])
    fw["w_in"] = _blocks_to_cols(got[1])
    per_dev = [_unpack(got[2][i], small_shapes) for i in range(N_DEV)]
    for j, n in enumerate(_SHARDED_SMALL):
        fw[n] = jnp.concatenate([per_dev[i][j] for i in range(N_DEV)], axis=-1)
    late = [loc[n].astype(BF16) for n in ("w_out", "mlp_w1", "mlp_w2")]

    loss_t, grad_x, grads, exchanged, s_rows, g_rows = _local_step(cfg, x, c, ctx, loss_target, fw, late)
    loss = lax.psum(jnp.sum(loss_t[:, :, 0, 0]), ("x", "y", "c"))

    small_blocks = jnp.stack([_pack([_cols_to_blocks(grads[n])[i] for n in _SHARDED_SMALL]) for i in range(N_DEV)])
    sent = _exchange(
        [_cols_to_blocks(grads["w_in"].astype(BF16)), small_blocks,
         _pack([grads[n] for n in _REPLICATED]), s_rows, g_rows],
        [True] * 2 + [False] * 3, "exchange_grads")
    s_all = sent[3].reshape(N_DEV * SUBLANES, D)
    g_all = sent[4].reshape(N_DEV * SUBLANES, 6 * D)
    ns = 6 * D // N_DEV
    g_mine = lax.dynamic_slice_in_dim(g_all, me * ns, ns, axis=1)
    d_ada_w, d_ada_b, d_c_ctx = _ada_bwd(s_all, g_all, g_mine, loc["c_ctx"], fw["ada_w"], B)

    res = {}

    def update(name, parts):
        res[name] = _adamw(parts, loc[name], loc["m_" + name], loc["v_" + name], "adamw_" + name)

    update("w_in", sent[0])
    for n in ("w_out", "mlp_w1", "mlp_w2"):
        update(n, exchanged[n])
    update("ada_w", d_ada_w[None])

    def update_packed(names, parts, tag):
        shapes = [loc[n].shape for n in names]
        packed = _adamw(parts, *[_pack([loc[pre + n] for n in names]) for pre in ("", "m_", "v_")], "adamw_" + tag)
        unpacked = [_unpack(p, shapes) for p in packed]
        for j, n in enumerate(names):
            res[n] = tuple(u[j] for u in unpacked)

    update_packed(_SHARDED_SMALL, sent[1], "sharded_small")
    update_packed(_REPLICATED, sent[2], "replicated")
    update_packed(_ADA_SMALL, _pack([d_c_ctx, d_ada_b])[None], "ada_small")

    outs = [loss, grad_x]
    for k in range(4):
        for n in _WEIGHTS:
            outs.append(res[n][k].reshape(given[n].shape))
    return tuple(outs)
```

```python
import functools

import numpy as np
import jax
import jax.numpy as jnp
from jax import lax
from jax.experimental import pallas as pl
from jax.experimental.pallas import tpu as pltpu

F32 = jnp.float32
BF16 = jnp.bfloat16

EPS_RMS = 1e-6
EPS_LN = 1e-5
EPS_GN = 64e-5
LINE = 64
HEAD = 64
LANES = 128
SUBLANES = 8
MXU_DIM = 256
SCAN_CHUNK = 16
N_DEV = 8
VMEM_LIMIT = 56 * 1024 * 1024

ADAM_LR = 0.001
ADAM_B1 = 0.9
ADAM_B2 = 0.999
ADAM_EPS = 1e-08
ADAM_WD = 0.01
ADAM_STEP = 10


def _round_up(n, m):
    return (n + m - 1) // m * m


def _params(semantics=None, vmem=VMEM_LIMIT):
    return pltpu.CompilerParams(dimension_semantics=semantics, vmem_limit_bytes=vmem)


def _bdot(a, b):
    return jnp.dot(a.astype(BF16), b.astype(BF16), preferred_element_type=F32)


def _bdot_nt(a, b):
    return lax.dot_general(a.astype(BF16), b.astype(BF16), (((1,), (1,)), ((), ())),
                           preferred_element_type=F32)


def _bdot_tn(a, b):
    return lax.dot_general(a.astype(BF16), b.astype(BF16), (((0,), (0,)), ((), ())),
                           preferred_element_type=F32)


@jax.custom_vjp
def _mm(a, b):
    return _bdot(a, b)


def _mm_fwd(a, b):
    return _bdot(a, b), (a, b)


def _mm_bwd(res, g):
    a, b = res
    return _bdot_nt(g, b), _bdot_tn(a, g)


_mm.defvjp(_mm_fwd, _mm_bwd)


def _seg_sum_raw(x, e):
    hi = x.astype(BF16)
    lo = (x - hi.astype(F32)).astype(BF16)
    n = e.shape[0]
    parts = [jnp.dot(hi[:, s:s + n], e, preferred_element_type=F32)
             + jnp.dot(lo[:, s:s + n], e, preferred_element_type=F32) for s in range(0, x.shape[1], n)]
    return parts[0] if len(parts) == 1 else jnp.concatenate(parts, axis=1)


@jax.custom_vjp
def _seg_sum(x, e):
    return _seg_sum_raw(x, e)


def _seg_sum_fwd(x, e):
    return _seg_sum_raw(x, e), e


def _seg_sum_bwd(e, g):
    return _seg_sum_raw(g, e), None


_seg_sum.defvjp(_seg_sum_fwd, _seg_sum_bwd)


def _block_ones(n, seg=HEAD):
    i = np.arange(n) // seg
    return jnp.asarray((i[:, None] == i[None, :]).astype(np.float32), dtype=BF16)


def _rms(xv, g):
    ms = jnp.mean(xv * xv, axis=-1, keepdims=True)
    return xv * lax.rsqrt(ms + EPS_RMS) * g


def _rms_mod(xv, g, shift, scale):
    return _rms(xv, g) * (1.0 + scale) + shift


def _sigmoid(z):
    return 1.0 / (1.0 + jnp.exp(-z))


def _silu(z):
    return z * _sigmoid(z)


def _softplus(z):
    return jnp.maximum(z, 0.0) + jnp.log(1.0 + jnp.exp(-jnp.abs(z)))


class _Cfg:
    def __init__(self, B, TX, TC, D, W, CW, R, GR, KC, F):
        self.B, self.TX, self.TC, self.D = B, TX, TC, D
        self.W, self.CW, self.R, self.GR, self.KC, self.F = W, CW, R, GR, KC, F
        self.T = TX + TC
        self.TT = min(256, TC)
        assert TC % self.TT == 0 and TX % self.TT == 0 and self.TT % LINE == 0
        self.JC = TC // self.TT
        self.JX = TX // self.TT
        self.J = self.JC + self.JX
        self.HP = W // LANES
        self.G = B * self.HP
        self.PW = _round_up(2 * R, LANES)
        self.GP = _round_up(GR, LANES)
        self.SP = 3 * W + 2 * self.PW + self.GP
        self.CP = self.SP + 2 * CW
        self.KP = _round_up(KC, SUBLANES)
        assert self.T % SCAN_CHUNK == 0 and TC % SCAN_CHUNK == 0
        self.NCH = self.T // SCAN_CHUNK
        self.NCC = TC // SCAN_CHUNK
        W_, R_ = W, R
        segs = [(0, 3 * W_, 0),
                (3 * W_, 2 * R_, 3 * W_),
                (3 * W_ + 2 * R_, 2 * R_, 3 * W_ + self.PW),
                (3 * W_ + 4 * R_, GR, 3 * W_ + 2 * self.PW),
                (3 * W_ + 4 * R_ + GR, 2 * CW, self.SP)]
        self.col_segs = segs
        self.shift_cols = 3 * W_ + 4 * R_ + GR
        self.in_cols = self.shift_cols + 2 * CW


def _pad_cols(a, cfg, upto_shift=False):
    width = cfg.SP if upto_shift else cfg.CP
    pieces, pos = [], 0
    for src, n, dst in cfg.col_segs:
        if upto_shift and dst >= cfg.SP:
            break
        if dst > pos:
            pieces.append(jnp.zeros(a.shape[:-1] + (dst - pos,), a.dtype))
        pieces.append(a[..., src:src + n])
        pos = dst + n
    if width > pos:
        pieces.append(jnp.zeros(a.shape[:-1] + (width - pos,), a.dtype))
    return jnp.concatenate(pieces, axis=-1)


def _unpad_cols(a, cfg, upto_shift=False):
    pieces = []
    for src, n, dst in cfg.col_segs:
        if upto_shift and dst >= cfg.SP:
            break
        pieces.append(a[..., dst:dst + n])
    return jnp.concatenate(pieces, axis=-1)


def _pair_weight(w2, cfg):
    R, W = cfg.R, cfg.W
    out = jnp.zeros((cfg.PW, 2 * W), w2.dtype)
    out = out.at[0:R, 0:W].set(w2[0])
    out = out.at[R:2 * R, W:2 * W].set(w2[1])
    return out


def _unpair_weight(g, cfg):
    R, W = cfg.R, cfg.W
    return jnp.stack([g[0:R, 0:W], g[R:2 * R, W:2 * W]])


def _row_ids(n):
    return lax.broadcasted_iota(jnp.int32, (n, 1), 0)


def _shift_rows(z, prev_row, next_row):
    n = z.shape[0]
    rows = _row_ids(n)
    zp = jnp.where(rows == 0, prev_row, pltpu.roll(z, 1, 0))
    zn = jnp.where(rows == n - 1, next_row, pltpu.roll(z, n - 1, 0))
    return zp, zn


def _line_shift(u, d):
    if d == 0:
        return u
    n = u.shape[0]
    lt = _row_ids(n) % LINE
    ok = jnp.logical_and(lt + d >= 0, lt + d < LINE)
    return jnp.where(ok, pltpu.roll(u, (-d) % n, 0), 0.0)


def _conv_tables(cw, kc):
    pad = kc // 2
    t = np.arange(LINE)[None, :]
    d = (np.arange(kc) - pad)[:, None]
    fwd = ((t + d >= 0) & (t + d < LINE)).astype(np.float32)
    bwd = ((t - d >= 0) & (t - d < LINE)).astype(np.float32)
    w = cw[:kc, None, :]
    return jnp.asarray(fwd)[:, :, None] * w, jnp.asarray(bwd)[:, :, None] * w


def _conv_lines(src_ref, wm_ref, dst_ref, kc, transpose):
    pad = kc // 2
    n, width = src_ref.shape
    for l in range(n // LINE):
        for b in range(width // LANES):
            rs, cs = slice(l * LINE, (l + 1) * LINE), slice(b * LANES, (b + 1) * LANES)
            tile = src_ref[rs, cs]
            acc = jnp.zeros_like(tile)
            for i in range(kc):
                d = (pad - i) if transpose else (i - pad)
                acc = acc + pltpu.roll(tile, (-d) % LINE, 0) * wm_ref[i, :, cs]
            dst_ref[rs, cs] = acc


def _rwkv_prep(rw, w0, w2p, a0, a2p, k_k, k_a, e, cfg):
    W, PW = cfg.W, cfg.PW
    r = rw[:, 0:W]
    k = rw[:, W:2 * W]
    v = rw[:, 2 * W:3 * W]
    wdp = rw[:, 3 * W:3 * W + PW]
    adp = rw[:, 3 * W + PW:3 * W + 2 * PW]
    wl = w0 + _mm(jnp.tanh(wdp), w2p)
    w_log = -_softplus(-wl) - 0.5
    decay = jnp.exp(-jnp.exp(w_log))
    iclr = _sigmoid(a0 + _mm(adp, a2p))
    kkr = k * k_k
    nrm = jnp.sqrt(_seg_sum(kkr * kkr, e))
    kk = kkr / jnp.maximum(nrm, 1e-12)
    outs = [r, v, kk]
    for d in range(2):
        ic = iclr[:, d * W:(d + 1) * W]
        outs += [decay[:, d * W:(d + 1) * W], k * (1.0 + (ic - 1.0) * k_a), kk * ic]
    return tuple(outs)


def _glu(cv, cfg):
    return cv[:, :cfg.CW] * _sigmoid(cv[:, cfg.CW:])


def _conv_post(y, cb, lw, lb):
    yf = y + cb
    mu = jnp.mean(yf, axis=-1, keepdims=True)
    var = jnp.mean(jnp.square(yf - mu), axis=-1, keepdims=True)
    return _silu((yf - mu) * lax.rsqrt(var + EPS_LN) * lw + lb)


def _readout(y, kbar, r, v, gd, r_k, gw2, lnx_w, lnx_b, e):
    inv = 1.0 / HEAD
    mu = _seg_sum(y, e) * inv
    yc = y - mu
    var = _seg_sum(yc * yc, e) * inv
    yn = yc * lax.rsqrt(var + EPS_GN) * lnx_w + lnx_b
    bonus = _seg_sum(r * kbar * r_k, e) * v
    g = _mm(_sigmoid(gd), gw2)
    return (yn + bonus) * g


def _post_res(xv, mix, gate, g):
    return xv + gate * _rms(mix, g)


def _head_spec(cfg, tmap):
    return pl.BlockSpec((1, cfg.HP, cfg.TT, LANES), lambda b, j: (b, 0, tmap(j), 0))


def _full_spec(shape):
    n = len(shape)
    return pl.BlockSpec(shape, lambda *_: (0,) * n)


def _to_heads(ref, val, cfg):
    for hp in range(cfg.HP):
        ref[0, hp] = val[:, hp * LANES:(hp + 1) * LANES]


def _from_heads(ref, cfg):
    return jnp.concatenate([ref[0, hp] for hp in range(cfg.HP)], axis=-1)


def _in_proj(xcat, modt, g1, w_in_p, cfg):
    B, T, D, TT, CP = cfg.B, cfg.T, cfg.D, cfg.TT, cfg.CP

    def body(x_ref, mod_ref, g_ref, w_ref, p_ref, h_ref):
        h = _rms_mod(x_ref[0], g_ref[...], mod_ref[0, 0, 0:1, :], mod_ref[0, 0, 1:2, :])
        hb = h.astype(BF16)
        h_ref[0] = hb
        p_ref[0] = jnp.dot(hb, w_ref[...], preferred_element_type=F32)

    return pl.pallas_call(
        body, name="in_proj", grid=(B, cfg.J),
        in_specs=[pl.BlockSpec((1, TT, D), lambda b, j: (b, j, 0)),
                  pl.BlockSpec((1, 1, 2, D), lambda b, j: (b, j, 0, 0)),
                  _full_spec((1, D)), _full_spec((D, CP))],
        out_specs=[pl.BlockSpec((1, TT, CP), lambda b, j: (b, j, 0)),
                   pl.BlockSpec((1, TT, D), lambda b, j: (b, j, 0))],
        out_shape=[jax.ShapeDtypeStruct((B, T, CP), F32), jax.ShapeDtypeStruct((B, T, D), BF16)],
        compiler_params=_params(("parallel", "parallel")),
    )(xcat, modt, g1, w_in_p)


def _halo_specs(cfg, width):
    per = cfg.TT // SUBLANES
    last = cfg.T // SUBLANES - 1
    prev = pl.BlockSpec((1, SUBLANES, width), lambda b, j: (b, jnp.maximum(j * per - 1, 0), 0))
    nxt = pl.BlockSpec((1, SUBLANES, width), lambda b, j: (b, jnp.minimum((j + 1) * per, last), 0))
    return prev, nxt


def _halo_flags(j, cfg):
    has_prev = jnp.logical_and(j != 0, j != cfg.JC).astype(F32)
    has_next = jnp.logical_and(j != cfg.JC - 1, j != cfg.J - 1).astype(F32)
    return has_prev, has_next


def _shifted(p_ref, prev_ref, next_ref, mup, mun, j, cfg):
    SP = cfg.SP
    has_prev, has_next = _halo_flags(j, cfg)
    z = p_ref[0][:, :SP]
    zp, zn = _shift_rows(z, prev_ref[0, SUBLANES - 1:SUBLANES, :] * has_prev, next_ref[0, 0:1, :] * has_next)
    return z, zp, zn, z + mup * (zp - z) + mun * (zn - z)


def _mix_prep(p, mup, mun, w0, w2p, a0, a2p, k_k, k_a, wm, cb, clw, clb, e_w, late, cfg):
    B, T, TT, SP, CP, W, CW, HP, JC = cfg.B, cfg.T, cfg.TT, cfg.SP, cfg.CP, cfg.W, cfg.CW, cfg.HP, cfg.JC
    nl = len(late)
    steps = B * cfg.J

    def body(p_ref, prev_ref, next_ref, mup_ref, mun_ref, w0_ref, w2_ref, a0_ref, a2_ref, kk_ref, ka_ref,
             wm_ref, cb_ref, clw_ref, clb_ref, e_ref, *rest):
        late_in, rest = rest[:nl], rest[nl:]
        outs, late_out, u_ref, sems = rest[:12], rest[12:12 + nl], rest[12 + nl], rest[13 + nl:]
        j = pl.program_id(1)
        step = pl.program_id(0) * cfg.J + j
        for phase, at in enumerate((0, steps // 2)):
            @pl.when(step == at)
            def _(phase=phase):
                _gather_phase(phase, late_in, late_out, *sems)
        _, _, _, rw = _shifted(p_ref, prev_ref, next_ref, mup_ref[...], mun_ref[...], j, cfg)
        vals = _rwkv_prep(rw, w0_ref[...], w2_ref[...], a0_ref[...], a2_ref[...], kk_ref[...], ka_ref[...],
                          e_ref[...], cfg)
        for ref, val in zip(outs[:9], vals):
            _to_heads(ref, val, cfg)
        outs[9][0] = rw[:, 3 * W + 2 * cfg.PW:SP]

        @pl.when(j >= JC)
        def _():
            u_ref[...] = _glu(p_ref[0, :, SP:], cfg)
            _conv_lines(u_ref, wm_ref, outs[11].at[0], cfg.KC, False)
            outs[10][0] = _conv_post(outs[11][0], cb_ref[...], clw_ref[...], clb_ref[...])

        @pl.when(step == steps - 1)
        def _():
            _gather_phase(2, late_in, late_out, *sems)

    prev, nxt = _halo_specs(cfg, SP)
    head = jax.ShapeDtypeStruct((B, HP, T, LANES), F32)
    hbm = pl.BlockSpec(memory_space=pltpu.HBM)
    tile = lambda n: pl.BlockSpec((1, TT, n), lambda b, j: (b, j, 0))
    return pl.pallas_call(
        body, name="mix_prep", grid=(B, cfg.J),
        in_specs=[tile(CP), prev, nxt,
                  _full_spec((1, SP)), _full_spec((1, SP)),
                  _full_spec((1, 2 * W)), _full_spec((cfg.PW, 2 * W)),
                  _full_spec((1, 2 * W)), _full_spec((cfg.PW, 2 * W)),
                  _full_spec((1, W)), _full_spec((1, W)),
                  pl.BlockSpec((cfg.KC, LINE, CW), lambda b, j: (0, 0, 0), pipeline_mode=pl.Buffered(1)),
                  _full_spec((1, CW)), _full_spec((1, CW)), _full_spec((1, CW)),
                  _full_spec(e_w.shape)] + [hbm] * nl,
        out_specs=[_head_spec(cfg, lambda j: j)] * 9 + [tile(cfg.GP), tile(CW), tile(CW)] + [hbm] * nl,
        out_shape=[head] * 9 + [jax.ShapeDtypeStruct((B, T, cfg.GP), F32),
                                jax.ShapeDtypeStruct((B, T, CW), F32), jax.ShapeDtypeStruct((B, T, CW), F32)]
                  + _exchange_shapes(late, [False] * nl),
        scratch_shapes=[pltpu.VMEM((TT, CW), F32)] + _exchange_sems(nl),
        compiler_params=_params(("arbitrary", "arbitrary")),
    )(p, p, p, mup, mun, w0, w2p, a0, a2p, k_k, k_a, wm, cb, clw, clb, e_w, *late)


def _chunk_pos(c, reverse, cfg):
    if not reverse:
        return c
    return jnp.where(c < cfg.NCC, cfg.NCC - 1 - c, cfg.NCH - 1 + cfg.NCC - c)


def _diag_mask():
    r = lax.broadcasted_iota(jnp.int32, (HEAD, LANES), 0)
    l = lax.broadcasted_iota(jnp.int32, (HEAD, LANES), 1)
    return (r == l % HEAD).astype(F32)


def _col_lhs(row, diag_b):
    hi = row.astype(BF16)
    lo = (row - hi.astype(F32)).astype(BF16)
    return diag_b * hi, diag_b * lo


def _col_dot(row_list, diag_b, e2):
    n, g = len(row_list), row_list[0].shape[0]
    lhs = jnp.concatenate([jnp.concatenate(_col_lhs(r, diag_b), axis=-1) for r in row_list], axis=0)
    out = jnp.dot(lhs.reshape(n * g * HEAD, 2 * LANES), e2, preferred_element_type=F32)
    return out.reshape(n, g, HEAD, LANES)


def _col_form(row):
    n = row.shape[0]
    t = jnp.swapaxes(jnp.broadcast_to(row, (n, LANES, LANES)), 1, 2)
    lane = lax.broadcasted_iota(jnp.int32, (HEAD, LANES), 1)
    return jnp.where(lane < HEAD, t[:, :HEAD, :], t[:, HEAD:, :])


def _col_both(row, diag_b, e2):
    half = row.shape[0] // 2
    return jnp.concatenate([_col_form(row[:half]), _col_dot([row[half:]], diag_b, e2)[0]], axis=0)


def _both_rows(ins, idx, i):
    return jnp.concatenate([ins[d][idx][:, pl.ds(_tok(i, d == 1), 1), :] for d in range(2)], axis=0)


def _seg_dot(blocks, e):
    n, g = len(blocks), blocks[0].shape[0]
    lhs = jnp.concatenate(blocks, axis=0).reshape(n * g * HEAD, LANES)
    return jnp.dot(lhs, e, preferred_element_type=F32).reshape(n, g, HEAD, LANES)


def _tok(i, reverse):
    return (SCAN_CHUNK - 1 - i) if reverse else i


def _scan_fwd(ops_f, ops_b, e128, e256, cfg):
    G, T, NCH = cfg.G, cfg.T, cfg.NCH
    CH = SCAN_CHUNK
    G2 = 2 * G

    def body(*refs):
        ins = (refs[0:6], refs[6:12])
        ahead = (refs[12:14], refs[14:16])
        e_ref, e2_ref = refs[16], refs[17]
        ys, hist_ref, fin_ref = (refs[18], refs[19]), refs[20], refs[21]
        s_ref, mm_ref = refs[22], refs[23]
        c = pl.program_id(0)
        diag = _diag_mask()
        diag_b = diag.astype(BF16)
        e, e2 = e_ref[...], e2_ref[...]

        @pl.when(c == 0)
        def _():
            s_ref[...] = jnp.zeros_like(s_ref)
            mm_ref[0] = jnp.zeros_like(s_ref)
            mm_ref[1] = _col_both(_both_rows(ins, 3, 0), diag_b, e2)

        def step(i, nxt_row):
            res = []
            for d in range(2):
                sl = slice(d * G, (d + 1) * G)
                row = lambda idx: ins[d][idx][:, pl.ds(_tok(i, d == 1), 1), :]
                s_old = s_ref[sl]
                hist_ref[i, sl] = s_old
                S = s_old * row(1) + mm_ref[0, sl] * row(5) + mm_ref[1, sl] * row(2)
                s_ref[sl] = S
                sb = S.astype(BF16)
                res.append(_seg_dot([sb * (-nxt_row(d, 4)).astype(BF16), sb * row(0).astype(BF16)], e))
            for d in range(2):
                sl = slice(d * G, (d + 1) * G)
                v_next = nxt_row(d, 3)
                mm_ref[0, sl] = res[d][0]
                mm_ref[1, sl] = _col_form(v_next) if d == 0 else _col_dot([v_next], diag_b, e2)[0]
                ys[d][:, pl.ds(_tok(i, d == 1), 1), :] = jnp.sum(diag * res[d][1], axis=1, keepdims=True)

        def inner(i, carry):
            step(i, lambda d, idx: ins[d][idx][:, pl.ds(_tok(i + 1, d == 1), 1), :])
            return carry

        lax.fori_loop(0, CH - 1, inner, 0)
        edge = lambda d: (SUBLANES - 1) if d == 1 else 0
        step(CH - 1, lambda d, idx: ahead[d][0 if idx == 4 else 1][:, edge(d):edge(d) + 1, :])
        fin_ref[...] = s_ref[...]

    toks = [pl.BlockSpec((G, CH, LANES), lambda c, rev=rev: (0, _chunk_pos(c, rev, cfg), 0)) for rev in (False, True)]
    per = CH // SUBLANES

    def ahead_block(c, rev):
        pos = _chunk_pos(jnp.minimum(c + 1, NCH - 1), rev, cfg)
        return pos * per + (per - 1 if rev else 0)

    ahead = [pl.BlockSpec((G, SUBLANES, LANES), lambda c, rev=rev: (0, ahead_block(c, rev), 0)) for rev in (False, True)]
    y_shape = jax.ShapeDtypeStruct((G, T, LANES), F32)
    return pl.pallas_call(
        body, name="scan_fwd", grid=(NCH,),
        in_specs=[toks[0]] * 6 + [toks[1]] * 6 + [ahead[0]] * 2 + [ahead[1]] * 2
                 + [_full_spec((LANES, LANES)), _full_spec((2 * LANES, LANES))],
        out_specs=[toks[0], toks[1], pl.BlockSpec((CH, G2, HEAD, LANES), lambda c: (c, 0, 0, 0)),
                   _full_spec((G2, HEAD, LANES))],
        out_shape=[y_shape, y_shape, jax.ShapeDtypeStruct(((NCH + 1) * CH, G2, HEAD, LANES), F32),
                   jax.ShapeDtypeStruct((G2, HEAD, LANES), F32)],
        scratch_shapes=[pltpu.VMEM((G2, HEAD, LANES), F32), pltpu.VMEM((2, G2, HEAD, LANES), F32)],
        compiler_params=_params(("arbitrary",)),
    )(*ops_f, *ops_b, ops_f[4], ops_f[3], ops_b[4], ops_b[3], e128, e256)


def _mix_out(yf, yb, kdf, kdb, r, v, gd, conv, x, mod2, r_k, gw2p, lnx_w, lnx_b, w_out, g2, e_w, cfg):
    B, TX, D, TT, W, CW, JC = cfg.B, cfg.TX, cfg.D, cfg.TT, cfg.W, cfg.CW, cfg.JC

    def body(yf_ref, yb_ref, kdf_ref, kdb_ref, r_ref, v_ref, gd_ref, cv_ref, x_ref, mod_ref,
             rk_ref, gw_ref, lw_ref, lb_ref, wo_ref, g_ref, e_ref, x1_ref):
        y = _from_heads(yf_ref, cfg) + _from_heads(yb_ref, cfg)
        kbar = 0.5 * (_from_heads(kdf_ref, cfg) + _from_heads(kdb_ref, cfg))
        ro = _readout(y, kbar, _from_heads(r_ref, cfg), _from_heads(v_ref, cfg), gd_ref[0], rk_ref[...],
                      gw_ref[...], lw_ref[...], lb_ref[...], e_ref[...])
        cat = jnp.concatenate([ro, cv_ref[0]], axis=-1)
        mix = _bdot(cat, wo_ref[...])
        x1_ref[0] = _post_res(x_ref[0], mix, mod_ref[0], g_ref[...])

    hs = _head_spec(cfg, lambda j: j + JC)
    lat = lambda n: pl.BlockSpec((1, TT, n), lambda b, j: (b, j + JC, 0))
    return pl.pallas_call(
        body, name="mix_out", grid=(B, cfg.JX),
        in_specs=[hs] * 6 + [lat(cfg.GP), lat(CW),
                             pl.BlockSpec((1, TT, D), lambda b, j: (b, j, 0)),
                             pl.BlockSpec((1, 1, D), lambda b, j: (b, 0, 0)),
                             _full_spec((1, W)), _full_spec((cfg.GP, W)), _full_spec((1, W)), _full_spec((1, W)),
                             _full_spec((W + CW, D)), _full_spec((1, D)), _full_spec(e_w.shape)],
        out_specs=pl.BlockSpec((1, TT, D), lambda b, j: (b, j, 0)),
        out_shape=jax.ShapeDtypeStruct((B, TX, D), F32),
        compiler_params=_params(("parallel", "parallel")),
    )(yf, yb, kdf, kdb, r, v, gd, conv, x, mod2, r_k, gw2p, lnx_w, lnx_b, w_out, g2, e_w)


def _acc(ref, val, first):
    @pl.when(first)
    def _():
        ref[...] = val

    @pl.when(jnp.logical_not(first))
    def _():
        ref[...] += val


def _mlp_fwd_bwd(x1, tgt, mod345, g3, g4, w1, w2, cfg):
    B, TX, D, TT, F, JX = cfg.B, cfg.TX, cfg.D, cfg.TT, cfg.F, cfg.JX

    def body(x1_ref, t_ref, mod_ref, g3_ref, g4_ref, w1_ref, w2_ref,
             dx1_ref, loss_ref, h2_ref, dpre_ref, act_ref, dff_ref, dmod_ref, dg3_ref, dg4_ref):
        b, j = pl.program_id(0), pl.program_id(1)
        x1v = x1_ref[0]
        sh, sc, gt = mod_ref[0, 0:1, :], mod_ref[0, 1:2, :], mod_ref[0, 2:3, :]
        h2, vjp_pre = jax.vjp(_rms_mod, x1v, g3_ref[...], sh, sc)
        h2b = h2.astype(BF16)
        pre = jnp.dot(h2b, w1_ref[...], preferred_element_type=F32)
        rl = jnp.maximum(pre, 0.0)
        actb = (rl * rl).astype(BF16)
        ff = jnp.dot(actb, w2_ref[...], preferred_element_type=F32)
        x2, vjp_post = jax.vjp(_post_res, x1v, ff, gt, g4_ref[...])
        err = x2 - t_ref[0]
        loss = 0.5 * jnp.sum(jnp.mean(err * err, axis=-1, keepdims=True))
        dx1a, dff, dgt, dg4 = vjp_post(err * (1.0 / D))
        dffb = dff.astype(BF16)
        dpre = _bdot_nt(dffb, w2_ref[...]) * (2.0 * rl)
        dpreb = dpre.astype(BF16)
        dx1b, dg3, dsh, dsc = vjp_pre(_bdot_nt(dpreb, w1_ref[...]))
        dx1_ref[0] = dx1a + dx1b
        loss_ref[0, 0] = jnp.zeros((SUBLANES, LANES), F32) + loss
        h2_ref[0] = h2b
        dpre_ref[0] = dpreb
        act_ref[0] = actb
        dff_ref[0] = dffb
        _acc(dmod_ref, jnp.concatenate([dsh, dsc, dgt], axis=0)[None], j == 0)
        first = jnp.logical_and(b == 0, j == 0)
        _acc(dg3_ref, dg3, first)
        _acc(dg4_ref, dg4, first)

    tile = lambda n: pl.BlockSpec((1, TT, n), lambda b, j: (b, j, 0))
    return pl.pallas_call(
        body, name="mlp_fwd_bwd", grid=(B, JX),
        in_specs=[tile(D), tile(D), pl.BlockSpec((1, 3, D), lambda b, j: (b, 0, 0)),
                  _full_spec((1, D)), _full_spec((1, D)),
                  pl.BlockSpec((D, F), lambda b, j: (0, 0), pipeline_mode=pl.Buffered(1)),
                  pl.BlockSpec((F, D), lambda b, j: (0, 0), pipeline_mode=pl.Buffered(1))],
        out_specs=[tile(D), pl.BlockSpec((1, 1, SUBLANES, LANES), lambda b, j: (b, j, 0, 0)),
                   tile(D), tile(F), tile(F), tile(D),
                   pl.BlockSpec((1, 3, D), lambda b, j: (b, 0, 0)),
                   _full_spec((1, D)), _full_spec((1, D))],
        out_shape=[jax.ShapeDtypeStruct((B, TX, D), F32),
                   jax.ShapeDtypeStruct((B, JX, SUBLANES, LANES), F32),
                   jax.ShapeDtypeStruct((B, TX, D), BF16), jax.ShapeDtypeStruct((B, TX, F), BF16),
                   jax.ShapeDtypeStruct((B, TX, F), BF16), jax.ShapeDtypeStruct((B, TX, D), BF16),
                   jax.ShapeDtypeStruct((B, 3, D), F32),
                   jax.ShapeDtypeStruct((1, D), F32), jax.ShapeDtypeStruct((1, D), F32)],
        compiler_params=_params(("arbitrary", "arbitrary")),
    )(x1, tgt, mod345, g3, g4, w1, w2)


def _mix_out_bwd(yf, yb, kdf, kdb, r, v, gd, conv, x, mod2, r_k, gw2p, lnx_w, lnx_b, w_out, g2, e_w, dx1, cfg):
    B, TX, D, TT, W, CW, JC, HP, GP = cfg.B, cfg.TX, cfg.D, cfg.TT, cfg.W, cfg.CW, cfg.JC, cfg.HP, cfg.GP

    def body(yf_ref, yb_ref, kdf_ref, kdb_ref, r_ref, v_ref, gd_ref, cv_ref, x_ref, mod_ref,
             rk_ref, gw_ref, lw_ref, lb_ref, wo_ref, g_ref, e_ref, dx1_ref,
             dy_ref, dkb_ref, dr_ref, dv_ref, dgd_ref, dcv_ref, cat_ref, dmix_ref,
             dmod_ref, dg2_ref, drk_ref, dgw_ref, dlw_ref, dlb_ref):
        b, j = pl.program_id(0), pl.program_id(1)
        e = e_ref[...]
        y = _from_heads(yf_ref, cfg) + _from_heads(yb_ref, cfg)
        kbar = 0.5 * (_from_heads(kdf_ref, cfg) + _from_heads(kdb_ref, cfg))
        ro, vjp_ro = jax.vjp(lambda *a: _readout(*a, e), y, kbar, _from_heads(r_ref, cfg),
                             _from_heads(v_ref, cfg), gd_ref[0], rk_ref[...], gw_ref[...], lw_ref[...], lb_ref[...])
        catb = jnp.concatenate([ro, cv_ref[0]], axis=-1).astype(BF16)
        mix = jnp.dot(catb, wo_ref[...], preferred_element_type=F32)
        _, vjp_post = jax.vjp(_post_res, x_ref[0], mix, mod_ref[0], g_ref[...])
        _, dmix, dgate, dg2 = vjp_post(dx1_ref[0])
        dmixb = dmix.astype(BF16)
        dcat = _bdot_nt(dmixb, wo_ref[...])
        dy, dkb, dr, dv, dgd, drk, dgw, dlw, dlb = vjp_ro(dcat[:, :W])
        _to_heads(dy_ref, dy, cfg)
        _to_heads(dkb_ref, dkb, cfg)
        _to_heads(dr_ref, dr, cfg)
        _to_heads(dv_ref, dv, cfg)
        dgd_ref[0] = dgd
        dcv_ref[0] = dcat[:, W:]
        cat_ref[0] = catb
        dmix_ref[0] = dmixb
        _acc(dmod_ref, dgate[None], j == 0)
        first = jnp.logical_and(b == 0, j == 0)
        _acc(dg2_ref, dg2, first)
        _acc(drk_ref, drk, first)
        _acc(dgw_ref, dgw, first)
        _acc(dlw_ref, dlw, first)
        _acc(dlb_ref, dlb, first)

    hs = _head_spec(cfg, lambda j: j + JC)
    ho = _head_spec(cfg, lambda j: j)
    lat = lambda n: pl.BlockSpec((1, TT, n), lambda b, j: (b, j + JC, 0))
    tile = lambda n: pl.BlockSpec((1, TT, n), lambda b, j: (b, j, 0))
    head = jax.ShapeDtypeStruct((B, HP, TX, LANES), F32)
    vec = lambda n: jax.ShapeDtypeStruct((1, n), F32)
    return pl.pallas_call(
        body, name="mix_out_bwd", grid=(B, cfg.JX),
        in_specs=[hs] * 6 + [lat(GP), lat(CW), tile(D),
                             pl.BlockSpec((1, 1, D), lambda b, j: (b, 0, 0)),
                             _full_spec((1, W)), _full_spec((GP, W)), _full_spec((1, W)), _full_spec((1, W)),
                             _full_spec((W + CW, D)), _full_spec((1, D)), _full_spec(e_w.shape), tile(D)],
        out_specs=[ho] * 4 + [tile(GP), tile(CW), tile(W + CW), tile(D),
                              pl.BlockSpec((1, 1, D), lambda b, j: (b, 0, 0)),
                              _full_spec((1, D)), _full_spec((1, W)), _full_spec((GP, W)),
                              _full_spec((1, W)), _full_spec((1, W))],
        out_shape=[head] * 4 + [jax.ShapeDtypeStruct((B, TX, GP), F32), jax.ShapeDtypeStruct((B, TX, CW), F32),
                                jax.ShapeDtypeStruct((B, TX, W + CW), BF16), jax.ShapeDtypeStruct((B, TX, D), BF16),
                                jax.ShapeDtypeStruct((B, 1, D), F32),
                                vec(D), vec(W), jax.ShapeDtypeStruct((GP, W), F32), vec(W), vec(W)],
        compiler_params=_params(("arbitrary", "arbitrary")),
    )(yf, yb, kdf, kdb, r, v, gd, conv, x, mod2, r_k, gw2p, lnx_w, lnx_b, w_out, g2, e_w, dx1)


def _scan_bwd(ops_f, ops_b, dy, hist, e128, e256, cfg):
    G, T, NCH, NCC = cfg.G, cfg.T, cfg.NCH, cfg.NCC
    CH = SCAN_CHUNK
    G2 = 2 * G

    def body(*refs):
        ins = (refs[0:6], refs[6:12])
        dys, hist_ref, next_ref = (refs[12], refs[13]), refs[14], refs[15]
        ahead, hist_ahead = (refs[16:19], refs[19:22]), refs[22]
        e_ref, e2_ref = refs[23], refs[24]
        outs = (refs[25:31], refs[31:37])
        ds_ref, mm_ref = refs[37], refs[38]
        gi = pl.program_id(0)
        diag = _diag_mask()
        diag_b = diag.astype(BF16)
        e, e2 = e_ref[...], e2_ref[...]
        rows = functools.partial(_both_rows, ins)
        latent = [(_chunk_pos(NCH - 1 - gi, d == 1, cfg) >= NCC).astype(F32) for d in range(2)]
        latent_ahead = [(_chunk_pos(jnp.maximum(NCH - 2 - gi, 0), d == 1, cfg) >= NCC).astype(F32) for d in range(2)]

        def put(idx, i, val):
            outs[0][idx][:, pl.ds(_tok(i, False), 1), :] = val[:G]
            outs[1][idx][:, pl.ds(_tok(i, True), 1), :] = val[G:]

        rsum = lambda z: jnp.sum(z, axis=1, keepdims=True)

        def at(i):
            dy_rows = jnp.concatenate([dys[d][:, pl.ds(_tok(i, d == 1), 1), :] * latent[d] for d in range(2)], axis=0)
            return dy_rows, rows(3, i), rows(4, i), hist_ref[i]

        def at_ahead():
            edge = lambda d: 0 if d == 1 else SUBLANES - 1
            pick = lambda k, scale: jnp.concatenate(
                [ahead[d][k][:, edge(d):edge(d) + 1, :] * scale[d] for d in range(2)], axis=0)
            return pick(0, latent_ahead), pick(1, (1.0, 1.0)), pick(2, (1.0, 1.0)), hist_ahead[0]

        def prepare(dy_rows, v_rows, kk_rows, s_before):
            return (_col_form(dy_rows), _col_dot([v_rows], diag_b, e2)[0],
                    s_before.astype(BF16) * (-kk_rows).astype(BF16))

        @pl.when(gi == 0)
        def _():
            ds_ref[...] = jnp.zeros_like(ds_ref)
            dyc0, vb0, sa_lhs = prepare(*at(CH - 1))
            mm_ref[0] = dyc0
            mm_ref[1] = vb0
            mm_ref[2] = _seg_dot([sa_lhs], e)[0]

        def one_step(i, s_after, coming):
            sp, dyc = hist_ref[i], mm_ref[0]
            ds = ds_ref[...] + dyc * rows(0, i)
            put(0, i, rsum(s_after * dyc))
            put(1, i, rsum(ds * sp))
            put(5, i, rsum(ds * mm_ref[2]))
            put(2, i, rsum(ds * mm_ref[1]))
            dyc_n, vb_n, sa_lhs_n = prepare(*coming)
            dsb = ds.astype(BF16)
            res = _seg_dot([dsb * rows(5, i).astype(BF16), dsb * rows(2, i).astype(BF16), sa_lhs_n], e)
            dsa = res[0]
            put(4, i, -rsum(sp * dsa))
            put(3, i, rsum(diag * res[1]))
            mm_ref[0] = dyc_n
            mm_ref[1] = vb_n
            mm_ref[2] = res[2]
            ds_ref[...] = ds * rows(1, i) - dsa * rows(4, i)

        one_step(CH - 1, next_ref[0], at(CH - 2))

        def bstep(ii, carry):
            i = CH - 1 - ii
            one_step(i, hist_ref[i + 1], at(i - 1))
            return carry

        lax.fori_loop(1, CH - 1, bstep, 0)
        one_step(0, hist_ref[1], at_ahead())

    per = CH // SUBLANES
    cpos = lambda g, rev: _chunk_pos(NCH - 1 - g, rev, cfg)
    apos = lambda g, rev: _chunk_pos(jnp.maximum(NCH - 2 - g, 0), rev, cfg)
    toks = [pl.BlockSpec((G, CH, LANES), lambda g, rev=rev: (0, cpos(g, rev), 0)) for rev in (False, True)]
    dy_specs = [pl.BlockSpec((G, CH, LANES), lambda g, rev=rev: (0, jnp.maximum(cpos(g, rev) - NCC, 0), 0))
                for rev in (False, True)]
    edge_blk = lambda rev: 0 if rev else per - 1
    ahead_tok = [pl.BlockSpec((G, SUBLANES, LANES), lambda g, rev=rev: (0, apos(g, rev) * per + edge_blk(rev), 0))
                 for rev in (False, True)]
    ahead_dy = [pl.BlockSpec((G, SUBLANES, LANES),
                             lambda g, rev=rev: (0, jnp.maximum(apos(g, rev) - NCC, 0) * per + edge_blk(rev), 0))
                for rev in (False, True)]
    state_row = lambda f: pl.BlockSpec((1, G2, HEAD, LANES), lambda g: (f(g), 0, 0, 0))
    out = jax.ShapeDtypeStruct((G, T, LANES), F32)
    res = pl.pallas_call(
        body, name="scan_bwd", grid=(NCH,),
        in_specs=[toks[0]] * 6 + [toks[1]] * 6 + dy_specs
                 + [pl.BlockSpec((CH, G2, HEAD, LANES), lambda g: (NCH - 1 - g, 0, 0, 0)),
                    state_row(lambda g: (NCH - g) * CH)]
                 + [ahead_dy[0], ahead_tok[0], ahead_tok[0], ahead_dy[1], ahead_tok[1], ahead_tok[1],
                    state_row(lambda g: jnp.maximum((NCH - 1 - g) * CH - 1, 0)),
                    _full_spec((LANES, LANES)), _full_spec((2 * LANES, LANES))],
        out_specs=[toks[0]] * 6 + [toks[1]] * 6,
        out_shape=[out] * 12,
        scratch_shapes=[pltpu.VMEM((G2, HEAD, LANES), F32), pltpu.VMEM((3, G2, HEAD, LANES), F32)],
        compiler_params=_params(("arbitrary",)),
    )(*ops_f, *ops_b, dy, dy, hist, hist,
      dy, ops_f[3], ops_f[4], dy, ops_b[3], ops_b[4], hist, e128, e256)
    return res[:6], res[6:]


def _mix_prep_bwd(p, mup, mun, w0, w2p, a0, a2p, k_k, k_a, wm, cb, clw, clb, e_w, yconv, sf, sb, ro, xch, cfg):
    B, T, TT, SP, CP, W, CW, HP, JC, PW, GP, KC, KP = (cfg.B, cfg.T, cfg.TT, cfg.SP, cfg.CP, cfg.W, cfg.CW,
                                                       cfg.HP, cfg.JC, cfg.PW, cfg.GP, cfg.KC, cfg.KP)
    pad = KC // 2
    nx = len(xch)

    def body(p_ref, prev_ref, next_ref, mup_ref, mun_ref, w0_ref, w2_ref, a0_ref, a2_ref, kk_ref, ka_ref,
             wm_ref, cb_ref, clw_ref, clb_ref, e_ref, yc_ref, *rest):
        sf_refs, sb_refs = rest[0:6], rest[6:12]
        rdr_ref, rdv_ref, rdkb_ref, rdgd_ref, rdcv_ref = rest[12:17]
        xin, rest = rest[17:17 + nx], rest[17 + nx:]
        (dpz_ref, dmup_ref, dmun_ref, dw0_ref, dw2_ref, da0_ref, da2_ref, dkk_ref, dka_ref,
         dcw_ref, dcb_ref, dclw_ref, dclb_ref) = rest[:13]
        xout, (dyc_ref, du_ref), sems = rest[13:13 + nx], rest[13 + nx:15 + nx], rest[15 + nx:]
        b, j = pl.program_id(0), pl.program_id(1)
        first = jnp.logical_and(b == 0, j == 0)

        @pl.when(first)
        def _():
            _exchange_start(_exchange_copies(xin, xout, *sems, [True] * nx))
        lat = (j >= JC).astype(F32)
        e = e_ref[...]
        mup_v, mun_v = mup_ref[...], mun_ref[...]
        z, zp, zn, rw = _shifted(p_ref, prev_ref, next_ref, mup_v, mun_v, j, cfg)

        def prep(rw_, w0_, w2_, a0_, a2_, kk_, ka_):
            return _rwkv_prep(rw_, w0_, w2_, a0_, a2_, kk_, ka_, e, cfg) + (rw_[:, 3 * W + 2 * PW:SP],)

        _, vjp_prep = jax.vjp(prep, rw, w0_ref[...], w2_ref[...], a0_ref[...], a2_ref[...], kk_ref[...], ka_ref[...])
        fr, fw, fk, fv, fkk, fb = [_from_heads(r_, cfg) for r_ in sf_refs]
        br, bw, bk, bv, bkk, bb = [_from_heads(r_, cfg) for r_ in sb_refs]
        half_kb = (0.5 * lat) * _from_heads(rdkb_ref, cfg)
        cots = (fr + br + lat * _from_heads(rdr_ref, cfg), fv + bv + lat * _from_heads(rdv_ref, cfg), fkk + bkk,
                fw, fk + half_kb, fb, bw, bk + half_kb, bb, lat * rdgd_ref[0])
        drw, dw0, dw2, da0, da2, dkk, dka = vjp_prep(cots)
        _acc(dmup_ref, jnp.sum(drw * (zp - z), axis=0, keepdims=True), first)
        _acc(dmun_ref, jnp.sum(drw * (zn - z), axis=0, keepdims=True), first)
        for ref, val in ((dw0_ref, dw0), (dw2_ref, dw2), (da0_ref, da0), (da2_ref, da2), (dkk_ref, dkk), (dka_ref, dka)):
            _acc(ref, val, first)

        dpz_ref[0, :, 0:SP] = drw

        @pl.when(first)
        def _():
            for ref in (dcw_ref, dcb_ref, dclw_ref, dclb_ref):
                ref[...] = jnp.zeros_like(ref)

        @pl.when(j < JC)
        def _():
            dpz_ref[0, :, SP:] = jnp.zeros((TT, 2 * CW), F32)

        @pl.when(j >= JC)
        def _():
            u, vjp_glu = jax.vjp(lambda c_: _glu(c_, cfg), p_ref[0, :, SP:])
            _, vjp_post = jax.vjp(_conv_post, yc_ref[0], cb_ref[...], clw_ref[...], clb_ref[...])
            dyc, dcb, dclw, dclb = vjp_post(rdcv_ref[0])
            dyc_ref[...] = dyc
            _conv_lines(dyc_ref, wm_ref, du_ref, KC, True)
            (dcv,) = vjp_glu(du_ref[...])
            dpz_ref[0, :, SP:] = dcv
            for i in range(KC):
                dcw_ref[i:i + 1, :] += jnp.sum(dyc * _line_shift(u, i - pad), axis=0, keepdims=True)
            dcb_ref[...] += dcb
            dclw_ref[...] += dclw
            dclb_ref[...] += dclb

        @pl.when(jnp.logical_and(b == B - 1, j == cfg.J - 1))
        def _():
            _exchange_wait(_exchange_copies(xin, xout, *sems, [True] * nx))

    prev, nxt = _halo_specs(cfg, SP)
    hs = _head_spec(cfg, lambda j: j)
    hl = _head_spec(cfg, lambda j: jnp.maximum(j - JC, 0))
    latn = lambda n: pl.BlockSpec((1, TT, n), lambda b, j: (b, jnp.maximum(j - JC, 0), 0))
    hbm = pl.BlockSpec(memory_space=pltpu.HBM)
    vec = lambda n: jax.ShapeDtypeStruct((1, n), F32)
    small_shapes = [vec(SP), vec(SP), vec(2 * W), jax.ShapeDtypeStruct((PW, 2 * W), F32), vec(2 * W),
                    jax.ShapeDtypeStruct((PW, 2 * W), F32), vec(W), vec(W),
                    jax.ShapeDtypeStruct((KP, CW), F32), vec(CW), vec(CW), vec(CW)]
    return pl.pallas_call(
        body, name="mix_prep_bwd", grid=(B, cfg.J),
        in_specs=[pl.BlockSpec((1, TT, CP), lambda b, j: (b, j, 0)), prev, nxt,
                  _full_spec((1, SP)), _full_spec((1, SP)),
                  _full_spec((1, 2 * W)), _full_spec((PW, 2 * W)),
                  _full_spec((1, 2 * W)), _full_spec((PW, 2 * W)),
                  _full_spec((1, W)), _full_spec((1, W)),
                  pl.BlockSpec((KC, LINE, CW), lambda b, j: (0, 0, 0), pipeline_mode=pl.Buffered(1)),
                  _full_spec((1, CW)), _full_spec((1, CW)), _full_spec((1, CW)),
                  _full_spec(e_w.shape), pl.BlockSpec((1, TT, CW), lambda b, j: (b, j, 0))]
                 + [hs] * 12 + [hl] * 3 + [latn(GP), latn(CW)] + [hbm] * nx,
        out_specs=[pl.BlockSpec((1, TT, CP), lambda b, j: (b, j, 0))] + [_full_spec(s.shape) for s in small_shapes]
                  + [hbm] * nx,
        out_shape=[jax.ShapeDtypeStruct((B, T, CP), F32)] + small_shapes + _exchange_shapes(xch, [True] * nx),
        scratch_shapes=[pltpu.VMEM((TT, CW), F32), pltpu.VMEM((TT, CW), F32)] + _exchange_sems(nx),
        compiler_params=_params(("arbitrary", "arbitrary")),
    )(p, p, p, mup, mun, w0, w2p, a0, a2p, k_k, k_a, wm, cb, clw, clb, e_w, yconv, *sf, *sb, *ro, *xch)


def _in_proj_bwd(dpz, xcat, modt, g1, w_in_p, mup, mun, dx1, cfg):
    B, T, TX, D, TT, SP, CP, JC = cfg.B, cfg.T, cfg.TX, cfg.D, cfg.TT, cfg.SP, cfg.CP, cfg.JC

    def body(d_ref, prev_ref, next_ref, x_ref, mod_ref, g_ref, w_ref, mup_ref, mun_ref, dx1_ref,
             gx_ref, dp_ref, dmod_ref, dg_ref):
        b, j = pl.program_id(0), pl.program_id(1)
        has_prev, has_next = _halo_flags(j, cfg)
        mp, mn = mup_ref[...], mun_ref[...]
        drw = d_ref[0, :, 0:SP]
        dprev, dnext = _shift_rows(drw, prev_ref[0, SUBLANES - 1:SUBLANES, :] * has_prev,
                                   next_ref[0, 0:1, :] * has_next)
        dz = drw * (1.0 - mp - mn) + mp * dnext + mn * dprev
        dpb = jnp.concatenate([dz, d_ref[0, :, SP:]], axis=-1).astype(BF16)
        dp_ref[0] = dpb
        dh = _bdot_nt(dpb, w_ref[...])
        _, vjp_h = jax.vjp(_rms_mod, x_ref[0], g_ref[...], mod_ref[0, 0, 0:1, :], mod_ref[0, 0, 1:2, :])
        dx, dg, dsh, dsc = vjp_h(dh)
        dmod_ref[0, 0] = jnp.concatenate([dsh, dsc], axis=0)
        _acc(dg_ref, dg, jnp.logical_and(b == 0, j == 0))

        @pl.when(j >= JC)
        def _():
            gx_ref[0] = dx + dx1_ref[0]

    prev, nxt = _halo_specs(cfg, SP)
    lat = pl.BlockSpec((1, TT, D), lambda b, j: (b, jnp.maximum(j - JC, 0), 0))
    return pl.pallas_call(
        body, name="in_proj_bwd", grid=(B, cfg.J),
        in_specs=[pl.BlockSpec((1, TT, CP), lambda b, j: (b, j, 0)), prev, nxt,
                  pl.BlockSpec((1, TT, D), lambda b, j: (b, j, 0)),
                  pl.BlockSpec((1, 1, 2, D), lambda b, j: (b, j, 0, 0)),
                  _full_spec((1, D)), _full_spec((D, CP)), _full_spec((1, SP)), _full_spec((1, SP)), lat],
        out_specs=[lat, pl.BlockSpec((1, TT, CP), lambda b, j: (b, j, 0)),
                   pl.BlockSpec((1, 1, 2, D), lambda b, j: (b, j, 0, 0)), _full_spec((1, D))],
        out_shape=[jax.ShapeDtypeStruct((B, TX, D), F32), jax.ShapeDtypeStruct((B, T, CP), BF16),
                   jax.ShapeDtypeStruct((B, cfg.J, 2, D), F32), jax.ShapeDtypeStruct((1, D), F32)],
        compiler_params=_params(("arbitrary", "arbitrary")),
    )(dpz, dpz, dpz, xcat, modt, g1, w_in_p, mup, mun, dx1)


def _pick_tile(n, pref):
    for t in pref:
        if n % t == 0:
            return t
    return n


def _grad_matmul(a, g, name):
    K, M = a.shape
    N = g.shape[1]
    tm = _pick_tile(M, (512, 256, 128))
    tn = _pick_tile(N, (1024, 768, 512, 256, 128))
    tk = _pick_tile(K, (2048, 1024, 512, 256, 128, 64))
    nk = K // tk

    def body(a_ref, g_ref, o_ref):
        k = pl.program_id(2)
        _acc(o_ref, _bdot_tn(a_ref[...], g_ref[...]), k == 0)

    return pl.pallas_call(
        body, name=name, grid=(M // tm, N // tn, nk),
        in_specs=[pl.BlockSpec((tk, tm), lambda i, j, k: (k, i)),
                  pl.BlockSpec((tk, tn), lambda i, j, k: (k, j))],
        out_specs=pl.BlockSpec((tm, tn), lambda i, j, k: (i, j)),
        out_shape=jax.ShapeDtypeStruct((M, N), F32),
        compiler_params=_params(("parallel", "parallel", "arbitrary")),
    )(a, g)


def _ada_fwd(crows, ada_w, ada_b):
    D = crows.shape[1]
    n6 = ada_w.shape[1]
    tn = _pick_tile(n6, (1024, 512, 256, 128))

    def body(c_ref, w_ref, b_ref, s_ref, m_ref):
        s = _silu(c_ref[...])
        s_ref[...] = s
        m_ref[...] = _bdot(s, w_ref[...]) + b_ref[...]

    return pl.pallas_call(
        body, name="ada_fwd", grid=(n6 // tn,),
        in_specs=[_full_spec((SUBLANES, D)), pl.BlockSpec((D, tn), lambda i: (0, i)),
                  pl.BlockSpec((1, tn), lambda i: (0, i))],
        out_specs=[_full_spec((SUBLANES, D)), pl.BlockSpec((SUBLANES, tn), lambda i: (0, i))],
        out_shape=[jax.ShapeDtypeStruct((SUBLANES, D), F32), jax.ShapeDtypeStruct((SUBLANES, n6), F32)],
        compiler_params=_params(("arbitrary",)),
    )(crows, ada_w, ada_b)


def _ada_bwd(s_all, g_all, g_mine, c_ctx, ada_w, nb):
    D = s_all.shape[1]
    n6 = g_all.shape[1]
    ns = g_mine.shape[1]

    def body(s_ref, g_ref, gm_ref, c_ref, w_ref, dw_ref, db_ref, dc_ref):
        g = g_ref[...]
        dw_ref[...] = _bdot_tn(s_ref[...], gm_ref[...])
        db_ref[...] = jnp.sum(g, axis=0, keepdims=True)
        rows = lax.broadcasted_iota(jnp.int32, (g.shape[0], 1), 0)
        gc = jnp.sum(jnp.where(rows % SUBLANES == nb, g, 0.0), axis=0, keepdims=True)
        ds = _bdot_nt(gc, w_ref[...])
        c = c_ref[...]
        sg = _sigmoid(c)
        dc_ref[...] = ds * (sg + c * sg * (1.0 - sg))

    return pl.pallas_call(
        body, name="ada_bwd",
        out_shape=[jax.ShapeDtypeStruct((D, ns), F32), jax.ShapeDtypeStruct((1, n6), F32),
                   jax.ShapeDtypeStruct((1, D), F32)],
        compiler_params=_params(),
    )(s_all, g_all, g_mine, c_ctx, ada_w)


def _adamw(parts, w, m, v, name):
    P, R, C = parts.shape
    small = R * C * (P + 7) * 4 <= 4 * 1024 * 1024
    tr = R if small else _pick_tile(R, (256, 128, 64, 32, 16, 8))

    def body(p_ref, w_ref, m_ref, v_ref, g_ref, d_ref, nm_ref, nv_ref):
        g = p_ref[0].astype(F32)
        for i in range(1, P):
            g = g + p_ref[i].astype(F32)
        nm = ADAM_B1 * m_ref[...] + (1.0 - ADAM_B1) * g
        nv = ADAM_B2 * v_ref[...] + (1.0 - ADAM_B2) * (g * g)
        m_hat = nm / (1.0 - ADAM_B1 ** ADAM_STEP)
        v_hat = nv / (1.0 - ADAM_B2 ** ADAM_STEP)
        g_ref[...] = g
        d_ref[...] = -ADAM_LR * (m_hat / (jnp.sqrt(v_hat) + ADAM_EPS) + ADAM_WD * w_ref[...])
        nm_ref[...] = nm
        nv_ref[...] = nv

    blk = pl.BlockSpec((tr, C), lambda i: (i, 0))
    out = jax.ShapeDtypeStruct((R, C), F32)
    return pl.pallas_call(
        body, name=name, grid=(R // tr,),
        in_specs=[pl.BlockSpec((P, tr, C), lambda i: (0, i, 0)), blk, blk, blk],
        out_specs=[blk] * 4, out_shape=[out] * 4,
        compiler_params=_params(("parallel",)),
    )(parts, w, m, v)


def _local_step(cfg, x, c, ctx, tgt, fw, late):
    B, D, W, CW, JC, T, TX = cfg.B, cfg.D, cfg.W, cfg.CW, cfg.JC, cfg.T, cfg.TX
    e_w = _block_ones(min(W, MXU_DIM))
    e128 = _block_ones(LANES)
    e256 = jnp.concatenate([e128, e128], axis=0)
    row = lambda a: a.reshape(1, -1)

    ada_wb = fw["ada_w"].astype(BF16)
    w_in_p = _pad_cols(fw["w_in"], cfg).astype(BF16)
    mup = _pad_cols(fw["mu_prev"], cfg, True)
    mun = _pad_cols(fw["mu_next"], cfg, True)
    w0, a0 = row(fw["decay_w0"]), row(fw["iclr_a0"])
    w2p, a2p = _pair_weight(fw["decay_w2"], cfg), _pair_weight(fw["iclr_a2"], cfg)
    wm_fwd, wm_bwd = _conv_tables(fw["conv_w"], cfg.KC)
    gw2p = jnp.pad(fw["gate_w2"], ((0, cfg.GP - cfg.GR), (0, 0)))
    r_k = row(fw["r_k"])

    crows = jnp.concatenate([c, fw["c_ctx"], jnp.zeros((SUBLANES - B - 1, D), F32)], axis=0)
    s_rows, mods = _ada_fwd(crows, ada_wb, fw["ada_b"])
    mod_x = mods[:B].reshape(B, 6, D)
    mod_c = mods[B].reshape(6, D)
    modt = jnp.concatenate([jnp.broadcast_to(mod_c[None, None, 0:2], (B, JC, 2, D)),
                            jnp.broadcast_to(mod_x[:, None, 0:2], (B, cfg.JX, 2, D))], axis=1)
    mod2, mod345 = mod_x[:, 2:3], mod_x[:, 3:6]

    xcat = jnp.concatenate([ctx, x], axis=1)
    p, hb = _in_proj(xcat, modt, fw["mix_pre_g"], w_in_p, cfg)
    prep_w = (mup, mun, w0, w2p, a0, a2p, fw["k_k"], fw["k_a"])
    conv_w = (fw["conv_b"], fw["conv_ln_w"], fw["conv_ln_b"], e_w)
    (r, v, kk, w_f, kd_f, b_f, w_b, kd_b, b_b, gd, conv, yconv, g_w_out, g_w1, g_w2) = _mix_prep(
        p, *prep_w, wm_fwd, *conv_w, late, cfg)
    w_outb, w1b, w2b = g_w_out.reshape(-1, D), _blocks_to_cols(g_w1), g_w2.reshape(-1, D)
    flat = lambda a: a.reshape(cfg.G, a.shape[2], LANES)
    heads = lambda a: a.reshape(B, cfg.HP, a.shape[1], LANES)
    ops_f = tuple(flat(a) for a in (r, w_f, kd_f, v, kk, b_f))
    ops_b = tuple(flat(a) for a in (r, w_b, kd_b, v, kk, b_b))
    y_f, y_b, hist, s_fin = _scan_fwd(ops_f, ops_b, e128, e256, cfg)
    hist = lax.dynamic_update_slice_in_dim(hist, s_fin[None], cfg.NCH * SCAN_CHUNK, axis=0)
    out_args = (heads(y_f), heads(y_b), kd_f, kd_b, r, v, gd, conv, x, mod2, r_k, gw2p, fw["lnx_w"], fw["lnx_b"],
                w_outb, fw["mix_post_g"], e_w)
    x1 = _mix_out(*out_args, cfg)

    dx1, loss_t, h2b, dpreb, actb, dffb, dmod345, dg3, dg4 = _mlp_fwd_bwd(
        x1, tgt, mod345, fw["mlp_pre_g"], fw["mlp_post_g"], w1b, w2b, cfg)
    (dy, dkb, dr_c, dv_c, dgd, dconv, catb, dmixb, dmod2, dg2, drk, dgw, dlw, dlb) = _mix_out_bwd(
        *out_args, dx1, cfg)
    sf, sb = _scan_bwd(ops_f, ops_b, flat(dy), hist, e128, e256, cfg)

    tokens = lambda a: a.reshape(-1, a.shape[-1])
    d_w_out = _grad_matmul(tokens(catb), tokens(dmixb), "grad_w_out")
    d_w1 = _grad_matmul(tokens(h2b), tokens(dpreb), "grad_mlp_w1")
    d_w2 = _grad_matmul(tokens(actb), tokens(dffb), "grad_mlp_w2")
    early = [d_w_out.astype(BF16).reshape(N_DEV, -1, D), _cols_to_blocks(d_w1.astype(BF16)),
             d_w2.astype(BF16).reshape(N_DEV, -1, D)]
    (dpz, dmup, dmun, dw0, dw2p, da0, da2p, dkk, dka, dcw, dcb, dclw, dclb, x_w_out, x_w1, x_w2) = _mix_prep_bwd(
        p, *prep_w, wm_bwd, *conv_w, yconv, [heads(a) for a in sf], [heads(a) for a in sb],
        (dr_c, dv_c, dkb, dgd, dconv), early, cfg)
    grad_x, dpb, dmodt, dg1 = _in_proj_bwd(dpz, xcat, modt, fw["mix_pre_g"], w_in_p, mup, mun, dx1, cfg)
    d_w_in = _grad_matmul(tokens(hb), tokens(dpb), "grad_w_in")
    exchanged = {"w_out": x_w_out, "mlp_w1": x_w1, "mlp_w2": x_w2}

    grads = {
        "mix_pre_g": dg1, "mix_post_g": dg2, "mlp_pre_g": dg3, "mlp_post_g": dg4,
        "w_in": _unpad_cols(d_w_in, cfg),
        "mu_prev": _unpad_cols(dmup, cfg, True), "mu_next": _unpad_cols(dmun, cfg, True),
        "decay_w0": dw0.reshape(2, W), "decay_w2": _unpair_weight(dw2p, cfg),
        "iclr_a0": da0.reshape(2, W), "iclr_a2": _unpair_weight(da2p, cfg),
        "k_k": dkk, "k_a": dka, "r_k": drk.reshape(fw["r_k"].shape),
        "gate_w2": dgw[:cfg.GR], "lnx_w": dlw, "lnx_b": dlb,
        "conv_w": dcw[:cfg.KC], "conv_b": dcb, "conv_ln_w": dclw, "conv_ln_b": dclb,
    }
    dmod_x = jnp.concatenate([jnp.sum(dmodt[:, JC:], axis=1), dmod2, dmod345], axis=1).reshape(B, 6 * D)
    dmod_c = jnp.concatenate([jnp.sum(dmodt[:, :JC], axis=(0, 1)), jnp.zeros((4, D), F32)], axis=0).reshape(1, 6 * D)
    g_rows = jnp.concatenate([dmod_x, dmod_c, jnp.zeros((SUBLANES - B - 1, 6 * D), F32)], axis=0)
    return loss_t, grad_x, grads, exchanged, s_rows, g_rows


def _my_index():
    return 4 * lax.axis_index("x") + 2 * lax.axis_index("y") + lax.axis_index("c")


def _gather_phase(phase, ins, outs, send_sems, recv_sems, local_sems):
    n = len(ins)
    x, y, c = lax.axis_index("x"), lax.axis_index("y"), lax.axis_index("c")
    index = lambda px, py, pc: 4 * px + 2 * py + pc
    me, sibling = (x, y, c), (x, y, 1 - c)
    chips = [(1 - x, y), (x, 1 - y), (1 - x, 1 - y)]

    def copy(a, k, block, to, src=None):
        dst = outs[a].at[index(*block)]
        return pltpu.make_async_remote_copy(
            src_ref=dst if src is None else src, dst_ref=dst,
            send_sem=send_sems.at[k, a], recv_sem=recv_sems.at[k, a],
            device_id=to, device_id_type=pl.DeviceIdType.MESH)

    def local():
        return [pltpu.make_async_copy(ins[a], outs[a].at[index(*me)], local_sems.at[a]) for a in range(n)]

    def first():
        return [cp for a in range(n) for cp in
                [copy(a, 0, me, sibling, src=ins[a])]
                + [copy(a, 1 + j, me, (*chip, c), src=ins[a]) for j, chip in enumerate(chips)]]

    def passed():
        return [copy(a, 4 + j, (*chip, c), sibling) for j, chip in enumerate(chips) for a in range(n)]

    if phase == 0:
        for cp in local() + first():
            cp.start()
    elif phase == 1:
        arrived = [copy(a, 1 + j, (*chip, c), me) for j, chip in enumerate(chips) for a in range(n)]
        for got, fwd in zip(arrived, passed()):
            got.wait_recv()
            fwd.start()
    else:
        for a in range(n):
            copy(a, 0, sibling, me).wait_recv()
            for j, chip in enumerate(chips):
                copy(a, 4 + j, (*chip, 1 - c), me).wait_recv()
        for cp in first() + passed():
            cp.wait_send()
        for cp in local():
            cp.wait()


def _gather_two_level(arrays, name):
    n = len(arrays)

    def body(*refs):
        for phase in range(3):
            _gather_phase(phase, refs[:n], refs[n:2 * n], *refs[2 * n:])

    hbm = pl.BlockSpec(memory_space=pltpu.HBM)
    return pl.pallas_call(
        body, name=name, out_shape=_exchange_shapes(arrays, [False] * n),
        in_specs=[hbm] * n, out_specs=[hbm] * n, scratch_shapes=_exchange_sems(n),
    )(*arrays)


def _exchange_copies(ins, outs, send_sems, recv_sems, local_sems, scatter):
    n = len(ins)
    x, y, c = lax.axis_index("x"), lax.axis_index("y"), lax.axis_index("c")
    me = 4 * x + 2 * y + c
    flip = lambda v, f: 1 - v if f else v

    def piece(a, dest):
        return ins[a].at[dest] if scatter[a] else ins[a]

    local = [pltpu.make_async_copy(piece(a, me), outs[a].at[me], local_sems.at[a]) for a in range(n)]
    sends, recvs = [], []
    for k in range(1, N_DEV):
        fx, fy, fc = (k >> 2) & 1, (k >> 1) & 1, k & 1
        peer = (flip(x, fx), flip(y, fy), flip(c, fc))
        peer_idx = 4 * peer[0] + 2 * peer[1] + peer[2]
        for a in range(n):
            sends.append(pltpu.make_async_remote_copy(
                src_ref=piece(a, peer_idx), dst_ref=outs[a].at[me],
                send_sem=send_sems.at[k - 1, a], recv_sem=recv_sems.at[k - 1, a],
                device_id=peer, device_id_type=pl.DeviceIdType.MESH))
            recvs.append(pltpu.make_async_remote_copy(
                src_ref=piece(a, peer_idx), dst_ref=outs[a].at[peer_idx],
                send_sem=send_sems.at[k - 1, a], recv_sem=recv_sems.at[k - 1, a],
                device_id=peer, device_id_type=pl.DeviceIdType.MESH))
    return local, sends, recvs


def _exchange_start(copies):
    local, sends, _ = copies
    for cp in local + sends:
        cp.start()


def _exchange_wait(copies):
    local, sends, recvs = copies
    for cp in recvs:
        cp.wait_recv()
    for cp in sends:
        cp.wait_send()
    for cp in local:
        cp.wait()


def _exchange_shapes(arrays, scatter):
    return [jax.ShapeDtypeStruct(a.shape if s else (N_DEV,) + a.shape, a.dtype) for a, s in zip(arrays, scatter)]


def _exchange_sems(n):
    return [pltpu.SemaphoreType.DMA((N_DEV - 1, n)), pltpu.SemaphoreType.DMA((N_DEV - 1, n)),
            pltpu.SemaphoreType.DMA((n,))]


def _exchange(arrays, scatter, name):
    n = len(arrays)

    def body(*refs):
        copies = _exchange_copies(refs[:n], refs[n:2 * n], *refs[2 * n:], scatter)
        _exchange_start(copies)
        _exchange_wait(copies)

    hbm = pl.BlockSpec(memory_space=pltpu.HBM)
    return pl.pallas_call(
        body, name=name, out_shape=_exchange_shapes(arrays, scatter),
        in_specs=[hbm] * n, out_specs=[hbm] * n, scratch_shapes=_exchange_sems(n),
    )(*arrays)


def _pack(parts):
    flat = jnp.concatenate([p.reshape(-1) for p in parts])
    total = _round_up(flat.shape[0], SUBLANES * LANES)
    return jnp.pad(flat, (0, total - flat.shape[0])).reshape(-1, LANES)


def _unpack(buf, shapes):
    flat = buf.reshape(-1)
    out, pos = [], 0
    for s in shapes:
        n = int(np.prod(s))
        out.append(flat[pos:pos + n].reshape(s))
        pos += n
    return out


_SHARDED_SMALL = ("decay_w0", "decay_w2", "iclr_a0", "iclr_a2", "gate_w2", "conv_w")
_REPLICATED = ("mix_pre_g", "mix_post_g", "mlp_pre_g", "mlp_post_g", "mu_prev", "mu_next", "k_k", "k_a", "r_k",
               "lnx_w", "lnx_b", "conv_b", "conv_ln_w", "conv_ln_b")
_ADA_SMALL = ("c_ctx", "ada_b")
_WEIGHTS = ("c_ctx", "ada_w", "ada_b", "mix_pre_g", "mix_post_g", "mlp_pre_g", "mlp_post_g", "w_in", "mu_prev",
            "mu_next", "decay_w0", "decay_w2", "iclr_a0", "iclr_a2", "k_k", "k_a", "r_k", "gate_w2", "lnx_w", "lnx_b",
            "conv_w", "conv_b", "conv_ln_w", "conv_ln_b", "w_out", "mlp_w1", "mlp_w2")
_INPUTS = ("x", "c", "ctx") + _WEIGHTS + ("loss_target",) + tuple("m_" + n for n in _WEIGHTS) + tuple(
    "v_" + n for n in _WEIGHTS)


def _cols_to_blocks(a):
    a = a.reshape(a.shape[:-1] + (N_DEV, a.shape[-1] // N_DEV))
    return jnp.moveaxis(a, -2, 0)


def _blocks_to_cols(a):
    a = jnp.moveaxis(a, 0, -2)
    return a.reshape(a.shape[:-2] + (a.shape[-2] * a.shape[-1],))


def kernel(x, c, ctx, c_ctx, ada_w, ada_b, mix_pre_g, mix_post_g, mlp_pre_g, mlp_post_g, w_in, mu_prev, mu_next, decay_w0, decay_w2, iclr_a0, iclr_a2, k_k, k_a, r_k, gate_w2, lnx_w, lnx_b, conv_w, conv_b, conv_ln_w, conv_ln_b, w_out, mlp_w1, mlp_w2, loss_target, m_c_ctx, m_ada_w, m_ada_b, m_mix_pre_g, m_mix_post_g, m_mlp_pre_g, m_mlp_post_g, m_w_in, m_mu_prev, m_mu_next, m_decay_w0, m_decay_w2, m_iclr_a0, m_iclr_a2, m_k_k, m_k_a, m_r_k, m_gate_w2, m_lnx_w, m_lnx_b, m_conv_w, m_conv_b, m_conv_ln_w, m_conv_ln_b, m_w_out, m_mlp_w1, m_mlp_w2, v_c_ctx, v_ada_w, v_ada_b, v_mix_pre_g, v_mix_post_g, v_mlp_pre_g, v_mlp_post_g, v_w_in, v_mu_prev, v_mu_next, v_decay_w0, v_decay_w2, v_iclr_a0, v_iclr_a2, v_k_k, v_k_a, v_r_k, v_gate_w2, v_lnx_w, v_lnx_b, v_conv_w, v_conv_b, v_conv_ln_w, v_conv_ln_b, v_w_out, v_mlp_w1, v_mlp_w2):
    given = dict(zip(_INPUTS, (x, c, ctx, c_ctx, ada_w, ada_b, mix_pre_g, mix_post_g, mlp_pre_g, mlp_post_g, w_in, mu_prev, mu_next, decay_w0, decay_w2, iclr_a0, iclr_a2, k_k, k_a, r_k, gate_w2, lnx_w, lnx_b, conv_w, conv_b, conv_ln_w, conv_ln_b, w_out, mlp_w1, mlp_w2, loss_target, m_c_ctx, m_ada_w, m_ada_b, m_mix_pre_g, m_mix_post_g, m_mlp_pre_g, m_mlp_post_g, m_w_in, m_mu_prev, m_mu_next, m_decay_w0, m_decay_w2, m_iclr_a0, m_iclr_a2, m_k_k, m_k_a, m_r_k, m_gate_w2, m_lnx_w, m_lnx_b, m_conv_w, m_conv_b, m_conv_ln_w, m_conv_ln_b, m_w_out, m_mlp_w1, m_mlp_w2, v_c_ctx, v_ada_w, v_ada_b, v_mix_pre_g, v_mix_post_g, v_mlp_pre_g, v_mlp_post_g, v_w_in, v_mu_prev, v_mu_next, v_decay_w0, v_decay_w2, v_iclr_a0, v_iclr_a2, v_k_k, v_k_a, v_r_k, v_gate_w2, v_lnx_w, v_lnx_b, v_conv_w, v_conv_b, v_conv_ln_w, v_conv_ln_b, v_w_out, v_mlp_w1, v_mlp_w2)))
    loc = {}
    for pre in ("", "m_", "v_"):
        for n in _WEIGHTS:
            a = given[pre + n]
            a = a.reshape(1, -1) if n == "c_ctx" else a[0]
            loc[pre + n] = a.reshape(1, -1) if a.ndim == 1 else a
    B, TX, D = x.shape
    W, CW = loc["k_k"].shape[1], loc["conv_b"].shape[1]
    cfg = _Cfg(B, TX, ctx.shape[1], D, W, CW, loc["decay_w2"].shape[1], loc["gate_w2"].shape[0],
               loc["conv_w"].shape[0], loc["mlp_w1"].shape[1] * N_DEV)
    me = _my_index()

    small_shapes = [loc[n].shape for n in _SHARDED_SMALL]
    got = _gather_two_level(
        [loc["ada_w"].astype(BF16), loc["w_in"].astype(BF16), _pack([loc[n] for n in _SHARDED_SMALL])],
        "gather_weights")
    fw = {n: loc[n] for n in _REPLICATED + _ADA_SMALL}
    fw["ada_w"] = _blocks_to_cols(got[0])
    fw["w_in"] = _blocks_to_cols(got[1])
    per_dev = [_unpack(got[2][i], small_shapes) for i in range(N_DEV)]
    for j, n in enumerate(_SHARDED_SMALL):
        fw[n] = jnp.concatenate([per_dev[i][j] for i in range(N_DEV)], axis=-1)
    late = [loc[n].astype(BF16) for n in ("w_out", "mlp_w1", "mlp_w2")]

    loss_t, grad_x, grads, exchanged, s_rows, g_rows = _local_step(cfg, x, c, ctx, loss_target, fw, late)
    loss = lax.psum(jnp.sum(loss_t[:, :, 0, 0]), ("x", "y", "c"))

    small_blocks = jnp.stack([_pack([_cols_to_blocks(grads[n])[i] for n in _SHARDED_SMALL]) for i in range(N_DEV)])
    sent = _exchange(
        [_cols_to_blocks(grads["w_in"].astype(BF16)), small_blocks,
         _pack([grads[n] for n in _REPLICATED]), s_rows, g_rows],
        [True] * 2 + [False] * 3, "exchange_grads")
    s_all = sent[3].reshape(N_DEV * SUBLANES, D)
    g_all = sent[4].reshape(N_DEV * SUBLANES, 6 * D)
    ns = 6 * D // N_DEV
    g_mine = lax.dynamic_slice_in_dim(g_all, me * ns, ns, axis=1)
    d_ada_w, d_ada_b, d_c_ctx = _ada_bwd(s_all, g_all, g_mine, loc["c_ctx"], fw["ada_w"], B)

    res = {}

    def update(name, parts):
        res[name] = _adamw(parts, loc[name], loc["m_" + name], loc["v_" + name], "adamw_" + name)

    update("w_in", sent[0])
    for n in ("w_out", "mlp_w1", "mlp_w2"):
        update(n, exchanged[n])
    update("ada_w", d_ada_w[None])

    def update_packed(names, parts, tag):
        shapes = [loc[n].shape for n in names]
        packed = _adamw(parts, *[_pack([loc[pre + n] for n in names]) for pre in ("", "m_", "v_")], "adamw_" + tag)
        unpacked = [_unpack(p, shapes) for p in packed]
        for j, n in enumerate(names):
            res[n] = tuple(u[j] for u in unpacked)

    update_packed(_SHARDED_SMALL, sent[1], "sharded_small")
    update_packed(_REPLICATED, sent[2], "replicated")
    update_packed(_ADA_SMALL, _pack([d_c_ctx, d_ada_b])[None], "ada_small")

    outs = [loss, grad_x]
    for k in range(4):
        for n in _WEIGHTS:
            outs.append(res[n][k].reshape(given[n].shape))
    return tuple(outs)
```

```python
import functools

import numpy as np
import jax
import jax.numpy as jnp
from jax import lax
from jax.experimental import pallas as pl
from jax.experimental.pallas import tpu as pltpu

F32 = jnp.float32
BF16 = jnp.bfloat16

EPS_RMS = 1e-6
EPS_LN = 1e-5
EPS_GN = 64e-5
LINE = 64
HEAD = 64
LANES = 128
SUBLANES = 8
MXU_DIM = 256
SCAN_CHUNK = 16
N_DEV = 8
VMEM_LIMIT = 56 * 1024 * 1024

ADAM_LR = 0.001
ADAM_B1 = 0.9
ADAM_B2 = 0.999
ADAM_EPS = 1e-08
ADAM_WD = 0.01
ADAM_STEP = 10


def _round_up(n, m):
    return (n + m - 1) // m * m


def _params(semantics=None, vmem=VMEM_LIMIT):
    return pltpu.CompilerParams(dimension_semantics=semantics, vmem_limit_bytes=vmem)


def _bdot(a, b):
    return jnp.dot(a.astype(BF16), b.astype(BF16), preferred_element_type=F32)


def _bdot_nt(a, b):
    return lax.dot_general(a.astype(BF16), b.astype(BF16), (((1,), (1,)), ((), ())),
                           preferred_element_type=F32)


def _bdot_tn(a, b):
    return lax.dot_general(a.astype(BF16), b.astype(BF16), (((0,), (0,)), ((), ())),
                           preferred_element_type=F32)


@jax.custom_vjp
def _mm(a, b):
    return _bdot(a, b)


def _mm_fwd(a, b):
    return _bdot(a, b), (a, b)


def _mm_bwd(res, g):
    a, b = res
    return _bdot_nt(g, b), _bdot_tn(a, g)


_mm.defvjp(_mm_fwd, _mm_bwd)


def _seg_sum_raw(x, e):
    hi = x.astype(BF16)
    lo = (x - hi.astype(F32)).astype(BF16)
    n = e.shape[0]
    parts = [jnp.dot(hi[:, s:s + n], e, preferred_element_type=F32)
             + jnp.dot(lo[:, s:s + n], e, preferred_element_type=F32) for s in range(0, x.shape[1], n)]
    return parts[0] if len(parts) == 1 else jnp.concatenate(parts, axis=1)


@jax.custom_vjp
def _seg_sum(x, e):
    return _seg_sum_raw(x, e)


def _seg_sum_fwd(x, e):
    return _seg_sum_raw(x, e), e


def _seg_sum_bwd(e, g):
    return _seg_sum_raw(g, e), None


_seg_sum.defvjp(_seg_sum_fwd, _seg_sum_bwd)


def _block_ones(n, seg=HEAD):
    i = np.arange(n) // seg
    return jnp.asarray((i[:, None] == i[None, :]).astype(np.float32), dtype=BF16)


def _rms(xv, g):
    ms = jnp.mean(xv * xv, axis=-1, keepdims=True)
    return xv * lax.rsqrt(ms + EPS_RMS) * g


def _rms_mod(xv, g, shift, scale):
    return _rms(xv, g) * (1.0 + scale) + shift


def _sigmoid(z):
    return 1.0 / (1.0 + jnp.exp(-z))


def _silu(z):
    return z * _sigmoid(z)


def _softplus(z):
    return jnp.maximum(z, 0.0) + jnp.log(1.0 + jnp.exp(-jnp.abs(z)))


class _Cfg:
    def __init__(self, B, TX, TC, D, W, CW, R, GR, KC, F):
        self.B, self.TX, self.TC, self.D = B, TX, TC, D
        self.W, self.CW, self.R, self.GR, self.KC, self.F = W, CW, R, GR, KC, F
        self.T = TX + TC
        self.TT = min(256, TC)
        assert TC % self.TT == 0 and TX % self.TT == 0 and self.TT % LINE == 0
        self.JC = TC // self.TT
        self.JX = TX // self.TT
        self.J = self.JC + self.JX
        self.HP = W // LANES
        self.G = B * self.HP
        self.PW = _round_up(2 * R, LANES)
        self.GP = _round_up(GR, LANES)
        self.SP = 3 * W + 2 * self.PW + self.GP
        self.CP = self.SP + 2 * CW
        self.KP = _round_up(KC, SUBLANES)
        assert self.T % SCAN_CHUNK == 0 and TC % SCAN_CHUNK == 0
        self.NCH = self.T // SCAN_CHUNK
        self.NCC = TC // SCAN_CHUNK
        W_, R_ = W, R
        segs = [(0, 3 * W_, 0),
                (3 * W_, 2 * R_, 3 * W_),
                (3 * W_ + 2 * R_, 2 * R_, 3 * W_ + self.PW),
                (3 * W_ + 4 * R_, GR, 3 * W_ + 2 * self.PW),
                (3 * W_ + 4 * R_ + GR, 2 * CW, self.SP)]
        self.col_segs = segs
        self.shift_cols = 3 * W_ + 4 * R_ + GR
        self.in_cols = self.shift_cols + 2 * CW


def _pad_cols(a, cfg, upto_shift=False):
    width = cfg.SP if upto_shift else cfg.CP
    pieces, pos = [], 0
    for src, n, dst in cfg.col_segs:
        if upto_shift and dst >= cfg.SP:
            break
        if dst > pos:
            pieces.append(jnp.zeros(a.shape[:-1] + (dst - pos,), a.dtype))
        pieces.append(a[..., src:src + n])
        pos = dst + n
    if width > pos:
        pieces.append(jnp.zeros(a.shape[:-1] + (width - pos,), a.dtype))
    return jnp.concatenate(pieces, axis=-1)


def _unpad_cols(a, cfg, upto_shift=False):
    pieces = []
    for src, n, dst in cfg.col_segs:
        if upto_shift and dst >= cfg.SP:
            break
        pieces.append(a[..., dst:dst + n])
    return jnp.concatenate(pieces, axis=-1)


def _pair_weight(w2, cfg):
    R, W = cfg.R, cfg.W
    out = jnp.zeros((cfg.PW, 2 * W), w2.dtype)
    out = out.at[0:R, 0:W].set(w2[0])
    out = out.at[R:2 * R, W:2 * W].set(w2[1])
    return out


def _unpair_weight(g, cfg):
    R, W = cfg.R, cfg.W
    return jnp.stack([g[0:R, 0:W], g[R:2 * R, W:2 * W]])


def _row_ids(n):
    return lax.broadcasted_iota(jnp.int32, (n, 1), 0)


def _shift_rows(z, prev_row, next_row):
    n = z.shape[0]
    rows = _row_ids(n)
    zp = jnp.where(rows == 0, prev_row, pltpu.roll(z, 1, 0))
    zn = jnp.where(rows == n - 1, next_row, pltpu.roll(z, n - 1, 0))
    return zp, zn


def _line_shift(u, d):
    if d == 0:
        return u
    n = u.shape[0]
    lt = _row_ids(n) % LINE
    ok = jnp.logical_and(lt + d >= 0, lt + d < LINE)
    return jnp.where(ok, pltpu.roll(u, (-d) % n, 0), 0.0)


def _conv_tables(cw, kc):
    pad = kc // 2
    t = np.arange(LINE)[None, :]
    d = (np.arange(kc) - pad)[:, None]
    fwd = ((t + d >= 0) & (t + d < LINE)).astype(np.float32)
    bwd = ((t - d >= 0) & (t - d < LINE)).astype(np.float32)
    w = cw[:kc, None, :]
    return jnp.asarray(fwd)[:, :, None] * w, jnp.asarray(bwd)[:, :, None] * w


def _conv_lines(src_ref, wm_ref, dst_ref, kc, transpose):
    pad = kc // 2
    n, width = src_ref.shape
    for l in range(n // LINE):
        for b in range(width // LANES):
            rs, cs = slice(l * LINE, (l + 1) * LINE), slice(b * LANES, (b + 1) * LANES)
            tile = src_ref[rs, cs]
            acc = jnp.zeros_like(tile)
            for i in range(kc):
                d = (pad - i) if transpose else (i - pad)
                acc = acc + pltpu.roll(tile, (-d) % LINE, 0) * wm_ref[i, :, cs]
            dst_ref[rs, cs] = acc


def _rwkv_prep(rw, w0, w2p, a0, a2p, k_k, k_a, e, cfg):
    W, PW = cfg.W, cfg.PW
    r = rw[:, 0:W]
    k = rw[:, W:2 * W]
    v = rw[:, 2 * W:3 * W]
    wdp = rw[:, 3 * W:3 * W + PW]
    adp = rw[:, 3 * W + PW:3 * W + 2 * PW]
    wl = w0 + _mm(jnp.tanh(wdp), w2p)
    w_log = -_softplus(-wl) - 0.5
    decay = jnp.exp(-jnp.exp(w_log))
    iclr = _sigmoid(a0 + _mm(adp, a2p))
    kkr = k * k_k
    nrm = jnp.sqrt(_seg_sum(kkr * kkr, e))
    kk = kkr / jnp.maximum(nrm, 1e-12)
    outs = [r, v, kk]
    for d in range(2):
        ic = iclr[:, d * W:(d + 1) * W]
        outs += [decay[:, d * W:(d + 1) * W], k * (1.0 + (ic - 1.0) * k_a), kk * ic]
    return tuple(outs)


def _glu(cv, cfg):
    return cv[:, :cfg.CW] * _sigmoid(cv[:, cfg.CW:])


def _conv_post(y, cb, lw, lb):
    yf = y + cb
    mu = jnp.mean(yf, axis=-1, keepdims=True)
    var = jnp.mean(jnp.square(yf - mu), axis=-1, keepdims=True)
    return _silu((yf - mu) * lax.rsqrt(var + EPS_LN) * lw + lb)


def _readout(y, kbar, r, v, gd, r_k, gw2, lnx_w, lnx_b, e):
    inv = 1.0 / HEAD
    mu = _seg_sum(y, e) * inv
    yc = y - mu
    var = _seg_sum(yc * yc, e) * inv
    yn = yc * lax.rsqrt(var + EPS_GN) * lnx_w + lnx_b
    bonus = _seg_sum(r * kbar * r_k, e) * v
    g = _mm(_sigmoid(gd), gw2)
    return (yn + bonus) * g


def _post_res(xv, mix, gate, g):
    return xv + gate * _rms(mix, g)


def _head_spec(cfg, tmap):
    return pl.BlockSpec((1, cfg.HP, cfg.TT, LANES), lambda b, j: (b, 0, tmap(j), 0))


def _full_spec(shape):
    n = len(shape)
    return pl.BlockSpec(shape, lambda *_: (0,) * n)


def _to_heads(ref, val, cfg):
    for hp in range(cfg.HP):
        ref[0, hp] = val[:, hp * LANES:(hp + 1) * LANES]


def _from_heads(ref, cfg):
    return jnp.concatenate([ref[0, hp] for hp in range(cfg.HP)], axis=-1)


def _token_specs(cfg):
    TT, D, JC = cfg.TT, cfg.D, cfg.JC
    return [pl.BlockSpec((1, TT, D), lambda b, j: (b, jnp.minimum(j, JC - 1), 0)),
            pl.BlockSpec((1, TT, D), lambda b, j: (b, jnp.maximum(j - JC, 0), 0))]


def _in_proj(ctx, x, modt, g1, w_in_p, cfg):
    B, T, D, TT, CP, JC = cfg.B, cfg.T, cfg.D, cfg.TT, cfg.CP, cfg.JC

    def body(c_ref, x_ref, mod_ref, g_ref, w_ref, p_ref, h_ref):
        xv = jnp.where(pl.program_id(1) < JC, c_ref[0], x_ref[0])
        h = _rms_mod(xv, g_ref[...], mod_ref[0, 0, 0:1, :], mod_ref[0, 0, 1:2, :])
        hb = h.astype(BF16)
        h_ref[0] = hb
        p_ref[0] = jnp.dot(hb, w_ref[...], preferred_element_type=F32)

    return pl.pallas_call(
        body, name="in_proj", grid=(B, cfg.J),
        in_specs=_token_specs(cfg) + [pl.BlockSpec((1, 1, 2, D), lambda b, j: (b, j, 0, 0)),
                                      _full_spec((1, D)), _full_spec((D, CP))],
        out_specs=[pl.BlockSpec((1, TT, CP), lambda b, j: (b, j, 0)),
                   pl.BlockSpec((1, TT, D), lambda b, j: (b, j, 0))],
        out_shape=[jax.ShapeDtypeStruct((B, T, CP), F32), jax.ShapeDtypeStruct((B, T, D), BF16)],
        compiler_params=_params(("arbitrary", "arbitrary")),
    )(ctx, x, modt, g1, w_in_p)


def _halo_specs(cfg, width):
    per = cfg.TT // SUBLANES
    last = cfg.T // SUBLANES - 1
    prev = pl.BlockSpec((1, SUBLANES, width), lambda b, j: (b, jnp.maximum(j * per - 1, 0), 0))
    nxt = pl.BlockSpec((1, SUBLANES, width), lambda b, j: (b, jnp.minimum((j + 1) * per, last), 0))
    return prev, nxt


def _halo_flags(j, cfg):
    has_prev = jnp.logical_and(j != 0, j != cfg.JC).astype(F32)
    has_next = jnp.logical_and(j != cfg.JC - 1, j != cfg.J - 1).astype(F32)
    return has_prev, has_next


def _shifted(p_ref, prev_ref, next_ref, mup, mun, j, cfg):
    SP = cfg.SP
    has_prev, has_next = _halo_flags(j, cfg)
    z = p_ref[0][:, :SP]
    zp, zn = _shift_rows(z, prev_ref[0, SUBLANES - 1:SUBLANES, :] * has_prev, next_ref[0, 0:1, :] * has_next)
    return z, zp, zn, z + mup * (zp - z) + mun * (zn - z)


def _mix_prep(p, mup, mun, w0, w2p, a0, a2p, k_k, k_a, wm, cb, clw, clb, e_w, late, cfg):
    B, T, TT, SP, CP, W, CW, HP, JC = cfg.B, cfg.T, cfg.TT, cfg.SP, cfg.CP, cfg.W, cfg.CW, cfg.HP, cfg.JC
    nl = len(late)
    steps = B * cfg.J

    def body(p_ref, prev_ref, next_ref, mup_ref, mun_ref, w0_ref, w2_ref, a0_ref, a2_ref, kk_ref, ka_ref,
             wm_ref, cb_ref, clw_ref, clb_ref, e_ref, *rest):
        late_in, rest = rest[:nl], rest[nl:]
        outs, late_out, u_ref, sems = rest[:12], rest[12:12 + nl], rest[12 + nl], rest[13 + nl:]
        j = pl.program_id(1)
        step = pl.program_id(0) * cfg.J + j
        for phase, at in enumerate((0, steps // 2)):
            @pl.when(step == at)
            def _(phase=phase):
                _gather_phase(phase, late_in, late_out, *sems)
        _, _, _, rw = _shifted(p_ref, prev_ref, next_ref, mup_ref[...], mun_ref[...], j, cfg)
        vals = _rwkv_prep(rw, w0_ref[...], w2_ref[...], a0_ref[...], a2_ref[...], kk_ref[...], ka_ref[...],
                          e_ref[...], cfg)
        for ref, val in zip(outs[:9], vals):
            _to_heads(ref, val, cfg)
        outs[9][0] = rw[:, 3 * W + 2 * cfg.PW:SP]

        @pl.when(j >= JC)
        def _():
            u_ref[...] = _glu(p_ref[0, :, SP:], cfg)
            _conv_lines(u_ref, wm_ref, outs[11].at[0], cfg.KC, False)
            outs[10][0] = _conv_post(outs[11][0], cb_ref[...], clw_ref[...], clb_ref[...])

        @pl.when(step == steps - 1)
        def _():
            _gather_phase(2, late_in, late_out, *sems)

    prev, nxt = _halo_specs(cfg, SP)
    head = jax.ShapeDtypeStruct((B, HP, T, LANES), F32)
    hbm = pl.BlockSpec(memory_space=pltpu.HBM)
    tile = lambda n: pl.BlockSpec((1, TT, n), lambda b, j: (b, j, 0))
    return pl.pallas_call(
        body, name="mix_prep", grid=(B, cfg.J),
        in_specs=[tile(CP), prev, nxt,
                  _full_spec((1, SP)), _full_spec((1, SP)),
                  _full_spec((1, 2 * W)), _full_spec((cfg.PW, 2 * W)),
                  _full_spec((1, 2 * W)), _full_spec((cfg.PW, 2 * W)),
                  _full_spec((1, W)), _full_spec((1, W)),
                  pl.BlockSpec((cfg.KC, LINE, CW), lambda b, j: (0, 0, 0), pipeline_mode=pl.Buffered(1)),
                  _full_spec((1, CW)), _full_spec((1, CW)), _full_spec((1, CW)),
                  _full_spec(e_w.shape)] + [hbm] * nl,
        out_specs=[_head_spec(cfg, lambda j: j)] * 9 + [tile(cfg.GP), tile(CW), tile(CW)] + [hbm] * nl,
        out_shape=[head] * 9 + [jax.ShapeDtypeStruct((B, T, cfg.GP), F32),
                                jax.ShapeDtypeStruct((B, T, CW), F32), jax.ShapeDtypeStruct((B, T, CW), F32)]
                  + _exchange_shapes(late, [False] * nl),
        scratch_shapes=[pltpu.VMEM((TT, CW), F32)] + _exchange_sems(nl),
        compiler_params=_params(("arbitrary", "arbitrary")),
    )(p, p, p, mup, mun, w0, w2p, a0, a2p, k_k, k_a, wm, cb, clw, clb, e_w, *late)


def _chunk_pos(c, reverse, cfg):
    if not reverse:
        return c
    return jnp.where(c < cfg.NCC, cfg.NCC - 1 - c, cfg.NCH - 1 + cfg.NCC - c)


def _diag_mask():
    r = lax.broadcasted_iota(jnp.int32, (HEAD, LANES), 0)
    l = lax.broadcasted_iota(jnp.int32, (HEAD, LANES), 1)
    return (r == l % HEAD).astype(F32)


def _col_lhs(row, diag_b):
    hi = row.astype(BF16)
    lo = (row - hi.astype(F32)).astype(BF16)
    return diag_b * hi, diag_b * lo


def _col_dot(row_list, diag_b, e2):
    n, g = len(row_list), row_list[0].shape[0]
    lhs = jnp.concatenate([jnp.concatenate(_col_lhs(r, diag_b), axis=-1) for r in row_list], axis=0)
    out = jnp.dot(lhs.reshape(n * g * HEAD, 2 * LANES), e2, preferred_element_type=F32)
    return out.reshape(n, g, HEAD, LANES)


def _col_form(row):
    n = row.shape[0]
    t = jnp.swapaxes(jnp.broadcast_to(row, (n, LANES, LANES)), 1, 2)
    lane = lax.broadcasted_iota(jnp.int32, (HEAD, LANES), 1)
    return jnp.where(lane < HEAD, t[:, :HEAD, :], t[:, HEAD:, :])


def _col_both(row, diag_b, e2):
    half = row.shape[0] // 2
    return jnp.concatenate([_col_form(row[:half]), _col_dot([row[half:]], diag_b, e2)[0]], axis=0)


def _both_rows(ins, idx, i):
    return jnp.concatenate([ins[d][idx][:, pl.ds(_tok(i, d == 1), 1), :] for d in range(2)], axis=0)


def _seg_dot(blocks, e):
    n, g = len(blocks), blocks[0].shape[0]
    lhs = jnp.concatenate(blocks, axis=0).reshape(n * g * HEAD, LANES)
    return jnp.dot(lhs, e, preferred_element_type=F32).reshape(n, g, HEAD, LANES)


def _tok(i, reverse):
    return (SCAN_CHUNK - 1 - i) if reverse else i


def _scan_fwd(ops_f, ops_b, e128, e256, cfg):
    G, T, NCH = cfg.G, cfg.T, cfg.NCH
    CH = SCAN_CHUNK
    G2 = 2 * G

    def body(*refs):
        ins = (refs[0:6], refs[6:12])
        ahead = (refs[12:14], refs[14:16])
        e_ref, e2_ref = refs[16], refs[17]
        ys, hist_ref, fin_ref = (refs[18], refs[19]), refs[20], refs[21]
        s_ref, mm_ref = refs[22], refs[23]
        c = pl.program_id(0)
        diag = _diag_mask()
        diag_b = diag.astype(BF16)
        e, e2 = e_ref[...], e2_ref[...]

        @pl.when(c == 0)
        def _():
            s_ref[...] = jnp.zeros_like(s_ref)
            mm_ref[0] = jnp.zeros_like(s_ref)
            mm_ref[1] = _col_both(_both_rows(ins, 3, 0), diag_b, e2)

        def step(i, nxt_row):
            res = []
            for d in range(2):
                sl = slice(d * G, (d + 1) * G)
                row = lambda idx: ins[d][idx][:, pl.ds(_tok(i, d == 1), 1), :]
                s_old = s_ref[sl]
                hist_ref[i, sl] = s_old
                S = s_old * row(1) + mm_ref[0, sl] * row(5) + mm_ref[1, sl] * row(2)
                s_ref[sl] = S
                sb = S.astype(BF16)
                res.append(_seg_dot([sb * (-nxt_row(d, 4)).astype(BF16), sb * row(0).astype(BF16)], e))
            for d in range(2):
                sl = slice(d * G, (d + 1) * G)
                v_next = nxt_row(d, 3)
                mm_ref[0, sl] = res[d][0]
                mm_ref[1, sl] = _col_form(v_next) if d == 0 else _col_dot([v_next], diag_b, e2)[0]
                ys[d][:, pl.ds(_tok(i, d == 1), 1), :] = jnp.sum(diag * res[d][1], axis=1, keepdims=True)

        def inner(i, carry):
            step(i, lambda d, idx: ins[d][idx][:, pl.ds(_tok(i + 1, d == 1), 1), :])
            return carry

        lax.fori_loop(0, CH - 1, inner, 0)
        edge = lambda d: (SUBLANES - 1) if d == 1 else 0
        step(CH - 1, lambda d, idx: ahead[d][0 if idx == 4 else 1][:, edge(d):edge(d) + 1, :])
        fin_ref[...] = s_ref[...]

    toks = [pl.BlockSpec((G, CH, LANES), lambda c, rev=rev: (0, _chunk_pos(c, rev, cfg), 0)) for rev in (False, True)]
    per = CH // SUBLANES

    def ahead_block(c, rev):
        pos = _chunk_pos(jnp.minimum(c + 1, NCH - 1), rev, cfg)
        return pos * per + (per - 1 if rev else 0)

    ahead = [pl.BlockSpec((G, SUBLANES, LANES), lambda c, rev=rev: (0, ahead_block(c, rev), 0)) for rev in (False, True)]
    y_shape = jax.ShapeDtypeStruct((G, T, LANES), F32)
    return pl.pallas_call(
        body, name="scan_fwd", grid=(NCH,),
        in_specs=[toks[0]] * 6 + [toks[1]] * 6 + [ahead[0]] * 2 + [ahead[1]] * 2
                 + [_full_spec((LANES, LANES)), _full_spec((2 * LANES, LANES))],
        out_specs=[toks[0], toks[1], pl.BlockSpec((CH, G2, HEAD, LANES), lambda c: (c, 0, 0, 0)),
                   _full_spec((G2, HEAD, LANES))],
        out_shape=[y_shape, y_shape, jax.ShapeDtypeStruct(((NCH + 1) * CH, G2, HEAD, LANES), F32),
                   jax.ShapeDtypeStruct((G2, HEAD, LANES), F32)],
        scratch_shapes=[pltpu.VMEM((G2, HEAD, LANES), F32), pltpu.VMEM((2, G2, HEAD, LANES), F32)],
        compiler_params=_params(("arbitrary",)),
    )(*ops_f, *ops_b, ops_f[4], ops_f[3], ops_b[4], ops_b[3], e128, e256)


def _mix_out(yf, yb, kdf, kdb, r, v, gd, conv, x, mod2, r_k, gw2p, lnx_w, lnx_b, w_out, g2, e_w, cfg):
    B, TX, D, TT, W, CW, JC = cfg.B, cfg.TX, cfg.D, cfg.TT, cfg.W, cfg.CW, cfg.JC

    def body(yf_ref, yb_ref, kdf_ref, kdb_ref, r_ref, v_ref, gd_ref, cv_ref, x_ref, mod_ref,
             rk_ref, gw_ref, lw_ref, lb_ref, wo_ref, g_ref, e_ref, x1_ref):
        y = _from_heads(yf_ref, cfg) + _from_heads(yb_ref, cfg)
        kbar = 0.5 * (_from_heads(kdf_ref, cfg) + _from_heads(kdb_ref, cfg))
        ro = _readout(y, kbar, _from_heads(r_ref, cfg), _from_heads(v_ref, cfg), gd_ref[0], rk_ref[...],
                      gw_ref[...], lw_ref[...], lb_ref[...], e_ref[...])
        cat = jnp.concatenate([ro, cv_ref[0]], axis=-1)
        mix = _bdot(cat, wo_ref[...])
        x1_ref[0] = _post_res(x_ref[0], mix, mod_ref[0], g_ref[...])

    hs = _head_spec(cfg, lambda j: j + JC)
    lat = lambda n: pl.BlockSpec((1, TT, n), lambda b, j: (b, j + JC, 0))
    return pl.pallas_call(
        body, name="mix_out", grid=(B, cfg.JX),
        in_specs=[hs] * 6 + [lat(cfg.GP), lat(CW),
                             pl.BlockSpec((1, TT, D), lambda b, j: (b, j, 0)),
                             pl.BlockSpec((1, 1, D), lambda b, j: (b, 0, 0)),
                             _full_spec((1, W)), _full_spec((cfg.GP, W)), _full_spec((1, W)), _full_spec((1, W)),
                             _full_spec((W + CW, D)), _full_spec((1, D)), _full_spec(e_w.shape)],
        out_specs=pl.BlockSpec((1, TT, D), lambda b, j: (b, j, 0)),
        out_shape=jax.ShapeDtypeStruct((B, TX, D), F32),
        compiler_params=_params(("parallel", "parallel")),
    )(yf, yb, kdf, kdb, r, v, gd, conv, x, mod2, r_k, gw2p, lnx_w, lnx_b, w_out, g2, e_w)


def _acc(ref, val, first):
    @pl.when(first)
    def _():
        ref[...] = val

    @pl.when(jnp.logical_not(first))
    def _():
        ref[...] += val


def _mlp_fwd_bwd(x1, tgt, mod345, g3, g4, w1, w2, cfg):
    B, TX, D, TT, F, JX = cfg.B, cfg.TX, cfg.D, cfg.TT, cfg.F, cfg.JX

    def body(x1_ref, t_ref, mod_ref, g3_ref, g4_ref, w1_ref, w2_ref,
             dx1_ref, loss_ref, h2_ref, dpre_ref, act_ref, dff_ref, dmod_ref, dg3_ref, dg4_ref):
        b, j = pl.program_id(0), pl.program_id(1)
        x1v = x1_ref[0]
        sh, sc, gt = mod_ref[0, 0:1, :], mod_ref[0, 1:2, :], mod_ref[0, 2:3, :]
        h2, vjp_pre = jax.vjp(_rms_mod, x1v, g3_ref[...], sh, sc)
        h2b = h2.astype(BF16)
        pre = jnp.dot(h2b, w1_ref[...], preferred_element_type=F32)
        rl = jnp.maximum(pre, 0.0)
        actb = (rl * rl).astype(BF16)
        ff = jnp.dot(actb, w2_ref[...], preferred_element_type=F32)
        x2, vjp_post = jax.vjp(_post_res, x1v, ff, gt, g4_ref[...])
        err = x2 - t_ref[0]
        loss = 0.5 * jnp.sum(jnp.mean(err * err, axis=-1, keepdims=True))
        dx1a, dff, dgt, dg4 = vjp_post(err * (1.0 / D))
        dffb = dff.astype(BF16)
        dpre = _bdot_nt(dffb, w2_ref[...]) * (2.0 * rl)
        dpreb = dpre.astype(BF16)
        dx1b, dg3, dsh, dsc = vjp_pre(_bdot_nt(dpreb, w1_ref[...]))
        dx1_ref[0] = dx1a + dx1b
        loss_ref[0, 0] = jnp.zeros((SUBLANES, LANES), F32) + loss
        h2_ref[0] = h2b
        dpre_ref[0] = dpreb
        act_ref[0] = actb
        dff_ref[0] = dffb
        _acc(dmod_ref, jnp.concatenate([dsh, dsc, dgt], axis=0)[None], j == 0)
        first = jnp.logical_and(b == 0, j == 0)
        _acc(dg3_ref, dg3, first)
        _acc(dg4_ref, dg4, first)

    tile = lambda n: pl.BlockSpec((1, TT, n), lambda b, j: (b, j, 0))
    return pl.pallas_call(
        body, name="mlp_fwd_bwd", grid=(B, JX),
        in_specs=[tile(D), tile(D), pl.BlockSpec((1, 3, D), lambda b, j: (b, 0, 0)),
                  _full_spec((1, D)), _full_spec((1, D)),
                  pl.BlockSpec((D, F), lambda b, j: (0, 0), pipeline_mode=pl.Buffered(1)),
                  pl.BlockSpec((F, D), lambda b, j: (0, 0), pipeline_mode=pl.Buffered(1))],
        out_specs=[tile(D), pl.BlockSpec((1, 1, SUBLANES, LANES), lambda b, j: (b, j, 0, 0)),
                   tile(D), tile(F), tile(F), tile(D),
                   pl.BlockSpec((1, 3, D), lambda b, j: (b, 0, 0)),
                   _full_spec((1, D)), _full_spec((1, D))],
        out_shape=[jax.ShapeDtypeStruct((B, TX, D), F32),
                   jax.ShapeDtypeStruct((B, JX, SUBLANES, LANES), F32),
                   jax.ShapeDtypeStruct((B, TX, D), BF16), jax.ShapeDtypeStruct((B, TX, F), BF16),
                   jax.ShapeDtypeStruct((B, TX, F), BF16), jax.ShapeDtypeStruct((B, TX, D), BF16),
                   jax.ShapeDtypeStruct((B, 3, D), F32),
                   jax.ShapeDtypeStruct((1, D), F32), jax.ShapeDtypeStruct((1, D), F32)],
        compiler_params=_params(("arbitrary", "arbitrary")),
    )(x1, tgt, mod345, g3, g4, w1, w2)


def _mix_out_bwd(yf, yb, kdf, kdb, r, v, gd, conv, x, mod2, r_k, gw2p, lnx_w, lnx_b, w_out, g2, e_w, dx1, cfg):
    B, TX, D, TT, W, CW, JC, HP, GP = cfg.B, cfg.TX, cfg.D, cfg.TT, cfg.W, cfg.CW, cfg.JC, cfg.HP, cfg.GP

    def body(yf_ref, yb_ref, kdf_ref, kdb_ref, r_ref, v_ref, gd_ref, cv_ref, x_ref, mod_ref,
             rk_ref, gw_ref, lw_ref, lb_ref, wo_ref, g_ref, e_ref, dx1_ref,
             dy_ref, dkb_ref, dr_ref, dv_ref, dgd_ref, dcv_ref, cat_ref, dmix_ref,
             dmod_ref, dg2_ref, drk_ref, dgw_ref, dlw_ref, dlb_ref):
        b, j = pl.program_id(0), pl.program_id(1)
        e = e_ref[...]
        y = _from_heads(yf_ref, cfg) + _from_heads(yb_ref, cfg)
        kbar = 0.5 * (_from_heads(kdf_ref, cfg) + _from_heads(kdb_ref, cfg))
        ro, vjp_ro = jax.vjp(lambda *a: _readout(*a, e), y, kbar, _from_heads(r_ref, cfg),
                             _from_heads(v_ref, cfg), gd_ref[0], rk_ref[...], gw_ref[...], lw_ref[...], lb_ref[...])
        catb = jnp.concatenate([ro, cv_ref[0]], axis=-1).astype(BF16)
        mix = jnp.dot(catb, wo_ref[...], preferred_element_type=F32)
        _, vjp_post = jax.vjp(_post_res, x_ref[0], mix, mod_ref[0], g_ref[...])
        _, dmix, dgate, dg2 = vjp_post(dx1_ref[0])
        dmixb = dmix.astype(BF16)
        dcat = _bdot_nt(dmixb, wo_ref[...])
        dy, dkb, dr, dv, dgd, drk, dgw, dlw, dlb = vjp_ro(dcat[:, :W])
        _to_heads(dy_ref, dy, cfg)
        _to_heads(dkb_ref, dkb, cfg)
        _to_heads(dr_ref, dr, cfg)
        _to_heads(dv_ref, dv, cfg)
        dgd_ref[0] = dgd
        dcv_ref[0] = dcat[:, W:]
        cat_ref[0] = catb
        dmix_ref[0] = dmixb
        _acc(dmod_ref, dgate[None], j == 0)
        first = jnp.logical_and(b == 0, j == 0)
        _acc(dg2_ref, dg2, first)
        _acc(drk_ref, drk, first)
        _acc(dgw_ref, dgw, first)
        _acc(dlw_ref, dlw, first)
        _acc(dlb_ref, dlb, first)

    hs = _head_spec(cfg, lambda j: j + JC)
    ho = _head_spec(cfg, lambda j: j)
    lat = lambda n: pl.BlockSpec((1, TT, n), lambda b, j: (b, j + JC, 0))
    tile = lambda n: pl.BlockSpec((1, TT, n), lambda b, j: (b, j, 0))
    head = jax.ShapeDtypeStruct((B, HP, TX, LANES), F32)
    vec = lambda n: jax.ShapeDtypeStruct((1, n), F32)
    return pl.pallas_call(
        body, name="mix_out_bwd", grid=(B, cfg.JX),
        in_specs=[hs] * 6 + [lat(GP), lat(CW), tile(D),
                             pl.BlockSpec((1, 1, D), lambda b, j: (b, 0, 0)),
                             _full_spec((1, W)), _full_spec((GP, W)), _full_spec((1, W)), _full_spec((1, W)),
                             _full_spec((W + CW, D)), _full_spec((1, D)), _full_spec(e_w.shape), tile(D)],
        out_specs=[ho] * 4 + [tile(GP), tile(CW), tile(W + CW), tile(D),
                              pl.BlockSpec((1, 1, D), lambda b, j: (b, 0, 0)),
                              _full_spec((1, D)), _full_spec((1, W)), _full_spec((GP, W)),
                              _full_spec((1, W)), _full_spec((1, W))],
        out_shape=[head] * 4 + [jax.ShapeDtypeStruct((B, TX, GP), F32), jax.ShapeDtypeStruct((B, TX, CW), F32),
                                jax.ShapeDtypeStruct((B, TX, W + CW), BF16), jax.ShapeDtypeStruct((B, TX, D), BF16),
                                jax.ShapeDtypeStruct((B, 1, D), F32),
                                vec(D), vec(W), jax.ShapeDtypeStruct((GP, W), F32), vec(W), vec(W)],
        compiler_params=_params(("arbitrary", "arbitrary")),
    )(yf, yb, kdf, kdb, r, v, gd, conv, x, mod2, r_k, gw2p, lnx_w, lnx_b, w_out, g2, e_w, dx1)


def _scan_bwd(ops_f, ops_b, dy, hist, e128, e256, cfg):
    G, T, NCH, NCC = cfg.G, cfg.T, cfg.NCH, cfg.NCC
    CH = SCAN_CHUNK
    G2 = 2 * G

    def body(*refs):
        ins = (refs[0:6], refs[6:12])
        dys, hist_ref, next_ref = (refs[12], refs[13]), refs[14], refs[15]
        ahead, hist_ahead = (refs[16:19], refs[19:22]), refs[22]
        e_ref, e2_ref = refs[23], refs[24]
        outs = (refs[25:31], refs[31:37])
        ds_ref, mm_ref = refs[37], refs[38]
        gi = pl.program_id(0)
        diag = _diag_mask()
        diag_b = diag.astype(BF16)
        e, e2 = e_ref[...], e2_ref[...]
        rows = functools.partial(_both_rows, ins)
        latent = [(_chunk_pos(NCH - 1 - gi, d == 1, cfg) >= NCC).astype(F32) for d in range(2)]
        latent_ahead = [(_chunk_pos(jnp.maximum(NCH - 2 - gi, 0), d == 1, cfg) >= NCC).astype(F32) for d in range(2)]

        def put(idx, i, val):
            outs[0][idx][:, pl.ds(_tok(i, False), 1), :] = val[:G]
            outs[1][idx][:, pl.ds(_tok(i, True), 1), :] = val[G:]

        rsum = lambda z: jnp.sum(z, axis=1, keepdims=True)

        def at(i):
            dy_rows = jnp.concatenate([dys[d][:, pl.ds(_tok(i, d == 1), 1), :] * latent[d] for d in range(2)], axis=0)
            return dy_rows, rows(3, i), rows(4, i), hist_ref[i]

        def at_ahead():
            edge = lambda d: 0 if d == 1 else SUBLANES - 1
            pick = lambda k, scale: jnp.concatenate(
                [ahead[d][k][:, edge(d):edge(d) + 1, :] * scale[d] for d in range(2)], axis=0)
            return pick(0, latent_ahead), pick(1, (1.0, 1.0)), pick(2, (1.0, 1.0)), hist_ahead[0]

        def prepare(dy_rows, v_rows, kk_rows, s_before):
            return (_col_form(dy_rows), _col_dot([v_rows], diag_b, e2)[0],
                    s_before.astype(BF16) * (-kk_rows).astype(BF16))

        @pl.when(gi == 0)
        def _():
            ds_ref[...] = jnp.zeros_like(ds_ref)
            dyc0, vb0, sa_lhs = prepare(*at(CH - 1))
            mm_ref[0] = dyc0
            mm_ref[1] = vb0
            mm_ref[2] = _seg_dot([sa_lhs], e)[0]

        def one_step(i, s_after, coming):
            sp, dyc = hist_ref[i], mm_ref[0]
            ds = ds_ref[...] + dyc * rows(0, i)
            put(0, i, rsum(s_after * dyc))
            put(1, i, rsum(ds * sp))
            put(5, i, rsum(ds * mm_ref[2]))
            put(2, i, rsum(ds * mm_ref[1]))
            dyc_n, vb_n, sa_lhs_n = prepare(*coming)
            dsb = ds.astype(BF16)
            res = _seg_dot([dsb * rows(5, i).astype(BF16), dsb * rows(2, i).astype(BF16), sa_lhs_n], e)
            dsa = res[0]
            put(4, i, -rsum(sp * dsa))
            put(3, i, rsum(diag * res[1]))
            mm_ref[0] = dyc_n
            mm_ref[1] = vb_n
            mm_ref[2] = res[2]
            ds_ref[...] = ds * rows(1, i) - dsa * rows(4, i)

        one_step(CH - 1, next_ref[0], at(CH - 2))

        def bstep(ii, carry):
            i = CH - 1 - ii
            one_step(i, hist_ref[i + 1], at(i - 1))
            return carry

        lax.fori_loop(1, CH - 1, bstep, 0)
        one_step(0, hist_ref[1], at_ahead())

    per = CH // SUBLANES
    cpos = lambda g, rev: _chunk_pos(NCH - 1 - g, rev, cfg)
    apos = lambda g, rev: _chunk_pos(jnp.maximum(NCH - 2 - g, 0), rev, cfg)
    toks = [pl.BlockSpec((G, CH, LANES), lambda g, rev=rev: (0, cpos(g, rev), 0)) for rev in (False, True)]
    dy_specs = [pl.BlockSpec((G, CH, LANES), lambda g, rev=rev: (0, jnp.maximum(cpos(g, rev) - NCC, 0), 0))
                for rev in (False, True)]
    edge_blk = lambda rev: 0 if rev else per - 1
    ahead_tok = [pl.BlockSpec((G, SUBLANES, LANES), lambda g, rev=rev: (0, apos(g, rev) * per + edge_blk(rev), 0))
                 for rev in (False, True)]
    ahead_dy = [pl.BlockSpec((G, SUBLANES, LANES),
                             lambda g, rev=rev: (0, jnp.maximum(apos(g, rev) - NCC, 0) * per + edge_blk(rev), 0))
                for rev in (False, True)]
    state_row = lambda f: pl.BlockSpec((1, G2, HEAD, LANES), lambda g: (f(g), 0, 0, 0))
    out = jax.ShapeDtypeStruct((G, T, LANES), F32)
    res = pl.pallas_call(
        body, name="scan_bwd", grid=(NCH,),
        in_specs=[toks[0]] * 6 + [toks[1]] * 6 + dy_specs
                 + [pl.BlockSpec((CH, G2, HEAD, LANES), lambda g: (NCH - 1 - g, 0, 0, 0)),
                    state_row(lambda g: (NCH - g) * CH)]
                 + [ahead_dy[0], ahead_tok[0], ahead_tok[0], ahead_dy[1], ahead_tok[1], ahead_tok[1],
                    state_row(lambda g: jnp.maximum((NCH - 1 - g) * CH - 1, 0)),
                    _full_spec((LANES, LANES)), _full_spec((2 * LANES, LANES))],
        out_specs=[toks[0]] * 6 + [toks[1]] * 6,
        out_shape=[out] * 12,
        scratch_shapes=[pltpu.VMEM((G2, HEAD, LANES), F32), pltpu.VMEM((3, G2, HEAD, LANES), F32)],
        compiler_params=_params(("arbitrary",)),
    )(*ops_f, *ops_b, dy, dy, hist, hist,
      dy, ops_f[3], ops_f[4], dy, ops_b[3], ops_b[4], hist, e128, e256)
    return res[:6], res[6:]


def _mix_prep_bwd(p, mup, mun, w0, w2p, a0, a2p, k_k, k_a, wm, cb, clw, clb, e_w, yconv, sf, sb, ro, xch, cfg):
    B, T, TT, SP, CP, W, CW, HP, JC, PW, GP, KC, KP = (cfg.B, cfg.T, cfg.TT, cfg.SP, cfg.CP, cfg.W, cfg.CW,
                                                       cfg.HP, cfg.JC, cfg.PW, cfg.GP, cfg.KC, cfg.KP)
    pad = KC // 2
    nx = len(xch)

    def body(p_ref, prev_ref, next_ref, mup_ref, mun_ref, w0_ref, w2_ref, a0_ref, a2_ref, kk_ref, ka_ref,
             wm_ref, cb_ref, clw_ref, clb_ref, e_ref, yc_ref, *rest):
        sf_refs, sb_refs = rest[0:6], rest[6:12]
        rdr_ref, rdv_ref, rdkb_ref, rdgd_ref, rdcv_ref = rest[12:17]
        xin, rest = rest[17:17 + nx], rest[17 + nx:]
        (dpz_ref, dmup_ref, dmun_ref, dw0_ref, dw2_ref, da0_ref, da2_ref, dkk_ref, dka_ref,
         dcw_ref, dcb_ref, dclw_ref, dclb_ref) = rest[:13]
        xout, (dyc_ref, du_ref), sems = rest[13:13 + nx], rest[13 + nx:15 + nx], rest[15 + nx:]
        b, j = pl.program_id(0), pl.program_id(1)
        first = jnp.logical_and(b == 0, j == 0)

        @pl.when(first)
        def _():
            _exchange_start(_exchange_copies(xin, xout, *sems, [True] * nx))
        lat = (j >= JC).astype(F32)
        e = e_ref[...]
        mup_v, mun_v = mup_ref[...], mun_ref[...]
        z, zp, zn, rw = _shifted(p_ref, prev_ref, next_ref, mup_v, mun_v, j, cfg)

        def prep(rw_, w0_, w2_, a0_, a2_, kk_, ka_):
            return _rwkv_prep(rw_, w0_, w2_, a0_, a2_, kk_, ka_, e, cfg) + (rw_[:, 3 * W + 2 * PW:SP],)

        _, vjp_prep = jax.vjp(prep, rw, w0_ref[...], w2_ref[...], a0_ref[...], a2_ref[...], kk_ref[...], ka_ref[...])
        fr, fw, fk, fv, fkk, fb = [_from_heads(r_, cfg) for r_ in sf_refs]
        br, bw, bk, bv, bkk, bb = [_from_heads(r_, cfg) for r_ in sb_refs]
        half_kb = (0.5 * lat) * _from_heads(rdkb_ref, cfg)
        cots = (fr + br + lat * _from_heads(rdr_ref, cfg), fv + bv + lat * _from_heads(rdv_ref, cfg), fkk + bkk,
                fw, fk + half_kb, fb, bw, bk + half_kb, bb, lat * rdgd_ref[0])
        drw, dw0, dw2, da0, da2, dkk, dka = vjp_prep(cots)
        _acc(dmup_ref, jnp.sum(drw * (zp - z), axis=0, keepdims=True), first)
        _acc(dmun_ref, jnp.sum(drw * (zn - z), axis=0, keepdims=True), first)
        for ref, val in ((dw0_ref, dw0), (dw2_ref, dw2), (da0_ref, da0), (da2_ref, da2), (dkk_ref, dkk), (dka_ref, dka)):
            _acc(ref, val, first)

        dpz_ref[0, :, 0:SP] = drw

        @pl.when(first)
        def _():
            for ref in (dcw_ref, dcb_ref, dclw_ref, dclb_ref):
                ref[...] = jnp.zeros_like(ref)

        @pl.when(j < JC)
        def _():
            dpz_ref[0, :, SP:] = jnp.zeros((TT, 2 * CW), F32)

        @pl.when(j >= JC)
        def _():
            u, vjp_glu = jax.vjp(lambda c_: _glu(c_, cfg), p_ref[0, :, SP:])
            _, vjp_post = jax.vjp(_conv_post, yc_ref[0], cb_ref[...], clw_ref[...], clb_ref[...])
            dyc, dcb, dclw, dclb = vjp_post(rdcv_ref[0])
            dyc_ref[...] = dyc
            _conv_lines(dyc_ref, wm_ref, du_ref, KC, True)
            (dcv,) = vjp_glu(du_ref[...])
            dpz_ref[0, :, SP:] = dcv
            for i in range(KC):
                dcw_ref[i:i + 1, :] += jnp.sum(dyc * _line_shift(u, i - pad), axis=0, keepdims=True)
            dcb_ref[...] += dcb
            dclw_ref[...] += dclw
            dclb_ref[...] += dclb

        @pl.when(jnp.logical_and(b == B - 1, j == cfg.J - 1))
        def _():
            _exchange_wait(_exchange_copies(xin, xout, *sems, [True] * nx))

    prev, nxt = _halo_specs(cfg, SP)
    hs = _head_spec(cfg, lambda j: j)
    hl = _head_spec(cfg, lambda j: jnp.maximum(j - JC, 0))
    latn = lambda n: pl.BlockSpec((1, TT, n), lambda b, j: (b, jnp.maximum(j - JC, 0), 0))
    hbm = pl.BlockSpec(memory_space=pltpu.HBM)
    vec = lambda n: jax.ShapeDtypeStruct((1, n), F32)
    small_shapes = [vec(SP), vec(SP), vec(2 * W), jax.ShapeDtypeStruct((PW, 2 * W), F32), vec(2 * W),
                    jax.ShapeDtypeStruct((PW, 2 * W), F32), vec(W), vec(W),
                    jax.ShapeDtypeStruct((KP, CW), F32), vec(CW), vec(CW), vec(CW)]
    return pl.pallas_call(
        body, name="mix_prep_bwd", grid=(B, cfg.J),
        in_specs=[pl.BlockSpec((1, TT, CP), lambda b, j: (b, j, 0)), prev, nxt,
                  _full_spec((1, SP)), _full_spec((1, SP)),
                  _full_spec((1, 2 * W)), _full_spec((PW, 2 * W)),
                  _full_spec((1, 2 * W)), _full_spec((PW, 2 * W)),
                  _full_spec((1, W)), _full_spec((1, W)),
                  pl.BlockSpec((KC, LINE, CW), lambda b, j: (0, 0, 0), pipeline_mode=pl.Buffered(1)),
                  _full_spec((1, CW)), _full_spec((1, CW)), _full_spec((1, CW)),
                  _full_spec(e_w.shape), pl.BlockSpec((1, TT, CW), lambda b, j: (b, j, 0))]
                 + [hs] * 12 + [hl] * 3 + [latn(GP), latn(CW)] + [hbm] * nx,
        out_specs=[pl.BlockSpec((1, TT, CP), lambda b, j: (b, j, 0))] + [_full_spec(s.shape) for s in small_shapes]
                  + [hbm] * nx,
        out_shape=[jax.ShapeDtypeStruct((B, T, CP), F32)] + small_shapes + _exchange_shapes(xch, [True] * nx),
        scratch_shapes=[pltpu.VMEM((TT, CW), F32), pltpu.VMEM((TT, CW), F32)] + _exchange_sems(nx),
        compiler_params=_params(("arbitrary", "arbitrary")),
    )(p, p, p, mup, mun, w0, w2p, a0, a2p, k_k, k_a, wm, cb, clw, clb, e_w, yconv, *sf, *sb, *ro, *xch)


def _in_proj_bwd(dpz, ctx, x, modt, g1, w_in_p, mup, mun, dx1, cfg):
    B, T, TX, D, TT, SP, CP, JC = cfg.B, cfg.T, cfg.TX, cfg.D, cfg.TT, cfg.SP, cfg.CP, cfg.JC

    def body(d_ref, prev_ref, next_ref, c_ref, x_ref, mod_ref, g_ref, w_ref, mup_ref, mun_ref, dx1_ref,
             gx_ref, dp_ref, dmod_ref, dg_ref):
        b, j = pl.program_id(0), pl.program_id(1)
        xv = jnp.where(j < JC, c_ref[0], x_ref[0])
        has_prev, has_next = _halo_flags(j, cfg)
        mp, mn = mup_ref[...], mun_ref[...]
        drw = d_ref[0, :, 0:SP]
        dprev, dnext = _shift_rows(drw, prev_ref[0, SUBLANES - 1:SUBLANES, :] * has_prev,
                                   next_ref[0, 0:1, :] * has_next)
        dz = drw * (1.0 - mp - mn) + mp * dnext + mn * dprev
        dpb = jnp.concatenate([dz, d_ref[0, :, SP:]], axis=-1).astype(BF16)
        dp_ref[0] = dpb
        dh = _bdot_nt(dpb, w_ref[...])
        _, vjp_h = jax.vjp(_rms_mod, xv, g_ref[...], mod_ref[0, 0, 0:1, :], mod_ref[0, 0, 1:2, :])
        dx, dg, dsh, dsc = vjp_h(dh)
        dmod_ref[0, 0] = jnp.concatenate([dsh, dsc], axis=0)
        _acc(dg_ref, dg, jnp.logical_and(b == 0, j == 0))

        @pl.when(j >= JC)
        def _():
            gx_ref[0] = dx + dx1_ref[0]

    prev, nxt = _halo_specs(cfg, SP)
    lat = pl.BlockSpec((1, TT, D), lambda b, j: (b, jnp.maximum(j - JC, 0), 0))
    return pl.pallas_call(
        body, name="in_proj_bwd", grid=(B, cfg.J),
        in_specs=[pl.BlockSpec((1, TT, CP), lambda b, j: (b, j, 0)), prev, nxt] + _token_specs(cfg)
                 + [pl.BlockSpec((1, 1, 2, D), lambda b, j: (b, j, 0, 0)),
                    _full_spec((1, D)), _full_spec((D, CP)), _full_spec((1, SP)), _full_spec((1, SP)), lat],
        out_specs=[lat, pl.BlockSpec((1, TT, CP), lambda b, j: (b, j, 0)),
                   pl.BlockSpec((1, 1, 2, D), lambda b, j: (b, j, 0, 0)), _full_spec((1, D))],
        out_shape=[jax.ShapeDtypeStruct((B, TX, D), F32), jax.ShapeDtypeStruct((B, T, CP), BF16),
                   jax.ShapeDtypeStruct((B, cfg.J, 2, D), F32), jax.ShapeDtypeStruct((1, D), F32)],
        compiler_params=_params(("arbitrary", "arbitrary")),
    )(dpz, dpz, dpz, ctx, x, modt, g1, w_in_p, mup, mun, dx1)


def _pick_tile(n, pref):
    for t in pref:
        if n % t == 0:
            return t
    return n


def _grad_matmul(a, g, name):
    K, M = a.shape
    N = g.shape[1]
    tm = _pick_tile(M, (512, 256, 128))
    tn = _pick_tile(N, (1024, 768, 512, 256, 128))
    tk = _pick_tile(K, (2048, 1024, 512, 256, 128, 64))
    nk = K // tk

    def body(a_ref, g_ref, o_ref):
        k = pl.program_id(2)
        _acc(o_ref, _bdot_tn(a_ref[...], g_ref[...]), k == 0)

    return pl.pallas_call(
        body, name=name, grid=(M // tm, N // tn, nk),
        in_specs=[pl.BlockSpec((tk, tm), lambda i, j, k: (k, i)),
                  pl.BlockSpec((tk, tn), lambda i, j, k: (k, j))],
        out_specs=pl.BlockSpec((tm, tn), lambda i, j, k: (i, j)),
        out_shape=jax.ShapeDtypeStruct((M, N), F32),
        compiler_params=_params(("parallel", "parallel", "arbitrary")),
    )(a, g)


def _ada_fwd(crows, ada_w, ada_b):
    D = crows.shape[1]
    n6 = ada_w.shape[1]
    tn = _pick_tile(n6, (1024, 512, 256, 128))

    def body(c_ref, w_ref, b_ref, s_ref, m_ref):
        s = _silu(c_ref[...])
        s_ref[...] = s
        m_ref[...] = _bdot(s, w_ref[...]) + b_ref[...]

    return pl.pallas_call(
        body, name="ada_fwd", grid=(n6 // tn,),
        in_specs=[_full_spec((SUBLANES, D)), pl.BlockSpec((D, tn), lambda i: (0, i)),
                  pl.BlockSpec((1, tn), lambda i: (0, i))],
        out_specs=[_full_spec((SUBLANES, D)), pl.BlockSpec((SUBLANES, tn), lambda i: (0, i))],
        out_shape=[jax.ShapeDtypeStruct((SUBLANES, D), F32), jax.ShapeDtypeStruct((SUBLANES, n6), F32)],
        compiler_params=_params(("arbitrary",)),
    )(crows, ada_w, ada_b)


def _ada_bwd(s_all, g_all, g_mine, c_ctx, ada_w, nb):
    D = s_all.shape[1]
    n6 = g_all.shape[1]
    ns = g_mine.shape[1]

    def body(s_ref, g_ref, gm_ref, c_ref, w_ref, dw_ref, db_ref, dc_ref):
        g = g_ref[...]
        dw_ref[...] = _bdot_tn(s_ref[...], gm_ref[...])
        db_ref[...] = jnp.sum(g, axis=0, keepdims=True)
        rows = lax.broadcasted_iota(jnp.int32, (g.shape[0], 1), 0)
        gc = jnp.sum(jnp.where(rows % SUBLANES == nb, g, 0.0), axis=0, keepdims=True)
        ds = _bdot_nt(gc, w_ref[...])
        c = c_ref[...]
        sg = _sigmoid(c)
        dc_ref[...] = ds * (sg + c * sg * (1.0 - sg))

    return pl.pallas_call(
        body, name="ada_bwd",
        out_shape=[jax.ShapeDtypeStruct((D, ns), F32), jax.ShapeDtypeStruct((1, n6), F32),
                   jax.ShapeDtypeStruct((1, D), F32)],
        compiler_params=_params(),
    )(s_all, g_all, g_mine, c_ctx, ada_w)


def _adamw(parts, w, m, v, name):
    P, R, C = parts.shape
    small = R * C * (P + 7) * 4 <= 4 * 1024 * 1024
    tr = R if small else _pick_tile(R, (256, 128, 64, 32, 16, 8))

    def body(p_ref, w_ref, m_ref, v_ref, g_ref, d_ref, nm_ref, nv_ref):
        g = p_ref[0].astype(F32)
        for i in range(1, P):
            g = g + p_ref[i].astype(F32)
        nm = ADAM_B1 * m_ref[...] + (1.0 - ADAM_B1) * g
        nv = ADAM_B2 * v_ref[...] + (1.0 - ADAM_B2) * (g * g)
        m_hat = nm / (1.0 - ADAM_B1 ** ADAM_STEP)
        v_hat = nv / (1.0 - ADAM_B2 ** ADAM_STEP)
        g_ref[...] = g
        d_ref[...] = -ADAM_LR * (m_hat / (jnp.sqrt(v_hat) + ADAM_EPS) + ADAM_WD * w_ref[...])
        nm_ref[...] = nm
        nv_ref[...] = nv

    blk = pl.BlockSpec((tr, C), lambda i: (i, 0))
    out = jax.ShapeDtypeStruct((R, C), F32)
    return pl.pallas_call(
        body, name=name, grid=(R // tr,),
        in_specs=[pl.BlockSpec((P, tr, C), lambda i: (0, i, 0)), blk, blk, blk],
        out_specs=[blk] * 4, out_shape=[out] * 4,
        compiler_params=_params(("parallel",)),
    )(parts, w, m, v)


def _local_step(cfg, x, c, ctx, tgt, fw, late):
    B, D, W, CW, JC, T, TX = cfg.B, cfg.D, cfg.W, cfg.CW, cfg.JC, cfg.T, cfg.TX
    e_w = _block_ones(min(W, MXU_DIM))
    e128 = _block_ones(LANES)
    e256 = jnp.concatenate([e128, e128], axis=0)
    row = lambda a: a.reshape(1, -1)

    ada_wb = fw["ada_w"].astype(BF16)
    w_in_p = _pad_cols(fw["w_in"], cfg).astype(BF16)
    mup = _pad_cols(fw["mu_prev"], cfg, True)
    mun = _pad_cols(fw["mu_next"], cfg, True)
    w0, a0 = row(fw["decay_w0"]), row(fw["iclr_a0"])
    w2p, a2p = _pair_weight(fw["decay_w2"], cfg), _pair_weight(fw["iclr_a2"], cfg)
    wm_fwd, wm_bwd = _conv_tables(fw["conv_w"], cfg.KC)
    gw2p = jnp.pad(fw["gate_w2"], ((0, cfg.GP - cfg.GR), (0, 0)))
    r_k = row(fw["r_k"])

    crows = jnp.concatenate([c, fw["c_ctx"], jnp.zeros((SUBLANES - B - 1, D), F32)], axis=0)
    s_rows, mods = _ada_fwd(crows, ada_wb, fw["ada_b"])
    mod_x = mods[:B].reshape(B, 6, D)
    mod_c = mods[B].reshape(6, D)
    modt = jnp.concatenate([jnp.broadcast_to(mod_c[None, None, 0:2], (B, JC, 2, D)),
                            jnp.broadcast_to(mod_x[:, None, 0:2], (B, cfg.JX, 2, D))], axis=1)
    mod2, mod345 = mod_x[:, 2:3], mod_x[:, 3:6]

    p, hb = _in_proj(ctx, x, modt, fw["mix_pre_g"], w_in_p, cfg)
    prep_w = (mup, mun, w0, w2p, a0, a2p, fw["k_k"], fw["k_a"])
    conv_w = (fw["conv_b"], fw["conv_ln_w"], fw["conv_ln_b"], e_w)
    (r, v, kk, w_f, kd_f, b_f, w_b, kd_b, b_b, gd, conv, yconv, g_w_out, g_w1, g_w2) = _mix_prep(
        p, *prep_w, wm_fwd, *conv_w, late, cfg)
    w_outb, w1b, w2b = g_w_out.reshape(-1, D), _blocks_to_cols(g_w1), g_w2.reshape(-1, D)
    flat = lambda a: a.reshape(cfg.G, a.shape[2], LANES)
    heads = lambda a: a.reshape(B, cfg.HP, a.shape[1], LANES)
    ops_f = tuple(flat(a) for a in (r, w_f, kd_f, v, kk, b_f))
    ops_b = tuple(flat(a) for a in (r, w_b, kd_b, v, kk, b_b))
    y_f, y_b, hist, s_fin = _scan_fwd(ops_f, ops_b, e128, e256, cfg)
    hist = lax.dynamic_update_slice_in_dim(hist, s_fin[None], cfg.NCH * SCAN_CHUNK, axis=0)
    out_args = (heads(y_f), heads(y_b), kd_f, kd_b, r, v, gd, conv, x, mod2, r_k, gw2p, fw["lnx_w"], fw["lnx_b"],
                w_outb, fw["mix_post_g"], e_w)
    x1 = _mix_out(*out_args, cfg)

    dx1, loss_t, h2b, dpreb, actb, dffb, dmod345, dg3, dg4 = _mlp_fwd_bwd(
        x1, tgt, mod345, fw["mlp_pre_g"], fw["mlp_post_g"], w1b, w2b, cfg)
    (dy, dkb, dr_c, dv_c, dgd, dconv, catb, dmixb, dmod2, dg2, drk, dgw, dlw, dlb) = _mix_out_bwd(
        *out_args, dx1, cfg)
    sf, sb = _scan_bwd(ops_f, ops_b, flat(dy), hist, e128, e256, cfg)

    tokens = lambda a: a.reshape(-1, a.shape[-1])
    d_w_out = _grad_matmul(tokens(catb), tokens(dmixb), "grad_w_out")
    d_w1 = _grad_matmul(tokens(h2b), tokens(dpreb), "grad_mlp_w1")
    d_w2 = _grad_matmul(tokens(actb), tokens(dffb), "grad_mlp_w2")
    early = [d_w_out.astype(BF16).reshape(N_DEV, -1, D), _cols_to_blocks(d_w1.astype(BF16)),
             d_w2.astype(BF16).reshape(N_DEV, -1, D)]
    (dpz, dmup, dmun, dw0, dw2p, da0, da2p, dkk, dka, dcw, dcb, dclw, dclb, x_w_out, x_w1, x_w2) = _mix_prep_bwd(
        p, *prep_w, wm_bwd, *conv_w, yconv, [heads(a) for a in sf], [heads(a) for a in sb],
        (dr_c, dv_c, dkb, dgd, dconv), early, cfg)
    grad_x, dpb, dmodt, dg1 = _in_proj_bwd(dpz, ctx, x, modt, fw["mix_pre_g"], w_in_p, mup, mun, dx1, cfg)
    d_w_in = _grad_matmul(tokens(hb), tokens(dpb), "grad_w_in")
    exchanged = {"w_out": x_w_out, "mlp_w1": x_w1, "mlp_w2": x_w2}

    grads = {
        "mix_pre_g": dg1, "mix_post_g": dg2, "mlp_pre_g": dg3, "mlp_post_g": dg4,
        "w_in": _unpad_cols(d_w_in, cfg),
        "mu_prev": _unpad_cols(dmup, cfg, True), "mu_next": _unpad_cols(dmun, cfg, True),
        "decay_w0": dw0.reshape(2, W), "decay_w2": _unpair_weight(dw2p, cfg),
        "iclr_a0": da0.reshape(2, W), "iclr_a2": _unpair_weight(da2p, cfg),
        "k_k": dkk, "k_a": dka, "r_k": drk.reshape(fw["r_k"].shape),
        "gate_w2": dgw[:cfg.GR], "lnx_w": dlw, "lnx_b": dlb,
        "conv_w": dcw[:cfg.KC], "conv_b": dcb, "conv_ln_w": dclw, "conv_ln_b": dclb,
    }
    dmod_x = jnp.concatenate([jnp.sum(dmodt[:, JC:], axis=1), dmod2, dmod345], axis=1).reshape(B, 6 * D)
    dmod_c = jnp.concatenate([jnp.sum(dmodt[:, :JC], axis=(0, 1)), jnp.zeros((4, D), F32)], axis=0).reshape(1, 6 * D)
    g_rows = jnp.concatenate([dmod_x, dmod_c, jnp.zeros((SUBLANES - B - 1, 6 * D), F32)], axis=0)
    return loss_t, grad_x, grads, exchanged, s_rows, g_rows


def _my_index():
    return 4 * lax.axis_index("x") + 2 * lax.axis_index("y") + lax.axis_index("c")


def _gather_phase(phase, ins, outs, send_sems, recv_sems, local_sems):
    n = len(ins)
    x, y, c = lax.axis_index("x"), lax.axis_index("y"), lax.axis_index("c")
    index = lambda px, py, pc: 4 * px + 2 * py + pc
    me, sibling = (x, y, c), (x, y, 1 - c)
    chips = [(1 - x, y), (x, 1 - y), (1 - x, 1 - y)]

    def copy(a, k, block, to, src=None):
        dst = outs[a].at[index(*block)]
        return pltpu.make_async_remote_copy(
            src_ref=dst if src is None else src, dst_ref=dst,
            send_sem=send_sems.at[k, a], recv_sem=recv_sems.at[k, a],
            device_id=to, device_id_type=pl.DeviceIdType.MESH)

    def local():
        return [pltpu.make_async_copy(ins[a], outs[a].at[index(*me)], local_sems.at[a]) for a in range(n)]

    def first():
        return [cp for a in range(n) for cp in
                [copy(a, 0, me, sibling, src=ins[a])]
                + [copy(a, 1 + j, me, (*chip, c), src=ins[a]) for j, chip in enumerate(chips)]]

    def passed():
        return [copy(a, 4 + j, (*chip, c), sibling) for j, chip in enumerate(chips) for a in range(n)]

    if phase == 0:
        for cp in local() + first():
            cp.start()
    elif phase == 1:
        arrived = [copy(a, 1 + j, (*chip, c), me) for j, chip in enumerate(chips) for a in range(n)]
        for got, fwd in zip(arrived, passed()):
            got.wait_recv()
            fwd.start()
    else:
        for a in range(n):
            copy(a, 0, sibling, me).wait_recv()
            for j, chip in enumerate(chips):
                copy(a, 4 + j, (*chip, 1 - c), me).wait_recv()
        for cp in first() + passed():
            cp.wait_send()
        for cp in local():
            cp.wait()


def _gather_two_level(arrays, name):
    n = len(arrays)

    def body(*refs):
        for phase in range(3):
            _gather_phase(phase, refs[:n], refs[n:2 * n], *refs[2 * n:])

    hbm = pl.BlockSpec(memory_space=pltpu.HBM)
    return pl.pallas_call(
        body, name=name, out_shape=_exchange_shapes(arrays, [False] * n),
        in_specs=[hbm] * n, out_specs=[hbm] * n, scratch_shapes=_exchange_sems(n),
    )(*arrays)


def _exchange_copies(ins, outs, send_sems, recv_sems, local_sems, scatter):
    n = len(ins)
    x, y, c = lax.axis_index("x"), lax.axis_index("y"), lax.axis_index("c")
    me = 4 * x + 2 * y + c
    flip = lambda v, f: 1 - v if f else v

    def piece(a, dest):
        return ins[a].at[dest] if scatter[a] else ins[a]

    local = [pltpu.make_async_copy(piece(a, me), outs[a].at[me], local_sems.at[a]) for a in range(n)]
    sends, recvs = [], []
    for k in range(1, N_DEV):
        fx, fy, fc = (k >> 2) & 1, (k >> 1) & 1, k & 1
        peer = (flip(x, fx), flip(y, fy), flip(c, fc))
        peer_idx = 4 * peer[0] + 2 * peer[1] + peer[2]
        for a in range(n):
            sends.append(pltpu.make_async_remote_copy(
                src_ref=piece(a, peer_idx), dst_ref=outs[a].at[me],
                send_sem=send_sems.at[k - 1, a], recv_sem=recv_sems.at[k - 1, a],
                device_id=peer, device_id_type=pl.DeviceIdType.MESH))
            recvs.append(pltpu.make_async_remote_copy(
                src_ref=piece(a, peer_idx), dst_ref=outs[a].at[peer_idx],
                send_sem=send_sems.at[k - 1, a], recv_sem=recv_sems.at[k - 1, a],
                device_id=peer, device_id_type=pl.DeviceIdType.MESH))
    return local, sends, recvs


def _exchange_start(copies):
    local, sends, _ = copies
    for cp in local + sends:
        cp.start()


def _exchange_wait(copies):
    local, sends, recvs = copies
    for cp in recvs:
        cp.wait_recv()
    for cp in sends:
        cp.wait_send()
    for cp in local:
        cp.wait()


def _exchange_shapes(arrays, scatter):
    return [jax.ShapeDtypeStruct(a.shape if s else (N_DEV,) + a.shape, a.dtype) for a, s in zip(arrays, scatter)]


def _exchange_sems(n):
    return [pltpu.SemaphoreType.DMA((N_DEV - 1, n)), pltpu.SemaphoreType.DMA((N_DEV - 1, n)),
            pltpu.SemaphoreType.DMA((n,))]


def _exchange(arrays, scatter, name):
    n = len(arrays)

    def body(*refs):
        copies = _exchange_copies(refs[:n], refs[n:2 * n], *refs[2 * n:], scatter)
        _exchange_start(copies)
        _exchange_wait(copies)

    hbm = pl.BlockSpec(memory_space=pltpu.HBM)
    return pl.pallas_call(
        body, name=name, out_shape=_exchange_shapes(arrays, scatter),
        in_specs=[hbm] * n, out_specs=[hbm] * n, scratch_shapes=_exchange_sems(n),
    )(*arrays)


def _pack(parts):
    flat = jnp.concatenate([p.reshape(-1) for p in parts])
    total = _round_up(flat.shape[0], SUBLANES * LANES)
    return jnp.pad(flat, (0, total - flat.shape[0])).reshape(-1, LANES)


def _unpack(buf, shapes):
    flat = buf.reshape(-1)
    out, pos = [], 0
    for s in shapes:
        n = int(np.prod(s))
        out.append(flat[pos:pos + n].reshape(s))
        pos += n
    return out


_SHARDED_SMALL = ("decay_w0", "decay_w2", "iclr_a0", "iclr_a2", "gate_w2", "conv_w")
_REPLICATED = ("mix_pre_g", "mix_post_g", "mlp_pre_g", "mlp_post_g", "mu_prev", "mu_next", "k_k", "k_a", "r_k",
               "lnx_w", "lnx_b", "conv_b", "conv_ln_w", "conv_ln_b")
_ADA_SMALL = ("c_ctx", "ada_b")
_WEIGHTS = ("c_ctx", "ada_w", "ada_b", "mix_pre_g", "mix_post_g", "mlp_pre_g", "mlp_post_g", "w_in", "mu_prev",
            "mu_next", "decay_w0", "decay_w2", "iclr_a0", "iclr_a2", "k_k", "k_a", "r_k", "gate_w2", "lnx_w", "lnx_b",
            "conv_w", "conv_b", "conv_ln_w", "conv_ln_b", "w_out", "mlp_w1", "mlp_w2")
_INPUTS = ("x", "c", "ctx") + _WEIGHTS + ("loss_target",) + tuple("m_" + n for n in _WEIGHTS) + tuple(
    "v_" + n for n in _WEIGHTS)


def _cols_to_blocks(a):
    a = a.reshape(a.shape[:-1] + (N_DEV, a.shape[-1] // N_DEV))
    return jnp.moveaxis(a, -2, 0)


def _blocks_to_cols(a):
    a = jnp.moveaxis(a, 0, -2)
    return a.reshape(a.shape[:-2] + (a.shape[-2] * a.shape[-1],))


def kernel(x, c, ctx, c_ctx, ada_w, ada_b, mix_pre_g, mix_post_g, mlp_pre_g, mlp_post_g, w_in, mu_prev, mu_next, decay_w0, decay_w2, iclr_a0, iclr_a2, k_k, k_a, r_k, gate_w2, lnx_w, lnx_b, conv_w, conv_b, conv_ln_w, conv_ln_b, w_out, mlp_w1, mlp_w2, loss_target, m_c_ctx, m_ada_w, m_ada_b, m_mix_pre_g, m_mix_post_g, m_mlp_pre_g, m_mlp_post_g, m_w_in, m_mu_prev, m_mu_next, m_decay_w0, m_decay_w2, m_iclr_a0, m_iclr_a2, m_k_k, m_k_a, m_r_k, m_gate_w2, m_lnx_w, m_lnx_b, m_conv_w, m_conv_b, m_conv_ln_w, m_conv_ln_b, m_w_out, m_mlp_w1, m_mlp_w2, v_c_ctx, v_ada_w, v_ada_b, v_mix_pre_g, v_mix_post_g, v_mlp_pre_g, v_mlp_post_g, v_w_in, v_mu_prev, v_mu_next, v_decay_w0, v_decay_w2, v_iclr_a0, v_iclr_a2, v_k_k, v_k_a, v_r_k, v_gate_w2, v_lnx_w, v_lnx_b, v_conv_w, v_conv_b, v_conv_ln_w, v_conv_ln_b, v_w_out, v_mlp_w1, v_mlp_w2):
    given = dict(zip(_INPUTS, (x, c, ctx, c_ctx, ada_w, ada_b, mix_pre_g, mix_post_g, mlp_pre_g, mlp_post_g, w_in, mu_prev, mu_next, decay_w0, decay_w2, iclr_a0, iclr_a2, k_k, k_a, r_k, gate_w2, lnx_w, lnx_b, conv_w, conv_b, conv_ln_w, conv_ln_b, w_out, mlp_w1, mlp_w2, loss_target, m_c_ctx, m_ada_w, m_ada_b, m_mix_pre_g, m_mix_post_g, m_mlp_pre_g, m_mlp_post_g, m_w_in, m_mu_prev, m_mu_next, m_decay_w0, m_decay_w2, m_iclr_a0, m_iclr_a2, m_k_k, m_k_a, m_r_k, m_gate_w2, m_lnx_w, m_lnx_b, m_conv_w, m_conv_b, m_conv_ln_w, m_conv_ln_b, m_w_out, m_mlp_w1, m_mlp_w2, v_c_ctx, v_ada_w, v_ada_b, v_mix_pre_g, v_mix_post_g, v_mlp_pre_g, v_mlp_post_g, v_w_in, v_mu_prev, v_mu_next, v_decay_w0, v_decay_w2, v_iclr_a0, v_iclr_a2, v_k_k, v_k_a, v_r_k, v_gate_w2, v_lnx_w, v_lnx_b, v_conv_w, v_conv_b, v_conv_ln_w, v_conv_ln_b, v_w_out, v_mlp_w1, v_mlp_w2)))
    loc = {}
    for pre in ("", "m_", "v_"):
        for n in _WEIGHTS:
            a = given[pre + n]
            a = a.reshape(1, -1) if n == "c_ctx" else a[0]
            loc[pre + n] = a.reshape(1, -1) if a.ndim == 1 else a
    B, TX, D = x.shape
    W, CW = loc["k_k"].shape[1], loc["conv_b"].shape[1]
    cfg = _Cfg(B, TX, ctx.shape[1], D, W, CW, loc["decay_w2"].shape[1], loc["gate_w2"].shape[0],
               loc["conv_w"].shape[0], loc["mlp_w1"].shape[1] * N_DEV)
    me = _my_index()

    small_shapes = [loc[n].shape for n in _SHARDED_SMALL]
    got = _gather_two_level(
        [loc["ada_w"].astype(BF16), loc["w_in"].astype(BF16), _pack([loc[n] for n in _SHARDED_SMALL])],
        "gather_weights")
    fw = {n: loc[n] for n in _REPLICATED + _ADA_SMALL}
    fw["ada_w"] = _blocks_to_cols(got[0])
    fw["w_in"] = _blocks_to_cols(got[1])
    per_dev = [_unpack(got[2][i], small_shapes) for i in range(N_DEV)]
    for j, n in enumerate(_SHARDED_SMALL):
        fw[n] = jnp.concatenate([per_dev[i][j] for i in range(N_DEV)], axis=-1)
    late = [loc[n].astype(BF16) for n in ("w_out", "mlp_w1", "mlp_w2")]

    loss_t, grad_x, grads, exchanged, s_rows, g_rows = _local_step(cfg, x, c, ctx, loss_target, fw, late)
    loss = lax.psum(jnp.sum(loss_t[:, :, 0, 0]), ("x", "y", "c"))

    small_blocks = jnp.stack([_pack([_cols_to_blocks(grads[n])[i] for n in _SHARDED_SMALL]) for i in range(N_DEV)])
    sent = _exchange(
        [_cols_to_blocks(grads["w_in"].astype(BF16)), small_blocks,
         _pack([grads[n] for n in _REPLICATED]), s_rows, g_rows],
        [True] * 2 + [False] * 3, "exchange_grads")
    s_all = sent[3].reshape(N_DEV * SUBLANES, D)
    g_all = sent[4].reshape(N_DEV * SUBLANES, 6 * D)
    ns = 6 * D // N_DEV
    g_mine = lax.dynamic_slice_in_dim(g_all, me * ns, ns, axis=1)
    d_ada_w, d_ada_b, d_c_ctx = _ada_bwd(s_all, g_all, g_mine, loc["c_ctx"], fw["ada_w"], B)

    res = {}

    def update(name, parts):
        res[name] = _adamw(parts, loc[name], loc["m_" + name], loc["v_" + name], "adamw_" + name)

    update("w_in", sent[0])
    for n in ("w_out", "mlp_w1", "mlp_w2"):
        update(n, exchanged[n])
    update("ada_w", d_ada_w[None])

    def update_packed(names, parts, tag):
        shapes = [loc[n].shape for n in names]
        packed = _adamw(parts, *[_pack([loc[pre + n] for n in names]) for pre in ("", "m_", "v_")], "adamw_" + tag)
        unpacked = [_unpack(p, shapes) for p in packed]
        for j, n in enumerate(names):
            res[n] = tuple(u[j] for u in unpacked)

    update_packed(_SHARDED_SMALL, sent[1], "sharded_small")
    update_packed(_REPLICATED, sent[2], "replicated")
    update_packed(_ADA_SMALL, _pack([d_c_ctx, d_ada_b])[None], "ada_small")

    outs = [loss, grad_x]
    for k in range(4):
        for n in _WEIGHTS:
            outs.append(res[n][k].reshape(given[n].shape))
    return tuple(outs)
```

```python
import functools

import numpy as np
import jax
import jax.numpy as jnp
from jax import lax
from jax.experimental import pallas as pl
from jax.experimental.pallas import tpu as pltpu

F32 = jnp.float32
BF16 = jnp.bfloat16

EPS_RMS = 1e-6
EPS_LN = 1e-5
EPS_GN = 64e-5
LINE = 64
HEAD = 64
LANES = 128
SUBLANES = 8
MXU_DIM = 256
SCAN_CHUNK = 16
N_DEV = 8
VMEM_LIMIT = 56 * 1024 * 1024

ADAM_LR = 0.001
ADAM_B1 = 0.9
ADAM_B2 = 0.999
ADAM_EPS = 1e-08
ADAM_WD = 0.01
ADAM_STEP = 10


def _round_up(n, m):
    return (n + m - 1) // m * m


def _params(semantics=None, vmem=VMEM_LIMIT):
    return pltpu.CompilerParams(dimension_semantics=semantics, vmem_limit_bytes=vmem)


def _bdot(a, b):
    return jnp.dot(a.astype(BF16), b.astype(BF16), preferred_element_type=F32)


def _bdot_nt(a, b):
    return lax.dot_general(a.astype(BF16), b.astype(BF16), (((1,), (1,)), ((), ())),
                           preferred_element_type=F32)


def _bdot_tn(a, b):
    return lax.dot_general(a.astype(BF16), b.astype(BF16), (((0,), (0,)), ((), ())),
                           preferred_element_type=F32)


@jax.custom_vjp
def _mm(a, b):
    return _bdot(a, b)


def _mm_fwd(a, b):
    return _bdot(a, b), (a, b)


def _mm_bwd(res, g):
    a, b = res
    return _bdot_nt(g, b), _bdot_tn(a, g)


_mm.defvjp(_mm_fwd, _mm_bwd)


def _seg_sum_raw(x, e):
    hi = x.astype(BF16)
    lo = (x - hi.astype(F32)).astype(BF16)
    n = e.shape[0]
    parts = [jnp.dot(hi[:, s:s + n], e, preferred_element_type=F32)
             + jnp.dot(lo[:, s:s + n], e, preferred_element_type=F32) for s in range(0, x.shape[1], n)]
    return parts[0] if len(parts) == 1 else jnp.concatenate(parts, axis=1)


@jax.custom_vjp
def _seg_sum(x, e):
    return _seg_sum_raw(x, e)


def _seg_sum_fwd(x, e):
    return _seg_sum_raw(x, e), e


def _seg_sum_bwd(e, g):
    return _seg_sum_raw(g, e), None


_seg_sum.defvjp(_seg_sum_fwd, _seg_sum_bwd)


def _block_ones(n, seg=HEAD):
    i = np.arange(n) // seg
    return jnp.asarray((i[:, None] == i[None, :]).astype(np.float32), dtype=BF16)


def _rms(xv, g):
    ms = jnp.mean(xv * xv, axis=-1, keepdims=True)
    return xv * lax.rsqrt(ms + EPS_RMS) * g


def _rms_mod(xv, g, shift, scale):
    return _rms(xv, g) * (1.0 + scale) + shift


def _sigmoid(z):
    return 1.0 / (1.0 + jnp.exp(-z))


def _silu(z):
    return z * _sigmoid(z)


def _softplus(z):
    return jnp.maximum(z, 0.0) + jnp.log(1.0 + jnp.exp(-jnp.abs(z)))


class _Cfg:
    def __init__(self, B, TX, TC, D, W, CW, R, GR, KC, F):
        self.B, self.TX, self.TC, self.D = B, TX, TC, D
        self.W, self.CW, self.R, self.GR, self.KC, self.F = W, CW, R, GR, KC, F
        self.T = TX + TC
        self.TT = min(256, TC)
        assert TC % self.TT == 0 and TX % self.TT == 0 and self.TT % LINE == 0
        self.JC = TC // self.TT
        self.JX = TX // self.TT
        self.J = self.JC + self.JX
        self.HP = W // LANES
        self.G = B * self.HP
        self.PW = _round_up(2 * R, LANES)
        self.GP = _round_up(GR, LANES)
        self.SP = 3 * W + 2 * self.PW + self.GP
        self.CP = self.SP + 2 * CW
        self.KP = _round_up(KC, SUBLANES)
        assert self.T % SCAN_CHUNK == 0 and TC % SCAN_CHUNK == 0
        self.NCH = self.T // SCAN_CHUNK
        self.NCC = TC // SCAN_CHUNK
        W_, R_ = W, R
        segs = [(0, 3 * W_, 0),
                (3 * W_, 2 * R_, 3 * W_),
                (3 * W_ + 2 * R_, 2 * R_, 3 * W_ + self.PW),
                (3 * W_ + 4 * R_, GR, 3 * W_ + 2 * self.PW),
                (3 * W_ + 4 * R_ + GR, 2 * CW, self.SP)]
        self.col_segs = segs
        self.shift_cols = 3 * W_ + 4 * R_ + GR
        self.in_cols = self.shift_cols + 2 * CW


def _pad_cols(a, cfg, upto_shift=False):
    width = cfg.SP if upto_shift else cfg.CP
    pieces, pos = [], 0
    for src, n, dst in cfg.col_segs:
        if upto_shift and dst >= cfg.SP:
            break
        if dst > pos:
            pieces.append(jnp.zeros(a.shape[:-1] + (dst - pos,), a.dtype))
        pieces.append(a[..., src:src + n])
        pos = dst + n
    if width > pos:
        pieces.append(jnp.zeros(a.shape[:-1] + (width - pos,), a.dtype))
    return jnp.concatenate(pieces, axis=-1)


def _unpad_cols(a, cfg, upto_shift=False):
    pieces = []
    for src, n, dst in cfg.col_segs:
        if upto_shift and dst >= cfg.SP:
            break
        pieces.append(a[..., dst:dst + n])
    return jnp.concatenate(pieces, axis=-1)


def _pair_weight(w2, cfg):
    R, W = cfg.R, cfg.W
    out = jnp.zeros((cfg.PW, 2 * W), w2.dtype)
    out = out.at[0:R, 0:W].set(w2[0])
    out = out.at[R:2 * R, W:2 * W].set(w2[1])
    return out


def _unpair_weight(g, cfg):
    R, W = cfg.R, cfg.W
    return jnp.stack([g[0:R, 0:W], g[R:2 * R, W:2 * W]])


def _row_ids(n):
    return lax.broadcasted_iota(jnp.int32, (n, 1), 0)


def _shift_rows(z, prev_row, next_row):
    n = z.shape[0]
    rows = _row_ids(n)
    zp = jnp.where(rows == 0, prev_row, pltpu.roll(z, 1, 0))
    zn = jnp.where(rows == n - 1, next_row, pltpu.roll(z, n - 1, 0))
    return zp, zn


def _line_shift(u, d):
    if d == 0:
        return u
    n = u.shape[0]
    lt = _row_ids(n) % LINE
    ok = jnp.logical_and(lt + d >= 0, lt + d < LINE)
    return jnp.where(ok, pltpu.roll(u, (-d) % n, 0), 0.0)


def _conv_tables(cw, kc):
    pad = kc // 2
    t = np.arange(LINE)[None, :]
    d = (np.arange(kc) - pad)[:, None]
    fwd = ((t + d >= 0) & (t + d < LINE)).astype(np.float32)
    bwd = ((t - d >= 0) & (t - d < LINE)).astype(np.float32)
    w = cw[:kc, None, :]
    return jnp.asarray(fwd)[:, :, None] * w, jnp.asarray(bwd)[:, :, None] * w


def _conv_lines(src_ref, wm_ref, dst_ref, kc, transpose):
    pad = kc // 2
    n, width = src_ref.shape
    wide = 2 * LANES if width % (2 * LANES) == 0 else LANES
    for l in range(n // LINE):
        for b in range(width // wide):
            rs, cs = slice(l * LINE, (l + 1) * LINE), slice(b * wide, (b + 1) * wide)
            tile = src_ref[rs, cs]
            acc = jnp.zeros_like(tile)
            for i in range(kc):
                d = (pad - i) if transpose else (i - pad)
                acc = acc + pltpu.roll(tile, (-d) % LINE, 0) * wm_ref[i, :, cs]
            dst_ref[rs, cs] = acc


def _rwkv_prep(rw, w0, w2p, a0, a2p, k_k, k_a, e, cfg):
    W, PW = cfg.W, cfg.PW
    r = rw[:, 0:W]
    k = rw[:, W:2 * W]
    v = rw[:, 2 * W:3 * W]
    wdp = rw[:, 3 * W:3 * W + PW]
    adp = rw[:, 3 * W + PW:3 * W + 2 * PW]
    wl = w0 + _mm(jnp.tanh(wdp), w2p)
    w_log = -_softplus(-wl) - 0.5
    decay = jnp.exp(-jnp.exp(w_log))
    iclr = _sigmoid(a0 + _mm(adp, a2p))
    kkr = k * k_k
    nrm = jnp.sqrt(_seg_sum(kkr * kkr, e))
    kk = kkr / jnp.maximum(nrm, 1e-12)
    outs = [r, v, kk]
    for d in range(2):
        ic = iclr[:, d * W:(d + 1) * W]
        outs += [decay[:, d * W:(d + 1) * W], k * (1.0 + (ic - 1.0) * k_a), kk * ic]
    return tuple(outs)


def _glu(cv, cfg):
    return cv[:, :cfg.CW] * _sigmoid(cv[:, cfg.CW:])


def _conv_post(y, cb, lw, lb):
    yf = y + cb
    mu = jnp.mean(yf, axis=-1, keepdims=True)
    var = jnp.mean(jnp.square(yf - mu), axis=-1, keepdims=True)
    return _silu((yf - mu) * lax.rsqrt(var + EPS_LN) * lw + lb)


def _readout(y, kbar, r, v, gd, r_k, gw2, lnx_w, lnx_b, e):
    inv = 1.0 / HEAD
    mu = _seg_sum(y, e) * inv
    yc = y - mu
    var = _seg_sum(yc * yc, e) * inv
    yn = yc * lax.rsqrt(var + EPS_GN) * lnx_w + lnx_b
    bonus = _seg_sum(r * kbar * r_k, e) * v
    g = _mm(_sigmoid(gd), gw2)
    return (yn + bonus) * g


def _post_res(xv, mix, gate, g):
    return xv + gate * _rms(mix, g)


def _head_spec(cfg, tmap):
    return pl.BlockSpec((1, cfg.HP, cfg.TT, LANES), lambda b, j: (b, 0, tmap(j), 0))


def _full_spec(shape):
    n = len(shape)
    return pl.BlockSpec(shape, lambda *_: (0,) * n)


def _to_heads(ref, val, cfg):
    for hp in range(cfg.HP):
        ref[0, hp] = val[:, hp * LANES:(hp + 1) * LANES]


def _from_heads(ref, cfg):
    return jnp.concatenate([ref[0, hp] for hp in range(cfg.HP)], axis=-1)


def _token_specs(cfg):
    TT, D, JC = cfg.TT, cfg.D, cfg.JC
    return [pl.BlockSpec((1, TT, D), lambda b, j: (b, jnp.minimum(j, JC - 1), 0)),
            pl.BlockSpec((1, TT, D), lambda b, j: (b, jnp.maximum(j - JC, 0), 0))]


def _in_proj(ctx, x, modt, g1, w_in_p, cfg):
    B, T, D, TT, CP, JC = cfg.B, cfg.T, cfg.D, cfg.TT, cfg.CP, cfg.JC

    def body(c_ref, x_ref, mod_ref, g_ref, w_ref, p_ref, h_ref):
        xv = jnp.where(pl.program_id(1) < JC, c_ref[0], x_ref[0])
        h = _rms_mod(xv, g_ref[...], mod_ref[0, 0, 0:1, :], mod_ref[0, 0, 1:2, :])
        hb = h.astype(BF16)
        h_ref[0] = hb
        p_ref[0] = jnp.dot(hb, w_ref[...], preferred_element_type=F32)

    return pl.pallas_call(
        body, name="in_proj", grid=(B, cfg.J),
        in_specs=_token_specs(cfg) + [pl.BlockSpec((1, 1, 2, D), lambda b, j: (b, j, 0, 0)),
                                      _full_spec((1, D)), _full_spec((D, CP))],
        out_specs=[pl.BlockSpec((1, TT, CP), lambda b, j: (b, j, 0)),
                   pl.BlockSpec((1, TT, D), lambda b, j: (b, j, 0))],
        out_shape=[jax.ShapeDtypeStruct((B, T, CP), F32), jax.ShapeDtypeStruct((B, T, D), BF16)],
        compiler_params=_params(("arbitrary", "arbitrary")),
    )(ctx, x, modt, g1, w_in_p)


def _halo_specs(cfg, width):
    per = cfg.TT // SUBLANES
    last = cfg.T // SUBLANES - 1
    prev = pl.BlockSpec((1, SUBLANES, width), lambda b, j: (b, jnp.maximum(j * per - 1, 0), 0))
    nxt = pl.BlockSpec((1, SUBLANES, width), lambda b, j: (b, jnp.minimum((j + 1) * per, last), 0))
    return prev, nxt


def _halo_flags(j, cfg):
    has_prev = jnp.logical_and(j != 0, j != cfg.JC).astype(F32)
    has_next = jnp.logical_and(j != cfg.JC - 1, j != cfg.J - 1).astype(F32)
    return has_prev, has_next


def _shifted(p_ref, prev_ref, next_ref, mup, mun, j, cfg):
    SP = cfg.SP
    has_prev, has_next = _halo_flags(j, cfg)
    z = p_ref[0][:, :SP]
    zp, zn = _shift_rows(z, prev_ref[0, SUBLANES - 1:SUBLANES, :] * has_prev, next_ref[0, 0:1, :] * has_next)
    return z, zp, zn, z + mup * (zp - z) + mun * (zn - z)


def _mix_prep(p, mup, mun, w0, w2p, a0, a2p, k_k, k_a, wm, cb, clw, clb, e_w, late, cfg):
    B, T, TT, SP, CP, W, CW, HP, JC = cfg.B, cfg.T, cfg.TT, cfg.SP, cfg.CP, cfg.W, cfg.CW, cfg.HP, cfg.JC
    nl = len(late)
    steps = B * cfg.J

    def body(p_ref, prev_ref, next_ref, mup_ref, mun_ref, w0_ref, w2_ref, a0_ref, a2_ref, kk_ref, ka_ref,
             wm_ref, cb_ref, clw_ref, clb_ref, e_ref, *rest):
        late_in, rest = rest[:nl], rest[nl:]
        outs, late_out, u_ref, sems = rest[:12], rest[12:12 + nl], rest[12 + nl], rest[13 + nl:]
        j = pl.program_id(1)
        step = pl.program_id(0) * cfg.J + j
        for phase, at in enumerate((0, steps // 2)):
            @pl.when(step == at)
            def _(phase=phase):
                _gather_phase(phase, late_in, late_out, *sems)
        _, _, _, rw = _shifted(p_ref, prev_ref, next_ref, mup_ref[...], mun_ref[...], j, cfg)
        vals = _rwkv_prep(rw, w0_ref[...], w2_ref[...], a0_ref[...], a2_ref[...], kk_ref[...], ka_ref[...],
                          e_ref[...], cfg)
        for ref, val in zip(outs[:9], vals):
            _to_heads(ref, val, cfg)
        outs[9][0] = rw[:, 3 * W + 2 * cfg.PW:SP]

        @pl.when(j >= JC)
        def _():
            u_ref[...] = _glu(p_ref[0, :, SP:], cfg)
            _conv_lines(u_ref, wm_ref, outs[11].at[0], cfg.KC, False)
            outs[10][0] = _conv_post(outs[11][0], cb_ref[...], clw_ref[...], clb_ref[...])

        @pl.when(step == steps - 1)
        def _():
            _gather_phase(2, late_in, late_out, *sems)

    prev, nxt = _halo_specs(cfg, SP)
    head = jax.ShapeDtypeStruct((B, HP, T, LANES), F32)
    hbm = pl.BlockSpec(memory_space=pltpu.HBM)
    tile = lambda n: pl.BlockSpec((1, TT, n), lambda b, j: (b, j, 0))
    return pl.pallas_call(
        body, name="mix_prep", grid=(B, cfg.J),
        in_specs=[tile(CP), prev, nxt,
                  _full_spec((1, SP)), _full_spec((1, SP)),
                  _full_spec((1, 2 * W)), _full_spec((cfg.PW, 2 * W)),
                  _full_spec((1, 2 * W)), _full_spec((cfg.PW, 2 * W)),
                  _full_spec((1, W)), _full_spec((1, W)),
                  pl.BlockSpec((cfg.KC, LINE, CW), lambda b, j: (0, 0, 0), pipeline_mode=pl.Buffered(1)),
                  _full_spec((1, CW)), _full_spec((1, CW)), _full_spec((1, CW)),
                  _full_spec(e_w.shape)] + [hbm] * nl,
        out_specs=[_head_spec(cfg, lambda j: j)] * 9 + [tile(cfg.GP), tile(CW), tile(CW)] + [hbm] * nl,
        out_shape=[head] * 9 + [jax.ShapeDtypeStruct((B, T, cfg.GP), F32),
                                jax.ShapeDtypeStruct((B, T, CW), F32), jax.ShapeDtypeStruct((B, T, CW), F32)]
                  + _exchange_shapes(late, [False] * nl),
        scratch_shapes=[pltpu.VMEM((TT, CW), F32)] + _exchange_sems(nl),
        compiler_params=_params(("arbitrary", "arbitrary")),
    )(p, p, p, mup, mun, w0, w2p, a0, a2p, k_k, k_a, wm, cb, clw, clb, e_w, *late)


def _chunk_pos(c, reverse, cfg):
    if not reverse:
        return c
    return jnp.where(c < cfg.NCC, cfg.NCC - 1 - c, cfg.NCH - 1 + cfg.NCC - c)


def _diag_mask():
    r = lax.broadcasted_iota(jnp.int32, (HEAD, LANES), 0)
    l = lax.broadcasted_iota(jnp.int32, (HEAD, LANES), 1)
    return (r == l % HEAD).astype(F32)


def _col_lhs(row, diag_b):
    hi = row.astype(BF16)
    lo = (row - hi.astype(F32)).astype(BF16)
    return diag_b * hi, diag_b * lo


def _col_dot(row_list, diag_b, e2):
    n, g = len(row_list), row_list[0].shape[0]
    lhs = jnp.concatenate([jnp.concatenate(_col_lhs(r, diag_b), axis=-1) for r in row_list], axis=0)
    out = jnp.dot(lhs.reshape(n * g * HEAD, 2 * LANES), e2, preferred_element_type=F32)
    return out.reshape(n, g, HEAD, LANES)


def _col_form(row):
    n = row.shape[0]
    t = jnp.swapaxes(jnp.broadcast_to(row, (n, LANES, LANES)), 1, 2)
    lane = lax.broadcasted_iota(jnp.int32, (HEAD, LANES), 1)
    return jnp.where(lane < HEAD, t[:, :HEAD, :], t[:, HEAD:, :])


def _col_both(row, diag_b, e2):
    half = row.shape[0] // 2
    return jnp.concatenate([_col_form(row[:half]), _col_dot([row[half:]], diag_b, e2)[0]], axis=0)


def _both_rows(ins, idx, i):
    return jnp.concatenate([ins[d][idx][:, pl.ds(_tok(i, d == 1), 1), :] for d in range(2)], axis=0)


def _seg_dot(blocks, e):
    n, g = len(blocks), blocks[0].shape[0]
    lhs = jnp.concatenate(blocks, axis=0).reshape(n * g * HEAD, LANES)
    return jnp.dot(lhs, e, preferred_element_type=F32).reshape(n, g, HEAD, LANES)


def _tok(i, reverse):
    return (SCAN_CHUNK - 1 - i) if reverse else i


def _scan_fwd(ops_f, ops_b, e128, e256, cfg):
    G, T, NCH = cfg.G, cfg.T, cfg.NCH
    CH = SCAN_CHUNK
    G2 = 2 * G

    def body(*refs):
        ins = (refs[0:6], refs[6:12])
        ahead = (refs[12:14], refs[14:16])
        e_ref, e2_ref = refs[16], refs[17]
        ys, hist_ref, fin_ref = (refs[18], refs[19]), refs[20], refs[21]
        s_ref, mm_ref = refs[22], refs[23]
        c = pl.program_id(0)
        diag = _diag_mask()
        diag_b = diag.astype(BF16)
        e, e2 = e_ref[...], e2_ref[...]

        @pl.when(c == 0)
        def _():
            s_ref[...] = jnp.zeros_like(s_ref)
            mm_ref[0] = jnp.zeros_like(s_ref)
            mm_ref[1] = _col_both(_both_rows(ins, 3, 0), diag_b, e2)

        def step(i, nxt_row):
            res = []
            for d in range(2):
                sl = slice(d * G, (d + 1) * G)
                row = lambda idx: ins[d][idx][:, pl.ds(_tok(i, d == 1), 1), :]
                s_old = s_ref[sl]
                hist_ref[i, sl] = s_old
                S = s_old * row(1) + mm_ref[0, sl] * row(5) + mm_ref[1, sl] * row(2)
                s_ref[sl] = S
                sb = S.astype(BF16)
                res.append(_seg_dot([sb * (-nxt_row(d, 4)).astype(BF16), sb * row(0).astype(BF16)], e))
            for d in range(2):
                sl = slice(d * G, (d + 1) * G)
                v_next = nxt_row(d, 3)
                mm_ref[0, sl] = res[d][0]
                mm_ref[1, sl] = _col_form(v_next) if d == 0 else _col_dot([v_next], diag_b, e2)[0]
                ys[d][:, pl.ds(_tok(i, d == 1), 1), :] = jnp.sum(diag * res[d][1], axis=1, keepdims=True)

        def inner(i, carry):
            step(i, lambda d, idx: ins[d][idx][:, pl.ds(_tok(i + 1, d == 1), 1), :])
            return carry

        lax.fori_loop(0, CH - 1, inner, 0)
        edge = lambda d: (SUBLANES - 1) if d == 1 else 0
        step(CH - 1, lambda d, idx: ahead[d][0 if idx == 4 else 1][:, edge(d):edge(d) + 1, :])
        fin_ref[...] = s_ref[...]

    toks = [pl.BlockSpec((G, CH, LANES), lambda c, rev=rev: (0, _chunk_pos(c, rev, cfg), 0)) for rev in (False, True)]
    per = CH // SUBLANES

    def ahead_block(c, rev):
        pos = _chunk_pos(jnp.minimum(c + 1, NCH - 1), rev, cfg)
        return pos * per + (per - 1 if rev else 0)

    ahead = [pl.BlockSpec((G, SUBLANES, LANES), lambda c, rev=rev: (0, ahead_block(c, rev), 0)) for rev in (False, True)]
    y_shape = jax.ShapeDtypeStruct((G, T, LANES), F32)
    return pl.pallas_call(
        body, name="scan_fwd", grid=(NCH,),
        in_specs=[toks[0]] * 6 + [toks[1]] * 6 + [ahead[0]] * 2 + [ahead[1]] * 2
                 + [_full_spec((LANES, LANES)), _full_spec((2 * LANES, LANES))],
        out_specs=[toks[0], toks[1], pl.BlockSpec((CH, G2, HEAD, LANES), lambda c: (c, 0, 0, 0)),
                   _full_spec((G2, HEAD, LANES))],
        out_shape=[y_shape, y_shape, jax.ShapeDtypeStruct(((NCH + 1) * CH, G2, HEAD, LANES), F32),
                   jax.ShapeDtypeStruct((G2, HEAD, LANES), F32)],
        scratch_shapes=[pltpu.VMEM((G2, HEAD, LANES), F32), pltpu.VMEM((2, G2, HEAD, LANES), F32)],
        compiler_params=_params(("arbitrary",)),
    )(*ops_f, *ops_b, ops_f[4], ops_f[3], ops_b[4], ops_b[3], e128, e256)


def _mix_out(yf, yb, kdf, kdb, r, v, gd, conv, x, mod2, r_k, gw2p, lnx_w, lnx_b, w_out, g2, e_w, cfg):
    B, TX, D, TT, W, CW, JC = cfg.B, cfg.TX, cfg.D, cfg.TT, cfg.W, cfg.CW, cfg.JC

    def body(yf_ref, yb_ref, kdf_ref, kdb_ref, r_ref, v_ref, gd_ref, cv_ref, x_ref, mod_ref,
             rk_ref, gw_ref, lw_ref, lb_ref, wo_ref, g_ref, e_ref, x1_ref):
        y = _from_heads(yf_ref, cfg) + _from_heads(yb_ref, cfg)
        kbar = 0.5 * (_from_heads(kdf_ref, cfg) + _from_heads(kdb_ref, cfg))
        ro = _readout(y, kbar, _from_heads(r_ref, cfg), _from_heads(v_ref, cfg), gd_ref[0], rk_ref[...],
                      gw_ref[...], lw_ref[...], lb_ref[...], e_ref[...])
        cat = jnp.concatenate([ro, cv_ref[0]], axis=-1)
        mix = _bdot(cat, wo_ref[...])
        x1_ref[0] = _post_res(x_ref[0], mix, mod_ref[0], g_ref[...])

    hs = _head_spec(cfg, lambda j: j + JC)
    lat = lambda n: pl.BlockSpec((1, TT, n), lambda b, j: (b, j + JC, 0))
    return pl.pallas_call(
        body, name="mix_out", grid=(B, cfg.JX),
        in_specs=[hs] * 6 + [lat(cfg.GP), lat(CW),
                             pl.BlockSpec((1, TT, D), lambda b, j: (b, j, 0)),
                             pl.BlockSpec((1, 1, D), lambda b, j: (b, 0, 0)),
                             _full_spec((1, W)), _full_spec((cfg.GP, W)), _full_spec((1, W)), _full_spec((1, W)),
                             _full_spec((W + CW, D)), _full_spec((1, D)), _full_spec(e_w.shape)],
        out_specs=pl.BlockSpec((1, TT, D), lambda b, j: (b, j, 0)),
        out_shape=jax.ShapeDtypeStruct((B, TX, D), F32),
        compiler_params=_params(("parallel", "parallel")),
    )(yf, yb, kdf, kdb, r, v, gd, conv, x, mod2, r_k, gw2p, lnx_w, lnx_b, w_out, g2, e_w)


def _acc(ref, val, first):
    @pl.when(first)
    def _():
        ref[...] = val

    @pl.when(jnp.logical_not(first))
    def _():
        ref[...] += val


def _mlp_fwd_bwd(x1, tgt, mod345, g3, g4, w1, w2, cfg):
    B, TX, D, TT, F, JX = cfg.B, cfg.TX, cfg.D, cfg.TT, cfg.F, cfg.JX

    def body(x1_ref, t_ref, mod_ref, g3_ref, g4_ref, w1_ref, w2_ref,
             dx1_ref, loss_ref, h2_ref, dpre_ref, act_ref, dff_ref, dmod_ref, dg3_ref, dg4_ref):
        b, j = pl.program_id(0), pl.program_id(1)
        x1v = x1_ref[0]
        sh, sc, gt = mod_ref[0, 0:1, :], mod_ref[0, 1:2, :], mod_ref[0, 2:3, :]
        h2, vjp_pre = jax.vjp(_rms_mod, x1v, g3_ref[...], sh, sc)
        h2b = h2.astype(BF16)
        pre = jnp.dot(h2b, w1_ref[...], preferred_element_type=F32)
        rl = jnp.maximum(pre, 0.0)
        actb = (rl * rl).astype(BF16)
        ff = jnp.dot(actb, w2_ref[...], preferred_element_type=F32)
        x2, vjp_post = jax.vjp(_post_res, x1v, ff, gt, g4_ref[...])
        err = x2 - t_ref[0]
        loss = 0.5 * jnp.sum(jnp.mean(err * err, axis=-1, keepdims=True))
        dx1a, dff, dgt, dg4 = vjp_post(err * (1.0 / D))
        dffb = dff.astype(BF16)
        dpre = _bdot_nt(dffb, w2_ref[...]) * (2.0 * rl)
        dpreb = dpre.astype(BF16)
        dx1b, dg3, dsh, dsc = vjp_pre(_bdot_nt(dpreb, w1_ref[...]))
        dx1_ref[0] = dx1a + dx1b
        loss_ref[0, 0] = jnp.zeros((SUBLANES, LANES), F32) + loss
        h2_ref[0] = h2b
        dpre_ref[0] = dpreb
        act_ref[0] = actb
        dff_ref[0] = dffb
        _acc(dmod_ref, jnp.concatenate([dsh, dsc, dgt], axis=0)[None], j == 0)
        first = jnp.logical_and(b == 0, j == 0)
        _acc(dg3_ref, dg3, first)
        _acc(dg4_ref, dg4, first)

    tile = lambda n: pl.BlockSpec((1, TT, n), lambda b, j: (b, j, 0))
    return pl.pallas_call(
        body, name="mlp_fwd_bwd", grid=(B, JX),
        in_specs=[tile(D), tile(D), pl.BlockSpec((1, 3, D), lambda b, j: (b, 0, 0)),
                  _full_spec((1, D)), _full_spec((1, D)),
                  pl.BlockSpec((D, F), lambda b, j: (0, 0), pipeline_mode=pl.Buffered(1)),
                  pl.BlockSpec((F, D), lambda b, j: (0, 0), pipeline_mode=pl.Buffered(1))],
        out_specs=[tile(D), pl.BlockSpec((1, 1, SUBLANES, LANES), lambda b, j: (b, j, 0, 0)),
                   tile(D), tile(F), tile(F), tile(D),
                   pl.BlockSpec((1, 3, D), lambda b, j: (b, 0, 0)),
                   _full_spec((1, D)), _full_spec((1, D))],
        out_shape=[jax.ShapeDtypeStruct((B, TX, D), F32),
                   jax.ShapeDtypeStruct((B, JX, SUBLANES, LANES), F32),
                   jax.ShapeDtypeStruct((B, TX, D), BF16), jax.ShapeDtypeStruct((B, TX, F), BF16),
                   jax.ShapeDtypeStruct((B, TX, F), BF16), jax.ShapeDtypeStruct((B, TX, D), BF16),
                   jax.ShapeDtypeStruct((B, 3, D), F32),
                   jax.ShapeDtypeStruct((1, D), F32), jax.ShapeDtypeStruct((1, D), F32)],
        compiler_params=_params(("arbitrary", "arbitrary")),
    )(x1, tgt, mod345, g3, g4, w1, w2)


def _mix_out_bwd(yf, yb, kdf, kdb, r, v, gd, conv, x, mod2, r_k, gw2p, lnx_w, lnx_b, w_out, g2, e_w, dx1, cfg):
    B, TX, D, TT, W, CW, JC, HP, GP = cfg.B, cfg.TX, cfg.D, cfg.TT, cfg.W, cfg.CW, cfg.JC, cfg.HP, cfg.GP

    def body(yf_ref, yb_ref, kdf_ref, kdb_ref, r_ref, v_ref, gd_ref, cv_ref, x_ref, mod_ref,
             rk_ref, gw_ref, lw_ref, lb_ref, wo_ref, g_ref, e_ref, dx1_ref,
             dy_ref, dkb_ref, dr_ref, dv_ref, dgd_ref, dcv_ref, cat_ref, dmix_ref,
             dmod_ref, dg2_ref, drk_ref, dgw_ref, dlw_ref, dlb_ref):
        b, j = pl.program_id(0), pl.program_id(1)
        e = e_ref[...]
        y = _from_heads(yf_ref, cfg) + _from_heads(yb_ref, cfg)
        kbar = 0.5 * (_from_heads(kdf_ref, cfg) + _from_heads(kdb_ref, cfg))
        ro, vjp_ro = jax.vjp(lambda *a: _readout(*a, e), y, kbar, _from_heads(r_ref, cfg),
                             _from_heads(v_ref, cfg), gd_ref[0], rk_ref[...], gw_ref[...], lw_ref[...], lb_ref[...])
        catb = jnp.concatenate([ro, cv_ref[0]], axis=-1).astype(BF16)
        mix = jnp.dot(catb, wo_ref[...], preferred_element_type=F32)
        _, vjp_post = jax.vjp(_post_res, x_ref[0], mix, mod_ref[0], g_ref[...])
        _, dmix, dgate, dg2 = vjp_post(dx1_ref[0])
        dmixb = dmix.astype(BF16)
        dcat = _bdot_nt(dmixb, wo_ref[...])
        dy, dkb, dr, dv, dgd, drk, dgw, dlw, dlb = vjp_ro(dcat[:, :W])
        _to_heads(dy_ref, dy, cfg)
        _to_heads(dkb_ref, dkb, cfg)
        _to_heads(dr_ref, dr, cfg)
        _to_heads(dv_ref, dv, cfg)
        dgd_ref[0] = dgd
        dcv_ref[0] = dcat[:, W:]
        cat_ref[0] = catb
        dmix_ref[0] = dmixb
        _acc(dmod_ref, dgate[None], j == 0)
        first = jnp.logical_and(b == 0, j == 0)
        _acc(dg2_ref, dg2, first)
        _acc(drk_ref, drk, first)
        _acc(dgw_ref, dgw, first)
        _acc(dlw_ref, dlw, first)
        _acc(dlb_ref, dlb, first)

    hs = _head_spec(cfg, lambda j: j + JC)
    ho = _head_spec(cfg, lambda j: j)
    lat = lambda n: pl.BlockSpec((1, TT, n), lambda b, j: (b, j + JC, 0))
    tile = lambda n: pl.BlockSpec((1, TT, n), lambda b, j: (b, j, 0))
    head = jax.ShapeDtypeStruct((B, HP, TX, LANES), F32)
    vec = lambda n: jax.ShapeDtypeStruct((1, n), F32)
    return pl.pallas_call(
        body, name="mix_out_bwd", grid=(B, cfg.JX),
        in_specs=[hs] * 6 + [lat(GP), lat(CW), tile(D),
                             pl.BlockSpec((1, 1, D), lambda b, j: (b, 0, 0)),
                             _full_spec((1, W)), _full_spec((GP, W)), _full_spec((1, W)), _full_spec((1, W)),
                             _full_spec((W + CW, D)), _full_spec((1, D)), _full_spec(e_w.shape), tile(D)],
        out_specs=[ho] * 4 + [tile(GP), tile(CW), tile(W + CW), tile(D),
                              pl.BlockSpec((1, 1, D), lambda b, j: (b, 0, 0)),
                              _full_spec((1, D)), _full_spec((1, W)), _full_spec((GP, W)),
                              _full_spec((1, W)), _full_spec((1, W))],
        out_shape=[head] * 4 + [jax.ShapeDtypeStruct((B, TX, GP), F32), jax.ShapeDtypeStruct((B, TX, CW), F32),
                                jax.ShapeDtypeStruct((B, TX, W + CW), BF16), jax.ShapeDtypeStruct((B, TX, D), BF16),
                                jax.ShapeDtypeStruct((B, 1, D), F32),
                                vec(D), vec(W), jax.ShapeDtypeStruct((GP, W), F32), vec(W), vec(W)],
        compiler_params=_params(("arbitrary", "arbitrary")),
    )(yf, yb, kdf, kdb, r, v, gd, conv, x, mod2, r_k, gw2p, lnx_w, lnx_b, w_out, g2, e_w, dx1)


def _scan_bwd(ops_f, ops_b, dy, hist, e128, e256, cfg):
    G, T, NCH, NCC = cfg.G, cfg.T, cfg.NCH, cfg.NCC
    CH = SCAN_CHUNK
    G2 = 2 * G

    def body(*refs):
        ins = (refs[0:6], refs[6:12])
        dys, hist_ref, next_ref = (refs[12], refs[13]), refs[14], refs[15]
        ahead, hist_ahead = (refs[16:19], refs[19:22]), refs[22]
        e_ref, e2_ref = refs[23], refs[24]
        outs = (refs[25:31], refs[31:37])
        ds_ref, mm_ref = refs[37], refs[38]
        gi = pl.program_id(0)
        diag = _diag_mask()
        diag_b = diag.astype(BF16)
        e, e2 = e_ref[...], e2_ref[...]
        rows = functools.partial(_both_rows, ins)
        latent = [(_chunk_pos(NCH - 1 - gi, d == 1, cfg) >= NCC).astype(F32) for d in range(2)]
        latent_ahead = [(_chunk_pos(jnp.maximum(NCH - 2 - gi, 0), d == 1, cfg) >= NCC).astype(F32) for d in range(2)]

        def put(idx, i, val):
            outs[0][idx][:, pl.ds(_tok(i, False), 1), :] = val[:G]
            outs[1][idx][:, pl.ds(_tok(i, True), 1), :] = val[G:]

        rsum = lambda z: jnp.sum(z, axis=1, keepdims=True)

        def at(i):
            dy_rows = jnp.concatenate([dys[d][:, pl.ds(_tok(i, d == 1), 1), :] * latent[d] for d in range(2)], axis=0)
            return dy_rows, rows(3, i), rows(4, i), hist_ref[i]

        def at_ahead():
            edge = lambda d: 0 if d == 1 else SUBLANES - 1
            pick = lambda k, scale: jnp.concatenate(
                [ahead[d][k][:, edge(d):edge(d) + 1, :] * scale[d] for d in range(2)], axis=0)
            return pick(0, latent_ahead), pick(1, (1.0, 1.0)), pick(2, (1.0, 1.0)), hist_ahead[0]

        def prepare(dy_rows, v_rows, kk_rows, s_before):
            return (_col_form(dy_rows), _col_dot([v_rows], diag_b, e2)[0],
                    s_before.astype(BF16) * (-kk_rows).astype(BF16))

        @pl.when(gi == 0)
        def _():
            ds_ref[...] = jnp.zeros_like(ds_ref)
            dyc0, vb0, sa_lhs = prepare(*at(CH - 1))
            mm_ref[0] = dyc0
            mm_ref[1] = vb0
            mm_ref[2] = _seg_dot([sa_lhs], e)[0]

        def one_step(i, s_after, coming):
            sp, dyc = hist_ref[i], mm_ref[0]
            ds = ds_ref[...] + dyc * rows(0, i)
            put(0, i, rsum(s_after * dyc))
            put(1, i, rsum(ds * sp))
            put(5, i, rsum(ds * mm_ref[2]))
            put(2, i, rsum(ds * mm_ref[1]))
            dyc_n, vb_n, sa_lhs_n = prepare(*coming)
            dsb = ds.astype(BF16)
            res = _seg_dot([dsb * rows(5, i).astype(BF16), dsb * rows(2, i).astype(BF16), sa_lhs_n], e)
            dsa = res[0]
            put(4, i, -rsum(sp * dsa))
            put(3, i, rsum(diag * res[1]))
            mm_ref[0] = dyc_n
            mm_ref[1] = vb_n
            mm_ref[2] = res[2]
            ds_ref[...] = ds * rows(1, i) - dsa * rows(4, i)

        one_step(CH - 1, next_ref[0], at(CH - 2))

        def bstep(ii, carry):
            i = CH - 1 - ii
            one_step(i, hist_ref[i + 1], at(i - 1))
            return carry

        lax.fori_loop(1, CH - 1, bstep, 0)
        one_step(0, hist_ref[1], at_ahead())

    per = CH // SUBLANES
    cpos = lambda g, rev: _chunk_pos(NCH - 1 - g, rev, cfg)
    apos = lambda g, rev: _chunk_pos(jnp.maximum(NCH - 2 - g, 0), rev, cfg)
    toks = [pl.BlockSpec((G, CH, LANES), lambda g, rev=rev: (0, cpos(g, rev), 0)) for rev in (False, True)]
    dy_specs = [pl.BlockSpec((G, CH, LANES), lambda g, rev=rev: (0, jnp.maximum(cpos(g, rev) - NCC, 0), 0))
                for rev in (False, True)]
    edge_blk = lambda rev: 0 if rev else per - 1
    ahead_tok = [pl.BlockSpec((G, SUBLANES, LANES), lambda g, rev=rev: (0, apos(g, rev) * per + edge_blk(rev), 0))
                 for rev in (False, True)]
    ahead_dy = [pl.BlockSpec((G, SUBLANES, LANES),
                             lambda g, rev=rev: (0, jnp.maximum(apos(g, rev) - NCC, 0) * per + edge_blk(rev), 0))
                for rev in (False, True)]
    state_row = lambda f: pl.BlockSpec((1, G2, HEAD, LANES), lambda g: (f(g), 0, 0, 0))
    out = jax.ShapeDtypeStruct((G, T, LANES), F32)
    res = pl.pallas_call(
        body, name="scan_bwd", grid=(NCH,),
        in_specs=[toks[0]] * 6 + [toks[1]] * 6 + dy_specs
                 + [pl.BlockSpec((CH, G2, HEAD, LANES), lambda g: (NCH - 1 - g, 0, 0, 0)),
                    state_row(lambda g: (NCH - g) * CH)]
                 + [ahead_dy[0], ahead_tok[0], ahead_tok[0], ahead_dy[1], ahead_tok[1], ahead_tok[1],
                    state_row(lambda g: jnp.maximum((NCH - 1 - g) * CH - 1, 0)),
                    _full_spec((LANES, LANES)), _full_spec((2 * LANES, LANES))],
        out_specs=[toks[0]] * 6 + [toks[1]] * 6,
        out_shape=[out] * 12,
        scratch_shapes=[pltpu.VMEM((G2, HEAD, LANES), F32), pltpu.VMEM((3, G2, HEAD, LANES), F32)],
        compiler_params=_params(("arbitrary",)),
    )(*ops_f, *ops_b, dy, dy, hist, hist,
      dy, ops_f[3], ops_f[4], dy, ops_b[3], ops_b[4], hist, e128, e256)
    return res[:6], res[6:]


def _mix_prep_bwd(p, mup, mun, w0, w2p, a0, a2p, k_k, k_a, wm, cb, clw, clb, e_w, yconv, sf, sb, ro, xch, cfg):
    B, T, TT, SP, CP, W, CW, HP, JC, PW, GP, KC, KP = (cfg.B, cfg.T, cfg.TT, cfg.SP, cfg.CP, cfg.W, cfg.CW,
                                                       cfg.HP, cfg.JC, cfg.PW, cfg.GP, cfg.KC, cfg.KP)
    pad = KC // 2
    nx = len(xch)

    def body(p_ref, prev_ref, next_ref, mup_ref, mun_ref, w0_ref, w2_ref, a0_ref, a2_ref, kk_ref, ka_ref,
             wm_ref, cb_ref, clw_ref, clb_ref, e_ref, yc_ref, *rest):
        sf_refs, sb_refs = rest[0:6], rest[6:12]
        rdr_ref, rdv_ref, rdkb_ref, rdgd_ref, rdcv_ref = rest[12:17]
        xin, rest = rest[17:17 + nx], rest[17 + nx:]
        (dpz_ref, dmup_ref, dmun_ref, dw0_ref, dw2_ref, da0_ref, da2_ref, dkk_ref, dka_ref,
         dcw_ref, dcb_ref, dclw_ref, dclb_ref) = rest[:13]
        xout, (dyc_ref, du_ref), sems = rest[13:13 + nx], rest[13 + nx:15 + nx], rest[15 + nx:]
        b, j = pl.program_id(0), pl.program_id(1)
        first = jnp.logical_and(b == 0, j == 0)

        @pl.when(first)
        def _():
            _exchange_start(_exchange_copies(xin, xout, *sems, [True] * nx))
        lat = (j >= JC).astype(F32)
        e = e_ref[...]
        mup_v, mun_v = mup_ref[...], mun_ref[...]
        z, zp, zn, rw = _shifted(p_ref, prev_ref, next_ref, mup_v, mun_v, j, cfg)

        def prep(rw_, w0_, w2_, a0_, a2_, kk_, ka_):
            return _rwkv_prep(rw_, w0_, w2_, a0_, a2_, kk_, ka_, e, cfg) + (rw_[:, 3 * W + 2 * PW:SP],)

        _, vjp_prep = jax.vjp(prep, rw, w0_ref[...], w2_ref[...], a0_ref[...], a2_ref[...], kk_ref[...], ka_ref[...])
        fr, fw, fk, fv, fkk, fb = [_from_heads(r_, cfg) for r_ in sf_refs]
        br, bw, bk, bv, bkk, bb = [_from_heads(r_, cfg) for r_ in sb_refs]
        half_kb = (0.5 * lat) * _from_heads(rdkb_ref, cfg)
        cots = (fr + br + lat * _from_heads(rdr_ref, cfg), fv + bv + lat * _from_heads(rdv_ref, cfg), fkk + bkk,
                fw, fk + half_kb, fb, bw, bk + half_kb, bb, lat * rdgd_ref[0])
        drw, dw0, dw2, da0, da2, dkk, dka = vjp_prep(cots)
        _acc(dmup_ref, jnp.sum(drw * (zp - z), axis=0, keepdims=True), first)
        _acc(dmun_ref, jnp.sum(drw * (zn - z), axis=0, keepdims=True), first)
        for ref, val in ((dw0_ref, dw0), (dw2_ref, dw2), (da0_ref, da0), (da2_ref, da2), (dkk_ref, dkk), (dka_ref, dka)):
            _acc(ref, val, first)

        dpz_ref[0, :, 0:SP] = drw

        @pl.when(first)
        def _():
            for ref in (dcw_ref, dcb_ref, dclw_ref, dclb_ref):
                ref[...] = jnp.zeros_like(ref)

        @pl.when(j < JC)
        def _():
            dpz_ref[0, :, SP:] = jnp.zeros((TT, 2 * CW), F32)

        @pl.when(j >= JC)
        def _():
            u, vjp_glu = jax.vjp(lambda c_: _glu(c_, cfg), p_ref[0, :, SP:])
            _, vjp_post = jax.vjp(_conv_post, yc_ref[0], cb_ref[...], clw_ref[...], clb_ref[...])
            dyc, dcb, dclw, dclb = vjp_post(rdcv_ref[0])
            dyc_ref[...] = dyc
            _conv_lines(dyc_ref, wm_ref, du_ref, KC, True)
            (dcv,) = vjp_glu(du_ref[...])
            dpz_ref[0, :, SP:] = dcv
            for i in range(KC):
                dcw_ref[i:i + 1, :] += jnp.sum(dyc * _line_shift(u, i - pad), axis=0, keepdims=True)
            dcb_ref[...] += dcb
            dclw_ref[...] += dclw
            dclb_ref[...] += dclb

        @pl.when(jnp.logical_and(b == B - 1, j == cfg.J - 1))
        def _():
            _exchange_wait(_exchange_copies(xin, xout, *sems, [True] * nx))

    prev, nxt = _halo_specs(cfg, SP)
    hs = _head_spec(cfg, lambda j: j)
    hl = _head_spec(cfg, lambda j: jnp.maximum(j - JC, 0))
    latn = lambda n: pl.BlockSpec((1, TT, n), lambda b, j: (b, jnp.maximum(j - JC, 0), 0))
    hbm = pl.BlockSpec(memory_space=pltpu.HBM)
    vec = lambda n: jax.ShapeDtypeStruct((1, n), F32)
    small_shapes = [vec(SP), vec(SP), vec(2 * W), jax.ShapeDtypeStruct((PW, 2 * W), F32), vec(2 * W),
                    jax.ShapeDtypeStruct((PW, 2 * W), F32), vec(W), vec(W),
                    jax.ShapeDtypeStruct((KP, CW), F32), vec(CW), vec(CW), vec(CW)]
    return pl.pallas_call(
        body, name="mix_prep_bwd", grid=(B, cfg.J),
        in_specs=[pl.BlockSpec((1, TT, CP), lambda b, j: (b, j, 0)), prev, nxt,
                  _full_spec((1, SP)), _full_spec((1, SP)),
                  _full_spec((1, 2 * W)), _full_spec((PW, 2 * W)),
                  _full_spec((1, 2 * W)), _full_spec((PW, 2 * W)),
                  _full_spec((1, W)), _full_spec((1, W)),
                  pl.BlockSpec((KC, LINE, CW), lambda b, j: (0, 0, 0), pipeline_mode=pl.Buffered(1)),
                  _full_spec((1, CW)), _full_spec((1, CW)), _full_spec((1, CW)),
                  _full_spec(e_w.shape), pl.BlockSpec((1, TT, CW), lambda b, j: (b, j, 0))]
                 + [hs] * 12 + [hl] * 3 + [latn(GP), latn(CW)] + [hbm] * nx,
        out_specs=[pl.BlockSpec((1, TT, CP), lambda b, j: (b, j, 0))] + [_full_spec(s.shape) for s in small_shapes]
                  + [hbm] * nx,
        out_shape=[jax.ShapeDtypeStruct((B, T, CP), F32)] + small_shapes + _exchange_shapes(xch, [True] * nx),
        scratch_shapes=[pltpu.VMEM((TT, CW), F32), pltpu.VMEM((TT, CW), F32)] + _exchange_sems(nx),
        compiler_params=_params(("arbitrary", "arbitrary")),
    )(p, p, p, mup, mun, w0, w2p, a0, a2p, k_k, k_a, wm, cb, clw, clb, e_w, yconv, *sf, *sb, *ro, *xch)


def _in_proj_bwd(dpz, ctx, x, modt, g1, w_in_p, mup, mun, dx1, cfg):
    B, T, TX, D, TT, SP, CP, JC = cfg.B, cfg.T, cfg.TX, cfg.D, cfg.TT, cfg.SP, cfg.CP, cfg.JC

    def body(d_ref, prev_ref, next_ref, c_ref, x_ref, mod_ref, g_ref, w_ref, mup_ref, mun_ref, dx1_ref,
             gx_ref, dp_ref, dmod_ref, dg_ref):
        b, j = pl.program_id(0), pl.program_id(1)
        xv = jnp.where(j < JC, c_ref[0], x_ref[0])
        has_prev, has_next = _halo_flags(j, cfg)
        mp, mn = mup_ref[...], mun_ref[...]
        drw = d_ref[0, :, 0:SP]
        dprev, dnext = _shift_rows(drw, prev_ref[0, SUBLANES - 1:SUBLANES, :] * has_prev,
                                   next_ref[0, 0:1, :] * has_next)
        dz = drw * (1.0 - mp - mn) + mp * dnext + mn * dprev
        dpb = jnp.concatenate([dz, d_ref[0, :, SP:]], axis=-1).astype(BF16)
        dp_ref[0] = dpb
        dh = _bdot_nt(dpb, w_ref[...])
        _, vjp_h = jax.vjp(_rms_mod, xv, g_ref[...], mod_ref[0, 0, 0:1, :], mod_ref[0, 0, 1:2, :])
        dx, dg, dsh, dsc = vjp_h(dh)
        dmod_ref[0, 0] = jnp.concatenate([dsh, dsc], axis=0)
        _acc(dg_ref, dg, jnp.logical_and(b == 0, j == 0))

        @pl.when(j >= JC)
        def _():
            gx_ref[0] = dx + dx1_ref[0]

    prev, nxt = _halo_specs(cfg, SP)
    lat = pl.BlockSpec((1, TT, D), lambda b, j: (b, jnp.maximum(j - JC, 0), 0))
    return pl.pallas_call(
        body, name="in_proj_bwd", grid=(B, cfg.J),
        in_specs=[pl.BlockSpec((1, TT, CP), lambda b, j: (b, j, 0)), prev, nxt] + _token_specs(cfg)
                 + [pl.BlockSpec((1, 1, 2, D), lambda b, j: (b, j, 0, 0)),
                    _full_spec((1, D)), _full_spec((D, CP)), _full_spec((1, SP)), _full_spec((1, SP)), lat],
        out_specs=[lat, pl.BlockSpec((1, TT, CP), lambda b, j: (b, j, 0)),
                   pl.BlockSpec((1, 1, 2, D), lambda b, j: (b, j, 0, 0)), _full_spec((1, D))],
        out_shape=[jax.ShapeDtypeStruct((B, TX, D), F32), jax.ShapeDtypeStruct((B, T, CP), BF16),
                   jax.ShapeDtypeStruct((B, cfg.J, 2, D), F32), jax.ShapeDtypeStruct((1, D), F32)],
        compiler_params=_params(("arbitrary", "arbitrary")),
    )(dpz, dpz, dpz, ctx, x, modt, g1, w_in_p, mup, mun, dx1)


def _pick_tile(n, pref):
    for t in pref:
        if n % t == 0:
            return t
    return n


def _grad_matmul(a, g, name):
    K, M = a.shape
    N = g.shape[1]
    tm = _pick_tile(M, (512, 256, 128))
    tn = _pick_tile(N, (1024, 768, 512, 256, 128))
    tk = _pick_tile(K, (2048, 1024, 512, 256, 128, 64))
    nk = K // tk

    def body(a_ref, g_ref, o_ref):
        k = pl.program_id(2)
        _acc(o_ref, _bdot_tn(a_ref[...], g_ref[...]), k == 0)

    return pl.pallas_call(
        body, name=name, grid=(M // tm, N // tn, nk),
        in_specs=[pl.BlockSpec((tk, tm), lambda i, j, k: (k, i)),
                  pl.BlockSpec((tk, tn), lambda i, j, k: (k, j))],
        out_specs=pl.BlockSpec((tm, tn), lambda i, j, k: (i, j)),
        out_shape=jax.ShapeDtypeStruct((M, N), F32),
        compiler_params=_params(("parallel", "parallel", "arbitrary")),
    )(a, g)


def _ada_fwd(crows, ada_w, ada_b):
    D = crows.shape[1]
    n6 = ada_w.shape[1]
    tn = _pick_tile(n6, (1024, 512, 256, 128))

    def body(c_ref, w_ref, b_ref, s_ref, m_ref):
        s = _silu(c_ref[...])
        s_ref[...] = s
        m_ref[...] = _bdot(s, w_ref[...]) + b_ref[...]

    return pl.pallas_call(
        body, name="ada_fwd", grid=(n6 // tn,),
        in_specs=[_full_spec((SUBLANES, D)), pl.BlockSpec((D, tn), lambda i: (0, i)),
                  pl.BlockSpec((1, tn), lambda i: (0, i))],
        out_specs=[_full_spec((SUBLANES, D)), pl.BlockSpec((SUBLANES, tn), lambda i: (0, i))],
        out_shape=[jax.ShapeDtypeStruct((SUBLANES, D), F32), jax.ShapeDtypeStruct((SUBLANES, n6), F32)],
        compiler_params=_params(("arbitrary",)),
    )(crows, ada_w, ada_b)


def _ada_bwd(s_all, g_all, g_mine, c_ctx, ada_w, nb):
    D = s_all.shape[1]
    n6 = g_all.shape[1]
    ns = g_mine.shape[1]

    def body(s_ref, g_ref, gm_ref, c_ref, w_ref, dw_ref, db_ref, dc_ref):
        g = g_ref[...]
        dw_ref[...] = _bdot_tn(s_ref[...], gm_ref[...])
        db_ref[...] = jnp.sum(g, axis=0, keepdims=True)
        rows = lax.broadcasted_iota(jnp.int32, (g.shape[0], 1), 0)
        gc = jnp.sum(jnp.where(rows % SUBLANES == nb, g, 0.0), axis=0, keepdims=True)
        ds = _bdot_nt(gc, w_ref[...])
        c = c_ref[...]
        sg = _sigmoid(c)
        dc_ref[...] = ds * (sg + c * sg * (1.0 - sg))

    return pl.pallas_call(
        body, name="ada_bwd",
        out_shape=[jax.ShapeDtypeStruct((D, ns), F32), jax.ShapeDtypeStruct((1, n6), F32),
                   jax.ShapeDtypeStruct((1, D), F32)],
        compiler_params=_params(),
    )(s_all, g_all, g_mine, c_ctx, ada_w)


def _adamw(parts, w, m, v, name):
    P, R, C = parts.shape
    small = R * C * (P + 7) * 4 <= 4 * 1024 * 1024
    tr = R if small else _pick_tile(R, (256, 128, 64, 32, 16, 8))

    def body(p_ref, w_ref, m_ref, v_ref, g_ref, d_ref, nm_ref, nv_ref):
        g = p_ref[0].astype(F32)
        for i in range(1, P):
            g = g + p_ref[i].astype(F32)
        nm = ADAM_B1 * m_ref[...] + (1.0 - ADAM_B1) * g
        nv = ADAM_B2 * v_ref[...] + (1.0 - ADAM_B2) * (g * g)
        m_hat = nm / (1.0 - ADAM_B1 ** ADAM_STEP)
        v_hat = nv / (1.0 - ADAM_B2 ** ADAM_STEP)
        g_ref[...] = g
        d_ref[...] = -ADAM_LR * (m_hat / (jnp.sqrt(v_hat) + ADAM_EPS) + ADAM_WD * w_ref[...])
        nm_ref[...] = nm
        nv_ref[...] = nv

    blk = pl.BlockSpec((tr, C), lambda i: (i, 0))
    out = jax.ShapeDtypeStruct((R, C), F32)
    return pl.pallas_call(
        body, name=name, grid=(R // tr,),
        in_specs=[pl.BlockSpec((P, tr, C), lambda i: (0, i, 0)), blk, blk, blk],
        out_specs=[blk] * 4, out_shape=[out] * 4,
        compiler_params=_params(("parallel",)),
    )(parts, w, m, v)


def _local_step(cfg, x, c, ctx, tgt, fw, late):
    B, D, W, CW, JC, T, TX = cfg.B, cfg.D, cfg.W, cfg.CW, cfg.JC, cfg.T, cfg.TX
    e_w = _block_ones(min(W, MXU_DIM))
    e128 = _block_ones(LANES)
    e256 = jnp.concatenate([e128, e128], axis=0)
    row = lambda a: a.reshape(1, -1)

    ada_wb = fw["ada_w"].astype(BF16)
    w_in_p = _pad_cols(fw["w_in"], cfg).astype(BF16)
    mup = _pad_cols(fw["mu_prev"], cfg, True)
    mun = _pad_cols(fw["mu_next"], cfg, True)
    w0, a0 = row(fw["decay_w0"]), row(fw["iclr_a0"])
    w2p, a2p = _pair_weight(fw["decay_w2"], cfg), _pair_weight(fw["iclr_a2"], cfg)
    wm_fwd, wm_bwd = _conv_tables(fw["conv_w"], cfg.KC)
    gw2p = jnp.pad(fw["gate_w2"], ((0, cfg.GP - cfg.GR), (0, 0)))
    r_k = row(fw["r_k"])

    crows = jnp.concatenate([c, fw["c_ctx"], jnp.zeros((SUBLANES - B - 1, D), F32)], axis=0)
    s_rows, mods = _ada_fwd(crows, ada_wb, fw["ada_b"])
    mod_x = mods[:B].reshape(B, 6, D)
    mod_c = mods[B].reshape(6, D)
    modt = jnp.concatenate([jnp.broadcast_to(mod_c[None, None, 0:2], (B, JC, 2, D)),
                            jnp.broadcast_to(mod_x[:, None, 0:2], (B, cfg.JX, 2, D))], axis=1)
    mod2, mod345 = mod_x[:, 2:3], mod_x[:, 3:6]

    p, hb = _in_proj(ctx, x, modt, fw["mix_pre_g"], w_in_p, cfg)
    prep_w = (mup, mun, w0, w2p, a0, a2p, fw["k_k"], fw["k_a"])
    conv_w = (fw["conv_b"], fw["conv_ln_w"], fw["conv_ln_b"], e_w)
    (r, v, kk, w_f, kd_f, b_f, w_b, kd_b, b_b, gd, conv, yconv, g_w_out, g_w1, g_w2) = _mix_prep(
        p, *prep_w, wm_fwd, *conv_w, late, cfg)
    w_outb, w1b, w2b = g_w_out.reshape(-1, D), _blocks_to_cols(g_w1), g_w2.reshape(-1, D)
    flat = lambda a: a.reshape(cfg.G, a.shape[2], LANES)
    heads = lambda a: a.reshape(B, cfg.HP, a.shape[1], LANES)
    ops_f = tuple(flat(a) for a in (r, w_f, kd_f, v, kk, b_f))
    ops_b = tuple(flat(a) for a in (r, w_b, kd_b, v, kk, b_b))
    y_f, y_b, hist, s_fin = _scan_fwd(ops_f, ops_b, e128, e256, cfg)
    hist = lax.dynamic_update_slice_in_dim(hist, s_fin[None], cfg.NCH * SCAN_CHUNK, axis=0)
    out_args = (heads(y_f), heads(y_b), kd_f, kd_b, r, v, gd, conv, x, mod2, r_k, gw2p, fw["lnx_w"], fw["lnx_b"],
                w_outb, fw["mix_post_g"], e_w)
    x1 = _mix_out(*out_args, cfg)

    dx1, loss_t, h2b, dpreb, actb, dffb, dmod345, dg3, dg4 = _mlp_fwd_bwd(
        x1, tgt, mod345, fw["mlp_pre_g"], fw["mlp_post_g"], w1b, w2b, cfg)
    (dy, dkb, dr_c, dv_c, dgd, dconv, catb, dmixb, dmod2, dg2, drk, dgw, dlw, dlb) = _mix_out_bwd(
        *out_args, dx1, cfg)
    sf, sb = _scan_bwd(ops_f, ops_b, flat(dy), hist, e128, e256, cfg)

    tokens = lambda a: a.reshape(-1, a.shape[-1])
    d_w_out = _grad_matmul(tokens(catb), tokens(dmixb), "grad_w_out")
    d_w1 = _grad_matmul(tokens(h2b), tokens(dpreb), "grad_mlp_w1")
    d_w2 = _grad_matmul(tokens(actb), tokens(dffb), "grad_mlp_w2")
    early = [d_w_out.astype(BF16).reshape(N_DEV, -1, D), _cols_to_blocks(d_w1.astype(BF16)),
             d_w2.astype(BF16).reshape(N_DEV, -1, D)]
    (dpz, dmup, dmun, dw0, dw2p, da0, da2p, dkk, dka, dcw, dcb, dclw, dclb, x_w_out, x_w1, x_w2) = _mix_prep_bwd(
        p, *prep_w, wm_bwd, *conv_w, yconv, [heads(a) for a in sf], [heads(a) for a in sb],
        (dr_c, dv_c, dkb, dgd, dconv), early, cfg)
    grad_x, dpb, dmodt, dg1 = _in_proj_bwd(dpz, ctx, x, modt, fw["mix_pre_g"], w_in_p, mup, mun, dx1, cfg)
    d_w_in = _grad_matmul(tokens(hb), tokens(dpb), "grad_w_in")
    exchanged = {"w_out": x_w_out, "mlp_w1": x_w1, "mlp_w2": x_w2}

    grads = {
        "mix_pre_g": dg1, "mix_post_g": dg2, "mlp_pre_g": dg3, "mlp_post_g": dg4,
        "w_in": _unpad_cols(d_w_in, cfg),
        "mu_prev": _unpad_cols(dmup, cfg, True), "mu_next": _unpad_cols(dmun, cfg, True),
        "decay_w0": dw0.reshape(2, W), "decay_w2": _unpair_weight(dw2p, cfg),
        "iclr_a0": da0.reshape(2, W), "iclr_a2": _unpair_weight(da2p, cfg),
        "k_k": dkk, "k_a": dka, "r_k": drk.reshape(fw["r_k"].shape),
        "gate_w2": dgw[:cfg.GR], "lnx_w": dlw, "lnx_b": dlb,
        "conv_w": dcw[:cfg.KC], "conv_b": dcb, "conv_ln_w": dclw, "conv_ln_b": dclb,
    }
    dmod_x = jnp.concatenate([jnp.sum(dmodt[:, JC:], axis=1), dmod2, dmod345], axis=1).reshape(B, 6 * D)
    dmod_c = jnp.concatenate([jnp.sum(dmodt[:, :JC], axis=(0, 1)), jnp.zeros((4, D), F32)], axis=0).reshape(1, 6 * D)
    g_rows = jnp.concatenate([dmod_x, dmod_c, jnp.zeros((SUBLANES - B - 1, 6 * D), F32)], axis=0)
    return loss_t, grad_x, grads, exchanged, s_rows, g_rows


def _my_index():
    return 4 * lax.axis_index("x") + 2 * lax.axis_index("y") + lax.axis_index("c")


def _gather_phase(phase, ins, outs, send_sems, recv_sems, local_sems):
    n = len(ins)
    x, y, c = lax.axis_index("x"), lax.axis_index("y"), lax.axis_index("c")
    index = lambda px, py, pc: 4 * px + 2 * py + pc
    me, sibling = (x, y, c), (x, y, 1 - c)
    chips = [(1 - x, y), (x, 1 - y), (1 - x, 1 - y)]

    def copy(a, k, block, to, src=None):
        dst = outs[a].at[index(*block)]
        return pltpu.make_async_remote_copy(
            src_ref=dst if src is None else src, dst_ref=dst,
            send_sem=send_sems.at[k, a], recv_sem=recv_sems.at[k, a],
            device_id=to, device_id_type=pl.DeviceIdType.MESH)

    def local():
        return [pltpu.make_async_copy(ins[a], outs[a].at[index(*me)], local_sems.at[a]) for a in range(n)]

    def first():
        return [cp for a in range(n) for cp in
                [copy(a, 0, me, sibling, src=ins[a])]
                + [copy(a, 1 + j, me, (*chip, c), src=ins[a]) for j, chip in enumerate(chips)]]

    def passed():
        return [copy(a, 4 + j, (*chip, c), sibling) for j, chip in enumerate(chips) for a in range(n)]

    if phase == 0:
        for cp in local() + first():
            cp.start()
    elif phase == 1:
        arrived = [copy(a, 1 + j, (*chip, c), me) for j, chip in enumerate(chips) for a in range(n)]
        for got, fwd in zip(arrived, passed()):
            got.wait_recv()
            fwd.start()
    else:
        for a in range(n):
            copy(a, 0, sibling, me).wait_recv()
            for j, chip in enumerate(chips):
                copy(a, 4 + j, (*chip, 1 - c), me).wait_recv()
        for cp in first() + passed():
            cp.wait_send()
        for cp in local():
            cp.wait()


def _gather_two_level(arrays, name):
    n = len(arrays)

    def body(*refs):
        for phase in range(3):
            _gather_phase(phase, refs[:n], refs[n:2 * n], *refs[2 * n:])

    hbm = pl.BlockSpec(memory_space=pltpu.HBM)
    return pl.pallas_call(
        body, name=name, out_shape=_exchange_shapes(arrays, [False] * n),
        in_specs=[hbm] * n, out_specs=[hbm] * n, scratch_shapes=_exchange_sems(n),
    )(*arrays)


def _exchange_copies(ins, outs, send_sems, recv_sems, local_sems, scatter):
    n = len(ins)
    x, y, c = lax.axis_index("x"), lax.axis_index("y"), lax.axis_index("c")
    me = 4 * x + 2 * y + c
    flip = lambda v, f: 1 - v if f else v

    def piece(a, dest):
        return ins[a].at[dest] if scatter[a] else ins[a]

    local = [pltpu.make_async_copy(piece(a, me), outs[a].at[me], local_sems.at[a]) for a in range(n)]
    sends, recvs = [], []
    for k in range(1, N_DEV):
        fx, fy, fc = (k >> 2) & 1, (k >> 1) & 1, k & 1
        peer = (flip(x, fx), flip(y, fy), flip(c, fc))
        peer_idx = 4 * peer[0] + 2 * peer[1] + peer[2]
        for a in range(n):
            sends.append(pltpu.make_async_remote_copy(
                src_ref=piece(a, peer_idx), dst_ref=outs[a].at[me],
                send_sem=send_sems.at[k - 1, a], recv_sem=recv_sems.at[k - 1, a],
                device_id=peer, device_id_type=pl.DeviceIdType.MESH))
            recvs.append(pltpu.make_async_remote_copy(
                src_ref=piece(a, peer_idx), dst_ref=outs[a].at[peer_idx],
                send_sem=send_sems.at[k - 1, a], recv_sem=recv_sems.at[k - 1, a],
                device_id=peer, device_id_type=pl.DeviceIdType.MESH))
    return local, sends, recvs


def _exchange_start(copies):
    local, sends, _ = copies
    for cp in local + sends:
        cp.start()


def _exchange_wait(copies):
    local, sends, recvs = copies
    for cp in recvs:
        cp.wait_recv()
    for cp in sends:
        cp.wait_send()
    for cp in local:
        cp.wait()


def _exchange_shapes(arrays, scatter):
    return [jax.ShapeDtypeStruct(a.shape if s else (N_DEV,) + a.shape, a.dtype) for a, s in zip(arrays, scatter)]


def _exchange_sems(n):
    return [pltpu.SemaphoreType.DMA((N_DEV - 1, n)), pltpu.SemaphoreType.DMA((N_DEV - 1, n)),
            pltpu.SemaphoreType.DMA((n,))]


def _exchange(arrays, scatter, name):
    n = len(arrays)

    def body(*refs):
        copies = _exchange_copies(refs[:n], refs[n:2 * n], *refs[2 * n:], scatter)
        _exchange_start(copies)
        _exchange_wait(copies)

    hbm = pl.BlockSpec(memory_space=pltpu.HBM)
    return pl.pallas_call(
        body, name=name, out_shape=_exchange_shapes(arrays, scatter),
        in_specs=[hbm] * n, out_specs=[hbm] * n, scratch_shapes=_exchange_sems(n),
    )(*arrays)


def _pack(parts):
    flat = jnp.concatenate([p.reshape(-1) for p in parts])
    total = _round_up(flat.shape[0], SUBLANES * LANES)
    return jnp.pad(flat, (0, total - flat.shape[0])).reshape(-1, LANES)


def _unpack(buf, shapes):
    flat = buf.reshape(-1)
    out, pos = [], 0
    for s in shapes:
        n = int(np.prod(s))
        out.append(flat[pos:pos + n].reshape(s))
        pos += n
    return out


_SHARDED_SMALL = ("decay_w0", "decay_w2", "iclr_a0", "iclr_a2", "gate_w2", "conv_w")
_REPLICATED = ("mix_pre_g", "mix_post_g", "mlp_pre_g", "mlp_post_g", "mu_prev", "mu_next", "k_k", "k_a", "r_k",
               "lnx_w", "lnx_b", "conv_b", "conv_ln_w", "conv_ln_b")
_ADA_SMALL = ("c_ctx", "ada_b")
_WEIGHTS = ("c_ctx", "ada_w", "ada_b", "mix_pre_g", "mix_post_g", "mlp_pre_g", "mlp_post_g", "w_in", "mu_prev",
            "mu_next", "decay_w0", "decay_w2", "iclr_a0", "iclr_a2", "k_k", "k_a", "r_k", "gate_w2", "lnx_w", "lnx_b",
            "conv_w", "conv_b", "conv_ln_w", "conv_ln_b", "w_out", "mlp_w1", "mlp_w2")
_INPUTS = ("x", "c", "ctx") + _WEIGHTS + ("loss_target",) + tuple("m_" + n for n in _WEIGHTS) + tuple(
    "v_" + n for n in _WEIGHTS)


def _cols_to_blocks(a):
    a = a.reshape(a.shape[:-1] + (N_DEV, a.shape[-1] // N_DEV))
    return jnp.moveaxis(a, -2, 0)


def _blocks_to_cols(a):
    a = jnp.moveaxis(a, 0, -2)
    return a.reshape(a.shape[:-2] + (a.shape[-2] * a.shape[-1],))


def kernel(x, c, ctx, c_ctx, ada_w, ada_b, mix_pre_g, mix_post_g, mlp_pre_g, mlp_post_g, w_in, mu_prev, mu_next, decay_w0, decay_w2, iclr_a0, iclr_a2, k_k, k_a, r_k, gate_w2, lnx_w, lnx_b, conv_w, conv_b, conv_ln_w, conv_ln_b, w_out, mlp_w1, mlp_w2, loss_target, m_c_ctx, m_ada_w, m_ada_b, m_mix_pre_g, m_mix_post_g, m_mlp_pre_g, m_mlp_post_g, m_w_in, m_mu_prev, m_mu_next, m_decay_w0, m_decay_w2, m_iclr_a0, m_iclr_a2, m_k_k, m_k_a, m_r_k, m_gate_w2, m_lnx_w, m_lnx_b, m_conv_w, m_conv_b, m_conv_ln_w, m_conv_ln_b, m_w_out, m_mlp_w1, m_mlp_w2, v_c_ctx, v_ada_w, v_ada_b, v_mix_pre_g, v_mix_post_g, v_mlp_pre_g, v_mlp_post_g, v_w_in, v_mu_prev, v_mu_next, v_decay_w0, v_decay_w2, v_iclr_a0, v_iclr_a2, v_k_k, v_k_a, v_r_k, v_gate_w2, v_lnx_w, v_lnx_b, v_conv_w, v_conv_b, v_conv_ln_w, v_conv_ln_b, v_w_out, v_mlp_w1, v_mlp_w2):
    given = dict(zip(_INPUTS, (x, c, ctx, c_ctx, ada_w, ada_b, mix_pre_g, mix_post_g, mlp_pre_g, mlp_post_g, w_in, mu_prev, mu_next, decay_w0, decay_w2, iclr_a0, iclr_a2, k_k, k_a, r_k, gate_w2, lnx_w, lnx_b, conv_w, conv_b, conv_ln_w, conv_ln_b, w_out, mlp_w1, mlp_w2, loss_target, m_c_ctx, m_ada_w, m_ada_b, m_mix_pre_g, m_mix_post_g, m_mlp_pre_g, m_mlp_post_g, m_w_in, m_mu_prev, m_mu_next, m_decay_w0, m_decay_w2, m_iclr_a0, m_iclr_a2, m_k_k, m_k_a, m_r_k, m_gate_w2, m_lnx_w, m_lnx_b, m_conv_w, m_conv_b, m_conv_ln_w, m_conv_ln_b, m_w_out, m_mlp_w1, m_mlp_w2, v_c_ctx, v_ada_w, v_ada_b, v_mix_pre_g, v_mix_post_g, v_mlp_pre_g, v_mlp_post_g, v_w_in, v_mu_prev, v_mu_next, v_decay_w0, v_decay_w2, v_iclr_a0, v_iclr_a2, v_k_k, v_k_a, v_r_k, v_gate_w2, v_lnx_w, v_lnx_b, v_conv_w, v_conv_b, v_conv_ln_w, v_conv_ln_b, v_w_out, v_mlp_w1, v_mlp_w2)))
    loc = {}
    for pre in ("", "m_", "v_"):
        for n in _WEIGHTS:
            a = given[pre + n]
            a = a.reshape(1, -1) if n == "c_ctx" else a[0]
            loc[pre + n] = a.reshape(1, -1) if a.ndim == 1 else a
    B, TX, D = x.shape
    W, CW = loc["k_k"].shape[1], loc["conv_b"].shape[1]
    cfg = _Cfg(B, TX, ctx.shape[1], D, W, CW, loc["decay_w2"].shape[1], loc["gate_w2"].shape[0],
               loc["conv_w"].shape[0], loc["mlp_w1"].shape[1] * N_DEV)
    me = _my_index()

    small_shapes = [loc[n].shape for n in _SHARDED_SMALL]
    got = _gather_two_level(
        [loc["ada_w"].astype(BF16), loc["w_in"].astype(BF16), _pack([loc[n] for n in _SHARDED_SMALL])],
        "gather_weights")
    fw = {n: loc[n] for n in _REPLICATED + _ADA_SMALL}
    fw["ada_w"] = _blocks_to_cols(got[0])
    fw["w_in"] = _blocks_to_cols(got[1])
    per_dev = [_unpack(got[2][i], small_shapes) for i in range(N_DEV)]
    for j, n in enumerate(_SHARDED_SMALL):
        fw[n] = jnp.concatenate([per_dev[i][j] for i in range(N_DEV)], axis=-1)
    late = [loc[n].astype(BF16) for n in ("w_out", "mlp_w1", "mlp_w2")]

    loss_t, grad_x, grads, exchanged, s_rows, g_rows = _local_step(cfg, x, c, ctx, loss_target, fw, late)
    loss = lax.psum(jnp.sum(loss_t[:, :, 0, 0]), ("x", "y", "c"))

    small_blocks = jnp.stack([_pack([_cols_to_blocks(grads[n])[i] for n in _SHARDED_SMALL]) for i in range(N_DEV)])
    sent = _exchange(
        [_cols_to_blocks(grads["w_in"].astype(BF16)), small_blocks,
         _pack([grads[n] for n in _REPLICATED]), s_rows, g_rows],
        [True] * 2 + [False] * 3, "exchange_grads")
    s_all = sent[3].reshape(N_DEV * SUBLANES, D)
    g_all = sent[4].reshape(N_DEV * SUBLANES, 6 * D)
    ns = 6 * D // N_DEV
    g_mine = lax.dynamic_slice_in_dim(g_all, me * ns, ns, axis=1)
    d_ada_w, d_ada_b, d_c_ctx = _ada_bwd(s_all, g_all, g_mine, loc["c_ctx"], fw["ada_w"], B)

    res = {}

    def update(name, parts):
        res[name] = _adamw(parts, loc[name], loc["m_" + name], loc["v_" + name], "adamw_" + name)

    update("w_in", sent[0])
    for n in ("w_out", "mlp_w1", "mlp_w2"):
        update(n, exchanged[n])
    update("ada_w", d_ada_w[None])

    def update_packed(names, parts, tag):
        shapes = [loc[n].shape for n in names]
        packed = _adamw(parts, *[_pack([loc[pre + n] for n in names]) for pre in ("", "m_", "v_")], "adamw_" + tag)
        unpacked = [_unpack(p, shapes) for p in packed]
        for j, n in enumerate(names):
            res[n] = tuple(u[j] for u in unpacked)

    update_packed(_SHARDED_SMALL, sent[1], "sharded_small")
    update_packed(_REPLICATED, sent[2], "replicated")
    update_packed(_ADA_SMALL, _pack([d_c_ctx, d_ada_b])[None], "ada_small")

    outs = [loss, grad_x]
    for k in range(4):
        for n in _WEIGHTS:
            outs.append(res[n][k].reshape(given[n].shape))
    return tuple(outs)
```

```python
import functools

import numpy as np
import jax
import jax.numpy as jnp
from jax import lax
from jax.experimental import pallas as pl
from jax.experimental.pallas import tpu as pltpu

F32 = jnp.float32
BF16 = jnp.bfloat16

EPS_RMS = 1e-6
EPS_LN = 1e-5
EPS_GN = 64e-5
LINE = 64
HEAD = 64
LANES = 128
SUBLANES = 8
MXU_DIM = 256
SCAN_CHUNK = 16
N_DEV = 8
VMEM_LIMIT = 56 * 1024 * 1024

ADAM_LR = 0.001
ADAM_B1 = 0.9
ADAM_B2 = 0.999
ADAM_EPS = 1e-08
ADAM_WD = 0.01
ADAM_STEP = 10


def _round_up(n, m):
    return (n + m - 1) // m * m


def _params(semantics=None, vmem=VMEM_LIMIT):
    return pltpu.CompilerParams(dimension_semantics=semantics, vmem_limit_bytes=vmem)


def _bdot(a, b):
    return jnp.dot(a.astype(BF16), b.astype(BF16), preferred_element_type=F32)


def _bdot_nt(a, b):
    return lax.dot_general(a.astype(BF16), b.astype(BF16), (((1,), (1,)), ((), ())),
                           preferred_element_type=F32)


def _bdot_tn(a, b):
    return lax.dot_general(a.astype(BF16), b.astype(BF16), (((0,), (0,)), ((), ())),
                           preferred_element_type=F32)


@jax.custom_vjp
def _mm(a, b):
    return _bdot(a, b)


def _mm_fwd(a, b):
    return _bdot(a, b), (a, b)


def _mm_bwd(res, g):
    a, b = res
    return _bdot_nt(g, b), _bdot_tn(a, g)


_mm.defvjp(_mm_fwd, _mm_bwd)


def _seg_sum_raw(x, e):
    hi = x.astype(BF16)
    lo = (x - hi.astype(F32)).astype(BF16)
    n = e.shape[0]
    parts = [jnp.dot(hi[:, s:s + n], e, preferred_element_type=F32)
             + jnp.dot(lo[:, s:s + n], e, preferred_element_type=F32) for s in range(0, x.shape[1], n)]
    return parts[0] if len(parts) == 1 else jnp.concatenate(parts, axis=1)


@jax.custom_vjp
def _seg_sum(x, e):
    return _seg_sum_raw(x, e)


def _seg_sum_fwd(x, e):
    return _seg_sum_raw(x, e), e


def _seg_sum_bwd(e, g):
    return _seg_sum_raw(g, e), None


_seg_sum.defvjp(_seg_sum_fwd, _seg_sum_bwd)


def _block_ones(n, seg=HEAD):
    i = np.arange(n) // seg
    return jnp.asarray((i[:, None] == i[None, :]).astype(np.float32), dtype=BF16)


def _rms(xv, g):
    ms = jnp.mean(xv * xv, axis=-1, keepdims=True)
    return xv * lax.rsqrt(ms + EPS_RMS) * g


def _rms_mod(xv, g, shift, scale):
    return _rms(xv, g) * (1.0 + scale) + shift


def _sigmoid(z):
    return 1.0 / (1.0 + jnp.exp(-z))


def _silu(z):
    return z * _sigmoid(z)


def _softplus(z):
    return jnp.maximum(z, 0.0) + jnp.log(1.0 + jnp.exp(-jnp.abs(z)))


class _Cfg:
    def __init__(self, B, TX, TC, D, W, CW, R, GR, KC, F):
        self.B, self.TX, self.TC, self.D = B, TX, TC, D
        self.W, self.CW, self.R, self.GR, self.KC, self.F = W, CW, R, GR, KC, F
        self.T = TX + TC
        self.TT = min(256, TC)
        assert TC % self.TT == 0 and TX % self.TT == 0 and self.TT % LINE == 0
        self.JC = TC // self.TT
        self.JX = TX // self.TT
        self.J = self.JC + self.JX
        self.HP = W // LANES
        self.G = B * self.HP
        self.PW = _round_up(2 * R, LANES)
        self.GP = _round_up(GR, LANES)
        self.SP = 3 * W + 2 * self.PW + self.GP
        self.CP = self.SP + 2 * CW
        self.KP = _round_up(KC, SUBLANES)
        assert self.T % SCAN_CHUNK == 0 and TC % SCAN_CHUNK == 0
        self.NCH = self.T // SCAN_CHUNK
        self.NCC = TC // SCAN_CHUNK
        W_, R_ = W, R
        segs = [(0, 3 * W_, 0),
                (3 * W_, 2 * R_, 3 * W_),
                (3 * W_ + 2 * R_, 2 * R_, 3 * W_ + self.PW),
                (3 * W_ + 4 * R_, GR, 3 * W_ + 2 * self.PW),
                (3 * W_ + 4 * R_ + GR, 2 * CW, self.SP)]
        self.col_segs = segs
        self.shift_cols = 3 * W_ + 4 * R_ + GR
        self.in_cols = self.shift_cols + 2 * CW


def _pad_cols(a, cfg, upto_shift=False):
    width = cfg.SP if upto_shift else cfg.CP
    pieces, pos = [], 0
    for src, n, dst in cfg.col_segs:
        if upto_shift and dst >= cfg.SP:
            break
        if dst > pos:
            pieces.append(jnp.zeros(a.shape[:-1] + (dst - pos,), a.dtype))
        pieces.append(a[..., src:src + n])
        pos = dst + n
    if width > pos:
        pieces.append(jnp.zeros(a.shape[:-1] + (width - pos,), a.dtype))
    return jnp.concatenate(pieces, axis=-1)


def _unpad_cols(a, cfg, upto_shift=False):
    pieces = []
    for src, n, dst in cfg.col_segs:
        if upto_shift and dst >= cfg.SP:
            break
        pieces.append(a[..., dst:dst + n])
    return jnp.concatenate(pieces, axis=-1)


def _pair_weight(w2, cfg):
    R, W = cfg.R, cfg.W
    out = jnp.zeros((cfg.PW, 2 * W), w2.dtype)
    out = out.at[0:R, 0:W].set(w2[0])
    out = out.at[R:2 * R, W:2 * W].set(w2[1])
    return out


def _unpair_weight(g, cfg):
    R, W = cfg.R, cfg.W
    return jnp.stack([g[0:R, 0:W], g[R:2 * R, W:2 * W]])


def _row_ids(n):
    return lax.broadcasted_iota(jnp.int32, (n, 1), 0)


def _shift_rows(z, prev_row, next_row):
    n = z.shape[0]
    rows = _row_ids(n)
    zp = jnp.where(rows == 0, prev_row, pltpu.roll(z, 1, 0))
    zn = jnp.where(rows == n - 1, next_row, pltpu.roll(z, n - 1, 0))
    return zp, zn


def _line_shift(u, d):
    if d == 0:
        return u
    n = u.shape[0]
    lt = _row_ids(n) % LINE
    ok = jnp.logical_and(lt + d >= 0, lt + d < LINE)
    return jnp.where(ok, pltpu.roll(u, (-d) % n, 0), 0.0)


def _conv_tables(cw, kc):
    pad = kc // 2
    t = np.arange(LINE)[None, :]
    d = (np.arange(kc) - pad)[:, None]
    fwd = ((t + d >= 0) & (t + d < LINE)).astype(np.float32)
    bwd = ((t - d >= 0) & (t - d < LINE)).astype(np.float32)
    w = cw[:kc, None, :]
    return jnp.asarray(fwd)[:, :, None] * w, jnp.asarray(bwd)[:, :, None] * w


def _conv_lines(src_ref, wm_ref, dst_ref, kc, transpose):
    pad = kc // 2
    n, width = src_ref.shape
    wide = 2 * LANES if width % (2 * LANES) == 0 else LANES
    for l in range(n // LINE):
        for b in range(width // wide):
            rs, cs = slice(l * LINE, (l + 1) * LINE), slice(b * wide, (b + 1) * wide)
            tile = src_ref[rs, cs]
            acc = jnp.zeros_like(tile)
            for i in range(kc):
                d = (pad - i) if transpose else (i - pad)
                acc = acc + pltpu.roll(tile, (-d) % LINE, 0) * wm_ref[i, :, cs]
            dst_ref[rs, cs] = acc


def _rwkv_prep(rw, w0, w2p, a0, a2p, k_k, k_a, e, cfg):
    W, PW = cfg.W, cfg.PW
    r = rw[:, 0:W]
    k = rw[:, W:2 * W]
    v = rw[:, 2 * W:3 * W]
    wdp = rw[:, 3 * W:3 * W + PW]
    adp = rw[:, 3 * W + PW:3 * W + 2 * PW]
    wl = w0 + _mm(jnp.tanh(wdp), w2p)
    w_log = -_softplus(-wl) - 0.5
    decay = jnp.exp(-jnp.exp(w_log))
    iclr = _sigmoid(a0 + _mm(adp, a2p))
    kkr = k * k_k
    nrm = jnp.sqrt(_seg_sum(kkr * kkr, e))
    kk = kkr / jnp.maximum(nrm, 1e-12)
    outs = [r, v, kk]
    for d in range(2):
        ic = iclr[:, d * W:(d + 1) * W]
        outs += [decay[:, d * W:(d + 1) * W], k * (1.0 + (ic - 1.0) * k_a), kk * ic]
    return tuple(outs)


def _glu(cv, cfg):
    return cv[:, :cfg.CW] * _sigmoid(cv[:, cfg.CW:])


def _conv_post(y, cb, lw, lb):
    yf = y + cb
    mu = jnp.mean(yf, axis=-1, keepdims=True)
    var = jnp.mean(jnp.square(yf - mu), axis=-1, keepdims=True)
    return _silu((yf - mu) * lax.rsqrt(var + EPS_LN) * lw + lb)


def _readout(y, kbar, r, v, gd, r_k, gw2, lnx_w, lnx_b, e):
    inv = 1.0 / HEAD
    mu = _seg_sum(y, e) * inv
    yc = y - mu
    var = _seg_sum(yc * yc, e) * inv
    yn = yc * lax.rsqrt(var + EPS_GN) * lnx_w + lnx_b
    bonus = _seg_sum(r * kbar * r_k, e) * v
    g = _mm(_sigmoid(gd), gw2)
    return (yn + bonus) * g


def _post_res(xv, mix, gate, g):
    return xv + gate * _rms(mix, g)


def _head_spec(cfg, tmap):
    return pl.BlockSpec((1, cfg.HP, cfg.TT, LANES), lambda b, j: (b, 0, tmap(j), 0))


def _full_spec(shape):
    n = len(shape)
    return pl.BlockSpec(shape, lambda *_: (0,) * n)


def _to_heads(ref, val, cfg):
    for hp in range(cfg.HP):
        ref[0, hp] = val[:, hp * LANES:(hp + 1) * LANES]


def _from_heads(ref, cfg):
    return jnp.concatenate([ref[0, hp] for hp in range(cfg.HP)], axis=-1)


def _token_specs(cfg):
    TT, D, JC = cfg.TT, cfg.D, cfg.JC
    return [pl.BlockSpec((1, TT, D), lambda b, j: (b, jnp.minimum(j, JC - 1), 0)),
            pl.BlockSpec((1, TT, D), lambda b, j: (b, jnp.maximum(j - JC, 0), 0))]


def _in_proj(ctx, x, modt, g1, w_in_p, cfg):
    B, T, D, TT, CP, JC = cfg.B, cfg.T, cfg.D, cfg.TT, cfg.CP, cfg.JC

    def body(c_ref, x_ref, mod_ref, g_ref, w_ref, p_ref, h_ref):
        xv = jnp.where(pl.program_id(1) < JC, c_ref[0], x_ref[0])
        h = _rms_mod(xv, g_ref[...], mod_ref[0, 0, 0:1, :], mod_ref[0, 0, 1:2, :])
        hb = h.astype(BF16)
        h_ref[0] = hb
        p_ref[0] = jnp.dot(hb, w_ref[...], preferred_element_type=F32)

    return pl.pallas_call(
        body, name="in_proj", grid=(B, cfg.J),
        in_specs=_token_specs(cfg) + [pl.BlockSpec((1, 1, 2, D), lambda b, j: (b, j, 0, 0)),
                                      _full_spec((1, D)), _full_spec((D, CP))],
        out_specs=[pl.BlockSpec((1, TT, CP), lambda b, j: (b, j, 0)),
                   pl.BlockSpec((1, TT, D), lambda b, j: (b, j, 0))],
        out_shape=[jax.ShapeDtypeStruct((B, T, CP), F32), jax.ShapeDtypeStruct((B, T, D), BF16)],
        compiler_params=_params(("arbitrary", "arbitrary")),
    )(ctx, x, modt, g1, w_in_p)


def _halo_specs(cfg, width):
    per = cfg.TT // SUBLANES
    last = cfg.T // SUBLANES - 1
    prev = pl.BlockSpec((1, SUBLANES, width), lambda b, j: (b, jnp.maximum(j * per - 1, 0), 0))
    nxt = pl.BlockSpec((1, SUBLANES, width), lambda b, j: (b, jnp.minimum((j + 1) * per, last), 0))
    return prev, nxt


def _halo_flags(j, cfg):
    has_prev = jnp.logical_and(j != 0, j != cfg.JC).astype(F32)
    has_next = jnp.logical_and(j != cfg.JC - 1, j != cfg.J - 1).astype(F32)
    return has_prev, has_next


def _shifted(p_ref, prev_ref, next_ref, mup, mun, j, cfg):
    SP = cfg.SP
    has_prev, has_next = _halo_flags(j, cfg)
    z = p_ref[0][:, :SP]
    zp, zn = _shift_rows(z, prev_ref[0, SUBLANES - 1:SUBLANES, :] * has_prev, next_ref[0, 0:1, :] * has_next)
    return z, zp, zn, z + mup * (zp - z) + mun * (zn - z)


def _mix_prep(p, mup, mun, w0, w2p, a0, a2p, k_k, k_a, wm, cb, clw, clb, e_w, late, cfg):
    B, T, TT, SP, CP, W, CW, HP, JC = cfg.B, cfg.T, cfg.TT, cfg.SP, cfg.CP, cfg.W, cfg.CW, cfg.HP, cfg.JC
    nl = len(late)
    steps = B * cfg.J

    def body(p_ref, prev_ref, next_ref, mup_ref, mun_ref, w0_ref, w2_ref, a0_ref, a2_ref, kk_ref, ka_ref,
             wm_ref, cb_ref, clw_ref, clb_ref, e_ref, *rest):
        late_in, rest = rest[:nl], rest[nl:]
        outs, late_out, u_ref, sems = rest[:12], rest[12:12 + nl], rest[12 + nl], rest[13 + nl:]
        j = pl.program_id(1)
        step = pl.program_id(0) * cfg.J + j
        for phase, at in enumerate((0, steps // 2)):
            @pl.when(step == at)
            def _(phase=phase):
                _gather_phase(phase, late_in, late_out, *sems)
        _, _, _, rw = _shifted(p_ref, prev_ref, next_ref, mup_ref[...], mun_ref[...], j, cfg)
        vals = _rwkv_prep(rw, w0_ref[...], w2_ref[...], a0_ref[...], a2_ref[...], kk_ref[...], ka_ref[...],
                          e_ref[...], cfg)
        for ref, val in zip(outs[:9], vals):
            _to_heads(ref, val, cfg)
        outs[9][0] = rw[:, 3 * W + 2 * cfg.PW:SP]

        @pl.when(j >= JC)
        def _():
            u_ref[...] = _glu(p_ref[0, :, SP:], cfg)
            _conv_lines(u_ref, wm_ref, outs[11].at[0], cfg.KC, False)
            outs[10][0] = _conv_post(outs[11][0], cb_ref[...], clw_ref[...], clb_ref[...])

        @pl.when(step == steps - 1)
        def _():
            _gather_phase(2, late_in, late_out, *sems)

    prev, nxt = _halo_specs(cfg, SP)
    head = jax.ShapeDtypeStruct((B, HP, T, LANES), F32)
    hbm = pl.BlockSpec(memory_space=pltpu.HBM)
    tile = lambda n: pl.BlockSpec((1, TT, n), lambda b, j: (b, j, 0))
    return pl.pallas_call(
        body, name="mix_prep", grid=(B, cfg.J),
        in_specs=[tile(CP), prev, nxt,
                  _full_spec((1, SP)), _full_spec((1, SP)),
                  _full_spec((1, 2 * W)), _full_spec((cfg.PW, 2 * W)),
                  _full_spec((1, 2 * W)), _full_spec((cfg.PW, 2 * W)),
                  _full_spec((1, W)), _full_spec((1, W)),
                  pl.BlockSpec((cfg.KC, LINE, CW), lambda b, j: (0, 0, 0), pipeline_mode=pl.Buffered(1)),
                  _full_spec((1, CW)), _full_spec((1, CW)), _full_spec((1, CW)),
                  _full_spec(e_w.shape)] + [hbm] * nl,
        out_specs=[_head_spec(cfg, lambda j: j)] * 9 + [tile(cfg.GP), tile(CW), tile(CW)] + [hbm] * nl,
        out_shape=[head] * 9 + [jax.ShapeDtypeStruct((B, T, cfg.GP), F32),
                                jax.ShapeDtypeStruct((B, T, CW), F32), jax.ShapeDtypeStruct((B, T, CW), F32)]
                  + _exchange_shapes(late, [False] * nl),
        scratch_shapes=[pltpu.VMEM((TT, CW), F32)] + _exchange_sems(nl),
        compiler_params=_params(("arbitrary", "arbitrary")),
    )(p, p, p, mup, mun, w0, w2p, a0, a2p, k_k, k_a, wm, cb, clw, clb, e_w, *late)


def _chunk_pos(c, reverse, cfg):
    if not reverse:
        return c
    return jnp.where(c < cfg.NCC, cfg.NCC - 1 - c, cfg.NCH - 1 + cfg.NCC - c)


def _diag_mask():
    r = lax.broadcasted_iota(jnp.int32, (HEAD, LANES), 0)
    l = lax.broadcasted_iota(jnp.int32, (HEAD, LANES), 1)
    return (r == l % HEAD).astype(F32)


def _col_lhs(row, diag_b):
    hi = row.astype(BF16)
    lo = (row - hi.astype(F32)).astype(BF16)
    return diag_b * hi, diag_b * lo


def _col_dot(row_list, diag_b, e2):
    n, g = len(row_list), row_list[0].shape[0]
    lhs = jnp.concatenate([jnp.concatenate(_col_lhs(r, diag_b), axis=-1) for r in row_list], axis=0)
    out = jnp.dot(lhs.reshape(n * g * HEAD, 2 * LANES), e2, preferred_element_type=F32)
    return out.reshape(n, g, HEAD, LANES)


def _col_form(row):
    n = row.shape[0]
    t = jnp.swapaxes(jnp.broadcast_to(row, (n, LANES, LANES)), 1, 2)
    lane = lax.broadcasted_iota(jnp.int32, (HEAD, LANES), 1)
    return jnp.where(lane < HEAD, t[:, :HEAD, :], t[:, HEAD:, :])


def _col_both(row, diag_b, e2):
    half = row.shape[0] // 2
    return jnp.concatenate([_col_form(row[:half]), _col_dot([row[half:]], diag_b, e2)[0]], axis=0)


def _both_rows(ins, idx, i):
    return jnp.concatenate([ins[d][idx][:, pl.ds(_tok(i, d == 1), 1), :] for d in range(2)], axis=0)


def _seg_dot(blocks, e):
    n, g = len(blocks), blocks[0].shape[0]
    lhs = jnp.concatenate(blocks, axis=0).reshape(n * g * HEAD, LANES)
    return jnp.dot(lhs, e, preferred_element_type=F32).reshape(n, g, HEAD, LANES)


def _tok(i, reverse):
    return (SCAN_CHUNK - 1 - i) if reverse else i


def _scan_fwd(ops_f, ops_b, e128, e256, cfg):
    G, T, NCH = cfg.G, cfg.T, cfg.NCH
    CH = SCAN_CHUNK
    G2 = 2 * G

    def body(*refs):
        ins = (refs[0:6], refs[6:12])
        ahead = (refs[12:14], refs[14:16])
        e_ref, e2_ref = refs[16], refs[17]
        ys, hist_ref, fin_ref = (refs[18], refs[19]), refs[20], refs[21]
        s_ref, mm_ref = refs[22], refs[23]
        c = pl.program_id(0)
        diag = _diag_mask()
        diag_b = diag.astype(BF16)
        e, e2 = e_ref[...], e2_ref[...]

        @pl.when(c == 0)
        def _():
            s_ref[...] = jnp.zeros_like(s_ref)
            mm_ref[0] = jnp.zeros_like(s_ref)
            mm_ref[1] = _col_both(_both_rows(ins, 3, 0), diag_b, e2)

        def step(i, nxt_row):
            res = []
            for d in range(2):
                sl = slice(d * G, (d + 1) * G)
                row = lambda idx: ins[d][idx][:, pl.ds(_tok(i, d == 1), 1), :]
                s_old = s_ref[sl]
                hist_ref[i, sl] = s_old
                S = s_old * row(1) + mm_ref[0, sl] * row(5) + mm_ref[1, sl] * row(2)
                s_ref[sl] = S
                sb = S.astype(BF16)
                res.append(_seg_dot([sb * (-nxt_row(d, 4)).astype(BF16), sb * row(0).astype(BF16)], e))
            for d in range(2):
                sl = slice(d * G, (d + 1) * G)
                v_next = nxt_row(d, 3)
                mm_ref[0, sl] = res[d][0]
                mm_ref[1, sl] = _col_form(v_next) if d == 0 else _col_dot([v_next], diag_b, e2)[0]
                ys[d][:, pl.ds(_tok(i, d == 1), 1), :] = jnp.sum(diag * res[d][1], axis=1, keepdims=True)

        def inner(i, carry):
            step(i, lambda d, idx: ins[d][idx][:, pl.ds(_tok(i + 1, d == 1), 1), :])
            return carry

        lax.fori_loop(0, CH - 1, inner, 0)
        edge = lambda d: (SUBLANES - 1) if d == 1 else 0
        step(CH - 1, lambda d, idx: ahead[d][0 if idx == 4 else 1][:, edge(d):edge(d) + 1, :])
        fin_ref[...] = s_ref[...]

    toks = [pl.BlockSpec((G, CH, LANES), lambda c, rev=rev: (0, _chunk_pos(c, rev, cfg), 0)) for rev in (False, True)]
    per = CH // SUBLANES

    def ahead_block(c, rev):
        pos = _chunk_pos(jnp.minimum(c + 1, NCH - 1), rev, cfg)
        return pos * per + (per - 1 if rev else 0)

    ahead = [pl.BlockSpec((G, SUBLANES, LANES), lambda c, rev=rev: (0, ahead_block(c, rev), 0)) for rev in (False, True)]
    y_shape = jax.ShapeDtypeStruct((G, T, LANES), F32)
    return pl.pallas_call(
        body, name="scan_fwd", grid=(NCH,),
        in_specs=[toks[0]] * 6 + [toks[1]] * 6 + [ahead[0]] * 2 + [ahead[1]] * 2
                 + [_full_spec((LANES, LANES)), _full_spec((2 * LANES, LANES))],
        out_specs=[toks[0], toks[1], pl.BlockSpec((CH, G2, HEAD, LANES), lambda c: (c, 0, 0, 0)),
                   _full_spec((G2, HEAD, LANES))],
        out_shape=[y_shape, y_shape, jax.ShapeDtypeStruct(((NCH + 1) * CH, G2, HEAD, LANES), F32),
                   jax.ShapeDtypeStruct((G2, HEAD, LANES), F32)],
        scratch_shapes=[pltpu.VMEM((G2, HEAD, LANES), F32), pltpu.VMEM((2, G2, HEAD, LANES), F32)],
        compiler_params=_params(("arbitrary",)),
    )(*ops_f, *ops_b, ops_f[4], ops_f[3], ops_b[4], ops_b[3], e128, e256)


def _mix_out(yf, yb, kdf, kdb, r, v, gd, conv, x, mod2, r_k, gw2p, lnx_w, lnx_b, w_out, g2, e_w, cfg):
    B, TX, D, TT, W, CW, JC = cfg.B, cfg.TX, cfg.D, cfg.TT, cfg.W, cfg.CW, cfg.JC

    def body(yf_ref, yb_ref, kdf_ref, kdb_ref, r_ref, v_ref, gd_ref, cv_ref, x_ref, mod_ref,
             rk_ref, gw_ref, lw_ref, lb_ref, wo_ref, g_ref, e_ref, x1_ref):
        y = _from_heads(yf_ref, cfg) + _from_heads(yb_ref, cfg)
        kbar = 0.5 * (_from_heads(kdf_ref, cfg) + _from_heads(kdb_ref, cfg))
        ro = _readout(y, kbar, _from_heads(r_ref, cfg), _from_heads(v_ref, cfg), gd_ref[0], rk_ref[...],
                      gw_ref[...], lw_ref[...], lb_ref[...], e_ref[...])
        cat = jnp.concatenate([ro, cv_ref[0]], axis=-1)
        mix = _bdot(cat, wo_ref[...])
        x1_ref[0] = _post_res(x_ref[0], mix, mod_ref[0], g_ref[...])

    hs = _head_spec(cfg, lambda j: j + JC)
    lat = lambda n: pl.BlockSpec((1, TT, n), lambda b, j: (b, j + JC, 0))
    return pl.pallas_call(
        body, name="mix_out", grid=(B, cfg.JX),
        in_specs=[hs] * 6 + [lat(cfg.GP), lat(CW),
                             pl.BlockSpec((1, TT, D), lambda b, j: (b, j, 0)),
                             pl.BlockSpec((1, 1, D), lambda b, j: (b, 0, 0)),
                             _full_spec((1, W)), _full_spec((cfg.GP, W)), _full_spec((1, W)), _full_spec((1, W)),
                             _full_spec((W + CW, D)), _full_spec((1, D)), _full_spec(e_w.shape)],
        out_specs=pl.BlockSpec((1, TT, D), lambda b, j: (b, j, 0)),
        out_shape=jax.ShapeDtypeStruct((B, TX, D), F32),
        compiler_params=_params(("parallel", "parallel")),
    )(yf, yb, kdf, kdb, r, v, gd, conv, x, mod2, r_k, gw2p, lnx_w, lnx_b, w_out, g2, e_w)


def _acc(ref, val, first):
    @pl.when(first)
    def _():
        ref[...] = val

    @pl.when(jnp.logical_not(first))
    def _():
        ref[...] += val


def _mlp_fwd_bwd(x1, tgt, mod345, g3, g4, w1, w2, cfg):
    B, TX, D, TT, F, JX = cfg.B, cfg.TX, cfg.D, cfg.TT, cfg.F, cfg.JX

    def body(x1_ref, t_ref, mod_ref, g3_ref, g4_ref, w1_ref, w2_ref,
             dx1_ref, loss_ref, h2_ref, dpre_ref, act_ref, dff_ref, dmod_ref, dg3_ref, dg4_ref):
        b, j = pl.program_id(0), pl.program_id(1)
        x1v = x1_ref[0]
        sh, sc, gt = mod_ref[0, 0:1, :], mod_ref[0, 1:2, :], mod_ref[0, 2:3, :]
        h2, vjp_pre = jax.vjp(_rms_mod, x1v, g3_ref[...], sh, sc)
        h2b = h2.astype(BF16)
        pre = jnp.dot(h2b, w1_ref[...], preferred_element_type=F32)
        rl = jnp.maximum(pre, 0.0)
        actb = (rl * rl).astype(BF16)
        ff = jnp.dot(actb, w2_ref[...], preferred_element_type=F32)
        x2, vjp_post = jax.vjp(_post_res, x1v, ff, gt, g4_ref[...])
        err = x2 - t_ref[0]
        loss = 0.5 * jnp.sum(jnp.mean(err * err, axis=-1, keepdims=True))
        dx1a, dff, dgt, dg4 = vjp_post(err * (1.0 / D))
        dffb = dff.astype(BF16)
        dpre = _bdot_nt(dffb, w2_ref[...]) * (2.0 * rl)
        dpreb = dpre.astype(BF16)
        dx1b, dg3, dsh, dsc = vjp_pre(_bdot_nt(dpreb, w1_ref[...]))
        dx1_ref[0] = dx1a + dx1b
        loss_ref[0, 0] = jnp.zeros((SUBLANES, LANES), F32) + loss
        h2_ref[0] = h2b
        dpre_ref[0] = dpreb
        act_ref[0] = actb
        dff_ref[0] = dffb
        _acc(dmod_ref, jnp.concatenate([dsh, dsc, dgt], axis=0)[None], j == 0)
        first = jnp.logical_and(b == 0, j == 0)
        _acc(dg3_ref, dg3, first)
        _acc(dg4_ref, dg4, first)

    tile = lambda n: pl.BlockSpec((1, TT, n), lambda b, j: (b, j, 0))
    return pl.pallas_call(
        body, name="mlp_fwd_bwd", grid=(B, JX),
        in_specs=[tile(D), tile(D), pl.BlockSpec((1, 3, D), lambda b, j: (b, 0, 0)),
                  _full_spec((1, D)), _full_spec((1, D)),
                  pl.BlockSpec((D, F), lambda b, j: (0, 0), pipeline_mode=pl.Buffered(1)),
                  pl.BlockSpec((F, D), lambda b, j: (0, 0), pipeline_mode=pl.Buffered(1))],
        out_specs=[tile(D), pl.BlockSpec((1, 1, SUBLANES, LANES), lambda b, j: (b, j, 0, 0)),
                   tile(D), tile(F), tile(F), tile(D),
                   pl.BlockSpec((1, 3, D), lambda b, j: (b, 0, 0)),
                   _full_spec((1, D)), _full_spec((1, D))],
        out_shape=[jax.ShapeDtypeStruct((B, TX, D), F32),
                   jax.ShapeDtypeStruct((B, JX, SUBLANES, LANES), F32),
                   jax.ShapeDtypeStruct((B, TX, D), BF16), jax.ShapeDtypeStruct((B, TX, F), BF16),
                   jax.ShapeDtypeStruct((B, TX, F), BF16), jax.ShapeDtypeStruct((B, TX, D), BF16),
                   jax.ShapeDtypeStruct((B, 3, D), F32),
                   jax.ShapeDtypeStruct((1, D), F32), jax.ShapeDtypeStruct((1, D), F32)],
        compiler_params=_params(("arbitrary", "arbitrary")),
    )(x1, tgt, mod345, g3, g4, w1, w2)


def _mix_out_bwd(yf, yb, kdf, kdb, r, v, gd, conv, x, mod2, r_k, gw2p, lnx_w, lnx_b, w_out, g2, e_w, dx1, cfg):
    B, TX, D, TT, W, CW, JC, HP, GP = cfg.B, cfg.TX, cfg.D, cfg.TT, cfg.W, cfg.CW, cfg.JC, cfg.HP, cfg.GP

    def body(yf_ref, yb_ref, kdf_ref, kdb_ref, r_ref, v_ref, gd_ref, cv_ref, x_ref, mod_ref,
             rk_ref, gw_ref, lw_ref, lb_ref, wo_ref, g_ref, e_ref, dx1_ref,
             dy_ref, dkb_ref, dr_ref, dv_ref, dgd_ref, dcv_ref, cat_ref, dmix_ref,
             dmod_ref, dg2_ref, drk_ref, dgw_ref, dlw_ref, dlb_ref):
        b, j = pl.program_id(0), pl.program_id(1)
        e = e_ref[...]
        y = _from_heads(yf_ref, cfg) + _from_heads(yb_ref, cfg)
        kbar = 0.5 * (_from_heads(kdf_ref, cfg) + _from_heads(kdb_ref, cfg))
        ro, vjp_ro = jax.vjp(lambda *a: _readout(*a, e), y, kbar, _from_heads(r_ref, cfg),
                             _from_heads(v_ref, cfg), gd_ref[0], rk_ref[...], gw_ref[...], lw_ref[...], lb_ref[...])
        catb = jnp.concatenate([ro, cv_ref[0]], axis=-1).astype(BF16)
        mix = jnp.dot(catb, wo_ref[...], preferred_element_type=F32)
        _, vjp_post = jax.vjp(_post_res, x_ref[0], mix, mod_ref[0], g_ref[...])
        _, dmix, dgate, dg2 = vjp_post(dx1_ref[0])
        dmixb = dmix.astype(BF16)
        dcat = _bdot_nt(dmixb, wo_ref[...])
        dy, dkb, dr, dv, dgd, drk, dgw, dlw, dlb = vjp_ro(dcat[:, :W])
        _to_heads(dy_ref, dy, cfg)
        _to_heads(dkb_ref, dkb, cfg)
        _to_heads(dr_ref, dr, cfg)
        _to_heads(dv_ref, dv, cfg)
        dgd_ref[0] = dgd
        dcv_ref[0] = dcat[:, W:]
        cat_ref[0] = catb
        dmix_ref[0] = dmixb
        _acc(dmod_ref, dgate[None], j == 0)
        first = jnp.logical_and(b == 0, j == 0)
        _acc(dg2_ref, dg2, first)
        _acc(drk_ref, drk, first)
        _acc(dgw_ref, dgw, first)
        _acc(dlw_ref, dlw, first)
        _acc(dlb_ref, dlb, first)

    hs = _head_spec(cfg, lambda j: j + JC)
    ho = _head_spec(cfg, lambda j: j)
    lat = lambda n: pl.BlockSpec((1, TT, n), lambda b, j: (b, j + JC, 0))
    tile = lambda n: pl.BlockSpec((1, TT, n), lambda b, j: (b, j, 0))
    head = jax.ShapeDtypeStruct((B, HP, TX, LANES), F32)
    vec = lambda n: jax.ShapeDtypeStruct((1, n), F32)
    return pl.pallas_call(
        body, name="mix_out_bwd", grid=(B, cfg.JX),
        in_specs=[hs] * 6 + [lat(GP), lat(CW), tile(D),
                             pl.BlockSpec((1, 1, D), lambda b, j: (b, 0, 0)),
                             _full_spec((1, W)), _full_spec((GP, W)), _full_spec((1, W)), _full_spec((1, W)),
                             _full_spec((W + CW, D)), _full_spec((1, D)), _full_spec(e_w.shape), tile(D)],
        out_specs=[ho] * 4 + [tile(GP), tile(CW), tile(W + CW), tile(D),
                              pl.BlockSpec((1, 1, D), lambda b, j: (b, 0, 0)),
                              _full_spec((1, D)), _full_spec((1, W)), _full_spec((GP, W)),
                              _full_spec((1, W)), _full_spec((1, W))],
        out_shape=[head] * 4 + [jax.ShapeDtypeStruct((B, TX, GP), F32), jax.ShapeDtypeStruct((B, TX, CW), F32),
                                jax.ShapeDtypeStruct((B, TX, W + CW), BF16), jax.ShapeDtypeStruct((B, TX, D), BF16),
                                jax.ShapeDtypeStruct((B, 1, D), F32),
                                vec(D), vec(W), jax.ShapeDtypeStruct((GP, W), F32), vec(W), vec(W)],
        compiler_params=_params(("arbitrary", "arbitrary")),
    )(yf, yb, kdf, kdb, r, v, gd, conv, x, mod2, r_k, gw2p, lnx_w, lnx_b, w_out, g2, e_w, dx1)


def _scan_bwd(ops_f, ops_b, dy, hist, e128, e256, cfg):
    G, T, NCH, NCC = cfg.G, cfg.T, cfg.NCH, cfg.NCC
    CH = SCAN_CHUNK
    G2 = 2 * G

    def body(*refs):
        ins = (refs[0:6], refs[6:12])
        dys, hist_ref, next_ref = (refs[12], refs[13]), refs[14], refs[15]
        ahead, hist_ahead = (refs[16:19], refs[19:22]), refs[22]
        e_ref, e2_ref = refs[23], refs[24]
        outs = (refs[25:31], refs[31:37])
        ds_ref, mm_ref = refs[37], refs[38]
        gi = pl.program_id(0)
        diag = _diag_mask()
        diag_b = diag.astype(BF16)
        e, e2 = e_ref[...], e2_ref[...]
        rows = functools.partial(_both_rows, ins)
        latent = [(_chunk_pos(NCH - 1 - gi, d == 1, cfg) >= NCC).astype(F32) for d in range(2)]
        latent_ahead = [(_chunk_pos(jnp.maximum(NCH - 2 - gi, 0), d == 1, cfg) >= NCC).astype(F32) for d in range(2)]

        def put(idx, i, val):
            outs[0][idx][:, pl.ds(_tok(i, False), 1), :] = val[:G]
            outs[1][idx][:, pl.ds(_tok(i, True), 1), :] = val[G:]

        rsum = lambda z: jnp.sum(z, axis=1, keepdims=True)

        def at(i):
            dy_rows = jnp.concatenate([dys[d][:, pl.ds(_tok(i, d == 1), 1), :] * latent[d] for d in range(2)], axis=0)
            return dy_rows, rows(3, i), rows(4, i), hist_ref[i]

        def at_ahead():
            edge = lambda d: 0 if d == 1 else SUBLANES - 1
            pick = lambda k, scale: jnp.concatenate(
                [ahead[d][k][:, edge(d):edge(d) + 1, :] * scale[d] for d in range(2)], axis=0)
            return pick(0, latent_ahead), pick(1, (1.0, 1.0)), pick(2, (1.0, 1.0)), hist_ahead[0]

        def prepare(dy_rows, v_rows, kk_rows, s_before):
            return (_col_form(dy_rows), _col_dot([v_rows], diag_b, e2)[0],
                    s_before.astype(BF16) * (-kk_rows).astype(BF16))

        @pl.when(gi == 0)
        def _():
            ds_ref[...] = jnp.zeros_like(ds_ref)
            dyc0, vb0, sa_lhs = prepare(*at(CH - 1))
            mm_ref[0] = dyc0
            mm_ref[1] = vb0
            mm_ref[2] = _seg_dot([sa_lhs], e)[0]

        def one_step(i, s_after, coming):
            sp, dyc = hist_ref[i], mm_ref[0]
            ds = ds_ref[...] + dyc * rows(0, i)
            put(0, i, rsum(s_after * dyc))
            put(1, i, rsum(ds * sp))
            put(5, i, rsum(ds * mm_ref[2]))
            put(2, i, rsum(ds * mm_ref[1]))
            dyc_n, vb_n, sa_lhs_n = prepare(*coming)
            dsb = ds.astype(BF16)
            res = _seg_dot([dsb * rows(5, i).astype(BF16), dsb * rows(2, i).astype(BF16), sa_lhs_n], e)
            dsa = res[0]
            put(4, i, -rsum(sp * dsa))
            put(3, i, rsum(diag * res[1]))
            mm_ref[0] = dyc_n
            mm_ref[1] = vb_n
            mm_ref[2] = res[2]
            ds_ref[...] = ds * rows(1, i) - dsa * rows(4, i)

        one_step(CH - 1, next_ref[0], at(CH - 2))

        def bstep(ii, carry):
            i = CH - 1 - ii
            one_step(i, hist_ref[i + 1], at(i - 1))
            return carry

        lax.fori_loop(1, CH - 1, bstep, 0)
        one_step(0, hist_ref[1], at_ahead())

    per = CH // SUBLANES
    cpos = lambda g, rev: _chunk_pos(NCH - 1 - g, rev, cfg)
    apos = lambda g, rev: _chunk_pos(jnp.maximum(NCH - 2 - g, 0), rev, cfg)
    toks = [pl.BlockSpec((G, CH, LANES), lambda g, rev=rev: (0, cpos(g, rev), 0)) for rev in (False, True)]
    dy_specs = [pl.BlockSpec((G, CH, LANES), lambda g, rev=rev: (0, jnp.maximum(cpos(g, rev) - NCC, 0), 0))
                for rev in (False, True)]
    edge_blk = lambda rev: 0 if rev else per - 1
    ahead_tok = [pl.BlockSpec((G, SUBLANES, LANES), lambda g, rev=rev: (0, apos(g, rev) * per + edge_blk(rev), 0))
                 for rev in (False, True)]
    ahead_dy = [pl.BlockSpec((G, SUBLANES, LANES),
                             lambda g, rev=rev: (0, jnp.maximum(apos(g, rev) - NCC, 0) * per + edge_blk(rev), 0))
                for rev in (False, True)]
    state_row = lambda f: pl.BlockSpec((1, G2, HEAD, LANES), lambda g: (f(g), 0, 0, 0))
    out = jax.ShapeDtypeStruct((G, T, LANES), F32)
    res = pl.pallas_call(
        body, name="scan_bwd", grid=(NCH,),
        in_specs=[toks[0]] * 6 + [toks[1]] * 6 + dy_specs
                 + [pl.BlockSpec((CH, G2, HEAD, LANES), lambda g: (NCH - 1 - g, 0, 0, 0)),
                    state_row(lambda g: (NCH - g) * CH)]
                 + [ahead_dy[0], ahead_tok[0], ahead_tok[0], ahead_dy[1], ahead_tok[1], ahead_tok[1],
                    state_row(lambda g: jnp.maximum((NCH - 1 - g) * CH - 1, 0)),
                    _full_spec((LANES, LANES)), _full_spec((2 * LANES, LANES))],
        out_specs=[toks[0]] * 6 + [toks[1]] * 6,
        out_shape=[out] * 12,
        scratch_shapes=[pltpu.VMEM((G2, HEAD, LANES), F32), pltpu.VMEM((3, G2, HEAD, LANES), F32)],
        compiler_params=_params(("arbitrary",)),
    )(*ops_f, *ops_b, dy, dy, hist, hist,
      dy, ops_f[3], ops_f[4], dy, ops_b[3], ops_b[4], hist, e128, e256)
    return res[:6], res[6:]


def _mix_prep_bwd(p, mup, mun, w0, w2p, a0, a2p, k_k, k_a, wm, cb, clw, clb, e_w, yconv, sf, sb, ro, xch, cfg):
    B, T, TT, SP, CP, W, CW, HP, JC, PW, GP, KC, KP = (cfg.B, cfg.T, cfg.TT, cfg.SP, cfg.CP, cfg.W, cfg.CW,
                                                       cfg.HP, cfg.JC, cfg.PW, cfg.GP, cfg.KC, cfg.KP)
    pad = KC // 2
    nx = len(xch)

    def body(p_ref, prev_ref, next_ref, mup_ref, mun_ref, w0_ref, w2_ref, a0_ref, a2_ref, kk_ref, ka_ref,
             wm_ref, cb_ref, clw_ref, clb_ref, e_ref, yc_ref, *rest):
        sf_refs, sb_refs = rest[0:6], rest[6:12]
        rdr_ref, rdv_ref, rdkb_ref, rdgd_ref, rdcv_ref = rest[12:17]
        xin, rest = rest[17:17 + nx], rest[17 + nx:]
        (dpz_ref, dmup_ref, dmun_ref, dw0_ref, dw2_ref, da0_ref, da2_ref, dkk_ref, dka_ref,
         dcw_ref, dcb_ref, dclw_ref, dclb_ref) = rest[:13]
        xout, (dyc_ref, du_ref), sems = rest[13:13 + nx], rest[13 + nx:15 + nx], rest[15 + nx:]
        b, j = pl.program_id(0), pl.program_id(1)
        first = jnp.logical_and(b == 0, j == 0)

        @pl.when(first)
        def _():
            _exchange_start(_exchange_copies(xin, xout, *sems, [True] * nx))
        lat = (j >= JC).astype(F32)
        e = e_ref[...]
        mup_v, mun_v = mup_ref[...], mun_ref[...]
        z, zp, zn, rw = _shifted(p_ref, prev_ref, next_ref, mup_v, mun_v, j, cfg)

        def prep(rw_, w0_, w2_, a0_, a2_, kk_, ka_):
            return _rwkv_prep(rw_, w0_, w2_, a0_, a2_, kk_, ka_, e, cfg) + (rw_[:, 3 * W + 2 * PW:SP],)

        _, vjp_prep = jax.vjp(prep, rw, w0_ref[...], w2_ref[...], a0_ref[...], a2_ref[...], kk_ref[...], ka_ref[...])
        fr, fw, fk, fv, fkk, fb = [_from_heads(r_, cfg) for r_ in sf_refs]
        br, bw, bk, bv, bkk, bb = [_from_heads(r_, cfg) for r_ in sb_refs]
        half_kb = (0.5 * lat) * _from_heads(rdkb_ref, cfg)
        cots = (fr + br + lat * _from_heads(rdr_ref, cfg), fv + bv + lat * _from_heads(rdv_ref, cfg), fkk + bkk,
                fw, fk + half_kb, fb, bw, bk + half_kb, bb, lat * rdgd_ref[0])
        drw, dw0, dw2, da0, da2, dkk, dka = vjp_prep(cots)
        _acc(dmup_ref, jnp.sum(drw * (zp - z), axis=0, keepdims=True), first)
        _acc(dmun_ref, jnp.sum(drw * (zn - z), axis=0, keepdims=True), first)
        for ref, val in ((dw0_ref, dw0), (dw2_ref, dw2), (da0_ref, da0), (da2_ref, da2), (dkk_ref, dkk), (dka_ref, dka)):
            _acc(ref, val, first)

        dpz_ref[0, :, 0:SP] = drw

        @pl.when(first)
        def _():
            for ref in (dcw_ref, dcb_ref, dclw_ref, dclb_ref):
                ref[...] = jnp.zeros_like(ref)

        @pl.when(j < JC)
        def _():
            dpz_ref[0, :, SP:] = jnp.zeros((TT, 2 * CW), F32)

        @pl.when(j >= JC)
        def _():
            u, vjp_glu = jax.vjp(lambda c_: _glu(c_, cfg), p_ref[0, :, SP:])
            _, vjp_post = jax.vjp(_conv_post, yc_ref[0], cb_ref[...], clw_ref[...], clb_ref[...])
            dyc, dcb, dclw, dclb = vjp_post(rdcv_ref[0])
            dyc_ref[...] = dyc
            _conv_lines(dyc_ref, wm_ref, du_ref, KC, True)
            (dcv,) = vjp_glu(du_ref[...])
            dpz_ref[0, :, SP:] = dcv
            du_ref[...] = u
            lt = _row_ids(LINE)
            wide = 2 * LANES if CW % (2 * LANES) == 0 else LANES
            for l in range(TT // LINE):
                for g in range(CW // wide):
                    rs, cs = slice(l * LINE, (l + 1) * LINE), slice(g * wide, (g + 1) * wide)
                    u_t, dy_t = du_ref[rs, cs], dyc_ref[rs, cs]
                    for i in range(KC):
                        d = i - pad
                        ok = jnp.logical_and(lt + d >= 0, lt + d < LINE)
                        prod = jnp.where(ok, pltpu.roll(u_t, (-d) % LINE, 0), 0.0) * dy_t
                        dcw_ref[i:i + 1, cs] += jnp.sum(prod, axis=0, keepdims=True)
            dcb_ref[...] += dcb
            dclw_ref[...] += dclw
            dclb_ref[...] += dclb

        @pl.when(jnp.logical_and(b == B - 1, j == cfg.J - 1))
        def _():
            _exchange_wait(_exchange_copies(xin, xout, *sems, [True] * nx))

    prev, nxt = _halo_specs(cfg, SP)
    hs = _head_spec(cfg, lambda j: j)
    hl = _head_spec(cfg, lambda j: jnp.maximum(j - JC, 0))
    latn = lambda n: pl.BlockSpec((1, TT, n), lambda b, j: (b, jnp.maximum(j - JC, 0), 0))
    hbm = pl.BlockSpec(memory_space=pltpu.HBM)
    vec = lambda n: jax.ShapeDtypeStruct((1, n), F32)
    small_shapes = [vec(SP), vec(SP), vec(2 * W), jax.ShapeDtypeStruct((PW, 2 * W), F32), vec(2 * W),
                    jax.ShapeDtypeStruct((PW, 2 * W), F32), vec(W), vec(W),
                    jax.ShapeDtypeStruct((KP, CW), F32), vec(CW), vec(CW), vec(CW)]
    return pl.pallas_call(
        body, name="mix_prep_bwd", grid=(B, cfg.J),
        in_specs=[pl.BlockSpec((1, TT, CP), lambda b, j: (b, j, 0)), prev, nxt,
                  _full_spec((1, SP)), _full_spec((1, SP)),
                  _full_spec((1, 2 * W)), _full_spec((PW, 2 * W)),
                  _full_spec((1, 2 * W)), _full_spec((PW, 2 * W)),
                  _full_spec((1, W)), _full_spec((1, W)),
                  pl.BlockSpec((KC, LINE, CW), lambda b, j: (0, 0, 0), pipeline_mode=pl.Buffered(1)),
                  _full_spec((1, CW)), _full_spec((1, CW)), _full_spec((1, CW)),
                  _full_spec(e_w.shape), pl.BlockSpec((1, TT, CW), lambda b, j: (b, j, 0))]
                 + [hs] * 12 + [hl] * 3 + [latn(GP), latn(CW)] + [hbm] * nx,
        out_specs=[pl.BlockSpec((1, TT, CP), lambda b, j: (b, j, 0))] + [_full_spec(s.shape) for s in small_shapes]
                  + [hbm] * nx,
        out_shape=[jax.ShapeDtypeStruct((B, T, CP), F32)] + small_shapes + _exchange_shapes(xch, [True] * nx),
        scratch_shapes=[pltpu.VMEM((TT, CW), F32), pltpu.VMEM((TT, CW), F32)] + _exchange_sems(nx),
        compiler_params=_params(("arbitrary", "arbitrary")),
    )(p, p, p, mup, mun, w0, w2p, a0, a2p, k_k, k_a, wm, cb, clw, clb, e_w, yconv, *sf, *sb, *ro, *xch)


def _in_proj_bwd(dpz, ctx, x, modt, g1, w_in_p, mup, mun, dx1, cfg):
    B, T, TX, D, TT, SP, CP, JC = cfg.B, cfg.T, cfg.TX, cfg.D, cfg.TT, cfg.SP, cfg.CP, cfg.JC

    def body(d_ref, prev_ref, next_ref, c_ref, x_ref, mod_ref, g_ref, w_ref, mup_ref, mun_ref, dx1_ref,
             gx_ref, dp_ref, dmod_ref, dg_ref):
        b, j = pl.program_id(0), pl.program_id(1)
        xv = jnp.where(j < JC, c_ref[0], x_ref[0])
        has_prev, has_next = _halo_flags(j, cfg)
        mp, mn = mup_ref[...], mun_ref[...]
        drw = d_ref[0, :, 0:SP]
        dprev, dnext = _shift_rows(drw, prev_ref[0, SUBLANES - 1:SUBLANES, :] * has_prev,
                                   next_ref[0, 0:1, :] * has_next)
        dz = drw * (1.0 - mp - mn) + mp * dnext + mn * dprev
        dpb = jnp.concatenate([dz, d_ref[0, :, SP:]], axis=-1).astype(BF16)
        dp_ref[0] = dpb
        dh = _bdot_nt(dpb, w_ref[...])
        _, vjp_h = jax.vjp(_rms_mod, xv, g_ref[...], mod_ref[0, 0, 0:1, :], mod_ref[0, 0, 1:2, :])
        dx, dg, dsh, dsc = vjp_h(dh)
        dmod_ref[0, 0] = jnp.concatenate([dsh, dsc], axis=0)
        _acc(dg_ref, dg, jnp.logical_and(b == 0, j == 0))

        @pl.when(j >= JC)
        def _():
            gx_ref[0] = dx + dx1_ref[0]

    prev, nxt = _halo_specs(cfg, SP)
    lat = pl.BlockSpec((1, TT, D), lambda b, j: (b, jnp.maximum(j - JC, 0), 0))
    return pl.pallas_call(
        body, name="in_proj_bwd", grid=(B, cfg.J),
        in_specs=[pl.BlockSpec((1, TT, CP), lambda b, j: (b, j, 0)), prev, nxt] + _token_specs(cfg)
                 + [pl.BlockSpec((1, 1, 2, D), lambda b, j: (b, j, 0, 0)),
                    _full_spec((1, D)), _full_spec((D, CP)), _full_spec((1, SP)), _full_spec((1, SP)), lat],
        out_specs=[lat, pl.BlockSpec((1, TT, CP), lambda b, j: (b, j, 0)),
                   pl.BlockSpec((1, 1, 2, D), lambda b, j: (b, j, 0, 0)), _full_spec((1, D))],
        out_shape=[jax.ShapeDtypeStruct((B, TX, D), F32), jax.ShapeDtypeStruct((B, T, CP), BF16),
                   jax.ShapeDtypeStruct((B, cfg.J, 2, D), F32), jax.ShapeDtypeStruct((1, D), F32)],
        compiler_params=_params(("arbitrary", "arbitrary")),
    )(dpz, dpz, dpz, ctx, x, modt, g1, w_in_p, mup, mun, dx1)


def _pick_tile(n, pref):
    for t in pref:
        if n % t == 0:
            return t
    return n


def _grad_matmul(a, g, name):
    K, M = a.shape
    N = g.shape[1]
    tm = _pick_tile(M, (512, 256, 128))
    tn = _pick_tile(N, (1024, 768, 512, 256, 128))
    tk = _pick_tile(K, (2048, 1024, 512, 256, 128, 64))
    nk = K // tk

    def body(a_ref, g_ref, o_ref):
        k = pl.program_id(2)
        _acc(o_ref, _bdot_tn(a_ref[...], g_ref[...]), k == 0)

    return pl.pallas_call(
        body, name=name, grid=(M // tm, N // tn, nk),
        in_specs=[pl.BlockSpec((tk, tm), lambda i, j, k: (k, i)),
                  pl.BlockSpec((tk, tn), lambda i, j, k: (k, j))],
        out_specs=pl.BlockSpec((tm, tn), lambda i, j, k: (i, j)),
        out_shape=jax.ShapeDtypeStruct((M, N), F32),
        compiler_params=_params(("parallel", "parallel", "arbitrary")),
    )(a, g)


def _ada_fwd(crows, ada_w, ada_b):
    D = crows.shape[1]
    n6 = ada_w.shape[1]
    tn = _pick_tile(n6, (1024, 512, 256, 128))

    def body(c_ref, w_ref, b_ref, s_ref, m_ref):
        s = _silu(c_ref[...])
        s_ref[...] = s
        m_ref[...] = _bdot(s, w_ref[...]) + b_ref[...]

    return pl.pallas_call(
        body, name="ada_fwd", grid=(n6 // tn,),
        in_specs=[_full_spec((SUBLANES, D)), pl.BlockSpec((D, tn), lambda i: (0, i)),
                  pl.BlockSpec((1, tn), lambda i: (0, i))],
        out_specs=[_full_spec((SUBLANES, D)), pl.BlockSpec((SUBLANES, tn), lambda i: (0, i))],
        out_shape=[jax.ShapeDtypeStruct((SUBLANES, D), F32), jax.ShapeDtypeStruct((SUBLANES, n6), F32)],
        compiler_params=_params(("arbitrary",)),
    )(crows, ada_w, ada_b)


def _ada_bwd(s_all, g_all, g_mine, c_ctx, ada_w, nb):
    D = s_all.shape[1]
    n6 = g_all.shape[1]
    ns = g_mine.shape[1]

    def body(s_ref, g_ref, gm_ref, c_ref, w_ref, dw_ref, db_ref, dc_ref):
        g = g_ref[...]
        dw_ref[...] = _bdot_tn(s_ref[...], gm_ref[...])
        db_ref[...] = jnp.sum(g, axis=0, keepdims=True)
        rows = lax.broadcasted_iota(jnp.int32, (g.shape[0], 1), 0)
        gc = jnp.sum(jnp.where(rows % SUBLANES == nb, g, 0.0), axis=0, keepdims=True)
        ds = _bdot_nt(gc, w_ref[...])
        c = c_ref[...]
        sg = _sigmoid(c)
        dc_ref[...] = ds * (sg + c * sg * (1.0 - sg))

    return pl.pallas_call(
        body, name="ada_bwd",
        out_shape=[jax.ShapeDtypeStruct((D, ns), F32), jax.ShapeDtypeStruct((1, n6), F32),
                   jax.ShapeDtypeStruct((1, D), F32)],
        compiler_params=_params(),
    )(s_all, g_all, g_mine, c_ctx, ada_w)


def _adamw(parts, w, m, v, name):
    P, R, C = parts.shape
    small = R * C * (P + 7) * 4 <= 4 * 1024 * 1024
    tr = R if small else _pick_tile(R, (256, 128, 64, 32, 16, 8))

    def body(p_ref, w_ref, m_ref, v_ref, g_ref, d_ref, nm_ref, nv_ref):
        g = p_ref[0].astype(F32)
        for i in range(1, P):
            g = g + p_ref[i].astype(F32)
        nm = ADAM_B1 * m_ref[...] + (1.0 - ADAM_B1) * g
        nv = ADAM_B2 * v_ref[...] + (1.0 - ADAM_B2) * (g * g)
        m_hat = nm / (1.0 - ADAM_B1 ** ADAM_STEP)
        v_hat = nv / (1.0 - ADAM_B2 ** ADAM_STEP)
        g_ref[...] = g
        d_ref[...] = -ADAM_LR * (m_hat / (jnp.sqrt(v_hat) + ADAM_EPS) + ADAM_WD * w_ref[...])
        nm_ref[...] = nm
        nv_ref[...] = nv

    blk = pl.BlockSpec((tr, C), lambda i: (i, 0))
    out = jax.ShapeDtypeStruct((R, C), F32)
    return pl.pallas_call(
        body, name=name, grid=(R // tr,),
        in_specs=[pl.BlockSpec((P, tr, C), lambda i: (0, i, 0)), blk, blk, blk],
        out_specs=[blk] * 4, out_shape=[out] * 4,
        compiler_params=_params(("parallel",)),
    )(parts, w, m, v)


def _local_step(cfg, x, c, ctx, tgt, fw, late):
    B, D, W, CW, JC, T, TX = cfg.B, cfg.D, cfg.W, cfg.CW, cfg.JC, cfg.T, cfg.TX
    e_w = _block_ones(min(W, MXU_DIM))
    e128 = _block_ones(LANES)
    e256 = jnp.concatenate([e128, e128], axis=0)
    row = lambda a: a.reshape(1, -1)

    ada_wb = fw["ada_w"].astype(BF16)
    w_in_p = _pad_cols(fw["w_in"], cfg).astype(BF16)
    mup = _pad_cols(fw["mu_prev"], cfg, True)
    mun = _pad_cols(fw["mu_next"], cfg, True)
    w0, a0 = row(fw["decay_w0"]), row(fw["iclr_a0"])
    w2p, a2p = _pair_weight(fw["decay_w2"], cfg), _pair_weight(fw["iclr_a2"], cfg)
    wm_fwd, wm_bwd = _conv_tables(fw["conv_w"], cfg.KC)
    gw2p = jnp.pad(fw["gate_w2"], ((0, cfg.GP - cfg.GR), (0, 0)))
    r_k = row(fw["r_k"])

    crows = jnp.concatenate([c, fw["c_ctx"], jnp.zeros((SUBLANES - B - 1, D), F32)], axis=0)
    s_rows, mods = _ada_fwd(crows, ada_wb, fw["ada_b"])
    mod_x = mods[:B].reshape(B, 6, D)
    mod_c = mods[B].reshape(6, D)
    modt = jnp.concatenate([jnp.broadcast_to(mod_c[None, None, 0:2], (B, JC, 2, D)),
                            jnp.broadcast_to(mod_x[:, None, 0:2], (B, cfg.JX, 2, D))], axis=1)
    mod2, mod345 = mod_x[:, 2:3], mod_x[:, 3:6]

    p, hb = _in_proj(ctx, x, modt, fw["mix_pre_g"], w_in_p, cfg)
    prep_w = (mup, mun, w0, w2p, a0, a2p, fw["k_k"], fw["k_a"])
    conv_w = (fw["conv_b"], fw["conv_ln_w"], fw["conv_ln_b"], e_w)
    (r, v, kk, w_f, kd_f, b_f, w_b, kd_b, b_b, gd, conv, yconv, g_w_out, g_w1, g_w2) = _mix_prep(
        p, *prep_w, wm_fwd, *conv_w, late, cfg)
    w_outb, w1b, w2b = g_w_out.reshape(-1, D), _blocks_to_cols(g_w1), g_w2.reshape(-1, D)
    flat = lambda a: a.reshape(cfg.G, a.shape[2], LANES)
    heads = lambda a: a.reshape(B, cfg.HP, a.shape[1], LANES)
    ops_f = tuple(flat(a) for a in (r, w_f, kd_f, v, kk, b_f))
    ops_b = tuple(flat(a) for a in (r, w_b, kd_b, v, kk, b_b))
    y_f, y_b, hist, s_fin = _scan_fwd(ops_f, ops_b, e128, e256, cfg)
    hist = lax.dynamic_update_slice_in_dim(hist, s_fin[None], cfg.NCH * SCAN_CHUNK, axis=0)
    out_args = (heads(y_f), heads(y_b), kd_f, kd_b, r, v, gd, conv, x, mod2, r_k, gw2p, fw["lnx_w"], fw["lnx_b"],
                w_outb, fw["mix_post_g"], e_w)
    x1 = _mix_out(*out_args, cfg)

    dx1, loss_t, h2b, dpreb, actb, dffb, dmod345, dg3, dg4 = _mlp_fwd_bwd(
        x1, tgt, mod345, fw["mlp_pre_g"], fw["mlp_post_g"], w1b, w2b, cfg)
    (dy, dkb, dr_c, dv_c, dgd, dconv, catb, dmixb, dmod2, dg2, drk, dgw, dlw, dlb) = _mix_out_bwd(
        *out_args, dx1, cfg)
    sf, sb = _scan_bwd(ops_f, ops_b, flat(dy), hist, e128, e256, cfg)

    tokens = lambda a: a.reshape(-1, a.shape[-1])
    d_w_out = _grad_matmul(tokens(catb), tokens(dmixb), "grad_w_out")
    d_w1 = _grad_matmul(tokens(h2b), tokens(dpreb), "grad_mlp_w1")
    d_w2 = _grad_matmul(tokens(actb), tokens(dffb), "grad_mlp_w2")
    early = [d_w_out.astype(BF16).reshape(N_DEV, -1, D), _cols_to_blocks(d_w1.astype(BF16)),
             d_w2.astype(BF16).reshape(N_DEV, -1, D)]
    (dpz, dmup, dmun, dw0, dw2p, da0, da2p, dkk, dka, dcw, dcb, dclw, dclb, x_w_out, x_w1, x_w2) = _mix_prep_bwd(
        p, *prep_w, wm_bwd, *conv_w, yconv, [heads(a) for a in sf], [heads(a) for a in sb],
        (dr_c, dv_c, dkb, dgd, dconv), early, cfg)
    grad_x, dpb, dmodt, dg1 = _in_proj_bwd(dpz, ctx, x, modt, fw["mix_pre_g"], w_in_p, mup, mun, dx1, cfg)
    d_w_in = _grad_matmul(tokens(hb), tokens(dpb), "grad_w_in")
    exchanged = {"w_out": x_w_out, "mlp_w1": x_w1, "mlp_w2": x_w2}

    grads = {
        "mix_pre_g": dg1, "mix_post_g": dg2, "mlp_pre_g": dg3, "mlp_post_g": dg4,
        "w_in": _unpad_cols(d_w_in, cfg),
        "mu_prev": _unpad_cols(dmup, cfg, True), "mu_next": _unpad_cols(dmun, cfg, True),
        "decay_w0": dw0.reshape(2, W), "decay_w2": _unpair_weight(dw2p, cfg),
        "iclr_a0": da0.reshape(2, W), "iclr_a2": _unpair_weight(da2p, cfg),
        "k_k": dkk, "k_a": dka, "r_k": drk.reshape(fw["r_k"].shape),
        "gate_w2": dgw[:cfg.GR], "lnx_w": dlw, "lnx_b": dlb,
        "conv_w": dcw[:cfg.KC], "conv_b": dcb, "conv_ln_w": dclw, "conv_ln_b": dclb,
    }
    dmod_x = jnp.concatenate([jnp.sum(dmodt[:, JC:], axis=1), dmod2, dmod345], axis=1).reshape(B, 6 * D)
    dmod_c = jnp.concatenate([jnp.sum(dmodt[:, :JC], axis=(0, 1)), jnp.zeros((4, D), F32)], axis=0).reshape(1, 6 * D)
    g_rows = jnp.concatenate([dmod_x, dmod_c, jnp.zeros((SUBLANES - B - 1, 6 * D), F32)], axis=0)
    return loss_t, grad_x, grads, exchanged, s_rows, g_rows


def _my_index():
    return 4 * lax.axis_index("x") + 2 * lax.axis_index("y") + lax.axis_index("c")


def _gather_phase(phase, ins, outs, send_sems, recv_sems, local_sems):
    n = len(ins)
    x, y, c = lax.axis_index("x"), lax.axis_index("y"), lax.axis_index("c")
    index = lambda px, py, pc: 4 * px + 2 * py + pc
    me, sibling = (x, y, c), (x, y, 1 - c)
    chips = [(1 - x, y), (x, 1 - y), (1 - x, 1 - y)]

    def copy(a, k, block, to, src=None):
        dst = outs[a].at[index(*block)]
        return pltpu.make_async_remote_copy(
            src_ref=dst if src is None else src, dst_ref=dst,
            send_sem=send_sems.at[k, a], recv_sem=recv_sems.at[k, a],
            device_id=to, device_id_type=pl.DeviceIdType.MESH)

    def local():
        return [pltpu.make_async_copy(ins[a], outs[a].at[index(*me)], local_sems.at[a]) for a in range(n)]

    def first():
        return [cp for a in range(n) for cp in
                [copy(a, 0, me, sibling, src=ins[a])]
                + [copy(a, 1 + j, me, (*chip, c), src=ins[a]) for j, chip in enumerate(chips)]]

    def passed():
        return [copy(a, 4 + j, (*chip, c), sibling) for j, chip in enumerate(chips) for a in range(n)]

    if phase == 0:
        for cp in local() + first():
            cp.start()
    elif phase == 1:
        arrived = [copy(a, 1 + j, (*chip, c), me) for j, chip in enumerate(chips) for a in range(n)]
        for got, fwd in zip(arrived, passed()):
            got.wait_recv()
            fwd.start()
    else:
        for a in range(n):
            copy(a, 0, sibling, me).wait_recv()
            for j, chip in enumerate(chips):
                copy(a, 4 + j, (*chip, 1 - c), me).wait_recv()
        for cp in first() + passed():
            cp.wait_send()
        for cp in local():
            cp.wait()


def _gather_two_level(arrays, name):
    n = len(arrays)

    def body(*refs):
        for phase in range(3):
            _gather_phase(phase, refs[:n], refs[n:2 * n], *refs[2 * n:])

    hbm = pl.BlockSpec(memory_space=pltpu.HBM)
    return pl.pallas_call(
        body, name=name, out_shape=_exchange_shapes(arrays, [False] * n),
        in_specs=[hbm] * n, out_specs=[hbm] * n, scratch_shapes=_exchange_sems(n),
    )(*arrays)


def _exchange_copies(ins, outs, send_sems, recv_sems, local_sems, scatter):
    n = len(ins)
    x, y, c = lax.axis_index("x"), lax.axis_index("y"), lax.axis_index("c")
    me = 4 * x + 2 * y + c
    flip = lambda v, f: 1 - v if f else v

    def piece(a, dest):
        return ins[a].at[dest] if scatter[a] else ins[a]

    local = [pltpu.make_async_copy(piece(a, me), outs[a].at[me], local_sems.at[a]) for a in range(n)]
    sends, recvs = [], []
    for k in range(1, N_DEV):
        fx, fy, fc = (k >> 2) & 1, (k >> 1) & 1, k & 1
        peer = (flip(x, fx), flip(y, fy), flip(c, fc))
        peer_idx = 4 * peer[0] + 2 * peer[1] + peer[2]
        for a in range(n):
            sends.append(pltpu.make_async_remote_copy(
                src_ref=piece(a, peer_idx), dst_ref=outs[a].at[me],
                send_sem=send_sems.at[k - 1, a], recv_sem=recv_sems.at[k - 1, a],
                device_id=peer, device_id_type=pl.DeviceIdType.MESH))
            recvs.append(pltpu.make_async_remote_copy(
                src_ref=piece(a, peer_idx), dst_ref=outs[a].at[peer_idx],
                send_sem=send_sems.at[k - 1, a], recv_sem=recv_sems.at[k - 1, a],
                device_id=peer, device_id_type=pl.DeviceIdType.MESH))
    return local, sends, recvs


def _exchange_start(copies):
    local, sends, _ = copies
    for cp in local + sends:
        cp.start()


def _exchange_wait(copies):
    local, sends, recvs = copies
    for cp in recvs:
        cp.wait_recv()
    for cp in sends:
        cp.wait_send()
    for cp in local:
        cp.wait()


def _exchange_shapes(arrays, scatter):
    return [jax.ShapeDtypeStruct(a.shape if s else (N_DEV,) + a.shape, a.dtype) for a, s in zip(arrays, scatter)]


def _exchange_sems(n):
    return [pltpu.SemaphoreType.DMA((N_DEV - 1, n)), pltpu.SemaphoreType.DMA((N_DEV - 1, n)),
            pltpu.SemaphoreType.DMA((n,))]


def _exchange(arrays, scatter, name):
    n = len(arrays)

    def body(*refs):
        copies = _exchange_copies(refs[:n], refs[n:2 * n], *refs[2 * n:], scatter)
        _exchange_start(copies)
        _exchange_wait(copies)

    hbm = pl.BlockSpec(memory_space=pltpu.HBM)
    return pl.pallas_call(
        body, name=name, out_shape=_exchange_shapes(arrays, scatter),
        in_specs=[hbm] * n, out_specs=[hbm] * n, scratch_shapes=_exchange_sems(n),
    )(*arrays)


def _pack(parts):
    flat = jnp.concatenate([p.reshape(-1) for p in parts])
    total = _round_up(flat.shape[0], SUBLANES * LANES)
    return jnp.pad(flat, (0, total - flat.shape[0])).reshape(-1, LANES)


def _unpack(buf, shapes):
    flat = buf.reshape(-1)
    out, pos = [], 0
    for s in shapes:
        n = int(np.prod(s))
        out.append(flat[pos:pos + n].reshape(s))
        pos += n
    return out


_SHARDED_SMALL = ("decay_w0", "decay_w2", "iclr_a0", "iclr_a2", "gate_w2", "conv_w")
_REPLICATED = ("mix_pre_g", "mix_post_g", "mlp_pre_g", "mlp_post_g", "mu_prev", "mu_next", "k_k", "k_a", "r_k",
               "lnx_w", "lnx_b", "conv_b", "conv_ln_w", "conv_ln_b")
_ADA_SMALL = ("c_ctx", "ada_b")
_WEIGHTS = ("c_ctx", "ada_w", "ada_b", "mix_pre_g", "mix_post_g", "mlp_pre_g", "mlp_post_g", "w_in", "mu_prev",
            "mu_next", "decay_w0", "decay_w2", "iclr_a0", "iclr_a2", "k_k", "k_a", "r_k", "gate_w2", "lnx_w", "lnx_b",
            "conv_w", "conv_b", "conv_ln_w", "conv_ln_b", "w_out", "mlp_w1", "mlp_w2")
_INPUTS = ("x", "c", "ctx") + _WEIGHTS + ("loss_target",) + tuple("m_" + n for n in _WEIGHTS) + tuple(
    "v_" + n for n in _WEIGHTS)


def _cols_to_blocks(a):
    a = a.reshape(a.shape[:-1] + (N_DEV, a.shape[-1] // N_DEV))
    return jnp.moveaxis(a, -2, 0)


def _blocks_to_cols(a):
    a = jnp.moveaxis(a, 0, -2)
    return a.reshape(a.shape[:-2] + (a.shape[-2] * a.shape[-1],))


def kernel(x, c, ctx, c_ctx, ada_w, ada_b, mix_pre_g, mix_post_g, mlp_pre_g, mlp_post_g, w_in, mu_prev, mu_next, decay_w0, decay_w2, iclr_a0, iclr_a2, k_k, k_a, r_k, gate_w2, lnx_w, lnx_b, conv_w, conv_b, conv_ln_w, conv_ln_b, w_out, mlp_w1, mlp_w2, loss_target, m_c_ctx, m_ada_w, m_ada_b, m_mix_pre_g, m_mix_post_g, m_mlp_pre_g, m_mlp_post_g, m_w_in, m_mu_prev, m_mu_next, m_decay_w0, m_decay_w2, m_iclr_a0, m_iclr_a2, m_k_k, m_k_a, m_r_k, m_gate_w2, m_lnx_w, m_lnx_b, m_conv_w, m_conv_b, m_conv_ln_w, m_conv_ln_b, m_w_out, m_mlp_w1, m_mlp_w2, v_c_ctx, v_ada_w, v_ada_b, v_mix_pre_g, v_mix_post_g, v_mlp_pre_g, v_mlp_post_g, v_w_in, v_mu_prev, v_mu_next, v_decay_w0, v_decay_w2, v_iclr_a0, v_iclr_a2, v_k_k, v_k_a, v_r_k, v_gate_w2, v_lnx_w, v_lnx_b, v_conv_w, v_conv_b, v_conv_ln_w, v_conv_ln_b, v_w_out, v_mlp_w1, v_mlp_w2):
    given = dict(zip(_INPUTS, (x, c, ctx, c_ctx, ada_w, ada_b, mix_pre_g, mix_post_g, mlp_pre_g, mlp_post_g, w_in, mu_prev, mu_next, decay_w0, decay_w2, iclr_a0, iclr_a2, k_k, k_a, r_k, gate_w2, lnx_w, lnx_b, conv_w, conv_b, conv_ln_w, conv_ln_b, w_out, mlp_w1, mlp_w2, loss_target, m_c_ctx, m_ada_w, m_ada_b, m_mix_pre_g, m_mix_post_g, m_mlp_pre_g, m_mlp_post_g, m_w_in, m_mu_prev, m_mu_next, m_decay_w0, m_decay_w2, m_iclr_a0, m_iclr_a2, m_k_k, m_k_a, m_r_k, m_gate_w2, m_lnx_w, m_lnx_b, m_conv_w, m_conv_b, m_conv_ln_w, m_conv_ln_b, m_w_out, m_mlp_w1, m_mlp_w2, v_c_ctx, v_ada_w, v_ada_b, v_mix_pre_g, v_mix_post_g, v_mlp_pre_g, v_mlp_post_g, v_w_in, v_mu_prev, v_mu_next, v_decay_w0, v_decay_w2, v_iclr_a0, v_iclr_a2, v_k_k, v_k_a, v_r_k, v_gate_w2, v_lnx_w, v_lnx_b, v_conv_w, v_conv_b, v_conv_ln_w, v_conv_ln_b, v_w_out, v_mlp_w1, v_mlp_w2)))
    loc = {}
    for pre in ("", "m_", "v_"):
        for n in _WEIGHTS:
            a = given[pre + n]
            a = a.reshape(1, -1) if n == "c_ctx" else a[0]
            loc[pre + n] = a.reshape(1, -1) if a.ndim == 1 else a
    B, TX, D = x.shape
    W, CW = loc["k_k"].shape[1], loc["conv_b"].shape[1]
    cfg = _Cfg(B, TX, ctx.shape[1], D, W, CW, loc["decay_w2"].shape[1], loc["gate_w2"].shape[0],
               loc["conv_w"].shape[0], loc["mlp_w1"].shape[1] * N_DEV)
    me = _my_index()

    small_shapes = [loc[n].shape for n in _SHARDED_SMALL]
    got = _gather_two_level(
        [loc["ada_w"].astype(BF16), loc["w_in"].astype(BF16), _pack([loc[n] for n in _SHARDED_SMALL])],
        "gather_weights")
    fw = {n: loc[n] for n in _REPLICATED + _ADA_SMALL}
    fw["ada_w"] = _blocks_to_cols(got[0])
    fw["w_in"] = _blocks_to_cols(got[1])
    per_dev = [_unpack(got[2][i], small_shapes) for i in range(N_DEV)]
    for j, n in enumerate(_SHARDED_SMALL):
        fw[n] = jnp.concatenate([per_dev[i][j] for i in range(N_DEV)], axis=-1)
    late = [loc[n].astype(BF16) for n in ("w_out", "mlp_w1", "mlp_w2")]

    loss_t, grad_x, grads, exchanged, s_rows, g_rows = _local_step(cfg, x, c, ctx, loss_target, fw, late)
    loss = lax.psum(jnp.sum(loss_t[:, :, 0, 0]), ("x", "y", "c"))

    small_blocks = jnp.stack([_pack([_cols_to_blocks(grads[n])[i] for n in _SHARDED_SMALL]) for i in range(N_DEV)])
    sent = _exchange(
        [_cols_to_blocks(grads["w_in"].astype(BF16)), small_blocks,
         _pack([grads[n] for n in _REPLICATED]), s_rows, g_rows],
        [True] * 2 + [False] * 3, "exchange_grads")
    s_all = sent[3].reshape(N_DEV * SUBLANES, D)
    g_all = sent[4].reshape(N_DEV * SUBLANES, 6 * D)
    ns = 6 * D // N_DEV
    g_mine = lax.dynamic_slice_in_dim(g_all, me * ns, ns, axis=1)
    d_ada_w, d_ada_b, d_c_ctx = _ada_bwd(s_all, g_all, g_mine, loc["c_ctx"], fw["ada_w"], B)

    res = {}

    def update(name, parts):
        res[name] = _adamw(parts, loc[name], loc["m_" + name], loc["v_" + name], "adamw_" + name)

    update("w_in", sent[0])
    for n in ("w_out", "mlp_w1", "mlp_w2"):
        update(n, exchanged[n])
    update("ada_w", d_ada_w[None])

    def update_packed(names, parts, tag):
        shapes = [loc[n].shape for n in names]
        packed = _adamw(parts, *[_pack([loc[pre + n] for n in names]) for pre in ("", "m_", "v_")], "adamw_" + tag)
        unpacked = [_unpack(p, shapes) for p in packed]
        for j, n in enumerate(names):
            res[n] = tuple(u[j] for u in unpacked)

    update_packed(_SHARDED_SMALL, sent[1], "sharded_small")
    update_packed(_REPLICATED, sent[2], "replicated")
    update_packed(_ADA_SMALL, _pack([d_c_ctx, d_ada_b])[None], "ada_small")

    outs = [loss, grad_x]
    for k in range(4):
        for n in _WEIGHTS:
            outs.append(res[n][k].reshape(given[n].shape))
    return tuple(outs)
```
